```python
import jax, jax.numpy as jnp
from jax import lax
import numpy as np

D_MODEL = 1024
BATCH = 8
SEQ = 8192
DEPTH = 1

D_RNN = 1024
RNN_HEADS = 16
RNN_HEAD_DIM = D_RNN // RNN_HEADS
CONV_WIDTH = 4
RG_C = 8.0
D_SGU = 1024
SGU_GROUPS = 8
SGU_GROUP_DIM = D_SGU // SGU_GROUPS
CHUNK = 128
IN_COLS = 2 * D_RNN + 2 * D_SGU + 2 * D_MODEL
SPLIT_POINTS = (D_RNN, 2 * D_RNN, 2 * D_RNN + D_SGU, 2 * D_RNN + 2 * D_SGU, 2 * D_RNN + 2 * D_SGU + D_MODEL)
D_FF = -(-8 * D_MODEL // (3 * 256)) * 256
EPS = 1e-6

kernel_name = "hybrid_rglru_chunked_sgu_gated_block"


def rms_norm(x, g):
    xf = x.astype(jnp.float32)
    y = xf * lax.rsqrt(jnp.mean(xf * xf, axis=-1, keepdims=True) + EPS)
    return (y * g.astype(jnp.float32)).astype(x.dtype)


def layer_norm(x, g, b):
    xf = x.astype(jnp.float32)
    mu = jnp.mean(xf, axis=-1, keepdims=True)
    var = jnp.mean(jnp.square(xf - mu), axis=-1, keepdims=True)
    y = (xf - mu) * lax.rsqrt(var + EPS)
    return (y * g.astype(jnp.float32) + b.astype(jnp.float32)).astype(x.dtype)


def causal_depthwise_conv(x, w, b):
    s = x.shape[1]
    xp = jnp.pad(x, ((0, 0), (CONV_WIDTH - 1, 0), (0, 0)))
    y = b
    for k in range(CONV_WIDTH):
        y = y + xp[:, k:k + s, :] * w[k]
    return y


def rg_lru(x, w_a, b_a, w_x, b_x, lam):
    bsz, s, _ = x.shape
    xh = x.reshape(bsz, s, RNN_HEADS, RNN_HEAD_DIM)
    r = jax.nn.sigmoid(jnp.einsum('bshd,hde->bshe', xh, w_a) + b_a).reshape(bsz, s, D_RNN)
    i = jax.nn.sigmoid(jnp.einsum('bshd,hde->bshe', xh, w_x) + b_x).reshape(bsz, s, D_RNN)
    log_a = -RG_C * r.astype(jnp.float32) * jax.nn.softplus(-lam.astype(jnp.float32))
    a = jnp.exp(log_a)
    gated_x = jnp.sqrt(-jnp.expm1(2.0 * log_a)) * (i * x).astype(jnp.float32)

    def combine(c1, c2):
        a1, b1 = c1
        a2, b2 = c2
        return a1 * a2, a2 * b1 + b2

    _, h = lax.associative_scan(combine, (a, gated_x), axis=1)
    return h.astype(x.dtype)


def chunked_sgu(u, v, ln_g, ln_b, w_s, b_s):
    bsz, s, _ = v.shape
    n_chunks = s // CHUNK
    v = layer_norm(v, ln_g, ln_b)
    vc = v.reshape(bsz, n_chunks, CHUNK, SGU_GROUPS, SGU_GROUP_DIM)
    causal = jnp.tril(jnp.ones((CHUNK, CHUNK), dtype=bool))
    w = jnp.where(causal[None], w_s, jnp.zeros_like(w_s))
    mixed = jnp.einsum('gts,bnsgd->bntgd', w, vc) + b_s.T[None, None, :, :, None]
    return u * mixed.reshape(bsz, s, D_SGU)


def _fwd_setup_inputs(seed: int = 0) -> dict:
    key = jax.random.key(seed)
    ks = jax.random.split(key, 24)
    f32 = jnp.float32

    def nrm(k, shape, scale):
        return jax.random.normal(k, shape, f32) * scale

    a_c = jax.random.uniform(ks[10], (DEPTH, D_RNN), f32, 0.9, 0.999)
    p = a_c ** (1.0 / RG_C)
    rg_lambda = jnp.log(p) - jnp.log1p(-p)
    return {
        "x": nrm(ks[0], (BATCH, SEQ, D_MODEL), 1.0),
        "norm_mix_g": 1.0 + nrm(ks[1], (DEPTH, D_MODEL), 0.02),
        "w_in": nrm(ks[2], (DEPTH, D_MODEL, IN_COLS), D_MODEL ** -0.5),
        "conv_w": nrm(ks[3], (DEPTH, CONV_WIDTH, D_RNN), CONV_WIDTH ** -0.5),
        "conv_b": nrm(ks[4], (DEPTH, D_RNN), 0.01),
        "rg_wa": nrm(ks[5], (DEPTH, RNN_HEADS, RNN_HEAD_DIM, RNN_HEAD_DIM), RNN_HEAD_DIM ** -0.5),
        "rg_ba": nrm(ks[6], (DEPTH, RNN_HEADS, RNN_HEAD_DIM), 0.01),
        "rg_wx": nrm(ks[7], (DEPTH, RNN_HEADS, RNN_HEAD_DIM, RNN_HEAD_DIM), RNN_HEAD_DIM ** -0.5),
        "rg_bx": nrm(ks[8], (DEPTH, RNN_HEADS, RNN_HEAD_DIM), 0.01),
        "rg_lambda": rg_lambda,
        "sgu_ln_g": 1.0 + nrm(ks[11], (DEPTH, D_SGU), 0.02),
        "sgu_ln_b": nrm(ks[12], (DEPTH, D_SGU), 0.01),
        "sgu_ws": nrm(ks[13], (DEPTH, SGU_GROUPS, CHUNK, CHUNK), CHUNK ** -0.5),
        "sgu_bs": 1.0 + nrm(ks[14], (DEPTH, SGU_GROUPS, CHUNK), 0.02),
        "w_proj_a": nrm(ks[15], (DEPTH, D_RNN, D_MODEL), D_RNN ** -0.5),
        "w_proj_b": nrm(ks[16], (DEPTH, D_SGU, D_MODEL), D_SGU ** -0.5),
        "w_out": nrm(ks[17], (DEPTH, D_MODEL, D_MODEL), D_MODEL ** -0.5),
        "norm_ffn_g": 1.0 + nrm(ks[18], (DEPTH, D_MODEL), 0.02),
        "w_gate_up": nrm(ks[19], (DEPTH, D_MODEL, 2 * D_FF), D_MODEL ** -0.5),
        "w_down": nrm(ks[20], (DEPTH, D_FF, D_MODEL), D_FF ** -0.5),
        "norm_final_g": 1.0 + nrm(ks[21], (D_MODEL,), 0.02),
    }


def _fwd_reference(x, norm_mix_g, w_in, conv_w, conv_b, rg_wa, rg_ba, rg_wx, rg_bx, rg_lambda,
              sgu_ln_g, sgu_ln_b, sgu_ws, sgu_bs, w_proj_a, w_proj_b, w_out,
              norm_ffn_g, w_gate_up, w_down, norm_final_g):
    for l in range(DEPTH):
        h = rms_norm(x, norm_mix_g[l])
        proj = h @ w_in[l]
        rnn_x, rnn_gate, sgu_u, sgu_v, gate_a, gate_b = jnp.split(proj, SPLIT_POINTS, axis=-1)
        rnn_x = causal_depthwise_conv(rnn_x, conv_w[l], conv_b[l])
        y_a = jax.nn.gelu(rnn_gate) * rg_lru(rnn_x, rg_wa[l], rg_ba[l], rg_wx[l], rg_bx[l], rg_lambda[l])
        y_b = chunked_sgu(jax.nn.gelu(sgu_u), jax.nn.gelu(sgu_v), sgu_ln_g[l], sgu_ln_b[l], sgu_ws[l], sgu_bs[l])
        merged = jax.nn.sigmoid(gate_a) * (y_a @ w_proj_a[l]) + jax.nn.sigmoid(gate_b) * (y_b @ w_proj_b[l])
        x = x + merged @ w_out[l]
        h = rms_norm(x, norm_ffn_g[l])
        g, u = jnp.split(h @ w_gate_up[l], 2, axis=-1)
        x = x + (jax.nn.silu(g) * u) @ w_down[l]
    return rms_norm(x, norm_final_g)


import jax as _jax
import jax.numpy as _jnp

TWIN_FORMAT = 'train_step'
FWD_PARAMS = ['x', 'norm_mix_g', 'w_in', 'conv_w', 'conv_b', 'rg_wa', 'rg_ba', 'rg_wx', 'rg_bx', 'rg_lambda', 'sgu_ln_g', 'sgu_ln_b', 'sgu_ws', 'sgu_bs', 'w_proj_a', 'w_proj_b', 'w_out', 'norm_ffn_g', 'w_gate_up', 'w_down', 'norm_final_g']
TWIN_WEIGHTS = ['norm_mix_g', 'w_in', 'conv_w', 'conv_b', 'rg_wa', 'rg_ba', 'rg_wx', 'rg_bx', 'rg_lambda', 'sgu_ln_g', 'sgu_ln_b', 'sgu_ws', 'sgu_bs', 'w_proj_a', 'w_proj_b', 'w_out', 'norm_ffn_g', 'w_gate_up', 'w_down', 'norm_final_g']
TWIN_DIFF_INPUT = 'x'
TWIN_INPUTS = ['x', 'norm_mix_g', 'w_in', 'conv_w', 'conv_b', 'rg_wa', 'rg_ba', 'rg_wx', 'rg_bx', 'rg_lambda', 'sgu_ln_g', 'sgu_ln_b', 'sgu_ws', 'sgu_bs', 'w_proj_a', 'w_proj_b', 'w_out', 'norm_ffn_g', 'w_gate_up', 'w_down', 'norm_final_g', 'loss_target', 'm_norm_mix_g', 'm_w_in', 'm_conv_w', 'm_conv_b', 'm_rg_wa', 'm_rg_ba', 'm_rg_wx', 'm_rg_bx', 'm_rg_lambda', 'm_sgu_ln_g', 'm_sgu_ln_b', 'm_sgu_ws', 'm_sgu_bs', 'm_w_proj_a', 'm_w_proj_b', 'm_w_out', 'm_norm_ffn_g', 'm_w_gate_up', 'm_w_down', 'm_norm_final_g', 'v_norm_mix_g', 'v_w_in', 'v_conv_w', 'v_conv_b', 'v_rg_wa', 'v_rg_ba', 'v_rg_wx', 'v_rg_bx', 'v_rg_lambda', 'v_sgu_ln_g', 'v_sgu_ln_b', 'v_sgu_ws', 'v_sgu_bs', 'v_w_proj_a', 'v_w_proj_b', 'v_w_out', 'v_norm_ffn_g', 'v_w_gate_up', 'v_w_down', 'v_norm_final_g']
TWIN_OUTPUTS = ['loss', 'grad_x', 'grad_norm_mix_g', 'grad_w_in', 'grad_conv_w', 'grad_conv_b', 'grad_rg_wa', 'grad_rg_ba', 'grad_rg_wx', 'grad_rg_bx', 'grad_rg_lambda', 'grad_sgu_ln_g', 'grad_sgu_ln_b', 'grad_sgu_ws', 'grad_sgu_bs', 'grad_w_proj_a', 'grad_w_proj_b', 'grad_w_out', 'grad_norm_ffn_g', 'grad_w_gate_up', 'grad_w_down', 'grad_norm_final_g', 'delta_norm_mix_g', 'delta_w_in', 'delta_conv_w', 'delta_conv_b', 'delta_rg_wa', 'delta_rg_ba', 'delta_rg_wx', 'delta_rg_bx', 'delta_rg_lambda', 'delta_sgu_ln_g', 'delta_sgu_ln_b', 'delta_sgu_ws', 'delta_sgu_bs', 'delta_w_proj_a', 'delta_w_proj_b', 'delta_w_out', 'delta_norm_ffn_g', 'delta_w_gate_up', 'delta_w_down', 'delta_norm_final_g', 'new_m_norm_mix_g', 'new_m_w_in', 'new_m_conv_w', 'new_m_conv_b', 'new_m_rg_wa', 'new_m_rg_ba', 'new_m_rg_wx', 'new_m_rg_bx', 'new_m_rg_lambda', 'new_m_sgu_ln_g', 'new_m_sgu_ln_b', 'new_m_sgu_ws', 'new_m_sgu_bs', 'new_m_w_proj_a', 'new_m_w_proj_b', 'new_m_w_out', 'new_m_norm_ffn_g', 'new_m_w_gate_up', 'new_m_w_down', 'new_m_norm_final_g', 'new_v_norm_mix_g', 'new_v_w_in', 'new_v_conv_w', 'new_v_conv_b', 'new_v_rg_wa', 'new_v_rg_ba', 'new_v_rg_wx', 'new_v_rg_bx', 'new_v_rg_lambda', 'new_v_sgu_ln_g', 'new_v_sgu_ln_b', 'new_v_sgu_ws', 'new_v_sgu_bs', 'new_v_w_proj_a', 'new_v_w_proj_b', 'new_v_w_out', 'new_v_norm_ffn_g', 'new_v_w_gate_up', 'new_v_w_down', 'new_v_norm_final_g']
TWIN_LEAF_KINDS = {'loss': 'loss', 'grad_x': 'grad_x', 'grad_norm_mix_g': 'grad_w', 'grad_w_in': 'grad_w', 'grad_conv_w': 'grad_w', 'grad_conv_b': 'grad_w', 'grad_rg_wa': 'grad_w', 'grad_rg_ba': 'grad_w', 'grad_rg_wx': 'grad_w', 'grad_rg_bx': 'grad_w', 'grad_rg_lambda': 'grad_w', 'grad_sgu_ln_g': 'grad_w', 'grad_sgu_ln_b': 'grad_w', 'grad_sgu_ws': 'grad_w', 'grad_sgu_bs': 'grad_w', 'grad_w_proj_a': 'grad_w', 'grad_w_proj_b': 'grad_w', 'grad_w_out': 'grad_w', 'grad_norm_ffn_g': 'grad_w', 'grad_w_gate_up': 'grad_w', 'grad_w_down': 'grad_w', 'grad_norm_final_g': 'grad_w', 'delta_norm_mix_g': 'delta_w', 'delta_w_in': 'delta_w', 'delta_conv_w': 'delta_w', 'delta_conv_b': 'delta_w', 'delta_rg_wa': 'delta_w', 'delta_rg_ba': 'delta_w', 'delta_rg_wx': 'delta_w', 'delta_rg_bx': 'delta_w', 'delta_rg_lambda': 'delta_w', 'delta_sgu_ln_g': 'delta_w', 'delta_sgu_ln_b': 'delta_w', 'delta_sgu_ws': 'delta_w', 'delta_sgu_bs': 'delta_w', 'delta_w_proj_a': 'delta_w', 'delta_w_proj_b': 'delta_w', 'delta_w_out': 'delta_w', 'delta_norm_ffn_g': 'delta_w', 'delta_w_gate_up': 'delta_w', 'delta_w_down': 'delta_w', 'delta_norm_final_g': 'delta_w', 'new_m_norm_mix_g': 'new_m', 'new_m_w_in': 'new_m', 'new_m_conv_w': 'new_m', 'new_m_conv_b': 'new_m', 'new_m_rg_wa': 'new_m', 'new_m_rg_ba': 'new_m', 'new_m_rg_wx': 'new_m', 'new_m_rg_bx': 'new_m', 'new_m_rg_lambda': 'new_m', 'new_m_sgu_ln_g': 'new_m', 'new_m_sgu_ln_b': 'new_m', 'new_m_sgu_ws': 'new_m', 'new_m_sgu_bs': 'new_m', 'new_m_w_proj_a': 'new_m', 'new_m_w_proj_b': 'new_m', 'new_m_w_out': 'new_m', 'new_m_norm_ffn_g': 'new_m', 'new_m_w_gate_up': 'new_m', 'new_m_w_down': 'new_m', 'new_m_norm_final_g': 'new_m', 'new_v_norm_mix_g': 'new_v', 'new_v_w_in': 'new_v', 'new_v_conv_w': 'new_v', 'new_v_conv_b': 'new_v', 'new_v_rg_wa': 'new_v', 'new_v_rg_ba': 'new_v', 'new_v_rg_wx': 'new_v', 'new_v_rg_bx': 'new_v', 'new_v_rg_lambda': 'new_v', 'new_v_sgu_ln_g': 'new_v', 'new_v_sgu_ln_b': 'new_v', 'new_v_sgu_ws': 'new_v', 'new_v_sgu_bs': 'new_v', 'new_v_w_proj_a': 'new_v', 'new_v_w_proj_b': 'new_v', 'new_v_w_out': 'new_v', 'new_v_norm_ffn_g': 'new_v', 'new_v_w_gate_up': 'new_v', 'new_v_w_down': 'new_v', 'new_v_norm_final_g': 'new_v'}


def _forward(args):
    return _fwd_reference(*[args[k] for k in FWD_PARAMS])


def _output_shape():
    def fwd():
        inp = _fwd_setup_inputs(0)
        return _fwd_reference(*[inp[k] for k in FWD_PARAMS])
    out = _jax.eval_shape(fwd)
    return out.shape, out.dtype

N_MICROBATCH = 1
ADAM_LR = 0.001
ADAM_B1 = 0.9
ADAM_B2 = 0.999
ADAM_EPS = 1e-08
ADAM_WD = 0.01
ADAM_STEP = 10
PER_EXAMPLE_BATCH_AXIS = {'x': 0, 'loss_target': 0}
SHARED_INPUTS = []
_WEIGHT_DTYPES = {'norm_mix_g': _jnp.float32, 'w_in': _jnp.float32, 'conv_w': _jnp.float32, 'conv_b': _jnp.float32, 'rg_wa': _jnp.float32, 'rg_ba': _jnp.float32, 'rg_wx': _jnp.float32, 'rg_bx': _jnp.float32, 'rg_lambda': _jnp.float32, 'sgu_ln_g': _jnp.float32, 'sgu_ln_b': _jnp.float32, 'sgu_ws': _jnp.float32, 'sgu_bs': _jnp.float32, 'w_proj_a': _jnp.float32, 'w_proj_b': _jnp.float32, 'w_out': _jnp.float32, 'norm_ffn_g': _jnp.float32, 'w_gate_up': _jnp.float32, 'w_down': _jnp.float32, 'norm_final_g': _jnp.float32}
MOMENT_SCALE = {'norm_mix_g': 1.797140e-01, 'w_in': 6.772386e-02, 'conv_w': 7.401759e-02, 'conv_b': 8.172570e-01, 'rg_wa': 3.057869e-02, 'rg_ba': 2.192893e-02, 'rg_wx': 5.434456e-02, 'rg_bx': 2.273989e-02, 'rg_lambda': 3.298721e-02, 'sgu_ln_g': 6.425725e-02, 'sgu_ln_b': 6.322325e-02, 'sgu_ws': 6.253346e-02, 'sgu_bs': 8.996302e-02, 'w_proj_a': 6.955954e-02, 'w_proj_b': 1.118114e-01, 'w_out': 1.292674e-01, 'norm_ffn_g': 1.719476e-01, 'w_gate_up': 7.281383e-02, 'w_down': 1.186069e-01, 'norm_final_g': 6.405701e+01}


def _to_microbatches(a, axis):
    t = _jnp.moveaxis(a, axis, 0)
    t = t.reshape((N_MICROBATCH, t.shape[0] // N_MICROBATCH) + t.shape[1:])
    return _jnp.moveaxis(t, 1, axis + 1)


def setup_inputs(seed: int = 0) -> dict:
    inp = _fwd_setup_inputs(seed)
    key = _jax.random.fold_in(_jax.random.key(seed), 7919)
    shape, _ = _output_shape()
    out = dict(inp)
    out["loss_target"] = _jax.random.normal(_jax.random.fold_in(key, 0), shape, _jnp.float32)
    for i, name in enumerate(TWIN_WEIGHTS):
        w = inp[name].astype(_jnp.float32)
        if MOMENT_SCALE is None:
            s = _jnp.sqrt(_jnp.mean(_jnp.square(w)) + 1e-30)
        else:
            s = MOMENT_SCALE[name]
        km, kv = _jax.random.split(_jax.random.fold_in(key, i + 1))
        out[name] = w
        out["m_" + name] = s * _jax.random.normal(km, w.shape, _jnp.float32)
        out["v_" + name] = (s * s) * _jax.random.uniform(kv, w.shape, _jnp.float32, 0.5, 1.5)
    if N_MICROBATCH > 1:
        for name, axis in PER_EXAMPLE_BATCH_AXIS.items():
            out[name] = _to_microbatches(out[name], axis)
    return {'x': out['x'], 'norm_mix_g': out['norm_mix_g'], 'w_in': out['w_in'], 'conv_w': out['conv_w'], 'conv_b': out['conv_b'], 'rg_wa': out['rg_wa'], 'rg_ba': out['rg_ba'], 'rg_wx': out['rg_wx'], 'rg_bx': out['rg_bx'], 'rg_lambda': out['rg_lambda'], 'sgu_ln_g': out['sgu_ln_g'], 'sgu_ln_b': out['sgu_ln_b'], 'sgu_ws': out['sgu_ws'], 'sgu_bs': out['sgu_bs'], 'w_proj_a': out['w_proj_a'], 'w_proj_b': out['w_proj_b'], 'w_out': out['w_out'], 'norm_ffn_g': out['norm_ffn_g'], 'w_gate_up': out['w_gate_up'], 'w_down': out['w_down'], 'norm_final_g': out['norm_final_g'], 'loss_target': out['loss_target'], 'm_norm_mix_g': out['m_norm_mix_g'], 'm_w_in': out['m_w_in'], 'm_conv_w': out['m_conv_w'], 'm_conv_b': out['m_conv_b'], 'm_rg_wa': out['m_rg_wa'], 'm_rg_ba': out['m_rg_ba'], 'm_rg_wx': out['m_rg_wx'], 'm_rg_bx': out['m_rg_bx'], 'm_rg_lambda': out['m_rg_lambda'], 'm_sgu_ln_g': out['m_sgu_ln_g'], 'm_sgu_ln_b': out['m_sgu_ln_b'], 'm_sgu_ws': out['m_sgu_ws'], 'm_sgu_bs': out['m_sgu_bs'], 'm_w_proj_a': out['m_w_proj_a'], 'm_w_proj_b': out['m_w_proj_b'], 'm_w_out': out['m_w_out'], 'm_norm_ffn_g': out['m_norm_ffn_g'], 'm_w_gate_up': out['m_w_gate_up'], 'm_w_down': out['m_w_down'], 'm_norm_final_g': out['m_norm_final_g'], 'v_norm_mix_g': out['v_norm_mix_g'], 'v_w_in': out['v_w_in'], 'v_conv_w': out['v_conv_w'], 'v_conv_b': out['v_conv_b'], 'v_rg_wa': out['v_rg_wa'], 'v_rg_ba': out['v_rg_ba'], 'v_rg_wx': out['v_rg_wx'], 'v_rg_bx': out['v_rg_bx'], 'v_rg_lambda': out['v_rg_lambda'], 'v_sgu_ln_g': out['v_sgu_ln_g'], 'v_sgu_ln_b': out['v_sgu_ln_b'], 'v_sgu_ws': out['v_sgu_ws'], 'v_sgu_bs': out['v_sgu_bs'], 'v_w_proj_a': out['v_w_proj_a'], 'v_w_proj_b': out['v_w_proj_b'], 'v_w_out': out['v_w_out'], 'v_norm_ffn_g': out['v_norm_ffn_g'], 'v_w_gate_up': out['v_w_gate_up'], 'v_w_down': out['v_w_down'], 'v_norm_final_g': out['v_norm_final_g']}


def _loss(weights, diff, rest, loss_target):
    with _jax.named_scope("forward"):
        args = {**rest, TWIN_DIFF_INPUT: diff, **{k: w.astype(_WEIGHT_DTYPES[k]) for k, w in weights.items()}}
        y = _forward(args)
    with _jax.named_scope("loss_head"):
        err = _jnp.square(y.astype(_jnp.float32) - loss_target)
        return 0.5 * _jnp.sum(_jnp.mean(err, axis=-1)) if err.ndim else 0.5 * err


def _adamw(w, g, m, v):
    m = ADAM_B1 * m + (1.0 - ADAM_B1) * g
    v = ADAM_B2 * v + (1.0 - ADAM_B2) * _jnp.square(g)
    m_hat = m / (1.0 - ADAM_B1 ** ADAM_STEP)
    v_hat = v / (1.0 - ADAM_B2 ** ADAM_STEP)
    delta = -ADAM_LR * (m_hat / (_jnp.sqrt(v_hat) + ADAM_EPS) + ADAM_WD * w)
    return delta, m, v


def reference(x, norm_mix_g, w_in, conv_w, conv_b, rg_wa, rg_ba, rg_wx, rg_bx, rg_lambda, sgu_ln_g, sgu_ln_b, sgu_ws, sgu_bs, w_proj_a, w_proj_b, w_out, norm_ffn_g, w_gate_up, w_down, norm_final_g, loss_target, m_norm_mix_g, m_w_in, m_conv_w, m_conv_b, m_rg_wa, m_rg_ba, m_rg_wx, m_rg_bx, m_rg_lambda, m_sgu_ln_g, m_sgu_ln_b, m_sgu_ws, m_sgu_bs, m_w_proj_a, m_w_proj_b, m_w_out, m_norm_ffn_g, m_w_gate_up, m_w_down, m_norm_final_g, v_norm_mix_g, v_w_in, v_conv_w, v_conv_b, v_rg_wa, v_rg_ba, v_rg_wx, v_rg_bx, v_rg_lambda, v_sgu_ln_g, v_sgu_ln_b, v_sgu_ws, v_sgu_bs, v_w_proj_a, v_w_proj_b, v_w_out, v_norm_ffn_g, v_w_gate_up, v_w_down, v_norm_final_g):
    given = dict(x=x, norm_mix_g=norm_mix_g, w_in=w_in, conv_w=conv_w, conv_b=conv_b, rg_wa=rg_wa, rg_ba=rg_ba, rg_wx=rg_wx, rg_bx=rg_bx, rg_lambda=rg_lambda, sgu_ln_g=sgu_ln_g, sgu_ln_b=sgu_ln_b, sgu_ws=sgu_ws, sgu_bs=sgu_bs, w_proj_a=w_proj_a, w_proj_b=w_proj_b, w_out=w_out, norm_ffn_g=norm_ffn_g, w_gate_up=w_gate_up, w_down=w_down, norm_final_g=norm_final_g, loss_target=loss_target, m_norm_mix_g=m_norm_mix_g, m_w_in=m_w_in, m_conv_w=m_conv_w, m_conv_b=m_conv_b, m_rg_wa=m_rg_wa, m_rg_ba=m_rg_ba, m_rg_wx=m_rg_wx, m_rg_bx=m_rg_bx, m_rg_lambda=m_rg_lambda, m_sgu_ln_g=m_sgu_ln_g, m_sgu_ln_b=m_sgu_ln_b, m_sgu_ws=m_sgu_ws, m_sgu_bs=m_sgu_bs, m_w_proj_a=m_w_proj_a, m_w_proj_b=m_w_proj_b, m_w_out=m_w_out, m_norm_ffn_g=m_norm_ffn_g, m_w_gate_up=m_w_gate_up, m_w_down=m_w_down, m_norm_final_g=m_norm_final_g, v_norm_mix_g=v_norm_mix_g, v_w_in=v_w_in, v_conv_w=v_conv_w, v_conv_b=v_conv_b, v_rg_wa=v_rg_wa, v_rg_ba=v_rg_ba, v_rg_wx=v_rg_wx, v_rg_bx=v_rg_bx, v_rg_lambda=v_rg_lambda, v_sgu_ln_g=v_sgu_ln_g, v_sgu_ln_b=v_sgu_ln_b, v_sgu_ws=v_sgu_ws, v_sgu_bs=v_sgu_bs, v_w_proj_a=v_w_proj_a, v_w_proj_b=v_w_proj_b, v_w_out=v_w_out, v_norm_ffn_g=v_norm_ffn_g, v_w_gate_up=v_w_gate_up, v_w_down=v_w_down, v_norm_final_g=v_norm_final_g)
    weights = {n: given[n] for n in TWIN_WEIGHTS}
    shared = {n: given[n] for n in SHARED_INPUTS}
    per_example = {n: given[n] for n in ['x']}
    grad_fn = _jax.value_and_grad(_loss, argnums=(0, 1))

    def one_microbatch(ex, loss_target):
        ex = dict(ex)
        diff = ex.pop(TWIN_DIFF_INPUT)
        return grad_fn(weights, diff, {**shared, **ex}, loss_target)

    if N_MICROBATCH == 1:
        loss, (grad_w, grad_x) = one_microbatch(per_example, given["loss_target"])
    else:
        def body(carry, xs):
            loss_sum, grad_sum = carry
            l_k, (gw_k, gx_k) = one_microbatch(xs[0], xs[1])
            with _jax.named_scope("update"):
                return (loss_sum + l_k, _jax.tree.map(_jnp.add, grad_sum, gw_k)), gx_k

        init = (_jnp.zeros((), _jnp.float32), _jax.tree.map(_jnp.zeros_like, weights))
        (loss, grad_w), grad_x = _jax.lax.scan(body, init, (per_example, given["loss_target"]))
    with _jax.named_scope("update"):
        delta_w, new_m, new_v = {}, {}, {}
        for n in TWIN_WEIGHTS:
            delta_w[n], new_m[n], new_v[n] = _adamw(weights[n], grad_w[n], given["m_" + n], given["v_" + n])
    return (loss, grad_x, *[grad_w[n] for n in TWIN_WEIGHTS], *[delta_w[n] for n in TWIN_WEIGHTS],
            *[new_m[n] for n in TWIN_WEIGHTS], *[new_v[n] for n in TWIN_WEIGHTS])
```

```python
import functools

import jax
import jax.numpy as jnp
from jax import lax
from jax.experimental import pallas as pl
from jax.experimental.pallas import tpu as pltpu

F32 = jnp.float32
BF16 = jnp.bfloat16
S = jax.ShapeDtypeStruct

D = 1024
N_SHARD = 4
IN_COLS = 6 * D
IN_SHARD = IN_COLS // N_SHARD
D_FF = 2816
FF_SHARD = 2 * D_FF // N_SHARD
RG_BLOCK = 256
N_RG_BLOCK = D // RG_BLOCK
CHUNK = 128
N_GROUP = 8
CONV_WIDTH = 4
RG_C = 8.0
EPS = 1e-6
ADAM_LR, ADAM_B1, ADAM_B2, ADAM_EPS, ADAM_WD, ADAM_STEP = 0.001, 0.9, 0.999, 1e-08, 0.01, 10

V7X_VMEM_BYTES = 64 * 1024 * 1024
VMEM_LIMIT = V7X_VMEM_BYTES * 3 // 4
SUBLANES = 8
MESH = pl.DeviceIdType.MESH

TM_MM = 512
TM_SCAN = 256
TM_FF = 256


def _params(n_axes):
    return pltpu.CompilerParams(dimension_semantics=("arbitrary",) * n_axes, vmem_limit_bytes=VMEM_LIMIT)


def _resident(shape):
    nd = len(shape)
    return pl.BlockSpec(shape, lambda *_: (0,) * nd, pipeline_mode=pl.Buffered(1))


def _sig(x):
    return 1.0 / (1.0 + jnp.exp(-x))


_GELU_K = 0.7978845608028654
_GELU_C = 0.044715


def _gelu(x):
    return 0.5 * x * (1.0 + jnp.tanh(_GELU_K * (x + _GELU_C * x * x * x)))


def _gelu_and_grad(x):
    x2 = x * x
    th = jnp.tanh(_GELU_K * (x + _GELU_C * x2 * x))
    g = 0.5 * x * (1.0 + th)
    dg = 0.5 * (1.0 + th) + 0.5 * x * (1.0 - th * th) * (_GELU_K * (1.0 + 3.0 * _GELU_C * x2))
    return g, dg


def _one_minus_exp(x):
    p = 1.0 + x * (1.0 / 9.0)
    for k in (8.0, 7.0, 6.0, 5.0, 4.0, 3.0, 2.0):
        p = 1.0 + x * (1.0 / k) * p
    return jnp.where(x > -0.3, -x * p, 1.0 - jnp.exp(x))


def _softplus_neg(lam):
    z = -lam
    e = jnp.exp(-jnp.abs(z))
    u = 1.0 + e
    log1p = jnp.where(u == 1.0, e, jnp.log(u) * e / (u - 1.0))
    return jnp.maximum(z, 0.0) + log1p


def _rms_stats(x):
    return lax.rsqrt(jnp.mean(x * x, axis=-1, keepdims=True) + EPS)


def _rms_bwd(dy, x, g):
    rstd = _rms_stats(x)
    xhat = x * rstd
    dxhat = dy * g
    dx = rstd * (dxhat - xhat * jnp.mean(dxhat * xhat, axis=-1, keepdims=True))
    return dx, dy * xhat


def _colsum(x):
    return jnp.sum(x, axis=0, keepdims=True)


def _shift_down(x, d, fill):
    n = x.shape[0]
    if d % SUBLANES == 0:
        return jnp.concatenate([jnp.full((d, x.shape[1]), fill, x.dtype), x[:n - d]], axis=0)
    row = lax.broadcasted_iota(jnp.int32, x.shape, 0)
    return jnp.where(row < d, fill, pltpu.roll(x, d, 0))


def _shift_up(x, d, fill):
    n = x.shape[0]
    if d % SUBLANES == 0:
        return jnp.concatenate([x[d:], jnp.full((d, x.shape[1]), fill, x.dtype)], axis=0)
    row = lax.broadcasted_iota(jnp.int32, x.shape, 0)
    return jnp.where(row >= n - d, fill, pltpu.roll(x, n - d, 0))


def _scan(a, b, shift):
    d = 1
    while d < a.shape[0]:
        b = a * shift(b, d, 0.0) + b
        a = a * shift(a, d, 1.0)
        d *= 2
    return a, b


def _dot(a, b):
    return jnp.dot(a, b, preferred_element_type=F32)


def _dot_nt(a, b):
    return lax.dot_general(a, b, (((1,), (1,)), ((), ())), preferred_element_type=F32)


def _dot_tn(a, b):
    return lax.dot_general(a, b, (((0,), (0,)), ((), ())), preferred_element_type=F32)


def _inproj(x, g, w_in):
    T = x.shape[0]
    tm = min(TM_FF, T)

    def body(x_ref, g_ref, w_ref, proj_ref, h_ref):
        xv = x_ref[...]
        h = (xv * _rms_stats(xv) * g_ref[...]).astype(BF16)
        h_ref[...] = h
        for k in range(N_SHARD):
            proj_ref[:, IN_SHARD * k:IN_SHARD * (k + 1)] = _dot(h, w_ref[k])

    return pl.pallas_call(
        body, name="inproj", grid=(T // tm,),
        in_specs=[pl.BlockSpec((tm, D), lambda i: (i, 0)), pl.BlockSpec((1, D), lambda i: (0, 0)),
                  _resident((N_SHARD, D, IN_SHARD))],
        out_specs=[pl.BlockSpec((tm, IN_COLS), lambda i: (i, 0)), pl.BlockSpec((tm, D), lambda i: (i, 0))],
        out_shape=[S((T, IN_COLS), F32), S((T, D), BF16)],
        compiler_params=_params(1),
    )(x, g, w_in)


def _rg_gates(xc, wa_ref, wx_ref, ba, bx, sp):
    xb = xc.astype(BF16)
    blocks = [xb[:, RG_BLOCK * j:RG_BLOCK * (j + 1)] for j in range(N_RG_BLOCK)]
    r = _sig(jnp.concatenate([_dot(blocks[j], wa_ref[j]) for j in range(N_RG_BLOCK)], axis=1) + ba)
    gi = _sig(jnp.concatenate([_dot(blocks[j], wx_ref[j]) for j in range(N_RG_BLOCK)], axis=1) + bx)
    log_a = (-RG_C) * r * sp
    a = jnp.exp(log_a)
    m = jnp.sqrt(_one_minus_exp(2.0 * log_a))
    return xb, r, gi, a, m


def _rglru_fwd(proj, conv_w, conv_b, wa, wx, ba, bx, lam):
    T = proj.shape[0]
    tm = min(TM_SCAN, T)

    def body(rx_ref, gate_ref, cw_ref, cb_ref, wa_ref, wx_ref, ba_ref, bx_ref, lam_ref,
             ya_ref, xc_ref, h_ref, ext, hc):
        @pl.when(pl.program_id(0) == 0)
        def _():
            ext[0:SUBLANES, :] = jnp.zeros((SUBLANES, D), F32)
            hc[...] = jnp.zeros((SUBLANES, D), F32)

        ext[SUBLANES:SUBLANES + tm, :] = rx_ref[...]
        xc = cb_ref[...]
        for k in range(CONV_WIDTH):
            xc = xc + ext[pl.ds(SUBLANES - (CONV_WIDTH - 1) + k, tm), :] * cw_ref[k:k + 1, :]
        ext[0:SUBLANES, :] = ext[tm:tm + SUBLANES, :]
        xc_ref[...] = xc
        _, _, gi, a, m = _rg_gates(xc, wa_ref, wx_ref, ba_ref[...], bx_ref[...], _softplus_neg(lam_ref[...]))
        A, B = _scan(a, m * (gi * xc), _shift_down)
        h = A * hc[0:1, :] + B
        hc[...] = jnp.broadcast_to(h[tm - 1:tm, :], (SUBLANES, D))
        h_ref[...] = h
        ya_ref[...] = (_gelu(gate_ref[...]) * h).astype(BF16)

    vec = pl.BlockSpec((1, D), lambda i: (0, 0))
    blk = pl.BlockSpec((N_RG_BLOCK, RG_BLOCK, RG_BLOCK), lambda i: (0, 0, 0))
    tile = pl.BlockSpec((tm, D), lambda i: (i, 0))
    return pl.pallas_call(
        body, name="rglru_fwd", grid=(T // tm,),
        in_specs=[pl.BlockSpec((tm, D), lambda i: (i, 0)), pl.BlockSpec((tm, D), lambda i: (i, 1)),
                  pl.BlockSpec((CONV_WIDTH, D), lambda i: (0, 0)), vec, blk, blk, vec, vec, vec],
        out_specs=[tile, tile, tile],
        out_shape=[S((T, D), BF16), S((T, D), F32), S((T, D), F32)],
        scratch_shapes=[pltpu.VMEM((tm + SUBLANES, D), F32), pltpu.VMEM((SUBLANES, D), F32)],
        compiler_params=_params(1),
    )(proj, proj, conv_w, conv_b, wa, wx, ba, bx, lam)


def _layer_norm_stats(v):
    mu = jnp.mean(v, axis=-1, keepdims=True)
    vc = v - mu
    rstd = lax.rsqrt(jnp.mean(vc * vc, axis=-1, keepdims=True) + EPS)
    return vc * rstd, rstd


def _sgu_mix(w_ref, vnb, bst_ref, n_chunk):
    cols = []
    for g in range(N_GROUP):
        vg = vnb[:, CHUNK * g:CHUNK * (g + 1)].reshape(n_chunk, CHUNK, CHUNK)
        wb = jnp.broadcast_to(w_ref[g][None], (n_chunk, CHUNK, CHUNK))
        mg = lax.dot_general(wb, vg, (((2,), (1,)), ((0,), (0,))), preferred_element_type=F32)
        mg = mg + bst_ref[:, g:g + 1][None]
        cols.append(mg.reshape(n_chunk * CHUNK, CHUNK))
    return jnp.concatenate(cols, axis=1)


def _sgu_fwd(proj, ln_g, ln_b, w_tril, bs_t):
    T = proj.shape[0]
    tm = min(TM_MM, T)
    n_chunk = tm // CHUNK

    def body(u_ref, v_ref, g_ref, b_ref, w_ref, bst_ref, yb_ref):
        vhat, _ = _layer_norm_stats(_gelu(v_ref[...]))
        vnb = (vhat * g_ref[...] + b_ref[...]).astype(BF16)
        yb_ref[...] = (_gelu(u_ref[...]) * _sgu_mix(w_ref, vnb, bst_ref, n_chunk)).astype(BF16)

    vec = pl.BlockSpec((1, D), lambda i: (0, 0))
    return pl.pallas_call(
        body, name="sgu_fwd", grid=(T // tm,),
        in_specs=[pl.BlockSpec((tm, D), lambda i: (i, 2)), pl.BlockSpec((tm, D), lambda i: (i, 3)), vec, vec,
                  pl.BlockSpec((N_GROUP, CHUNK, CHUNK), lambda i: (0, 0, 0)),
                  pl.BlockSpec((CHUNK, N_GROUP), lambda i: (0, 0))],
        out_specs=pl.BlockSpec((tm, D), lambda i: (i, 0)),
        out_shape=S((T, D), BF16),
        compiler_params=_params(1),
    )(proj, proj, ln_g, ln_b, w_tril, bs_t)


def _merge_fwd(x, proj, ya, yb, wpa, wpb, wout):
    T = x.shape[0]
    tm = min(TM_MM, T)

    def body(x_ref, ga_ref, gb_ref, ya_ref, yb_ref, wpa_ref, wpb_ref, wout_ref, pa_ref, pb_ref, mb_ref, x1_ref):
        pa = _dot(ya_ref[...], wpa_ref[...])
        pb = _dot(yb_ref[...], wpb_ref[...])
        pa_ref[...] = pa
        pb_ref[...] = pb
        mb = (_sig(ga_ref[...]) * pa + _sig(gb_ref[...]) * pb).astype(BF16)
        mb_ref[...] = mb
        x1_ref[...] = x_ref[...] + _dot(mb, wout_ref[...])

    tile = pl.BlockSpec((tm, D), lambda i: (i, 0))
    w = _resident((D, D))
    return pl.pallas_call(
        body, name="merge_fwd", grid=(T // tm,),
        in_specs=[tile, pl.BlockSpec((tm, D), lambda i: (i, 4)), pl.BlockSpec((tm, D), lambda i: (i, 5)),
                  tile, tile, w, w, w],
        out_specs=[tile, tile, tile, tile],
        out_shape=[S((T, D), F32), S((T, D), F32), S((T, D), BF16), S((T, D), F32)],
        compiler_params=_params(1),
    )(x, proj, proj, ya, yb, wpa, wpb, wout)


def _ffn_up(x1, g, w_gu):
    T = x1.shape[0]
    tm = min(TM_FF, T)

    def body(x_ref, g_ref, w_ref, h2_ref, gu_ref, act_ref):
        xv = x_ref[...]
        h2 = (xv * _rms_stats(xv) * g_ref[...]).astype(BF16)
        h2_ref[...] = h2
        for k in range(N_SHARD):
            gu_ref[:, FF_SHARD * k:FF_SHARD * (k + 1)] = _dot(h2, w_ref[k])
        gate = gu_ref[:, 0:D_FF]
        act_ref[...] = (gate * _sig(gate) * gu_ref[:, D_FF:2 * D_FF]).astype(BF16)

    return pl.pallas_call(
        body, name="ffn_up", grid=(T // tm,),
        in_specs=[pl.BlockSpec((tm, D), lambda i: (i, 0)), pl.BlockSpec((1, D), lambda i: (0, 0)),
                  _resident((N_SHARD, D, FF_SHARD))],
        out_specs=[pl.BlockSpec((tm, D), lambda i: (i, 0)), pl.BlockSpec((tm, 2 * D_FF), lambda i: (i, 0)),
                   pl.BlockSpec((tm, D_FF), lambda i: (i, 0))],
        out_shape=[S((T, D), BF16), S((T, 2 * D_FF), F32), S((T, D_FF), BF16)],
        compiler_params=_params(1),
    )(x1, g, w_gu)


def _ffn_down_loss(x1, act, w_down, g_final, target):
    T = x1.shape[0]
    tm = min(TM_MM, T)

    def body(x_ref, act_ref, w_ref, g_ref, t_ref, dx2_ref, dx2b_ref, loss_ref, dg_ref):
        @pl.when(pl.program_id(0) == 0)
        def _():
            loss_ref[...] = jnp.zeros_like(loss_ref)
            dg_ref[...] = jnp.zeros_like(dg_ref)

        x2 = x_ref[...] + _dot(act_ref[...], w_ref[...])
        gf = g_ref[...]
        err = x2 * _rms_stats(x2) * gf - t_ref[...]
        loss_ref[...] += 0.5 * jnp.sum(jnp.mean(err * err, axis=-1, keepdims=True), axis=0, keepdims=True)
        dx2, dg_rows = _rms_bwd(err * (1.0 / D), x2, gf)
        dg_ref[...] += _colsum(dg_rows)
        dx2_ref[...] = dx2
        dx2b_ref[...] = dx2.astype(BF16)

    tile = pl.BlockSpec((tm, D), lambda i: (i, 0))
    vec = pl.BlockSpec((1, D), lambda i: (0, 0))
    return pl.pallas_call(
        body, name="ffn_down_loss", grid=(T // tm,),
        in_specs=[tile, pl.BlockSpec((tm, D_FF), lambda i: (i, 0)), _resident((D_FF, D)), vec, tile],
        out_specs=[tile, tile, pl.BlockSpec((1, 1), lambda i: (0, 0)), vec],
        out_shape=[S((T, D), F32), S((T, D), BF16), S((1, 1), F32), S((1, D), F32)],
        compiler_params=_params(1),
    )(x1, act, w_down, g_final, target)


def _ffn_bwd_act(dx2b, gu, w_down):
    T = dx2b.shape[0]
    tm = min(TM_FF, T)

    def body(dx_ref, gu_ref, w_ref, dgu_ref):
        dact = _dot_nt(dx_ref[...], w_ref[...])
        gate = gu_ref[:, 0:D_FF]
        sg = _sig(gate)
        dgu_ref[:, 0:D_FF] = (dact * gu_ref[:, D_FF:2 * D_FF] * (sg * (1.0 + gate * (1.0 - sg)))).astype(BF16)
        dgu_ref[:, D_FF:2 * D_FF] = (dact * (gate * sg)).astype(BF16)

    return pl.pallas_call(
        body, name="ffn_bwd_act", grid=(T // tm,),
        in_specs=[pl.BlockSpec((tm, D), lambda i: (i, 0)), pl.BlockSpec((tm, 2 * D_FF), lambda i: (i, 0)),
                  _resident((D_FF, D))],
        out_specs=pl.BlockSpec((tm, 2 * D_FF), lambda i: (i, 0)),
        out_shape=S((T, 2 * D_FF), BF16),
        compiler_params=_params(1),
    )(dx2b, gu, w_down)


def _ffn_bwd_in(dgu, w_gu, x1, g, dx2):
    T = x1.shape[0]
    tm = min(TM_MM, T)

    def body(dgu_ref, w_ref, x_ref, g_ref, dx2_ref, dx1_ref, dx1b_ref, dg_ref):
        @pl.when(pl.program_id(0) == 0)
        def _():
            dg_ref[...] = jnp.zeros_like(dg_ref)

        dh2 = _dot_nt(dgu_ref[:, 0:FF_SHARD], w_ref[0])
        for k in range(1, N_SHARD):
            dh2 = dh2 + _dot_nt(dgu_ref[:, FF_SHARD * k:FF_SHARD * (k + 1)], w_ref[k])
        dx, dg_rows = _rms_bwd(dh2, x_ref[...], g_ref[...])
        dg_ref[...] += _colsum(dg_rows)
        dx1 = dx2_ref[...] + dx
        dx1_ref[...] = dx1
        dx1b_ref[...] = dx1.astype(BF16)

    tile = pl.BlockSpec((tm, D), lambda i: (i, 0))
    vec = pl.BlockSpec((1, D), lambda i: (0, 0))
    return pl.pallas_call(
        body, name="ffn_bwd_in", grid=(T // tm,),
        in_specs=[pl.BlockSpec((tm, 2 * D_FF), lambda i: (i, 0)), _resident((N_SHARD, D, FF_SHARD)), tile, vec, tile],
        out_specs=[tile, tile, vec],
        out_shape=[S((T, D), F32), S((T, D), BF16), S((1, D), F32)],
        compiler_params=_params(1),
    )(dgu, w_gu, x1, g, dx2)


def _matmul_tn(a, b, tn, shard_major, name):
    T, M = a.shape
    N = b.shape[1]
    tk = min(TM_MM, T)

    def body(a_ref, b_ref, o_ref):
        @pl.when(pl.program_id(1) == 0)
        def _():
            o_ref[...] = jnp.zeros_like(o_ref)

        acc = _dot_tn(a_ref[...], b_ref[...])
        if shard_major:
            o_ref[0] += acc
        else:
            o_ref[...] += acc

    if shard_major:
        out_spec, out_shape = pl.BlockSpec((1, M, tn), lambda j, k: (j, 0, 0)), S((N // tn, M, tn), F32)
    else:
        out_spec, out_shape = pl.BlockSpec((M, tn), lambda j, k: (0, j)), S((M, N), F32)
    return pl.pallas_call(
        body, name=name, grid=(N // tn, T // tk),
        in_specs=[pl.BlockSpec((tk, M), lambda j, k: (k, 0)), pl.BlockSpec((tk, tn), lambda j, k: (k, j))],
        out_specs=out_spec, out_shape=out_shape,
        compiler_params=_params(2),
    )(a, b)


def _merge_bwd(dx1b, proj, pa, pb, wpa, wpb, wout):
    T = dx1b.shape[0]
    tm = min(TM_MM, T)

    def body(dx_ref, ga_ref, gb_ref, pa_ref, pb_ref, wpa_ref, wpb_ref, wout_ref,
             dpa_ref, dpb_ref, dgate_ref, dya_ref, dyb_ref):
        dm = _dot_nt(dx_ref[...], wout_ref[...])
        sa = _sig(ga_ref[...])
        sb = _sig(gb_ref[...])
        dpa = (dm * sa).astype(BF16)
        dpb = (dm * sb).astype(BF16)
        dpa_ref[...] = dpa
        dpb_ref[...] = dpb
        dgate_ref[:, 0:D] = (dm * pa_ref[...] * (sa * (1.0 - sa))).astype(BF16)
        dgate_ref[:, D:2 * D] = (dm * pb_ref[...] * (sb * (1.0 - sb))).astype(BF16)
        dya_ref[...] = _dot_nt(dpa, wpa_ref[...])
        dyb_ref[...] = _dot_nt(dpb, wpb_ref[...])

    tile = pl.BlockSpec((tm, D), lambda i: (i, 0))
    w = _resident((D, D))
    return pl.pallas_call(
        body, name="merge_bwd", grid=(T // tm,),
        in_specs=[tile, pl.BlockSpec((tm, D), lambda i: (i, 4)), pl.BlockSpec((tm, D), lambda i: (i, 5)),
                  tile, tile, w, w, w],
        out_specs=[tile, tile, pl.BlockSpec((tm, 2 * D), lambda i: (i, 0)), tile, tile],
        out_shape=[S((T, D), BF16), S((T, D), BF16), S((T, 2 * D), BF16), S((T, D), F32), S((T, D), F32)],
        compiler_params=_params(1),
    )(dx1b, proj, proj, pa, pb, wpa, wpb, wout)


def _rglru_bwd(dya, proj, hseq, xc, conv_w, wa, wx, ba, bx, lam):
    T = dya.shape[0]
    tm = min(TM_SCAN, T)
    n = T // tm
    per8 = tm // SUBLANES

    def body(dya_ref, rx_ref, rxp_ref, gate_ref, h_ref, hp_ref, xc_ref, cw_ref, wa_ref, wx_ref, ba_ref, bx_ref, lam_ref,
             dab_ref, dcw_ref, dcb_ref, dwa_ref, dwx_ref, dba_ref, dbx_ref, dlam_ref,
             hext, rext, dext, carry_a, carry_dh):
        i = pl.program_id(0)
        first_tile = i == n - 1

        @pl.when(i == 0)
        def _():
            for ref in (dcw_ref, dcb_ref, dwa_ref, dwx_ref, dba_ref, dbx_ref, dlam_ref, carry_a, carry_dh):
                ref[...] = jnp.zeros_like(ref)
            dext[tm:tm + SUBLANES, :] = jnp.zeros((SUBLANES, D), F32)

        gel, dgel = _gelu_and_grad(gate_ref[...])
        dya_v = dya_ref[...]
        hseq_v = h_ref[...]
        dgate = dya_v * hseq_v * dgel
        xcv = xc_ref[...]
        lam_v = lam_ref[...]
        sp = _softplus_neg(lam_v)
        xb, r, gi, a, m = _rg_gates(xcv, wa_ref, wx_ref, ba_ref[...], bx_ref[...], sp)

        row = lax.broadcasted_iota(jnp.int32, (tm, D), 0)
        c = jnp.where(row == tm - 1, carry_a[0:1, :], _shift_up(a, 1, 0.0))
        C, G = _scan(c, dya_v * gel, _shift_up)
        dH = G + C * carry_dh[0:1, :]
        carry_a[...] = jnp.broadcast_to(a[0:1, :], (SUBLANES, D))
        carry_dh[...] = jnp.broadcast_to(dH[0:1, :], (SUBLANES, D))

        hext[0:SUBLANES, :] = jnp.where(first_tile, 0.0, hp_ref[...])
        hext[SUBLANES:SUBLANES + tm, :] = hseq_v
        h_prev = hext[pl.ds(SUBLANES - 1, tm), :]

        d_m = dH * (gi * xcv)
        d_la = dH * h_prev * a - d_m * (a * a) / m
        d_ia = dH * m * xcv * (gi * (1.0 - gi))
        d_ra = d_la * ((-RG_C) * sp) * (r * (1.0 - r))
        dlam_ref[...] += _colsum(d_la * ((-RG_C) * r)) * (-_sig(-lam_v))
        dba_ref[...] += _colsum(d_ra)
        dbx_ref[...] += _colsum(d_ia)
        drab = d_ra.astype(BF16)
        diab = d_ia.astype(BF16)
        dxc_cols = []
        for j in range(N_RG_BLOCK):
            sl = slice(RG_BLOCK * j, RG_BLOCK * (j + 1))
            dxc_cols.append(_dot_nt(drab[:, sl], wa_ref[j]) + _dot_nt(diab[:, sl], wx_ref[j]))
            dwa_ref[j] += _dot_tn(xb[:, sl], drab[:, sl])
            dwx_ref[j] += _dot_tn(xb[:, sl], diab[:, sl])
        dxc = dH * m * gi + jnp.concatenate(dxc_cols, axis=1)

        dcb_ref[...] += _colsum(dxc)
        dext[0:tm, :] = dxc
        rext[0:SUBLANES, :] = jnp.where(first_tile, 0.0, rxp_ref[...])
        rext[SUBLANES:SUBLANES + tm, :] = rx_ref[...]
        drx = jnp.zeros((tm, D), F32)
        for k in range(CONV_WIDTH):
            drx = drx + dext[pl.ds(CONV_WIDTH - 1 - k, tm), :] * cw_ref[k:k + 1, :]
            dcw_ref[k:k + 1, :] += _colsum(dxc * rext[pl.ds(SUBLANES - (CONV_WIDTH - 1) + k, tm), :])
        dext[tm:tm + SUBLANES, :] = dext[0:SUBLANES, :]
        dab_ref[:, 0:D] = drx.astype(BF16)
        dab_ref[:, D:2 * D] = dgate.astype(BF16)

    def rev(col):
        return lambda i: (n - 1 - i, col)

    def prev8(col):
        return lambda i: (jnp.maximum((n - 1 - i) * per8 - 1, 0), col)

    tile = pl.BlockSpec((tm, D), rev(0))
    vec = pl.BlockSpec((1, D), lambda i: (0, 0))
    blk = pl.BlockSpec((N_RG_BLOCK, RG_BLOCK, RG_BLOCK), lambda i: (0, 0, 0))
    cw = pl.BlockSpec((CONV_WIDTH, D), lambda i: (0, 0))
    return pl.pallas_call(
        body, name="rglru_bwd", grid=(n,),
        in_specs=[tile, pl.BlockSpec((tm, D), rev(0)), pl.BlockSpec((SUBLANES, D), prev8(0)),
                  pl.BlockSpec((tm, D), rev(1)), tile, pl.BlockSpec((SUBLANES, D), prev8(0)), tile,
                  cw, blk, blk, vec, vec, vec],
        out_specs=[pl.BlockSpec((tm, 2 * D), rev(0)), cw, vec, blk, blk, vec, vec, vec],
        out_shape=[S((T, 2 * D), BF16), S((CONV_WIDTH, D), F32), S((1, D), F32),
                   S((N_RG_BLOCK, RG_BLOCK, RG_BLOCK), F32), S((N_RG_BLOCK, RG_BLOCK, RG_BLOCK), F32),
                   S((1, D), F32), S((1, D), F32), S((1, D), F32)],
        scratch_shapes=[pltpu.VMEM((tm + SUBLANES, D), F32), pltpu.VMEM((tm + SUBLANES, D), F32),
                        pltpu.VMEM((tm + SUBLANES, D), F32), pltpu.VMEM((SUBLANES, D), F32),
                        pltpu.VMEM((SUBLANES, D), F32)],
        compiler_params=_params(1),
    )(dya, proj, proj, proj, hseq, hseq, xc, conv_w, wa, wx, ba, bx, lam)


def _sgu_bwd(dyb, proj, ln_g, ln_b, w_tril, w_tril_t, bs_t):
    T = dyb.shape[0]
    tm = min(TM_MM, T)
    n_chunk = tm // CHUNK

    def body(dyb_ref, u_ref, v_ref, g_ref, b_ref, w_ref, wt_ref, bst_ref,
             duv_ref, dw_ref, dbst_ref, dg_ref, db_ref):
        @pl.when(pl.program_id(0) == 0)
        def _():
            for ref in (dw_ref, dbst_ref, dg_ref, db_ref):
                ref[...] = jnp.zeros_like(ref)

        gu, dgu = _gelu_and_grad(u_ref[...])
        gv, dgv = _gelu_and_grad(v_ref[...])
        vhat, rstd = _layer_norm_stats(gv)
        lng = g_ref[...]
        vnb = (vhat * lng + b_ref[...]).astype(BF16)
        mixed = _sgu_mix(w_ref, vnb, bst_ref, n_chunk)
        dyb_v = dyb_ref[...]
        duv_ref[:, 0:D] = (dyb_v * mixed * dgu).astype(BF16)
        dmix = dyb_v * gu
        dmb = dmix.astype(BF16)
        keep = (lax.broadcasted_iota(jnp.int32, (CHUNK, CHUNK), 0)
                >= lax.broadcasted_iota(jnp.int32, (CHUNK, CHUNK), 1))
        dvn_cols, dbs_cols = [], []
        for g in range(N_GROUP):
            sl = slice(CHUNK * g, CHUNK * (g + 1))
            dmg = dmb[:, sl].reshape(n_chunk, CHUNK, CHUNK)
            vg = vnb[:, sl].reshape(n_chunk, CHUNK, CHUNK)
            wtb = jnp.broadcast_to(wt_ref[g][None], (n_chunk, CHUNK, CHUNK))
            dvn = lax.dot_general(wtb, dmg, (((2,), (1,)), ((0,), (0,))), preferred_element_type=F32)
            dvn_cols.append(dvn.reshape(tm, CHUNK))
            dw = lax.dot_general(dmg, vg, (((2,), (2,)), ((0,), (0,))), preferred_element_type=F32)
            dw_ref[g] += jnp.where(keep, jnp.sum(dw, axis=0), 0.0)
            rows = jnp.sum(dmix[:, sl], axis=1, keepdims=True)
            dbs_cols.append(jnp.sum(rows.reshape(n_chunk, CHUNK, 1), axis=0))
        dbst_ref[...] += jnp.concatenate(dbs_cols, axis=1)
        dvn = jnp.concatenate(dvn_cols, axis=1)
        dg_ref[...] += _colsum(dvn * vhat)
        db_ref[...] += _colsum(dvn)
        dvhat = dvn * lng
        dgv_in = rstd * (dvhat - jnp.mean(dvhat, axis=-1, keepdims=True)
                         - vhat * jnp.mean(dvhat * vhat, axis=-1, keepdims=True))
        duv_ref[:, D:2 * D] = (dgv_in * dgv).astype(BF16)

    tile = pl.BlockSpec((tm, D), lambda i: (i, 0))
    vec = pl.BlockSpec((1, D), lambda i: (0, 0))
    wsp = pl.BlockSpec((N_GROUP, CHUNK, CHUNK), lambda i: (0, 0, 0))
    bsp = pl.BlockSpec((CHUNK, N_GROUP), lambda i: (0, 0))
    return pl.pallas_call(
        body, name="sgu_bwd", grid=(T // tm,),
        in_specs=[tile, pl.BlockSpec((tm, D), lambda i: (i, 2)), pl.BlockSpec((tm, D), lambda i: (i, 3)),
                  vec, vec, wsp, wsp, bsp],
        out_specs=[pl.BlockSpec((tm, 2 * D), lambda i: (i, 0)), wsp, bsp, vec, vec],
        out_shape=[S((T, 2 * D), BF16), S((N_GROUP, CHUNK, CHUNK), F32), S((CHUNK, N_GROUP), F32),
                   S((1, D), F32), S((1, D), F32)],
        compiler_params=_params(1),
    )(dyb, proj, proj, ln_g, ln_b, w_tril, w_tril_t, bs_t)


def _dproj_columns(k, da, db, dg):
    if k == 0:
        return da[:, 0:IN_SHARD]
    if k == 1:
        return jnp.concatenate([da[:, IN_SHARD:2 * D], db[:, 0:D]], axis=1)
    if k == 2:
        return jnp.concatenate([db[:, D:2 * D], dg[:, 0:D // 2]], axis=1)
    return dg[:, D // 2:2 * D]


def _inproj_bwd(da, db, dg, w_in, x, g, dx1):
    T = x.shape[0]
    tm = min(TM_MM, T)

    def body(da_ref, db_ref, dg_ref, w_ref, x_ref, g_ref, dx1_ref, dx_ref, dgm_ref):
        @pl.when(pl.program_id(0) == 0)
        def _():
            dgm_ref[...] = jnp.zeros_like(dgm_ref)

        dav, dbv, dgv = da_ref[...], db_ref[...], dg_ref[...]
        dh = _dot_nt(_dproj_columns(0, dav, dbv, dgv), w_ref[0])
        for k in range(1, N_SHARD):
            dh = dh + _dot_nt(_dproj_columns(k, dav, dbv, dgv), w_ref[k])
        dx, dg_rows = _rms_bwd(dh, x_ref[...], g_ref[...])
        dgm_ref[...] += _colsum(dg_rows)
        dx_ref[...] = dx1_ref[...] + dx

    tile = pl.BlockSpec((tm, D), lambda i: (i, 0))
    wide = pl.BlockSpec((tm, 2 * D), lambda i: (i, 0))
    vec = pl.BlockSpec((1, D), lambda i: (0, 0))
    return pl.pallas_call(
        body, name="inproj_bwd", grid=(T // tm,),
        in_specs=[wide, wide, wide, _resident((N_SHARD, D, IN_SHARD)), tile, vec, tile],
        out_specs=[tile, vec],
        out_shape=[S((T, D), F32), S((1, D), F32)],
        compiler_params=_params(1),
    )(da, db, dg, w_in, x, g, dx1)


def _inproj_wgrad(h, da, db, dg):
    T = h.shape[0]
    tk = min(TM_MM, T)

    def body(h_ref, da_ref, db_ref, dg_ref, o_ref):
        @pl.when(pl.program_id(1) == 0)
        def _():
            o_ref[...] = jnp.zeros_like(o_ref)

        for k in range(N_SHARD):
            @pl.when(pl.program_id(0) == k)
            def _(k=k):
                o_ref[0] += _dot_tn(h_ref[...], _dproj_columns(k, da_ref[...], db_ref[...], dg_ref[...]))

    wide = pl.BlockSpec((tk, 2 * D), lambda j, k: (k, 0))
    return pl.pallas_call(
        body, name="inproj_wgrad", grid=(N_SHARD, T // tk),
        in_specs=[pl.BlockSpec((tk, D), lambda j, k: (k, 0)), wide, wide, wide],
        out_specs=pl.BlockSpec((1, D, IN_SHARD), lambda j, k: (j, 0, 0)),
        out_shape=S((N_SHARD, D, IN_SHARD), F32),
        compiler_params=_params(2),
    )(h, da, db, dg)


def _blockdiag(w):
    hd = w.shape[-1]
    per = RG_BLOCK // hd
    w4 = w.reshape(N_RG_BLOCK, per, hd, hd)
    out = jnp.zeros((N_RG_BLOCK, per, hd, per, hd), w.dtype)
    for h in range(per):
        out = out.at[:, h, :, h, :].set(w4[:, h])
    return out.reshape(N_RG_BLOCK, RG_BLOCK, RG_BLOCK)


def _blockdiag_heads(g, hd):
    per = RG_BLOCK // hd
    g5 = g.reshape(N_RG_BLOCK, per, hd, per, hd)
    return jnp.stack([g5[:, h, :, h, :] for h in range(per)], axis=1).reshape(N_RG_BLOCK * per, hd, hd)


def _local_step(x, target, w_in, wpa, wpb, wout, w_gu, w_down, small):
    norm_mix_g, conv_w, conv_b = small["norm_mix_g"], small["conv_w"], small["conv_b"]
    lam, ln_g, ln_b = small["rg_lambda"], small["sgu_ln_g"], small["sgu_ln_b"]
    hd = small["rg_wa"].shape[-1]
    wa = _blockdiag(small["rg_wa"]).astype(BF16)
    wx = _blockdiag(small["rg_wx"]).astype(BF16)
    ba = small["rg_ba"].reshape(1, D)
    bx = small["rg_bx"].reshape(1, D)
    causal = jnp.tril(jnp.ones((CHUNK, CHUNK), bool))
    w_tril = jnp.where(causal[None], small["sgu_ws"], 0.0).astype(BF16)
    w_tril_t = jnp.swapaxes(w_tril, 1, 2)
    bs_t = small["sgu_bs"].reshape(N_GROUP, CHUNK).T

    proj, h = _inproj(x, norm_mix_g, w_in)
    ya, xc, hseq = _rglru_fwd(proj, conv_w, conv_b, wa, wx, ba, bx, lam)
    yb = _sgu_fwd(proj, ln_g, ln_b, w_tril, bs_t)
    pa, pb, mb, x1 = _merge_fwd(x, proj, ya, yb, wpa, wpb, wout)
    h2, gu, act = _ffn_up(x1, small["norm_ffn_g"], w_gu)
    dx2, dx2b, loss, d_final_g = _ffn_down_loss(x1, act, w_down, small["norm_final_g"], target)

    dgu = _ffn_bwd_act(dx2b, gu, w_down)
    dx1, dx1b, d_ffn_g = _ffn_bwd_in(dgu, w_gu, x1, small["norm_ffn_g"], dx2)
    g_down = _matmul_tn(act, dx2b, D // 2, False, "wgrad_down")
    g_gu = _matmul_tn(h2, dgu, FF_SHARD, True, "wgrad_gate_up")
    dpa, dpb, dgate, dya, dyb = _merge_bwd(dx1b, proj, pa, pb, wpa, wpb, wout)
    g_out = _matmul_tn(mb, dx1b, D, False, "wgrad_out")
    g_pa = _matmul_tn(ya, dpa, D, False, "wgrad_proj_a")
    g_pb = _matmul_tn(yb, dpb, D, False, "wgrad_proj_b")
    dab, d_cw, d_cb, d_wa, d_wx, d_ba, d_bx, d_lam = _rglru_bwd(dya, proj, hseq, xc, conv_w, wa, wx, ba, bx, lam)
    duv, d_ws, d_bst, d_lng, d_lnb = _sgu_bwd(dyb, proj, ln_g, ln_b, w_tril, w_tril_t, bs_t)
    grad_x, d_mix_g = _inproj_bwd(dab, duv, dgate, w_in, x, norm_mix_g, dx1)
    g_in = _inproj_wgrad(h, dab, duv, dgate)

    small_grads = {
        "norm_mix_g": d_mix_g, "conv_w": d_cw, "conv_b": d_cb,
        "rg_wa": _blockdiag_heads(d_wa, hd), "rg_ba": d_ba.reshape(-1, hd),
        "rg_wx": _blockdiag_heads(d_wx, hd), "rg_bx": d_bx.reshape(-1, hd),
        "rg_lambda": d_lam, "sgu_ln_g": d_lng, "sgu_ln_b": d_lnb, "sgu_ws": d_ws, "sgu_bs": d_bst.T,
        "norm_ffn_g": d_ffn_g, "norm_final_g": d_final_g,
    }
    return loss, grad_x, (g_in, g_pa, g_pb, g_out, g_gu, g_down), small_grads


def _position():
    return lax.axis_index("x"), lax.axis_index("y"), lax.axis_index("c")


def _other_chips(x, y):
    return [(1 - x, y), (x, 1 - y), (1 - x, 1 - y)]


_ANY = pl.BlockSpec(memory_space=pl.ANY)


def _allgather_weights(shards):
    n = len(shards)

    def body(*refs):
        ins, outs = refs[:n], refs[n:2 * n]
        send, recv, local = refs[2 * n:]
        x, y, c = _position()
        mine = 2 * x + y
        chips = _other_chips(x, y)

        def push(w, j):
            px, py = chips[j]
            return pltpu.make_async_remote_copy(
                src_ref=ins[w], dst_ref=outs[w].at[mine], send_sem=send.at[3 * w + j], recv_sem=recv.at[3 * w + j],
                device_id=(px, py, c), device_id_type=MESH)

        def arrival(w, j):
            px, py = chips[j]
            return pltpu.make_async_remote_copy(
                src_ref=ins[w], dst_ref=outs[w].at[2 * px + py], send_sem=send.at[3 * w + j],
                recv_sem=recv.at[3 * w + j], device_id=(px, py, c), device_id_type=MESH)

        own = [pltpu.make_async_copy(ins[w], outs[w].at[mine], local.at[w]) for w in range(n)]
        pushes = [push(w, j) for w in range(n) for j in range(3)]
        for cp in pushes + own:
            cp.start()
        for w in range(n):
            for j in range(3):
                arrival(w, j).wait_recv()
        for cp in pushes:
            cp.wait_send()
        for cp in own:
            cp.wait()

    return pl.pallas_call(
        body, name="allgather_weights",
        in_specs=[_ANY] * n, out_specs=[_ANY] * n,
        out_shape=[S((N_SHARD,) + s.shape, s.dtype) for s in shards],
        scratch_shapes=[pltpu.SemaphoreType.DMA((3 * n,)), pltpu.SemaphoreType.DMA((3 * n,)),
                        pltpu.SemaphoreType.DMA((n,))],
    )(*shards)


def _exchange_halves(grads):
    n = len(grads)

    def body(*refs):
        ins, outs = refs[:n], refs[n:2 * n]
        send, recv = refs[2 * n:]
        x, y, c = _position()
        copies = []
        for w in range(n):
            half = ins[w].shape[1] // 2
            copies.append(pltpu.make_async_remote_copy(
                src_ref=ins[w].at[:, pl.ds((1 - c) * half, half), :], dst_ref=outs[w],
                send_sem=send.at[w], recv_sem=recv.at[w], device_id=(x, y, 1 - c), device_id_type=MESH))
        for cp in copies:
            cp.start()
        for cp in copies:
            cp.wait()

    return pl.pallas_call(
        body, name="exchange_halves",
        in_specs=[_ANY] * n, out_specs=[_ANY] * n,
        out_shape=[S((N_SHARD, g.shape[1] // 2, g.shape[2]), F32) for g in grads],
        scratch_shapes=[pltpu.SemaphoreType.DMA((n,)), pltpu.SemaphoreType.DMA((n,))],
    )(*grads)


def _row_tile(rows):
    for t in range(256, 0, -SUBLANES):
        if rows % t == 0:
            return t
    raise ValueError(rows)


def _add_halves(core, g, theirs, name):
    _, r, cols = g.shape
    half = r // 2
    tr = _row_tile(half)
    nb = half // tr

    def body(core_ref, g_ref, t_ref, o32_ref, o16_ref):
        s = g_ref[...] + t_ref[...]
        o32_ref[...] = s
        o16_ref[...] = s.astype(BF16)

    blk = pl.BlockSpec((1, tr, cols), lambda s, i, core_ref: (s, i, 0))
    gs = pltpu.PrefetchScalarGridSpec(
        num_scalar_prefetch=1, grid=(N_SHARD, nb),
        in_specs=[pl.BlockSpec((1, tr, cols), lambda s, i, core_ref: (s, core_ref[0] * nb + i, 0)), blk],
        out_specs=[blk, blk])
    return pl.pallas_call(
        body, name=name, grid_spec=gs,
        out_shape=[S((N_SHARD, half, cols), F32), S((N_SHARD, half, cols), BF16)],
        compiler_params=_params(2),
    )(core, g, theirs)


def _scatter_shards(partials):
    n = len(partials)

    def body(*refs):
        ins, outs = refs[:n], refs[n:2 * n]
        send, recv = refs[2 * n:]
        x, y, c = _position()
        chips = _other_chips(x, y)
        copies = []
        for w in range(n):
            for j, (px, py) in enumerate(chips):
                copies.append(pltpu.make_async_remote_copy(
                    src_ref=ins[w].at[2 * px + py], dst_ref=outs[w].at[j],
                    send_sem=send.at[3 * w + j], recv_sem=recv.at[3 * w + j],
                    device_id=(px, py, c), device_id_type=MESH))
        for cp in copies:
            cp.start()
        for cp in copies:
            cp.wait()

    return pl.pallas_call(
        body, name="scatter_shards",
        in_specs=[_ANY] * n, out_specs=[_ANY] * n,
        out_shape=[S((3,) + p.shape[1:], BF16) for p in partials],
        scratch_shapes=[pltpu.SemaphoreType.DMA((3 * n,)), pltpu.SemaphoreType.DMA((3 * n,))],
    )(*partials)


def _sum_shards(chip, own, others, name):
    _, half, cols = own.shape
    tr = _row_tile(half)

    def body(chip_ref, own_ref, oth_ref, o_ref):
        acc = own_ref[0]
        for j in range(3):
            acc = acc + oth_ref[j].astype(F32)
        o_ref[...] = acc

    gs = pltpu.PrefetchScalarGridSpec(
        num_scalar_prefetch=1, grid=(half // tr,),
        in_specs=[pl.BlockSpec((1, tr, cols), lambda i, chip_ref: (chip_ref[0], i, 0)),
                  pl.BlockSpec((3, tr, cols), lambda i, chip_ref: (0, i, 0))],
        out_specs=pl.BlockSpec((tr, cols), lambda i, chip_ref: (i, 0)))
    return pl.pallas_call(
        body, name=name, grid_spec=gs, out_shape=S((half, cols), F32), compiler_params=_params(1),
    )(chip, own, others)


def _swap_halves(halves):
    n = len(halves)

    def body(*refs):
        ins, outs = refs[:n], refs[n:2 * n]
        send, recv = refs[2 * n:]
        x, y, c = _position()
        copies = [pltpu.make_async_remote_copy(
            src_ref=ins[w], dst_ref=outs[w], send_sem=send.at[w], recv_sem=recv.at[w],
            device_id=(x, y, 1 - c), device_id_type=MESH) for w in range(n)]
        for cp in copies:
            cp.start()
        for cp in copies:
            cp.wait()

    return pl.pallas_call(
        body, name="swap_halves",
        in_specs=[_ANY] * n, out_specs=[_ANY] * n,
        out_shape=[S(h.shape, F32) for h in halves],
        scratch_shapes=[pltpu.SemaphoreType.DMA((n,)), pltpu.SemaphoreType.DMA((n,))],
    )(*halves)


def _adamw(w, g, m, v):
    m = ADAM_B1 * m + (1.0 - ADAM_B1) * g
    v = ADAM_B2 * v + (1.0 - ADAM_B2) * (g * g)
    m_hat = m / (1.0 - ADAM_B1 ** ADAM_STEP)
    v_hat = v / (1.0 - ADAM_B2 ** ADAM_STEP)
    delta = -ADAM_LR * (m_hat / (jnp.sqrt(v_hat) + ADAM_EPS) + ADAM_WD * w)
    return delta, m, v


def _adamw_shard(core, mine, theirs, w, m, v, name):
    r, cols = w.shape
    half = r // 2
    tr = _row_tile(half)
    nb = half // tr

    def body(core_ref, mine_ref, theirs_ref, w_ref, m_ref, v_ref, g_ref, d_ref, mo_ref, vo_ref):
        g = jnp.where(pl.program_id(0) == core_ref[0], mine_ref[...], theirs_ref[...])
        g_ref[...] = g
        d_ref[...], mo_ref[...], vo_ref[...] = _adamw(w_ref[...], g, m_ref[...], v_ref[...])

    hblk = pl.BlockSpec((tr, cols), lambda h, i, core_ref: (i, 0))
    blk = pl.BlockSpec((tr, cols), lambda h, i, core_ref: (h * nb + i, 0))
    gs = pltpu.PrefetchScalarGridSpec(num_scalar_prefetch=1, grid=(2, nb),
                                      in_specs=[hblk, hblk, blk, blk, blk], out_specs=[blk] * 4)
    return pl.pallas_call(
        body, name=name, grid_spec=gs, out_shape=[S((r, cols), F32)] * 4, compiler_params=_params(2),
    )(core, mine, theirs, w, m, v)


def _adamw_whole(w, g, m, v, name):
    def body(w_ref, g_ref, m_ref, v_ref, d_ref, mo_ref, vo_ref):
        d_ref[...], mo_ref[...], vo_ref[...] = _adamw(w_ref[...], g_ref[...], m_ref[...], v_ref[...])

    return pl.pallas_call(body, name=name, out_shape=[S(w.shape, F32)] * 3)(w, g, m, v)


def _small_reduce_adamw(g, w, m, v):
    rows = g.shape[0]

    def body(g_ref, w_ref, m_ref, v_ref, gs_ref, d_ref, mo_ref, vo_ref, sib, slots, send, recv):
        x, y, c = _position()
        mine = 2 * x + y
        chips = _other_chips(x, y)
        swap = pltpu.make_async_remote_copy(src_ref=g_ref, dst_ref=sib, send_sem=send.at[0], recv_sem=recv.at[0],
                                            device_id=(x, y, 1 - c), device_id_type=MESH)
        swap.start()
        swap.wait()
        slots[mine] = g_ref[...] + sib[...]

        def chip_copy(j, slot):
            px, py = chips[j]
            return pltpu.make_async_remote_copy(
                src_ref=slots.at[mine], dst_ref=slots.at[slot], send_sem=send.at[1 + j], recv_sem=recv.at[1 + j],
                device_id=(px, py, c), device_id_type=MESH)

        pushes = [chip_copy(j, mine) for j in range(3)]
        for cp in pushes:
            cp.start()
        for j, (px, py) in enumerate(chips):
            chip_copy(j, 2 * px + py).wait_recv()
        for cp in pushes:
            cp.wait_send()
        total = ((slots[0] + slots[1]) + slots[2]) + slots[3]
        gs_ref[...] = total
        d_ref[...], mo_ref[...], vo_ref[...] = _adamw(w_ref[...], total, m_ref[...], v_ref[...])

    return pl.pallas_call(
        body, name="small_reduce_adamw", out_shape=[S((rows, D), F32)] * 4,
        scratch_shapes=[pltpu.VMEM((rows, D), F32), pltpu.VMEM((N_SHARD, rows, D), F32),
                        pltpu.SemaphoreType.DMA((4,)), pltpu.SemaphoreType.DMA((4,))],
        compiler_params=pltpu.CompilerParams(vmem_limit_bytes=VMEM_LIMIT),
    )(g, w, m, v)


_BIG = ("w_in", "w_proj_a", "w_proj_b", "w_out", "w_gate_up", "w_down")
_SMALL = ("norm_mix_g", "conv_b", "rg_wa", "rg_ba", "rg_wx", "rg_bx", "rg_lambda", "sgu_ln_g", "sgu_ln_b",
          "sgu_ws", "sgu_bs", "norm_ffn_g", "norm_final_g")
_WEIGHTS = ("norm_mix_g", "w_in", "conv_w", "conv_b", "rg_wa", "rg_ba", "rg_wx", "rg_bx", "rg_lambda", "sgu_ln_g",
            "sgu_ln_b", "sgu_ws", "sgu_bs", "w_proj_a", "w_proj_b", "w_out", "norm_ffn_g", "w_gate_up", "w_down",
            "norm_final_g")


def _pack(arrays):
    rows = [a.reshape(-1, D) for a in arrays]
    pad = -sum(r.shape[0] for r in rows) % SUBLANES
    return jnp.concatenate(rows + [jnp.zeros((pad, D), F32)], axis=0)


def _unpack(packed, like):
    out, row = [], 0
    for a in like:
        n = a.size // D
        out.append(packed[row:row + n].reshape(a.shape))
        row += n
    return out


def kernel(x, norm_mix_g, w_in, conv_w, conv_b, rg_wa, rg_ba, rg_wx, rg_bx, rg_lambda, sgu_ln_g, sgu_ln_b, sgu_ws, sgu_bs, w_proj_a, w_proj_b, w_out, norm_ffn_g, w_gate_up, w_down, norm_final_g, loss_target, m_norm_mix_g, m_w_in, m_conv_w, m_conv_b, m_rg_wa, m_rg_ba, m_rg_wx, m_rg_bx, m_rg_lambda, m_sgu_ln_g, m_sgu_ln_b, m_sgu_ws, m_sgu_bs, m_w_proj_a, m_w_proj_b, m_w_out, m_norm_ffn_g, m_w_gate_up, m_w_down, m_norm_final_g, v_norm_mix_g, v_w_in, v_conv_w, v_conv_b, v_rg_wa, v_rg_ba, v_rg_wx, v_rg_bx, v_rg_lambda, v_sgu_ln_g, v_sgu_ln_b, v_sgu_ws, v_sgu_bs, v_w_proj_a, v_w_proj_b, v_w_out, v_norm_ffn_g, v_w_gate_up, v_w_down, v_norm_final_g):
    args = dict(locals())
    w = {n: args[n] for n in _WEIGHTS}
    mom = {n: args["m_" + n] for n in _WEIGHTS}
    var = {n: args["v_" + n] for n in _WEIGHTS}
    xi, yi, ci = _position()
    core = ci.astype(jnp.int32).reshape(1)
    chip = (2 * xi + yi).astype(jnp.int32).reshape(1)

    shards = [w[n][0].astype(BF16) for n in _BIG] + [w["conv_w"][0]]
    w_in_a, w_pa_a, w_pb_a, w_out_a, w_gu_a, w_down_a, conv_a = _allgather_weights(shards)
    conv_cols = conv_a.shape[-1]
    small = {n: w[n][0] if w[n].size > D else w[n].reshape(1, D) for n in _SMALL}
    small["conv_w"] =jnp.swapaxes(conv_a, 0, 1).reshape(CONV_WIDTH, D)

    loss, grad_x, big_grads, small_grads = _local_step(
        x[0], loss_target[0], w_in_a, w_pa_a.reshape(D, D), w_pb_a.reshape(D, D), w_out_a.reshape(D, D),
        w_gu_a, w_down_a.reshape(D_FF, D), small)

    big_grads = [g.reshape(N_SHARD, g.shape[0] // N_SHARD, g.shape[1]) if g.ndim == 2 else g for g in big_grads]
    theirs = _exchange_halves(big_grads)
    chip_sums = [_add_halves(core, g, t, "add_halves_" + n) for n, g, t in zip(_BIG, big_grads, theirs)]
    arrived = _scatter_shards([p16 for _, p16 in chip_sums])
    mine = [_sum_shards(chip, p32, a, "sum_shards_" + n) for n, (p32, _), a in zip(_BIG, chip_sums, arrived)]
    others = _swap_halves(mine)
    out = {}
    for n, gm, go in zip(_BIG, mine, others):
        g, d, mo, vo = _adamw_shard(core, gm, go, w[n][0], mom[n][0], var[n][0], "adamw_" + n)
        out[n] = tuple(a[None] for a in (g, d, mo, vo))

    zeros_conv = jnp.zeros((CONV_WIDTH, D), F32)
    g_pack, d_pack, m_pack, v_pack = _small_reduce_adamw(
        _pack([small_grads[n] for n in _SMALL] + [small_grads["conv_w"]]),
        _pack([w[n] for n in _SMALL] + [zeros_conv]),
        _pack([mom[n] for n in _SMALL] + [zeros_conv]),
        _pack([var[n] for n in _SMALL] + [zeros_conv]))
    like = [w[n] for n in _SMALL] + [zeros_conv]
    unpacked = [_unpack(p, like) for p in (g_pack, d_pack, m_pack, v_pack)]
    for k, n in enumerate(_SMALL):
        out[n] = tuple(u[k] for u in unpacked)
    conv_g = lax.dynamic_slice_in_dim(unpacked[0][-1], chip[0] * conv_cols, conv_cols, axis=1)
    d, mo, vo = _adamw_whole(w["conv_w"][0], conv_g, mom["conv_w"][0], var["conv_w"][0], "adamw_conv_w")
    out["conv_w"] = tuple(a[None] for a in (conv_g, d, mo, vo))

    loss = lax.psum(loss[0, 0], ("x", "y", "c"))
    return (loss, grad_x[None], *[out[n][0] for n in _WEIGHTS], *[out[n][1] for n in _WEIGHTS],
            *[out[n][2] for n in _WEIGHTS], *[out[n][3] for n in _WEIGHTS])
```

```python
import functools

import jax
import jax.numpy as jnp
from jax import lax
from jax.experimental import pallas as pl
from jax.experimental.pallas import tpu as pltpu

F32 = jnp.float32
BF16 = jnp.bfloat16
S = jax.ShapeDtypeStruct

D = 1024
N_SHARD = 4
IN_COLS = 6 * D
IN_SHARD = IN_COLS // N_SHARD
D_FF = 2816
FF_SHARD = 2 * D_FF // N_SHARD
RG_BLOCK = 256
N_RG_BLOCK = D // RG_BLOCK
CHUNK = 128
N_GROUP = 8
CONV_WIDTH = 4
RG_C = 8.0
EPS = 1e-6
ADAM_LR, ADAM_B1, ADAM_B2, ADAM_EPS, ADAM_WD, ADAM_STEP = 0.001, 0.9, 0.999, 1e-08, 0.01, 10

V7X_VMEM_BYTES = 64 * 1024 * 1024
VMEM_LIMIT = V7X_VMEM_BYTES * 3 // 4
SUBLANES = 8
MESH = pl.DeviceIdType.MESH

TM_MM = 512
TM_SCAN = 256
TM_FF = 256


def _params(n_axes):
    return pltpu.CompilerParams(dimension_semantics=("arbitrary",) * n_axes, vmem_limit_bytes=VMEM_LIMIT)


def _resident(shape):
    nd = len(shape)
    return pl.BlockSpec(shape, lambda *_: (0,) * nd, pipeline_mode=pl.Buffered(1))


def _sig(x):
    return 1.0 / (1.0 + jnp.exp(-x))


_GELU_K = 0.7978845608028654
_GELU_C = 0.044715


def _gelu(x):
    return 0.5 * x * (1.0 + jnp.tanh(_GELU_K * (x + _GELU_C * x * x * x)))


def _gelu_and_grad(x):
    x2 = x * x
    th = jnp.tanh(_GELU_K * (x + _GELU_C * x2 * x))
    g = 0.5 * x * (1.0 + th)
    dg = 0.5 * (1.0 + th) + 0.5 * x * (1.0 - th * th) * (_GELU_K * (1.0 + 3.0 * _GELU_C * x2))
    return g, dg


def _one_minus_exp(x):
    p = 1.0 + x * (1.0 / 9.0)
    for k in (8.0, 7.0, 6.0, 5.0, 4.0, 3.0, 2.0):
        p = 1.0 + x * (1.0 / k) * p
    return jnp.where(x > -0.3, -x * p, 1.0 - jnp.exp(x))


def _softplus_neg(lam):
    z = -lam
    e = jnp.exp(-jnp.abs(z))
    u = 1.0 + e
    log1p = jnp.where(u == 1.0, e, jnp.log(u) * e / (u - 1.0))
    return jnp.maximum(z, 0.0) + log1p


def _rms_stats(x):
    return lax.rsqrt(jnp.mean(x * x, axis=-1, keepdims=True) + EPS)


def _rms_bwd(dy, x, g):
    rstd = _rms_stats(x)
    xhat = x * rstd
    dxhat = dy * g
    dx = rstd * (dxhat - xhat * jnp.mean(dxhat * xhat, axis=-1, keepdims=True))
    return dx, dy * xhat


def _colsum(x):
    return jnp.sum(x, axis=0, keepdims=True)


def _shift_down(x, d, fill):
    n = x.shape[0]
    if d % SUBLANES == 0:
        return jnp.concatenate([jnp.full((d, x.shape[1]), fill, x.dtype), x[:n - d]], axis=0)
    row = lax.broadcasted_iota(jnp.int32, x.shape, 0)
    return jnp.where(row < d, fill, pltpu.roll(x, d, 0))


def _shift_up(x, d, fill):
    n = x.shape[0]
    if d % SUBLANES == 0:
        return jnp.concatenate([x[d:], jnp.full((d, x.shape[1]), fill, x.dtype)], axis=0)
    row = lax.broadcasted_iota(jnp.int32, x.shape, 0)
    return jnp.where(row >= n - d, fill, pltpu.roll(x, n - d, 0))


def _scan(a, b, shift):
    d = 1
    while d < a.shape[0]:
        b = a * shift(b, d, 0.0) + b
        a = a * shift(a, d, 1.0)
        d *= 2
    return a, b


def _dot(a, b):
    return jnp.dot(a, b, preferred_element_type=F32)


def _dot_nt(a, b):
    return lax.dot_general(a, b, (((1,), (1,)), ((), ())), preferred_element_type=F32)


def _dot_tn(a, b):
    return lax.dot_general(a, b, (((0,), (0,)), ((), ())), preferred_element_type=F32)


_ANY = pl.BlockSpec(memory_space=pl.ANY)


def _position():
    return lax.axis_index("x"), lax.axis_index("y"), lax.axis_index("c")


def _other_chips(x, y):
    return [(1 - x, y), (x, 1 - y), (1 - x, 1 - y)]


class _Side:
    def __init__(self, inputs, out_shapes, n_sems, make):
        self.inputs, self.out_shapes, self.n_sems, self.make = list(inputs), list(out_shapes), n_sems, make


def _call(body, *, name, grid, in_specs, out_specs, out_shape, args, scratch_shapes=(), sides=()):
    n_in, n_out, n_scr = len(in_specs), len(out_specs), len(scratch_shapes)
    side_in = [len(s.inputs) for s in sides]
    side_out = [len(s.out_shapes) for s in sides]

    def wrapped(*refs):
        refs = list(refs)
        take = lambda k: [refs.pop(0) for _ in range(k)]
        ins = take(n_in)
        sins = [take(k) for k in side_in]
        outs = take(n_out)
        souts = [take(k) for k in side_out]
        scr = take(n_scr)
        sems = [take(3) for _ in sides]
        if sides:
            at_first = functools.reduce(jnp.logical_and, [pl.program_id(a) == 0 for a in range(len(grid))])
            at_last = functools.reduce(jnp.logical_and, [pl.program_id(a) == grid[a] - 1 for a in range(len(grid))])

            @pl.when(at_first)
            def _():
                for s, si, so, sem in zip(sides, sins, souts, sems):
                    for start in s.make(si, so, *sem)[0]:
                        start()

        body(*ins, *outs, *scr)
        if sides:
            @pl.when(at_last)
            def _():
                for s, si, so, sem in zip(sides, sins, souts, sems):
                    for wait in s.make(si, so, *sem)[1]:
                        wait()

    res = pl.pallas_call(
        wrapped, name=name, grid=grid,
        in_specs=list(in_specs) + [_ANY] * sum(side_in),
        out_specs=list(out_specs) + [_ANY] * sum(side_out),
        out_shape=list(out_shape) + [o for s in sides for o in s.out_shapes],
        scratch_shapes=list(scratch_shapes) + [pltpu.SemaphoreType.DMA((s.n_sems,)) for s in sides for _ in range(3)],
        compiler_params=_params(len(grid)),
    )(*args, *[a for s in sides for a in s.inputs])
    main, rest, per_side = list(res[:n_out]), list(res[n_out:]), []
    for k in side_out:
        per_side.append(rest[:k])
        rest = rest[k:]
    return main, per_side


def _comm_only(sides, name):
    def body():
        pass

    return _call(body, name=name, grid=(1,), in_specs=[], out_specs=[], out_shape=[], args=[], sides=sides)[1]


def _remote(src, dst, send, recv, k, device):
    return pltpu.make_async_remote_copy(src_ref=src, dst_ref=dst, send_sem=send.at[k], recv_sem=recv.at[k],
                                        device_id=device, device_id_type=MESH)


def _both_ways(copy, keys):
    return [lambda k=k: copy(k).start() for k in keys], [lambda k=k: copy(k).wait() for k in keys]


def _gather_side(shards):
    n = len(shards)

    def make(ins, outs, send, recv, local):
        x, y, c = _position()
        mine = 2 * x + y
        chips = _other_chips(x, y)
        pairs = [(w, j) for w in range(n) for j in range(3)]

        def own(w):
            return pltpu.make_async_copy(ins[w], outs[w].at[mine], local.at[w])

        def push(w, j):
            return _remote(ins[w], outs[w].at[mine], send, recv, 3 * w + j, (*chips[j], c))

        def arrival(w, j):
            px, py = chips[j]
            return _remote(ins[w], outs[w].at[2 * px + py], send, recv, 3 * w + j, (px, py, c))

        starts = [lambda w=w: own(w).start() for w in range(n)] + [lambda w=w, j=j: push(w, j).start() for w, j in pairs]
        waits = ([lambda w=w, j=j: arrival(w, j).wait_recv() for w, j in pairs]
                 + [lambda w=w, j=j: push(w, j).wait_send() for w, j in pairs]
                 + [lambda w=w: own(w).wait() for w in range(n)])
        return starts, waits

    return _Side(shards, [S((N_SHARD,) + s.shape, s.dtype) for s in shards], 3 * n, make)


def _halves_side(grads):
    n = len(grads)

    def make(ins, outs, send, recv, local):
        x, y, c = _position()

        def copy(w):
            half = ins[w].shape[1] // 2
            return _remote(ins[w].at[:, pl.ds((1 - c) * half, half), :], outs[w], send, recv, w, (x, y, 1 - c))

        return _both_ways(copy, range(n))

    return _Side(grads, [S((N_SHARD, g.shape[1] // 2, g.shape[2]), F32) for g in grads], n, make)


def _scatter_side(partials):
    n = len(partials)

    def make(ins, outs, send, recv, local):
        x, y, c = _position()
        chips = _other_chips(x, y)

        def copy(k):
            w, j = divmod(k, 3)
            px, py = chips[j]
            return _remote(ins[w].at[2 * px + py], outs[w].at[j], send, recv, k, (px, py, c))

        return _both_ways(copy, range(3 * n))

    return _Side(partials, [S((3,) + p.shape[1:], p.dtype) for p in partials], 3 * n, make)


def _swap_side(halves):
    n = len(halves)

    def make(ins, outs, send, recv, local):
        x, y, c = _position()
        return _both_ways(lambda w: _remote(ins[w], outs[w], send, recv, w, (x, y, 1 - c)), range(n))

    return _Side(halves, [S(h.shape, h.dtype) for h in halves], n, make)


N_DEVICE = 8


def _everyone_side(packed):
    def make(ins, outs, send, recv, local):
        x, y, c = _position()
        mine = 4 * x + 2 * y + c
        others = range(1, N_DEVICE)

        def peer(k):
            return (1 - x if k & 4 else x, 1 - y if k & 2 else y, 1 - c if k & 1 else c)

        def own():
            return pltpu.make_async_copy(ins[0], outs[0].at[mine], local.at[0])

        def push(k):
            return _remote(ins[0], outs[0].at[mine], send, recv, k - 1, peer(k))

        def arrival(k):
            px, py, pc = peer(k)
            return _remote(ins[0], outs[0].at[4 * px + 2 * py + pc], send, recv, k - 1, (px, py, pc))

        starts = [lambda: own().start()] + [lambda k=k: push(k).start() for k in others]
        waits = ([lambda k=k: arrival(k).wait_recv() for k in others] + [lambda k=k: push(k).wait_send() for k in others]
                 + [lambda: own().wait()])
        return starts, waits

    return _Side([packed], [S((N_DEVICE,) + packed.shape, packed.dtype)], N_DEVICE - 1, make)


def _inproj(x, g, w_in, sides=()):
    T = x.shape[0]
    tm = min(TM_FF, T)

    def body(x_ref, g_ref, w_ref, proj_ref, h_ref):
        xv = x_ref[...]
        h = (xv * _rms_stats(xv) * g_ref[...]).astype(BF16)
        h_ref[...] = h
        for k in range(N_SHARD):
            proj_ref[:, IN_SHARD * k:IN_SHARD * (k + 1)] = _dot(h, w_ref[k])

    return _call(
        body, name="inproj", grid=(T // tm,),
        in_specs=[pl.BlockSpec((tm, D), lambda i: (i, 0)), pl.BlockSpec((1, D), lambda i: (0, 0)),
                  _resident((N_SHARD, D, IN_SHARD))],
        out_specs=[pl.BlockSpec((tm, IN_COLS), lambda i: (i, 0)), pl.BlockSpec((tm, D), lambda i: (i, 0))],
        out_shape=[S((T, IN_COLS), F32), S((T, D), BF16)],
        args=(x, g, w_in), sides=sides)


def _rg_gates(xc, wa_ref, wx_ref, ba, bx, sp):
    xb = xc.astype(BF16)
    blocks = [xb[:, RG_BLOCK * j:RG_BLOCK * (j + 1)] for j in range(N_RG_BLOCK)]
    r = _sig(jnp.concatenate([_dot(blocks[j], wa_ref[j]) for j in range(N_RG_BLOCK)], axis=1) + ba)
    gi = _sig(jnp.concatenate([_dot(blocks[j], wx_ref[j]) for j in range(N_RG_BLOCK)], axis=1) + bx)
    log_a = (-RG_C) * r * sp
    a = jnp.exp(log_a)
    m = jnp.sqrt(_one_minus_exp(2.0 * log_a))
    return xb, r, gi, a, m


def _rglru_fwd(proj, conv_w, conv_b, wa, wx, ba, bx, lam, sides=()):
    T = proj.shape[0]
    tm = min(TM_SCAN, T)

    def body(rx_ref, gate_ref, cw_ref, cb_ref, wa_ref, wx_ref, ba_ref, bx_ref, lam_ref,
             ya_ref, xc_ref, h_ref, ext, hc):
        @pl.when(pl.program_id(0) == 0)
        def _():
            ext[0:SUBLANES, :] = jnp.zeros((SUBLANES, D), F32)
            hc[...] = jnp.zeros((SUBLANES, D), F32)

        ext[SUBLANES:SUBLANES + tm, :] = rx_ref[...]
        xc = cb_ref[...]
        for k in range(CONV_WIDTH):
            xc = xc + ext[pl.ds(SUBLANES - (CONV_WIDTH - 1) + k, tm), :] * cw_ref[k:k + 1, :]
        ext[0:SUBLANES, :] = ext[tm:tm + SUBLANES, :]
        xc_ref[...] = xc
        _, _, gi, a, m = _rg_gates(xc, wa_ref, wx_ref, ba_ref[...], bx_ref[...], _softplus_neg(lam_ref[...]))
        A, B = _scan(a, m * (gi * xc), _shift_down)
        h = A * hc[0:1, :] + B
        hc[...] = jnp.broadcast_to(h[tm - 1:tm, :], (SUBLANES, D))
        h_ref[...] = h
        ya_ref[...] = (_gelu(gate_ref[...]) * h).astype(BF16)

    vec = pl.BlockSpec((1, D), lambda i: (0, 0))
    blk = pl.BlockSpec((N_RG_BLOCK, RG_BLOCK, RG_BLOCK), lambda i: (0, 0, 0))
    tile = pl.BlockSpec((tm, D), lambda i: (i, 0))
    return _call(
        body, name="rglru_fwd", grid=(T // tm,),
        in_specs=[pl.BlockSpec((tm, D), lambda i: (i, 0)), pl.BlockSpec((tm, D), lambda i: (i, 1)),
                  pl.BlockSpec((CONV_WIDTH, D), lambda i: (0, 0)), vec, blk, blk, vec, vec, vec],
        out_specs=[tile, tile, tile],
        out_shape=[S((T, D), BF16), S((T, D), F32), S((T, D), F32)],
        scratch_shapes=[pltpu.VMEM((tm + SUBLANES, D), F32), pltpu.VMEM((SUBLANES, D), F32)],
        args=(proj, proj, conv_w, conv_b, wa, wx, ba, bx, lam), sides=sides)


def _layer_norm_stats(v):
    mu = jnp.mean(v, axis=-1, keepdims=True)
    vc = v - mu
    rstd = lax.rsqrt(jnp.mean(vc * vc, axis=-1, keepdims=True) + EPS)
    return vc * rstd, rstd


def _sgu_mix(w_ref, vnb, bst_ref, n_chunk):
    cols = []
    for g in range(N_GROUP):
        vg = vnb[:, CHUNK * g:CHUNK * (g + 1)].reshape(n_chunk, CHUNK, CHUNK)
        wb = jnp.broadcast_to(w_ref[g][None], (n_chunk, CHUNK, CHUNK))
        mg = lax.dot_general(wb, vg, (((2,), (1,)), ((0,), (0,))), preferred_element_type=F32)
        mg = mg + bst_ref[:, g:g + 1][None]
        cols.append(mg.reshape(n_chunk * CHUNK, CHUNK))
    return jnp.concatenate(cols, axis=1)


def _sgu_fwd(proj, ln_g, ln_b, w_tril, bs_t):
    T = proj.shape[0]
    tm = min(TM_MM, T)
    n_chunk = tm // CHUNK

    def body(u_ref, v_ref, g_ref, b_ref, w_ref, bst_ref, yb_ref):
        vhat, _ = _layer_norm_stats(_gelu(v_ref[...]))
        vnb = (vhat * g_ref[...] + b_ref[...]).astype(BF16)
        yb_ref[...] = (_gelu(u_ref[...]) * _sgu_mix(w_ref, vnb, bst_ref, n_chunk)).astype(BF16)

    vec = pl.BlockSpec((1, D), lambda i: (0, 0))
    return pl.pallas_call(
        body, name="sgu_fwd", grid=(T // tm,),
        in_specs=[pl.BlockSpec((tm, D), lambda i: (i, 2)), pl.BlockSpec((tm, D), lambda i: (i, 3)), vec, vec,
                  pl.BlockSpec((N_GROUP, CHUNK, CHUNK), lambda i: (0, 0, 0)),
                  pl.BlockSpec((CHUNK, N_GROUP), lambda i: (0, 0))],
        out_specs=pl.BlockSpec((tm, D), lambda i: (i, 0)),
        out_shape=S((T, D), BF16),
        compiler_params=_params(1),
    )(proj, proj, ln_g, ln_b, w_tril, bs_t)


def _merge_fwd(x, proj, ya, yb, wpa, wpb, wout):
    T = x.shape[0]
    tm = min(TM_MM, T)

    def body(x_ref, ga_ref, gb_ref, ya_ref, yb_ref, wpa_ref, wpb_ref, wout_ref, pa_ref, pb_ref, mb_ref, x1_ref):
        pa = _dot(ya_ref[...], wpa_ref[...])
        pb = _dot(yb_ref[...], wpb_ref[...])
        pa_ref[...] = pa
        pb_ref[...] = pb
        mb = (_sig(ga_ref[...]) * pa + _sig(gb_ref[...]) * pb).astype(BF16)
        mb_ref[...] = mb
        x1_ref[...] = x_ref[...] + _dot(mb, wout_ref[...])

    tile = pl.BlockSpec((tm, D), lambda i: (i, 0))
    w = _resident((D, D))
    return pl.pallas_call(
        body, name="merge_fwd", grid=(T // tm,),
        in_specs=[tile, pl.BlockSpec((tm, D), lambda i: (i, 4)), pl.BlockSpec((tm, D), lambda i: (i, 5)),
                  tile, tile, w, w, w],
        out_specs=[tile, tile, tile, tile],
        out_shape=[S((T, D), F32), S((T, D), F32), S((T, D), BF16), S((T, D), F32)],
        compiler_params=_params(1),
    )(x, proj, proj, ya, yb, wpa, wpb, wout)


def _ffn_up(x1, g, w_gu):
    T = x1.shape[0]
    tm = min(TM_FF, T)

    def body(x_ref, g_ref, w_ref, h2_ref, gu_ref, act_ref):
        xv = x_ref[...]
        h2 = (xv * _rms_stats(xv) * g_ref[...]).astype(BF16)
        h2_ref[...] = h2
        for k in range(N_SHARD):
            gu_ref[:, FF_SHARD * k:FF_SHARD * (k + 1)] = _dot(h2, w_ref[k])
        gate = gu_ref[:, 0:D_FF]
        act_ref[...] = (gate * _sig(gate) * gu_ref[:, D_FF:2 * D_FF]).astype(BF16)

    return pl.pallas_call(
        body, name="ffn_up", grid=(T // tm,),
        in_specs=[pl.BlockSpec((tm, D), lambda i: (i, 0)), pl.BlockSpec((1, D), lambda i: (0, 0)),
                  _resident((N_SHARD, D, FF_SHARD))],
        out_specs=[pl.BlockSpec((tm, D), lambda i: (i, 0)), pl.BlockSpec((tm, 2 * D_FF), lambda i: (i, 0)),
                   pl.BlockSpec((tm, D_FF), lambda i: (i, 0))],
        out_shape=[S((T, D), BF16), S((T, 2 * D_FF), F32), S((T, D_FF), BF16)],
        compiler_params=_params(1),
    )(x1, g, w_gu)


def _ffn_down_loss(x1, act, w_down, g_final, target):
    T = x1.shape[0]
    tm = min(TM_MM, T)

    def body(x_ref, act_ref, w_ref, g_ref, t_ref, dx2_ref, dx2b_ref, loss_ref, dg_ref):
        @pl.when(pl.program_id(0) == 0)
        def _():
            loss_ref[...] = jnp.zeros_like(loss_ref)
            dg_ref[...] = jnp.zeros_like(dg_ref)

        x2 = x_ref[...] + _dot(act_ref[...], w_ref[...])
        gf = g_ref[...]
        err = x2 * _rms_stats(x2) * gf - t_ref[...]
        loss_ref[...] += 0.5 * jnp.sum(jnp.mean(err * err, axis=-1, keepdims=True), axis=0, keepdims=True)
        dx2, dg_rows = _rms_bwd(err * (1.0 / D), x2, gf)
        dg_ref[...] += _colsum(dg_rows)
        dx2_ref[...] = dx2
        dx2b_ref[...] = dx2.astype(BF16)

    tile = pl.BlockSpec((tm, D), lambda i: (i, 0))
    vec = pl.BlockSpec((1, D), lambda i: (0, 0))
    return pl.pallas_call(
        body, name="ffn_down_loss", grid=(T // tm,),
        in_specs=[tile, pl.BlockSpec((tm, D_FF), lambda i: (i, 0)), _resident((D_FF, D)), vec, tile],
        out_specs=[tile, tile, pl.BlockSpec((1, 1), lambda i: (0, 0)), vec],
        out_shape=[S((T, D), F32), S((T, D), BF16), S((1, 1), F32), S((1, D), F32)],
        compiler_params=_params(1),
    )(x1, act, w_down, g_final, target)


def _ffn_bwd_act(dx2b, gu, w_down):
    T = dx2b.shape[0]
    tm = min(TM_FF, T)

    def body(dx_ref, gu_ref, w_ref, dgu_ref):
        dact = _dot_nt(dx_ref[...], w_ref[...])
        gate = gu_ref[:, 0:D_FF]
        sg = _sig(gate)
        dgu_ref[:, 0:D_FF] = (dact * gu_ref[:, D_FF:2 * D_FF] * (sg * (1.0 + gate * (1.0 - sg)))).astype(BF16)
        dgu_ref[:, D_FF:2 * D_FF] = (dact * (gate * sg)).astype(BF16)

    return pl.pallas_call(
        body, name="ffn_bwd_act", grid=(T // tm,),
        in_specs=[pl.BlockSpec((tm, D), lambda i: (i, 0)), pl.BlockSpec((tm, 2 * D_FF), lambda i: (i, 0)),
                  _resident((D_FF, D))],
        out_specs=pl.BlockSpec((tm, 2 * D_FF), lambda i: (i, 0)),
        out_shape=S((T, 2 * D_FF), BF16),
        compiler_params=_params(1),
    )(dx2b, gu, w_down)


def _ffn_bwd_in(dgu, w_gu, x1, g, dx2):
    T = x1.shape[0]
    tm = min(TM_MM, T)

    def body(dgu_ref, w_ref, x_ref, g_ref, dx2_ref, dx1_ref, dx1b_ref, dg_ref):
        @pl.when(pl.program_id(0) == 0)
        def _():
            dg_ref[...] = jnp.zeros_like(dg_ref)

        dh2 = _dot_nt(dgu_ref[:, 0:FF_SHARD], w_ref[0])
        for k in range(1, N_SHARD):
            dh2 = dh2 + _dot_nt(dgu_ref[:, FF_SHARD * k:FF_SHARD * (k + 1)], w_ref[k])
        dx, dg_rows = _rms_bwd(dh2, x_ref[...], g_ref[...])
        dg_ref[...] += _colsum(dg_rows)
        dx1 = dx2_ref[...] + dx
        dx1_ref[...] = dx1
        dx1b_ref[...] = dx1.astype(BF16)

    tile = pl.BlockSpec((tm, D), lambda i: (i, 0))
    vec = pl.BlockSpec((1, D), lambda i: (0, 0))
    return pl.pallas_call(
        body, name="ffn_bwd_in", grid=(T // tm,),
        in_specs=[pl.BlockSpec((tm, 2 * D_FF), lambda i: (i, 0)), _resident((N_SHARD, D, FF_SHARD)), tile, vec, tile],
        out_specs=[tile, tile, vec],
        out_shape=[S((T, D), F32), S((T, D), BF16), S((1, D), F32)],
        compiler_params=_params(1),
    )(dgu, w_gu, x1, g, dx2)


def _matmul_tn(a, b, tn, shard_major, name):
    T, M = a.shape
    N = b.shape[1]
    tk = min(TM_MM, T)

    def body(a_ref, b_ref, o_ref):
        @pl.when(pl.program_id(1) == 0)
        def _():
            o_ref[...] = jnp.zeros_like(o_ref)

        acc = _dot_tn(a_ref[...], b_ref[...])
        if shard_major:
            o_ref[0] += acc
        else:
            o_ref[...] += acc

    if shard_major:
        out_spec, out_shape = pl.BlockSpec((1, M, tn), lambda j, k: (j, 0, 0)), S((N // tn, M, tn), F32)
    else:
        out_spec, out_shape = pl.BlockSpec((M, tn), lambda j, k: (0, j)), S((M, N), F32)
    return pl.pallas_call(
        body, name=name, grid=(N // tn, T // tk),
        in_specs=[pl.BlockSpec((tk, M), lambda j, k: (k, 0)), pl.BlockSpec((tk, tn), lambda j, k: (k, j))],
        out_specs=out_spec, out_shape=out_shape,
        compiler_params=_params(2),
    )(a, b)


def _merge_bwd(dx1b, proj, pa, pb, wpa, wpb, wout, sides=()):
    T = dx1b.shape[0]
    tm = min(TM_MM, T)

    def body(dx_ref, ga_ref, gb_ref, pa_ref, pb_ref, wpa_ref, wpb_ref, wout_ref,
             dpa_ref, dpb_ref, dgate_ref, dya_ref, dyb_ref):
        dm = _dot_nt(dx_ref[...], wout_ref[...])
        sa = _sig(ga_ref[...])
        sb = _sig(gb_ref[...])
        dpa = (dm * sa).astype(BF16)
        dpb = (dm * sb).astype(BF16)
        dpa_ref[...] = dpa
        dpb_ref[...] = dpb
        dgate_ref[:, 0:D] = (dm * pa_ref[...] * (sa * (1.0 - sa))).astype(BF16)
        dgate_ref[:, D:2 * D] = (dm * pb_ref[...] * (sb * (1.0 - sb))).astype(BF16)
        dya_ref[...] = _dot_nt(dpa, wpa_ref[...])
        dyb_ref[...] = _dot_nt(dpb, wpb_ref[...])

    tile = pl.BlockSpec((tm, D), lambda i: (i, 0))
    w = _resident((D, D))
    return _call(
        body, name="merge_bwd", grid=(T // tm,),
        in_specs=[tile, pl.BlockSpec((tm, D), lambda i: (i, 4)), pl.BlockSpec((tm, D), lambda i: (i, 5)),
                  tile, tile, w, w, w],
        out_specs=[tile, tile, pl.BlockSpec((tm, 2 * D), lambda i: (i, 0)), tile, tile],
        out_shape=[S((T, D), BF16), S((T, D), BF16), S((T, 2 * D), BF16), S((T, D), F32), S((T, D), F32)],
        args=(dx1b, proj, proj, pa, pb, wpa, wpb, wout), sides=sides)


def _rglru_bwd(dya, proj, hseq, xc, conv_w, wa, wx, ba, bx, lam, sides=()):
    T = dya.shape[0]
    tm = min(TM_SCAN, T)
    n = T // tm
    per8 = tm // SUBLANES

    def body(dya_ref, rx_ref, rxp_ref, gate_ref, h_ref, hp_ref, xc_ref, cw_ref, wa_ref, wx_ref, ba_ref, bx_ref, lam_ref,
             dab_ref, dcw_ref, dcb_ref, dwa_ref, dwx_ref, dba_ref, dbx_ref, dlam_ref,
             hext, rext, dext, carry_a, carry_dh):
        i = pl.program_id(0)
        first_tile = i == n - 1

        @pl.when(i == 0)
        def _():
            for ref in (dcw_ref, dcb_ref, dwa_ref, dwx_ref, dba_ref, dbx_ref, dlam_ref, carry_a, carry_dh):
                ref[...] = jnp.zeros_like(ref)
            dext[tm:tm + SUBLANES, :] = jnp.zeros((SUBLANES, D), F32)

        gel, dgel = _gelu_and_grad(gate_ref[...])
        dya_v = dya_ref[...]
        hseq_v = h_ref[...]
        dgate = dya_v * hseq_v * dgel
        xcv = xc_ref[...]
        lam_v = lam_ref[...]
        sp = _softplus_neg(lam_v)
        xb, r, gi, a, m = _rg_gates(xcv, wa_ref, wx_ref, ba_ref[...], bx_ref[...], sp)

        row = lax.broadcasted_iota(jnp.int32, (tm, D), 0)
        c = jnp.where(row == tm - 1, carry_a[0:1, :], _shift_up(a, 1, 0.0))
        C, G = _scan(c, dya_v * gel, _shift_up)
        dH = G + C * carry_dh[0:1, :]
        carry_a[...] = jnp.broadcast_to(a[0:1, :], (SUBLANES, D))
        carry_dh[...] = jnp.broadcast_to(dH[0:1, :], (SUBLANES, D))

        hext[0:SUBLANES, :] = jnp.where(first_tile, 0.0, hp_ref[...])
        hext[SUBLANES:SUBLANES + tm, :] = hseq_v
        h_prev = hext[pl.ds(SUBLANES - 1, tm), :]

        d_m = dH * (gi * xcv)
        d_la = dH * h_prev * a - d_m * (a * a) / m
        d_ia = dH * m * xcv * (gi * (1.0 - gi))
        d_ra = d_la * ((-RG_C) * sp) * (r * (1.0 - r))
        dlam_ref[...] += _colsum(d_la * ((-RG_C) * r)) * (-_sig(-lam_v))
        dba_ref[...] += _colsum(d_ra)
        dbx_ref[...] += _colsum(d_ia)
        drab = d_ra.astype(BF16)
        diab = d_ia.astype(BF16)
        dxc_cols = []
        for j in range(N_RG_BLOCK):
            sl = slice(RG_BLOCK * j, RG_BLOCK * (j + 1))
            dxc_cols.append(_dot_nt(drab[:, sl], wa_ref[j]) + _dot_nt(diab[:, sl], wx_ref[j]))
            dwa_ref[j] += _dot_tn(xb[:, sl], drab[:, sl])
            dwx_ref[j] += _dot_tn(xb[:, sl], diab[:, sl])
        dxc = dH * m * gi + jnp.concatenate(dxc_cols, axis=1)

        dcb_ref[...] += _colsum(dxc)
        dext[0:tm, :] = dxc
        rext[0:SUBLANES, :] = jnp.where(first_tile, 0.0, rxp_ref[...])
        rext[SUBLANES:SUBLANES + tm, :] = rx_ref[...]
        drx = jnp.zeros((tm, D), F32)
        for k in range(CONV_WIDTH):
            drx = drx + dext[pl.ds(CONV_WIDTH - 1 - k, tm), :] * cw_ref[k:k + 1, :]
            dcw_ref[k:k + 1, :] += _colsum(dxc * rext[pl.ds(SUBLANES - (CONV_WIDTH - 1) + k, tm), :])
        dext[tm:tm + SUBLANES, :] = dext[0:SUBLANES, :]
        dab_ref[:, 0:D] = drx.astype(BF16)
        dab_ref[:, D:2 * D] = dgate.astype(BF16)

    def rev(col):
        return lambda i: (n - 1 - i, col)

    def prev8(col):
        return lambda i: (jnp.maximum((n - 1 - i) * per8 - 1, 0), col)

    tile = pl.BlockSpec((tm, D), rev(0))
    vec = pl.BlockSpec((1, D), lambda i: (0, 0))
    blk = pl.BlockSpec((N_RG_BLOCK, RG_BLOCK, RG_BLOCK), lambda i: (0, 0, 0))
    cw = pl.BlockSpec((CONV_WIDTH, D), lambda i: (0, 0))
    return _call(
        body, name="rglru_bwd", grid=(n,),
        in_specs=[tile, pl.BlockSpec((tm, D), rev(0)), pl.BlockSpec((SUBLANES, D), prev8(0)),
                  pl.BlockSpec((tm, D), rev(1)), tile, pl.BlockSpec((SUBLANES, D), prev8(0)), tile,
                  cw, blk, blk, vec, vec, vec],
        out_specs=[pl.BlockSpec((tm, 2 * D), rev(0)), cw, vec, blk, blk, vec, vec, vec],
        out_shape=[S((T, 2 * D), BF16), S((CONV_WIDTH, D), F32), S((1, D), F32),
                   S((N_RG_BLOCK, RG_BLOCK, RG_BLOCK), F32), S((N_RG_BLOCK, RG_BLOCK, RG_BLOCK), F32),
                   S((1, D), F32), S((1, D), F32), S((1, D), F32)],
        scratch_shapes=[pltpu.VMEM((tm + SUBLANES, D), F32), pltpu.VMEM((tm + SUBLANES, D), F32),
                        pltpu.VMEM((tm + SUBLANES, D), F32), pltpu.VMEM((SUBLANES, D), F32),
                        pltpu.VMEM((SUBLANES, D), F32)],
        args=(dya, proj, proj, proj, hseq, hseq, xc, conv_w, wa, wx, ba, bx, lam), sides=sides)


def _sgu_bwd(dyb, proj, ln_g, ln_b, w_tril, w_tril_t, bs_t, sides=()):
    T = dyb.shape[0]
    tm = min(TM_MM, T)
    n_chunk = tm // CHUNK

    def body(dyb_ref, u_ref, v_ref, g_ref, b_ref, w_ref, wt_ref, bst_ref,
             duv_ref, dw_ref, dbst_ref, dg_ref, db_ref):
        @pl.when(pl.program_id(0) == 0)
        def _():
            for ref in (dw_ref, dbst_ref, dg_ref, db_ref):
                ref[...] = jnp.zeros_like(ref)

        gu, dgu = _gelu_and_grad(u_ref[...])
        gv, dgv = _gelu_and_grad(v_ref[...])
        vhat, rstd = _layer_norm_stats(gv)
        lng = g_ref[...]
        vnb = (vhat * lng + b_ref[...]).astype(BF16)
        mixed = _sgu_mix(w_ref, vnb, bst_ref, n_chunk)
        dyb_v = dyb_ref[...]
        duv_ref[:, 0:D] = (dyb_v * mixed * dgu).astype(BF16)
        dmix = dyb_v * gu
        dmb = dmix.astype(BF16)
        keep = (lax.broadcasted_iota(jnp.int32, (CHUNK, CHUNK), 0)
                >= lax.broadcasted_iota(jnp.int32, (CHUNK, CHUNK), 1))
        dvn_cols, dbs_cols = [], []
        for g in range(N_GROUP):
            sl = slice(CHUNK * g, CHUNK * (g + 1))
            dmg = dmb[:, sl].reshape(n_chunk, CHUNK, CHUNK)
            vg = vnb[:, sl].reshape(n_chunk, CHUNK, CHUNK)
            wtb = jnp.broadcast_to(wt_ref[g][None], (n_chunk, CHUNK, CHUNK))
            dvn = lax.dot_general(wtb, dmg, (((2,), (1,)), ((0,), (0,))), preferred_element_type=F32)
            dvn_cols.append(dvn.reshape(tm, CHUNK))
            dw = lax.dot_general(dmg, vg, (((2,), (2,)), ((0,), (0,))), preferred_element_type=F32)
            dw_ref[g] += jnp.where(keep, jnp.sum(dw, axis=0), 0.0)
            rows = jnp.sum(dmix[:, sl], axis=1, keepdims=True)
            dbs_cols.append(jnp.sum(rows.reshape(n_chunk, CHUNK, 1), axis=0))
        dbst_ref[...] += jnp.concatenate(dbs_cols, axis=1)
        dvn = jnp.concatenate(dvn_cols, axis=1)
        dg_ref[...] += _colsum(dvn * vhat)
        db_ref[...] += _colsum(dvn)
        dvhat = dvn * lng
        dgv_in = rstd * (dvhat - jnp.mean(dvhat, axis=-1, keepdims=True)
                         - vhat * jnp.mean(dvhat * vhat, axis=-1, keepdims=True))
        duv_ref[:, D:2 * D] = (dgv_in * dgv).astype(BF16)

    tile = pl.BlockSpec((tm, D), lambda i: (i, 0))
    vec = pl.BlockSpec((1, D), lambda i: (0, 0))
    wsp = pl.BlockSpec((N_GROUP, CHUNK, CHUNK), lambda i: (0, 0, 0))
    bsp = pl.BlockSpec((CHUNK, N_GROUP), lambda i: (0, 0))
    return _call(
        body, name="sgu_bwd", grid=(T // tm,),
        in_specs=[tile, pl.BlockSpec((tm, D), lambda i: (i, 2)), pl.BlockSpec((tm, D), lambda i: (i, 3)),
                  vec, vec, wsp, wsp, bsp],
        out_specs=[pl.BlockSpec((tm, 2 * D), lambda i: (i, 0)), wsp, bsp, vec, vec],
        out_shape=[S((T, 2 * D), BF16), S((N_GROUP, CHUNK, CHUNK), F32), S((CHUNK, N_GROUP), F32),
                   S((1, D), F32), S((1, D), F32)],
        args=(dyb, proj, proj, ln_g, ln_b, w_tril, w_tril_t, bs_t), sides=sides)


def _dproj_columns(k, da, db, dg):
    if k == 0:
        return da[:, 0:IN_SHARD]
    if k == 1:
        return jnp.concatenate([da[:, IN_SHARD:2 * D], db[:, 0:D]], axis=1)
    if k == 2:
        return jnp.concatenate([db[:, D:2 * D], dg[:, 0:D // 2]], axis=1)
    return dg[:, D // 2:2 * D]


def _inproj_bwd(da, db, dg, w_in, x, g, dx1, sides=()):
    T = x.shape[0]
    tm = min(TM_MM, T)

    def body(da_ref, db_ref, dg_ref, w_ref, x_ref, g_ref, dx1_ref, dx_ref, dgm_ref):
        @pl.when(pl.program_id(0) == 0)
        def _():
            dgm_ref[...] = jnp.zeros_like(dgm_ref)

        dav, dbv, dgv = da_ref[...], db_ref[...], dg_ref[...]
        dh = _dot_nt(_dproj_columns(0, dav, dbv, dgv), w_ref[0])
        for k in range(1, N_SHARD):
            dh = dh + _dot_nt(_dproj_columns(k, dav, dbv, dgv), w_ref[k])
        dx, dg_rows = _rms_bwd(dh, x_ref[...], g_ref[...])
        dgm_ref[...] += _colsum(dg_rows)
        dx_ref[...] = dx1_ref[...] + dx

    tile = pl.BlockSpec((tm, D), lambda i: (i, 0))
    wide = pl.BlockSpec((tm, 2 * D), lambda i: (i, 0))
    vec = pl.BlockSpec((1, D), lambda i: (0, 0))
    return _call(
        body, name="inproj_bwd", grid=(T // tm,),
        in_specs=[wide, wide, wide, _resident((N_SHARD, D, IN_SHARD)), tile, vec, tile],
        out_specs=[tile, vec],
        out_shape=[S((T, D), F32), S((1, D), F32)],
        args=(da, db, dg, w_in, x, g, dx1), sides=sides)


def _inproj_wgrad(h, da, db, dg, sides=()):
    T = h.shape[0]
    tk = min(TM_MM, T)

    def body(h_ref, da_ref, db_ref, dg_ref, o_ref):
        @pl.when(pl.program_id(1) == 0)
        def _():
            o_ref[...] = jnp.zeros_like(o_ref)

        for k in range(N_SHARD):
            @pl.when(pl.program_id(0) == k)
            def _(k=k):
                o_ref[0] += _dot_tn(h_ref[...], _dproj_columns(k, da_ref[...], db_ref[...], dg_ref[...]))

    wide = pl.BlockSpec((tk, 2 * D), lambda j, k: (k, 0))
    return _call(
        body, name="inproj_wgrad", grid=(N_SHARD, T // tk),
        in_specs=[pl.BlockSpec((tk, D), lambda j, k: (k, 0)), wide, wide, wide],
        out_specs=[pl.BlockSpec((1, D, IN_SHARD), lambda j, k: (j, 0, 0))],
        out_shape=[S((N_SHARD, D, IN_SHARD), F32)],
        args=(h, da, db, dg), sides=sides)


def _blockdiag(w):
    hd = w.shape[-1]
    per = RG_BLOCK // hd
    w4 = w.reshape(N_RG_BLOCK, per, hd, hd)
    out = jnp.zeros((N_RG_BLOCK, per, hd, per, hd), w.dtype)
    for h in range(per):
        out = out.at[:, h, :, h, :].set(w4[:, h])
    return out.reshape(N_RG_BLOCK, RG_BLOCK, RG_BLOCK)


def _blockdiag_heads(g, hd):
    per = RG_BLOCK // hd
    g5 = g.reshape(N_RG_BLOCK, per, hd, per, hd)
    return jnp.stack([g5[:, h, :, h, :] for h in range(per)], axis=1).reshape(N_RG_BLOCK * per, hd, hd)


def _row_tile(rows):
    for t in range(256, 0, -SUBLANES):
        if rows % t == 0:
            return t
    raise ValueError(rows)


def _add_halves(core, g, theirs, name):
    _, r, cols = g.shape
    half = r // 2
    tr = _row_tile(half)
    nb = half // tr

    def body(core_ref, g_ref, t_ref, o32_ref, o16_ref):
        s = g_ref[...] + t_ref[...]
        o32_ref[...] = s
        o16_ref[...] = s.astype(BF16)

    blk = pl.BlockSpec((1, tr, cols), lambda s, i, core_ref: (s, i, 0))
    gs = pltpu.PrefetchScalarGridSpec(
        num_scalar_prefetch=1, grid=(N_SHARD, nb),
        in_specs=[pl.BlockSpec((1, tr, cols), lambda s, i, core_ref: (s, core_ref[0] * nb + i, 0)), blk],
        out_specs=[blk, blk])
    return pl.pallas_call(
        body, name=name, grid_spec=gs,
        out_shape=[S((N_SHARD, half, cols), F32), S((N_SHARD, half, cols), BF16)],
        compiler_params=_params(2),
    )(core, g, theirs)


def _sum_shards(chip, own, others, name):
    _, half, cols = own.shape
    tr = _row_tile(half)

    def body(chip_ref, own_ref, oth_ref, o_ref):
        acc = own_ref[0]
        for j in range(3):
            acc = acc + oth_ref[j].astype(F32)
        o_ref[...] = acc

    gs = pltpu.PrefetchScalarGridSpec(
        num_scalar_prefetch=1, grid=(half // tr,),
        in_specs=[pl.BlockSpec((1, tr, cols), lambda i, chip_ref: (chip_ref[0], i, 0)),
                  pl.BlockSpec((3, tr, cols), lambda i, chip_ref: (0, i, 0))],
        out_specs=pl.BlockSpec((tr, cols), lambda i, chip_ref: (i, 0)))
    return pl.pallas_call(
        body, name=name, grid_spec=gs, out_shape=S((half, cols), F32), compiler_params=_params(1),
    )(chip, own, others)


def _adamw(w, g, m, v):
    m = ADAM_B1 * m + (1.0 - ADAM_B1) * g
    v = ADAM_B2 * v + (1.0 - ADAM_B2) * (g * g)
    m_hat = m / (1.0 - ADAM_B1 ** ADAM_STEP)
    v_hat = v / (1.0 - ADAM_B2 ** ADAM_STEP)
    delta = -ADAM_LR * (m_hat / (jnp.sqrt(v_hat) + ADAM_EPS) + ADAM_WD * w)
    return delta, m, v


def _adamw_shard(core, mine, theirs, w, m, v, name):
    r, cols = w.shape
    half = r // 2
    tr = _row_tile(half)
    nb = half // tr

    def body(core_ref, mine_ref, theirs_ref, w_ref, m_ref, v_ref, g_ref, d_ref, mo_ref, vo_ref):
        g = jnp.where(pl.program_id(0) == core_ref[0], mine_ref[...], theirs_ref[...])
        g_ref[...] = g
        d_ref[...], mo_ref[...], vo_ref[...] = _adamw(w_ref[...], g, m_ref[...], v_ref[...])

    hblk = pl.BlockSpec((tr, cols), lambda h, i, core_ref: (i, 0))
    blk = pl.BlockSpec((tr, cols), lambda h, i, core_ref: (h * nb + i, 0))
    gs = pltpu.PrefetchScalarGridSpec(num_scalar_prefetch=1, grid=(2, nb),
                                      in_specs=[hblk, hblk, blk, blk, blk], out_specs=[blk] * 4)
    return pl.pallas_call(
        body, name=name, grid_spec=gs, out_shape=[S((r, cols), F32)] * 4, compiler_params=_params(2),
    )(core, mine, theirs, w, m, v)


def _adamw_whole(w, g, m, v, name):
    def body(w_ref, g_ref, m_ref, v_ref, d_ref, mo_ref, vo_ref):
        d_ref[...], mo_ref[...], vo_ref[...] = _adamw(w_ref[...], g_ref[...], m_ref[...], v_ref[...])

    return pl.pallas_call(body, name=name, out_shape=[S(w.shape, F32)] * 3)(w, g, m, v)


def _small_sum_adamw(parts, w, m, v):
    rows = w.shape[0]

    def body(p_ref, w_ref, m_ref, v_ref, g_ref, d_ref, mo_ref, vo_ref):
        total = p_ref[0]
        for k in range(1, N_DEVICE):
            total = total + p_ref[k]
        g_ref[...] = total
        d_ref[...], mo_ref[...], vo_ref[...] = _adamw(w_ref[...], total, m_ref[...], v_ref[...])

    return pl.pallas_call(
        body, name="small_sum_adamw", out_shape=[S((rows, D), F32)] * 4,
        compiler_params=pltpu.CompilerParams(vmem_limit_bytes=VMEM_LIMIT),
    )(parts, w, m, v)


_BIG = ("w_in", "w_proj_a", "w_proj_b", "w_out", "w_gate_up", "w_down")
_SMALL = ("norm_mix_g", "conv_b", "rg_wa", "rg_ba", "rg_wx", "rg_bx", "rg_lambda", "sgu_ln_g", "sgu_ln_b",
          "sgu_ws", "sgu_bs", "norm_ffn_g", "norm_final_g")
_WEIGHTS = ("norm_mix_g", "w_in", "conv_w", "conv_b", "rg_wa", "rg_ba", "rg_wx", "rg_bx", "rg_lambda", "sgu_ln_g",
            "sgu_ln_b", "sgu_ws", "sgu_bs", "w_proj_a", "w_proj_b", "w_out", "norm_ffn_g", "w_gate_up", "w_down",
            "norm_final_g")


def _pack(arrays):
    rows = [a.reshape(-1, D) for a in arrays]
    pad = -sum(r.shape[0] for r in rows) % SUBLANES
    return jnp.concatenate(rows + [jnp.zeros((pad, D), F32)], axis=0)


def _unpack(packed, like):
    out, row = [], 0
    for a in like:
        n = a.size // D
        out.append(packed[row:row + n].reshape(a.shape))
        row += n
    return out


def kernel(x, norm_mix_g, w_in, conv_w, conv_b, rg_wa, rg_ba, rg_wx, rg_bx, rg_lambda, sgu_ln_g, sgu_ln_b, sgu_ws, sgu_bs, w_proj_a, w_proj_b, w_out, norm_ffn_g, w_gate_up, w_down, norm_final_g, loss_target, m_norm_mix_g, m_w_in, m_conv_w, m_conv_b, m_rg_wa, m_rg_ba, m_rg_wx, m_rg_bx, m_rg_lambda, m_sgu_ln_g, m_sgu_ln_b, m_sgu_ws, m_sgu_bs, m_w_proj_a, m_w_proj_b, m_w_out, m_norm_ffn_g, m_w_gate_up, m_w_down, m_norm_final_g, v_norm_mix_g, v_w_in, v_conv_w, v_conv_b, v_rg_wa, v_rg_ba, v_rg_wx, v_rg_bx, v_rg_lambda, v_sgu_ln_g, v_sgu_ln_b, v_sgu_ws, v_sgu_bs, v_w_proj_a, v_w_proj_b, v_w_out, v_norm_ffn_g, v_w_gate_up, v_w_down, v_norm_final_g):
    args = dict(locals())
    w = {n: args[n] for n in _WEIGHTS}
    mom = {n: args["m_" + n] for n in _WEIGHTS}
    var = {n: args["v_" + n] for n in _WEIGHTS}
    xi, yi, ci = _position()
    core = ci.astype(jnp.int32).reshape(1)
    chip = (2 * xi + yi).astype(jnp.int32).reshape(1)

    bf = {n: w[n][0].astype(BF16) for n in _BIG}
    sm = {n: w[n][0] if w[n].size > D else w[n].reshape(1, D) for n in _SMALL}
    hd = sm["rg_wa"].shape[-1]
    wa = _blockdiag(sm["rg_wa"]).astype(BF16)
    wx = _blockdiag(sm["rg_wx"]).astype(BF16)
    w_tril = jnp.where(jnp.tril(jnp.ones((CHUNK, CHUNK), bool))[None], sm["sgu_ws"], 0.0).astype(BF16)
    w_tril_t = jnp.swapaxes(w_tril, 1, 2)
    bs_t = sm["sgu_bs"].reshape(N_GROUP, CHUNK).T
    lam, ln_g, ln_b = sm["rg_lambda"], sm["sgu_ln_g"], sm["sgu_ln_b"]
    x0, target = x[0], loss_target[0]

    def shard_major(g):
        return g.reshape(N_SHARD, g.shape[0] // N_SHARD, g.shape[1])

    def chip_sums(names, grads, theirs):
        return [_add_halves(core, g, t, "add_halves_" + n) for n, g, t in zip(names, grads, theirs)]

    def my_halves(names, sums, arrived):
        return [_sum_shards(chip, p32, a, "sum_shards_" + n) for n, (p32, _), a in zip(names, sums, arrived)]

    w_in_a, conv_a = _comm_only([_gather_side([bf["w_in"], w["conv_w"][0]])], "gather_w_in")[0]
    conv_cols = conv_a.shape[-1]
    conv_full = jnp.swapaxes(conv_a, 0, 1).reshape(CONV_WIDTH, D)
    (proj, h), ((w_pa_a, w_pb_a, w_out_a),) = _inproj(
        x0, sm["norm_mix_g"], w_in_a, sides=[_gather_side([bf["w_proj_a"], bf["w_proj_b"], bf["w_out"]])])
    (ya, xc, hseq), ((w_gu_a, w_down_a),) = _rglru_fwd(
        proj, conv_full, sm["conv_b"], wa, wx, sm["rg_ba"], sm["rg_bx"], lam,
        sides=[_gather_side([bf["w_gate_up"], bf["w_down"]])])
    wpa, wpb, wout, wdown = w_pa_a.reshape(D, D), w_pb_a.reshape(D, D), w_out_a.reshape(D, D), w_down_a.reshape(D_FF, D)
    yb = _sgu_fwd(proj, ln_g, ln_b, w_tril, bs_t)
    pa, pb, mb, x1 = _merge_fwd(x0, proj, ya, yb, wpa, wpb, wout)
    h2, gu, act = _ffn_up(x1, sm["norm_ffn_g"], w_gu_a)
    dx2, dx2b, loss, d_final_g = _ffn_down_loss(x1, act, wdown, sm["norm_final_g"], target)

    dgu = _ffn_bwd_act(dx2b, gu, wdown)
    dx1, dx1b, d_ffn_g = _ffn_bwd_in(dgu, w_gu_a, x1, sm["norm_ffn_g"], dx2)
    ffn = ("w_gate_up", "w_down")
    g_ffn = [_matmul_tn(h2, dgu, FF_SHARD, True, "wgrad_gate_up"),
             shard_major(_matmul_tn(act, dx2b, D // 2, False, "wgrad_down"))]
    (dpa, dpb, dgate, dya, dyb), (theirs_ffn,) = _merge_bwd(
        dx1b, proj, pa, pb, wpa, wpb, wout, sides=[_halves_side(g_ffn)])
    sums_ffn = chip_sums(ffn, g_ffn, theirs_ffn)
    mix = ("w_proj_a", "w_proj_b", "w_out")
    g_mix = [shard_major(_matmul_tn(ya, dpa, D, False, "wgrad_proj_a")),
             shard_major(_matmul_tn(yb, dpb, D, False, "wgrad_proj_b")),
             shard_major(_matmul_tn(mb, dx1b, D, False, "wgrad_out"))]
    (dab, d_cw, d_cb, d_wa, d_wx, d_ba, d_bx, d_lam), (arrived_ffn, theirs_mix) = _rglru_bwd(
        dya, proj, hseq, xc, conv_full, wa, wx, sm["rg_ba"], sm["rg_bx"], lam,
        sides=[_scatter_side([p16 for _, p16 in sums_ffn]), _halves_side(g_mix)])
    mine_ffn = my_halves(ffn, sums_ffn, arrived_ffn)
    sums_mix = chip_sums(mix, g_mix, theirs_mix)
    (duv, d_ws, d_bst, d_lng, d_lnb), (other_ffn, arrived_mix) = _sgu_bwd(
        dyb, proj, ln_g, ln_b, w_tril, w_tril_t, bs_t,
        sides=[_swap_side(mine_ffn), _scatter_side([p16 for _, p16 in sums_mix])])
    mine_mix = my_halves(mix, sums_mix, arrived_mix)
    (grad_x, d_mix_g), (other_mix,) = _inproj_bwd(dab, duv, dgate, w_in_a, x0, sm["norm_mix_g"], dx1,
                                                  sides=[_swap_side(mine_mix)])
    small_grads = {
        "norm_mix_g": d_mix_g, "conv_b": d_cb, "rg_wa": _blockdiag_heads(d_wa, hd), "rg_ba": d_ba,
        "rg_wx": _blockdiag_heads(d_wx, hd), "rg_bx": d_bx, "rg_lambda": d_lam, "sgu_ln_g": d_lng,
        "sgu_ln_b": d_lnb, "sgu_ws": d_ws, "sgu_bs": d_bst.T, "norm_ffn_g": d_ffn_g, "norm_final_g": d_final_g,
    }
    (g_in,), ((small_parts,),) = _inproj_wgrad(
        h, dab, duv, dgate, sides=[_everyone_side(_pack([small_grads[n] for n in _SMALL] + [d_cw]))])
    theirs_in = _comm_only([_halves_side([g_in])], "halves_w_in")[0]
    sums_in = chip_sums(("w_in",), [g_in], theirs_in)
    arrived_in = _comm_only([_scatter_side([p16 for _, p16 in sums_in])], "scatter_w_in")[0]
    mine_in = my_halves(("w_in",), sums_in, arrived_in)
    other_in = _comm_only([_swap_side(mine_in)], "swap_w_in")[0]

    out = {}
    for n, gm, go in zip(ffn + mix + ("w_in",), mine_ffn + mine_mix + mine_in, other_ffn + other_mix + other_in):
        g, d, mo, vo = _adamw_shard(core, gm, go, w[n][0], mom[n][0], var[n][0], "adamw_" + n)
        out[n] = tuple(a[None] for a in (g, d, mo, vo))
    zeros_conv = jnp.zeros((CONV_WIDTH, D), F32)
    g_pack, d_pack, m_pack, v_pack = _small_sum_adamw(
        small_parts,
        _pack([w[n] for n in _SMALL] + [zeros_conv]),
        _pack([mom[n] for n in _SMALL] + [zeros_conv]),
        _pack([var[n] for n in _SMALL] + [zeros_conv]))
    like = [w[n] for n in _SMALL] + [zeros_conv]
    unpacked = [_unpack(p, like) for p in (g_pack, d_pack, m_pack, v_pack)]
    for k, n in enumerate(_SMALL):
        out[n] = tuple(u[k] for u in unpacked)
    conv_g = lax.dynamic_slice_in_dim(unpacked[0][-1], chip[0] * conv_cols, conv_cols, axis=1)
    d, mo, vo = _adamw_whole(w["conv_w"][0], conv_g, mom["conv_w"][0], var["conv_w"][0], "adamw_conv_w")
    out["conv_w"] = tuple(a[None] for a in (conv_g, d, mo, vo))

    loss = lax.psum(loss[0, 0], ("x", "y", "c"))
    return (loss, grad_x[None], *[out[n][0] for n in _WEIGHTS], *[out[n][1] for n in _WEIGHTS],
            *[out[n][2] for n in _WEIGHTS], *[out[n][3] for n in _WEIGHTS])
```

```python
import functools

import jax
import jax.numpy as jnp
from jax import lax
from jax.experimental import pallas as pl
from jax.experimental.pallas import tpu as pltpu

F32 = jnp.float32
BF16 = jnp.bfloat16
S = jax.ShapeDtypeStruct

D = 1024
N_SHARD = 4
IN_COLS = 6 * D
IN_SHARD = IN_COLS // N_SHARD
D_FF = 2816
FF_SHARD = 2 * D_FF // N_SHARD
RG_BLOCK = 256
N_RG_BLOCK = D // RG_BLOCK
CHUNK = 128
N_GROUP = 8
CONV_WIDTH = 4
RG_C = 8.0
EPS = 1e-6
ADAM_LR, ADAM_B1, ADAM_B2, ADAM_EPS, ADAM_WD, ADAM_STEP = 0.001, 0.9, 0.999, 1e-08, 0.01, 10

V7X_VMEM_BYTES = 64 * 1024 * 1024
VMEM_LIMIT = V7X_VMEM_BYTES * 3 // 4
SUBLANES = 8
MESH = pl.DeviceIdType.MESH

TM_MM = 512
TM_SCAN = 256
TM_FF = 256


def _params(n_axes):
    return pltpu.CompilerParams(dimension_semantics=("arbitrary",) * n_axes, vmem_limit_bytes=VMEM_LIMIT)


def _resident(shape):
    nd = len(shape)
    return pl.BlockSpec(shape, lambda *_: (0,) * nd, pipeline_mode=pl.Buffered(1))


def _sig(x):
    return 1.0 / (1.0 + jnp.exp(-x))


_GELU_K2 = 2.0 * 0.7978845608028654
_GELU_C = 0.044715


def _gelu(x):
    return x * _sig(x * (_GELU_K2 + (_GELU_K2 * _GELU_C) * (x * x)))


def _gelu_and_grad(x):
    x2 = x * x
    s = _sig(x * (_GELU_K2 + (_GELU_K2 * _GELU_C) * x2))
    g = x * s
    return g, s + g * (1.0 - s) * (_GELU_K2 + (3.0 * _GELU_K2 * _GELU_C) * x2)


_EXPM1_SERIES = tuple(1.0 / f for f in (5040.0, 720.0, 120.0, 24.0, 6.0, 2.0, 1.0))


def _one_minus_exp(x):
    p = _EXPM1_SERIES[0]
    for coef in _EXPM1_SERIES[1:]:
        p = p * x + coef
    return jnp.where(x > -0.125, -x * p, 1.0 - jnp.exp(x))


def _softplus_neg(lam):
    z = -lam
    e = jnp.exp(-jnp.abs(z))
    u = 1.0 + e
    log1p = jnp.where(u == 1.0, e, jnp.log(u) * e / (u - 1.0))
    return jnp.maximum(z, 0.0) + log1p


def _rms_stats(x):
    return lax.rsqrt(jnp.mean(x * x, axis=-1, keepdims=True) + EPS)


def _rms_bwd(dy, x, g):
    rstd = _rms_stats(x)
    xhat = x * rstd
    dxhat = dy * g
    dx = rstd * (dxhat - xhat * jnp.mean(dxhat * xhat, axis=-1, keepdims=True))
    return dx, dy * xhat


def _colsum(x):
    return jnp.sum(x, axis=0, keepdims=True)


def _shift_down(x, d, fill):
    n = x.shape[0]
    if d % SUBLANES == 0:
        return jnp.concatenate([jnp.full((d, x.shape[1]), fill, x.dtype), x[:n - d]], axis=0)
    row = lax.broadcasted_iota(jnp.int32, x.shape, 0)
    return jnp.where(row < d, fill, pltpu.roll(x, d, 0))


def _shift_up(x, d, fill):
    n = x.shape[0]
    if d % SUBLANES == 0:
        return jnp.concatenate([x[d:], jnp.full((d, x.shape[1]), fill, x.dtype)], axis=0)
    row = lax.broadcasted_iota(jnp.int32, x.shape, 0)
    return jnp.where(row >= n - d, fill, pltpu.roll(x, n - d, 0))


def _scan(a, b, shift):
    d = 1
    while d < a.shape[0]:
        b = a * shift(b, d, 0.0) + b
        a = a * shift(a, d, 1.0)
        d *= 2
    return a, b


LANES = 128


def _scan_tile(a, b, outside, a_s, b_s, h_s, reverse):
    tm = a.shape[0]
    groups = tm // SUBLANES
    order = list(range(SUBLANES - 1, -1, -1) if reverse else range(SUBLANES))
    shift = _shift_up if reverse else _shift_down
    edge = groups - 1 if reverse else 0
    for j in range(D // LANES):
        a_s[j] = a[:, LANES * j:LANES * (j + 1)]
        b_s[j] = b[:, LANES * j:LANES * (j + 1)]
    for j in range(D // LANES):
        def slab(ref, k):
            return ref[j, pl.ds(k, groups, stride=SUBLANES), :]

        ga, gb = slab(a_s, order[0]), slab(b_s, order[0])
        for k in order[1:]:
            ak = slab(a_s, k)
            gb = ak * gb + slab(b_s, k)
            ga = ak * ga
        ga, gb = _scan(ga, gb, shift)
        h_out = outside[:, LANES * j:LANES * (j + 1)]
        group_end = ga * h_out + gb
        row = lax.broadcasted_iota(jnp.int32, (groups, LANES), 0)
        h = jnp.where(row == edge, h_out, shift(group_end, 1, 0.0))
        for k in order:
            h = slab(a_s, k) * h + slab(b_s, k)
            h_s[j, pl.ds(k, groups, stride=SUBLANES), :] = h
    return jnp.concatenate([h_s[j] for j in range(D // LANES)], axis=1)


def _dot(a, b):
    return jnp.dot(a, b, preferred_element_type=F32)


def _dot_nt(a, b):
    return lax.dot_general(a, b, (((1,), (1,)), ((), ())), preferred_element_type=F32)


def _dot_tn(a, b):
    return lax.dot_general(a, b, (((0,), (0,)), ((), ())), preferred_element_type=F32)


_ANY = pl.BlockSpec(memory_space=pl.ANY)


def _position():
    return lax.axis_index("x"), lax.axis_index("y"), lax.axis_index("c")


def _other_chips(x, y):
    return [(1 - x, y), (x, 1 - y), (1 - x, 1 - y)]


class _Side:
    def __init__(self, inputs, out_shapes, n_sems, make):
        self.inputs, self.out_shapes, self.n_sems, self.make = list(inputs), list(out_shapes), n_sems, make


MID_STEP = 0.625


def _call(body, *, name, grid, in_specs, out_specs, out_shape, args, scratch_shapes=(), sides=()):
    n_in, n_out, n_scr = len(in_specs), len(out_specs), len(scratch_shapes)
    side_in = [len(s.inputs) for s in sides]
    side_out = [len(s.out_shapes) for s in sides]

    def wrapped(*refs):
        refs = list(refs)
        take = lambda k: [refs.pop(0) for _ in range(k)]
        ins = take(n_in)
        sins = [take(k) for k in side_in]
        outs = take(n_out)
        souts = [take(k) for k in side_out]
        scr = take(n_scr)
        sems = [take(3) for _ in sides]
        def run(phase):
            for s, si, so, sem in zip(sides, sins, souts, sems):
                for thunk in s.make(si, so, *sem)[phase]:
                    thunk()

        if sides:
            n_steps = functools.reduce(lambda a, b: a * b, grid)
            step = functools.reduce(lambda a, b: a + b, [
                pl.program_id(a) * functools.reduce(lambda p, q: p * q, grid[a + 1:], 1) for a in range(len(grid))])
            pl.when(step == 0)(lambda: run(0))
        body(*ins, *outs, *scr)
        if sides:
            pl.when(step == int(MID_STEP * (n_steps - 1)))(lambda: run(1))
            pl.when(step == n_steps - 1)(lambda: run(2))

    res = pl.pallas_call(
        wrapped, name=name, grid=grid,
        in_specs=list(in_specs) + [_ANY] * sum(side_in),
        out_specs=list(out_specs) + [_ANY] * sum(side_out),
        out_shape=list(out_shape) + [o for s in sides for o in s.out_shapes],
        scratch_shapes=list(scratch_shapes) + [pltpu.SemaphoreType.DMA((s.n_sems,)) for s in sides for _ in range(3)],
        compiler_params=_params(len(grid)),
    )(*args, *[a for s in sides for a in s.inputs])
    main, rest, per_side = list(res[:n_out]), list(res[n_out:]), []
    for k in side_out:
        per_side.append(rest[:k])
        rest = rest[k:]
    return main, per_side


def _comm_only(sides, name):
    def body():
        pass

    return _call(body, name=name, grid=(1,), in_specs=[], out_specs=[], out_shape=[], args=[], sides=sides)[1]


def _remote(src, dst, send, recv, k, device):
    return pltpu.make_async_remote_copy(src_ref=src, dst_ref=dst, send_sem=send.at[k], recv_sem=recv.at[k],
                                        device_id=device, device_id_type=MESH)


def _both_ways(copy, keys):
    return [lambda k=k: copy(k).start() for k in keys], [], [lambda k=k: copy(k).wait() for k in keys]


def _gather_side(shards):
    n = len(shards)

    def make(ins, outs, send, recv, local):
        x, y, c = _position()
        mine = 2 * x + y
        chips = _other_chips(x, y)
        pairs = [(w, j) for w in range(n) for j in range(3)]

        def own(w):
            return pltpu.make_async_copy(ins[w], outs[w].at[mine], local.at[w])

        def push(w, j):
            return _remote(ins[w], outs[w].at[mine], send, recv, 3 * w + j, (*chips[j], c))

        def arrival(w, j):
            px, py = chips[j]
            return _remote(ins[w], outs[w].at[2 * px + py], send, recv, 3 * w + j, (px, py, c))

        starts = [lambda w=w: own(w).start() for w in range(n)] + [lambda w=w, j=j: push(w, j).start() for w, j in pairs]
        waits = ([lambda w=w, j=j: arrival(w, j).wait_recv() for w, j in pairs]
                 + [lambda w=w, j=j: push(w, j).wait_send() for w, j in pairs]
                 + [lambda w=w: own(w).wait() for w in range(n)])
        return starts, [], waits

    return _Side(shards, [S((N_SHARD,) + s.shape, s.dtype) for s in shards], 3 * n, make)


def _gather_half_side(shards):
    n = len(shards)

    def make(ins, outs, send, recv, local):
        x, y, c = _position()
        mine = 2 * x + y
        chips = _other_chips(x, y)
        pairs = [(w, j) for w in range(n) for j in range(3)]

        def rows(w, core):
            half = ins[w].shape[0] // 2
            return pl.ds(core * half, half)

        def own(w):
            return pltpu.make_async_copy(ins[w], outs[w].at[mine], local.at[w])

        def push(w, j):
            return _remote(ins[w].at[rows(w, c), :], outs[w].at[mine, rows(w, c), :], send, recv, 3 * w + j,
                           (*chips[j], c))

        def landed(w, j, core):
            px, py = chips[j]
            return outs[w].at[2 * px + py, rows(w, core), :]

        def arrival(w, j):
            return _remote(ins[w].at[rows(w, c), :], landed(w, j, c), send, recv, 3 * w + j, (*chips[j], c))

        def passed(w, j, core):
            return _remote(landed(w, j, core), landed(w, j, core), send, recv, 3 * n + 3 * w + j, (x, y, 1 - c))

        starts = [lambda w=w: own(w).start() for w in range(n)] + [lambda w=w, j=j: push(w, j).start() for w, j in pairs]
        mids = [t for w, j in pairs for t in (lambda w=w, j=j: arrival(w, j).wait_recv(),
                                              lambda w=w, j=j: passed(w, j, c).start())]
        waits = ([lambda w=w, j=j: passed(w, j, 1 - c).wait_recv() for w, j in pairs]
                 + [lambda w=w, j=j: passed(w, j, c).wait_send() for w, j in pairs]
                 + [lambda w=w, j=j: push(w, j).wait_send() for w, j in pairs]
                 + [lambda w=w: own(w).wait() for w in range(n)])
        return starts, mids, waits

    return _Side(shards, [S((N_SHARD,) + s.shape, s.dtype) for s in shards], 6 * n, make)


def _halves_side(grads):
    n = len(grads)

    def make(ins, outs, send, recv, local):
        x, y, c = _position()

        def copy(w):
            half = ins[w].shape[1] // 2
            return _remote(ins[w].at[:, pl.ds((1 - c) * half, half), :], outs[w], send, recv, w, (x, y, 1 - c))

        return _both_ways(copy, range(n))

    return _Side(grads, [S((N_SHARD, g.shape[1] // 2, g.shape[2]), F32) for g in grads], n, make)


def _scatter_side(partials):
    n = len(partials)

    def make(ins, outs, send, recv, local):
        x, y, c = _position()
        chips = _other_chips(x, y)

        def copy(k):
            w, j = divmod(k, 3)
            px, py = chips[j]
            return _remote(ins[w].at[2 * px + py], outs[w].at[j], send, recv, k, (px, py, c))

        return _both_ways(copy, range(3 * n))

    return _Side(partials, [S((3,) + p.shape[1:], p.dtype) for p in partials], 3 * n, make)


def _swap_side(halves):
    n = len(halves)

    def make(ins, outs, send, recv, local):
        x, y, c = _position()
        return _both_ways(lambda w: _remote(ins[w], outs[w], send, recv, w, (x, y, 1 - c)), range(n))

    return _Side(halves, [S(h.shape, h.dtype) for h in halves], n, make)


N_DEVICE = 8


def _everyone_side(packed):
    def make(ins, outs, send, recv, local):
        x, y, c = _position()
        mine = 4 * x + 2 * y + c
        others = range(1, N_DEVICE)

        def peer(k):
            return (1 - x if k & 4 else x, 1 - y if k & 2 else y, 1 - c if k & 1 else c)

        def own():
            return pltpu.make_async_copy(ins[0], outs[0].at[mine], local.at[0])

        def push(k):
            return _remote(ins[0], outs[0].at[mine], send, recv, k - 1, peer(k))

        def arrival(k):
            px, py, pc = peer(k)
            return _remote(ins[0], outs[0].at[4 * px + 2 * py + pc], send, recv, k - 1, (px, py, pc))

        starts = [lambda: own().start()] + [lambda k=k: push(k).start() for k in others]
        waits = ([lambda k=k: arrival(k).wait_recv() for k in others] + [lambda k=k: push(k).wait_send() for k in others]
                 + [lambda: own().wait()])
        return starts, [], waits

    return _Side([packed], [S((N_DEVICE,) + packed.shape, packed.dtype)], N_DEVICE - 1, make)


def _inproj(x, g, w_in, sides=()):
    T = x.shape[0]
    tm = min(TM_FF, T)

    def body(x_ref, g_ref, w_ref, proj_ref, h_ref):
        xv = x_ref[...]
        h = (xv * _rms_stats(xv) * g_ref[...]).astype(BF16)
        h_ref[...] = h
        for k in range(N_SHARD):
            proj_ref[:, IN_SHARD * k:IN_SHARD * (k + 1)] = _dot(h, w_ref[k])

    return _call(
        body, name="inproj", grid=(T // tm,),
        in_specs=[pl.BlockSpec((tm, D), lambda i: (i, 0)), pl.BlockSpec((1, D), lambda i: (0, 0)),
                  _resident((N_SHARD, D, IN_SHARD))],
        out_specs=[pl.BlockSpec((tm, IN_COLS), lambda i: (i, 0)), pl.BlockSpec((tm, D), lambda i: (i, 0))],
        out_shape=[S((T, IN_COLS), F32), S((T, D), BF16)],
        args=(x, g, w_in), sides=sides)


def _rg_gates(xc, wa_ref, wx_ref, ba, bx, sp):
    xb = xc.astype(BF16)
    blocks = [xb[:, RG_BLOCK * j:RG_BLOCK * (j + 1)] for j in range(N_RG_BLOCK)]
    r = _sig(jnp.concatenate([_dot(blocks[j], wa_ref[j]) for j in range(N_RG_BLOCK)], axis=1) + ba)
    gi = _sig(jnp.concatenate([_dot(blocks[j], wx_ref[j]) for j in range(N_RG_BLOCK)], axis=1) + bx)
    log_a = (-RG_C) * r * sp
    a = jnp.exp(log_a)
    m = jnp.sqrt(_one_minus_exp(2.0 * log_a))
    return xb, r, gi, a, m


def _rglru_fwd(proj, conv_w, conv_b, wa, wx, ba, bx, lam, sides=()):
    T = proj.shape[0]
    tm = min(TM_SCAN, T)

    def body(rx_ref, gate_ref, cw_ref, cb_ref, wa_ref, wx_ref, ba_ref, bx_ref, lam_ref,
             ya_ref, xc_ref, h_ref, ext, hc, a_s, b_s, h_s):
        @pl.when(pl.program_id(0) == 0)
        def _():
            ext[0:SUBLANES, :] = jnp.zeros((SUBLANES, D), F32)
            hc[...] = jnp.zeros((SUBLANES, D), F32)

        ext[SUBLANES:SUBLANES + tm, :] = rx_ref[...]
        xc = cb_ref[...]
        for k in range(CONV_WIDTH):
            xc = xc + ext[pl.ds(SUBLANES - (CONV_WIDTH - 1) + k, tm), :] * cw_ref[k:k + 1, :]
        ext[0:SUBLANES, :] = ext[tm:tm + SUBLANES, :]
        xc_ref[...] = xc
        _, _, gi, a, m = _rg_gates(xc, wa_ref, wx_ref, ba_ref[...], bx_ref[...], _softplus_neg(lam_ref[...]))
        h = _scan_tile(a, m * (gi * xc), hc[0:1, :], a_s, b_s, h_s, reverse=False)
        hc[...] = jnp.broadcast_to(h[tm - 1:tm, :], (SUBLANES, D))
        h_ref[...] = h
        ya_ref[...] = (_gelu(gate_ref[...]) * h).astype(BF16)

    vec = pl.BlockSpec((1, D), lambda i: (0, 0))
    blk = pl.BlockSpec((N_RG_BLOCK, RG_BLOCK, RG_BLOCK), lambda i: (0, 0, 0))
    tile = pl.BlockSpec((tm, D), lambda i: (i, 0))
    return _call(
        body, name="rglru_fwd", grid=(T // tm,),
        in_specs=[pl.BlockSpec((tm, D), lambda i: (i, 0)), pl.BlockSpec((tm, D), lambda i: (i, 1)),
                  pl.BlockSpec((CONV_WIDTH, D), lambda i: (0, 0)), vec, blk, blk, vec, vec, vec],
        out_specs=[tile, tile, tile],
        out_shape=[S((T, D), BF16), S((T, D), F32), S((T, D), F32)],
        scratch_shapes=[pltpu.VMEM((tm + SUBLANES, D), F32), pltpu.VMEM((SUBLANES, D), F32)]
        + [pltpu.VMEM((D // LANES, tm, LANES), F32)] * 3,
        args=(proj, proj, conv_w, conv_b, wa, wx, ba, bx, lam), sides=sides)


def _layer_norm_stats(v):
    mu = jnp.mean(v, axis=-1, keepdims=True)
    vc = v - mu
    rstd = lax.rsqrt(jnp.mean(vc * vc, axis=-1, keepdims=True) + EPS)
    return vc * rstd, rstd


def _sgu_mix(w_ref, vnb, bst_ref, n_chunk):
    cols = []
    for g in range(N_GROUP):
        vg = vnb[:, CHUNK * g:CHUNK * (g + 1)].reshape(n_chunk, CHUNK, CHUNK)
        wb = jnp.broadcast_to(w_ref[g][None], (n_chunk, CHUNK, CHUNK))
        mg = lax.dot_general(wb, vg, (((2,), (1,)), ((0,), (0,))), preferred_element_type=F32)
        mg = mg + bst_ref[:, g:g + 1][None]
        cols.append(mg.reshape(n_chunk * CHUNK, CHUNK))
    return jnp.concatenate(cols, axis=1)


def _sgu_fwd(proj, ln_g, ln_b, w_tril, bs_t):
    T = proj.shape[0]
    tm = min(TM_MM, T)
    n_chunk = tm // CHUNK

    def body(u_ref, v_ref, g_ref, b_ref, w_ref, bst_ref, yb_ref):
        vhat, _ = _layer_norm_stats(_gelu(v_ref[...]))
        vnb = (vhat * g_ref[...] + b_ref[...]).astype(BF16)
        yb_ref[...] = (_gelu(u_ref[...]) * _sgu_mix(w_ref, vnb, bst_ref, n_chunk)).astype(BF16)

    vec = pl.BlockSpec((1, D), lambda i: (0, 0))
    return pl.pallas_call(
        body, name="sgu_fwd", grid=(T // tm,),
        in_specs=[pl.BlockSpec((tm, D), lambda i: (i, 2)), pl.BlockSpec((tm, D), lambda i: (i, 3)), vec, vec,
                  pl.BlockSpec((N_GROUP, CHUNK, CHUNK), lambda i: (0, 0, 0)),
                  pl.BlockSpec((CHUNK, N_GROUP), lambda i: (0, 0))],
        out_specs=pl.BlockSpec((tm, D), lambda i: (i, 0)),
        out_shape=S((T, D), BF16),
        compiler_params=_params(1),
    )(proj, proj, ln_g, ln_b, w_tril, bs_t)


def _merge_fwd(x, proj, ya, yb, wpa, wpb, wout):
    T = x.shape[0]
    tm = min(TM_MM, T)

    def body(x_ref, ga_ref, gb_ref, ya_ref, yb_ref, wpa_ref, wpb_ref, wout_ref, pa_ref, pb_ref, mb_ref, x1_ref):
        pa = _dot(ya_ref[...], wpa_ref[...])
        pb = _dot(yb_ref[...], wpb_ref[...])
        pa_ref[...] = pa
        pb_ref[...] = pb
        mb = (_sig(ga_ref[...]) * pa + _sig(gb_ref[...]) * pb).astype(BF16)
        mb_ref[...] = mb
        x1_ref[...] = x_ref[...] + _dot(mb, wout_ref[...])

    tile = pl.BlockSpec((tm, D), lambda i: (i, 0))
    w = _resident((D, D))
    return pl.pallas_call(
        body, name="merge_fwd", grid=(T // tm,),
        in_specs=[tile, pl.BlockSpec((tm, D), lambda i: (i, 4)), pl.BlockSpec((tm, D), lambda i: (i, 5)),
                  tile, tile, w, w, w],
        out_specs=[tile, tile, tile, tile],
        out_shape=[S((T, D), F32), S((T, D), F32), S((T, D), BF16), S((T, D), F32)],
        compiler_params=_params(1),
    )(x, proj, proj, ya, yb, wpa, wpb, wout)


def _ffn_up(x1, g, w_gu):
    T = x1.shape[0]
    tm = min(TM_FF, T)

    def body(x_ref, g_ref, w_ref, h2_ref, gu_ref, act_ref):
        xv = x_ref[...]
        h2 = (xv * _rms_stats(xv) * g_ref[...]).astype(BF16)
        h2_ref[...] = h2
        for k in range(N_SHARD // 2):
            cols = slice(FF_SHARD * k, FF_SHARD * (k + 1))
            gate = _dot(h2, w_ref[k])
            up = _dot(h2, w_ref[k + N_SHARD // 2])
            gu_ref[:, cols] = gate.astype(BF16)
            gu_ref[:, D_FF + FF_SHARD * k:D_FF + FF_SHARD * (k + 1)] = up.astype(BF16)
            act_ref[:, cols] = (gate * _sig(gate) * up).astype(BF16)

    return pl.pallas_call(
        body, name="ffn_up", grid=(T // tm,),
        in_specs=[pl.BlockSpec((tm, D), lambda i: (i, 0)), pl.BlockSpec((1, D), lambda i: (0, 0)),
                  _resident((N_SHARD, D, FF_SHARD))],
        out_specs=[pl.BlockSpec((tm, D), lambda i: (i, 0)), pl.BlockSpec((tm, 2 * D_FF), lambda i: (i, 0)),
                   pl.BlockSpec((tm, D_FF), lambda i: (i, 0))],
        out_shape=[S((T, D), BF16), S((T, 2 * D_FF), BF16), S((T, D_FF), BF16)],
        compiler_params=_params(1),
    )(x1, g, w_gu)


def _ffn_down_loss(x1, act, w_down, g_final, target):
    T = x1.shape[0]
    tm = min(TM_MM, T)

    def body(x_ref, act_ref, w_ref, g_ref, t_ref, dx2_ref, dx2b_ref, loss_ref, dg_ref):
        @pl.when(pl.program_id(0) == 0)
        def _():
            loss_ref[...] = jnp.zeros_like(loss_ref)
            dg_ref[...] = jnp.zeros_like(dg_ref)

        x2 = x_ref[...] + _dot(act_ref[...], w_ref[...])
        gf = g_ref[...]
        err = x2 * _rms_stats(x2) * gf - t_ref[...]
        loss_ref[...] += 0.5 * jnp.sum(jnp.mean(err * err, axis=-1, keepdims=True), axis=0, keepdims=True)
        dx2, dg_rows = _rms_bwd(err * (1.0 / D), x2, gf)
        dg_ref[...] += _colsum(dg_rows)
        dx2_ref[...] = dx2
        dx2b_ref[...] = dx2.astype(BF16)

    tile = pl.BlockSpec((tm, D), lambda i: (i, 0))
    vec = pl.BlockSpec((1, D), lambda i: (0, 0))
    return pl.pallas_call(
        body, name="ffn_down_loss", grid=(T // tm,),
        in_specs=[tile, pl.BlockSpec((tm, D_FF), lambda i: (i, 0)), _resident((D_FF, D)), vec, tile],
        out_specs=[tile, tile, pl.BlockSpec((1, 1), lambda i: (0, 0)), vec],
        out_shape=[S((T, D), F32), S((T, D), BF16), S((1, 1), F32), S((1, D), F32)],
        compiler_params=_params(1),
    )(x1, act, w_down, g_final, target)


def _ffn_bwd_act(dx2b, gu, w_down):
    T = dx2b.shape[0]
    tm = min(TM_FF, T)

    def body(dx_ref, gu_ref, w_ref, dgu_ref):
        dact = _dot_nt(dx_ref[...], w_ref[...])
        gate = gu_ref[:, 0:D_FF].astype(F32)
        sg = _sig(gate)
        up = gu_ref[:, D_FF:2 * D_FF].astype(F32)
        dgu_ref[:, 0:D_FF] = (dact * up * (sg * (1.0 + gate * (1.0 - sg)))).astype(BF16)
        dgu_ref[:, D_FF:2 * D_FF] = (dact * (gate * sg)).astype(BF16)

    return pl.pallas_call(
        body, name="ffn_bwd_act", grid=(T // tm,),
        in_specs=[pl.BlockSpec((tm, D), lambda i: (i, 0)), pl.BlockSpec((tm, 2 * D_FF), lambda i: (i, 0)),
                  _resident((D_FF, D))],
        out_specs=pl.BlockSpec((tm, 2 * D_FF), lambda i: (i, 0)),
        out_shape=S((T, 2 * D_FF), BF16),
        compiler_params=_params(1),
    )(dx2b, gu, w_down)


def _ffn_bwd_in(dgu, w_gu, x1, g, dx2):
    T = x1.shape[0]
    tm = min(TM_MM, T)

    def body(dgu_ref, w_ref, x_ref, g_ref, dx2_ref, dx1_ref, dx1b_ref, dg_ref):
        @pl.when(pl.program_id(0) == 0)
        def _():
            dg_ref[...] = jnp.zeros_like(dg_ref)

        dh2 = _dot_nt(dgu_ref[:, 0:FF_SHARD], w_ref[0])
        for k in range(1, N_SHARD):
            dh2 = dh2 + _dot_nt(dgu_ref[:, FF_SHARD * k:FF_SHARD * (k + 1)], w_ref[k])
        dx, dg_rows = _rms_bwd(dh2, x_ref[...], g_ref[...])
        dg_ref[...] += _colsum(dg_rows)
        dx1 = dx2_ref[...] + dx
        dx1_ref[...] = dx1
        dx1b_ref[...] = dx1.astype(BF16)

    tile = pl.BlockSpec((tm, D), lambda i: (i, 0))
    vec = pl.BlockSpec((1, D), lambda i: (0, 0))
    return pl.pallas_call(
        body, name="ffn_bwd_in", grid=(T // tm,),
        in_specs=[pl.BlockSpec((tm, 2 * D_FF), lambda i: (i, 0)), _resident((N_SHARD, D, FF_SHARD)), tile, vec, tile],
        out_specs=[tile, tile, vec],
        out_shape=[S((T, D), F32), S((T, D), BF16), S((1, D), F32)],
        compiler_params=_params(1),
    )(dgu, w_gu, x1, g, dx2)


def _matmul_tn(a, b, tn, shard_major, name):
    T, M = a.shape
    N = b.shape[1]
    tk = min(TM_MM, T)

    def body(a_ref, b_ref, o_ref):
        @pl.when(pl.program_id(1) == 0)
        def _():
            o_ref[...] = jnp.zeros_like(o_ref)

        acc = _dot_tn(a_ref[...], b_ref[...])
        if shard_major:
            o_ref[0] += acc
        else:
            o_ref[...] += acc

    if shard_major:
        out_spec, out_shape = pl.BlockSpec((1, M, tn), lambda j, k: (j, 0, 0)), S((N // tn, M, tn), F32)
    else:
        out_spec, out_shape = pl.BlockSpec((M, tn), lambda j, k: (0, j)), S((M, N), F32)
    return pl.pallas_call(
        body, name=name, grid=(N // tn, T // tk),
        in_specs=[pl.BlockSpec((tk, M), lambda j, k: (k, 0)), pl.BlockSpec((tk, tn), lambda j, k: (k, j))],
        out_specs=out_spec, out_shape=out_shape,
        compiler_params=_params(2),
    )(a, b)


def _merge_bwd(dx1b, proj, pa, pb, wpa, wpb, wout, sides=()):
    T = dx1b.shape[0]
    tm = min(TM_MM, T)

    def body(dx_ref, ga_ref, gb_ref, pa_ref, pb_ref, wpa_ref, wpb_ref, wout_ref,
             dpa_ref, dpb_ref, dgate_ref, dya_ref, dyb_ref):
        dm = _dot_nt(dx_ref[...], wout_ref[...])
        sa = _sig(ga_ref[...])
        sb = _sig(gb_ref[...])
        dpa = (dm * sa).astype(BF16)
        dpb = (dm * sb).astype(BF16)
        dpa_ref[...] = dpa
        dpb_ref[...] = dpb
        dgate_ref[:, 0:D] = (dm * pa_ref[...] * (sa * (1.0 - sa))).astype(BF16)
        dgate_ref[:, D:2 * D] = (dm * pb_ref[...] * (sb * (1.0 - sb))).astype(BF16)
        dya_ref[...] = _dot_nt(dpa, wpa_ref[...])
        dyb_ref[...] = _dot_nt(dpb, wpb_ref[...])

    tile = pl.BlockSpec((tm, D), lambda i: (i, 0))
    w = _resident((D, D))
    return _call(
        body, name="merge_bwd", grid=(T // tm,),
        in_specs=[tile, pl.BlockSpec((tm, D), lambda i: (i, 4)), pl.BlockSpec((tm, D), lambda i: (i, 5)),
                  tile, tile, w, w, w],
        out_specs=[tile, tile, pl.BlockSpec((tm, 2 * D), lambda i: (i, 0)), tile, tile],
        out_shape=[S((T, D), BF16), S((T, D), BF16), S((T, 2 * D), BF16), S((T, D), F32), S((T, D), F32)],
        args=(dx1b, proj, proj, pa, pb, wpa, wpb, wout), sides=sides)


def _rglru_bwd(dya, proj, hseq, xc, conv_w, wa, wx, ba, bx, lam, sides=()):
    T = dya.shape[0]
    tm = min(TM_SCAN, T)
    n = T // tm
    per8 = tm // SUBLANES

    def body(dya_ref, rx_ref, rxp_ref, gate_ref, h_ref, hp_ref, xc_ref, cw_ref, wa_ref, wx_ref, ba_ref, bx_ref, lam_ref,
             dab_ref, dcw_ref, dcb_ref, dwa_ref, dwx_ref, dba_ref, dbx_ref, dlam_ref,
             hext, rext, dext, carry_a, carry_dh, a_s, b_s, h_s):
        i = pl.program_id(0)
        first_tile = i == n - 1

        @pl.when(i == 0)
        def _():
            for ref in (dcw_ref, dcb_ref, dwa_ref, dwx_ref, dba_ref, dbx_ref, dlam_ref, carry_a, carry_dh):
                ref[...] = jnp.zeros_like(ref)
            dext[tm:tm + SUBLANES, :] = jnp.zeros((SUBLANES, D), F32)

        gel, dgel = _gelu_and_grad(gate_ref[...])
        dya_v = dya_ref[...]
        hseq_v = h_ref[...]
        dgate = dya_v * hseq_v * dgel
        xcv = xc_ref[...]
        lam_v = lam_ref[...]
        sp = _softplus_neg(lam_v)
        xb, r, gi, a, m = _rg_gates(xcv, wa_ref, wx_ref, ba_ref[...], bx_ref[...], sp)

        row = lax.broadcasted_iota(jnp.int32, (tm, D), 0)
        c = jnp.where(row == tm - 1, carry_a[0:1, :], _shift_up(a, 1, 0.0))
        dH = _scan_tile(c, dya_v * gel, carry_dh[0:1, :], a_s, b_s, h_s, reverse=True)
        carry_a[...] = jnp.broadcast_to(a[0:1, :], (SUBLANES, D))
        carry_dh[...] = jnp.broadcast_to(dH[0:1, :], (SUBLANES, D))

        hext[0:SUBLANES, :] = jnp.where(first_tile, 0.0, hp_ref[...])
        hext[SUBLANES:SUBLANES + tm, :] = hseq_v
        h_prev = hext[pl.ds(SUBLANES - 1, tm), :]

        d_m = dH * (gi * xcv)
        d_la = dH * h_prev * a - d_m * (a * a) / m
        d_ia = dH * m * xcv * (gi * (1.0 - gi))
        d_ra = d_la * ((-RG_C) * sp) * (r * (1.0 - r))
        dlam_ref[...] += _colsum(d_la * ((-RG_C) * r)) * (-_sig(-lam_v))
        dba_ref[...] += _colsum(d_ra)
        dbx_ref[...] += _colsum(d_ia)
        drab = d_ra.astype(BF16)
        diab = d_ia.astype(BF16)
        dxc_cols = []
        for j in range(N_RG_BLOCK):
            sl = slice(RG_BLOCK * j, RG_BLOCK * (j + 1))
            dxc_cols.append(_dot_nt(drab[:, sl], wa_ref[j]) + _dot_nt(diab[:, sl], wx_ref[j]))
            dwa_ref[j] += _dot_tn(xb[:, sl], drab[:, sl])
            dwx_ref[j] += _dot_tn(xb[:, sl], diab[:, sl])
        dxc = dH * m * gi + jnp.concatenate(dxc_cols, axis=1)

        dcb_ref[...] += _colsum(dxc)
        dext[0:tm, :] = dxc
        rext[0:SUBLANES, :] = jnp.where(first_tile, 0.0, rxp_ref[...])
        rext[SUBLANES:SUBLANES + tm, :] = rx_ref[...]
        drx = jnp.zeros((tm, D), F32)
        for k in range(CONV_WIDTH):
            drx = drx + dext[pl.ds(CONV_WIDTH - 1 - k, tm), :] * cw_ref[k:k + 1, :]
            dcw_ref[k:k + 1, :] += _colsum(dxc * rext[pl.ds(SUBLANES - (CONV_WIDTH - 1) + k, tm), :])
        dext[tm:tm + SUBLANES, :] = dext[0:SUBLANES, :]
        dab_ref[:, 0:D] = drx.astype(BF16)
        dab_ref[:, D:2 * D] = dgate.astype(BF16)

    def rev(col):
        return lambda i: (n - 1 - i, col)

    def prev8(col):
        return lambda i: (jnp.maximum((n - 1 - i) * per8 - 1, 0), col)

    tile = pl.BlockSpec((tm, D), rev(0))
    vec = pl.BlockSpec((1, D), lambda i: (0, 0))
    blk = pl.BlockSpec((N_RG_BLOCK, RG_BLOCK, RG_BLOCK), lambda i: (0, 0, 0))
    cw = pl.BlockSpec((CONV_WIDTH, D), lambda i: (0, 0))
    return _call(
        body, name="rglru_bwd", grid=(n,),
        in_specs=[tile, pl.BlockSpec((tm, D), rev(0)), pl.BlockSpec((SUBLANES, D), prev8(0)),
                  pl.BlockSpec((tm, D), rev(1)), tile, pl.BlockSpec((SUBLANES, D), prev8(0)), tile,
                  cw, blk, blk, vec, vec, vec],
        out_specs=[pl.BlockSpec((tm, 2 * D), rev(0)), cw, vec, blk, blk, vec, vec, vec],
        out_shape=[S((T, 2 * D), BF16), S((CONV_WIDTH, D), F32), S((1, D), F32),
                   S((N_RG_BLOCK, RG_BLOCK, RG_BLOCK), F32), S((N_RG_BLOCK, RG_BLOCK, RG_BLOCK), F32),
                   S((1, D), F32), S((1, D), F32), S((1, D), F32)],
        scratch_shapes=[pltpu.VMEM((tm + SUBLANES, D), F32), pltpu.VMEM((tm + SUBLANES, D), F32),
                        pltpu.VMEM((tm + SUBLANES, D), F32), pltpu.VMEM((SUBLANES, D), F32),
                        pltpu.VMEM((SUBLANES, D), F32)] + [pltpu.VMEM((D // LANES, tm, LANES), F32)] * 3,
        args=(dya, proj, proj, proj, hseq, hseq, xc, conv_w, wa, wx, ba, bx, lam), sides=sides)


def _sgu_bwd(dyb, proj, ln_g, ln_b, w_tril, w_tril_t, bs_t, sides=()):
    T = dyb.shape[0]
    tm = min(TM_MM, T)
    n_chunk = tm // CHUNK

    def body(dyb_ref, u_ref, v_ref, g_ref, b_ref, w_ref, wt_ref, bst_ref,
             duv_ref, dw_ref, dbst_ref, dg_ref, db_ref):
        @pl.when(pl.program_id(0) == 0)
        def _():
            for ref in (dw_ref, dbst_ref, dg_ref, db_ref):
                ref[...] = jnp.zeros_like(ref)

        gu, dgu = _gelu_and_grad(u_ref[...])
        gv, dgv = _gelu_and_grad(v_ref[...])
        vhat, rstd = _layer_norm_stats(gv)
        lng = g_ref[...]
        vnb = (vhat * lng + b_ref[...]).astype(BF16)
        mixed = _sgu_mix(w_ref, vnb, bst_ref, n_chunk)
        dyb_v = dyb_ref[...]
        duv_ref[:, 0:D] = (dyb_v * mixed * dgu).astype(BF16)
        dmix = dyb_v * gu
        dmb = dmix.astype(BF16)
        keep = (lax.broadcasted_iota(jnp.int32, (CHUNK, CHUNK), 0)
                >= lax.broadcasted_iota(jnp.int32, (CHUNK, CHUNK), 1))
        dvn_cols, dbs_cols = [], []
        for g in range(N_GROUP):
            sl = slice(CHUNK * g, CHUNK * (g + 1))
            dmg = dmb[:, sl].reshape(n_chunk, CHUNK, CHUNK)
            vg = vnb[:, sl].reshape(n_chunk, CHUNK, CHUNK)
            wtb = jnp.broadcast_to(wt_ref[g][None], (n_chunk, CHUNK, CHUNK))
            dvn = lax.dot_general(wtb, dmg, (((2,), (1,)), ((0,), (0,))), preferred_element_type=F32)
            dvn_cols.append(dvn.reshape(tm, CHUNK))
            dw = lax.dot_general(dmg, vg, (((2,), (2,)), ((0,), (0,))), preferred_element_type=F32)
            dw_ref[g] += jnp.where(keep, jnp.sum(dw, axis=0), 0.0)
            rows = jnp.sum(dmix[:, sl], axis=1, keepdims=True)
            dbs_cols.append(jnp.sum(rows.reshape(n_chunk, CHUNK, 1), axis=0))
        dbst_ref[...] += jnp.concatenate(dbs_cols, axis=1)
        dvn = jnp.concatenate(dvn_cols, axis=1)
        dg_ref[...] += _colsum(dvn * vhat)
        db_ref[...] += _colsum(dvn)
        dvhat = dvn * lng
        dgv_in = rstd * (dvhat - jnp.mean(dvhat, axis=-1, keepdims=True)
                         - vhat * jnp.mean(dvhat * vhat, axis=-1, keepdims=True))
        duv_ref[:, D:2 * D] = (dgv_in * dgv).astype(BF16)

    tile = pl.BlockSpec((tm, D), lambda i: (i, 0))
    vec = pl.BlockSpec((1, D), lambda i: (0, 0))
    wsp = pl.BlockSpec((N_GROUP, CHUNK, CHUNK), lambda i: (0, 0, 0))
    bsp = pl.BlockSpec((CHUNK, N_GROUP), lambda i: (0, 0))
    return _call(
        body, name="sgu_bwd", grid=(T // tm,),
        in_specs=[tile, pl.BlockSpec((tm, D), lambda i: (i, 2)), pl.BlockSpec((tm, D), lambda i: (i, 3)),
                  vec, vec, wsp, wsp, bsp],
        out_specs=[pl.BlockSpec((tm, 2 * D), lambda i: (i, 0)), wsp, bsp, vec, vec],
        out_shape=[S((T, 2 * D), BF16), S((N_GROUP, CHUNK, CHUNK), F32), S((CHUNK, N_GROUP), F32),
                   S((1, D), F32), S((1, D), F32)],
        args=(dyb, proj, proj, ln_g, ln_b, w_tril, w_tril_t, bs_t), sides=sides)


def _dproj_columns(k, da, db, dg):
    if k == 0:
        return da[:, 0:IN_SHARD]
    if k == 1:
        return jnp.concatenate([da[:, IN_SHARD:2 * D], db[:, 0:D]], axis=1)
    if k == 2:
        return jnp.concatenate([db[:, D:2 * D], dg[:, 0:D // 2]], axis=1)
    return dg[:, D // 2:2 * D]


def _inproj_bwd(da, db, dg, w_in, x, g, dx1, sides=()):
    T = x.shape[0]
    tm = min(TM_MM, T)

    def body(da_ref, db_ref, dg_ref, w_ref, x_ref, g_ref, dx1_ref, dx_ref, dgm_ref):
        @pl.when(pl.program_id(0) == 0)
        def _():
            dgm_ref[...] = jnp.zeros_like(dgm_ref)

        dav, dbv, dgv = da_ref[...], db_ref[...], dg_ref[...]
        dh = _dot_nt(_dproj_columns(0, dav, dbv, dgv), w_ref[0])
        for k in range(1, N_SHARD):
            dh = dh + _dot_nt(_dproj_columns(k, dav, dbv, dgv), w_ref[k])
        dx, dg_rows = _rms_bwd(dh, x_ref[...], g_ref[...])
        dgm_ref[...] += _colsum(dg_rows)
        dx_ref[...] = dx1_ref[...] + dx

    tile = pl.BlockSpec((tm, D), lambda i: (i, 0))
    wide = pl.BlockSpec((tm, 2 * D), lambda i: (i, 0))
    vec = pl.BlockSpec((1, D), lambda i: (0, 0))
    return _call(
        body, name="inproj_bwd", grid=(T // tm,),
        in_specs=[wide, wide, wide, _resident((N_SHARD, D, IN_SHARD)), tile, vec, tile],
        out_specs=[tile, vec],
        out_shape=[S((T, D), F32), S((1, D), F32)],
        args=(da, db, dg, w_in, x, g, dx1), sides=sides)


def _inproj_wgrad(h, da, db, dg, sides=()):
    T = h.shape[0]
    tk = min(TM_MM, T)

    def body(h_ref, da_ref, db_ref, dg_ref, o_ref):
        @pl.when(pl.program_id(1) == 0)
        def _():
            o_ref[...] = jnp.zeros_like(o_ref)

        for k in range(N_SHARD):
            @pl.when(pl.program_id(0) == k)
            def _(k=k):
                o_ref[0] += _dot_tn(h_ref[...], _dproj_columns(k, da_ref[...], db_ref[...], dg_ref[...]))

    wide = pl.BlockSpec((tk, 2 * D), lambda j, k: (k, 0))
    return _call(
        body, name="inproj_wgrad", grid=(N_SHARD, T // tk),
        in_specs=[pl.BlockSpec((tk, D), lambda j, k: (k, 0)), wide, wide, wide],
        out_specs=[pl.BlockSpec((1, D, IN_SHARD), lambda j, k: (j, 0, 0))],
        out_shape=[S((N_SHARD, D, IN_SHARD), F32)],
        args=(h, da, db, dg), sides=sides)


def _blockdiag(w):
    hd = w.shape[-1]
    per = RG_BLOCK // hd
    w4 = w.reshape(N_RG_BLOCK, per, hd, hd)
    out = jnp.zeros((N_RG_BLOCK, per, hd, per, hd), w.dtype)
    for h in range(per):
        out = out.at[:, h, :, h, :].set(w4[:, h])
    return out.reshape(N_RG_BLOCK, RG_BLOCK, RG_BLOCK)


def _blockdiag_heads(g, hd):
    per = RG_BLOCK // hd
    g5 = g.reshape(N_RG_BLOCK, per, hd, per, hd)
    return jnp.stack([g5[:, h, :, h, :] for h in range(per)], axis=1).reshape(N_RG_BLOCK * per, hd, hd)


def _row_tile(rows):
    for t in range(256, 0, -SUBLANES):
        if rows % t == 0:
            return t
    raise ValueError(rows)


def _add_halves(core, g, theirs, name):
    _, r, cols = g.shape
    half = r // 2
    tr = _row_tile(half)
    nb = half // tr

    def body(core_ref, g_ref, t_ref, o32_ref, o16_ref):
        s = g_ref[...] + t_ref[...]
        o32_ref[...] = s
        o16_ref[...] = s.astype(BF16)

    blk = pl.BlockSpec((1, tr, cols), lambda s, i, core_ref: (s, i, 0))
    gs = pltpu.PrefetchScalarGridSpec(
        num_scalar_prefetch=1, grid=(N_SHARD, nb),
        in_specs=[pl.BlockSpec((1, tr, cols), lambda s, i, core_ref: (s, core_ref[0] * nb + i, 0)), blk],
        out_specs=[blk, blk])
    return pl.pallas_call(
        body, name=name, grid_spec=gs,
        out_shape=[S((N_SHARD, half, cols), F32), S((N_SHARD, half, cols), BF16)],
        compiler_params=_params(2),
    )(core, g, theirs)


def _sum_shards(chip, own, others, name):
    _, half, cols = own.shape
    tr = _row_tile(half)

    def body(chip_ref, own_ref, oth_ref, o_ref):
        acc = own_ref[0]
        for j in range(3):
            acc = acc + oth_ref[j].astype(F32)
        o_ref[...] = acc

    gs = pltpu.PrefetchScalarGridSpec(
        num_scalar_prefetch=1, grid=(half // tr,),
        in_specs=[pl.BlockSpec((1, tr, cols), lambda i, chip_ref: (chip_ref[0], i, 0)),
                  pl.BlockSpec((3, tr, cols), lambda i, chip_ref: (0, i, 0))],
        out_specs=pl.BlockSpec((tr, cols), lambda i, chip_ref: (i, 0)))
    return pl.pallas_call(
        body, name=name, grid_spec=gs, out_shape=S((half, cols), F32), compiler_params=_params(1),
    )(chip, own, others)


def _adamw(w, g, m, v):
    m = ADAM_B1 * m + (1.0 - ADAM_B1) * g
    v = ADAM_B2 * v + (1.0 - ADAM_B2) * (g * g)
    m_hat = m / (1.0 - ADAM_B1 ** ADAM_STEP)
    v_hat = v / (1.0 - ADAM_B2 ** ADAM_STEP)
    delta = -ADAM_LR * (m_hat / (jnp.sqrt(v_hat) + ADAM_EPS) + ADAM_WD * w)
    return delta, m, v


def _adamw_shard(core, mine, theirs, w, m, v, name):
    r, cols = w.shape
    half = r // 2
    tr = _row_tile(half)
    nb = half // tr

    def body(core_ref, mine_ref, theirs_ref, w_ref, m_ref, v_ref, g_ref, d_ref, mo_ref, vo_ref):
        g = jnp.where(pl.program_id(0) == core_ref[0], mine_ref[...], theirs_ref[...])
        g_ref[...] = g
        d_ref[...], mo_ref[...], vo_ref[...] = _adamw(w_ref[...], g, m_ref[...], v_ref[...])

    hblk = pl.BlockSpec((tr, cols), lambda h, i, core_ref: (i, 0))
    blk = pl.BlockSpec((tr, cols), lambda h, i, core_ref: (h * nb + i, 0))
    gs = pltpu.PrefetchScalarGridSpec(num_scalar_prefetch=1, grid=(2, nb),
                                      in_specs=[hblk, hblk, blk, blk, blk], out_specs=[blk] * 4)
    return pl.pallas_call(
        body, name=name, grid_spec=gs, out_shape=[S((r, cols), F32)] * 4, compiler_params=_params(2),
    )(core, mine, theirs, w, m, v)


def _adamw_whole(w, g, m, v, name):
    def body(w_ref, g_ref, m_ref, v_ref, d_ref, mo_ref, vo_ref):
        d_ref[...], mo_ref[...], vo_ref[...] = _adamw(w_ref[...], g_ref[...], m_ref[...], v_ref[...])

    return pl.pallas_call(body, name=name, out_shape=[S(w.shape, F32)] * 3)(w, g, m, v)


def _small_sum_adamw(parts, w, m, v):
    rows = w.shape[0]

    def body(p_ref, w_ref, m_ref, v_ref, g_ref, d_ref, mo_ref, vo_ref):
        total = p_ref[0]
        for k in range(1, N_DEVICE):
            total = total + p_ref[k]
        g_ref[...] = total
        d_ref[...], mo_ref[...], vo_ref[...] = _adamw(w_ref[...], total, m_ref[...], v_ref[...])

    return pl.pallas_call(
        body, name="small_sum_adamw", out_shape=[S((rows, D), F32)] * 4,
        compiler_params=pltpu.CompilerParams(vmem_limit_bytes=VMEM_LIMIT),
    )(parts, w, m, v)


_BIG = ("w_in", "w_proj_a", "w_proj_b", "w_out", "w_gate_up", "w_down")
_SMALL = ("norm_mix_g", "conv_b", "rg_wa", "rg_ba", "rg_wx", "rg_bx", "rg_lambda", "sgu_ln_g", "sgu_ln_b",
          "sgu_ws", "sgu_bs", "norm_ffn_g", "norm_final_g")
_WEIGHTS = ("norm_mix_g", "w_in", "conv_w", "conv_b", "rg_wa", "rg_ba", "rg_wx", "rg_bx", "rg_lambda", "sgu_ln_g",
            "sgu_ln_b", "sgu_ws", "sgu_bs", "w_proj_a", "w_proj_b", "w_out", "norm_ffn_g", "w_gate_up", "w_down",
            "norm_final_g")


def _pack(arrays):
    rows = [a.reshape(-1, D) for a in arrays]
    pad = -sum(r.shape[0] for r in rows) % SUBLANES
    return jnp.concatenate(rows + [jnp.zeros((pad, D), F32)], axis=0)


def _unpack(packed, like):
    out, row = [], 0
    for a in like:
        n = a.size // D
        out.append(packed[row:row + n].reshape(a.shape))
        row += n
    return out


def kernel(x, norm_mix_g, w_in, conv_w, conv_b, rg_wa, rg_ba, rg_wx, rg_bx, rg_lambda, sgu_ln_g, sgu_ln_b, sgu_ws, sgu_bs, w_proj_a, w_proj_b, w_out, norm_ffn_g, w_gate_up, w_down, norm_final_g, loss_target, m_norm_mix_g, m_w_in, m_conv_w, m_conv_b, m_rg_wa, m_rg_ba, m_rg_wx, m_rg_bx, m_rg_lambda, m_sgu_ln_g, m_sgu_ln_b, m_sgu_ws, m_sgu_bs, m_w_proj_a, m_w_proj_b, m_w_out, m_norm_ffn_g, m_w_gate_up, m_w_down, m_norm_final_g, v_norm_mix_g, v_w_in, v_conv_w, v_conv_b, v_rg_wa, v_rg_ba, v_rg_wx, v_rg_bx, v_rg_lambda, v_sgu_ln_g, v_sgu_ln_b, v_sgu_ws, v_sgu_bs, v_w_proj_a, v_w_proj_b, v_w_out, v_norm_ffn_g, v_w_gate_up, v_w_down, v_norm_final_g):
    args = dict(locals())
    w = {n: args[n] for n in _WEIGHTS}
    mom = {n: args["m_" + n] for n in _WEIGHTS}
    var = {n: args["v_" + n] for n in _WEIGHTS}
    xi, yi, ci = _position()
    core = ci.astype(jnp.int32).reshape(1)
    chip = (2 * xi + yi).astype(jnp.int32).reshape(1)

    bf = {n: w[n][0].astype(BF16) for n in _BIG}
    sm = {n: w[n][0] if w[n].size > D else w[n].reshape(1, D) for n in _SMALL}
    hd = sm["rg_wa"].shape[-1]
    wa = _blockdiag(sm["rg_wa"]).astype(BF16)
    wx = _blockdiag(sm["rg_wx"]).astype(BF16)
    w_tril = jnp.where(jnp.tril(jnp.ones((CHUNK, CHUNK), bool))[None], sm["sgu_ws"], 0.0).astype(BF16)
    w_tril_t = jnp.swapaxes(w_tril, 1, 2)
    bs_t = sm["sgu_bs"].reshape(N_GROUP, CHUNK).T
    lam, ln_g, ln_b = sm["rg_lambda"], sm["sgu_ln_g"], sm["sgu_ln_b"]
    x0, target = x[0], loss_target[0]

    def shard_major(g):
        return g.reshape(N_SHARD, g.shape[0] // N_SHARD, g.shape[1])

    def chip_sums(names, grads, theirs):
        return [_add_halves(core, g, t, "add_halves_" + n) for n, g, t in zip(names, grads, theirs)]

    def my_halves(names, sums, arrived):
        return [_sum_shards(chip, p32, a, "sum_shards_" + n) for n, (p32, _), a in zip(names, sums, arrived)]

    (w_in_a,), (conv_a,) = _comm_only([_gather_half_side([bf["w_in"]]), _gather_side([w["conv_w"][0]])], "gather_w_in")
    conv_cols = conv_a.shape[-1]
    conv_full = jnp.swapaxes(conv_a, 0, 1).reshape(CONV_WIDTH, D)
    (proj, h), ((w_pa_a, w_pb_a, w_out_a),) = _inproj(
        x0, sm["norm_mix_g"], w_in_a, sides=[_gather_half_side([bf["w_proj_a"], bf["w_proj_b"], bf["w_out"]])])
    (ya, xc, hseq), ((w_gu_a, w_down_a),) = _rglru_fwd(
        proj, conv_full, sm["conv_b"], wa, wx, sm["rg_ba"], sm["rg_bx"], lam,
        sides=[_gather_half_side([bf["w_gate_up"], bf["w_down"]])])
    wpa, wpb, wout, wdown = w_pa_a.reshape(D, D), w_pb_a.reshape(D, D), w_out_a.reshape(D, D), w_down_a.reshape(D_FF, D)
    yb = _sgu_fwd(proj, ln_g, ln_b, w_tril, bs_t)
    pa, pb, mb, x1 = _merge_fwd(x0, proj, ya, yb, wpa, wpb, wout)
    h2, gu, act = _ffn_up(x1, sm["norm_ffn_g"], w_gu_a)
    dx2, dx2b, loss, d_final_g = _ffn_down_loss(x1, act, wdown, sm["norm_final_g"], target)

    dgu = _ffn_bwd_act(dx2b, gu, wdown)
    dx1, dx1b, d_ffn_g = _ffn_bwd_in(dgu, w_gu_a, x1, sm["norm_ffn_g"], dx2)
    ffn = ("w_gate_up", "w_down")
    g_ffn = [_matmul_tn(h2, dgu, FF_SHARD, True, "wgrad_gate_up"),
             shard_major(_matmul_tn(act, dx2b, D // 2, False, "wgrad_down"))]
    (dpa, dpb, dgate, dya, dyb), (theirs_ffn,) = _merge_bwd(
        dx1b, proj, pa, pb, wpa, wpb, wout, sides=[_halves_side(g_ffn)])
    sums_ffn = chip_sums(ffn, g_ffn, theirs_ffn)
    mix = ("w_proj_a", "w_proj_b", "w_out")
    g_mix = [shard_major(_matmul_tn(ya, dpa, D, False, "wgrad_proj_a")),
             shard_major(_matmul_tn(yb, dpb, D, False, "wgrad_proj_b")),
             shard_major(_matmul_tn(mb, dx1b, D, False, "wgrad_out"))]
    (dab, d_cw, d_cb, d_wa, d_wx, d_ba, d_bx, d_lam), (arrived_ffn, theirs_mix) = _rglru_bwd(
        dya, proj, hseq, xc, conv_full, wa, wx, sm["rg_ba"], sm["rg_bx"], lam,
        sides=[_scatter_side([p16 for _, p16 in sums_ffn]), _halves_side(g_mix)])
    mine_ffn = my_halves(ffn, sums_ffn, arrived_ffn)
    sums_mix = chip_sums(mix, g_mix, theirs_mix)
    (duv, d_ws, d_bst, d_lng, d_lnb), (other_ffn, arrived_mix) = _sgu_bwd(
        dyb, proj, ln_g, ln_b, w_tril, w_tril_t, bs_t,
        sides=[_swap_side(mine_ffn), _scatter_side([p16 for _, p16 in sums_mix])])
    mine_mix = my_halves(mix, sums_mix, arrived_mix)
    (grad_x, d_mix_g), _ = _inproj_bwd(dab, duv, dgate, w_in_a, x0, sm["norm_mix_g"], dx1)
    small_grads = {
        "norm_mix_g": d_mix_g, "conv_b": d_cb, "rg_wa": _blockdiag_heads(d_wa, hd), "rg_ba": d_ba,
        "rg_wx": _blockdiag_heads(d_wx, hd), "rg_bx": d_bx, "rg_lambda": d_lam, "sgu_ln_g": d_lng,
        "sgu_ln_b": d_lnb, "sgu_ws": d_ws, "sgu_bs": d_bst.T, "norm_ffn_g": d_ffn_g, "norm_final_g": d_final_g,
    }
    loss_row = jnp.pad(loss, ((0, 0), (0, D - 1)))
    (g_in,), ((small_parts,), other_mix) = _inproj_wgrad(
        h, dab, duv, dgate,
        sides=[_everyone_side(_pack([small_grads[n] for n in _SMALL] + [d_cw, loss_row])), _swap_side(mine_mix)])
    theirs_in = _comm_only([_halves_side([g_in])], "halves_w_in")[0]
    sums_in = chip_sums(("w_in",), [g_in], theirs_in)
    arrived_in = _comm_only([_scatter_side([p16 for _, p16 in sums_in])], "scatter_w_in")[0]
    mine_in = my_halves(("w_in",), sums_in, arrived_in)
    other_in = _comm_only([_swap_side(mine_in)], "swap_w_in")[0]

    out = {}
    for n, gm, go in zip(ffn + mix + ("w_in",), mine_ffn + mine_mix + mine_in, other_ffn + other_mix + other_in):
        g, d, mo, vo = _adamw_shard(core, gm, go, w[n][0], mom[n][0], var[n][0], "adamw_" + n)
        out[n] = tuple(a[None] for a in (g, d, mo, vo))
    zeros_conv = jnp.zeros((CONV_WIDTH, D), F32)
    g_pack, d_pack, m_pack, v_pack = _small_sum_adamw(
        small_parts,
        _pack([w[n] for n in _SMALL] + [zeros_conv]),
        _pack([mom[n] for n in _SMALL] + [zeros_conv]),
        _pack([var[n] for n in _SMALL] + [zeros_conv]))
    like = [w[n] for n in _SMALL] + [zeros_conv, loss_row]
    unpacked = [_unpack(p, like) for p in (g_pack, d_pack, m_pack, v_pack)]
    for k, n in enumerate(_SMALL):
        out[n] = tuple(u[k] for u in unpacked)
    conv_g = lax.dynamic_slice_in_dim(unpacked[0][-2], chip[0] * conv_cols, conv_cols, axis=1)
    d, mo, vo = _adamw_whole(w["conv_w"][0], conv_g, mom["conv_w"][0], var["conv_w"][0], "adamw_conv_w")
    out["conv_w"] = tuple(a[None] for a in (conv_g, d, mo, vo))

    return (unpacked[0][-1][0, 0], grad_x[None], *[out[n][0] for n in _WEIGHTS], *[out[n][1] for n in _WEIGHTS],
            *[out[n][2] for n in _WEIGHTS], *[out[n][3] for n in _WEIGHTS])
```

```python
import functools

import jax
import jax.numpy as jnp
from jax import lax
from jax.experimental import pallas as pl
from jax.experimental.pallas import tpu as pltpu

F32 = jnp.float32
BF16 = jnp.bfloat16
S = jax.ShapeDtypeStruct

D = 1024
N_SHARD = 4
IN_COLS = 6 * D
IN_SHARD = IN_COLS // N_SHARD
D_FF = 2816
FF_SHARD = 2 * D_FF // N_SHARD
RG_BLOCK = 256
N_RG_BLOCK = D // RG_BLOCK
CHUNK = 128
N_GROUP = 8
CONV_WIDTH = 4
RG_C = 8.0
EPS = 1e-6
ADAM_LR, ADAM_B1, ADAM_B2, ADAM_EPS, ADAM_WD, ADAM_STEP = 0.001, 0.9, 0.999, 1e-08, 0.01, 10

V7X_VMEM_BYTES = 64 * 1024 * 1024
VMEM_LIMIT = V7X_VMEM_BYTES * 3 // 4
SUBLANES = 8
MESH = pl.DeviceIdType.MESH

TM_MM = 512
TM_SCAN = 256
TM_FF = 256
TK_WGRAD = 1024


def _params(n_axes):
    return pltpu.CompilerParams(dimension_semantics=("arbitrary",) * n_axes, vmem_limit_bytes=VMEM_LIMIT)


def _resident(shape):
    nd = len(shape)
    return pl.BlockSpec(shape, lambda *_: (0,) * nd, pipeline_mode=pl.Buffered(1))


def _sig(x):
    return 1.0 / (1.0 + jnp.exp(-x))


_GELU_K2 = 2.0 * 0.7978845608028654
_GELU_C = 0.044715


def _gelu(x):
    return x * _sig(x * (_GELU_K2 + (_GELU_K2 * _GELU_C) * (x * x)))


def _gelu_and_grad(x):
    x2 = x * x
    s = _sig(x * (_GELU_K2 + (_GELU_K2 * _GELU_C) * x2))
    g = x * s
    return g, s + g * (1.0 - s) * (_GELU_K2 + (3.0 * _GELU_K2 * _GELU_C) * x2)


_EXPM1_SERIES = tuple(1.0 / f for f in (5040.0, 720.0, 120.0, 24.0, 6.0, 2.0, 1.0))


def _one_minus_exp(x):
    p = _EXPM1_SERIES[0]
    for coef in _EXPM1_SERIES[1:]:
        p = p * x + coef
    return jnp.where(x > -0.125, -x * p, 1.0 - jnp.exp(x))


def _softplus_neg(lam):
    z = -lam
    e = jnp.exp(-jnp.abs(z))
    u = 1.0 + e
    log1p = jnp.where(u == 1.0, e, jnp.log(u) * e / (u - 1.0))
    return jnp.maximum(z, 0.0) + log1p


def _rms_stats(x):
    return lax.rsqrt(jnp.mean(x * x, axis=-1, keepdims=True) + EPS)


def _rms_bwd(dy, x, g):
    rstd = _rms_stats(x)
    xhat = x * rstd
    dxhat = dy * g
    dx = rstd * (dxhat - xhat * jnp.mean(dxhat * xhat, axis=-1, keepdims=True))
    return dx, dy * xhat


def _colsum(x):
    return jnp.sum(x, axis=0, keepdims=True)


def _shift_down(x, d, fill):
    n = x.shape[0]
    if d % SUBLANES == 0:
        return jnp.concatenate([jnp.full((d, x.shape[1]), fill, x.dtype), x[:n - d]], axis=0)
    row = lax.broadcasted_iota(jnp.int32, x.shape, 0)
    return jnp.where(row < d, fill, pltpu.roll(x, d, 0))


def _shift_up(x, d, fill):
    n = x.shape[0]
    if d % SUBLANES == 0:
        return jnp.concatenate([x[d:], jnp.full((d, x.shape[1]), fill, x.dtype)], axis=0)
    row = lax.broadcasted_iota(jnp.int32, x.shape, 0)
    return jnp.where(row >= n - d, fill, pltpu.roll(x, n - d, 0))


def _scan(a, b, shift):
    d = 1
    while d < a.shape[0]:
        b = a * shift(b, d, 0.0) + b
        a = a * shift(a, d, 1.0)
        d *= 2
    return a, b


LANES = 128


def _scan_tile(a, b, outside, a_s, b_s, h_s, reverse):
    tm = a.shape[0]
    groups = tm // SUBLANES
    order = list(range(SUBLANES - 1, -1, -1) if reverse else range(SUBLANES))
    shift = _shift_up if reverse else _shift_down
    edge = groups - 1 if reverse else 0
    for j in range(D // LANES):
        a_s[j] = a[:, LANES * j:LANES * (j + 1)]
        b_s[j] = b[:, LANES * j:LANES * (j + 1)]
    for j in range(D // LANES):
        def slab(ref, k):
            return ref[j, pl.ds(k, groups, stride=SUBLANES), :]

        ga, gb = slab(a_s, order[0]), slab(b_s, order[0])
        for k in order[1:]:
            ak = slab(a_s, k)
            gb = ak * gb + slab(b_s, k)
            ga = ak * ga
        ga, gb = _scan(ga, gb, shift)
        h_out = outside[:, LANES * j:LANES * (j + 1)]
        group_end = ga * h_out + gb
        row = lax.broadcasted_iota(jnp.int32, (groups, LANES), 0)
        h = jnp.where(row == edge, h_out, shift(group_end, 1, 0.0))
        for k in order:
            h = slab(a_s, k) * h + slab(b_s, k)
            h_s[j, pl.ds(k, groups, stride=SUBLANES), :] = h
    return jnp.concatenate([h_s[j] for j in range(D // LANES)], axis=1)


def _dot(a, b):
    return jnp.dot(a, b, preferred_element_type=F32)


def _dot_nt(a, b):
    return lax.dot_general(a, b, (((1,), (1,)), ((), ())), preferred_element_type=F32)


def _dot_tn(a, b):
    return lax.dot_general(a, b, (((0,), (0,)), ((), ())), preferred_element_type=F32)


_ANY = pl.BlockSpec(memory_space=pl.ANY)


def _position():
    return lax.axis_index("x"), lax.axis_index("y"), lax.axis_index("c")


def _other_chips(x, y):
    return [(1 - x, y), (x, 1 - y), (1 - x, 1 - y)]


class _Side:
    def __init__(self, inputs, out_shapes, n_sems, make):
        self.inputs, self.out_shapes, self.n_sems, self.make = list(inputs), list(out_shapes), n_sems, make


MID_STEP = 0.625


def _call(body, *, name, grid, in_specs, out_specs, out_shape, args, scratch_shapes=(), sides=(), aliases=None):
    n_in, n_out, n_scr = len(in_specs), len(out_specs), len(scratch_shapes)
    side_in = [len(s.inputs) for s in sides]
    side_out = [len(s.out_shapes) for s in sides]

    def wrapped(*refs):
        refs = list(refs)
        take = lambda k: [refs.pop(0) for _ in range(k)]
        ins = take(n_in)
        sins = [take(k) for k in side_in]
        outs = take(n_out)
        souts = [take(k) for k in side_out]
        scr = take(n_scr)
        sems = [take(3) for _ in sides]
        def run(phase):
            for s, si, so, sem in zip(sides, sins, souts, sems):
                for thunk in s.make(si, so, *sem)[phase]:
                    thunk()

        if sides:
            n_steps = functools.reduce(lambda a, b: a * b, grid)
            step = functools.reduce(lambda a, b: a + b, [
                pl.program_id(a) * functools.reduce(lambda p, q: p * q, grid[a + 1:], 1) for a in range(len(grid))])
            pl.when(step == 0)(lambda: run(0))
        body(*ins, *outs, *scr)
        if sides:
            pl.when(step == int(MID_STEP * (n_steps - 1)))(lambda: run(1))
            pl.when(step == n_steps - 1)(lambda: run(2))

    res = pl.pallas_call(
        wrapped, name=name, grid=grid,
        in_specs=list(in_specs) + [_ANY] * sum(side_in),
        out_specs=list(out_specs) + [_ANY] * sum(side_out),
        out_shape=list(out_shape) + [o for s in sides for o in s.out_shapes],
        scratch_shapes=list(scratch_shapes) + [pltpu.SemaphoreType.DMA((s.n_sems,)) for s in sides for _ in range(3)],
        input_output_aliases=aliases or {},
        compiler_params=_params(len(grid)),
    )(*args, *[a for s in sides for a in s.inputs])
    main, rest, per_side = list(res[:n_out]), list(res[n_out:]), []
    for k in side_out:
        per_side.append(rest[:k])
        rest = rest[k:]
    return main, per_side


def _comm_only(sides, name):
    def body():
        pass

    return _call(body, name=name, grid=(1,), in_specs=[], out_specs=[], out_shape=[], args=[], sides=sides)[1]


def _remote(src, dst, send, recv, k, device):
    return pltpu.make_async_remote_copy(src_ref=src, dst_ref=dst, send_sem=send.at[k], recv_sem=recv.at[k],
                                        device_id=device, device_id_type=MESH)


def _both_ways(copy, keys):
    return [lambda k=k: copy(k).start() for k in keys], [], [lambda k=k: copy(k).wait() for k in keys]


def _gather_side(shards):
    n = len(shards)

    def make(ins, outs, send, recv, local):
        x, y, c = _position()
        mine = 2 * x + y
        chips = _other_chips(x, y)
        pairs = [(w, j) for w in range(n) for j in range(3)]

        def own(w):
            return pltpu.make_async_copy(ins[w], outs[w].at[mine], local.at[w])

        def push(w, j):
            return _remote(ins[w], outs[w].at[mine], send, recv, 3 * w + j, (*chips[j], c))

        def arrival(w, j):
            px, py = chips[j]
            return _remote(ins[w], outs[w].at[2 * px + py], send, recv, 3 * w + j, (px, py, c))

        starts = [lambda w=w: own(w).start() for w in range(n)] + [lambda w=w, j=j: push(w, j).start() for w, j in pairs]
        waits = ([lambda w=w, j=j: arrival(w, j).wait_recv() for w, j in pairs]
                 + [lambda w=w, j=j: push(w, j).wait_send() for w, j in pairs]
                 + [lambda w=w: own(w).wait() for w in range(n)])
        return starts, [], waits

    return _Side(shards, [S((N_SHARD,) + s.shape, s.dtype) for s in shards], 3 * n, make)


def _gather_half_side(shards):
    n = len(shards)

    def make(ins, outs, send, recv, local):
        x, y, c = _position()
        mine = 2 * x + y
        chips = _other_chips(x, y)
        pairs = [(w, j) for w in range(n) for j in range(3)]

        def rows(w, core):
            half = ins[w].shape[0] // 2
            return pl.ds(core * half, half)

        def own(w):
            return pltpu.make_async_copy(ins[w], outs[w].at[mine], local.at[w])

        def push(w, j):
            return _remote(ins[w].at[rows(w, c), :], outs[w].at[mine, rows(w, c), :], send, recv, 3 * w + j,
                           (*chips[j], c))

        def landed(w, j, core):
            px, py = chips[j]
            return outs[w].at[2 * px + py, rows(w, core), :]

        def arrival(w, j):
            return _remote(ins[w].at[rows(w, c), :], landed(w, j, c), send, recv, 3 * w + j, (*chips[j], c))

        def passed(w, j, core):
            return _remote(landed(w, j, core), landed(w, j, core), send, recv, 3 * n + 3 * w + j, (x, y, 1 - c))

        starts = [lambda w=w: own(w).start() for w in range(n)] + [lambda w=w, j=j: push(w, j).start() for w, j in pairs]
        mids = [t for w, j in pairs for t in (lambda w=w, j=j: arrival(w, j).wait_recv(),
                                              lambda w=w, j=j: passed(w, j, c).start())]
        waits = ([lambda w=w, j=j: passed(w, j, 1 - c).wait_recv() for w, j in pairs]
                 + [lambda w=w, j=j: passed(w, j, c).wait_send() for w, j in pairs]
                 + [lambda w=w, j=j: push(w, j).wait_send() for w, j in pairs]
                 + [lambda w=w: own(w).wait() for w in range(n)])
        return starts, mids, waits

    return _Side(shards, [S((N_SHARD,) + s.shape, s.dtype) for s in shards], 6 * n, make)


def _halves_side(grads):
    n = len(grads)

    def make(ins, outs, send, recv, local):
        x, y, c = _position()

        def copy(w):
            half = ins[w].shape[1] // 2
            return _remote(ins[w].at[:, pl.ds((1 - c) * half, half), :], outs[w], send, recv, w, (x, y, 1 - c))

        return _both_ways(copy, range(n))

    return _Side(grads, [S((N_SHARD, g.shape[1] // 2, g.shape[2]), F32) for g in grads], n, make)


def _scatter_side(partials):
    n = len(partials)

    def make(ins, outs, send, recv, local):
        x, y, c = _position()
        chips = _other_chips(x, y)

        def copy(k):
            w, j = divmod(k, 3)
            px, py = chips[j]
            return _remote(ins[w].at[2 * px + py], outs[w].at[j], send, recv, k, (px, py, c))

        return _both_ways(copy, range(3 * n))

    return _Side(partials, [S((3,) + p.shape[1:], p.dtype) for p in partials], 3 * n, make)


def _swap_side(halves):
    n = len(halves)

    def make(ins, outs, send, recv, local):
        x, y, c = _position()
        return _both_ways(lambda w: _remote(ins[w], outs[w], send, recv, w, (x, y, 1 - c)), range(n))

    return _Side(halves, [S(h.shape, h.dtype) for h in halves], n, make)


N_DEVICE = 8


def _everyone_side(arrays):
    n = len(arrays)
    peers = N_DEVICE - 1

    def make(ins, outs, send, recv, local):
        x, y, c = _position()
        mine = 4 * x + 2 * y + c
        pairs = [(w, k) for w in range(n) for k in range(1, N_DEVICE)]

        def peer(k):
            return (1 - x if k & 4 else x, 1 - y if k & 2 else y, 1 - c if k & 1 else c)

        def own(w):
            return pltpu.make_async_copy(ins[w], outs[w].at[mine], local.at[w])

        def push(w, k):
            return _remote(ins[w], outs[w].at[mine], send, recv, peers * w + k - 1, peer(k))

        def arrival(w, k):
            px, py, pc = peer(k)
            return _remote(ins[w], outs[w].at[4 * px + 2 * py + pc], send, recv, peers * w + k - 1, (px, py, pc))

        starts = [lambda w=w: own(w).start() for w in range(n)] + [lambda w=w, k=k: push(w, k).start() for w, k in pairs]
        waits = ([lambda w=w, k=k: arrival(w, k).wait_recv() for w, k in pairs]
                 + [lambda w=w, k=k: push(w, k).wait_send() for w, k in pairs]
                 + [lambda w=w: own(w).wait() for w in range(n)])
        return starts, [], waits

    return _Side(arrays, [S((N_DEVICE,) + a.shape, a.dtype) for a in arrays], peers * n, make)


def _inproj(x, g, w_in, sides=()):
    T = x.shape[0]
    tm = min(TM_FF, T)

    def body(x_ref, g_ref, w_ref, proj_ref, h_ref):
        xv = x_ref[...]
        h = (xv * _rms_stats(xv) * g_ref[...]).astype(BF16)
        h_ref[...] = h
        for k in range(N_SHARD):
            proj_ref[:, IN_SHARD * k:IN_SHARD * (k + 1)] = _dot(h, w_ref[k])

    return _call(
        body, name="inproj", grid=(T // tm,),
        in_specs=[pl.BlockSpec((tm, D), lambda i: (i, 0)), pl.BlockSpec((1, D), lambda i: (0, 0)),
                  _resident((N_SHARD, D, IN_SHARD))],
        out_specs=[pl.BlockSpec((tm, IN_COLS), lambda i: (i, 0)), pl.BlockSpec((tm, D), lambda i: (i, 0))],
        out_shape=[S((T, IN_COLS), F32), S((T, D), BF16)],
        args=(x, g, w_in), sides=sides)


def _rg_gates(xc, wa_ref, wx_ref, ba, bx, sp):
    xb = xc.astype(BF16)
    blocks = [xb[:, RG_BLOCK * j:RG_BLOCK * (j + 1)] for j in range(N_RG_BLOCK)]
    r = _sig(jnp.concatenate([_dot(blocks[j], wa_ref[j]) for j in range(N_RG_BLOCK)], axis=1) + ba)
    gi = _sig(jnp.concatenate([_dot(blocks[j], wx_ref[j]) for j in range(N_RG_BLOCK)], axis=1) + bx)
    log_a = (-RG_C) * r * sp
    a = jnp.exp(log_a)
    m = jnp.sqrt(_one_minus_exp(2.0 * log_a))
    return xb, r, gi, a, m


HEADS_PER_BLOCK = 4
HEAD_DIM = RG_BLOCK // HEADS_PER_BLOCK
_RG_BLOCKS_BF16 = pltpu.VMEM((N_RG_BLOCK, RG_BLOCK, RG_BLOCK), BF16)


def _fill_blockdiag(heads_ref, blocks):
    blocks[...] = jnp.zeros_like(blocks)
    for j in range(N_RG_BLOCK):
        for h in range(HEADS_PER_BLOCK):
            sl = slice(HEAD_DIM * h, HEAD_DIM * (h + 1))
            blocks[j, sl, sl] = heads_ref[0, HEADS_PER_BLOCK * j + h].astype(BF16)


def _rglru_fwd(proj, conv_w, conv_b, rg_wa, rg_wx, ba, bx, lam, sides=()):
    T = proj.shape[0]
    tm = min(TM_SCAN, T)

    def body(rx_ref, gate_ref, cw_ref, cb_ref, wah_ref, wxh_ref, ba_ref, bx_ref, lam_ref,
             ya_ref, xc_ref, h_ref, ext, hc, a_s, b_s, h_s, wa_ref, wx_ref):
        @pl.when(pl.program_id(0) == 0)
        def _():
            ext[0:SUBLANES, :] = jnp.zeros((SUBLANES, D), F32)
            hc[...] = jnp.zeros((SUBLANES, D), F32)
            _fill_blockdiag(wah_ref, wa_ref)
            _fill_blockdiag(wxh_ref, wx_ref)

        ext[SUBLANES:SUBLANES + tm, :] = rx_ref[...]
        xc = cb_ref[...]
        for k in range(CONV_WIDTH):
            xc = xc + ext[pl.ds(SUBLANES - (CONV_WIDTH - 1) + k, tm), :] * cw_ref[k:k + 1, :]
        ext[0:SUBLANES, :] = ext[tm:tm + SUBLANES, :]
        xc_ref[...] = xc
        _, _, gi, a, m = _rg_gates(xc, wa_ref, wx_ref, ba_ref[...], bx_ref[...], _softplus_neg(lam_ref[...]))
        h = _scan_tile(a, m * (gi * xc), hc[0:1, :], a_s, b_s, h_s, reverse=False)
        hc[...] = jnp.broadcast_to(h[tm - 1:tm, :], (SUBLANES, D))
        h_ref[...] = h
        ya_ref[...] = (_gelu(gate_ref[...]) * h).astype(BF16)

    vec = pl.BlockSpec((1, D), lambda i: (0, 0))
    heads = pl.BlockSpec(rg_wa.shape, lambda i: (0, 0, 0, 0))
    tile = pl.BlockSpec((tm, D), lambda i: (i, 0))
    return _call(
        body, name="rglru_fwd", grid=(T // tm,),
        in_specs=[pl.BlockSpec((tm, D), lambda i: (i, 0)), pl.BlockSpec((tm, D), lambda i: (i, 1)),
                  pl.BlockSpec((CONV_WIDTH, D), lambda i: (0, 0)), vec, heads, heads, vec, vec, vec],
        out_specs=[tile, tile, tile],
        out_shape=[S((T, D), BF16), S((T, D), F32), S((T, D), F32)],
        scratch_shapes=[pltpu.VMEM((tm + SUBLANES, D), F32), pltpu.VMEM((SUBLANES, D), F32)]
        + [pltpu.VMEM((D // LANES, tm, LANES), F32)] * 3 + [_RG_BLOCKS_BF16] * 2,
        args=(proj, proj, conv_w, conv_b, rg_wa, rg_wx, ba, bx, lam), sides=sides)


def _layer_norm_stats(v):
    mu = jnp.mean(v, axis=-1, keepdims=True)
    vc = v - mu
    rstd = lax.rsqrt(jnp.mean(vc * vc, axis=-1, keepdims=True) + EPS)
    return vc * rstd, rstd


def _sgu_mix(w_ref, vnb, bst_ref, n_chunk):
    cols = []
    for g in range(N_GROUP):
        vg = vnb[:, CHUNK * g:CHUNK * (g + 1)].reshape(n_chunk, CHUNK, CHUNK)
        wb = jnp.broadcast_to(w_ref[g][None], (n_chunk, CHUNK, CHUNK))
        mg = lax.dot_general(wb, vg, (((2,), (1,)), ((0,), (0,))), preferred_element_type=F32)
        mg = mg + bst_ref[:, g:g + 1][None]
        cols.append(mg.reshape(n_chunk * CHUNK, CHUNK))
    return jnp.concatenate(cols, axis=1)


def _causal_mask():
    return (lax.broadcasted_iota(jnp.int32, (CHUNK, CHUNK), 0) >= lax.broadcasted_iota(jnp.int32, (CHUNK, CHUNK), 1))


def _fill_sgu_weights(ws_ref, bs_ref, w_tril, bs_t, w_tril_t=None):
    keep = _causal_mask()
    for g in range(N_GROUP):
        wg = jnp.where(keep, ws_ref[0, g], 0.0)
        w_tril[g] = wg.astype(BF16)
        if w_tril_t is not None:
            w_tril_t[g] = wg.T.astype(BF16)
    bs_t[...] = bs_ref[0].T


_SGU_W_BF16 = pltpu.VMEM((N_GROUP, CHUNK, CHUNK), BF16)
_SGU_BT = pltpu.VMEM((CHUNK, N_GROUP), F32)


def _sgu_fwd(proj, ln_g, ln_b, sgu_ws, sgu_bs):
    T = proj.shape[0]
    tm = min(TM_MM, T)
    n_chunk = tm // CHUNK

    def body(u_ref, v_ref, g_ref, b_ref, ws_ref, bs_ref, yb_ref, w_ref, bst_ref):
        @pl.when(pl.program_id(0) == 0)
        def _():
            _fill_sgu_weights(ws_ref, bs_ref, w_ref, bst_ref)

        vhat, _ = _layer_norm_stats(_gelu(v_ref[...]))
        vnb = (vhat * g_ref[...] + b_ref[...]).astype(BF16)
        yb_ref[...] = (_gelu(u_ref[...]) * _sgu_mix(w_ref, vnb, bst_ref, n_chunk)).astype(BF16)

    vec = pl.BlockSpec((1, D), lambda i: (0, 0))
    return pl.pallas_call(
        body, name="sgu_fwd", grid=(T // tm,),
        in_specs=[pl.BlockSpec((tm, D), lambda i: (i, 2)), pl.BlockSpec((tm, D), lambda i: (i, 3)), vec, vec,
                  pl.BlockSpec(sgu_ws.shape, lambda i: (0, 0, 0, 0)), pl.BlockSpec(sgu_bs.shape, lambda i: (0, 0, 0))],
        out_specs=pl.BlockSpec((tm, D), lambda i: (i, 0)),
        out_shape=S((T, D), BF16),
        scratch_shapes=[_SGU_W_BF16, _SGU_BT],
        compiler_params=_params(1),
    )(proj, proj, ln_g, ln_b, sgu_ws, sgu_bs)


def _merge_fwd(x, proj, ya, yb, wpa, wpb, wout):
    T = x.shape[0]
    tm = min(TM_MM, T)

    def body(x_ref, ga_ref, gb_ref, ya_ref, yb_ref, wpa_ref, wpb_ref, wout_ref, pa_ref, pb_ref, mb_ref, x1_ref):
        pa = _dot(ya_ref[...], wpa_ref[...])
        pb = _dot(yb_ref[...], wpb_ref[...])
        pa_ref[...] = pa
        pb_ref[...] = pb
        mb = (_sig(ga_ref[...]) * pa + _sig(gb_ref[...]) * pb).astype(BF16)
        mb_ref[...] = mb
        x1_ref[...] = x_ref[...] + _dot(mb, wout_ref[...])

    tile = pl.BlockSpec((tm, D), lambda i: (i, 0))
    w = _resident((D, D))
    return pl.pallas_call(
        body, name="merge_fwd", grid=(T // tm,),
        in_specs=[tile, pl.BlockSpec((tm, D), lambda i: (i, 4)), pl.BlockSpec((tm, D), lambda i: (i, 5)),
                  tile, tile, w, w, w],
        out_specs=[tile, tile, tile, tile],
        out_shape=[S((T, D), F32), S((T, D), F32), S((T, D), BF16), S((T, D), F32)],
        compiler_params=_params(1),
    )(x, proj, proj, ya, yb, wpa, wpb, wout)


def _ffn_up(x1, g, w_gu):
    T = x1.shape[0]
    tm = min(TM_FF, T)

    def body(x_ref, g_ref, w_ref, h2_ref, gu_ref, act_ref):
        xv = x_ref[...]
        h2 = (xv * _rms_stats(xv) * g_ref[...]).astype(BF16)
        h2_ref[...] = h2
        for k in range(N_SHARD // 2):
            cols = slice(FF_SHARD * k, FF_SHARD * (k + 1))
            gate = _dot(h2, w_ref[k])
            up = _dot(h2, w_ref[k + N_SHARD // 2])
            gu_ref[:, cols] = gate.astype(BF16)
            gu_ref[:, D_FF + FF_SHARD * k:D_FF + FF_SHARD * (k + 1)] = up.astype(BF16)
            act_ref[:, cols] = (gate * _sig(gate) * up).astype(BF16)

    return pl.pallas_call(
        body, name="ffn_up", grid=(T // tm,),
        in_specs=[pl.BlockSpec((tm, D), lambda i: (i, 0)), pl.BlockSpec((1, D), lambda i: (0, 0)),
                  _resident((N_SHARD, D, FF_SHARD))],
        out_specs=[pl.BlockSpec((tm, D), lambda i: (i, 0)), pl.BlockSpec((tm, 2 * D_FF), lambda i: (i, 0)),
                   pl.BlockSpec((tm, D_FF), lambda i: (i, 0))],
        out_shape=[S((T, D), BF16), S((T, 2 * D_FF), BF16), S((T, D_FF), BF16)],
        compiler_params=_params(1),
    )(x1, g, w_gu)


def _ffn_down_loss(x1, act, w_down, g_final, target):
    T = x1.shape[0]
    tm = min(TM_MM, T)

    def body(x_ref, act_ref, w_ref, g_ref, t_ref, dx2_ref, dx2b_ref, loss_ref, dg_ref):
        @pl.when(pl.program_id(0) == 0)
        def _():
            loss_ref[...] = jnp.zeros_like(loss_ref)
            dg_ref[...] = jnp.zeros_like(dg_ref)

        x2 = x_ref[...] + _dot(act_ref[...], w_ref[...])
        gf = g_ref[...]
        err = x2 * _rms_stats(x2) * gf - t_ref[...]
        loss_ref[...] += 0.5 * jnp.sum(jnp.mean(err * err, axis=-1, keepdims=True), axis=0, keepdims=True)
        dx2, dg_rows = _rms_bwd(err * (1.0 / D), x2, gf)
        dg_ref[...] += _colsum(dg_rows)
        dx2_ref[...] = dx2
        dx2b_ref[...] = dx2.astype(BF16)

    tile = pl.BlockSpec((tm, D), lambda i: (i, 0))
    vec = pl.BlockSpec((1, D), lambda i: (0, 0))
    return pl.pallas_call(
        body, name="ffn_down_loss", grid=(T // tm,),
        in_specs=[tile, pl.BlockSpec((tm, D_FF), lambda i: (i, 0)), _resident((D_FF, D)), vec, tile],
        out_specs=[tile, tile, pl.BlockSpec((1, 1), lambda i: (0, 0)), vec],
        out_shape=[S((T, D), F32), S((T, D), BF16), S((1, 1), F32), S((1, D), F32)],
        compiler_params=_params(1),
    )(x1, act, w_down, g_final, target)


def _ffn_bwd_act(dx2b, gu, w_down):
    T = dx2b.shape[0]
    tm = min(TM_FF, T)

    def body(dx_ref, gu_ref, w_ref, dgu_ref):
        dact = _dot_nt(dx_ref[...], w_ref[...])
        gate = gu_ref[:, 0:D_FF].astype(F32)
        sg = _sig(gate)
        up = gu_ref[:, D_FF:2 * D_FF].astype(F32)
        dgu_ref[:, 0:D_FF] = (dact * up * (sg * (1.0 + gate * (1.0 - sg)))).astype(BF16)
        dgu_ref[:, D_FF:2 * D_FF] = (dact * (gate * sg)).astype(BF16)

    return pl.pallas_call(
        body, name="ffn_bwd_act", grid=(T // tm,),
        in_specs=[pl.BlockSpec((tm, D), lambda i: (i, 0)), pl.BlockSpec((tm, 2 * D_FF), lambda i: (i, 0)),
                  _resident((D_FF, D))],
        out_specs=pl.BlockSpec((tm, 2 * D_FF), lambda i: (i, 0)),
        out_shape=S((T, 2 * D_FF), BF16),
        compiler_params=_params(1),
    )(dx2b, gu, w_down)


def _ffn_bwd_in(dgu, w_gu, x1, g, dx2):
    T = x1.shape[0]
    tm = min(TM_MM, T)

    def body(dgu_ref, w_ref, x_ref, g_ref, dx2_ref, dx1_ref, dx1b_ref, dg_ref):
        @pl.when(pl.program_id(0) == 0)
        def _():
            dg_ref[...] = jnp.zeros_like(dg_ref)

        dh2 = _dot_nt(dgu_ref[:, 0:FF_SHARD], w_ref[0])
        for k in range(1, N_SHARD):
            dh2 = dh2 + _dot_nt(dgu_ref[:, FF_SHARD * k:FF_SHARD * (k + 1)], w_ref[k])
        dx, dg_rows = _rms_bwd(dh2, x_ref[...], g_ref[...])
        dg_ref[...] += _colsum(dg_rows)
        dx1 = dx2_ref[...] + dx
        dx1_ref[...] = dx1
        dx1b_ref[...] = dx1.astype(BF16)

    tile = pl.BlockSpec((tm, D), lambda i: (i, 0))
    vec = pl.BlockSpec((1, D), lambda i: (0, 0))
    return pl.pallas_call(
        body, name="ffn_bwd_in", grid=(T // tm,),
        in_specs=[pl.BlockSpec((tm, 2 * D_FF), lambda i: (i, 0)), _resident((N_SHARD, D, FF_SHARD)), tile, vec, tile],
        out_specs=[tile, tile, vec],
        out_shape=[S((T, D), F32), S((T, D), BF16), S((1, D), F32)],
        compiler_params=_params(1),
    )(dgu, w_gu, x1, g, dx2)


def _matmul_tn(a, b, tn, shard_major, name, sides=()):
    T, M = a.shape
    N = b.shape[1]
    tk = min(TK_WGRAD, T)

    def body(a_ref, b_ref, o_ref):
        @pl.when(pl.program_id(1) == 0)
        def _():
            o_ref[...] = jnp.zeros_like(o_ref)

        acc = _dot_tn(a_ref[...], b_ref[...])
        if shard_major:
            o_ref[0] += acc
        else:
            o_ref[...] += acc

    if shard_major:
        out_spec, out_shape = pl.BlockSpec((1, M, tn), lambda j, k: (j, 0, 0)), S((N // tn, M, tn), F32)
    else:
        out_spec, out_shape = pl.BlockSpec((M, tn), lambda j, k: (0, j)), S((M, N), F32)
    (out,), side_outs = _call(
        body, name=name, grid=(N // tn, T // tk),
        in_specs=[pl.BlockSpec((tk, M), lambda j, k: (k, 0)), pl.BlockSpec((tk, tn), lambda j, k: (k, j))],
        out_specs=[out_spec], out_shape=[out_shape], args=(a, b), sides=sides)
    return out, side_outs


def _merge_bwd(dx1b, proj, pa, pb, wpa, wpb, wout, sides=()):
    T = dx1b.shape[0]
    tm = min(TM_MM, T)

    def body(dx_ref, ga_ref, gb_ref, pa_ref, pb_ref, wpa_ref, wpb_ref, wout_ref,
             dpa_ref, dpb_ref, dgate_ref, dya_ref, dyb_ref):
        dm = _dot_nt(dx_ref[...], wout_ref[...])
        sa = _sig(ga_ref[...])
        sb = _sig(gb_ref[...])
        dpa = (dm * sa).astype(BF16)
        dpb = (dm * sb).astype(BF16)
        dpa_ref[...] = dpa
        dpb_ref[...] = dpb
        dgate_ref[:, 0:D] = (dm * pa_ref[...] * (sa * (1.0 - sa))).astype(BF16)
        dgate_ref[:, D:2 * D] = (dm * pb_ref[...] * (sb * (1.0 - sb))).astype(BF16)
        dya_ref[...] = _dot_nt(dpa, wpa_ref[...])
        dyb_ref[...] = _dot_nt(dpb, wpb_ref[...])

    tile = pl.BlockSpec((tm, D), lambda i: (i, 0))
    w = _resident((D, D))
    return _call(
        body, name="merge_bwd", grid=(T // tm,),
        in_specs=[tile, pl.BlockSpec((tm, D), lambda i: (i, 4)), pl.BlockSpec((tm, D), lambda i: (i, 5)),
                  tile, tile, w, w, w],
        out_specs=[tile, tile, pl.BlockSpec((tm, 2 * D), lambda i: (i, 2)), tile, tile],
        out_shape=[S((T, D), BF16), S((T, D), BF16), S((T, IN_COLS), BF16), S((T, D), F32), S((T, D), F32)],
        args=(dx1b, proj, proj, pa, pb, wpa, wpb, wout), sides=sides)


def _rglru_bwd(dya, dproj, proj, hseq, xc, conv_w, rg_wa, rg_wx, ba, bx, lam, sides=()):
    T = dya.shape[0]
    tm = min(TM_SCAN, T)
    n = T // tm
    per8 = tm // SUBLANES

    def body(dya_ref, _, rx_ref, rxp_ref, gate_ref, h_ref, hp_ref, xc_ref, cw_ref, wah_ref, wxh_ref, ba_ref, bx_ref,
             lam_ref, dab_ref, dcw_ref, dcb_ref, dwah_ref, dwxh_ref, dba_ref, dbx_ref, dlam_ref,
             hext, rext, dext, carry_a, carry_dh, a_s, b_s, h_s, wa_ref, wx_ref, dwa_ref, dwx_ref):
        i = pl.program_id(0)
        first_tile = i == n - 1

        @pl.when(i == 0)
        def _():
            for ref in (dcw_ref, dcb_ref, dwa_ref, dwx_ref, dba_ref, dbx_ref, dlam_ref, carry_a, carry_dh):
                ref[...] = jnp.zeros_like(ref)
            dext[tm:tm + SUBLANES, :] = jnp.zeros((SUBLANES, D), F32)
            _fill_blockdiag(wah_ref, wa_ref)
            _fill_blockdiag(wxh_ref, wx_ref)

        gel, dgel = _gelu_and_grad(gate_ref[...])
        dya_v = dya_ref[...]
        hseq_v = h_ref[...]
        dgate = dya_v * hseq_v * dgel
        xcv = xc_ref[...]
        lam_v = lam_ref[...]
        sp = _softplus_neg(lam_v)
        xb, r, gi, a, m = _rg_gates(xcv, wa_ref, wx_ref, ba_ref[...], bx_ref[...], sp)

        row = lax.broadcasted_iota(jnp.int32, (tm, D), 0)
        c = jnp.where(row == tm - 1, carry_a[0:1, :], _shift_up(a, 1, 0.0))
        dH = _scan_tile(c, dya_v * gel, carry_dh[0:1, :], a_s, b_s, h_s, reverse=True)
        carry_a[...] = jnp.broadcast_to(a[0:1, :], (SUBLANES, D))
        carry_dh[...] = jnp.broadcast_to(dH[0:1, :], (SUBLANES, D))

        hext[0:SUBLANES, :] = jnp.where(first_tile, 0.0, hp_ref[...])
        hext[SUBLANES:SUBLANES + tm, :] = hseq_v
        h_prev = hext[pl.ds(SUBLANES - 1, tm), :]

        d_m = dH * (gi * xcv)
        d_la = dH * h_prev * a - d_m * (a * a) / m
        d_ia = dH * m * xcv * (gi * (1.0 - gi))
        d_ra = d_la * ((-RG_C) * sp) * (r * (1.0 - r))
        dlam_ref[...] += _colsum(d_la * ((-RG_C) * r)) * (-_sig(-lam_v))
        dba_ref[...] += _colsum(d_ra)
        dbx_ref[...] += _colsum(d_ia)
        drab = d_ra.astype(BF16)
        diab = d_ia.astype(BF16)
        dxc_cols = []
        for j in range(N_RG_BLOCK):
            sl = slice(RG_BLOCK * j, RG_BLOCK * (j + 1))
            dxc_cols.append(_dot_nt(drab[:, sl], wa_ref[j]) + _dot_nt(diab[:, sl], wx_ref[j]))
            dwa_ref[j] += _dot_tn(xb[:, sl], drab[:, sl])
            dwx_ref[j] += _dot_tn(xb[:, sl], diab[:, sl])
        dxc = dH * m * gi + jnp.concatenate(dxc_cols, axis=1)

        dcb_ref[...] += _colsum(dxc)
        dext[0:tm, :] = dxc
        rext[0:SUBLANES, :] = jnp.where(first_tile, 0.0, rxp_ref[...])
        rext[SUBLANES:SUBLANES + tm, :] = rx_ref[...]
        drx = jnp.zeros((tm, D), F32)
        for k in range(CONV_WIDTH):
            drx = drx + dext[pl.ds(CONV_WIDTH - 1 - k, tm), :] * cw_ref[k:k + 1, :]
            dcw_ref[k:k + 1, :] += _colsum(dxc * rext[pl.ds(SUBLANES - (CONV_WIDTH - 1) + k, tm), :])
        dext[tm:tm + SUBLANES, :] = dext[0:SUBLANES, :]
        dab_ref[:, 0:D] = drx.astype(BF16)
        dab_ref[:, D:2 * D] = dgate.astype(BF16)

        @pl.when(first_tile)
        def _():
            for j in range(N_RG_BLOCK):
                for h in range(HEADS_PER_BLOCK):
                    sl = slice(HEAD_DIM * h, HEAD_DIM * (h + 1))
                    dwah_ref[HEADS_PER_BLOCK * j + h] = dwa_ref[j, sl, sl]
                    dwxh_ref[HEADS_PER_BLOCK * j + h] = dwx_ref[j, sl, sl]

    def rev(col):
        return lambda i: (n - 1 - i, col)

    def prev8(col):
        return lambda i: (jnp.maximum((n - 1 - i) * per8 - 1, 0), col)

    tile = pl.BlockSpec((tm, D), rev(0))
    vec = pl.BlockSpec((1, D), lambda i: (0, 0))
    heads_in = pl.BlockSpec(rg_wa.shape, lambda i: (0, 0, 0, 0))
    heads_out = pl.BlockSpec(rg_wa.shape[1:], lambda i: (0, 0, 0))
    cw = pl.BlockSpec((CONV_WIDTH, D), lambda i: (0, 0))
    blocks_f32 = pltpu.VMEM((N_RG_BLOCK, RG_BLOCK, RG_BLOCK), F32)
    return _call(
        body, name="rglru_bwd", grid=(n,),
        in_specs=[tile, _ANY, pl.BlockSpec((tm, D), rev(0)), pl.BlockSpec((SUBLANES, D), prev8(0)),
                  pl.BlockSpec((tm, D), rev(1)), tile, pl.BlockSpec((SUBLANES, D), prev8(0)), tile,
                  cw, heads_in, heads_in, vec, vec, vec],
        out_specs=[pl.BlockSpec((tm, 2 * D), rev(0)), cw, vec, heads_out, heads_out, vec, vec, vec],
        out_shape=[S((T, IN_COLS), BF16), S((CONV_WIDTH, D), F32), S((1, D), F32),
                   S(rg_wa.shape[1:], F32), S(rg_wa.shape[1:], F32), S((1, D), F32), S((1, D), F32), S((1, D), F32)],
        scratch_shapes=[pltpu.VMEM((tm + SUBLANES, D), F32), pltpu.VMEM((tm + SUBLANES, D), F32),
                        pltpu.VMEM((tm + SUBLANES, D), F32), pltpu.VMEM((SUBLANES, D), F32),
                        pltpu.VMEM((SUBLANES, D), F32)] + [pltpu.VMEM((D // LANES, tm, LANES), F32)] * 3
        + [_RG_BLOCKS_BF16] * 2 + [blocks_f32] * 2,
        args=(dya, dproj, proj, proj, proj, hseq, hseq, xc, conv_w, rg_wa, rg_wx, ba, bx, lam), sides=sides,
        aliases={1: 0})


def _sgu_bwd(dyb, dproj, proj, ln_g, ln_b, sgu_ws, sgu_bs, sides=()):
    T = dyb.shape[0]
    tm = min(TM_MM, T)
    n_chunk = tm // CHUNK

    def body(dyb_ref, _, u_ref, v_ref, g_ref, b_ref, ws_ref, bs_ref,
             duv_ref, dw_ref, dbs_ref, dg_ref, db_ref, w_ref, wt_ref, bst_ref):
        @pl.when(pl.program_id(0) == 0)
        def _():
            for ref in (dw_ref, dbs_ref, dg_ref, db_ref):
                ref[...] = jnp.zeros_like(ref)
            _fill_sgu_weights(ws_ref, bs_ref, w_ref, bst_ref, wt_ref)

        gu, dgu = _gelu_and_grad(u_ref[...])
        gv, dgv = _gelu_and_grad(v_ref[...])
        vhat, rstd = _layer_norm_stats(gv)
        lng = g_ref[...]
        vnb = (vhat * lng + b_ref[...]).astype(BF16)
        mixed = _sgu_mix(w_ref, vnb, bst_ref, n_chunk)
        dyb_v = dyb_ref[...]
        duv_ref[:, 0:D] = (dyb_v * mixed * dgu).astype(BF16)
        dmix = dyb_v * gu
        dmb = dmix.astype(BF16)
        keep = _causal_mask()
        dvn_cols, dbs_rows = [], []
        for g in range(N_GROUP):
            sl = slice(CHUNK * g, CHUNK * (g + 1))
            dmg = dmb[:, sl].reshape(n_chunk, CHUNK, CHUNK)
            vg = vnb[:, sl].reshape(n_chunk, CHUNK, CHUNK)
            wtb = jnp.broadcast_to(wt_ref[g][None], (n_chunk, CHUNK, CHUNK))
            dvn = lax.dot_general(wtb, dmg, (((2,), (1,)), ((0,), (0,))), preferred_element_type=F32)
            dvn_cols.append(dvn.reshape(tm, CHUNK))
            dw = lax.dot_general(dmg, vg, (((2,), (2,)), ((0,), (0,))), preferred_element_type=F32)
            dw_ref[g] += jnp.where(keep, jnp.sum(dw, axis=0), 0.0)
            per_token = jnp.sum(dmix[:, sl], axis=1)
            dbs_rows.append(jnp.sum(per_token.reshape(n_chunk, CHUNK), axis=0, keepdims=True))
        dbs_ref[...] += jnp.concatenate(dbs_rows, axis=0)
        dvn = jnp.concatenate(dvn_cols, axis=1)
        dg_ref[...] += _colsum(dvn * vhat)
        db_ref[...] += _colsum(dvn)
        dvhat = dvn * lng
        dgv_in = rstd * (dvhat - jnp.mean(dvhat, axis=-1, keepdims=True)
                         - vhat * jnp.mean(dvhat * vhat, axis=-1, keepdims=True))
        duv_ref[:, D:2 * D] = (dgv_in * dgv).astype(BF16)

    tile = pl.BlockSpec((tm, D), lambda i: (i, 0))
    vec = pl.BlockSpec((1, D), lambda i: (0, 0))
    wsp = pl.BlockSpec((N_GROUP, CHUNK, CHUNK), lambda i: (0, 0, 0))
    bsp = pl.BlockSpec((N_GROUP, CHUNK), lambda i: (0, 0))
    return _call(
        body, name="sgu_bwd", grid=(T // tm,),
        in_specs=[tile, _ANY, pl.BlockSpec((tm, D), lambda i: (i, 2)), pl.BlockSpec((tm, D), lambda i: (i, 3)),
                  vec, vec, pl.BlockSpec(sgu_ws.shape, lambda i: (0, 0, 0, 0)),
                  pl.BlockSpec(sgu_bs.shape, lambda i: (0, 0, 0))],
        out_specs=[pl.BlockSpec((tm, 2 * D), lambda i: (i, 1)), wsp, bsp, vec, vec],
        out_shape=[S((T, IN_COLS), BF16), S((N_GROUP, CHUNK, CHUNK), F32), S((N_GROUP, CHUNK), F32),
                   S((1, D), F32), S((1, D), F32)],
        scratch_shapes=[_SGU_W_BF16, _SGU_W_BF16, _SGU_BT],
        args=(dyb, dproj, proj, proj, ln_g, ln_b, sgu_ws, sgu_bs), sides=sides, aliases={1: 0})


def _inproj_bwd(dproj, w_in, x, g, dx1):
    T = x.shape[0]
    tm = min(TM_MM, T)

    def body(dp_ref, w_ref, x_ref, g_ref, dx1_ref, dx_ref, dgm_ref):
        @pl.when(pl.program_id(0) == 0)
        def _():
            dgm_ref[...] = jnp.zeros_like(dgm_ref)

        dh = _dot_nt(dp_ref[:, 0:IN_SHARD], w_ref[0])
        for k in range(1, N_SHARD):
            dh = dh + _dot_nt(dp_ref[:, IN_SHARD * k:IN_SHARD * (k + 1)], w_ref[k])
        dx, dg_rows = _rms_bwd(dh, x_ref[...], g_ref[...])
        dgm_ref[...] += _colsum(dg_rows)
        dx_ref[...] = dx1_ref[...] + dx

    tile = pl.BlockSpec((tm, D), lambda i: (i, 0))
    vec = pl.BlockSpec((1, D), lambda i: (0, 0))
    return pl.pallas_call(
        body, name="inproj_bwd", grid=(T // tm,),
        in_specs=[pl.BlockSpec((tm, IN_COLS), lambda i: (i, 0)), _resident((N_SHARD, D, IN_SHARD)), tile, vec, tile],
        out_specs=[tile, vec],
        out_shape=[S((T, D), F32), S((1, D), F32)],
        compiler_params=_params(1),
    )(dproj, w_in, x, g, dx1)


def _row_tile(rows):
    for t in range(256, 0, -SUBLANES):
        if rows % t == 0:
            return t
    raise ValueError(rows)


def _add_halves(core, g, theirs, name):
    _, r, cols = g.shape
    half = r // 2
    tr = _row_tile(half)
    nb = half // tr

    def body(core_ref, g_ref, t_ref, o32_ref, o16_ref):
        s = g_ref[...] + t_ref[...]
        o32_ref[...] = s
        o16_ref[...] = s.astype(BF16)

    blk = pl.BlockSpec((1, tr, cols), lambda s, i, core_ref: (s, i, 0))
    gs = pltpu.PrefetchScalarGridSpec(
        num_scalar_prefetch=1, grid=(N_SHARD, nb),
        in_specs=[pl.BlockSpec((1, tr, cols), lambda s, i, core_ref: (s, core_ref[0] * nb + i, 0)), blk],
        out_specs=[blk, blk])
    return pl.pallas_call(
        body, name=name, grid_spec=gs,
        out_shape=[S((N_SHARD, half, cols), F32), S((N_SHARD, half, cols), BF16)],
        compiler_params=_params(2),
    )(core, g, theirs)


def _sum_shards(chip, own, others, name):
    _, half, cols = own.shape
    tr = _row_tile(half)

    def body(chip_ref, own_ref, oth_ref, o_ref):
        acc = own_ref[0]
        for j in range(3):
            acc = acc + oth_ref[j].astype(F32)
        o_ref[...] = acc

    gs = pltpu.PrefetchScalarGridSpec(
        num_scalar_prefetch=1, grid=(half // tr,),
        in_specs=[pl.BlockSpec((1, tr, cols), lambda i, chip_ref: (chip_ref[0], i, 0)),
                  pl.BlockSpec((3, tr, cols), lambda i, chip_ref: (0, i, 0))],
        out_specs=pl.BlockSpec((tr, cols), lambda i, chip_ref: (i, 0)))
    return pl.pallas_call(
        body, name=name, grid_spec=gs, out_shape=S((half, cols), F32), compiler_params=_params(1),
    )(chip, own, others)


def _adamw(w, g, m, v):
    m = ADAM_B1 * m + (1.0 - ADAM_B1) * g
    v = ADAM_B2 * v + (1.0 - ADAM_B2) * (g * g)
    m_hat = m / (1.0 - ADAM_B1 ** ADAM_STEP)
    v_hat = v / (1.0 - ADAM_B2 ** ADAM_STEP)
    delta = -ADAM_LR * (m_hat / (jnp.sqrt(v_hat) + ADAM_EPS) + ADAM_WD * w)
    return delta, m, v


def _adamw_shard(core, mine, theirs, w, m, v, name):
    r, cols = w.shape
    half = r // 2
    tr = _row_tile(half)
    nb = half // tr

    def body(core_ref, mine_ref, theirs_ref, w_ref, m_ref, v_ref, g_ref, d_ref, mo_ref, vo_ref):
        g = jnp.where(pl.program_id(0) == core_ref[0], mine_ref[...], theirs_ref[...])
        g_ref[...] = g
        d_ref[...], mo_ref[...], vo_ref[...] = _adamw(w_ref[...], g, m_ref[...], v_ref[...])

    hblk = pl.BlockSpec((tr, cols), lambda h, i, core_ref: (i, 0))
    blk = pl.BlockSpec((tr, cols), lambda h, i, core_ref: (h * nb + i, 0))
    gs = pltpu.PrefetchScalarGridSpec(num_scalar_prefetch=1, grid=(2, nb),
                                      in_specs=[hblk, hblk, blk, blk, blk], out_specs=[blk] * 4)
    return pl.pallas_call(
        body, name=name, grid_spec=gs, out_shape=[S((r, cols), F32)] * 4, compiler_params=_params(2),
    )(core, mine, theirs, w, m, v)


def _adamw_whole(w, g, m, v, name):
    def body(w_ref, g_ref, m_ref, v_ref, d_ref, mo_ref, vo_ref):
        d_ref[...], mo_ref[...], vo_ref[...] = _adamw(w_ref[...], g_ref[...], m_ref[...], v_ref[...])

    return pl.pallas_call(body, name=name, out_shape=[S(w.shape, F32)] * 3)(w, g, m, v)


_VEC_ROWS = ("norm_mix_g", "conv_b", "rg_lambda", "sgu_ln_g", "sgu_ln_b", "norm_ffn_g", "norm_final_g", "rg_ba",
             "rg_bx")
_CONV_ROW = len(_VEC_ROWS)
_LOSS_ROW = _CONV_ROW + CONV_WIDTH
_VEC_PAD = -(_LOSS_ROW + 1) % SUBLANES
_HEAD_BIASES = ("rg_ba", "rg_bx")
_TENSORS = ("sgu_bs", "rg_wa", "rg_wx", "sgu_ws")


def _small_sum_adamw(parts, w, m, v):
    names = [n for n in _VEC_ROWS] + list(_TENSORS)
    n_parts = len(parts)

    def total(ref):
        acc = ref[0]
        for k in range(1, N_DEVICE):
            acc = acc + ref[k]
        return acc

    def body(*refs):
        part_refs, refs = refs[:n_parts], refs[n_parts:]
        w_refs, m_refs, v_refs = (dict(zip(names, refs[k * len(names):(k + 1) * len(names)])) for k in range(3))
        outs = refs[3 * len(names):]
        out_refs = {n: outs[4 * k:4 * k + 4] for k, n in enumerate(names)}
        conv_ref, loss_ref = outs[4 * len(names):]
        vec = total(part_refs[0])
        grads = {n: total(p)[None] for n, p in zip(_TENSORS, part_refs[1:])}
        for row, n in enumerate(_VEC_ROWS):
            g = vec[row:row + 1, :]
            if n in _HEAD_BIASES:
                g = jnp.concatenate([g[:, HEAD_DIM * h:HEAD_DIM * (h + 1)] for h in range(D // HEAD_DIM)], axis=0)[None]
            grads[n] = g
        for n in names:
            g_ref, d_ref, mo_ref, vo_ref = out_refs[n]
            g_ref[...] = grads[n]
            d_ref[...], mo_ref[...], vo_ref[...] = _adamw(w_refs[n][...], grads[n], m_refs[n][...], v_refs[n][...])
        conv_ref[...] = vec[_CONV_ROW:_CONV_ROW + CONV_WIDTH, :]
        loss_ref[...] = vec[_LOSS_ROW:_LOSS_ROW + 1, 0:1]

    res = pl.pallas_call(
        body, name="small_sum_adamw",
        out_shape=[S(w[n].shape, F32) for n in names for _ in range(4)] + [S((CONV_WIDTH, D), F32), S((1, 1), F32)],
        compiler_params=pltpu.CompilerParams(vmem_limit_bytes=VMEM_LIMIT),
    )(*parts, *[w[n] for n in names], *[m[n] for n in names], *[v[n] for n in names])
    return {n: tuple(res[4 * k:4 * k + 4]) for k, n in enumerate(names)}, res[-2], res[-1]


_BIG = ("w_in", "w_proj_a", "w_proj_b", "w_out", "w_gate_up", "w_down")
_WEIGHTS = ("norm_mix_g", "w_in", "conv_w", "conv_b", "rg_wa", "rg_ba", "rg_wx", "rg_bx", "rg_lambda", "sgu_ln_g",
            "sgu_ln_b", "sgu_ws", "sgu_bs", "w_proj_a", "w_proj_b", "w_out", "norm_ffn_g", "w_gate_up", "w_down",
            "norm_final_g")


def kernel(x, norm_mix_g, w_in, conv_w, conv_b, rg_wa, rg_ba, rg_wx, rg_bx, rg_lambda, sgu_ln_g, sgu_ln_b, sgu_ws, sgu_bs, w_proj_a, w_proj_b, w_out, norm_ffn_g, w_gate_up, w_down, norm_final_g, loss_target, m_norm_mix_g, m_w_in, m_conv_w, m_conv_b, m_rg_wa, m_rg_ba, m_rg_wx, m_rg_bx, m_rg_lambda, m_sgu_ln_g, m_sgu_ln_b, m_sgu_ws, m_sgu_bs, m_w_proj_a, m_w_proj_b, m_w_out, m_norm_ffn_g, m_w_gate_up, m_w_down, m_norm_final_g, v_norm_mix_g, v_w_in, v_conv_w, v_conv_b, v_rg_wa, v_rg_ba, v_rg_wx, v_rg_bx, v_rg_lambda, v_sgu_ln_g, v_sgu_ln_b, v_sgu_ws, v_sgu_bs, v_w_proj_a, v_w_proj_b, v_w_out, v_norm_ffn_g, v_w_gate_up, v_w_down, v_norm_final_g):
    args = dict(locals())
    w = {n: args[n] for n in _WEIGHTS}
    mom = {n: args["m_" + n] for n in _WEIGHTS}
    var = {n: args["v_" + n] for n in _WEIGHTS}
    xi, yi, ci = _position()
    core = ci.astype(jnp.int32).reshape(1)
    chip = (2 * xi + yi).astype(jnp.int32).reshape(1)

    bf = {n: w[n][0].astype(BF16) for n in _BIG}
    final_g = w["norm_final_g"].reshape(1, D)
    ba, bx = w["rg_ba"].reshape(1, D), w["rg_bx"].reshape(1, D)
    lam, ln_g, ln_b = w["rg_lambda"], w["sgu_ln_g"], w["sgu_ln_b"]
    x0, target = x[0], loss_target[0]

    def shard_major(g):
        return g.reshape(N_SHARD, g.shape[0] // N_SHARD, g.shape[1])

    def chip_sums(names, grads, theirs):
        return [_add_halves(core, g, t, "add_halves_" + n) for n, g, t in zip(names, grads, theirs)]

    def my_halves(names, sums, arrived):
        return [_sum_shards(chip, p32, a, "sum_shards_" + n) for n, (p32, _), a in zip(names, sums, arrived)]

    (w_in_a,), (conv_a,) = _comm_only([_gather_half_side([bf["w_in"]]), _gather_side([w["conv_w"][0]])], "gather_w_in")
    conv_cols = conv_a.shape[-1]
    conv_full = jnp.swapaxes(conv_a, 0, 1).reshape(CONV_WIDTH, D)
    (proj, h), ((w_pa_a, w_pb_a, w_out_a),) = _inproj(
        x0, w["norm_mix_g"], w_in_a, sides=[_gather_half_side([bf["w_proj_a"], bf["w_proj_b"], bf["w_out"]])])
    (ya, xc, hseq), ((w_gu_a, w_down_a),) = _rglru_fwd(
        proj, conv_full, w["conv_b"], w["rg_wa"], w["rg_wx"], ba, bx, lam,
        sides=[_gather_half_side([bf["w_gate_up"], bf["w_down"]])])
    wpa, wpb, wout, wdown = w_pa_a.reshape(D, D), w_pb_a.reshape(D, D), w_out_a.reshape(D, D), w_down_a.reshape(D_FF, D)
    yb = _sgu_fwd(proj, ln_g, ln_b, w["sgu_ws"], w["sgu_bs"])
    pa, pb, mb, x1 = _merge_fwd(x0, proj, ya, yb, wpa, wpb, wout)
    h2, gu, act = _ffn_up(x1, w["norm_ffn_g"], w_gu_a)
    dx2, dx2b, loss, d_final_g = _ffn_down_loss(x1, act, wdown, final_g, target)

    dgu = _ffn_bwd_act(dx2b, gu, wdown)
    dx1, dx1b, d_ffn_g = _ffn_bwd_in(dgu, w_gu_a, x1, w["norm_ffn_g"], dx2)
    ffn = ("w_gate_up", "w_down")
    g_ffn = [_matmul_tn(h2, dgu, FF_SHARD, True, "wgrad_gate_up")[0],
             shard_major(_matmul_tn(act, dx2b, D // 2, False, "wgrad_down")[0])]
    (dpa, dpb, dproj, dya, dyb), (theirs_ffn,) = _merge_bwd(
        dx1b, proj, pa, pb, wpa, wpb, wout, sides=[_halves_side(g_ffn)])
    sums_ffn = chip_sums(ffn, g_ffn, theirs_ffn)
    mix = ("w_proj_a", "w_proj_b", "w_out")
    g_mix = [shard_major(_matmul_tn(ya, dpa, D, False, "wgrad_proj_a")[0]),
             shard_major(_matmul_tn(yb, dpb, D, False, "wgrad_proj_b")[0]),
             shard_major(_matmul_tn(mb, dx1b, D, False, "wgrad_out")[0])]
    (dproj, d_cw, d_cb, d_wa, d_wx, d_ba, d_bx, d_lam), (arrived_ffn, theirs_mix) = _rglru_bwd(
        dya, dproj, proj, hseq, xc, conv_full, w["rg_wa"], w["rg_wx"], ba, bx, lam,
        sides=[_scatter_side([p16 for _, p16 in sums_ffn]), _halves_side(g_mix)])
    mine_ffn = my_halves(ffn, sums_ffn, arrived_ffn)
    sums_mix = chip_sums(mix, g_mix, theirs_mix)
    (dproj, d_ws, d_bs, d_lng, d_lnb), (other_ffn, arrived_mix) = _sgu_bwd(
        dyb, dproj, proj, ln_g, ln_b, w["sgu_ws"], w["sgu_bs"],
        sides=[_swap_side(mine_ffn), _scatter_side([p16 for _, p16 in sums_mix])])
    mine_mix = my_halves(mix, sums_mix, arrived_mix)
    grad_x, d_mix_g = _inproj_bwd(dproj, w_in_a, x0, w["norm_mix_g"], dx1)
    rows = {"norm_mix_g": d_mix_g, "conv_b": d_cb, "rg_lambda": d_lam, "sgu_ln_g": d_lng, "sgu_ln_b": d_lnb,
            "norm_ffn_g": d_ffn_g, "norm_final_g": d_final_g, "rg_ba": d_ba, "rg_bx": d_bx}
    vec = jnp.concatenate([rows[n] for n in _VEC_ROWS]
                          + [d_cw, jnp.pad(loss, ((0, _VEC_PAD), (0, D - 1)))], axis=0)
    tensors = {"sgu_bs": d_bs, "rg_wa": d_wa, "rg_wx": d_wx, "sgu_ws": d_ws}
    g_in, (small_parts, other_mix) = _matmul_tn(
        h, dproj, IN_SHARD, True, "inproj_wgrad",
        sides=[_everyone_side([vec] + [tensors[n] for n in _TENSORS]), _swap_side(mine_mix)])
    theirs_in = _comm_only([_halves_side([g_in])], "halves_w_in")[0]
    sums_in = chip_sums(("w_in",), [g_in], theirs_in)
    arrived_in = _comm_only([_scatter_side([p16 for _, p16 in sums_in])], "scatter_w_in")[0]
    mine_in = my_halves(("w_in",), sums_in, arrived_in)
    other_in = _comm_only([_swap_side(mine_in)], "swap_w_in")[0]

    out = {}
    for n, gm, go in zip(ffn + mix + ("w_in",), mine_ffn + mine_mix + mine_in, other_ffn + other_mix + other_in):
        g, d, mo, vo = _adamw_shard(core, gm, go, w[n][0], mom[n][0], var[n][0], "adamw_" + n)
        out[n] = tuple(a[None] for a in (g, d, mo, vo))
    as_row = lambda t: {n: a.reshape(1, D) if n == "norm_final_g" else a for n, a in t.items()}
    small_out, conv_sum, loss_sum = _small_sum_adamw(small_parts, as_row(w), as_row(mom), as_row(var))
    out.update(small_out)
    out["norm_final_g"] = tuple(a.reshape(D) for a in small_out["norm_final_g"])
    conv_g = lax.dynamic_slice_in_dim(conv_sum, chip[0] * conv_cols, conv_cols, axis=1)
    d, mo, vo = _adamw_whole(w["conv_w"][0], conv_g, mom["conv_w"][0], var["conv_w"][0], "adamw_conv_w")
    out["conv_w"] = tuple(a[None] for a in (conv_g, d, mo, vo))

    return (loss_sum[0, 0], grad_x[None], *[out[n][0] for n in _WEIGHTS], *[out[n][1] for n in _WEIGHTS],
            *[out[n][2] for n in _WEIGHTS], *[out[n][3] for n in _WEIGHTS])
```

```python
import functools

import jax
import jax.numpy as jnp
from jax import lax
from jax.experimental import pallas as pl
from jax.experimental.pallas import tpu as pltpu

F32 = jnp.float32
BF16 = jnp.bfloat16
S = jax.ShapeDtypeStruct

D = 1024
N_SHARD = 4
IN_COLS = 6 * D
IN_SHARD = IN_COLS // N_SHARD
D_FF = 2816
FF_SHARD = 2 * D_FF // N_SHARD
RG_BLOCK = 256
N_RG_BLOCK = D // RG_BLOCK
CHUNK = 128
N_GROUP = 8
CONV_WIDTH = 4
RG_C = 8.0
EPS = 1e-6
ADAM_LR, ADAM_B1, ADAM_B2, ADAM_EPS, ADAM_WD, ADAM_STEP = 0.001, 0.9, 0.999, 1e-08, 0.01, 10

V7X_VMEM_BYTES = 64 * 1024 * 1024
VMEM_LIMIT = V7X_VMEM_BYTES * 3 // 4
SUBLANES = 8
MESH = pl.DeviceIdType.MESH

TM_MM = 512
TM_SCAN = 256
TM_FF = 256
TK_WGRAD = 1024


def _params(n_axes):
    return pltpu.CompilerParams(dimension_semantics=("arbitrary",) * n_axes, vmem_limit_bytes=VMEM_LIMIT)


def _resident(shape):
    nd = len(shape)
    return pl.BlockSpec(shape, lambda *_: (0,) * nd, pipeline_mode=pl.Buffered(1))


def _sig(x):
    return 1.0 / (1.0 + jnp.exp(-x))


_GELU_K2 = 2.0 * 0.7978845608028654
_GELU_C = 0.044715


def _gelu(x):
    return x * _sig(x * (_GELU_K2 + (_GELU_K2 * _GELU_C) * (x * x)))


def _gelu_and_grad(x):
    x2 = x * x
    s = _sig(x * (_GELU_K2 + (_GELU_K2 * _GELU_C) * x2))
    g = x * s
    return g, s + g * (1.0 - s) * (_GELU_K2 + (3.0 * _GELU_K2 * _GELU_C) * x2)


_EXPM1_SERIES = tuple(1.0 / f for f in (5040.0, 720.0, 120.0, 24.0, 6.0, 2.0, 1.0))


def _one_minus_exp(x):
    p = _EXPM1_SERIES[0]
    for coef in _EXPM1_SERIES[1:]:
        p = p * x + coef
    return jnp.where(x > -0.125, -x * p, 1.0 - jnp.exp(x))


def _softplus_neg(lam):
    z = -lam
    e = jnp.exp(-jnp.abs(z))
    u = 1.0 + e
    log1p = jnp.where(u == 1.0, e, jnp.log(u) * e / (u - 1.0))
    return jnp.maximum(z, 0.0) + log1p


def _rms_stats(x):
    return lax.rsqrt(jnp.mean(x * x, axis=-1, keepdims=True) + EPS)


def _rms_bwd(dy, x, g):
    rstd = _rms_stats(x)
    xhat = x * rstd
    dxhat = dy * g
    dx = rstd * (dxhat - xhat * jnp.mean(dxhat * xhat, axis=-1, keepdims=True))
    return dx, dy * xhat


def _colsum(x):
    return jnp.sum(x, axis=0, keepdims=True)


def _shift_down(x, d, fill):
    n = x.shape[0]
    if d % SUBLANES == 0:
        return jnp.concatenate([jnp.full((d, x.shape[1]), fill, x.dtype), x[:n - d]], axis=0)
    row = lax.broadcasted_iota(jnp.int32, x.shape, 0)
    return jnp.where(row < d, fill, pltpu.roll(x, d, 0))


def _shift_up(x, d, fill):
    n = x.shape[0]
    if d % SUBLANES == 0:
        return jnp.concatenate([x[d:], jnp.full((d, x.shape[1]), fill, x.dtype)], axis=0)
    row = lax.broadcasted_iota(jnp.int32, x.shape, 0)
    return jnp.where(row >= n - d, fill, pltpu.roll(x, n - d, 0))


def _scan(a, b, shift):
    d = 1
    while d < a.shape[0]:
        b = a * shift(b, d, 0.0) + b
        a = a * shift(a, d, 1.0)
        d *= 2
    return a, b


LANES = 128


def _scan_tile(a, b, outside, a_s, b_s, h_s, reverse):
    tm = a.shape[0]
    groups = tm // SUBLANES
    order = list(range(SUBLANES - 1, -1, -1) if reverse else range(SUBLANES))
    shift = _shift_up if reverse else _shift_down
    edge = groups - 1 if reverse else 0
    for j in range(D // LANES):
        a_s[j] = a[:, LANES * j:LANES * (j + 1)]
        b_s[j] = b[:, LANES * j:LANES * (j + 1)]
    for j in range(D // LANES):
        def slab(ref, k):
            return ref[j, pl.ds(k, groups, stride=SUBLANES), :]

        ga, gb = slab(a_s, order[0]), slab(b_s, order[0])
        for k in order[1:]:
            ak = slab(a_s, k)
            gb = ak * gb + slab(b_s, k)
            ga = ak * ga
        ga, gb = _scan(ga, gb, shift)
        h_out = outside[:, LANES * j:LANES * (j + 1)]
        group_end = ga * h_out + gb
        row = lax.broadcasted_iota(jnp.int32, (groups, LANES), 0)
        h = jnp.where(row == edge, h_out, shift(group_end, 1, 0.0))
        for k in order:
            h = slab(a_s, k) * h + slab(b_s, k)
            h_s[j, pl.ds(k, groups, stride=SUBLANES), :] = h
    return jnp.concatenate([h_s[j] for j in range(D // LANES)], axis=1)


def _dot(a, b):
    return jnp.dot(a, b, preferred_element_type=F32)


def _dot_nt(a, b):
    return lax.dot_general(a, b, (((1,), (1,)), ((), ())), preferred_element_type=F32)


def _dot_tn(a, b):
    return lax.dot_general(a, b, (((0,), (0,)), ((), ())), preferred_element_type=F32)


_ANY = pl.BlockSpec(memory_space=pl.ANY)


def _position():
    return lax.axis_index("x"), lax.axis_index("y"), lax.axis_index("c")


def _other_chips(x, y):
    return [(1 - x, y), (x, 1 - y), (1 - x, 1 - y)]


class _Side:
    def __init__(self, inputs, out_shapes, n_sems, make):
        self.inputs, self.out_shapes, self.n_sems, self.make = list(inputs), list(out_shapes), n_sems, make


MID_STEP = 0.625


def _call(body, *, name, grid, in_specs, out_specs, out_shape, args, scratch_shapes=(), sides=(), aliases=None):
    n_in, n_out, n_scr = len(in_specs), len(out_specs), len(scratch_shapes)
    side_in = [len(s.inputs) for s in sides]
    side_out = [len(s.out_shapes) for s in sides]

    def wrapped(*refs):
        refs = list(refs)
        take = lambda k: [refs.pop(0) for _ in range(k)]
        ins = take(n_in)
        sins = [take(k) for k in side_in]
        outs = take(n_out)
        souts = [take(k) for k in side_out]
        scr = take(n_scr)
        sems = [take(3) for _ in sides]
        def run(phase):
            for s, si, so, sem in zip(sides, sins, souts, sems):
                for thunk in s.make(si, so, *sem)[phase]:
                    thunk()

        if sides:
            n_steps = functools.reduce(lambda a, b: a * b, grid)
            step = functools.reduce(lambda a, b: a + b, [
                pl.program_id(a) * functools.reduce(lambda p, q: p * q, grid[a + 1:], 1) for a in range(len(grid))])
            pl.when(step == 0)(lambda: run(0))
        body(*ins, *outs, *scr)
        if sides:
            pl.when(step == int(MID_STEP * (n_steps - 1)))(lambda: run(1))
            pl.when(step == n_steps - 1)(lambda: run(2))

    res = pl.pallas_call(
        wrapped, name=name, grid=grid,
        in_specs=list(in_specs) + [_ANY] * sum(side_in),
        out_specs=list(out_specs) + [_ANY] * sum(side_out),
        out_shape=list(out_shape) + [o for s in sides for o in s.out_shapes],
        scratch_shapes=list(scratch_shapes) + [pltpu.SemaphoreType.DMA((s.n_sems,)) for s in sides for _ in range(3)],
        input_output_aliases=aliases or {},
        compiler_params=_params(len(grid)),
    )(*args, *[a for s in sides for a in s.inputs])
    main, rest, per_side = list(res[:n_out]), list(res[n_out:]), []
    for k in side_out:
        per_side.append(rest[:k])
        rest = rest[k:]
    return main, per_side


def _comm_only(sides, name):
    def body():
        pass

    return _call(body, name=name, grid=(1,), in_specs=[], out_specs=[], out_shape=[], args=[], sides=sides)[1]


def _remote(src, dst, send, recv, k, device):
    return pltpu.make_async_remote_copy(src_ref=src, dst_ref=dst, send_sem=send.at[k], recv_sem=recv.at[k],
                                        device_id=device, device_id_type=MESH)


def _both_ways(copy, keys):
    return [lambda k=k: copy(k).start() for k in keys], [], [lambda k=k: copy(k).wait() for k in keys]


def _gather_side(shards):
    n = len(shards)

    def make(ins, outs, send, recv, local):
        x, y, c = _position()
        mine = 2 * x + y
        chips = _other_chips(x, y)
        pairs = [(w, j) for w in range(n) for j in range(3)]

        def own(w):
            return pltpu.make_async_copy(ins[w], outs[w].at[mine], local.at[w])

        def push(w, j):
            return _remote(ins[w], outs[w].at[mine], send, recv, 3 * w + j, (*chips[j], c))

        def arrival(w, j):
            px, py = chips[j]
            return _remote(ins[w], outs[w].at[2 * px + py], send, recv, 3 * w + j, (px, py, c))

        starts = [lambda w=w: own(w).start() for w in range(n)] + [lambda w=w, j=j: push(w, j).start() for w, j in pairs]
        waits = ([lambda w=w, j=j: arrival(w, j).wait_recv() for w, j in pairs]
                 + [lambda w=w, j=j: push(w, j).wait_send() for w, j in pairs]
                 + [lambda w=w: own(w).wait() for w in range(n)])
        return starts, [], waits

    return _Side(shards, [S((N_SHARD,) + s.shape, s.dtype) for s in shards], 3 * n, make)


def _gather_half_side(shards):
    n = len(shards)

    def make(ins, outs, send, recv, local):
        x, y, c = _position()
        mine = 2 * x + y
        chips = _other_chips(x, y)
        pairs = [(w, j) for w in range(n) for j in range(3)]

        def rows(w, core):
            half = ins[w].shape[0] // 2
            return pl.ds(core * half, half)

        def own(w):
            return pltpu.make_async_copy(ins[w], outs[w].at[mine], local.at[w])

        def push(w, j):
            return _remote(ins[w].at[rows(w, c), :], outs[w].at[mine, rows(w, c), :], send, recv, 3 * w + j,
                           (*chips[j], c))

        def landed(w, j, core):
            px, py = chips[j]
            return outs[w].at[2 * px + py, rows(w, core), :]

        def arrival(w, j):
            return _remote(ins[w].at[rows(w, c), :], landed(w, j, c), send, recv, 3 * w + j, (*chips[j], c))

        def passed(w, j, core):
            return _remote(landed(w, j, core), landed(w, j, core), send, recv, 3 * n + 3 * w + j, (x, y, 1 - c))

        starts = [lambda w=w: own(w).start() for w in range(n)] + [lambda w=w, j=j: push(w, j).start() for w, j in pairs]
        mids = [t for w, j in pairs for t in (lambda w=w, j=j: arrival(w, j).wait_recv(),
                                              lambda w=w, j=j: passed(w, j, c).start())]
        waits = ([lambda w=w, j=j: passed(w, j, 1 - c).wait_recv() for w, j in pairs]
                 + [lambda w=w, j=j: passed(w, j, c).wait_send() for w, j in pairs]
                 + [lambda w=w, j=j: push(w, j).wait_send() for w, j in pairs]
                 + [lambda w=w: own(w).wait() for w in range(n)])
        return starts, mids, waits

    return _Side(shards, [S((N_SHARD,) + s.shape, s.dtype) for s in shards], 6 * n, make)


def _halves_side(grads):
    n = len(grads)

    def make(ins, outs, send, recv, local):
        x, y, c = _position()

        def copy(w):
            half = ins[w].shape[1] // 2
            return _remote(ins[w].at[:, pl.ds((1 - c) * half, half), :], outs[w], send, recv, w, (x, y, 1 - c))

        return _both_ways(copy, range(n))

    return _Side(grads, [S((N_SHARD, g.shape[1] // 2, g.shape[2]), F32) for g in grads], n, make)


def _scatter_side(partials):
    n = len(partials)

    def make(ins, outs, send, recv, local):
        x, y, c = _position()
        chips = _other_chips(x, y)

        def copy(k):
            w, j = divmod(k, 3)
            px, py = chips[j]
            return _remote(ins[w].at[2 * px + py], outs[w].at[j], send, recv, k, (px, py, c))

        return _both_ways(copy, range(3 * n))

    return _Side(partials, [S((3,) + p.shape[1:], p.dtype) for p in partials], 3 * n, make)


def _swap_side(halves):
    n = len(halves)

    def make(ins, outs, send, recv, local):
        x, y, c = _position()
        return _both_ways(lambda w: _remote(ins[w], outs[w], send, recv, w, (x, y, 1 - c)), range(n))

    return _Side(halves, [S(h.shape, h.dtype) for h in halves], n, make)


N_DEVICE = 8


def _everyone_side(arrays):
    n = len(arrays)
    peers = N_DEVICE - 1

    def make(ins, outs, send, recv, local):
        x, y, c = _position()
        mine = 4 * x + 2 * y + c
        pairs = [(w, k) for w in range(n) for k in range(1, N_DEVICE)]

        def peer(k):
            return (1 - x if k & 4 else x, 1 - y if k & 2 else y, 1 - c if k & 1 else c)

        def own(w):
            return pltpu.make_async_copy(ins[w], outs[w].at[mine], local.at[w])

        def push(w, k):
            return _remote(ins[w], outs[w].at[mine], send, recv, peers * w + k - 1, peer(k))

        def arrival(w, k):
            px, py, pc = peer(k)
            return _remote(ins[w], outs[w].at[4 * px + 2 * py + pc], send, recv, peers * w + k - 1, (px, py, pc))

        starts = [lambda w=w: own(w).start() for w in range(n)] + [lambda w=w, k=k: push(w, k).start() for w, k in pairs]
        waits = ([lambda w=w, k=k: arrival(w, k).wait_recv() for w, k in pairs]
                 + [lambda w=w, k=k: push(w, k).wait_send() for w, k in pairs]
                 + [lambda w=w: own(w).wait() for w in range(n)])
        return starts, [], waits

    return _Side(arrays, [S((N_DEVICE,) + a.shape, a.dtype) for a in arrays], peers * n, make)


def _inproj(x, g, w_in, sides=()):
    T = x.shape[0]
    tm = min(TM_FF, T)

    def body(x_ref, g_ref, w_ref, proj_ref, h_ref):
        xv = x_ref[...]
        h = (xv * _rms_stats(xv) * g_ref[...]).astype(BF16)
        h_ref[...] = h
        for k in range(N_SHARD):
            proj_ref[:, IN_SHARD * k:IN_SHARD * (k + 1)] = _dot(h, w_ref[k])

    return _call(
        body, name="inproj", grid=(T // tm,),
        in_specs=[pl.BlockSpec((tm, D), lambda i: (i, 0)), pl.BlockSpec((1, D), lambda i: (0, 0)),
                  _resident((N_SHARD, D, IN_SHARD))],
        out_specs=[pl.BlockSpec((tm, IN_COLS), lambda i: (i, 0)), pl.BlockSpec((tm, D), lambda i: (i, 0))],
        out_shape=[S((T, IN_COLS), F32), S((T, D), BF16)],
        args=(x, g, w_in), sides=sides)


def _rg_gates(xc, wa_ref, wx_ref, ba, bx, sp):
    xb = xc.astype(BF16)
    blocks = [xb[:, RG_BLOCK * j:RG_BLOCK * (j + 1)] for j in range(N_RG_BLOCK)]
    r = _sig(jnp.concatenate([_dot(blocks[j], wa_ref[j]) for j in range(N_RG_BLOCK)], axis=1) + ba)
    gi = _sig(jnp.concatenate([_dot(blocks[j], wx_ref[j]) for j in range(N_RG_BLOCK)], axis=1) + bx)
    log_a = (-RG_C) * r * sp
    a = jnp.exp(log_a)
    m = jnp.sqrt(_one_minus_exp(2.0 * log_a))
    return xb, r, gi, a, m


N_GATES = 4
HEADS_PER_BLOCK = 4
HEAD_DIM = RG_BLOCK // HEADS_PER_BLOCK
_RG_BLOCKS_BF16 = pltpu.VMEM((N_RG_BLOCK, RG_BLOCK, RG_BLOCK), BF16)


def _fill_blockdiag(heads_ref, blocks):
    blocks[...] = jnp.zeros_like(blocks)
    for j in range(N_RG_BLOCK):
        for h in range(HEADS_PER_BLOCK):
            sl = slice(HEAD_DIM * h, HEAD_DIM * (h + 1))
            blocks[j, sl, sl] = heads_ref[0, HEADS_PER_BLOCK * j + h].astype(BF16)


def _rglru_fwd(proj, conv_w, conv_b, rg_wa, rg_wx, ba, bx, lam, sides=()):
    T = proj.shape[0]
    tm = min(TM_SCAN, T)

    def body(rx_ref, gate_ref, cw_ref, cb_ref, wah_ref, wxh_ref, ba_ref, bx_ref, lam_ref,
             ya_ref, xc_ref, h_ref, gates_ref, ext, hc, a_s, b_s, h_s, wa_ref, wx_ref):
        @pl.when(pl.program_id(0) == 0)
        def _():
            ext[0:SUBLANES, :] = jnp.zeros((SUBLANES, D), F32)
            hc[...] = jnp.zeros((SUBLANES, D), F32)
            _fill_blockdiag(wah_ref, wa_ref)
            _fill_blockdiag(wxh_ref, wx_ref)

        ext[SUBLANES:SUBLANES + tm, :] = rx_ref[...]
        xc = cb_ref[...]
        for k in range(CONV_WIDTH):
            xc = xc + ext[pl.ds(SUBLANES - (CONV_WIDTH - 1) + k, tm), :] * cw_ref[k:k + 1, :]
        ext[0:SUBLANES, :] = ext[tm:tm + SUBLANES, :]
        xc_ref[...] = xc
        _, r, gi, a, m = _rg_gates(xc, wa_ref, wx_ref, ba_ref[...], bx_ref[...], _softplus_neg(lam_ref[...]))
        for k, val in enumerate((r, gi, a, m)):
            gates_ref[:, D * k:D * (k + 1)] = val
        h = _scan_tile(a, m * (gi * xc), hc[0:1, :], a_s, b_s, h_s, reverse=False)
        hc[...] = jnp.broadcast_to(h[tm - 1:tm, :], (SUBLANES, D))
        h_ref[...] = h
        ya_ref[...] = (_gelu(gate_ref[...]) * h).astype(BF16)

    vec = pl.BlockSpec((1, D), lambda i: (0, 0))
    heads = pl.BlockSpec(rg_wa.shape, lambda i: (0, 0, 0, 0))
    tile = pl.BlockSpec((tm, D), lambda i: (i, 0))
    return _call(
        body, name="rglru_fwd", grid=(T // tm,),
        in_specs=[pl.BlockSpec((tm, D), lambda i: (i, 0)), pl.BlockSpec((tm, D), lambda i: (i, 1)),
                  pl.BlockSpec((CONV_WIDTH, D), lambda i: (0, 0)), vec, heads, heads, vec, vec, vec],
        out_specs=[tile, tile, tile, pl.BlockSpec((tm, N_GATES * D), lambda i: (i, 0))],
        out_shape=[S((T, D), BF16), S((T, D), F32), S((T, D), F32), S((T, N_GATES * D), F32)],
        scratch_shapes=[pltpu.VMEM((tm + SUBLANES, D), F32), pltpu.VMEM((SUBLANES, D), F32)]
        + [pltpu.VMEM((D // LANES, tm, LANES), F32)] * 3 + [_RG_BLOCKS_BF16] * 2,
        args=(proj, proj, conv_w, conv_b, rg_wa, rg_wx, ba, bx, lam), sides=sides)


def _layer_norm_stats(v):
    mu = jnp.mean(v, axis=-1, keepdims=True)
    vc = v - mu
    rstd = lax.rsqrt(jnp.mean(vc * vc, axis=-1, keepdims=True) + EPS)
    return vc * rstd, rstd


def _sgu_mix(w_ref, vnb, bst_ref, n_chunk):
    cols = []
    for g in range(N_GROUP):
        vg = vnb[:, CHUNK * g:CHUNK * (g + 1)].reshape(n_chunk, CHUNK, CHUNK)
        wb = jnp.broadcast_to(w_ref[g][None], (n_chunk, CHUNK, CHUNK))
        mg = lax.dot_general(wb, vg, (((2,), (1,)), ((0,), (0,))), preferred_element_type=F32)
        mg = mg + bst_ref[:, g:g + 1][None]
        cols.append(mg.reshape(n_chunk * CHUNK, CHUNK))
    return jnp.concatenate(cols, axis=1)


def _causal_mask():
    return (lax.broadcasted_iota(jnp.int32, (CHUNK, CHUNK), 0) >= lax.broadcasted_iota(jnp.int32, (CHUNK, CHUNK), 1))


def _fill_sgu_weights(ws_ref, bs_ref, w_tril, bs_t, w_tril_t=None):
    keep = _causal_mask()
    for g in range(N_GROUP):
        wg = jnp.where(keep, ws_ref[0, g], 0.0)
        w_tril[g] = wg.astype(BF16)
        if w_tril_t is not None:
            w_tril_t[g] = wg.T.astype(BF16)
    bs_t[...] = bs_ref[0].T


_SGU_W_BF16 = pltpu.VMEM((N_GROUP, CHUNK, CHUNK), BF16)
_SGU_BT = pltpu.VMEM((CHUNK, N_GROUP), F32)


def _sgu_fwd(proj, ln_g, ln_b, sgu_ws, sgu_bs):
    T = proj.shape[0]
    tm = min(TM_MM, T)
    n_chunk = tm // CHUNK

    def body(u_ref, v_ref, g_ref, b_ref, ws_ref, bs_ref, yb_ref, w_ref, bst_ref):
        @pl.when(pl.program_id(0) == 0)
        def _():
            _fill_sgu_weights(ws_ref, bs_ref, w_ref, bst_ref)

        vhat, _ = _layer_norm_stats(_gelu(v_ref[...]))
        vnb = (vhat * g_ref[...] + b_ref[...]).astype(BF16)
        yb_ref[...] = (_gelu(u_ref[...]) * _sgu_mix(w_ref, vnb, bst_ref, n_chunk)).astype(BF16)

    vec = pl.BlockSpec((1, D), lambda i: (0, 0))
    return pl.pallas_call(
        body, name="sgu_fwd", grid=(T // tm,),
        in_specs=[pl.BlockSpec((tm, D), lambda i: (i, 2)), pl.BlockSpec((tm, D), lambda i: (i, 3)), vec, vec,
                  pl.BlockSpec(sgu_ws.shape, lambda i: (0, 0, 0, 0)), pl.BlockSpec(sgu_bs.shape, lambda i: (0, 0, 0))],
        out_specs=pl.BlockSpec((tm, D), lambda i: (i, 0)),
        out_shape=S((T, D), BF16),
        scratch_shapes=[_SGU_W_BF16, _SGU_BT],
        compiler_params=_params(1),
    )(proj, proj, ln_g, ln_b, sgu_ws, sgu_bs)


def _merge_fwd(x, proj, ya, yb, wpa, wpb, wout):
    T = x.shape[0]
    tm = min(TM_MM, T)

    def body(x_ref, ga_ref, gb_ref, ya_ref, yb_ref, wpa_ref, wpb_ref, wout_ref, pa_ref, pb_ref, mb_ref, x1_ref):
        pa = _dot(ya_ref[...], wpa_ref[...])
        pb = _dot(yb_ref[...], wpb_ref[...])
        pa_ref[...] = pa
        pb_ref[...] = pb
        mb = (_sig(ga_ref[...]) * pa + _sig(gb_ref[...]) * pb).astype(BF16)
        mb_ref[...] = mb
        x1_ref[...] = x_ref[...] + _dot(mb, wout_ref[...])

    tile = pl.BlockSpec((tm, D), lambda i: (i, 0))
    w = _resident((D, D))
    return pl.pallas_call(
        body, name="merge_fwd", grid=(T // tm,),
        in_specs=[tile, pl.BlockSpec((tm, D), lambda i: (i, 4)), pl.BlockSpec((tm, D), lambda i: (i, 5)),
                  tile, tile, w, w, w],
        out_specs=[tile, tile, tile, tile],
        out_shape=[S((T, D), F32), S((T, D), F32), S((T, D), BF16), S((T, D), F32)],
        compiler_params=_params(1),
    )(x, proj, proj, ya, yb, wpa, wpb, wout)


def _ffn_up(x1, g, w_gu):
    T = x1.shape[0]
    tm = min(TM_FF, T)

    def body(x_ref, g_ref, w_ref, h2_ref, gu_ref, act_ref):
        xv = x_ref[...]
        h2 = (xv * _rms_stats(xv) * g_ref[...]).astype(BF16)
        h2_ref[...] = h2
        for k in range(N_SHARD // 2):
            cols = slice(FF_SHARD * k, FF_SHARD * (k + 1))
            gate = _dot(h2, w_ref[k])
            up = _dot(h2, w_ref[k + N_SHARD // 2])
            gu_ref[:, cols] = gate.astype(BF16)
            gu_ref[:, D_FF + FF_SHARD * k:D_FF + FF_SHARD * (k + 1)] = up.astype(BF16)
            act_ref[:, cols] = (gate * _sig(gate) * up).astype(BF16)

    return pl.pallas_call(
        body, name="ffn_up", grid=(T // tm,),
        in_specs=[pl.BlockSpec((tm, D), lambda i: (i, 0)), pl.BlockSpec((1, D), lambda i: (0, 0)),
                  _resident((N_SHARD, D, FF_SHARD))],
        out_specs=[pl.BlockSpec((tm, D), lambda i: (i, 0)), pl.BlockSpec((tm, 2 * D_FF), lambda i: (i, 0)),
                   pl.BlockSpec((tm, D_FF), lambda i: (i, 0))],
        out_shape=[S((T, D), BF16), S((T, 2 * D_FF), BF16), S((T, D_FF), BF16)],
        compiler_params=_params(1),
    )(x1, g, w_gu)


def _ffn_down_loss(x1, act, w_down, g_final, target):
    T = x1.shape[0]
    tm = min(TM_MM, T)

    def body(x_ref, act_ref, w_ref, g_ref, t_ref, dx2_ref, dx2b_ref, loss_ref, dg_ref):
        @pl.when(pl.program_id(0) == 0)
        def _():
            loss_ref[...] = jnp.zeros_like(loss_ref)
            dg_ref[...] = jnp.zeros_like(dg_ref)

        x2 = x_ref[...] + _dot(act_ref[...], w_ref[...])
        gf = g_ref[...]
        err = x2 * _rms_stats(x2) * gf - t_ref[...]
        loss_ref[...] += 0.5 * jnp.sum(jnp.mean(err * err, axis=-1, keepdims=True), axis=0, keepdims=True)
        dx2, dg_rows = _rms_bwd(err * (1.0 / D), x2, gf)
        dg_ref[...] += _colsum(dg_rows)
        dx2_ref[...] = dx2
        dx2b_ref[...] = dx2.astype(BF16)

    tile = pl.BlockSpec((tm, D), lambda i: (i, 0))
    vec = pl.BlockSpec((1, D), lambda i: (0, 0))
    return pl.pallas_call(
        body, name="ffn_down_loss", grid=(T // tm,),
        in_specs=[tile, pl.BlockSpec((tm, D_FF), lambda i: (i, 0)), _resident((D_FF, D)), vec, tile],
        out_specs=[tile, tile, pl.BlockSpec((1, 1), lambda i: (0, 0)), vec],
        out_shape=[S((T, D), F32), S((T, D), BF16), S((1, 1), F32), S((1, D), F32)],
        compiler_params=_params(1),
    )(x1, act, w_down, g_final, target)


def _ffn_bwd_act(dx2b, gu, w_down):
    T = dx2b.shape[0]
    tm = min(TM_FF, T)

    def body(dx_ref, gu_ref, w_ref, dgu_ref):
        dact = _dot_nt(dx_ref[...], w_ref[...])
        gate = gu_ref[:, 0:D_FF].astype(F32)
        sg = _sig(gate)
        up = gu_ref[:, D_FF:2 * D_FF].astype(F32)
        dgu_ref[:, 0:D_FF] = (dact * up * (sg * (1.0 + gate * (1.0 - sg)))).astype(BF16)
        dgu_ref[:, D_FF:2 * D_FF] = (dact * (gate * sg)).astype(BF16)

    return pl.pallas_call(
        body, name="ffn_bwd_act", grid=(T // tm,),
        in_specs=[pl.BlockSpec((tm, D), lambda i: (i, 0)), pl.BlockSpec((tm, 2 * D_FF), lambda i: (i, 0)),
                  _resident((D_FF, D))],
        out_specs=pl.BlockSpec((tm, 2 * D_FF), lambda i: (i, 0)),
        out_shape=S((T, 2 * D_FF), BF16),
        compiler_params=_params(1),
    )(dx2b, gu, w_down)


def _ffn_bwd_in(dgu, w_gu, x1, g, dx2):
    T = x1.shape[0]
    tm = min(TM_MM, T)

    def body(dgu_ref, w_ref, x_ref, g_ref, dx2_ref, dx1_ref, dx1b_ref, dg_ref):
        @pl.when(pl.program_id(0) == 0)
        def _():
            dg_ref[...] = jnp.zeros_like(dg_ref)

        dh2 = _dot_nt(dgu_ref[:, 0:FF_SHARD], w_ref[0])
        for k in range(1, N_SHARD):
            dh2 = dh2 + _dot_nt(dgu_ref[:, FF_SHARD * k:FF_SHARD * (k + 1)], w_ref[k])
        dx, dg_rows = _rms_bwd(dh2, x_ref[...], g_ref[...])
        dg_ref[...] += _colsum(dg_rows)
        dx1 = dx2_ref[...] + dx
        dx1_ref[...] = dx1
        dx1b_ref[...] = dx1.astype(BF16)

    tile = pl.BlockSpec((tm, D), lambda i: (i, 0))
    vec = pl.BlockSpec((1, D), lambda i: (0, 0))
    return pl.pallas_call(
        body, name="ffn_bwd_in", grid=(T // tm,),
        in_specs=[pl.BlockSpec((tm, 2 * D_FF), lambda i: (i, 0)), _resident((N_SHARD, D, FF_SHARD)), tile, vec, tile],
        out_specs=[tile, tile, vec],
        out_shape=[S((T, D), F32), S((T, D), BF16), S((1, D), F32)],
        compiler_params=_params(1),
    )(dgu, w_gu, x1, g, dx2)


def _matmul_tn(a, b, tn, shard_major, name, sides=()):
    T, M = a.shape
    N = b.shape[1]
    tk = min(TK_WGRAD, T)

    def body(a_ref, b_ref, o_ref):
        @pl.when(pl.program_id(1) == 0)
        def _():
            o_ref[...] = jnp.zeros_like(o_ref)

        acc = _dot_tn(a_ref[...], b_ref[...])
        if shard_major:
            o_ref[0] += acc
        else:
            o_ref[...] += acc

    if shard_major:
        out_spec, out_shape = pl.BlockSpec((1, M, tn), lambda j, k: (j, 0, 0)), S((N // tn, M, tn), F32)
    else:
        out_spec, out_shape = pl.BlockSpec((M, tn), lambda j, k: (0, j)), S((M, N), F32)
    (out,), side_outs = _call(
        body, name=name, grid=(N // tn, T // tk),
        in_specs=[pl.BlockSpec((tk, M), lambda j, k: (k, 0)), pl.BlockSpec((tk, tn), lambda j, k: (k, j))],
        out_specs=[out_spec], out_shape=[out_shape], args=(a, b), sides=sides)
    return out, side_outs


def _merge_bwd(dx1b, proj, pa, pb, wpa, wpb, wout, sides=()):
    T = dx1b.shape[0]
    tm = min(TM_MM, T)

    def body(dx_ref, ga_ref, gb_ref, pa_ref, pb_ref, wpa_ref, wpb_ref, wout_ref,
             dpa_ref, dpb_ref, dgate_ref, dya_ref, dyb_ref):
        dm = _dot_nt(dx_ref[...], wout_ref[...])
        sa = _sig(ga_ref[...])
        sb = _sig(gb_ref[...])
        dpa = (dm * sa).astype(BF16)
        dpb = (dm * sb).astype(BF16)
        dpa_ref[...] = dpa
        dpb_ref[...] = dpb
        dgate_ref[:, 0:D] = (dm * pa_ref[...] * (sa * (1.0 - sa))).astype(BF16)
        dgate_ref[:, D:2 * D] = (dm * pb_ref[...] * (sb * (1.0 - sb))).astype(BF16)
        dya_ref[...] = _dot_nt(dpa, wpa_ref[...])
        dyb_ref[...] = _dot_nt(dpb, wpb_ref[...])

    tile = pl.BlockSpec((tm, D), lambda i: (i, 0))
    w = _resident((D, D))
    return _call(
        body, name="merge_bwd", grid=(T // tm,),
        in_specs=[tile, pl.BlockSpec((tm, D), lambda i: (i, 4)), pl.BlockSpec((tm, D), lambda i: (i, 5)),
                  tile, tile, w, w, w],
        out_specs=[tile, tile, pl.BlockSpec((tm, 2 * D), lambda i: (i, 2)), tile, tile],
        out_shape=[S((T, D), BF16), S((T, D), BF16), S((T, IN_COLS), BF16), S((T, D), F32), S((T, D), F32)],
        args=(dx1b, proj, proj, pa, pb, wpa, wpb, wout), sides=sides)


def _rglru_bwd(dya, dproj, proj, hseq, xc, gates, conv_w, rg_wa, rg_wx, lam, sides=()):
    T = dya.shape[0]
    tm = min(TM_SCAN, T)
    n = T // tm
    per8 = tm // SUBLANES

    def body(dya_ref, _, rx_ref, rxp_ref, gate_ref, h_ref, hp_ref, xc_ref, gates_ref, cw_ref, wah_ref, wxh_ref,
             lam_ref, dab_ref, dcw_ref, dcb_ref, dwah_ref, dwxh_ref, dba_ref, dbx_ref, dlam_ref,
             hext, rext, dext, carry_a, carry_dh, a_s, b_s, h_s, wa_ref, wx_ref, dwa_ref, dwx_ref):
        i = pl.program_id(0)
        first_tile = i == n - 1

        @pl.when(i == 0)
        def _():
            for ref in (dcw_ref, dcb_ref, dwa_ref, dwx_ref, dba_ref, dbx_ref, dlam_ref, carry_a, carry_dh):
                ref[...] = jnp.zeros_like(ref)
            dext[tm:tm + SUBLANES, :] = jnp.zeros((SUBLANES, D), F32)
            _fill_blockdiag(wah_ref, wa_ref)
            _fill_blockdiag(wxh_ref, wx_ref)

        gel, dgel = _gelu_and_grad(gate_ref[...])
        dya_v = dya_ref[...]
        hseq_v = h_ref[...]
        dgate = dya_v * hseq_v * dgel
        xcv = xc_ref[...]
        lam_v = lam_ref[...]
        sp = _softplus_neg(lam_v)
        xb = xcv.astype(BF16)
        r, gi, a, m = (gates_ref[:, D * k:D * (k + 1)] for k in range(N_GATES))

        row = lax.broadcasted_iota(jnp.int32, (tm, D), 0)
        c = jnp.where(row == tm - 1, carry_a[0:1, :], _shift_up(a, 1, 0.0))
        dH = _scan_tile(c, dya_v * gel, carry_dh[0:1, :], a_s, b_s, h_s, reverse=True)
        carry_a[...] = jnp.broadcast_to(a[0:1, :], (SUBLANES, D))
        carry_dh[...] = jnp.broadcast_to(dH[0:1, :], (SUBLANES, D))

        hext[0:SUBLANES, :] = jnp.where(first_tile, 0.0, hp_ref[...])
        hext[SUBLANES:SUBLANES + tm, :] = hseq_v
        h_prev = hext[pl.ds(SUBLANES - 1, tm), :]

        d_m = dH * (gi * xcv)
        d_la = dH * h_prev * a - d_m * (a * a) / m
        d_ia = dH * m * xcv * (gi * (1.0 - gi))
        d_ra = d_la * ((-RG_C) * sp) * (r * (1.0 - r))
        dlam_ref[...] += _colsum(d_la * ((-RG_C) * r)) * (-_sig(-lam_v))
        dba_ref[...] += _colsum(d_ra)
        dbx_ref[...] += _colsum(d_ia)
        drab = d_ra.astype(BF16)
        diab = d_ia.astype(BF16)
        dxc_cols = []
        for j in range(N_RG_BLOCK):
            sl = slice(RG_BLOCK * j, RG_BLOCK * (j + 1))
            dxc_cols.append(_dot_nt(drab[:, sl], wa_ref[j]) + _dot_nt(diab[:, sl], wx_ref[j]))
            dwa_ref[j] += _dot_tn(xb[:, sl], drab[:, sl])
            dwx_ref[j] += _dot_tn(xb[:, sl], diab[:, sl])
        dxc = dH * m * gi + jnp.concatenate(dxc_cols, axis=1)

        dcb_ref[...] += _colsum(dxc)
        dext[0:tm, :] = dxc
        rext[0:SUBLANES, :] = jnp.where(first_tile, 0.0, rxp_ref[...])
        rext[SUBLANES:SUBLANES + tm, :] = rx_ref[...]
        drx = jnp.zeros((tm, D), F32)
        for k in range(CONV_WIDTH):
            drx = drx + dext[pl.ds(CONV_WIDTH - 1 - k, tm), :] * cw_ref[k:k + 1, :]
            dcw_ref[k:k + 1, :] += _colsum(dxc * rext[pl.ds(SUBLANES - (CONV_WIDTH - 1) + k, tm), :])
        dext[tm:tm + SUBLANES, :] = dext[0:SUBLANES, :]
        dab_ref[:, 0:D] = drx.astype(BF16)
        dab_ref[:, D:2 * D] = dgate.astype(BF16)

        @pl.when(first_tile)
        def _():
            for j in range(N_RG_BLOCK):
                for h in range(HEADS_PER_BLOCK):
                    sl = slice(HEAD_DIM * h, HEAD_DIM * (h + 1))
                    dwah_ref[HEADS_PER_BLOCK * j + h] = dwa_ref[j, sl, sl]
                    dwxh_ref[HEADS_PER_BLOCK * j + h] = dwx_ref[j, sl, sl]

    def rev(col):
        return lambda i: (n - 1 - i, col)

    def prev8(col):
        return lambda i: (jnp.maximum((n - 1 - i) * per8 - 1, 0), col)

    tile = pl.BlockSpec((tm, D), rev(0))
    vec = pl.BlockSpec((1, D), lambda i: (0, 0))
    heads_in = pl.BlockSpec(rg_wa.shape, lambda i: (0, 0, 0, 0))
    heads_out = pl.BlockSpec(rg_wa.shape[1:], lambda i: (0, 0, 0))
    cw = pl.BlockSpec((CONV_WIDTH, D), lambda i: (0, 0))
    blocks_f32 = pltpu.VMEM((N_RG_BLOCK, RG_BLOCK, RG_BLOCK), F32)
    return _call(
        body, name="rglru_bwd", grid=(n,),
        in_specs=[tile, _ANY, pl.BlockSpec((tm, D), rev(0)), pl.BlockSpec((SUBLANES, D), prev8(0)),
                  pl.BlockSpec((tm, D), rev(1)), tile, pl.BlockSpec((SUBLANES, D), prev8(0)), tile,
                  pl.BlockSpec((tm, N_GATES * D), rev(0)), cw, heads_in, heads_in, vec],
        out_specs=[pl.BlockSpec((tm, 2 * D), rev(0)), cw, vec, heads_out, heads_out, vec, vec, vec],
        out_shape=[S((T, IN_COLS), BF16), S((CONV_WIDTH, D), F32), S((1, D), F32),
                   S(rg_wa.shape[1:], F32), S(rg_wa.shape[1:], F32), S((1, D), F32), S((1, D), F32), S((1, D), F32)],
        scratch_shapes=[pltpu.VMEM((tm + SUBLANES, D), F32), pltpu.VMEM((tm + SUBLANES, D), F32),
                        pltpu.VMEM((tm + SUBLANES, D), F32), pltpu.VMEM((SUBLANES, D), F32),
                        pltpu.VMEM((SUBLANES, D), F32)] + [pltpu.VMEM((D // LANES, tm, LANES), F32)] * 3
        + [_RG_BLOCKS_BF16] * 2 + [blocks_f32] * 2,
        args=(dya, dproj, proj, proj, proj, hseq, hseq, xc, gates, conv_w, rg_wa, rg_wx, lam), sides=sides,
        aliases={1: 0})


def _sgu_bwd(dyb, dproj, proj, ln_g, ln_b, sgu_ws, sgu_bs, sides=()):
    T = dyb.shape[0]
    tm = min(TM_MM, T)
    n_chunk = tm // CHUNK

    def body(dyb_ref, _, u_ref, v_ref, g_ref, b_ref, ws_ref, bs_ref,
             duv_ref, dw_ref, dbs_ref, dg_ref, db_ref, w_ref, wt_ref, bst_ref):
        @pl.when(pl.program_id(0) == 0)
        def _():
            for ref in (dw_ref, dbs_ref, dg_ref, db_ref):
                ref[...] = jnp.zeros_like(ref)
            _fill_sgu_weights(ws_ref, bs_ref, w_ref, bst_ref, wt_ref)

        gu, dgu = _gelu_and_grad(u_ref[...])
        gv, dgv = _gelu_and_grad(v_ref[...])
        vhat, rstd = _layer_norm_stats(gv)
        lng = g_ref[...]
        vnb = (vhat * lng + b_ref[...]).astype(BF16)
        mixed = _sgu_mix(w_ref, vnb, bst_ref, n_chunk)
        dyb_v = dyb_ref[...]
        duv_ref[:, 0:D] = (dyb_v * mixed * dgu).astype(BF16)
        dmix = dyb_v * gu
        dmb = dmix.astype(BF16)
        keep = _causal_mask()
        dvn_cols, dbs_rows = [], []
        for g in range(N_GROUP):
            sl = slice(CHUNK * g, CHUNK * (g + 1))
            dmg = dmb[:, sl].reshape(n_chunk, CHUNK, CHUNK)
            vg = vnb[:, sl].reshape(n_chunk, CHUNK, CHUNK)
            wtb = jnp.broadcast_to(wt_ref[g][None], (n_chunk, CHUNK, CHUNK))
            dvn = lax.dot_general(wtb, dmg, (((2,), (1,)), ((0,), (0,))), preferred_element_type=F32)
            dvn_cols.append(dvn.reshape(tm, CHUNK))
            dw = lax.dot_general(dmg, vg, (((2,), (2,)), ((0,), (0,))), preferred_element_type=F32)
            dw_ref[g] += jnp.where(keep, jnp.sum(dw, axis=0), 0.0)
            per_token = jnp.sum(dmix[:, sl], axis=1)
            dbs_rows.append(jnp.sum(per_token.reshape(n_chunk, CHUNK), axis=0, keepdims=True))
        dbs_ref[...] += jnp.concatenate(dbs_rows, axis=0)
        dvn = jnp.concatenate(dvn_cols, axis=1)
        dg_ref[...] += _colsum(dvn * vhat)
        db_ref[...] += _colsum(dvn)
        dvhat = dvn * lng
        dgv_in = rstd * (dvhat - jnp.mean(dvhat, axis=-1, keepdims=True)
                         - vhat * jnp.mean(dvhat * vhat, axis=-1, keepdims=True))
        duv_ref[:, D:2 * D] = (dgv_in * dgv).astype(BF16)

    tile = pl.BlockSpec((tm, D), lambda i: (i, 0))
    vec = pl.BlockSpec((1, D), lambda i: (0, 0))
    wsp = pl.BlockSpec((N_GROUP, CHUNK, CHUNK), lambda i: (0, 0, 0))
    bsp = pl.BlockSpec((N_GROUP, CHUNK), lambda i: (0, 0))
    return _call(
        body, name="sgu_bwd", grid=(T // tm,),
        in_specs=[tile, _ANY, pl.BlockSpec((tm, D), lambda i: (i, 2)), pl.BlockSpec((tm, D), lambda i: (i, 3)),
                  vec, vec, pl.BlockSpec(sgu_ws.shape, lambda i: (0, 0, 0, 0)),
                  pl.BlockSpec(sgu_bs.shape, lambda i: (0, 0, 0))],
        out_specs=[pl.BlockSpec((tm, 2 * D), lambda i: (i, 1)), wsp, bsp, vec, vec],
        out_shape=[S((T, IN_COLS), BF16), S((N_GROUP, CHUNK, CHUNK), F32), S((N_GROUP, CHUNK), F32),
                   S((1, D), F32), S((1, D), F32)],
        scratch_shapes=[_SGU_W_BF16, _SGU_W_BF16, _SGU_BT],
        args=(dyb, dproj, proj, proj, ln_g, ln_b, sgu_ws, sgu_bs), sides=sides, aliases={1: 0})


def _inproj_bwd(dproj, w_in, x, g, dx1):
    T = x.shape[0]
    tm = min(TM_MM, T)

    def body(dp_ref, w_ref, x_ref, g_ref, dx1_ref, dx_ref, dgm_ref):
        @pl.when(pl.program_id(0) == 0)
        def _():
            dgm_ref[...] = jnp.zeros_like(dgm_ref)

        dh = _dot_nt(dp_ref[:, 0:IN_SHARD], w_ref[0])
        for k in range(1, N_SHARD):
            dh = dh + _dot_nt(dp_ref[:, IN_SHARD * k:IN_SHARD * (k + 1)], w_ref[k])
        dx, dg_rows = _rms_bwd(dh, x_ref[...], g_ref[...])
        dgm_ref[...] += _colsum(dg_rows)
        dx_ref[...] = dx1_ref[...] + dx

    tile = pl.BlockSpec((tm, D), lambda i: (i, 0))
    vec = pl.BlockSpec((1, D), lambda i: (0, 0))
    return pl.pallas_call(
        body, name="inproj_bwd", grid=(T // tm,),
        in_specs=[pl.BlockSpec((tm, IN_COLS), lambda i: (i, 0)), _resident((N_SHARD, D, IN_SHARD)), tile, vec, tile],
        out_specs=[tile, vec],
        out_shape=[S((T, D), F32), S((1, D), F32)],
        compiler_params=_params(1),
    )(dproj, w_in, x, g, dx1)


def _row_tile(rows):
    for t in range(256, 0, -SUBLANES):
        if rows % t == 0:
            return t
    raise ValueError(rows)


def _add_halves(core, g, theirs, name):
    _, r, cols = g.shape
    half = r // 2
    tr = _row_tile(half)
    nb = half // tr

    def body(core_ref, g_ref, t_ref, o32_ref, o16_ref):
        s = g_ref[...] + t_ref[...]
        o32_ref[...] = s
        o16_ref[...] = s.astype(BF16)

    blk = pl.BlockSpec((1, tr, cols), lambda s, i, core_ref: (s, i, 0))
    gs = pltpu.PrefetchScalarGridSpec(
        num_scalar_prefetch=1, grid=(N_SHARD, nb),
        in_specs=[pl.BlockSpec((1, tr, cols), lambda s, i, core_ref: (s, core_ref[0] * nb + i, 0)), blk],
        out_specs=[blk, blk])
    return pl.pallas_call(
        body, name=name, grid_spec=gs,
        out_shape=[S((N_SHARD, half, cols), F32), S((N_SHARD, half, cols), BF16)],
        compiler_params=_params(2),
    )(core, g, theirs)


def _sum_shards(chip, own, others, name):
    _, half, cols = own.shape
    tr = _row_tile(half)

    def body(chip_ref, own_ref, oth_ref, o_ref):
        acc = own_ref[0]
        for j in range(3):
            acc = acc + oth_ref[j].astype(F32)
        o_ref[...] = acc

    gs = pltpu.PrefetchScalarGridSpec(
        num_scalar_prefetch=1, grid=(half // tr,),
        in_specs=[pl.BlockSpec((1, tr, cols), lambda i, chip_ref: (chip_ref[0], i, 0)),
                  pl.BlockSpec((3, tr, cols), lambda i, chip_ref: (0, i, 0))],
        out_specs=pl.BlockSpec((tr, cols), lambda i, chip_ref: (i, 0)))
    return pl.pallas_call(
        body, name=name, grid_spec=gs, out_shape=S((half, cols), F32), compiler_params=_params(1),
    )(chip, own, others)


def _adamw(w, g, m, v):
    m = ADAM_B1 * m + (1.0 - ADAM_B1) * g
    v = ADAM_B2 * v + (1.0 - ADAM_B2) * (g * g)
    m_hat = m / (1.0 - ADAM_B1 ** ADAM_STEP)
    v_hat = v / (1.0 - ADAM_B2 ** ADAM_STEP)
    delta = -ADAM_LR * (m_hat / (jnp.sqrt(v_hat) + ADAM_EPS) + ADAM_WD * w)
    return delta, m, v


def _adamw_shard(core, mine, theirs, w, m, v, name):
    r, cols = w.shape
    half = r // 2
    tr = _row_tile(half)
    nb = half // tr

    def body(core_ref, mine_ref, theirs_ref, w_ref, m_ref, v_ref, g_ref, d_ref, mo_ref, vo_ref):
        g = jnp.where(pl.program_id(0) == core_ref[0], mine_ref[...], theirs_ref[...])
        g_ref[...] = g
        d_ref[...], mo_ref[...], vo_ref[...] = _adamw(w_ref[...], g, m_ref[...], v_ref[...])

    hblk = pl.BlockSpec((tr, cols), lambda h, i, core_ref: (i, 0))
    blk = pl.BlockSpec((tr, cols), lambda h, i, core_ref: (h * nb + i, 0))
    gs = pltpu.PrefetchScalarGridSpec(num_scalar_prefetch=1, grid=(2, nb),
                                      in_specs=[hblk, hblk, blk, blk, blk], out_specs=[blk] * 4)
    return pl.pallas_call(
        body, name=name, grid_spec=gs, out_shape=[S((r, cols), F32)] * 4, compiler_params=_params(2),
    )(core, mine, theirs, w, m, v)


def _adamw_whole(w, g, m, v, name):
    def body(w_ref, g_ref, m_ref, v_ref, d_ref, mo_ref, vo_ref):
        d_ref[...], mo_ref[...], vo_ref[...] = _adamw(w_ref[...], g_ref[...], m_ref[...], v_ref[...])

    return pl.pallas_call(body, name=name, out_shape=[S(w.shape, F32)] * 3)(w, g, m, v)


_VEC_ROWS = ("norm_mix_g", "conv_b", "rg_lambda", "sgu_ln_g", "sgu_ln_b", "norm_ffn_g", "norm_final_g", "rg_ba",
             "rg_bx")
_CONV_ROW = len(_VEC_ROWS)
_LOSS_ROW = _CONV_ROW + CONV_WIDTH
_VEC_PAD = -(_LOSS_ROW + 1) % SUBLANES
_HEAD_BIASES = ("rg_ba", "rg_bx")
_TENSORS = ("sgu_bs", "rg_wa", "rg_wx", "sgu_ws")


def _small_sum_adamw(parts, w, m, v):
    names = [n for n in _VEC_ROWS] + list(_TENSORS)
    n_parts = len(parts)

    def total(ref):
        acc = ref[0]
        for k in range(1, N_DEVICE):
            acc = acc + ref[k]
        return acc

    def body(*refs):
        part_refs, refs = refs[:n_parts], refs[n_parts:]
        w_refs, m_refs, v_refs = (dict(zip(names, refs[k * len(names):(k + 1) * len(names)])) for k in range(3))
        outs = refs[3 * len(names):]
        out_refs = {n: outs[4 * k:4 * k + 4] for k, n in enumerate(names)}
        conv_ref, loss_ref = outs[4 * len(names):]
        vec = total(part_refs[0])
        grads = {n: total(p)[None] for n, p in zip(_TENSORS, part_refs[1:])}
        for row, n in enumerate(_VEC_ROWS):
            g = vec[row:row + 1, :]
            if n in _HEAD_BIASES:
                g = jnp.concatenate([g[:, HEAD_DIM * h:HEAD_DIM * (h + 1)] for h in range(D // HEAD_DIM)], axis=0)[None]
            grads[n] = g
        for n in names:
            g_ref, d_ref, mo_ref, vo_ref = out_refs[n]
            g_ref[...] = grads[n]
            d_ref[...], mo_ref[...], vo_ref[...] = _adamw(w_refs[n][...], grads[n], m_refs[n][...], v_refs[n][...])
        conv_ref[...] = vec[_CONV_ROW:_CONV_ROW + CONV_WIDTH, :]
        loss_ref[...] = vec[_LOSS_ROW:_LOSS_ROW + 1, 0:1]

    res = pl.pallas_call(
        body, name="small_sum_adamw",
        out_shape=[S(w[n].shape, F32) for n in names for _ in range(4)] + [S((CONV_WIDTH, D), F32), S((1, 1), F32)],
        compiler_params=pltpu.CompilerParams(vmem_limit_bytes=VMEM_LIMIT),
    )(*parts, *[w[n] for n in names], *[m[n] for n in names], *[v[n] for n in names])
    return {n: tuple(res[4 * k:4 * k + 4]) for k, n in enumerate(names)}, res[-2], res[-1]


_BIG = ("w_in", "w_proj_a", "w_proj_b", "w_out", "w_gate_up", "w_down")
_WEIGHTS = ("norm_mix_g", "w_in", "conv_w", "conv_b", "rg_wa", "rg_ba", "rg_wx", "rg_bx", "rg_lambda", "sgu_ln_g",
            "sgu_ln_b", "sgu_ws", "sgu_bs", "w_proj_a", "w_proj_b", "w_out", "norm_ffn_g", "w_gate_up", "w_down",
            "norm_final_g")


def kernel(x, norm_mix_g, w_in, conv_w, conv_b, rg_wa, rg_ba, rg_wx, rg_bx, rg_lambda, sgu_ln_g, sgu_ln_b, sgu_ws, sgu_bs, w_proj_a, w_proj_b, w_out, norm_ffn_g, w_gate_up, w_down, norm_final_g, loss_target, m_norm_mix_g, m_w_in, m_conv_w, m_conv_b, m_rg_wa, m_rg_ba, m_rg_wx, m_rg_bx, m_rg_lambda, m_sgu_ln_g, m_sgu_ln_b, m_sgu_ws, m_sgu_bs, m_w_proj_a, m_w_proj_b, m_w_out, m_norm_ffn_g, m_w_gate_up, m_w_down, m_norm_final_g, v_norm_mix_g, v_w_in, v_conv_w, v_conv_b, v_rg_wa, v_rg_ba, v_rg_wx, v_rg_bx, v_rg_lambda, v_sgu_ln_g, v_sgu_ln_b, v_sgu_ws, v_sgu_bs, v_w_proj_a, v_w_proj_b, v_w_out, v_norm_ffn_g, v_w_gate_up, v_w_down, v_norm_final_g):
    args = dict(locals())
    w = {n: args[n] for n in _WEIGHTS}
    mom = {n: args["m_" + n] for n in _WEIGHTS}
    var = {n: args["v_" + n] for n in _WEIGHTS}
    xi, yi, ci = _position()
    core = ci.astype(jnp.int32).reshape(1)
    chip = (2 * xi + yi).astype(jnp.int32).reshape(1)

    bf = {n: w[n][0].astype(BF16) for n in _BIG}
    final_g = w["norm_final_g"].reshape(1, D)
    ba, bx = w["rg_ba"].reshape(1, D), w["rg_bx"].reshape(1, D)
    lam, ln_g, ln_b = w["rg_lambda"], w["sgu_ln_g"], w["sgu_ln_b"]
    x0, target = x[0], loss_target[0]

    def shard_major(g):
        return g.reshape(N_SHARD, g.shape[0] // N_SHARD, g.shape[1])

    def chip_sums(names, grads, theirs):
        return [_add_halves(core, g, t, "add_halves_" + n) for n, g, t in zip(names, grads, theirs)]

    def my_halves(names, sums, arrived):
        return [_sum_shards(chip, p32, a, "sum_shards_" + n) for n, (p32, _), a in zip(names, sums, arrived)]

    (w_in_a,), (conv_a,) = _comm_only([_gather_half_side([bf["w_in"]]), _gather_side([w["conv_w"][0]])], "gather_w_in")
    conv_cols = conv_a.shape[-1]
    conv_full = jnp.swapaxes(conv_a, 0, 1).reshape(CONV_WIDTH, D)
    (proj, h), ((w_pa_a, w_pb_a, w_out_a),) = _inproj(
        x0, w["norm_mix_g"], w_in_a, sides=[_gather_half_side([bf["w_proj_a"], bf["w_proj_b"], bf["w_out"]])])
    (ya, xc, hseq, gates), ((w_gu_a, w_down_a),) = _rglru_fwd(
        proj, conv_full, w["conv_b"], w["rg_wa"], w["rg_wx"], ba, bx, lam,
        sides=[_gather_half_side([bf["w_gate_up"], bf["w_down"]])])
    wpa, wpb, wout, wdown = w_pa_a.reshape(D, D), w_pb_a.reshape(D, D), w_out_a.reshape(D, D), w_down_a.reshape(D_FF, D)
    yb = _sgu_fwd(proj, ln_g, ln_b, w["sgu_ws"], w["sgu_bs"])
    pa, pb, mb, x1 = _merge_fwd(x0, proj, ya, yb, wpa, wpb, wout)
    h2, gu, act = _ffn_up(x1, w["norm_ffn_g"], w_gu_a)
    dx2, dx2b, loss, d_final_g = _ffn_down_loss(x1, act, wdown, final_g, target)

    dgu = _ffn_bwd_act(dx2b, gu, wdown)
    dx1, dx1b, d_ffn_g = _ffn_bwd_in(dgu, w_gu_a, x1, w["norm_ffn_g"], dx2)
    ffn = ("w_gate_up", "w_down")
    g_ffn = [_matmul_tn(h2, dgu, FF_SHARD, True, "wgrad_gate_up")[0],
             shard_major(_matmul_tn(act, dx2b, D // 2, False, "wgrad_down")[0])]
    (dpa, dpb, dproj, dya, dyb), (theirs_ffn,) = _merge_bwd(
        dx1b, proj, pa, pb, wpa, wpb, wout, sides=[_halves_side(g_ffn)])
    sums_ffn = chip_sums(ffn, g_ffn, theirs_ffn)
    mix = ("w_proj_a", "w_proj_b", "w_out")
    g_mix = [shard_major(_matmul_tn(ya, dpa, D, False, "wgrad_proj_a")[0]),
             shard_major(_matmul_tn(yb, dpb, D, False, "wgrad_proj_b")[0]),
             shard_major(_matmul_tn(mb, dx1b, D, False, "wgrad_out")[0])]
    (dproj, d_cw, d_cb, d_wa, d_wx, d_ba, d_bx, d_lam), (arrived_ffn, theirs_mix) = _rglru_bwd(
        dya, dproj, proj, hseq, xc, gates, conv_full, w["rg_wa"], w["rg_wx"], lam,
        sides=[_scatter_side([p16 for _, p16 in sums_ffn]), _halves_side(g_mix)])
    mine_ffn = my_halves(ffn, sums_ffn, arrived_ffn)
    sums_mix = chip_sums(mix, g_mix, theirs_mix)
    (dproj, d_ws, d_bs, d_lng, d_lnb), (other_ffn, arrived_mix) = _sgu_bwd(
        dyb, dproj, proj, ln_g, ln_b, w["sgu_ws"], w["sgu_bs"],
        sides=[_swap_side(mine_ffn), _scatter_side([p16 for _, p16 in sums_mix])])
    mine_mix = my_halves(mix, sums_mix, arrived_mix)
    grad_x, d_mix_g = _inproj_bwd(dproj, w_in_a, x0, w["norm_mix_g"], dx1)
    rows = {"norm_mix_g": d_mix_g, "conv_b": d_cb, "rg_lambda": d_lam, "sgu_ln_g": d_lng, "sgu_ln_b": d_lnb,
            "norm_ffn_g": d_ffn_g, "norm_final_g": d_final_g, "rg_ba": d_ba, "rg_bx": d_bx}
    vec = jnp.concatenate([rows[n] for n in _VEC_ROWS]
                          + [d_cw, jnp.pad(loss, ((0, _VEC_PAD), (0, D - 1)))], axis=0)
    tensors = {"sgu_bs": d_bs, "rg_wa": d_wa, "rg_wx": d_wx, "sgu_ws": d_ws}
    g_in, (small_parts, other_mix) = _matmul_tn(
        h, dproj, IN_SHARD, True, "inproj_wgrad",
        sides=[_everyone_side([vec] + [tensors[n] for n in _TENSORS]), _swap_side(mine_mix)])
    theirs_in = _comm_only([_halves_side([g_in])], "halves_w_in")[0]
    sums_in = chip_sums(("w_in",), [g_in], theirs_in)
    arrived_in = _comm_only([_scatter_side([p16 for _, p16 in sums_in])], "scatter_w_in")[0]
    mine_in = my_halves(("w_in",), sums_in, arrived_in)
    other_in = _comm_only([_swap_side(mine_in)], "swap_w_in")[0]

    out = {}
    for n, gm, go in zip(ffn + mix + ("w_in",), mine_ffn + mine_mix + mine_in, other_ffn + other_mix + other_in):
        g, d, mo, vo = _adamw_shard(core, gm, go, w[n][0], mom[n][0], var[n][0], "adamw_" + n)
        out[n] = tuple(a[None] for a in (g, d, mo, vo))
    as_row = lambda t: {n: a.reshape(1, D) if n == "norm_final_g" else a for n, a in t.items()}
    small_out, conv_sum, loss_sum = _small_sum_adamw(small_parts, as_row(w), as_row(mom), as_row(var))
    out.update(small_out)
    out["norm_final_g"] = tuple(a.reshape(D) for a in small_out["norm_final_g"])
    conv_g = lax.dynamic_slice_in_dim(conv_sum, chip[0] * conv_cols, conv_cols, axis=1)
    d, mo, vo = _adamw_whole(w["conv_w"][0], conv_g, mom["conv_w"][0], var["conv_w"][0], "adamw_conv_w")
    out["conv_w"] = tuple(a[None] for a in (conv_g, d, mo, vo))

    return (loss_sum[0, 0], grad_x[None], *[out[n][0] for n in _WEIGHTS], *[out[n][1] for n in _WEIGHTS],
            *[out[n][2] for n in _WEIGHTS], *[out[n][3] for n in _WEIGHTS])
```

```python
import functools

import jax
import jax.numpy as jnp
from jax import lax
from jax.experimental import pallas as pl
from jax.experimental.pallas import tpu as pltpu

F32 = jnp.float32
BF16 = jnp.bfloat16
S = jax.ShapeDtypeStruct

D = 1024
N_SHARD = 4
IN_COLS = 6 * D
IN_SHARD = IN_COLS // N_SHARD
D_FF = 2816
FF_SHARD = 2 * D_FF // N_SHARD
RG_BLOCK = 256
N_RG_BLOCK = D // RG_BLOCK
CHUNK = 128
N_GROUP = 8
CONV_WIDTH = 4
RG_C = 8.0
EPS = 1e-6
ADAM_LR, ADAM_B1, ADAM_B2, ADAM_EPS, ADAM_WD, ADAM_STEP = 0.001, 0.9, 0.999, 1e-08, 0.01, 10

V7X_VMEM_BYTES = 64 * 1024 * 1024
VMEM_LIMIT = V7X_VMEM_BYTES * 3 // 4
SUBLANES = 8
MESH = pl.DeviceIdType.MESH

TM_MM = 512
TM_SCAN = 256
TM_FF = 256
TK_WGRAD = 1024


def _params(n_axes):
    return pltpu.CompilerParams(dimension_semantics=("arbitrary",) * n_axes, vmem_limit_bytes=VMEM_LIMIT)


def _resident(shape):
    nd = len(shape)
    return pl.BlockSpec(shape, lambda *_: (0,) * nd, pipeline_mode=pl.Buffered(1))


def _sig(x):
    return 1.0 / (1.0 + jnp.exp(-x))


_GELU_K2 = 2.0 * 0.7978845608028654
_GELU_C = 0.044715


def _gelu(x):
    return x * _sig(x * (_GELU_K2 + (_GELU_K2 * _GELU_C) * (x * x)))


def _gelu_and_grad(x):
    x2 = x * x
    s = _sig(x * (_GELU_K2 + (_GELU_K2 * _GELU_C) * x2))
    g = x * s
    return g, s + g * (1.0 - s) * (_GELU_K2 + (3.0 * _GELU_K2 * _GELU_C) * x2)


_EXPM1_SERIES = tuple(1.0 / f for f in (5040.0, 720.0, 120.0, 24.0, 6.0, 2.0, 1.0))


def _one_minus_exp(x):
    p = _EXPM1_SERIES[0]
    for coef in _EXPM1_SERIES[1:]:
        p = p * x + coef
    return jnp.where(x > -0.125, -x * p, 1.0 - jnp.exp(x))


def _softplus_neg(lam):
    z = -lam
    e = jnp.exp(-jnp.abs(z))
    u = 1.0 + e
    log1p = jnp.where(u == 1.0, e, jnp.log(u) * e / (u - 1.0))
    return jnp.maximum(z, 0.0) + log1p


def _rms_stats(x):
    return lax.rsqrt(jnp.mean(x * x, axis=-1, keepdims=True) + EPS)


def _rms_bwd(dy, x, g):
    rstd = _rms_stats(x)
    xhat = x * rstd
    dxhat = dy * g
    dx = rstd * (dxhat - xhat * jnp.mean(dxhat * xhat, axis=-1, keepdims=True))
    return dx, dy * xhat


def _colsum(x):
    return jnp.sum(x, axis=0, keepdims=True)


def _shift_down(x, d, fill):
    n = x.shape[0]
    if d % SUBLANES == 0:
        return jnp.concatenate([jnp.full((d, x.shape[1]), fill, x.dtype), x[:n - d]], axis=0)
    row = lax.broadcasted_iota(jnp.int32, x.shape, 0)
    return jnp.where(row < d, fill, pltpu.roll(x, d, 0))


def _shift_up(x, d, fill):
    n = x.shape[0]
    if d % SUBLANES == 0:
        return jnp.concatenate([x[d:], jnp.full((d, x.shape[1]), fill, x.dtype)], axis=0)
    row = lax.broadcasted_iota(jnp.int32, x.shape, 0)
    return jnp.where(row >= n - d, fill, pltpu.roll(x, n - d, 0))


def _scan(a, b, shift):
    d = 1
    while d < a.shape[0]:
        b = a * shift(b, d, 0.0) + b
        a = a * shift(a, d, 1.0)
        d *= 2
    return a, b


LANES = 128


def _scan_tile(a, b, outside, a_s, b_s, h_s, reverse):
    tm = a.shape[0]
    groups = tm // SUBLANES
    order = list(range(SUBLANES - 1, -1, -1) if reverse else range(SUBLANES))
    shift = _shift_up if reverse else _shift_down
    edge = groups - 1 if reverse else 0
    for j in range(D // LANES):
        a_s[j] = a[:, LANES * j:LANES * (j + 1)]
        b_s[j] = b[:, LANES * j:LANES * (j + 1)]
    for j in range(D // LANES):
        def slab(ref, k):
            return ref[j, pl.ds(k, groups, stride=SUBLANES), :]

        ga, gb = slab(a_s, order[0]), slab(b_s, order[0])
        for k in order[1:]:
            ak = slab(a_s, k)
            gb = ak * gb + slab(b_s, k)
            ga = ak * ga
        ga, gb = _scan(ga, gb, shift)
        h_out = outside[:, LANES * j:LANES * (j + 1)]
        group_end = ga * h_out + gb
        row = lax.broadcasted_iota(jnp.int32, (groups, LANES), 0)
        h = jnp.where(row == edge, h_out, shift(group_end, 1, 0.0))
        for k in order:
            h = slab(a_s, k) * h + slab(b_s, k)
            h_s[j, pl.ds(k, groups, stride=SUBLANES), :] = h
    return jnp.concatenate([h_s[j] for j in range(D // LANES)], axis=1)


def _dot(a, b):
    return jnp.dot(a, b, preferred_element_type=F32)


def _dot_nt(a, b):
    return lax.dot_general(a, b, (((1,), (1,)), ((), ())), preferred_element_type=F32)


def _dot_tn(a, b):
    return lax.dot_general(a, b, (((0,), (0,)), ((), ())), preferred_element_type=F32)


_ANY = pl.BlockSpec(memory_space=pl.ANY)


def _position():
    return lax.axis_index("x"), lax.axis_index("y"), lax.axis_index("c")


def _other_chips(x, y):
    return [(1 - x, y), (x, 1 - y), (1 - x, 1 - y)]


class _Side:
    def __init__(self, inputs, out_shapes, n_sems, make):
        self.inputs, self.out_shapes, self.n_sems, self.make = list(inputs), list(out_shapes), n_sems, make


MID_STEP = 0.625


def _call(body, *, name, grid, in_specs, out_specs, out_shape, args, scratch_shapes=(), sides=(), aliases=None):
    n_in, n_out, n_scr = len(in_specs), len(out_specs), len(scratch_shapes)
    side_in = [len(s.inputs) for s in sides]
    side_out = [len(s.out_shapes) for s in sides]

    def wrapped(*refs):
        refs = list(refs)
        take = lambda k: [refs.pop(0) for _ in range(k)]
        ins = take(n_in)
        sins = [take(k) for k in side_in]
        outs = take(n_out)
        souts = [take(k) for k in side_out]
        scr = take(n_scr)
        sems = [take(3) for _ in sides]
        def run(phase):
            for s, si, so, sem in zip(sides, sins, souts, sems):
                for thunk in s.make(si, so, *sem)[phase]:
                    thunk()

        if sides:
            n_steps = functools.reduce(lambda a, b: a * b, grid)
            step = functools.reduce(lambda a, b: a + b, [
                pl.program_id(a) * functools.reduce(lambda p, q: p * q, grid[a + 1:], 1) for a in range(len(grid))])
            pl.when(step == 0)(lambda: run(0))
        body(*ins, *outs, *scr)
        if sides:
            pl.when(step == int(MID_STEP * (n_steps - 1)))(lambda: run(1))
            pl.when(step == n_steps - 1)(lambda: run(2))

    res = pl.pallas_call(
        wrapped, name=name, grid=grid,
        in_specs=list(in_specs) + [_ANY] * sum(side_in),
        out_specs=list(out_specs) + [_ANY] * sum(side_out),
        out_shape=list(out_shape) + [o for s in sides for o in s.out_shapes],
        scratch_shapes=list(scratch_shapes) + [pltpu.SemaphoreType.DMA((s.n_sems,)) for s in sides for _ in range(3)],
        input_output_aliases=aliases or {},
        compiler_params=_params(len(grid)),
    )(*args, *[a for s in sides for a in s.inputs])
    main, rest, per_side = list(res[:n_out]), list(res[n_out:]), []
    for k in side_out:
        per_side.append(rest[:k])
        rest = rest[k:]
    return main, per_side


def _comm_only(sides, name):
    def body():
        pass

    return _call(body, name=name, grid=(1,), in_specs=[], out_specs=[], out_shape=[], args=[], sides=sides)[1]


def _remote(src, dst, send, recv, k, device):
    return pltpu.make_async_remote_copy(src_ref=src, dst_ref=dst, send_sem=send.at[k], recv_sem=recv.at[k],
                                        device_id=device, device_id_type=MESH)


def _both_ways(copy, keys):
    return [lambda k=k: copy(k).start() for k in keys], [], [lambda k=k: copy(k).wait() for k in keys]


def _gather_side(shards):
    n = len(shards)

    def make(ins, outs, send, recv, local):
        x, y, c = _position()
        mine = 2 * x + y
        chips = _other_chips(x, y)
        pairs = [(w, j) for w in range(n) for j in range(3)]

        def own(w):
            return pltpu.make_async_copy(ins[w], outs[w].at[mine], local.at[w])

        def push(w, j):
            return _remote(ins[w], outs[w].at[mine], send, recv, 3 * w + j, (*chips[j], c))

        def arrival(w, j):
            px, py = chips[j]
            return _remote(ins[w], outs[w].at[2 * px + py], send, recv, 3 * w + j, (px, py, c))

        starts = [lambda w=w: own(w).start() for w in range(n)] + [lambda w=w, j=j: push(w, j).start() for w, j in pairs]
        waits = ([lambda w=w, j=j: arrival(w, j).wait_recv() for w, j in pairs]
                 + [lambda w=w, j=j: push(w, j).wait_send() for w, j in pairs]
                 + [lambda w=w: own(w).wait() for w in range(n)])
        return starts, [], waits

    return _Side(shards, [S((N_SHARD,) + s.shape, s.dtype) for s in shards], 3 * n, make)


def _gather_half_side(shards):
    n = len(shards)

    def make(ins, outs, send, recv, local):
        x, y, c = _position()
        mine = 2 * x + y
        chips = _other_chips(x, y)
        pairs = [(w, j) for w in range(n) for j in range(3)]

        def rows(w, core):
            half = ins[w].shape[0] // 2
            return pl.ds(core * half, half)

        def own(w):
            return pltpu.make_async_copy(ins[w], outs[w].at[mine], local.at[w])

        def push(w, j):
            return _remote(ins[w].at[rows(w, c), :], outs[w].at[mine, rows(w, c), :], send, recv, 3 * w + j,
                           (*chips[j], c))

        def landed(w, j, core):
            px, py = chips[j]
            return outs[w].at[2 * px + py, rows(w, core), :]

        def arrival(w, j):
            return _remote(ins[w].at[rows(w, c), :], landed(w, j, c), send, recv, 3 * w + j, (*chips[j], c))

        def passed(w, j, core):
            return _remote(landed(w, j, core), landed(w, j, core), send, recv, 3 * n + 3 * w + j, (x, y, 1 - c))

        starts = [lambda w=w: own(w).start() for w in range(n)] + [lambda w=w, j=j: push(w, j).start() for w, j in pairs]
        mids = [t for w, j in pairs for t in (lambda w=w, j=j: arrival(w, j).wait_recv(),
                                              lambda w=w, j=j: passed(w, j, c).start())]
        waits = ([lambda w=w, j=j: passed(w, j, 1 - c).wait_recv() for w, j in pairs]
                 + [lambda w=w, j=j: passed(w, j, c).wait_send() for w, j in pairs]
                 + [lambda w=w, j=j: push(w, j).wait_send() for w, j in pairs]
                 + [lambda w=w: own(w).wait() for w in range(n)])
        return starts, mids, waits

    return _Side(shards, [S((N_SHARD,) + s.shape, s.dtype) for s in shards], 6 * n, make)


def _halves_side(grads):
    n = len(grads)

    def make(ins, outs, send, recv, local):
        x, y, c = _position()

        def copy(w):
            half = ins[w].shape[1] // 2
            return _remote(ins[w].at[:, pl.ds((1 - c) * half, half), :], outs[w], send, recv, w, (x, y, 1 - c))

        return _both_ways(copy, range(n))

    return _Side(grads, [S((N_SHARD, g.shape[1] // 2, g.shape[2]), F32) for g in grads], n, make)


def _scatter_side(partials):
    n = len(partials)

    def make(ins, outs, send, recv, local):
        x, y, c = _position()
        chips = _other_chips(x, y)

        def copy(k):
            w, j = divmod(k, 3)
            px, py = chips[j]
            return _remote(ins[w].at[2 * px + py], outs[w].at[j], send, recv, k, (px, py, c))

        return _both_ways(copy, range(3 * n))

    return _Side(partials, [S((3,) + p.shape[1:], p.dtype) for p in partials], 3 * n, make)


def _swap_side(halves):
    n = len(halves)

    def make(ins, outs, send, recv, local):
        x, y, c = _position()
        return _both_ways(lambda w: _remote(ins[w], outs[w], send, recv, w, (x, y, 1 - c)), range(n))

    return _Side(halves, [S(h.shape, h.dtype) for h in halves], n, make)


N_DEVICE = 8


def _everyone_side(arrays):
    n = len(arrays)
    peers = N_DEVICE - 1

    def make(ins, outs, send, recv, local):
        x, y, c = _position()
        mine = 4 * x + 2 * y + c
        pairs = [(w, k) for w in range(n) for k in range(1, N_DEVICE)]

        def peer(k):
            return (1 - x if k & 4 else x, 1 - y if k & 2 else y, 1 - c if k & 1 else c)

        def own(w):
            return pltpu.make_async_copy(ins[w], outs[w].at[mine], local.at[w])

        def push(w, k):
            return _remote(ins[w], outs[w].at[mine], send, recv, peers * w + k - 1, peer(k))

        def arrival(w, k):
            px, py, pc = peer(k)
            return _remote(ins[w], outs[w].at[4 * px + 2 * py + pc], send, recv, peers * w + k - 1, (px, py, pc))

        starts = [lambda w=w: own(w).start() for w in range(n)] + [lambda w=w, k=k: push(w, k).start() for w, k in pairs]
        waits = ([lambda w=w, k=k: arrival(w, k).wait_recv() for w, k in pairs]
                 + [lambda w=w, k=k: push(w, k).wait_send() for w, k in pairs]
                 + [lambda w=w: own(w).wait() for w in range(n)])
        return starts, [], waits

    return _Side(arrays, [S((N_DEVICE,) + a.shape, a.dtype) for a in arrays], peers * n, make)


def _inproj(x, g, w_in, sides=()):
    T = x.shape[0]
    tm = min(TM_FF, T)

    def body(x_ref, g_ref, w_ref, proj_ref, h_ref):
        xv = x_ref[...]
        h = (xv * _rms_stats(xv) * g_ref[...]).astype(BF16)
        h_ref[...] = h
        for k in range(N_SHARD):
            proj_ref[:, IN_SHARD * k:IN_SHARD * (k + 1)] = _dot(h, w_ref[k])

    return _call(
        body, name="inproj", grid=(T // tm,),
        in_specs=[pl.BlockSpec((tm, D), lambda i: (i, 0)), pl.BlockSpec((1, D), lambda i: (0, 0)),
                  _resident((N_SHARD, D, IN_SHARD))],
        out_specs=[pl.BlockSpec((tm, IN_COLS), lambda i: (i, 0)), pl.BlockSpec((tm, D), lambda i: (i, 0))],
        out_shape=[S((T, IN_COLS), F32), S((T, D), BF16)],
        args=(x, g, w_in), sides=sides)


def _rg_gates(xc, wa_ref, wx_ref, ba, bx, sp):
    xb = xc.astype(BF16)
    blocks = [xb[:, RG_BLOCK * j:RG_BLOCK * (j + 1)] for j in range(N_RG_BLOCK)]
    r = _sig(jnp.concatenate([_dot(blocks[j], wa_ref[j]) for j in range(N_RG_BLOCK)], axis=1) + ba)
    gi = _sig(jnp.concatenate([_dot(blocks[j], wx_ref[j]) for j in range(N_RG_BLOCK)], axis=1) + bx)
    log_a = (-RG_C) * r * sp
    a = jnp.exp(log_a)
    m = jnp.sqrt(_one_minus_exp(2.0 * log_a))
    return xb, r, gi, a, m


N_GATES = 4
HEADS_PER_BLOCK = 4
HEAD_DIM = RG_BLOCK // HEADS_PER_BLOCK
_RG_BLOCKS_BF16 = pltpu.VMEM((N_RG_BLOCK, RG_BLOCK, RG_BLOCK), BF16)


def _fill_blockdiag(heads_ref, blocks):
    blocks[...] = jnp.zeros_like(blocks)
    for j in range(N_RG_BLOCK):
        for h in range(HEADS_PER_BLOCK):
            sl = slice(HEAD_DIM * h, HEAD_DIM * (h + 1))
            blocks[j, sl, sl] = heads_ref[0, HEADS_PER_BLOCK * j + h].astype(BF16)


def _rglru_fwd(proj, conv_w, conv_b, rg_wa, rg_wx, ba, bx, lam, sides=()):
    T = proj.shape[0]
    tm = min(TM_SCAN, T)

    def body(rx_ref, gate_ref, cw_ref, cb_ref, wah_ref, wxh_ref, ba_ref, bx_ref, lam_ref,
             ya_ref, xc_ref, h_ref, gates_ref, ext, hc, a_s, b_s, h_s, wa_ref, wx_ref):
        @pl.when(pl.program_id(0) == 0)
        def _():
            ext[0:SUBLANES, :] = jnp.zeros((SUBLANES, D), F32)
            hc[...] = jnp.zeros((SUBLANES, D), F32)
            _fill_blockdiag(wah_ref, wa_ref)
            _fill_blockdiag(wxh_ref, wx_ref)

        ext[SUBLANES:SUBLANES + tm, :] = rx_ref[...]
        xc = cb_ref[...]
        for k in range(CONV_WIDTH):
            xc = xc + ext[pl.ds(SUBLANES - (CONV_WIDTH - 1) + k, tm), :] * cw_ref[k:k + 1, :]
        ext[0:SUBLANES, :] = ext[tm:tm + SUBLANES, :]
        xc_ref[...] = xc
        _, r, gi, a, m = _rg_gates(xc, wa_ref, wx_ref, ba_ref[...], bx_ref[...], _softplus_neg(lam_ref[...]))
        for k, val in enumerate((r, gi, a, m)):
            gates_ref[:, D * k:D * (k + 1)] = val
        h = _scan_tile(a, m * (gi * xc), hc[0:1, :], a_s, b_s, h_s, reverse=False)
        hc[...] = jnp.broadcast_to(h[tm - 1:tm, :], (SUBLANES, D))
        h_ref[...] = h
        ya_ref[...] = (_gelu(gate_ref[...]) * h).astype(BF16)

    vec = pl.BlockSpec((1, D), lambda i: (0, 0))
    heads = pl.BlockSpec(rg_wa.shape, lambda i: (0, 0, 0, 0))
    tile = pl.BlockSpec((tm, D), lambda i: (i, 0))
    return _call(
        body, name="rglru_fwd", grid=(T // tm,),
        in_specs=[pl.BlockSpec((tm, D), lambda i: (i, 0)), pl.BlockSpec((tm, D), lambda i: (i, 1)),
                  pl.BlockSpec((CONV_WIDTH, D), lambda i: (0, 0)), vec, heads, heads, vec, vec, vec],
        out_specs=[tile, tile, tile, pl.BlockSpec((tm, N_GATES * D), lambda i: (i, 0))],
        out_shape=[S((T, D), BF16), S((T, D), F32), S((T, D), F32), S((T, N_GATES * D), F32)],
        scratch_shapes=[pltpu.VMEM((tm + SUBLANES, D), F32), pltpu.VMEM((SUBLANES, D), F32)]
        + [pltpu.VMEM((D // LANES, tm, LANES), F32)] * 3 + [_RG_BLOCKS_BF16] * 2,
        args=(proj, proj, conv_w, conv_b, rg_wa, rg_wx, ba, bx, lam), sides=sides)


def _layer_norm_stats(v):
    mu = jnp.mean(v, axis=-1, keepdims=True)
    vc = v - mu
    rstd = lax.rsqrt(jnp.mean(vc * vc, axis=-1, keepdims=True) + EPS)
    return vc * rstd, rstd


def _sgu_mix(w_ref, vnb, bst_ref, n_chunk):
    cols = []
    for g in range(N_GROUP):
        vg = vnb[:, CHUNK * g:CHUNK * (g + 1)].reshape(n_chunk, CHUNK, CHUNK)
        wb = jnp.broadcast_to(w_ref[g][None], (n_chunk, CHUNK, CHUNK))
        mg = lax.dot_general(wb, vg, (((2,), (1,)), ((0,), (0,))), preferred_element_type=F32)
        mg = mg + bst_ref[:, g:g + 1][None]
        cols.append(mg.reshape(n_chunk * CHUNK, CHUNK))
    return jnp.concatenate(cols, axis=1)


def _causal_mask():
    return (lax.broadcasted_iota(jnp.int32, (CHUNK, CHUNK), 0) >= lax.broadcasted_iota(jnp.int32, (CHUNK, CHUNK), 1))


def _fill_sgu_weights(ws_ref, bs_ref, w_tril, bs_t, w_tril_t=None):
    keep = _causal_mask()
    for g in range(N_GROUP):
        wg = jnp.where(keep, ws_ref[0, g], 0.0)
        w_tril[g] = wg.astype(BF16)
        if w_tril_t is not None:
            w_tril_t[g] = wg.T.astype(BF16)
    bs_t[...] = bs_ref[0].T


_SGU_W_BF16 = pltpu.VMEM((N_GROUP, CHUNK, CHUNK), BF16)
_SGU_BT = pltpu.VMEM((CHUNK, N_GROUP), F32)


def _sgu_merge_fwd(x, proj, ya, ln_g, ln_b, sgu_ws, sgu_bs, wpa, wpb, wout):
    T = x.shape[0]
    tm = min(TM_FF, T)
    n_chunk = tm // CHUNK

    def body(x_ref, uv_ref, gab_ref, ya_ref, g_ref, b_ref, ws_ref, bs_ref, wpa_ref, wpb_ref, wout_ref,
             yb_ref, pa_ref, pb_ref, mb_ref, x1_ref, w_ref, bst_ref):
        @pl.when(pl.program_id(0) == 0)
        def _():
            _fill_sgu_weights(ws_ref, bs_ref, w_ref, bst_ref)

        vhat, _ = _layer_norm_stats(_gelu(uv_ref[:, D:2 * D]))
        vnb = (vhat * g_ref[...] + b_ref[...]).astype(BF16)
        yb = (_gelu(uv_ref[:, 0:D]) * _sgu_mix(w_ref, vnb, bst_ref, n_chunk)).astype(BF16)
        yb_ref[...] = yb
        pa = _dot(ya_ref[...], wpa_ref[...])
        pb = _dot(yb, wpb_ref[...])
        pa_ref[...] = pa.astype(BF16)
        pb_ref[...] = pb.astype(BF16)
        mb = (_sig(gab_ref[:, 0:D]) * pa + _sig(gab_ref[:, D:2 * D]) * pb).astype(BF16)
        mb_ref[...] = mb
        x1_ref[...] = x_ref[...] + _dot(mb, wout_ref[...])

    tile = pl.BlockSpec((tm, D), lambda i: (i, 0))
    vec = pl.BlockSpec((1, D), lambda i: (0, 0))
    w = _resident((D, D))
    return pl.pallas_call(
        body, name="sgu_merge_fwd", grid=(T // tm,),
        in_specs=[tile, pl.BlockSpec((tm, 2 * D), lambda i: (i, 1)), pl.BlockSpec((tm, 2 * D), lambda i: (i, 2)), tile,
                  vec, vec, pl.BlockSpec(sgu_ws.shape, lambda i: (0, 0, 0, 0)),
                  pl.BlockSpec(sgu_bs.shape, lambda i: (0, 0, 0)), w, w, w],
        out_specs=[tile, tile, tile, tile, tile],
        out_shape=[S((T, D), BF16), S((T, D), BF16), S((T, D), BF16), S((T, D), BF16), S((T, D), F32)],
        scratch_shapes=[_SGU_W_BF16, _SGU_BT],
        compiler_params=_params(1),
    )(x, proj, proj, ya, ln_g, ln_b, sgu_ws, sgu_bs, wpa, wpb, wout)


def _ffn_fwd_loss(x1, g, w_gu, w_down, g_final, target):
    T = x1.shape[0]
    tm = min(TM_FF, T)

    def body(x_ref, g_ref, wgu_ref, wd_ref, gf_ref, t_ref,
             h2_ref, gu_ref, act_ref, dx2_ref, dx2b_ref, loss_ref, dg_ref):
        @pl.when(pl.program_id(0) == 0)
        def _():
            loss_ref[...] = jnp.zeros_like(loss_ref)
            dg_ref[...] = jnp.zeros_like(dg_ref)

        xv = x_ref[...]
        h2 = (xv * _rms_stats(xv) * g_ref[...]).astype(BF16)
        h2_ref[...] = h2
        x2 = xv
        for k in range(N_SHARD // 2):
            cols = slice(FF_SHARD * k, FF_SHARD * (k + 1))
            gate = _dot(h2, wgu_ref[k])
            up = _dot(h2, wgu_ref[k + N_SHARD // 2])
            gu_ref[:, cols] = gate.astype(BF16)
            gu_ref[:, D_FF + FF_SHARD * k:D_FF + FF_SHARD * (k + 1)] = up.astype(BF16)
            act = (gate * _sig(gate) * up).astype(BF16)
            act_ref[:, cols] = act
            x2 = x2 + _dot(act, wd_ref[cols, :])
        gf = gf_ref[...]
        err = x2 * _rms_stats(x2) * gf - t_ref[...]
        loss_ref[...] += 0.5 * jnp.sum(jnp.mean(err * err, axis=-1, keepdims=True), axis=0, keepdims=True)
        dx2, dg_rows = _rms_bwd(err * (1.0 / D), x2, gf)
        dg_ref[...] += _colsum(dg_rows)
        dx2_ref[...] = dx2
        dx2b_ref[...] = dx2.astype(BF16)

    tile = pl.BlockSpec((tm, D), lambda i: (i, 0))
    vec = pl.BlockSpec((1, D), lambda i: (0, 0))
    return pl.pallas_call(
        body, name="ffn_fwd_loss", grid=(T // tm,),
        in_specs=[tile, vec, _resident((N_SHARD, D, FF_SHARD)), _resident((D_FF, D)), vec, tile],
        out_specs=[tile, pl.BlockSpec((tm, 2 * D_FF), lambda i: (i, 0)), pl.BlockSpec((tm, D_FF), lambda i: (i, 0)),
                   tile, tile, pl.BlockSpec((1, 1), lambda i: (0, 0)), vec],
        out_shape=[S((T, D), BF16), S((T, 2 * D_FF), BF16), S((T, D_FF), BF16), S((T, D), F32), S((T, D), BF16),
                   S((1, 1), F32), S((1, D), F32)],
        compiler_params=_params(1),
    )(x1, g, w_gu, w_down, g_final, target)


def _ffn_bwd(dx2, dx2b, gu, w_down, w_gu, x1, g):
    T = x1.shape[0]
    tm = min(TM_FF, T)

    def body(dx2_ref, dx2b_ref, gu_ref, wd_ref, wgu_ref, x_ref, g_ref, dgu_ref, dx1_ref, dx1b_ref, dg_ref):
        @pl.when(pl.program_id(0) == 0)
        def _():
            dg_ref[...] = jnp.zeros_like(dg_ref)

        dxb = dx2b_ref[...]
        dh2 = jnp.zeros((tm, D), F32)
        for k in range(N_SHARD // 2):
            cols = slice(FF_SHARD * k, FF_SHARD * (k + 1))
            up_cols = slice(D_FF + FF_SHARD * k, D_FF + FF_SHARD * (k + 1))
            dact = _dot_nt(dxb, wd_ref[cols, :])
            gate = gu_ref[:, cols].astype(F32)
            sg = _sig(gate)
            dgate = (dact * gu_ref[:, up_cols].astype(F32) * (sg * (1.0 + gate * (1.0 - sg)))).astype(BF16)
            dup = (dact * (gate * sg)).astype(BF16)
            dgu_ref[:, cols] = dgate
            dgu_ref[:, up_cols] = dup
            dh2 = dh2 + _dot_nt(dgate, wgu_ref[k]) + _dot_nt(dup, wgu_ref[k + N_SHARD // 2])
        dx, dg_rows = _rms_bwd(dh2, x_ref[...], g_ref[...])
        dg_ref[...] += _colsum(dg_rows)
        dx1 = dx2_ref[...] + dx
        dx1_ref[...] = dx1
        dx1b_ref[...] = dx1.astype(BF16)

    tile = pl.BlockSpec((tm, D), lambda i: (i, 0))
    wide = pl.BlockSpec((tm, 2 * D_FF), lambda i: (i, 0))
    vec = pl.BlockSpec((1, D), lambda i: (0, 0))
    return pl.pallas_call(
        body, name="ffn_bwd", grid=(T // tm,),
        in_specs=[tile, tile, wide, _resident((D_FF, D)), _resident((N_SHARD, D, FF_SHARD)), tile, vec],
        out_specs=[wide, tile, tile, vec],
        out_shape=[S((T, 2 * D_FF), BF16), S((T, D), F32), S((T, D), BF16), S((1, D), F32)],
        compiler_params=_params(1),
    )(dx2, dx2b, gu, w_down, w_gu, x1, g)


def _matmul_tn(a, b, tn, shard_major, name, sides=()):
    T, M = a.shape
    N = b.shape[1]
    tk = min(TK_WGRAD, T)

    def body(a_ref, b_ref, o_ref):
        @pl.when(pl.program_id(1) == 0)
        def _():
            o_ref[...] = jnp.zeros_like(o_ref)

        acc = _dot_tn(a_ref[...], b_ref[...])
        if shard_major:
            o_ref[0] += acc
        else:
            o_ref[...] += acc

    if shard_major:
        out_spec, out_shape = pl.BlockSpec((1, M, tn), lambda j, k: (j, 0, 0)), S((N // tn, M, tn), F32)
    else:
        out_spec, out_shape = pl.BlockSpec((M, tn), lambda j, k: (0, j)), S((M, N), F32)
    (out,), side_outs = _call(
        body, name=name, grid=(N // tn, T // tk),
        in_specs=[pl.BlockSpec((tk, M), lambda j, k: (k, 0)), pl.BlockSpec((tk, tn), lambda j, k: (k, j))],
        out_specs=[out_spec], out_shape=[out_shape], args=(a, b), sides=sides)
    return out, side_outs


PIECE = IN_SHARD // 3
N_PIECE = IN_COLS // PIECE
DPROJ_ROTATION = 2 * D // PIECE


def _merge_sgu_bwd(dx1b, proj, pa, pb, ln_g, ln_b, sgu_ws, sgu_bs, wpa, wpb, wout, sides=()):
    T = dx1b.shape[0]
    tm = min(TM_FF, T)
    n_chunk = tm // CHUNK

    def body(dx_ref, uv_ref, gab_ref, pa_ref, pb_ref, g_ref, b_ref, ws_ref, bs_ref, wpa_ref, wpb_ref, wout_ref,
             dpa_ref, dpb_ref, dya_ref, dp_ref, dw_ref, dbs_ref, dg_ref, db_ref, w_ref, wt_ref, bst_ref):
        @pl.when(pl.program_id(0) == 0)
        def _():
            for ref in (dw_ref, dbs_ref, dg_ref, db_ref):
                ref[...] = jnp.zeros_like(ref)
            _fill_sgu_weights(ws_ref, bs_ref, w_ref, bst_ref, wt_ref)

        dm = _dot_nt(dx_ref[...], wout_ref[...])
        sa = _sig(gab_ref[:, 0:D])
        sb = _sig(gab_ref[:, D:2 * D])
        dpa = (dm * sa).astype(BF16)
        dpb = (dm * sb).astype(BF16)
        dpa_ref[...] = dpa
        dpb_ref[...] = dpb
        dp_ref[:, 2 * D:3 * D] = (dm * pa_ref[...].astype(F32) * (sa * (1.0 - sa))).astype(BF16)
        dp_ref[:, 3 * D:4 * D] = (dm * pb_ref[...].astype(F32) * (sb * (1.0 - sb))).astype(BF16)
        dya_ref[...] = _dot_nt(dpa, wpa_ref[...]).astype(BF16)
        dyb_v = _dot_nt(dpb, wpb_ref[...])

        gu, dgu = _gelu_and_grad(uv_ref[:, 0:D])
        gv, dgv = _gelu_and_grad(uv_ref[:, D:2 * D])
        vhat, rstd = _layer_norm_stats(gv)
        lng = g_ref[...]
        vnb = (vhat * lng + b_ref[...]).astype(BF16)
        mixed = _sgu_mix(w_ref, vnb, bst_ref, n_chunk)
        dp_ref[:, 0:D] = (dyb_v * mixed * dgu).astype(BF16)
        dmix = dyb_v * gu
        dmb = dmix.astype(BF16)
        keep = _causal_mask()
        dvn_cols, dbs_rows = [], []
        for g in range(N_GROUP):
            sl = slice(CHUNK * g, CHUNK * (g + 1))
            dmg = dmb[:, sl].reshape(n_chunk, CHUNK, CHUNK)
            vg = vnb[:, sl].reshape(n_chunk, CHUNK, CHUNK)
            wtb = jnp.broadcast_to(wt_ref[g][None], (n_chunk, CHUNK, CHUNK))
            dvn = lax.dot_general(wtb, dmg, (((2,), (1,)), ((0,), (0,))), preferred_element_type=F32)
            dvn_cols.append(dvn.reshape(tm, CHUNK))
            dw = lax.dot_general(dmg, vg, (((2,), (2,)), ((0,), (0,))), preferred_element_type=F32)
            dw_ref[g] += jnp.where(keep, jnp.sum(dw, axis=0), 0.0)
            per_token = jnp.sum(dmix[:, sl], axis=1)
            dbs_rows.append(jnp.sum(per_token.reshape(n_chunk, CHUNK), axis=0, keepdims=True))
        dbs_ref[...] += jnp.concatenate(dbs_rows, axis=0)
        dvn = jnp.concatenate(dvn_cols, axis=1)
        dg_ref[...] += _colsum(dvn * vhat)
        db_ref[...] += _colsum(dvn)
        dvhat = dvn * lng
        dgv_in = rstd * (dvhat - jnp.mean(dvhat, axis=-1, keepdims=True)
                         - vhat * jnp.mean(dvhat * vhat, axis=-1, keepdims=True))
        dp_ref[:, D:2 * D] = (dgv_in * dgv).astype(BF16)

    tile = pl.BlockSpec((tm, D), lambda i: (i, 0))
    vec = pl.BlockSpec((1, D), lambda i: (0, 0))
    w = _resident((D, D))
    wsp = pl.BlockSpec((N_GROUP, CHUNK, CHUNK), lambda i: (0, 0, 0))
    return _call(
        body, name="merge_sgu_bwd", grid=(T // tm,),
        in_specs=[tile, pl.BlockSpec((tm, 2 * D), lambda i: (i, 1)), pl.BlockSpec((tm, 2 * D), lambda i: (i, 2)),
                  tile, tile, vec, vec, pl.BlockSpec(sgu_ws.shape, lambda i: (0, 0, 0, 0)),
                  pl.BlockSpec(sgu_bs.shape, lambda i: (0, 0, 0)), w, w, w],
        out_specs=[tile, tile, tile, pl.BlockSpec((tm, 4 * D), lambda i: (i, 0)), wsp,
                   pl.BlockSpec((N_GROUP, CHUNK), lambda i: (0, 0)), vec, vec],
        out_shape=[S((T, D), BF16), S((T, D), BF16), S((T, D), BF16), S((T, IN_COLS), BF16),
                   S((N_GROUP, CHUNK, CHUNK), F32), S((N_GROUP, CHUNK), F32), S((1, D), F32), S((1, D), F32)],
        scratch_shapes=[_SGU_W_BF16, _SGU_W_BF16, _SGU_BT],
        args=(dx1b, proj, proj, pa, pb, ln_g, ln_b, sgu_ws, sgu_bs, wpa, wpb, wout), sides=sides)


def _rglru_bwd(dya, dproj, proj, hseq, xc, gates, conv_w, rg_wa, rg_wx, lam, sides=()):
    T = dya.shape[0]
    tm = min(TM_SCAN, T)
    n = T // tm
    per8 = tm // SUBLANES

    def body(dya_ref, _, rx_ref, rxp_ref, gate_ref, h_ref, hp_ref, xc_ref, gates_ref, cw_ref, wah_ref, wxh_ref,
             lam_ref, dab_ref, dcw_ref, dcb_ref, dwah_ref, dwxh_ref, dba_ref, dbx_ref, dlam_ref,
             hext, rext, dext, carry_a, carry_dh, a_s, b_s, h_s, wa_ref, wx_ref, dwa_ref, dwx_ref):
        i = pl.program_id(0)
        first_tile = i == n - 1

        @pl.when(i == 0)
        def _():
            for ref in (dcw_ref, dcb_ref, dwa_ref, dwx_ref, dba_ref, dbx_ref, dlam_ref, carry_a, carry_dh):
                ref[...] = jnp.zeros_like(ref)
            dext[tm:tm + SUBLANES, :] = jnp.zeros((SUBLANES, D), F32)
            _fill_blockdiag(wah_ref, wa_ref)
            _fill_blockdiag(wxh_ref, wx_ref)

        gel, dgel = _gelu_and_grad(gate_ref[...])
        dya_v = dya_ref[...].astype(F32)
        hseq_v = h_ref[...]
        dgate = dya_v * hseq_v * dgel
        xcv = xc_ref[...]
        lam_v = lam_ref[...]
        sp = _softplus_neg(lam_v)
        xb = xcv.astype(BF16)
        r, gi, a, m = (gates_ref[:, D * k:D * (k + 1)] for k in range(N_GATES))

        row = lax.broadcasted_iota(jnp.int32, (tm, D), 0)
        c = jnp.where(row == tm - 1, carry_a[0:1, :], _shift_up(a, 1, 0.0))
        dH = _scan_tile(c, dya_v * gel, carry_dh[0:1, :], a_s, b_s, h_s, reverse=True)
        carry_a[...] = jnp.broadcast_to(a[0:1, :], (SUBLANES, D))
        carry_dh[...] = jnp.broadcast_to(dH[0:1, :], (SUBLANES, D))

        hext[0:SUBLANES, :] = jnp.where(first_tile, 0.0, hp_ref[...])
        hext[SUBLANES:SUBLANES + tm, :] = hseq_v
        h_prev = hext[pl.ds(SUBLANES - 1, tm), :]

        d_m = dH * (gi * xcv)
        d_la = dH * h_prev * a - d_m * (a * a) / m
        d_ia = dH * m * xcv * (gi * (1.0 - gi))
        d_ra = d_la * ((-RG_C) * sp) * (r * (1.0 - r))
        dlam_ref[...] += _colsum(d_la * ((-RG_C) * r)) * (-_sig(-lam_v))
        dba_ref[...] += _colsum(d_ra)
        dbx_ref[...] += _colsum(d_ia)
        drab = d_ra.astype(BF16)
        diab = d_ia.astype(BF16)
        dxc_cols = []
        for j in range(N_RG_BLOCK):
            sl = slice(RG_BLOCK * j, RG_BLOCK * (j + 1))
            dxc_cols.append(_dot_nt(drab[:, sl], wa_ref[j]) + _dot_nt(diab[:, sl], wx_ref[j]))
            dwa_ref[j] += _dot_tn(xb[:, sl], drab[:, sl])
            dwx_ref[j] += _dot_tn(xb[:, sl], diab[:, sl])
        dxc = dH * m * gi + jnp.concatenate(dxc_cols, axis=1)

        dcb_ref[...] += _colsum(dxc)
        dext[0:tm, :] = dxc
        rext[0:SUBLANES, :] = jnp.where(first_tile, 0.0, rxp_ref[...])
        rext[SUBLANES:SUBLANES + tm, :] = rx_ref[...]
        drx = jnp.zeros((tm, D), F32)
        for k in range(CONV_WIDTH):
            drx = drx + dext[pl.ds(CONV_WIDTH - 1 - k, tm), :] * cw_ref[k:k + 1, :]
            dcw_ref[k:k + 1, :] += _colsum(dxc * rext[pl.ds(SUBLANES - (CONV_WIDTH - 1) + k, tm), :])
        dext[tm:tm + SUBLANES, :] = dext[0:SUBLANES, :]
        dab_ref[:, 0:D] = drx.astype(BF16)
        dab_ref[:, D:2 * D] = dgate.astype(BF16)

        @pl.when(first_tile)
        def _():
            for j in range(N_RG_BLOCK):
                for h in range(HEADS_PER_BLOCK):
                    sl = slice(HEAD_DIM * h, HEAD_DIM * (h + 1))
                    dwah_ref[HEADS_PER_BLOCK * j + h] = dwa_ref[j, sl, sl]
                    dwxh_ref[HEADS_PER_BLOCK * j + h] = dwx_ref[j, sl, sl]

    def rev(col):
        return lambda i: (n - 1 - i, col)

    def prev8(col):
        return lambda i: (jnp.maximum((n - 1 - i) * per8 - 1, 0), col)

    tile = pl.BlockSpec((tm, D), rev(0))
    vec = pl.BlockSpec((1, D), lambda i: (0, 0))
    heads_in = pl.BlockSpec(rg_wa.shape, lambda i: (0, 0, 0, 0))
    heads_out = pl.BlockSpec(rg_wa.shape[1:], lambda i: (0, 0, 0))
    cw = pl.BlockSpec((CONV_WIDTH, D), lambda i: (0, 0))
    blocks_f32 = pltpu.VMEM((N_RG_BLOCK, RG_BLOCK, RG_BLOCK), F32)
    return _call(
        body, name="rglru_bwd", grid=(n,),
        in_specs=[tile, _ANY, pl.BlockSpec((tm, D), rev(0)), pl.BlockSpec((SUBLANES, D), prev8(0)),
                  pl.BlockSpec((tm, D), rev(1)), tile, pl.BlockSpec((SUBLANES, D), prev8(0)), tile,
                  pl.BlockSpec((tm, N_GATES * D), rev(0)), cw, heads_in, heads_in, vec],
        out_specs=[pl.BlockSpec((tm, 2 * D), rev(2)), cw, vec, heads_out, heads_out, vec, vec, vec],
        out_shape=[S((T, IN_COLS), BF16), S((CONV_WIDTH, D), F32), S((1, D), F32),
                   S(rg_wa.shape[1:], F32), S(rg_wa.shape[1:], F32), S((1, D), F32), S((1, D), F32), S((1, D), F32)],
        scratch_shapes=[pltpu.VMEM((tm + SUBLANES, D), F32), pltpu.VMEM((tm + SUBLANES, D), F32),
                        pltpu.VMEM((tm + SUBLANES, D), F32), pltpu.VMEM((SUBLANES, D), F32),
                        pltpu.VMEM((SUBLANES, D), F32)] + [pltpu.VMEM((D // LANES, tm, LANES), F32)] * 3
        + [_RG_BLOCKS_BF16] * 2 + [blocks_f32] * 2,
        args=(dya, dproj, proj, proj, proj, hseq, hseq, xc, gates, conv_w, rg_wa, rg_wx, lam), sides=sides,
        aliases={1: 0})


def _inproj_bwd(dproj, w_in, x, g, dx1):
    T = x.shape[0]
    tm = min(TM_MM, T)

    def body(dp_ref, w_ref, x_ref, g_ref, dx1_ref, dx_ref, dgm_ref):
        @pl.when(pl.program_id(0) == 0)
        def _():
            dgm_ref[...] = jnp.zeros_like(dgm_ref)

        dh = jnp.zeros((tm, D), F32)
        for p in range(N_PIECE):
            shard, part = divmod((p + DPROJ_ROTATION) % N_PIECE, IN_SHARD // PIECE)
            dh = dh + _dot_nt(dp_ref[:, PIECE * p:PIECE * (p + 1)], w_ref[shard, :, PIECE * part:PIECE * (part + 1)])
        dx, dg_rows = _rms_bwd(dh, x_ref[...], g_ref[...])
        dgm_ref[...] += _colsum(dg_rows)
        dx_ref[...] = dx1_ref[...] + dx

    tile = pl.BlockSpec((tm, D), lambda i: (i, 0))
    vec = pl.BlockSpec((1, D), lambda i: (0, 0))
    return pl.pallas_call(
        body, name="inproj_bwd", grid=(T // tm,),
        in_specs=[pl.BlockSpec((tm, IN_COLS), lambda i: (i, 0)), _resident((N_SHARD, D, IN_SHARD)), tile, vec, tile],
        out_specs=[tile, vec],
        out_shape=[S((T, D), F32), S((1, D), F32)],
        compiler_params=_params(1),
    )(dproj, w_in, x, g, dx1)


def _inproj_wgrad(h, dproj, sides=()):
    T = h.shape[0]
    tk = min(TK_WGRAD, T)
    per = IN_SHARD // PIECE

    def body(h_ref, *refs):
        pieces, o_ref = refs[:per], refs[per]

        @pl.when(pl.program_id(1) == 0)
        def _():
            o_ref[...] = jnp.zeros_like(o_ref)

        o_ref[0] += _dot_tn(h_ref[...], jnp.concatenate([p[...] for p in pieces], axis=1))

    def piece(i):
        return pl.BlockSpec((tk, PIECE), lambda j, k: (k, (per * j + i + N_PIECE - DPROJ_ROTATION) % N_PIECE))

    (out,), side_outs = _call(
        body, name="inproj_wgrad", grid=(N_SHARD, T // tk),
        in_specs=[pl.BlockSpec((tk, D), lambda j, k: (k, 0))] + [piece(i) for i in range(per)],
        out_specs=[pl.BlockSpec((1, D, IN_SHARD), lambda j, k: (j, 0, 0))],
        out_shape=[S((N_SHARD, D, IN_SHARD), F32)], args=(h,) + (dproj,) * per, sides=sides)
    return out, side_outs


def _row_tile(rows):
    for t in range(256, 0, -SUBLANES):
        if rows % t == 0:
            return t
    raise ValueError(rows)


def _add_halves(core, g, theirs, name):
    _, r, cols = g.shape
    half = r // 2
    tr = _row_tile(half)
    nb = half // tr

    def body(core_ref, g_ref, t_ref, o32_ref, o16_ref):
        s = g_ref[...] + t_ref[...]
        o32_ref[...] = s
        o16_ref[...] = s.astype(BF16)

    blk = pl.BlockSpec((1, tr, cols), lambda s, i, core_ref: (s, i, 0))
    gs = pltpu.PrefetchScalarGridSpec(
        num_scalar_prefetch=1, grid=(N_SHARD, nb),
        in_specs=[pl.BlockSpec((1, tr, cols), lambda s, i, core_ref: (s, core_ref[0] * nb + i, 0)), blk],
        out_specs=[blk, blk])
    return pl.pallas_call(
        body, name=name, grid_spec=gs,
        out_shape=[S((N_SHARD, half, cols), F32), S((N_SHARD, half, cols), BF16)],
        compiler_params=_params(2),
    )(core, g, theirs)


def _sum_shards(chip, own, others, name):
    _, half, cols = own.shape
    tr = _row_tile(half)

    def body(chip_ref, own_ref, oth_ref, o_ref):
        acc = own_ref[0]
        for j in range(3):
            acc = acc + oth_ref[j].astype(F32)
        o_ref[...] = acc

    gs = pltpu.PrefetchScalarGridSpec(
        num_scalar_prefetch=1, grid=(half // tr,),
        in_specs=[pl.BlockSpec((1, tr, cols), lambda i, chip_ref: (chip_ref[0], i, 0)),
                  pl.BlockSpec((3, tr, cols), lambda i, chip_ref: (0, i, 0))],
        out_specs=pl.BlockSpec((tr, cols), lambda i, chip_ref: (i, 0)))
    return pl.pallas_call(
        body, name=name, grid_spec=gs, out_shape=S((half, cols), F32), compiler_params=_params(1),
    )(chip, own, others)


def _adamw(w, g, m, v):
    m = ADAM_B1 * m + (1.0 - ADAM_B1) * g
    v = ADAM_B2 * v + (1.0 - ADAM_B2) * (g * g)
    m_hat = m / (1.0 - ADAM_B1 ** ADAM_STEP)
    v_hat = v / (1.0 - ADAM_B2 ** ADAM_STEP)
    delta = -ADAM_LR * (m_hat / (jnp.sqrt(v_hat) + ADAM_EPS) + ADAM_WD * w)
    return delta, m, v


def _adamw_shard(core, mine, theirs, w, m, v, name):
    r, cols = w.shape
    half = r // 2
    tr = _row_tile(half)
    nb = half // tr

    def body(core_ref, mine_ref, theirs_ref, w_ref, m_ref, v_ref, g_ref, d_ref, mo_ref, vo_ref):
        g = jnp.where(pl.program_id(0) == core_ref[0], mine_ref[...], theirs_ref[...])
        g_ref[...] = g
        d_ref[...], mo_ref[...], vo_ref[...] = _adamw(w_ref[...], g, m_ref[...], v_ref[...])

    hblk = pl.BlockSpec((tr, cols), lambda h, i, core_ref: (i, 0))
    blk = pl.BlockSpec((tr, cols), lambda h, i, core_ref: (h * nb + i, 0))
    gs = pltpu.PrefetchScalarGridSpec(num_scalar_prefetch=1, grid=(2, nb),
                                      in_specs=[hblk, hblk, blk, blk, blk], out_specs=[blk] * 4)
    return pl.pallas_call(
        body, name=name, grid_spec=gs, out_shape=[S((r, cols), F32)] * 4, compiler_params=_params(2),
    )(core, mine, theirs, w, m, v)


def _adamw_whole(w, g, m, v, name):
    def body(w_ref, g_ref, m_ref, v_ref, d_ref, mo_ref, vo_ref):
        d_ref[...], mo_ref[...], vo_ref[...] = _adamw(w_ref[...], g_ref[...], m_ref[...], v_ref[...])

    return pl.pallas_call(body, name=name, out_shape=[S(w.shape, F32)] * 3)(w, g, m, v)


_VEC_ROWS = ("norm_mix_g", "conv_b", "rg_lambda", "sgu_ln_g", "sgu_ln_b", "norm_ffn_g", "norm_final_g", "rg_ba",
             "rg_bx")
_CONV_ROW = len(_VEC_ROWS)
_LOSS_ROW = _CONV_ROW + CONV_WIDTH
_VEC_PAD = -(_LOSS_ROW + 1) % SUBLANES
_HEAD_BIASES = ("rg_ba", "rg_bx")
_TENSORS = ("sgu_bs", "rg_wa", "rg_wx", "sgu_ws")


def _small_sum_adamw(parts, w, m, v):
    names = [n for n in _VEC_ROWS] + list(_TENSORS)
    n_parts = len(parts)

    def total(ref):
        acc = ref[0]
        for k in range(1, N_DEVICE):
            acc = acc + ref[k]
        return acc

    def body(*refs):
        part_refs, refs = refs[:n_parts], refs[n_parts:]
        w_refs, m_refs, v_refs = (dict(zip(names, refs[k * len(names):(k + 1) * len(names)])) for k in range(3))
        outs = refs[3 * len(names):]
        out_refs = {n: outs[4 * k:4 * k + 4] for k, n in enumerate(names)}
        conv_ref, loss_ref = outs[4 * len(names):]
        vec = total(part_refs[0])
        grads = {n: total(p)[None] for n, p in zip(_TENSORS, part_refs[1:])}
        for row, n in enumerate(_VEC_ROWS):
            g = vec[row:row + 1, :]
            if n in _HEAD_BIASES:
                g = jnp.concatenate([g[:, HEAD_DIM * h:HEAD_DIM * (h + 1)] for h in range(D // HEAD_DIM)], axis=0)[None]
            grads[n] = g
        for n in names:
            g_ref, d_ref, mo_ref, vo_ref = out_refs[n]
            g_ref[...] = grads[n]
            d_ref[...], mo_ref[...], vo_ref[...] = _adamw(w_refs[n][...], grads[n], m_refs[n][...], v_refs[n][...])
        conv_ref[...] = vec[_CONV_ROW:_CONV_ROW + CONV_WIDTH, :]
        loss_ref[...] = vec[_LOSS_ROW:_LOSS_ROW + 1, 0:1]

    res = pl.pallas_call(
        body, name="small_sum_adamw",
        out_shape=[S(w[n].shape, F32) for n in names for _ in range(4)] + [S((CONV_WIDTH, D), F32), S((1, 1), F32)],
        compiler_params=pltpu.CompilerParams(vmem_limit_bytes=VMEM_LIMIT),
    )(*parts, *[w[n] for n in names], *[m[n] for n in names], *[v[n] for n in names])
    return {n: tuple(res[4 * k:4 * k + 4]) for k, n in enumerate(names)}, res[-2], res[-1]


_BIG = ("w_in", "w_proj_a", "w_proj_b", "w_out", "w_gate_up", "w_down")
_WEIGHTS = ("norm_mix_g", "w_in", "conv_w", "conv_b", "rg_wa", "rg_ba", "rg_wx", "rg_bx", "rg_lambda", "sgu_ln_g",
            "sgu_ln_b", "sgu_ws", "sgu_bs", "w_proj_a", "w_proj_b", "w_out", "norm_ffn_g", "w_gate_up", "w_down",
            "norm_final_g")


def kernel(x, norm_mix_g, w_in, conv_w, conv_b, rg_wa, rg_ba, rg_wx, rg_bx, rg_lambda, sgu_ln_g, sgu_ln_b, sgu_ws, sgu_bs, w_proj_a, w_proj_b, w_out, norm_ffn_g, w_gate_up, w_down, norm_final_g, loss_target, m_norm_mix_g, m_w_in, m_conv_w, m_conv_b, m_rg_wa, m_rg_ba, m_rg_wx, m_rg_bx, m_rg_lambda, m_sgu_ln_g, m_sgu_ln_b, m_sgu_ws, m_sgu_bs, m_w_proj_a, m_w_proj_b, m_w_out, m_norm_ffn_g, m_w_gate_up, m_w_down, m_norm_final_g, v_norm_mix_g, v_w_in, v_conv_w, v_conv_b, v_rg_wa, v_rg_ba, v_rg_wx, v_rg_bx, v_rg_lambda, v_sgu_ln_g, v_sgu_ln_b, v_sgu_ws, v_sgu_bs, v_w_proj_a, v_w_proj_b, v_w_out, v_norm_ffn_g, v_w_gate_up, v_w_down, v_norm_final_g):
    args = dict(locals())
    w = {n: args[n] for n in _WEIGHTS}
    mom = {n: args["m_" + n] for n in _WEIGHTS}
    var = {n: args["v_" + n] for n in _WEIGHTS}
    xi, yi, ci = _position()
    core = ci.astype(jnp.int32).reshape(1)
    chip = (2 * xi + yi).astype(jnp.int32).reshape(1)

    bf = {n: w[n][0].astype(BF16) for n in _BIG}
    final_g = w["norm_final_g"].reshape(1, D)
    ba, bx = w["rg_ba"].reshape(1, D), w["rg_bx"].reshape(1, D)
    lam, ln_g, ln_b = w["rg_lambda"], w["sgu_ln_g"], w["sgu_ln_b"]
    x0, target = x[0], loss_target[0]

    def shard_major(g):
        return g.reshape(N_SHARD, g.shape[0] // N_SHARD, g.shape[1])

    def chip_sums(names, grads, theirs):
        return [_add_halves(core, g, t, "add_halves_" + n) for n, g, t in zip(names, grads, theirs)]

    def my_halves(names, sums, arrived):
        return [_sum_shards(chip, p32, a, "sum_shards_" + n) for n, (p32, _), a in zip(names, sums, arrived)]

    (w_in_a,), (conv_a,) = _comm_only([_gather_half_side([bf["w_in"]]), _gather_side([w["conv_w"][0]])], "gather_w_in")
    conv_cols = conv_a.shape[-1]
    conv_full = jnp.swapaxes(conv_a, 0, 1).reshape(CONV_WIDTH, D)
    (proj, h), ((w_pa_a, w_pb_a, w_out_a),) = _inproj(
        x0, w["norm_mix_g"], w_in_a, sides=[_gather_half_side([bf["w_proj_a"], bf["w_proj_b"], bf["w_out"]])])
    (ya, xc, hseq, gates), ((w_gu_a, w_down_a),) = _rglru_fwd(
        proj, conv_full, w["conv_b"], w["rg_wa"], w["rg_wx"], ba, bx, lam,
        sides=[_gather_half_side([bf["w_gate_up"], bf["w_down"]])])
    wpa, wpb, wout, wdown = w_pa_a.reshape(D, D), w_pb_a.reshape(D, D), w_out_a.reshape(D, D), w_down_a.reshape(D_FF, D)
    yb, pa, pb, mb, x1 = _sgu_merge_fwd(x0, proj, ya, ln_g, ln_b, w["sgu_ws"], w["sgu_bs"], wpa, wpb, wout)
    h2, gu, act, dx2, dx2b, loss, d_final_g = _ffn_fwd_loss(x1, w["norm_ffn_g"], w_gu_a, wdown, final_g, target)

    dgu, dx1, dx1b, d_ffn_g = _ffn_bwd(dx2, dx2b, gu, wdown, w_gu_a, x1, w["norm_ffn_g"])
    ffn = ("w_gate_up", "w_down")
    g_ffn = [_matmul_tn(h2, dgu, FF_SHARD, True, "wgrad_gate_up")[0],
             shard_major(_matmul_tn(act, dx2b, D // 2, False, "wgrad_down")[0])]
    (dpa, dpb, dya, dproj, d_ws, d_bs, d_lng, d_lnb), (theirs_ffn,) = _merge_sgu_bwd(
        dx1b, proj, pa, pb, ln_g, ln_b, w["sgu_ws"], w["sgu_bs"], wpa, wpb, wout, sides=[_halves_side(g_ffn)])
    sums_ffn = chip_sums(ffn, g_ffn, theirs_ffn)
    mix = ("w_proj_a", "w_proj_b", "w_out")
    g_mix = [shard_major(_matmul_tn(ya, dpa, D, False, "wgrad_proj_a")[0]),
             shard_major(_matmul_tn(yb, dpb, D, False, "wgrad_proj_b")[0]),
             shard_major(_matmul_tn(mb, dx1b, D, False, "wgrad_out")[0])]
    (dproj, d_cw, d_cb, d_wa, d_wx, d_ba, d_bx, d_lam), (arrived_ffn, theirs_mix) = _rglru_bwd(
        dya, dproj, proj, hseq, xc, gates, conv_full, w["rg_wa"], w["rg_wx"], lam,
        sides=[_scatter_side([p16 for _, p16 in sums_ffn]), _halves_side(g_mix)])
    mine_ffn = my_halves(ffn, sums_ffn, arrived_ffn)
    sums_mix = chip_sums(mix, g_mix, theirs_mix)
    grad_x, d_mix_g = _inproj_bwd(dproj, w_in_a, x0, w["norm_mix_g"], dx1)
    rows = {"norm_mix_g": d_mix_g, "conv_b": d_cb, "rg_lambda": d_lam, "sgu_ln_g": d_lng, "sgu_ln_b": d_lnb,
            "norm_ffn_g": d_ffn_g, "norm_final_g": d_final_g, "rg_ba": d_ba, "rg_bx": d_bx}
    vec = jnp.concatenate([rows[n] for n in _VEC_ROWS]
                          + [d_cw, jnp.pad(loss, ((0, _VEC_PAD), (0, D - 1)))], axis=0)
    tensors = {"sgu_bs": d_bs, "rg_wa": d_wa, "rg_wx": d_wx, "sgu_ws": d_ws}
    g_in, (small_parts, other_ffn, arrived_mix) = _inproj_wgrad(
        h, dproj, sides=[_everyone_side([vec] + [tensors[n] for n in _TENSORS]), _swap_side(mine_ffn),
                         _scatter_side([p16 for _, p16 in sums_mix])])
    mine_mix = my_halves(mix, sums_mix, arrived_mix)
    theirs_in, other_mix = _comm_only([_halves_side([g_in]), _swap_side(mine_mix)], "halves_w_in")
    sums_in = chip_sums(("w_in",), [g_in], theirs_in)
    arrived_in = _comm_only([_scatter_side([p16 for _, p16 in sums_in])], "scatter_w_in")[0]
    mine_in = my_halves(("w_in",), sums_in, arrived_in)
    other_in = _comm_only([_swap_side(mine_in)], "swap_w_in")[0]

    out = {}
    for n, gm, go in zip(ffn + mix + ("w_in",), mine_ffn + mine_mix + mine_in, other_ffn + other_mix + other_in):
        g, d, mo, vo = _adamw_shard(core, gm, go, w[n][0], mom[n][0], var[n][0], "adamw_" + n)
        out[n] = tuple(a[None] for a in (g, d, mo, vo))
    as_row = lambda t: {n: a.reshape(1, D) if n == "norm_final_g" else a for n, a in t.items()}
    small_out, conv_sum, loss_sum = _small_sum_adamw(small_parts, as_row(w), as_row(mom), as_row(var))
    out.update(small_out)
    out["norm_final_g"] = tuple(a.reshape(D) for a in small_out["norm_final_g"])
    conv_g = lax.dynamic_slice_in_dim(conv_sum, chip[0] * conv_cols, conv_cols, axis=1)
    d, mo, vo = _adamw_whole(w["conv_w"][0], conv_g, mom["conv_w"][0], var["conv_w"][0], "adamw_conv_w")
    out["conv_w"] = tuple(a[None] for a in (conv_g, d, mo, vo))

    return (loss_sum[0, 0], grad_x[None], *[out[n][0] for n in _WEIGHTS], *[out[n][1] for n in _WEIGHTS],
            *[out[n][2] for n in _WEIGHTS], *[out[n][3] for n in _WEIGHTS])
```

```python
import functools

import jax
import jax.numpy as jnp
from jax import lax
from jax.experimental import pallas as pl
from jax.experimental.pallas import tpu as pltpu

F32 = jnp.float32
BF16 = jnp.bfloat16
S = jax.ShapeDtypeStruct

D = 1024
N_SHARD = 4
IN_COLS = 6 * D
IN_SHARD = IN_COLS // N_SHARD
D_FF = 2816
FF_SHARD = 2 * D_FF // N_SHARD
RG_BLOCK = 256
N_RG_BLOCK = D // RG_BLOCK
CHUNK = 128
N_GROUP = 8
CONV_WIDTH = 4
RG_C = 8.0
EPS = 1e-6
ADAM_LR, ADAM_B1, ADAM_B2, ADAM_EPS, ADAM_WD, ADAM_STEP = 0.001, 0.9, 0.999, 1e-08, 0.01, 10

V7X_VMEM_BYTES = 64 * 1024 * 1024
VMEM_LIMIT = V7X_VMEM_BYTES * 3 // 4
SUBLANES = 8
MESH = pl.DeviceIdType.MESH

TM_MM = 512
TM_SCAN = 256
TM_FF = 256
TK_WGRAD = 1024


def _params(n_axes):
    return pltpu.CompilerParams(dimension_semantics=("arbitrary",) * n_axes, vmem_limit_bytes=VMEM_LIMIT)


def _resident(shape):
    nd = len(shape)
    return pl.BlockSpec(shape, lambda *_: (0,) * nd, pipeline_mode=pl.Buffered(1))


def _sig(x):
    return 1.0 / (1.0 + jnp.exp(-x))


_GELU_K2 = 2.0 * 0.7978845608028654
_GELU_C = 0.044715


def _gelu(x):
    return x * _sig(x * (_GELU_K2 + (_GELU_K2 * _GELU_C) * (x * x)))


def _gelu_and_grad(x):
    x2 = x * x
    s = _sig(x * (_GELU_K2 + (_GELU_K2 * _GELU_C) * x2))
    g = x * s
    return g, s + g * (1.0 - s) * (_GELU_K2 + (3.0 * _GELU_K2 * _GELU_C) * x2)


_EXPM1_SERIES = tuple(1.0 / f for f in (5040.0, 720.0, 120.0, 24.0, 6.0, 2.0, 1.0))


def _one_minus_exp(x):
    p = _EXPM1_SERIES[0]
    for coef in _EXPM1_SERIES[1:]:
        p = p * x + coef
    return jnp.where(x > -0.125, -x * p, 1.0 - jnp.exp(x))


def _softplus_neg(lam):
    z = -lam
    e = jnp.exp(-jnp.abs(z))
    u = 1.0 + e
    log1p = jnp.where(u == 1.0, e, jnp.log(u) * e / (u - 1.0))
    return jnp.maximum(z, 0.0) + log1p


def _rms_stats(x):
    return lax.rsqrt(jnp.mean(x * x, axis=-1, keepdims=True) + EPS)


def _rms_bwd(dy, x, g):
    rstd = _rms_stats(x)
    xhat = x * rstd
    dxhat = dy * g
    dx = rstd * (dxhat - xhat * jnp.mean(dxhat * xhat, axis=-1, keepdims=True))
    return dx, dy * xhat


def _colsum(x):
    return jnp.sum(x, axis=0, keepdims=True)


def _shift_down(x, d, fill):
    n = x.shape[0]
    if d % SUBLANES == 0:
        return jnp.concatenate([jnp.full((d, x.shape[1]), fill, x.dtype), x[:n - d]], axis=0)
    row = lax.broadcasted_iota(jnp.int32, x.shape, 0)
    return jnp.where(row < d, fill, pltpu.roll(x, d, 0))


def _shift_up(x, d, fill):
    n = x.shape[0]
    if d % SUBLANES == 0:
        return jnp.concatenate([x[d:], jnp.full((d, x.shape[1]), fill, x.dtype)], axis=0)
    row = lax.broadcasted_iota(jnp.int32, x.shape, 0)
    return jnp.where(row >= n - d, fill, pltpu.roll(x, n - d, 0))


def _scan(a, b, shift):
    d = 1
    while d < a.shape[0]:
        b = a * shift(b, d, 0.0) + b
        a = a * shift(a, d, 1.0)
        d *= 2
    return a, b


LANES = 128


def _scan_tile(a, b, outside, a_s, b_s, h_s, reverse):
    tm = a.shape[0]
    groups = tm // SUBLANES
    order = list(range(SUBLANES - 1, -1, -1) if reverse else range(SUBLANES))
    shift = _shift_up if reverse else _shift_down
    edge = groups - 1 if reverse else 0
    for j in range(D // LANES):
        a_s[j] = a[:, LANES * j:LANES * (j + 1)]
        b_s[j] = b[:, LANES * j:LANES * (j + 1)]
    for j in range(D // LANES):
        def slab(ref, k):
            return ref[j, pl.ds(k, groups, stride=SUBLANES), :]

        ga, gb = slab(a_s, order[0]), slab(b_s, order[0])
        for k in order[1:]:
            ak = slab(a_s, k)
            gb = ak * gb + slab(b_s, k)
            ga = ak * ga
        ga, gb = _scan(ga, gb, shift)
        h_out = outside[:, LANES * j:LANES * (j + 1)]
        group_end = ga * h_out + gb
        row = lax.broadcasted_iota(jnp.int32, (groups, LANES), 0)
        h = jnp.where(row == edge, h_out, shift(group_end, 1, 0.0))
        for k in order:
            h = slab(a_s, k) * h + slab(b_s, k)
            h_s[j, pl.ds(k, groups, stride=SUBLANES), :] = h
    return jnp.concatenate([h_s[j] for j in range(D // LANES)], axis=1)


def _dot(a, b):
    return jnp.dot(a, b, preferred_element_type=F32)


def _dot_nt(a, b):
    return lax.dot_general(a, b, (((1,), (1,)), ((), ())), preferred_element_type=F32)


def _dot_tn(a, b):
    return lax.dot_general(a, b, (((0,), (0,)), ((), ())), preferred_element_type=F32)


_ANY = pl.BlockSpec(memory_space=pl.ANY)


def _position():
    return lax.axis_index("x"), lax.axis_index("y"), lax.axis_index("c")


def _other_chips(x, y):
    return [(1 - x, y), (x, 1 - y), (1 - x, 1 - y)]


class _Side:
    def __init__(self, inputs, out_shapes, n_sems, make):
        self.inputs, self.out_shapes, self.n_sems, self.make = list(inputs), list(out_shapes), n_sems, make


MID_STEP = 0.625


def _call(body, *, name, grid, in_specs, out_specs, out_shape, args, scratch_shapes=(), sides=(), aliases=None):
    n_in, n_out, n_scr = len(in_specs), len(out_specs), len(scratch_shapes)
    side_in = [len(s.inputs) for s in sides]
    side_out = [len(s.out_shapes) for s in sides]

    def wrapped(*refs):
        refs = list(refs)
        take = lambda k: [refs.pop(0) for _ in range(k)]
        ins = take(n_in)
        sins = [take(k) for k in side_in]
        outs = take(n_out)
        souts = [take(k) for k in side_out]
        scr = take(n_scr)
        sems = [take(3) for _ in sides]
        def run(phase):
            for s, si, so, sem in zip(sides, sins, souts, sems):
                for thunk in s.make(si, so, *sem)[phase]:
                    thunk()

        if sides:
            n_steps = functools.reduce(lambda a, b: a * b, grid)
            step = functools.reduce(lambda a, b: a + b, [
                pl.program_id(a) * functools.reduce(lambda p, q: p * q, grid[a + 1:], 1) for a in range(len(grid))])
            pl.when(step == 0)(lambda: run(0))
        body(*ins, *outs, *scr)
        if sides:
            pl.when(step == int(MID_STEP * (n_steps - 1)))(lambda: run(1))
            pl.when(step == n_steps - 1)(lambda: run(2))

    res = pl.pallas_call(
        wrapped, name=name, grid=grid,
        in_specs=list(in_specs) + [_ANY] * sum(side_in),
        out_specs=list(out_specs) + [_ANY] * sum(side_out),
        out_shape=list(out_shape) + [o for s in sides for o in s.out_shapes],
        scratch_shapes=list(scratch_shapes) + [pltpu.SemaphoreType.DMA((s.n_sems,)) for s in sides for _ in range(3)],
        input_output_aliases=aliases or {},
        compiler_params=_params(len(grid)),
    )(*args, *[a for s in sides for a in s.inputs])
    main, rest, per_side = list(res[:n_out]), list(res[n_out:]), []
    for k in side_out:
        per_side.append(rest[:k])
        rest = rest[k:]
    return main, per_side


def _comm_only(sides, name):
    def body():
        pass

    return _call(body, name=name, grid=(1,), in_specs=[], out_specs=[], out_shape=[], args=[], sides=sides)[1]


def _remote(src, dst, send, recv, k, device):
    return pltpu.make_async_remote_copy(src_ref=src, dst_ref=dst, send_sem=send.at[k], recv_sem=recv.at[k],
                                        device_id=device, device_id_type=MESH)


def _both_ways(copy, keys):
    return [lambda k=k: copy(k).start() for k in keys], [], [lambda k=k: copy(k).wait() for k in keys]


def _gather_side(shards):
    n = len(shards)

    def make(ins, outs, send, recv, local):
        x, y, c = _position()
        mine = 2 * x + y
        chips = _other_chips(x, y)
        pairs = [(w, j) for w in range(n) for j in range(3)]

        def own(w):
            return pltpu.make_async_copy(ins[w], outs[w].at[mine], local.at[w])

        def push(w, j):
            return _remote(ins[w], outs[w].at[mine], send, recv, 3 * w + j, (*chips[j], c))

        def arrival(w, j):
            px, py = chips[j]
            return _remote(ins[w], outs[w].at[2 * px + py], send, recv, 3 * w + j, (px, py, c))

        starts = [lambda w=w: own(w).start() for w in range(n)] + [lambda w=w, j=j: push(w, j).start() for w, j in pairs]
        waits = ([lambda w=w, j=j: arrival(w, j).wait_recv() for w, j in pairs]
                 + [lambda w=w, j=j: push(w, j).wait_send() for w, j in pairs]
                 + [lambda w=w: own(w).wait() for w in range(n)])
        return starts, [], waits

    return _Side(shards, [S((N_SHARD,) + s.shape, s.dtype) for s in shards], 3 * n, make)


def _gather_half_side(shards):
    n = len(shards)

    def make(ins, outs, send, recv, local):
        x, y, c = _position()
        mine = 2 * x + y
        chips = _other_chips(x, y)
        pairs = [(w, j) for w in range(n) for j in range(3)]

        def rows(w, core):
            half = ins[w].shape[0] // 2
            return pl.ds(core * half, half)

        def own(w):
            return pltpu.make_async_copy(ins[w], outs[w].at[mine], local.at[w])

        def push(w, j):
            return _remote(ins[w].at[rows(w, c), :], outs[w].at[mine, rows(w, c), :], send, recv, 3 * w + j,
                           (*chips[j], c))

        def landed(w, j, core):
            px, py = chips[j]
            return outs[w].at[2 * px + py, rows(w, core), :]

        def arrival(w, j):
            return _remote(ins[w].at[rows(w, c), :], landed(w, j, c), send, recv, 3 * w + j, (*chips[j], c))

        def passed(w, j, core):
            return _remote(landed(w, j, core), landed(w, j, core), send, recv, 3 * n + 3 * w + j, (x, y, 1 - c))

        starts = [lambda w=w: own(w).start() for w in range(n)] + [lambda w=w, j=j: push(w, j).start() for w, j in pairs]
        mids = [t for w, j in pairs for t in (lambda w=w, j=j: arrival(w, j).wait_recv(),
                                              lambda w=w, j=j: passed(w, j, c).start())]
        waits = ([lambda w=w, j=j: passed(w, j, 1 - c).wait_recv() for w, j in pairs]
                 + [lambda w=w, j=j: passed(w, j, c).wait_send() for w, j in pairs]
                 + [lambda w=w, j=j: push(w, j).wait_send() for w, j in pairs]
                 + [lambda w=w: own(w).wait() for w in range(n)])
        return starts, mids, waits

    return _Side(shards, [S((N_SHARD,) + s.shape, s.dtype) for s in shards], 6 * n, make)


def _halves_side(grads):
    n = len(grads)

    def make(ins, outs, send, recv, local):
        x, y, c = _position()

        def copy(w):
            half = ins[w].shape[1] // 2
            return _remote(ins[w].at[:, pl.ds((1 - c) * half, half), :], outs[w], send, recv, w, (x, y, 1 - c))

        return _both_ways(copy, range(n))

    return _Side(grads, [S((N_SHARD, g.shape[1] // 2, g.shape[2]), F32) for g in grads], n, make)


def _scatter_side(partials):
    n = len(partials)

    def make(ins, outs, send, recv, local):
        x, y, c = _position()
        chips = _other_chips(x, y)

        def copy(k):
            w, j = divmod(k, 3)
            px, py = chips[j]
            return _remote(ins[w].at[2 * px + py], outs[w].at[j], send, recv, k, (px, py, c))

        return _both_ways(copy, range(3 * n))

    return _Side(partials, [S((3,) + p.shape[1:], p.dtype) for p in partials], 3 * n, make)


def _swap_side(halves):
    n = len(halves)

    def make(ins, outs, send, recv, local):
        x, y, c = _position()
        return _both_ways(lambda w: _remote(ins[w], outs[w], send, recv, w, (x, y, 1 - c)), range(n))

    return _Side(halves, [S(h.shape, h.dtype) for h in halves], n, make)


N_DEVICE = 8


def _everyone_side(arrays):
    n = len(arrays)
    peers = N_DEVICE - 1

    def make(ins, outs, send, recv, local):
        x, y, c = _position()
        mine = 4 * x + 2 * y + c
        pairs = [(w, k) for w in range(n) for k in range(1, N_DEVICE)]

        def peer(k):
            return (1 - x if k & 4 else x, 1 - y if k & 2 else y, 1 - c if k & 1 else c)

        def own(w):
            return pltpu.make_async_copy(ins[w], outs[w].at[mine], local.at[w])

        def push(w, k):
            return _remote(ins[w], outs[w].at[mine], send, recv, peers * w + k - 1, peer(k))

        def arrival(w, k):
            px, py, pc = peer(k)
            return _remote(ins[w], outs[w].at[4 * px + 2 * py + pc], send, recv, peers * w + k - 1, (px, py, pc))

        starts = [lambda w=w: own(w).start() for w in range(n)] + [lambda w=w, k=k: push(w, k).start() for w, k in pairs]
        waits = ([lambda w=w, k=k: arrival(w, k).wait_recv() for w, k in pairs]
                 + [lambda w=w, k=k: push(w, k).wait_send() for w, k in pairs]
                 + [lambda w=w: own(w).wait() for w in range(n)])
        return starts, [], waits

    return _Side(arrays, [S((N_DEVICE,) + a.shape, a.dtype) for a in arrays], peers * n, make)


def _inproj(x, g, w_in, sides=()):
    T = x.shape[0]
    tm = min(TM_FF, T)

    def body(x_ref, g_ref, w_ref, proj_ref, h_ref):
        xv = x_ref[...]
        h = (xv * _rms_stats(xv) * g_ref[...]).astype(BF16)
        h_ref[...] = h
        for k in range(N_SHARD):
            proj_ref[:, IN_SHARD * k:IN_SHARD * (k + 1)] = _dot(h, w_ref[k])

    return _call(
        body, name="inproj", grid=(T // tm,),
        in_specs=[pl.BlockSpec((tm, D), lambda i: (i, 0)), pl.BlockSpec((1, D), lambda i: (0, 0)),
                  _resident((N_SHARD, D, IN_SHARD))],
        out_specs=[pl.BlockSpec((tm, IN_COLS), lambda i: (i, 0)), pl.BlockSpec((tm, D), lambda i: (i, 0))],
        out_shape=[S((T, IN_COLS), F32), S((T, D), BF16)],
        args=(x, g, w_in), sides=sides)


def _rg_gates(xc, wa_ref, wx_ref, ba, bx, sp):
    xb = xc.astype(BF16)
    blocks = [xb[:, RG_BLOCK * j:RG_BLOCK * (j + 1)] for j in range(N_RG_BLOCK)]
    r = _sig(jnp.concatenate([_dot(blocks[j], wa_ref[j]) for j in range(N_RG_BLOCK)], axis=1) + ba)
    gi = _sig(jnp.concatenate([_dot(blocks[j], wx_ref[j]) for j in range(N_RG_BLOCK)], axis=1) + bx)
    log_a = (-RG_C) * r * sp
    a = jnp.exp(log_a)
    m = jnp.sqrt(_one_minus_exp(2.0 * log_a))
    return xb, r, gi, a, m


N_GATES = 4
HEADS_PER_BLOCK = 4
HEAD_DIM = RG_BLOCK // HEADS_PER_BLOCK
_RG_BLOCKS_BF16 = pltpu.VMEM((N_RG_BLOCK, RG_BLOCK, RG_BLOCK), BF16)


def _fill_blockdiag(heads_ref, blocks):
    blocks[...] = jnp.zeros_like(blocks)
    for j in range(N_RG_BLOCK):
        for h in range(HEADS_PER_BLOCK):
            sl = slice(HEAD_DIM * h, HEAD_DIM * (h + 1))
            blocks[j, sl, sl] = heads_ref[0, HEADS_PER_BLOCK * j + h].astype(BF16)


def _rglru_fwd(proj, conv_w, conv_b, rg_wa, rg_wx, ba, bx, lam, sides=()):
    T = proj.shape[0]
    tm = min(TM_SCAN, T)

    def body(rx_ref, gate_ref, cw_ref, cb_ref, wah_ref, wxh_ref, ba_ref, bx_ref, lam_ref,
             ya_ref, xc_ref, h_ref, gates_ref, ext, hc, a_s, b_s, h_s, wa_ref, wx_ref):
        @pl.when(pl.program_id(0) == 0)
        def _():
            ext[0:SUBLANES, :] = jnp.zeros((SUBLANES, D), F32)
            hc[...] = jnp.zeros((SUBLANES, D), F32)
            _fill_blockdiag(wah_ref, wa_ref)
            _fill_blockdiag(wxh_ref, wx_ref)

        ext[SUBLANES:SUBLANES + tm, :] = rx_ref[...]
        xc = cb_ref[...]
        for k in range(CONV_WIDTH):
            xc = xc + ext[pl.ds(SUBLANES - (CONV_WIDTH - 1) + k, tm), :] * cw_ref[k:k + 1, :]
        ext[0:SUBLANES, :] = ext[tm:tm + SUBLANES, :]
        xc_ref[...] = xc
        _, r, gi, a, m = _rg_gates(xc, wa_ref, wx_ref, ba_ref[...], bx_ref[...], _softplus_neg(lam_ref[...]))
        for k, val in enumerate((r, gi, a, m)):
            gates_ref[:, D * k:D * (k + 1)] = val
        h = _scan_tile(a, m * (gi * xc), hc[0:1, :], a_s, b_s, h_s, reverse=False)
        hc[...] = jnp.broadcast_to(h[tm - 1:tm, :], (SUBLANES, D))
        h_ref[...] = h
        ya_ref[...] = (_gelu(gate_ref[...]) * h).astype(BF16)

    vec = pl.BlockSpec((1, D), lambda i: (0, 0))
    heads = pl.BlockSpec(rg_wa.shape, lambda i: (0, 0, 0, 0))
    tile = pl.BlockSpec((tm, D), lambda i: (i, 0))
    return _call(
        body, name="rglru_fwd", grid=(T // tm,),
        in_specs=[pl.BlockSpec((tm, D), lambda i: (i, 0)), pl.BlockSpec((tm, D), lambda i: (i, 1)),
                  pl.BlockSpec((CONV_WIDTH, D), lambda i: (0, 0)), vec, heads, heads, vec, vec, vec],
        out_specs=[tile, tile, tile, pl.BlockSpec((tm, N_GATES * D), lambda i: (i, 0))],
        out_shape=[S((T, D), BF16), S((T, D), F32), S((T, D), F32), S((T, N_GATES * D), F32)],
        scratch_shapes=[pltpu.VMEM((tm + SUBLANES, D), F32), pltpu.VMEM((SUBLANES, D), F32)]
        + [pltpu.VMEM((D // LANES, tm, LANES), F32)] * 3 + [_RG_BLOCKS_BF16] * 2,
        args=(proj, proj, conv_w, conv_b, rg_wa, rg_wx, ba, bx, lam), sides=sides)


def _layer_norm_stats(v):
    mu = jnp.mean(v, axis=-1, keepdims=True)
    vc = v - mu
    rstd = lax.rsqrt(jnp.mean(vc * vc, axis=-1, keepdims=True) + EPS)
    return vc * rstd, rstd


def _sgu_mix(w_ref, vnb, bst_ref, n_chunk):
    cols = []
    for g in range(N_GROUP):
        vg = vnb[:, CHUNK * g:CHUNK * (g + 1)].reshape(n_chunk, CHUNK, CHUNK)
        wb = jnp.broadcast_to(w_ref[g][None], (n_chunk, CHUNK, CHUNK))
        mg = lax.dot_general(wb, vg, (((2,), (1,)), ((0,), (0,))), preferred_element_type=F32)
        mg = mg + bst_ref[:, g:g + 1][None]
        cols.append(mg.reshape(n_chunk * CHUNK, CHUNK))
    return jnp.concatenate(cols, axis=1)


def _causal_mask():
    return (lax.broadcasted_iota(jnp.int32, (CHUNK, CHUNK), 0) >= lax.broadcasted_iota(jnp.int32, (CHUNK, CHUNK), 1))


def _fill_sgu_weights(ws_ref, bs_ref, w_tril, bs_t, w_tril_t=None):
    keep = _causal_mask()
    for g in range(N_GROUP):
        wg = jnp.where(keep, ws_ref[0, g], 0.0)
        w_tril[g] = wg.astype(BF16)
        if w_tril_t is not None:
            w_tril_t[g] = wg.T.astype(BF16)
    bs_t[...] = bs_ref[0].T


_SGU_W_BF16 = pltpu.VMEM((N_GROUP, CHUNK, CHUNK), BF16)
_SGU_BT = pltpu.VMEM((CHUNK, N_GROUP), F32)


def _sgu_merge_fwd(x, proj, ya, ln_g, ln_b, sgu_ws, sgu_bs, wpa, wpb, wout):
    T = x.shape[0]
    tm = min(TM_FF, T)
    n_chunk = tm // CHUNK

    def body(x_ref, uv_ref, gab_ref, ya_ref, g_ref, b_ref, ws_ref, bs_ref, wpa_ref, wpb_ref, wout_ref,
             yb_ref, pa_ref, pb_ref, mb_ref, x1_ref, w_ref, bst_ref):
        @pl.when(pl.program_id(0) == 0)
        def _():
            _fill_sgu_weights(ws_ref, bs_ref, w_ref, bst_ref)

        vhat, _ = _layer_norm_stats(_gelu(uv_ref[:, D:2 * D]))
        vnb = (vhat * g_ref[...] + b_ref[...]).astype(BF16)
        yb = (_gelu(uv_ref[:, 0:D]) * _sgu_mix(w_ref, vnb, bst_ref, n_chunk)).astype(BF16)
        yb_ref[...] = yb
        pa = _dot(ya_ref[...], wpa_ref[...])
        pb = _dot(yb, wpb_ref[...])
        pa_ref[...] = pa.astype(BF16)
        pb_ref[...] = pb.astype(BF16)
        mb = (_sig(gab_ref[:, 0:D]) * pa + _sig(gab_ref[:, D:2 * D]) * pb).astype(BF16)
        mb_ref[...] = mb
        x1_ref[...] = x_ref[...] + _dot(mb, wout_ref[...])

    tile = pl.BlockSpec((tm, D), lambda i: (i, 0))
    vec = pl.BlockSpec((1, D), lambda i: (0, 0))
    w = _resident((D, D))
    return pl.pallas_call(
        body, name="sgu_merge_fwd", grid=(T // tm,),
        in_specs=[tile, pl.BlockSpec((tm, 2 * D), lambda i: (i, 1)), pl.BlockSpec((tm, 2 * D), lambda i: (i, 2)), tile,
                  vec, vec, pl.BlockSpec(sgu_ws.shape, lambda i: (0, 0, 0, 0)),
                  pl.BlockSpec(sgu_bs.shape, lambda i: (0, 0, 0)), w, w, w],
        out_specs=[tile, tile, tile, tile, tile],
        out_shape=[S((T, D), BF16), S((T, D), BF16), S((T, D), BF16), S((T, D), BF16), S((T, D), F32)],
        scratch_shapes=[_SGU_W_BF16, _SGU_BT],
        compiler_params=_params(1),
    )(x, proj, proj, ya, ln_g, ln_b, sgu_ws, sgu_bs, wpa, wpb, wout)


def _ffn_fwd_loss(x1, g, w_gu, w_down, g_final, target):
    T = x1.shape[0]
    tm = min(TM_FF, T)

    def body(x_ref, g_ref, wgu_ref, wd_ref, gf_ref, t_ref,
             h2_ref, gu_ref, act_ref, dx2_ref, dx2b_ref, loss_ref, dg_ref):
        @pl.when(pl.program_id(0) == 0)
        def _():
            loss_ref[...] = jnp.zeros_like(loss_ref)
            dg_ref[...] = jnp.zeros_like(dg_ref)

        xv = x_ref[...]
        h2 = (xv * _rms_stats(xv) * g_ref[...]).astype(BF16)
        h2_ref[...] = h2
        x2 = xv
        for k in range(N_SHARD // 2):
            cols = slice(FF_SHARD * k, FF_SHARD * (k + 1))
            gate = _dot(h2, wgu_ref[k])
            up = _dot(h2, wgu_ref[k + N_SHARD // 2])
            gu_ref[:, cols] = gate.astype(BF16)
            gu_ref[:, D_FF + FF_SHARD * k:D_FF + FF_SHARD * (k + 1)] = up.astype(BF16)
            act = (gate * _sig(gate) * up).astype(BF16)
            act_ref[:, cols] = act
            x2 = x2 + _dot(act, wd_ref[cols, :])
        gf = gf_ref[...]
        err = x2 * _rms_stats(x2) * gf - t_ref[...]
        loss_ref[...] += 0.5 * jnp.sum(jnp.mean(err * err, axis=-1, keepdims=True), axis=0, keepdims=True)
        dx2, dg_rows = _rms_bwd(err * (1.0 / D), x2, gf)
        dg_ref[...] += _colsum(dg_rows)
        dx2_ref[...] = dx2
        dx2b_ref[...] = dx2.astype(BF16)

    tile = pl.BlockSpec((tm, D), lambda i: (i, 0))
    vec = pl.BlockSpec((1, D), lambda i: (0, 0))
    return pl.pallas_call(
        body, name="ffn_fwd_loss", grid=(T // tm,),
        in_specs=[tile, vec, _resident((N_SHARD, D, FF_SHARD)), _resident((D_FF, D)), vec, tile],
        out_specs=[tile, pl.BlockSpec((tm, 2 * D_FF), lambda i: (i, 0)), pl.BlockSpec((tm, D_FF), lambda i: (i, 0)),
                   tile, tile, pl.BlockSpec((1, 1), lambda i: (0, 0)), vec],
        out_shape=[S((T, D), BF16), S((T, 2 * D_FF), BF16), S((T, D_FF), BF16), S((T, D), F32), S((T, D), BF16),
                   S((1, 1), F32), S((1, D), F32)],
        compiler_params=_params(1),
    )(x1, g, w_gu, w_down, g_final, target)


def _ffn_bwd(dx2, dx2b, gu, w_down, w_gu, x1, g):
    T = x1.shape[0]
    tm = min(TM_FF, T)

    def body(dx2_ref, dx2b_ref, gu_ref, wd_ref, wgu_ref, x_ref, g_ref, dgu_ref, dx1_ref, dx1b_ref, dg_ref):
        @pl.when(pl.program_id(0) == 0)
        def _():
            dg_ref[...] = jnp.zeros_like(dg_ref)

        dxb = dx2b_ref[...]
        dh2 = jnp.zeros((tm, D), F32)
        for k in range(N_SHARD // 2):
            cols = slice(FF_SHARD * k, FF_SHARD * (k + 1))
            up_cols = slice(D_FF + FF_SHARD * k, D_FF + FF_SHARD * (k + 1))
            dact = _dot_nt(dxb, wd_ref[cols, :])
            gate = gu_ref[:, cols].astype(F32)
            sg = _sig(gate)
            dgate = (dact * gu_ref[:, up_cols].astype(F32) * (sg * (1.0 + gate * (1.0 - sg)))).astype(BF16)
            dup = (dact * (gate * sg)).astype(BF16)
            dgu_ref[:, cols] = dgate
            dgu_ref[:, up_cols] = dup
            dh2 = dh2 + _dot_nt(dgate, wgu_ref[k]) + _dot_nt(dup, wgu_ref[k + N_SHARD // 2])
        dx, dg_rows = _rms_bwd(dh2, x_ref[...], g_ref[...])
        dg_ref[...] += _colsum(dg_rows)
        dx1 = dx2_ref[...] + dx
        dx1_ref[...] = dx1
        dx1b_ref[...] = dx1.astype(BF16)

    tile = pl.BlockSpec((tm, D), lambda i: (i, 0))
    wide = pl.BlockSpec((tm, 2 * D_FF), lambda i: (i, 0))
    vec = pl.BlockSpec((1, D), lambda i: (0, 0))
    return pl.pallas_call(
        body, name="ffn_bwd", grid=(T // tm,),
        in_specs=[tile, tile, wide, _resident((D_FF, D)), _resident((N_SHARD, D, FF_SHARD)), tile, vec],
        out_specs=[wide, tile, tile, vec],
        out_shape=[S((T, 2 * D_FF), BF16), S((T, D), F32), S((T, D), BF16), S((1, D), F32)],
        compiler_params=_params(1),
    )(dx2, dx2b, gu, w_down, w_gu, x1, g)


def _matmul_tn(a, b, tn, shard_major, name, sides=()):
    T, M = a.shape
    N = b.shape[1]
    tk = min(TK_WGRAD, T)

    def body(a_ref, b_ref, o_ref):
        @pl.when(pl.program_id(1) == 0)
        def _():
            o_ref[...] = jnp.zeros_like(o_ref)

        acc = _dot_tn(a_ref[...], b_ref[...])
        if shard_major:
            o_ref[0] += acc
        else:
            o_ref[...] += acc

    if shard_major:
        out_spec, out_shape = pl.BlockSpec((1, M, tn), lambda j, k: (j, 0, 0)), S((N // tn, M, tn), F32)
    else:
        out_spec, out_shape = pl.BlockSpec((M, tn), lambda j, k: (0, j)), S((M, N), F32)
    (out,), side_outs = _call(
        body, name=name, grid=(N // tn, T // tk),
        in_specs=[pl.BlockSpec((tk, M), lambda j, k: (k, 0)), pl.BlockSpec((tk, tn), lambda j, k: (k, j))],
        out_specs=[out_spec], out_shape=[out_shape], args=(a, b), sides=sides)
    return out, side_outs


PIECE = IN_SHARD // 3
N_PIECE = IN_COLS // PIECE
DPROJ_ROTATION = 2 * D // PIECE


def _merge_sgu_bwd(dx1b, proj, pa, pb, ln_g, ln_b, sgu_ws, sgu_bs, wpa, wpb, wout, sides=()):
    T = dx1b.shape[0]
    tm = min(TM_FF, T)
    n_chunk = tm // CHUNK

    def body(dx_ref, uv_ref, gab_ref, pa_ref, pb_ref, g_ref, b_ref, ws_ref, bs_ref, wpa_ref, wpb_ref, wout_ref,
             dpa_ref, dpb_ref, dya_ref, dp_ref, dw_ref, dbs_ref, dg_ref, db_ref, w_ref, wt_ref, bst_ref):
        @pl.when(pl.program_id(0) == 0)
        def _():
            for ref in (dw_ref, dbs_ref, dg_ref, db_ref):
                ref[...] = jnp.zeros_like(ref)
            _fill_sgu_weights(ws_ref, bs_ref, w_ref, bst_ref, wt_ref)

        dm = _dot_nt(dx_ref[...], wout_ref[...])
        sa = _sig(gab_ref[:, 0:D])
        sb = _sig(gab_ref[:, D:2 * D])
        dpa = (dm * sa).astype(BF16)
        dpb = (dm * sb).astype(BF16)
        dpa_ref[...] = dpa
        dpb_ref[...] = dpb
        dp_ref[:, 2 * D:3 * D] = (dm * pa_ref[...].astype(F32) * (sa * (1.0 - sa))).astype(BF16)
        dp_ref[:, 3 * D:4 * D] = (dm * pb_ref[...].astype(F32) * (sb * (1.0 - sb))).astype(BF16)
        dya_ref[...] = _dot_nt(dpa, wpa_ref[...]).astype(BF16)
        dyb_v = _dot_nt(dpb, wpb_ref[...])

        gu, dgu = _gelu_and_grad(uv_ref[:, 0:D])
        gv, dgv = _gelu_and_grad(uv_ref[:, D:2 * D])
        vhat, rstd = _layer_norm_stats(gv)
        lng = g_ref[...]
        vnb = (vhat * lng + b_ref[...]).astype(BF16)
        mixed = _sgu_mix(w_ref, vnb, bst_ref, n_chunk)
        dp_ref[:, 0:D] = (dyb_v * mixed * dgu).astype(BF16)
        dmix = dyb_v * gu
        dmb = dmix.astype(BF16)
        keep = _causal_mask()
        dvn_cols, dbs_rows = [], []
        for g in range(N_GROUP):
            sl = slice(CHUNK * g, CHUNK * (g + 1))
            dmg = dmb[:, sl].reshape(n_chunk, CHUNK, CHUNK)
            vg = vnb[:, sl].reshape(n_chunk, CHUNK, CHUNK)
            wtb = jnp.broadcast_to(wt_ref[g][None], (n_chunk, CHUNK, CHUNK))
            dvn = lax.dot_general(wtb, dmg, (((2,), (1,)), ((0,), (0,))), preferred_element_type=F32)
            dvn_cols.append(dvn.reshape(tm, CHUNK))
            dw = lax.dot_general(dmg, vg, (((2,), (2,)), ((0,), (0,))), preferred_element_type=F32)
            dw_ref[g] += jnp.where(keep, jnp.sum(dw, axis=0), 0.0)
            per_token = jnp.sum(dmix[:, sl], axis=1)
            dbs_rows.append(jnp.sum(per_token.reshape(n_chunk, CHUNK), axis=0, keepdims=True))
        dbs_ref[...] += jnp.concatenate(dbs_rows, axis=0)
        dvn = jnp.concatenate(dvn_cols, axis=1)
        dg_ref[...] += _colsum(dvn * vhat)
        db_ref[...] += _colsum(dvn)
        dvhat = dvn * lng
        dgv_in = rstd * (dvhat - jnp.mean(dvhat, axis=-1, keepdims=True)
                         - vhat * jnp.mean(dvhat * vhat, axis=-1, keepdims=True))
        dp_ref[:, D:2 * D] = (dgv_in * dgv).astype(BF16)

    tile = pl.BlockSpec((tm, D), lambda i: (i, 0))
    vec = pl.BlockSpec((1, D), lambda i: (0, 0))
    w = _resident((D, D))
    wsp = pl.BlockSpec((N_GROUP, CHUNK, CHUNK), lambda i: (0, 0, 0))
    return _call(
        body, name="merge_sgu_bwd", grid=(T // tm,),
        in_specs=[tile, pl.BlockSpec((tm, 2 * D), lambda i: (i, 1)), pl.BlockSpec((tm, 2 * D), lambda i: (i, 2)),
                  tile, tile, vec, vec, pl.BlockSpec(sgu_ws.shape, lambda i: (0, 0, 0, 0)),
                  pl.BlockSpec(sgu_bs.shape, lambda i: (0, 0, 0)), w, w, w],
        out_specs=[tile, tile, tile, pl.BlockSpec((tm, 4 * D), lambda i: (i, 0)), wsp,
                   pl.BlockSpec((N_GROUP, CHUNK), lambda i: (0, 0)), vec, vec],
        out_shape=[S((T, D), BF16), S((T, D), BF16), S((T, D), BF16), S((T, IN_COLS), BF16),
                   S((N_GROUP, CHUNK, CHUNK), F32), S((N_GROUP, CHUNK), F32), S((1, D), F32), S((1, D), F32)],
        scratch_shapes=[_SGU_W_BF16, _SGU_W_BF16, _SGU_BT],
        args=(dx1b, proj, proj, pa, pb, ln_g, ln_b, sgu_ws, sgu_bs, wpa, wpb, wout), sides=sides)


def _rglru_bwd(dya, dproj, proj, hseq, xc, gates, conv_w, rg_wa, rg_wx, lam, sides=()):
    T = dya.shape[0]
    tm = min(TM_SCAN, T)
    n = T // tm
    per8 = tm // SUBLANES

    def body(dya_ref, _, rx_ref, rxp_ref, gate_ref, h_ref, hp_ref, xc_ref, gates_ref, cw_ref, wah_ref, wxh_ref,
             lam_ref, dab_ref, dcw_ref, dcb_ref, dwah_ref, dwxh_ref, dba_ref, dbx_ref, dlam_ref,
             hext, rext, dext, carry_a, carry_dh, a_s, b_s, h_s, wa_ref, wx_ref, dwa_ref, dwx_ref):
        i = pl.program_id(0)
        first_tile = i == n - 1

        @pl.when(i == 0)
        def _():
            for ref in (dcw_ref, dcb_ref, dwa_ref, dwx_ref, dba_ref, dbx_ref, dlam_ref, carry_a, carry_dh):
                ref[...] = jnp.zeros_like(ref)
            dext[tm:tm + SUBLANES, :] = jnp.zeros((SUBLANES, D), F32)
            _fill_blockdiag(wah_ref, wa_ref)
            _fill_blockdiag(wxh_ref, wx_ref)

        gel, dgel = _gelu_and_grad(gate_ref[...])
        dya_v = dya_ref[...].astype(F32)
        hseq_v = h_ref[...]
        dgate = dya_v * hseq_v * dgel
        xcv = xc_ref[...]
        lam_v = lam_ref[...]
        sp = _softplus_neg(lam_v)
        xb = xcv.astype(BF16)
        r, gi, a, m = (gates_ref[:, D * k:D * (k + 1)] for k in range(N_GATES))

        row = lax.broadcasted_iota(jnp.int32, (tm, D), 0)
        c = jnp.where(row == tm - 1, carry_a[0:1, :], _shift_up(a, 1, 0.0))
        dH = _scan_tile(c, dya_v * gel, carry_dh[0:1, :], a_s, b_s, h_s, reverse=True)
        carry_a[...] = jnp.broadcast_to(a[0:1, :], (SUBLANES, D))
        carry_dh[...] = jnp.broadcast_to(dH[0:1, :], (SUBLANES, D))

        hext[0:SUBLANES, :] = jnp.where(first_tile, 0.0, hp_ref[...])
        hext[SUBLANES:SUBLANES + tm, :] = hseq_v
        h_prev = hext[pl.ds(SUBLANES - 1, tm), :]

        d_m = dH * (gi * xcv)
        d_la = dH * h_prev * a - d_m * (a * a) / m
        d_ia = dH * m * xcv * (gi * (1.0 - gi))
        d_ra = d_la * ((-RG_C) * sp) * (r * (1.0 - r))
        dlam_ref[...] += _colsum(d_la * ((-RG_C) * r)) * (-_sig(-lam_v))
        dba_ref[...] += _colsum(d_ra)
        dbx_ref[...] += _colsum(d_ia)
        drab = d_ra.astype(BF16)
        diab = d_ia.astype(BF16)
        dxc_cols = []
        for j in range(N_RG_BLOCK):
            sl = slice(RG_BLOCK * j, RG_BLOCK * (j + 1))
            dxc_cols.append(_dot_nt(drab[:, sl], wa_ref[j]) + _dot_nt(diab[:, sl], wx_ref[j]))
            dwa_ref[j] += _dot_tn(xb[:, sl], drab[:, sl])
            dwx_ref[j] += _dot_tn(xb[:, sl], diab[:, sl])
        dxc = dH * m * gi + jnp.concatenate(dxc_cols, axis=1)

        dcb_ref[...] += _colsum(dxc)
        dext[0:tm, :] = dxc
        rext[0:SUBLANES, :] = jnp.where(first_tile, 0.0, rxp_ref[...])
        rext[SUBLANES:SUBLANES + tm, :] = rx_ref[...]
        drx = jnp.zeros((tm, D), F32)
        for k in range(CONV_WIDTH):
            drx = drx + dext[pl.ds(CONV_WIDTH - 1 - k, tm), :] * cw_ref[k:k + 1, :]
            dcw_ref[k:k + 1, :] += _colsum(dxc * rext[pl.ds(SUBLANES - (CONV_WIDTH - 1) + k, tm), :])
        dext[tm:tm + SUBLANES, :] = dext[0:SUBLANES, :]
        dab_ref[:, 0:D] = drx.astype(BF16)
        dab_ref[:, D:2 * D] = dgate.astype(BF16)

        @pl.when(first_tile)
        def _():
            for j in range(N_RG_BLOCK):
                for h in range(HEADS_PER_BLOCK):
                    sl = slice(HEAD_DIM * h, HEAD_DIM * (h + 1))
                    pair, side = divmod(HEADS_PER_BLOCK * j + h, 2)
                    lanes = slice(HEAD_DIM * side, HEAD_DIM * (side + 1))
                    dwah_ref[pair, :, lanes] = dwa_ref[j, sl, sl]
                    dwxh_ref[pair, :, lanes] = dwx_ref[j, sl, sl]

    def rev(col):
        return lambda i: (n - 1 - i, col)

    def prev8(col):
        return lambda i: (jnp.maximum((n - 1 - i) * per8 - 1, 0), col)

    tile = pl.BlockSpec((tm, D), rev(0))
    vec = pl.BlockSpec((1, D), lambda i: (0, 0))
    heads_in = pl.BlockSpec(rg_wa.shape, lambda i: (0, 0, 0, 0))
    head_pairs = (rg_wa.shape[1] // 2, HEAD_DIM, 2 * HEAD_DIM)
    heads_out = pl.BlockSpec(head_pairs, lambda i: (0, 0, 0))
    cw = pl.BlockSpec((CONV_WIDTH, D), lambda i: (0, 0))
    blocks_f32 = pltpu.VMEM((N_RG_BLOCK, RG_BLOCK, RG_BLOCK), F32)
    return _call(
        body, name="rglru_bwd", grid=(n,),
        in_specs=[tile, _ANY, pl.BlockSpec((tm, D), rev(0)), pl.BlockSpec((SUBLANES, D), prev8(0)),
                  pl.BlockSpec((tm, D), rev(1)), tile, pl.BlockSpec((SUBLANES, D), prev8(0)), tile,
                  pl.BlockSpec((tm, N_GATES * D), rev(0)), cw, heads_in, heads_in, vec],
        out_specs=[pl.BlockSpec((tm, 2 * D), rev(2)), cw, vec, heads_out, heads_out, vec, vec, vec],
        out_shape=[S((T, IN_COLS), BF16), S((CONV_WIDTH, D), F32), S((1, D), F32),
                   S(head_pairs, F32), S(head_pairs, F32), S((1, D), F32), S((1, D), F32), S((1, D), F32)],
        scratch_shapes=[pltpu.VMEM((tm + SUBLANES, D), F32), pltpu.VMEM((tm + SUBLANES, D), F32),
                        pltpu.VMEM((tm + SUBLANES, D), F32), pltpu.VMEM((SUBLANES, D), F32),
                        pltpu.VMEM((SUBLANES, D), F32)] + [pltpu.VMEM((D // LANES, tm, LANES), F32)] * 3
        + [_RG_BLOCKS_BF16] * 2 + [blocks_f32] * 2,
        args=(dya, dproj, proj, proj, proj, hseq, hseq, xc, gates, conv_w, rg_wa, rg_wx, lam), sides=sides,
        aliases={1: 0})


def _inproj_bwd(dproj, w_in, x, g, dx1):
    T = x.shape[0]
    tm = min(TM_MM, T)

    def body(dp_ref, w_ref, x_ref, g_ref, dx1_ref, dx_ref, dgm_ref):
        @pl.when(pl.program_id(0) == 0)
        def _():
            dgm_ref[...] = jnp.zeros_like(dgm_ref)

        dh = jnp.zeros((tm, D), F32)
        for p in range(N_PIECE):
            shard, part = divmod((p + DPROJ_ROTATION) % N_PIECE, IN_SHARD // PIECE)
            dh = dh + _dot_nt(dp_ref[:, PIECE * p:PIECE * (p + 1)], w_ref[shard, :, PIECE * part:PIECE * (part + 1)])
        dx, dg_rows = _rms_bwd(dh, x_ref[...], g_ref[...])
        dgm_ref[...] += _colsum(dg_rows)
        dx_ref[...] = dx1_ref[...] + dx

    tile = pl.BlockSpec((tm, D), lambda i: (i, 0))
    vec = pl.BlockSpec((1, D), lambda i: (0, 0))
    return pl.pallas_call(
        body, name="inproj_bwd", grid=(T // tm,),
        in_specs=[pl.BlockSpec((tm, IN_COLS), lambda i: (i, 0)), _resident((N_SHARD, D, IN_SHARD)), tile, vec, tile],
        out_specs=[tile, vec],
        out_shape=[S((T, D), F32), S((1, D), F32)],
        compiler_params=_params(1),
    )(dproj, w_in, x, g, dx1)


def _inproj_wgrad(h, dproj, sides=()):
    T = h.shape[0]
    tk = min(TK_WGRAD, T)
    per = IN_SHARD // PIECE

    def body(h_ref, *refs):
        pieces, o_ref = refs[:per], refs[per]

        @pl.when(pl.program_id(1) == 0)
        def _():
            o_ref[...] = jnp.zeros_like(o_ref)

        o_ref[0] += _dot_tn(h_ref[...], jnp.concatenate([p[...] for p in pieces], axis=1))

    def piece(i):
        return pl.BlockSpec((tk, PIECE), lambda j, k: (k, (per * j + i + N_PIECE - DPROJ_ROTATION) % N_PIECE))

    (out,), side_outs = _call(
        body, name="inproj_wgrad", grid=(N_SHARD, T // tk),
        in_specs=[pl.BlockSpec((tk, D), lambda j, k: (k, 0))] + [piece(i) for i in range(per)],
        out_specs=[pl.BlockSpec((1, D, IN_SHARD), lambda j, k: (j, 0, 0))],
        out_shape=[S((N_SHARD, D, IN_SHARD), F32)], args=(h,) + (dproj,) * per, sides=sides)
    return out, side_outs


def _row_tile(rows):
    for t in range(256, 0, -SUBLANES):
        if rows % t == 0:
            return t
    raise ValueError(rows)


def _add_halves(core, g, theirs, name):
    _, r, cols = g.shape
    half = r // 2
    tr = _row_tile(half)
    nb = half // tr

    def body(core_ref, g_ref, t_ref, o32_ref, o16_ref):
        s = g_ref[...] + t_ref[...]
        o32_ref[...] = s
        o16_ref[...] = s.astype(BF16)

    blk = pl.BlockSpec((1, tr, cols), lambda s, i, core_ref: (s, i, 0))
    gs = pltpu.PrefetchScalarGridSpec(
        num_scalar_prefetch=1, grid=(N_SHARD, nb),
        in_specs=[pl.BlockSpec((1, tr, cols), lambda s, i, core_ref: (s, core_ref[0] * nb + i, 0)), blk],
        out_specs=[blk, blk])
    return pl.pallas_call(
        body, name=name, grid_spec=gs,
        out_shape=[S((N_SHARD, half, cols), F32), S((N_SHARD, half, cols), BF16)],
        compiler_params=_params(2),
    )(core, g, theirs)


def _sum_shards(chip, own, others, name):
    _, half, cols = own.shape
    tr = _row_tile(half)

    def body(chip_ref, own_ref, oth_ref, o_ref):
        acc = own_ref[0]
        for j in range(3):
            acc = acc + oth_ref[j].astype(F32)
        o_ref[...] = acc

    gs = pltpu.PrefetchScalarGridSpec(
        num_scalar_prefetch=1, grid=(half // tr,),
        in_specs=[pl.BlockSpec((1, tr, cols), lambda i, chip_ref: (chip_ref[0], i, 0)),
                  pl.BlockSpec((3, tr, cols), lambda i, chip_ref: (0, i, 0))],
        out_specs=pl.BlockSpec((tr, cols), lambda i, chip_ref: (i, 0)))
    return pl.pallas_call(
        body, name=name, grid_spec=gs, out_shape=S((half, cols), F32), compiler_params=_params(1),
    )(chip, own, others)


def _adamw(w, g, m, v):
    m = ADAM_B1 * m + (1.0 - ADAM_B1) * g
    v = ADAM_B2 * v + (1.0 - ADAM_B2) * (g * g)
    m_hat = m / (1.0 - ADAM_B1 ** ADAM_STEP)
    v_hat = v / (1.0 - ADAM_B2 ** ADAM_STEP)
    delta = -ADAM_LR * (m_hat / (jnp.sqrt(v_hat) + ADAM_EPS) + ADAM_WD * w)
    return delta, m, v


def _adamw_shard(core, mine, theirs, w, m, v, name):
    r, cols = w.shape
    half = r // 2
    tr = _row_tile(half)
    nb = half // tr

    def body(core_ref, mine_ref, theirs_ref, w_ref, m_ref, v_ref, g_ref, d_ref, mo_ref, vo_ref):
        g = jnp.where(pl.program_id(0) == core_ref[0], mine_ref[...], theirs_ref[...])
        g_ref[...] = g
        d_ref[...], mo_ref[...], vo_ref[...] = _adamw(w_ref[...], g, m_ref[...], v_ref[...])

    hblk = pl.BlockSpec((tr, cols), lambda h, i, core_ref: (i, 0))
    blk = pl.BlockSpec((tr, cols), lambda h, i, core_ref: (h * nb + i, 0))
    gs = pltpu.PrefetchScalarGridSpec(num_scalar_prefetch=1, grid=(2, nb),
                                      in_specs=[hblk, hblk, blk, blk, blk], out_specs=[blk] * 4)
    return pl.pallas_call(
        body, name=name, grid_spec=gs, out_shape=[S((r, cols), F32)] * 4, compiler_params=_params(2),
    )(core, mine, theirs, w, m, v)


def _adamw_whole(w, g, m, v, name):
    def body(w_ref, g_ref, m_ref, v_ref, d_ref, mo_ref, vo_ref):
        d_ref[...], mo_ref[...], vo_ref[...] = _adamw(w_ref[...], g_ref[...], m_ref[...], v_ref[...])

    return pl.pallas_call(body, name=name, out_shape=[S(w.shape, F32)] * 3)(w, g, m, v)


_VEC_ROWS = ("norm_mix_g", "conv_b", "rg_lambda", "sgu_ln_g", "sgu_ln_b", "norm_ffn_g", "norm_final_g", "rg_ba",
             "rg_bx")
_CONV_ROW = len(_VEC_ROWS)
_LOSS_ROW = _CONV_ROW + CONV_WIDTH
_VEC_PAD = -(_LOSS_ROW + 1) % SUBLANES
_HEAD_BIASES = ("rg_ba", "rg_bx")
_TENSORS = ("sgu_bs", "sgu_ws", "rg_wa", "rg_wx")
_HEAD_PAIRS = ("rg_wa", "rg_wx")


def _small_sum_adamw(parts, w, m, v):
    names = [n for n in _VEC_ROWS] + list(_TENSORS)
    n_parts = len(parts)

    def total(ref):
        acc = ref[0]
        for k in range(1, N_DEVICE):
            acc = acc + ref[k]
        return acc

    def body(*refs):
        part_refs, refs = refs[:n_parts], refs[n_parts:]
        w_refs, m_refs, v_refs = (dict(zip(names, refs[k * len(names):(k + 1) * len(names)])) for k in range(3))
        outs = refs[3 * len(names):]
        out_refs = {n: outs[4 * k:4 * k + 4] for k, n in enumerate(names)}
        conv_ref, loss_ref = outs[4 * len(names):]
        vec = total(part_refs[0])
        grads = {n: total(p) for n, p in zip(_TENSORS, part_refs[1:])}
        for n in _HEAD_PAIRS:
            pairs = grads[n]
            grads[n] = jnp.stack([pairs[k // 2, :, HEAD_DIM * (k % 2):HEAD_DIM * (k % 2 + 1)]
                                  for k in range(2 * pairs.shape[0])], axis=0)
        grads = {n: g[None] for n, g in grads.items()}
        for row, n in enumerate(_VEC_ROWS):
            g = vec[row:row + 1, :]
            if n in _HEAD_BIASES:
                g = jnp.concatenate([g[:, HEAD_DIM * h:HEAD_DIM * (h + 1)] for h in range(D // HEAD_DIM)], axis=0)[None]
            grads[n] = g
        for n in names:
            g_ref, d_ref, mo_ref, vo_ref = out_refs[n]
            g_ref[...] = grads[n]
            d_ref[...], mo_ref[...], vo_ref[...] = _adamw(w_refs[n][...], grads[n], m_refs[n][...], v_refs[n][...])
        conv_ref[...] = vec[_CONV_ROW:_CONV_ROW + CONV_WIDTH, :]
        loss_ref[...] = vec[_LOSS_ROW:_LOSS_ROW + 1, 0:1]

    res = pl.pallas_call(
        body, name="small_sum_adamw",
        out_shape=[S(w[n].shape, F32) for n in names for _ in range(4)] + [S((CONV_WIDTH, D), F32), S((1, 1), F32)],
        compiler_params=pltpu.CompilerParams(vmem_limit_bytes=VMEM_LIMIT),
    )(*parts, *[w[n] for n in names], *[m[n] for n in names], *[v[n] for n in names])
    return {n: tuple(res[4 * k:4 * k + 4]) for k, n in enumerate(names)}, res[-2], res[-1]


_BIG = ("w_in", "w_proj_a", "w_proj_b", "w_out", "w_gate_up", "w_down")
_WEIGHTS = ("norm_mix_g", "w_in", "conv_w", "conv_b", "rg_wa", "rg_ba", "rg_wx", "rg_bx", "rg_lambda", "sgu_ln_g",
            "sgu_ln_b", "sgu_ws", "sgu_bs", "w_proj_a", "w_proj_b", "w_out", "norm_ffn_g", "w_gate_up", "w_down",
            "norm_final_g")


def kernel(x, norm_mix_g, w_in, conv_w, conv_b, rg_wa, rg_ba, rg_wx, rg_bx, rg_lambda, sgu_ln_g, sgu_ln_b, sgu_ws, sgu_bs, w_proj_a, w_proj_b, w_out, norm_ffn_g, w_gate_up, w_down, norm_final_g, loss_target, m_norm_mix_g, m_w_in, m_conv_w, m_conv_b, m_rg_wa, m_rg_ba, m_rg_wx, m_rg_bx, m_rg_lambda, m_sgu_ln_g, m_sgu_ln_b, m_sgu_ws, m_sgu_bs, m_w_proj_a, m_w_proj_b, m_w_out, m_norm_ffn_g, m_w_gate_up, m_w_down, m_norm_final_g, v_norm_mix_g, v_w_in, v_conv_w, v_conv_b, v_rg_wa, v_rg_ba, v_rg_wx, v_rg_bx, v_rg_lambda, v_sgu_ln_g, v_sgu_ln_b, v_sgu_ws, v_sgu_bs, v_w_proj_a, v_w_proj_b, v_w_out, v_norm_ffn_g, v_w_gate_up, v_w_down, v_norm_final_g):
    args = dict(locals())
    w = {n: args[n] for n in _WEIGHTS}
    mom = {n: args["m_" + n] for n in _WEIGHTS}
    var = {n: args["v_" + n] for n in _WEIGHTS}
    xi, yi, ci = _position()
    core = ci.astype(jnp.int32).reshape(1)
    chip = (2 * xi + yi).astype(jnp.int32).reshape(1)

    bf = {n: w[n][0].astype(BF16) for n in _BIG}
    final_g = w["norm_final_g"].reshape(1, D)
    ba, bx = w["rg_ba"].reshape(1, D), w["rg_bx"].reshape(1, D)
    lam, ln_g, ln_b = w["rg_lambda"], w["sgu_ln_g"], w["sgu_ln_b"]
    x0, target = x[0], loss_target[0]

    def shard_major(g):
        return g.reshape(N_SHARD, g.shape[0] // N_SHARD, g.shape[1])

    def chip_sums(names, grads, theirs):
        return [_add_halves(core, g, t, "add_halves_" + n) for n, g, t in zip(names, grads, theirs)]

    def my_halves(names, sums, arrived):
        return [_sum_shards(chip, p32, a, "sum_shards_" + n) for n, (p32, _), a in zip(names, sums, arrived)]

    (w_in_a,), (conv_a,) = _comm_only([_gather_half_side([bf["w_in"]]), _gather_side([w["conv_w"][0]])], "gather_w_in")
    conv_cols = conv_a.shape[-1]
    conv_full = jnp.swapaxes(conv_a, 0, 1).reshape(CONV_WIDTH, D)
    (proj, h), ((w_pa_a, w_pb_a, w_out_a),) = _inproj(
        x0, w["norm_mix_g"], w_in_a, sides=[_gather_half_side([bf["w_proj_a"], bf["w_proj_b"], bf["w_out"]])])
    (ya, xc, hseq, gates), ((w_gu_a, w_down_a),) = _rglru_fwd(
        proj, conv_full, w["conv_b"], w["rg_wa"], w["rg_wx"], ba, bx, lam,
        sides=[_gather_half_side([bf["w_gate_up"], bf["w_down"]])])
    wpa, wpb, wout, wdown = w_pa_a.reshape(D, D), w_pb_a.reshape(D, D), w_out_a.reshape(D, D), w_down_a.reshape(D_FF, D)
    yb, pa, pb, mb, x1 = _sgu_merge_fwd(x0, proj, ya, ln_g, ln_b, w["sgu_ws"], w["sgu_bs"], wpa, wpb, wout)
    h2, gu, act, dx2, dx2b, loss, d_final_g = _ffn_fwd_loss(x1, w["norm_ffn_g"], w_gu_a, wdown, final_g, target)

    dgu, dx1, dx1b, d_ffn_g = _ffn_bwd(dx2, dx2b, gu, wdown, w_gu_a, x1, w["norm_ffn_g"])
    ffn = ("w_gate_up", "w_down")
    g_ffn = [_matmul_tn(h2, dgu, FF_SHARD, True, "wgrad_gate_up")[0],
             shard_major(_matmul_tn(act, dx2b, D // 2, False, "wgrad_down")[0])]
    (dpa, dpb, dya, dproj, d_ws, d_bs, d_lng, d_lnb), (theirs_ffn,) = _merge_sgu_bwd(
        dx1b, proj, pa, pb, ln_g, ln_b, w["sgu_ws"], w["sgu_bs"], wpa, wpb, wout, sides=[_halves_side(g_ffn)])
    sums_ffn = chip_sums(ffn, g_ffn, theirs_ffn)
    mix = ("w_proj_a", "w_proj_b", "w_out")
    g_mix = [shard_major(_matmul_tn(ya, dpa, D, False, "wgrad_proj_a")[0]),
             shard_major(_matmul_tn(yb, dpb, D, False, "wgrad_proj_b")[0]),
             shard_major(_matmul_tn(mb, dx1b, D, False, "wgrad_out")[0])]
    (dproj, d_cw, d_cb, d_wa, d_wx, d_ba, d_bx, d_lam), (arrived_ffn, theirs_mix, sgu_parts) = _rglru_bwd(
        dya, dproj, proj, hseq, xc, gates, conv_full, w["rg_wa"], w["rg_wx"], lam,
        sides=[_scatter_side([p16 for _, p16 in sums_ffn]), _halves_side(g_mix), _everyone_side([d_bs, d_ws])])
    mine_ffn = my_halves(ffn, sums_ffn, arrived_ffn)
    sums_mix = chip_sums(mix, g_mix, theirs_mix)
    grad_x, d_mix_g = _inproj_bwd(dproj, w_in_a, x0, w["norm_mix_g"], dx1)
    rows = {"norm_mix_g": d_mix_g, "conv_b": d_cb, "rg_lambda": d_lam, "sgu_ln_g": d_lng, "sgu_ln_b": d_lnb,
            "norm_ffn_g": d_ffn_g, "norm_final_g": d_final_g, "rg_ba": d_ba, "rg_bx": d_bx}
    vec = jnp.concatenate([rows[n] for n in _VEC_ROWS]
                          + [d_cw, jnp.pad(loss, ((0, _VEC_PAD), (0, D - 1)))], axis=0)
    g_in, ((vec_parts, wa_parts, wx_parts), other_ffn, arrived_mix) = _inproj_wgrad(
        h, dproj, sides=[_everyone_side([vec, d_wa, d_wx]), _swap_side(mine_ffn),
                         _scatter_side([p16 for _, p16 in sums_mix])])
    small_parts = [vec_parts] + sgu_parts + [wa_parts, wx_parts]
    mine_mix = my_halves(mix, sums_mix, arrived_mix)
    theirs_in, other_mix = _comm_only([_halves_side([g_in]), _swap_side(mine_mix)], "halves_w_in")
    sums_in = chip_sums(("w_in",), [g_in], theirs_in)
    arrived_in = _comm_only([_scatter_side([p16 for _, p16 in sums_in])], "scatter_w_in")[0]
    mine_in = my_halves(("w_in",), sums_in, arrived_in)
    other_in = _comm_only([_swap_side(mine_in)], "swap_w_in")[0]

    out = {}
    for n, gm, go in zip(ffn + mix + ("w_in",), mine_ffn + mine_mix + mine_in, other_ffn + other_mix + other_in):
        g, d, mo, vo = _adamw_shard(core, gm, go, w[n][0], mom[n][0], var[n][0], "adamw_" + n)
        out[n] = tuple(a[None] for a in (g, d, mo, vo))
    as_row = lambda t: {n: a.reshape(1, D) if n == "norm_final_g" else a for n, a in t.items()}
    small_out, conv_sum, loss_sum = _small_sum_adamw(small_parts, as_row(w), as_row(mom), as_row(var))
    out.update(small_out)
    out["norm_final_g"] = tuple(a.reshape(D) for a in small_out["norm_final_g"])
    conv_g = lax.dynamic_slice_in_dim(conv_sum, chip[0] * conv_cols, conv_cols, axis=1)
    d, mo, vo = _adamw_whole(w["conv_w"][0], conv_g, mom["conv_w"][0], var["conv_w"][0], "adamw_conv_w")
    out["conv_w"] = tuple(a[None] for a in (conv_g, d, mo, vo))

    return (loss_sum[0, 0], grad_x[None], *[out[n][0] for n in _WEIGHTS], *[out[n][1] for n in _WEIGHTS],
            *[out[n][2] for n in _WEIGHTS], *[out[n][3] for n in _WEIGHTS])
```

```python
import functools

import jax
import jax.numpy as jnp
from jax import lax
from jax.experimental import pallas as pl
from jax.experimental.pallas import tpu as pltpu

F32 = jnp.float32
BF16 = jnp.bfloat16
S = jax.ShapeDtypeStruct

D = 1024
N_SHARD = 4
IN_COLS = 6 * D
IN_SHARD = IN_COLS // N_SHARD
D_FF = 2816
FF_SHARD = 2 * D_FF // N_SHARD
RG_BLOCK = 256
N_RG_BLOCK = D // RG_BLOCK
CHUNK = 128
N_GROUP = 8
CONV_WIDTH = 4
RG_C = 8.0
EPS = 1e-6
ADAM_LR, ADAM_B1, ADAM_B2, ADAM_EPS, ADAM_WD, ADAM_STEP = 0.001, 0.9, 0.999, 1e-08, 0.01, 10

V7X_VMEM_BYTES = 64 * 1024 * 1024
VMEM_LIMIT = V7X_VMEM_BYTES * 3 // 4
SUBLANES = 8
MESH = pl.DeviceIdType.MESH

TM_MM = 512
TM_SCAN = 256
TM_FF = 256
TK_WGRAD = 1024


def _params(n_axes):
    return pltpu.CompilerParams(dimension_semantics=("arbitrary",) * n_axes, vmem_limit_bytes=VMEM_LIMIT)


def _resident(shape):
    nd = len(shape)
    return pl.BlockSpec(shape, lambda *_: (0,) * nd, pipeline_mode=pl.Buffered(1))


def _sig(x):
    return 1.0 / (1.0 + jnp.exp(-x))


_GELU_K2 = 2.0 * 0.7978845608028654
_GELU_C = 0.044715


def _gelu(x):
    return x * _sig(x * (_GELU_K2 + (_GELU_K2 * _GELU_C) * (x * x)))


def _gelu_and_grad(x):
    x2 = x * x
    s = _sig(x * (_GELU_K2 + (_GELU_K2 * _GELU_C) * x2))
    g = x * s
    return g, s + g * (1.0 - s) * (_GELU_K2 + (3.0 * _GELU_K2 * _GELU_C) * x2)


_EXPM1_SERIES = tuple(1.0 / f for f in (5040.0, 720.0, 120.0, 24.0, 6.0, 2.0, 1.0))


def _one_minus_exp(x):
    p = _EXPM1_SERIES[0]
    for coef in _EXPM1_SERIES[1:]:
        p = p * x + coef
    return jnp.where(x > -0.125, -x * p, 1.0 - jnp.exp(x))


def _softplus_neg(lam):
    z = -lam
    e = jnp.exp(-jnp.abs(z))
    u = 1.0 + e
    log1p = jnp.where(u == 1.0, e, jnp.log(u) * e / (u - 1.0))
    return jnp.maximum(z, 0.0) + log1p


def _rms_stats(x):
    return lax.rsqrt(jnp.mean(x * x, axis=-1, keepdims=True) + EPS)


def _rms_bwd(dy, x, g):
    rstd = _rms_stats(x)
    xhat = x * rstd
    dxhat = dy * g
    dx = rstd * (dxhat - xhat * jnp.mean(dxhat * xhat, axis=-1, keepdims=True))
    return dx, dy * xhat


def _colsum(x):
    return jnp.sum(x, axis=0, keepdims=True)


def _shift_down(x, d, fill):
    n = x.shape[0]
    if d % SUBLANES == 0:
        return jnp.concatenate([jnp.full((d, x.shape[1]), fill, x.dtype), x[:n - d]], axis=0)
    row = lax.broadcasted_iota(jnp.int32, x.shape, 0)
    return jnp.where(row < d, fill, pltpu.roll(x, d, 0))


def _shift_up(x, d, fill):
    n = x.shape[0]
    if d % SUBLANES == 0:
        return jnp.concatenate([x[d:], jnp.full((d, x.shape[1]), fill, x.dtype)], axis=0)
    row = lax.broadcasted_iota(jnp.int32, x.shape, 0)
    return jnp.where(row >= n - d, fill, pltpu.roll(x, n - d, 0))


def _scan(a, b, shift):
    d = 1
    while d < a.shape[0]:
        b = a * shift(b, d, 0.0) + b
        a = a * shift(a, d, 1.0)
        d *= 2
    return a, b


LANES = 128


def _scan_tile(a, b, outside, a_s, b_s, h_s, reverse):
    tm = a.shape[0]
    groups = tm // SUBLANES
    order = list(range(SUBLANES - 1, -1, -1) if reverse else range(SUBLANES))
    shift = _shift_up if reverse else _shift_down
    edge = groups - 1 if reverse else 0
    for j in range(D // LANES):
        a_s[j] = a[:, LANES * j:LANES * (j + 1)]
        b_s[j] = b[:, LANES * j:LANES * (j + 1)]
    for j in range(D // LANES):
        def slab(ref, k):
            return ref[j, pl.ds(k, groups, stride=SUBLANES), :]

        ga, gb = slab(a_s, order[0]), slab(b_s, order[0])
        for k in order[1:]:
            ak = slab(a_s, k)
            gb = ak * gb + slab(b_s, k)
            ga = ak * ga
        ga, gb = _scan(ga, gb, shift)
        h_out = outside[:, LANES * j:LANES * (j + 1)]
        group_end = ga * h_out + gb
        row = lax.broadcasted_iota(jnp.int32, (groups, LANES), 0)
        h = jnp.where(row == edge, h_out, shift(group_end, 1, 0.0))
        for k in order:
            h = slab(a_s, k) * h + slab(b_s, k)
            h_s[j, pl.ds(k, groups, stride=SUBLANES), :] = h
    return jnp.concatenate([h_s[j] for j in range(D // LANES)], axis=1)


def _dot(a, b):
    return jnp.dot(a, b, preferred_element_type=F32)


def _dot_nt(a, b):
    return lax.dot_general(a, b, (((1,), (1,)), ((), ())), preferred_element_type=F32)


def _dot_tn(a, b):
    return lax.dot_general(a, b, (((0,), (0,)), ((), ())), preferred_element_type=F32)


_ANY = pl.BlockSpec(memory_space=pl.ANY)


def _position():
    return lax.axis_index("x"), lax.axis_index("y"), lax.axis_index("c")


def _other_chips(x, y):
    return [(1 - x, y), (x, 1 - y), (1 - x, 1 - y)]


class _Side:
    def __init__(self, inputs, out_shapes, n_sems, make):
        self.inputs, self.out_shapes, self.n_sems, self.make = list(inputs), list(out_shapes), n_sems, make


MID_STEP = 0.625


def _call(body, *, name, grid, in_specs, out_specs, out_shape, args, scratch_shapes=(), sides=(), aliases=None,
          scalars=None, mid=MID_STEP):
    n_in, n_out, n_scr = len(in_specs), len(out_specs), len(scratch_shapes)
    n_scalar = 0 if scalars is None else 1
    side_in = [len(s.inputs) for s in sides]
    side_out = [len(s.out_shapes) for s in sides]

    def wrapped(*refs):
        refs = list(refs)
        take = lambda k: [refs.pop(0) for _ in range(k)]
        ins = take(n_scalar) + take(n_in)
        sins = [take(k) for k in side_in]
        outs = take(n_out)
        souts = [take(k) for k in side_out]
        scr = take(n_scr)
        sems = [take(3) for _ in sides]
        def run(phase):
            for s, si, so, sem in zip(sides, sins, souts, sems):
                for thunk in s.make(si, so, *sem)[phase]:
                    thunk()

        if sides:
            n_steps = functools.reduce(lambda a, b: a * b, grid)
            step = functools.reduce(lambda a, b: a + b, [
                pl.program_id(a) * functools.reduce(lambda p, q: p * q, grid[a + 1:], 1) for a in range(len(grid))])
            pl.when(step == 0)(lambda: run(0))
        body(*ins, *outs, *scr)
        if sides:
            pl.when(step == int(mid * (n_steps - 1)))(lambda: run(1))
            pl.when(step == n_steps - 1)(lambda: run(2))

    grid_spec = pltpu.PrefetchScalarGridSpec(
        num_scalar_prefetch=n_scalar, grid=grid,
        in_specs=list(in_specs) + [_ANY] * sum(side_in),
        out_specs=list(out_specs) + [_ANY] * sum(side_out),
        scratch_shapes=list(scratch_shapes) + [pltpu.SemaphoreType.DMA((s.n_sems,)) for s in sides for _ in range(3)])
    res = pl.pallas_call(
        wrapped, name=name, grid_spec=grid_spec,
        out_shape=list(out_shape) + [o for s in sides for o in s.out_shapes],
        input_output_aliases={k + n_scalar: v for k, v in (aliases or {}).items()},
        compiler_params=_params(len(grid)),
    )(*([scalars] if n_scalar else []), *args, *[a for s in sides for a in s.inputs])
    main, rest, per_side = list(res[:n_out]), list(res[n_out:]), []
    for k in side_out:
        per_side.append(rest[:k])
        rest = rest[k:]
    return main, per_side


def _comm_only(sides, name):
    def body():
        pass

    return _call(body, name=name, grid=(1,), in_specs=[], out_specs=[], out_shape=[], args=[], sides=sides)[1]


def _remote(src, dst, send, recv, k, device):
    return pltpu.make_async_remote_copy(src_ref=src, dst_ref=dst, send_sem=send.at[k], recv_sem=recv.at[k],
                                        device_id=device, device_id_type=MESH)


def _both_ways(copy, keys):
    return [lambda k=k: copy(k).start() for k in keys], [], [lambda k=k: copy(k).wait() for k in keys]


def _gather_side(shards):
    n = len(shards)

    def make(ins, outs, send, recv, local):
        x, y, c = _position()
        mine = 2 * x + y
        chips = _other_chips(x, y)
        pairs = [(w, j) for w in range(n) for j in range(3)]

        def own(w):
            return pltpu.make_async_copy(ins[w], outs[w].at[mine], local.at[w])

        def push(w, j):
            return _remote(ins[w], outs[w].at[mine], send, recv, 3 * w + j, (*chips[j], c))

        def arrival(w, j):
            px, py = chips[j]
            return _remote(ins[w], outs[w].at[2 * px + py], send, recv, 3 * w + j, (px, py, c))

        starts = [lambda w=w: own(w).start() for w in range(n)] + [lambda w=w, j=j: push(w, j).start() for w, j in pairs]
        waits = ([lambda w=w, j=j: arrival(w, j).wait_recv() for w, j in pairs]
                 + [lambda w=w, j=j: push(w, j).wait_send() for w, j in pairs]
                 + [lambda w=w: own(w).wait() for w in range(n)])
        return starts, [], waits

    return _Side(shards, [S((N_SHARD,) + s.shape, s.dtype) for s in shards], 3 * n, make)


def _gather_half_side(shards):
    n = len(shards)

    def make(ins, outs, send, recv, local):
        x, y, c = _position()
        mine = 2 * x + y
        chips = _other_chips(x, y)
        pairs = [(w, j) for w in range(n) for j in range(3)]

        def rows(w, core):
            half = ins[w].shape[0] // 2
            return pl.ds(core * half, half)

        def own(w):
            return pltpu.make_async_copy(ins[w], outs[w].at[mine], local.at[w])

        def push(w, j):
            return _remote(ins[w].at[rows(w, c), :], outs[w].at[mine, rows(w, c), :], send, recv, 3 * w + j,
                           (*chips[j], c))

        def landed(w, j, core):
            px, py = chips[j]
            return outs[w].at[2 * px + py, rows(w, core), :]

        def arrival(w, j):
            return _remote(ins[w].at[rows(w, c), :], landed(w, j, c), send, recv, 3 * w + j, (*chips[j], c))

        def passed(w, j, core):
            return _remote(landed(w, j, core), landed(w, j, core), send, recv, 3 * n + 3 * w + j, (x, y, 1 - c))

        starts = [lambda w=w: own(w).start() for w in range(n)] + [lambda w=w, j=j: push(w, j).start() for w, j in pairs]
        mids = [t for w, j in pairs for t in (lambda w=w, j=j: arrival(w, j).wait_recv(),
                                              lambda w=w, j=j: passed(w, j, c).start())]
        waits = ([lambda w=w, j=j: passed(w, j, 1 - c).wait_recv() for w, j in pairs]
                 + [lambda w=w, j=j: passed(w, j, c).wait_send() for w, j in pairs]
                 + [lambda w=w, j=j: push(w, j).wait_send() for w, j in pairs]
                 + [lambda w=w: own(w).wait() for w in range(n)])
        return starts, mids, waits

    return _Side(shards, [S((N_SHARD,) + s.shape, s.dtype) for s in shards], 6 * n, make)


def _halves_side(grads):
    n = len(grads)

    def make(ins, outs, send, recv, local):
        x, y, c = _position()

        def copy(w):
            half = ins[w].shape[1] // 2
            return _remote(ins[w].at[:, pl.ds((1 - c) * half, half), :], outs[w], send, recv, w, (x, y, 1 - c))

        return _both_ways(copy, range(n))

    return _Side(grads, [S((N_SHARD, g.shape[1] // 2, g.shape[2]), F32) for g in grads], n, make)


def _scatter_side(partials):
    n = len(partials)

    def make(ins, outs, send, recv, local):
        x, y, c = _position()
        chips = _other_chips(x, y)

        def copy(k):
            w, j = divmod(k, 3)
            px, py = chips[j]
            return _remote(ins[w].at[2 * px + py], outs[w].at[j], send, recv, k, (px, py, c))

        return _both_ways(copy, range(3 * n))

    return _Side(partials, [S((3,) + p.shape[1:], p.dtype) for p in partials], 3 * n, make)


def _swap_side(halves):
    n = len(halves)

    def make(ins, outs, send, recv, local):
        x, y, c = _position()
        return _both_ways(lambda w: _remote(ins[w], outs[w], send, recv, w, (x, y, 1 - c)), range(n))

    return _Side(halves, [S(h.shape, h.dtype) for h in halves], n, make)


N_DEVICE = 8


def _everyone_side(arrays):
    n = len(arrays)
    peers = N_DEVICE - 1

    def make(ins, outs, send, recv, local):
        x, y, c = _position()
        mine = 4 * x + 2 * y + c
        pairs = [(w, k) for w in range(n) for k in range(1, N_DEVICE)]

        def peer(k):
            return (1 - x if k & 4 else x, 1 - y if k & 2 else y, 1 - c if k & 1 else c)

        def own(w):
            return pltpu.make_async_copy(ins[w], outs[w].at[mine], local.at[w])

        def push(w, k):
            return _remote(ins[w], outs[w].at[mine], send, recv, peers * w + k - 1, peer(k))

        def arrival(w, k):
            px, py, pc = peer(k)
            return _remote(ins[w], outs[w].at[4 * px + 2 * py + pc], send, recv, peers * w + k - 1, (px, py, pc))

        starts = [lambda w=w: own(w).start() for w in range(n)] + [lambda w=w, k=k: push(w, k).start() for w, k in pairs]
        waits = ([lambda w=w, k=k: arrival(w, k).wait_recv() for w, k in pairs]
                 + [lambda w=w, k=k: push(w, k).wait_send() for w, k in pairs]
                 + [lambda w=w: own(w).wait() for w in range(n)])
        return starts, [], waits

    return _Side(arrays, [S((N_DEVICE,) + a.shape, a.dtype) for a in arrays], peers * n, make)


def _inproj_own(chip, x, g, w_shard, sides=()):
    T = x.shape[0]
    tm = min(TM_MM, T)

    def body(chip_ref, x_ref, g_ref, w_ref, proj_ref, h_ref):
        xv = x_ref[...]
        h = (xv * _rms_stats(xv) * g_ref[...]).astype(BF16)
        h_ref[...] = h
        proj_ref[...] = _dot(h, w_ref[...])

    return _call(
        body, name="inproj_own", grid=(T // tm,),
        in_specs=[pl.BlockSpec((tm, D), lambda i, c: (i, 0)), pl.BlockSpec((1, D), lambda i, c: (0, 0)),
                  pl.BlockSpec((D, IN_SHARD), lambda i, c: (0, 0), pipeline_mode=pl.Buffered(1))],
        out_specs=[pl.BlockSpec((tm, IN_SHARD), lambda i, c: (i, c[0])), pl.BlockSpec((tm, D), lambda i, c: (i, 0))],
        out_shape=[S((T, IN_COLS), F32), S((T, D), BF16)],
        args=(x, g, w_shard), sides=sides, scalars=chip, mid=1.0)


def _inproj_rest(chip, proj, h, w_in, sides=()):
    T = h.shape[0]
    tm = min(TM_MM, T)

    def body(chip_ref, _, h_ref, w_ref, proj_ref):
        proj_ref[...] = _dot(h_ref[...], w_ref[0])

    def other(p, c):
        return jnp.bitwise_xor(c[0], p + 1)

    return _call(
        body, name="inproj_rest", grid=(N_SHARD - 1, T // tm),
        in_specs=[_ANY, pl.BlockSpec((tm, D), lambda p, i, c: (i, 0)),
                  pl.BlockSpec((1, D, IN_SHARD), lambda p, i, c: (other(p, c), 0, 0))],
        out_specs=[pl.BlockSpec((tm, IN_SHARD), lambda p, i, c: (i, other(p, c)))],
        out_shape=[S((T, IN_COLS), F32)],
        args=(proj, h, w_in), sides=sides, scalars=chip, aliases={0: 0})


def _rg_gates(xc, wa_ref, wx_ref, ba, bx, sp):
    xb = xc.astype(BF16)
    blocks = [xb[:, RG_BLOCK * j:RG_BLOCK * (j + 1)] for j in range(N_RG_BLOCK)]
    r = _sig(jnp.concatenate([_dot(blocks[j], wa_ref[j]) for j in range(N_RG_BLOCK)], axis=1) + ba)
    gi = _sig(jnp.concatenate([_dot(blocks[j], wx_ref[j]) for j in range(N_RG_BLOCK)], axis=1) + bx)
    log_a = (-RG_C) * r * sp
    a = jnp.exp(log_a)
    m = jnp.sqrt(_one_minus_exp(2.0 * log_a))
    return xb, r, gi, a, m


N_GATES = 4
HEADS_PER_BLOCK = 4
HEAD_DIM = RG_BLOCK // HEADS_PER_BLOCK
_RG_BLOCKS_BF16 = pltpu.VMEM((N_RG_BLOCK, RG_BLOCK, RG_BLOCK), BF16)


def _fill_blockdiag(heads_ref, blocks):
    blocks[...] = jnp.zeros_like(blocks)
    for j in range(N_RG_BLOCK):
        for h in range(HEADS_PER_BLOCK):
            sl = slice(HEAD_DIM * h, HEAD_DIM * (h + 1))
            blocks[j, sl, sl] = heads_ref[0, HEADS_PER_BLOCK * j + h].astype(BF16)


def _rglru_fwd(proj, conv_w, conv_b, rg_wa, rg_wx, ba, bx, lam, sides=()):
    T = proj.shape[0]
    tm = min(TM_SCAN, T)

    def body(rx_ref, gate_ref, cw_ref, cb_ref, wah_ref, wxh_ref, ba_ref, bx_ref, lam_ref,
             ya_ref, xc_ref, h_ref, gates_ref, ext, hc, a_s, b_s, h_s, wa_ref, wx_ref):
        @pl.when(pl.program_id(0) == 0)
        def _():
            ext[0:SUBLANES, :] = jnp.zeros((SUBLANES, D), F32)
            hc[...] = jnp.zeros((SUBLANES, D), F32)
            _fill_blockdiag(wah_ref, wa_ref)
            _fill_blockdiag(wxh_ref, wx_ref)

        ext[SUBLANES:SUBLANES + tm, :] = rx_ref[...]
        xc = cb_ref[...]
        for k in range(CONV_WIDTH):
            xc = xc + ext[pl.ds(SUBLANES - (CONV_WIDTH - 1) + k, tm), :] * cw_ref[k:k + 1, :]
        ext[0:SUBLANES, :] = ext[tm:tm + SUBLANES, :]
        xc_ref[...] = xc
        _, r, gi, a, m = _rg_gates(xc, wa_ref, wx_ref, ba_ref[...], bx_ref[...], _softplus_neg(lam_ref[...]))
        for k, val in enumerate((r, gi, a, m)):
            gates_ref[:, D * k:D * (k + 1)] = val
        h = _scan_tile(a, m * (gi * xc), hc[0:1, :], a_s, b_s, h_s, reverse=False)
        hc[...] = jnp.broadcast_to(h[tm - 1:tm, :], (SUBLANES, D))
        h_ref[...] = h
        ya_ref[...] = (_gelu(gate_ref[...]) * h).astype(BF16)

    vec = pl.BlockSpec((1, D), lambda i: (0, 0))
    heads = pl.BlockSpec(rg_wa.shape, lambda i: (0, 0, 0, 0))
    tile = pl.BlockSpec((tm, D), lambda i: (i, 0))
    return _call(
        body, name="rglru_fwd", grid=(T // tm,),
        in_specs=[pl.BlockSpec((tm, D), lambda i: (i, 0)), pl.BlockSpec((tm, D), lambda i: (i, 1)),
                  pl.BlockSpec((CONV_WIDTH, D), lambda i: (0, 0)), vec, heads, heads, vec, vec, vec],
        out_specs=[tile, tile, tile, pl.BlockSpec((tm, N_GATES * D), lambda i: (i, 0))],
        out_shape=[S((T, D), BF16), S((T, D), F32), S((T, D), F32), S((T, N_GATES * D), F32)],
        scratch_shapes=[pltpu.VMEM((tm + SUBLANES, D), F32), pltpu.VMEM((SUBLANES, D), F32)]
        + [pltpu.VMEM((D // LANES, tm, LANES), F32)] * 3 + [_RG_BLOCKS_BF16] * 2,
        args=(proj, proj, conv_w, conv_b, rg_wa, rg_wx, ba, bx, lam), sides=sides)


def _layer_norm_stats(v):
    mu = jnp.mean(v, axis=-1, keepdims=True)
    vc = v - mu
    rstd = lax.rsqrt(jnp.mean(vc * vc, axis=-1, keepdims=True) + EPS)
    return vc * rstd, rstd


def _sgu_mix(w_ref, vnb, bst_ref, n_chunk):
    cols = []
    for g in range(N_GROUP):
        vg = vnb[:, CHUNK * g:CHUNK * (g + 1)].reshape(n_chunk, CHUNK, CHUNK)
        wb = jnp.broadcast_to(w_ref[g][None], (n_chunk, CHUNK, CHUNK))
        mg = lax.dot_general(wb, vg, (((2,), (1,)), ((0,), (0,))), preferred_element_type=F32)
        mg = mg + bst_ref[:, g:g + 1][None]
        cols.append(mg.reshape(n_chunk * CHUNK, CHUNK))
    return jnp.concatenate(cols, axis=1)


def _causal_mask():
    return (lax.broadcasted_iota(jnp.int32, (CHUNK, CHUNK), 0) >= lax.broadcasted_iota(jnp.int32, (CHUNK, CHUNK), 1))


def _fill_sgu_weights(ws_ref, bs_ref, w_tril, bs_t, w_tril_t=None):
    keep = _causal_mask()
    for g in range(N_GROUP):
        wg = jnp.where(keep, ws_ref[0, g], 0.0)
        w_tril[g] = wg.astype(BF16)
        if w_tril_t is not None:
            w_tril_t[g] = wg.T.astype(BF16)
    bs_t[...] = bs_ref[0].T


_SGU_W_BF16 = pltpu.VMEM((N_GROUP, CHUNK, CHUNK), BF16)
_SGU_BT = pltpu.VMEM((CHUNK, N_GROUP), F32)


def _sgu_merge_fwd(x, proj, ya, ln_g, ln_b, sgu_ws, sgu_bs, wpa, wpb, wout):
    T = x.shape[0]
    tm = min(TM_FF, T)
    n_chunk = tm // CHUNK

    def body(x_ref, uv_ref, gab_ref, ya_ref, g_ref, b_ref, ws_ref, bs_ref, wpa_ref, wpb_ref, wout_ref,
             yb_ref, pa_ref, pb_ref, mb_ref, x1_ref, w_ref, bst_ref):
        @pl.when(pl.program_id(0) == 0)
        def _():
            _fill_sgu_weights(ws_ref, bs_ref, w_ref, bst_ref)

        vhat, _ = _layer_norm_stats(_gelu(uv_ref[:, D:2 * D]))
        vnb = (vhat * g_ref[...] + b_ref[...]).astype(BF16)
        yb = (_gelu(uv_ref[:, 0:D]) * _sgu_mix(w_ref, vnb, bst_ref, n_chunk)).astype(BF16)
        yb_ref[...] = yb
        pa = _dot(ya_ref[...], wpa_ref[...])
        pb = _dot(yb, wpb_ref[...])
        pa_ref[...] = pa.astype(BF16)
        pb_ref[...] = pb.astype(BF16)
        mb = (_sig(gab_ref[:, 0:D]) * pa + _sig(gab_ref[:, D:2 * D]) * pb).astype(BF16)
        mb_ref[...] = mb
        x1_ref[...] = x_ref[...] + _dot(mb, wout_ref[...])

    tile = pl.BlockSpec((tm, D), lambda i: (i, 0))
    vec = pl.BlockSpec((1, D), lambda i: (0, 0))
    w = _resident((D, D))
    return pl.pallas_call(
        body, name="sgu_merge_fwd", grid=(T // tm,),
        in_specs=[tile, pl.BlockSpec((tm, 2 * D), lambda i: (i, 1)), pl.BlockSpec((tm, 2 * D), lambda i: (i, 2)), tile,
                  vec, vec, pl.BlockSpec(sgu_ws.shape, lambda i: (0, 0, 0, 0)),
                  pl.BlockSpec(sgu_bs.shape, lambda i: (0, 0, 0)), w, w, w],
        out_specs=[tile, tile, tile, tile, tile],
        out_shape=[S((T, D), BF16), S((T, D), BF16), S((T, D), BF16), S((T, D), BF16), S((T, D), F32)],
        scratch_shapes=[_SGU_W_BF16, _SGU_BT],
        compiler_params=_params(1),
    )(x, proj, proj, ya, ln_g, ln_b, sgu_ws, sgu_bs, wpa, wpb, wout)


def _ffn_fwd_loss(x1, g, w_gu, w_down, g_final, target):
    T = x1.shape[0]
    tm = min(TM_FF, T)

    def body(x_ref, g_ref, wgu_ref, wd_ref, gf_ref, t_ref,
             h2_ref, gu_ref, act_ref, dx2_ref, dx2b_ref, loss_ref, dg_ref):
        @pl.when(pl.program_id(0) == 0)
        def _():
            loss_ref[...] = jnp.zeros_like(loss_ref)
            dg_ref[...] = jnp.zeros_like(dg_ref)

        xv = x_ref[...]
        h2 = (xv * _rms_stats(xv) * g_ref[...]).astype(BF16)
        h2_ref[...] = h2
        x2 = xv
        for k in range(N_SHARD // 2):
            cols = slice(FF_SHARD * k, FF_SHARD * (k + 1))
            gate = _dot(h2, wgu_ref[k])
            up = _dot(h2, wgu_ref[k + N_SHARD // 2])
            gu_ref[:, cols] = gate.astype(BF16)
            gu_ref[:, D_FF + FF_SHARD * k:D_FF + FF_SHARD * (k + 1)] = up.astype(BF16)
            act = (gate * _sig(gate) * up).astype(BF16)
            act_ref[:, cols] = act
            x2 = x2 + _dot(act, wd_ref[cols, :])
        gf = gf_ref[...]
        err = x2 * _rms_stats(x2) * gf - t_ref[...]
        loss_ref[...] += 0.5 * jnp.sum(jnp.mean(err * err, axis=-1, keepdims=True), axis=0, keepdims=True)
        dx2, dg_rows = _rms_bwd(err * (1.0 / D), x2, gf)
        dg_ref[...] += _colsum(dg_rows)
        dx2_ref[...] = dx2
        dx2b_ref[...] = dx2.astype(BF16)

    tile = pl.BlockSpec((tm, D), lambda i: (i, 0))
    vec = pl.BlockSpec((1, D), lambda i: (0, 0))
    return pl.pallas_call(
        body, name="ffn_fwd_loss", grid=(T // tm,),
        in_specs=[tile, vec, _resident((N_SHARD, D, FF_SHARD)), _resident((D_FF, D)), vec, tile],
        out_specs=[tile, pl.BlockSpec((tm, 2 * D_FF), lambda i: (i, 0)), pl.BlockSpec((tm, D_FF), lambda i: (i, 0)),
                   tile, tile, pl.BlockSpec((1, 1), lambda i: (0, 0)), vec],
        out_shape=[S((T, D), BF16), S((T, 2 * D_FF), BF16), S((T, D_FF), BF16), S((T, D), F32), S((T, D), BF16),
                   S((1, 1), F32), S((1, D), F32)],
        compiler_params=_params(1),
    )(x1, g, w_gu, w_down, g_final, target)


def _ffn_bwd(dx2, dx2b, gu, w_down, w_gu, x1, g):
    T = x1.shape[0]
    tm = min(TM_FF, T)

    def body(dx2_ref, dx2b_ref, gu_ref, wd_ref, wgu_ref, x_ref, g_ref, dgu_ref, dx1_ref, dx1b_ref, dg_ref):
        @pl.when(pl.program_id(0) == 0)
        def _():
            dg_ref[...] = jnp.zeros_like(dg_ref)

        dxb = dx2b_ref[...]
        dh2 = jnp.zeros((tm, D), F32)
        for k in range(N_SHARD // 2):
            cols = slice(FF_SHARD * k, FF_SHARD * (k + 1))
            up_cols = slice(D_FF + FF_SHARD * k, D_FF + FF_SHARD * (k + 1))
            dact = _dot_nt(dxb, wd_ref[cols, :])
            gate = gu_ref[:, cols].astype(F32)
            sg = _sig(gate)
            dgate = (dact * gu_ref[:, up_cols].astype(F32) * (sg * (1.0 + gate * (1.0 - sg)))).astype(BF16)
            dup = (dact * (gate * sg)).astype(BF16)
            dgu_ref[:, cols] = dgate
            dgu_ref[:, up_cols] = dup
            dh2 = dh2 + _dot_nt(dgate, wgu_ref[k]) + _dot_nt(dup, wgu_ref[k + N_SHARD // 2])
        dx, dg_rows = _rms_bwd(dh2, x_ref[...], g_ref[...])
        dg_ref[...] += _colsum(dg_rows)
        dx1 = dx2_ref[...] + dx
        dx1_ref[...] = dx1
        dx1b_ref[...] = dx1.astype(BF16)

    tile = pl.BlockSpec((tm, D), lambda i: (i, 0))
    wide = pl.BlockSpec((tm, 2 * D_FF), lambda i: (i, 0))
    vec = pl.BlockSpec((1, D), lambda i: (0, 0))
    return pl.pallas_call(
        body, name="ffn_bwd", grid=(T // tm,),
        in_specs=[tile, tile, wide, _resident((D_FF, D)), _resident((N_SHARD, D, FF_SHARD)), tile, vec],
        out_specs=[wide, tile, tile, vec],
        out_shape=[S((T, 2 * D_FF), BF16), S((T, D), F32), S((T, D), BF16), S((1, D), F32)],
        compiler_params=_params(1),
    )(dx2, dx2b, gu, w_down, w_gu, x1, g)


def _matmul_tn(a, b, tn, shard_major, name, sides=()):
    T, M = a.shape
    N = b.shape[1]
    tk = min(TK_WGRAD, T)

    def body(a_ref, b_ref, o_ref):
        @pl.when(pl.program_id(1) == 0)
        def _():
            o_ref[...] = jnp.zeros_like(o_ref)

        acc = _dot_tn(a_ref[...], b_ref[...])
        if shard_major:
            o_ref[0] += acc
        else:
            o_ref[...] += acc

    if shard_major:
        out_spec, out_shape = pl.BlockSpec((1, M, tn), lambda j, k: (j, 0, 0)), S((N // tn, M, tn), F32)
    else:
        out_spec, out_shape = pl.BlockSpec((M, tn), lambda j, k: (0, j)), S((M, N), F32)
    (out,), side_outs = _call(
        body, name=name, grid=(N // tn, T // tk),
        in_specs=[pl.BlockSpec((tk, M), lambda j, k: (k, 0)), pl.BlockSpec((tk, tn), lambda j, k: (k, j))],
        out_specs=[out_spec], out_shape=[out_shape], args=(a, b), sides=sides)
    return out, side_outs


PIECE = IN_SHARD // 3
N_PIECE = IN_COLS // PIECE
DPROJ_ROTATION = 2 * D // PIECE


def _merge_sgu_bwd(dx1b, proj, pa, pb, ln_g, ln_b, sgu_ws, sgu_bs, wpa, wpb, wout, sides=()):
    T = dx1b.shape[0]
    tm = min(TM_FF, T)
    n_chunk = tm // CHUNK

    def body(dx_ref, uv_ref, gab_ref, pa_ref, pb_ref, g_ref, b_ref, ws_ref, bs_ref, wpa_ref, wpb_ref, wout_ref,
             dpa_ref, dpb_ref, dya_ref, dp_ref, dw_ref, dbs_ref, dg_ref, db_ref, w_ref, wt_ref, bst_ref):
        @pl.when(pl.program_id(0) == 0)
        def _():
            for ref in (dw_ref, dbs_ref, dg_ref, db_ref):
                ref[...] = jnp.zeros_like(ref)
            _fill_sgu_weights(ws_ref, bs_ref, w_ref, bst_ref, wt_ref)

        dm = _dot_nt(dx_ref[...], wout_ref[...])
        sa = _sig(gab_ref[:, 0:D])
        sb = _sig(gab_ref[:, D:2 * D])
        dpa = (dm * sa).astype(BF16)
        dpb = (dm * sb).astype(BF16)
        dpa_ref[...] = dpa
        dpb_ref[...] = dpb
        dp_ref[:, 2 * D:3 * D] = (dm * pa_ref[...].astype(F32) * (sa * (1.0 - sa))).astype(BF16)
        dp_ref[:, 3 * D:4 * D] = (dm * pb_ref[...].astype(F32) * (sb * (1.0 - sb))).astype(BF16)
        dya_ref[...] = _dot_nt(dpa, wpa_ref[...]).astype(BF16)
        dyb_v = _dot_nt(dpb, wpb_ref[...])

        gu, dgu = _gelu_and_grad(uv_ref[:, 0:D])
        gv, dgv = _gelu_and_grad(uv_ref[:, D:2 * D])
        vhat, rstd = _layer_norm_stats(gv)
        lng = g_ref[...]
        vnb = (vhat * lng + b_ref[...]).astype(BF16)
        mixed = _sgu_mix(w_ref, vnb, bst_ref, n_chunk)
        dp_ref[:, 0:D] = (dyb_v * mixed * dgu).astype(BF16)
        dmix = dyb_v * gu
        dmb = dmix.astype(BF16)
        keep = _causal_mask()
        dvn_cols, dbs_rows = [], []
        for g in range(N_GROUP):
            sl = slice(CHUNK * g, CHUNK * (g + 1))
            dmg = dmb[:, sl].reshape(n_chunk, CHUNK, CHUNK)
            vg = vnb[:, sl].reshape(n_chunk, CHUNK, CHUNK)
            wtb = jnp.broadcast_to(wt_ref[g][None], (n_chunk, CHUNK, CHUNK))
            dvn = lax.dot_general(wtb, dmg, (((2,), (1,)), ((0,), (0,))), preferred_element_type=F32)
            dvn_cols.append(dvn.reshape(tm, CHUNK))
            dw = lax.dot_general(dmg, vg, (((2,), (2,)), ((0,), (0,))), preferred_element_type=F32)
            dw_ref[g] += jnp.where(keep, jnp.sum(dw, axis=0), 0.0)
            per_token = jnp.sum(dmix[:, sl], axis=1)
            dbs_rows.append(jnp.sum(per_token.reshape(n_chunk, CHUNK), axis=0, keepdims=True))
        dbs_ref[...] += jnp.concatenate(dbs_rows, axis=0)
        dvn = jnp.concatenate(dvn_cols, axis=1)
        dg_ref[...] += _colsum(dvn * vhat)
        db_ref[...] += _colsum(dvn)
        dvhat = dvn * lng
        dgv_in = rstd * (dvhat - jnp.mean(dvhat, axis=-1, keepdims=True)
                         - vhat * jnp.mean(dvhat * vhat, axis=-1, keepdims=True))
        dp_ref[:, D:2 * D] = (dgv_in * dgv).astype(BF16)

    tile = pl.BlockSpec((tm, D), lambda i: (i, 0))
    vec = pl.BlockSpec((1, D), lambda i: (0, 0))
    w = _resident((D, D))
    wsp = pl.BlockSpec((N_GROUP, CHUNK, CHUNK), lambda i: (0, 0, 0))
    return _call(
        body, name="merge_sgu_bwd", grid=(T // tm,),
        in_specs=[tile, pl.BlockSpec((tm, 2 * D), lambda i: (i, 1)), pl.BlockSpec((tm, 2 * D), lambda i: (i, 2)),
                  tile, tile, vec, vec, pl.BlockSpec(sgu_ws.shape, lambda i: (0, 0, 0, 0)),
                  pl.BlockSpec(sgu_bs.shape, lambda i: (0, 0, 0)), w, w, w],
        out_specs=[tile, tile, tile, pl.BlockSpec((tm, 4 * D), lambda i: (i, 0)), wsp,
                   pl.BlockSpec((N_GROUP, CHUNK), lambda i: (0, 0)), vec, vec],
        out_shape=[S((T, D), BF16), S((T, D), BF16), S((T, D), BF16), S((T, IN_COLS), BF16),
                   S((N_GROUP, CHUNK, CHUNK), F32), S((N_GROUP, CHUNK), F32), S((1, D), F32), S((1, D), F32)],
        scratch_shapes=[_SGU_W_BF16, _SGU_W_BF16, _SGU_BT],
        args=(dx1b, proj, proj, pa, pb, ln_g, ln_b, sgu_ws, sgu_bs, wpa, wpb, wout), sides=sides)


def _rglru_bwd(dya, dproj, proj, hseq, xc, gates, conv_w, rg_wa, rg_wx, lam, sides=()):
    T = dya.shape[0]
    tm = min(TM_SCAN, T)
    n = T // tm
    per8 = tm // SUBLANES

    def body(dya_ref, _, rx_ref, rxp_ref, gate_ref, h_ref, hp_ref, xc_ref, gates_ref, cw_ref, wah_ref, wxh_ref,
             lam_ref, dab_ref, dcw_ref, dcb_ref, dwah_ref, dwxh_ref, dba_ref, dbx_ref, dlam_ref,
             hext, rext, dext, carry_a, carry_dh, a_s, b_s, h_s, wa_ref, wx_ref, dwa_ref, dwx_ref):
        i = pl.program_id(0)
        first_tile = i == n - 1

        @pl.when(i == 0)
        def _():
            for ref in (dcw_ref, dcb_ref, dwa_ref, dwx_ref, dba_ref, dbx_ref, dlam_ref, carry_a, carry_dh):
                ref[...] = jnp.zeros_like(ref)
            dext[tm:tm + SUBLANES, :] = jnp.zeros((SUBLANES, D), F32)
            _fill_blockdiag(wah_ref, wa_ref)
            _fill_blockdiag(wxh_ref, wx_ref)

        gel, dgel = _gelu_and_grad(gate_ref[...])
        dya_v = dya_ref[...].astype(F32)
        hseq_v = h_ref[...]
        dgate = dya_v * hseq_v * dgel
        xcv = xc_ref[...]
        lam_v = lam_ref[...]
        sp = _softplus_neg(lam_v)
        xb = xcv.astype(BF16)
        r, gi, a, m = (gates_ref[:, D * k:D * (k + 1)] for k in range(N_GATES))

        row = lax.broadcasted_iota(jnp.int32, (tm, D), 0)
        c = jnp.where(row == tm - 1, carry_a[0:1, :], _shift_up(a, 1, 0.0))
        dH = _scan_tile(c, dya_v * gel, carry_dh[0:1, :], a_s, b_s, h_s, reverse=True)
        carry_a[...] = jnp.broadcast_to(a[0:1, :], (SUBLANES, D))
        carry_dh[...] = jnp.broadcast_to(dH[0:1, :], (SUBLANES, D))

        hext[0:SUBLANES, :] = jnp.where(first_tile, 0.0, hp_ref[...])
        hext[SUBLANES:SUBLANES + tm, :] = hseq_v
        h_prev = hext[pl.ds(SUBLANES - 1, tm), :]

        d_m = dH * (gi * xcv)
        d_la = dH * h_prev * a - d_m * (a * a) / m
        d_ia = dH * m * xcv * (gi * (1.0 - gi))
        d_ra = d_la * ((-RG_C) * sp) * (r * (1.0 - r))
        dlam_ref[...] += _colsum(d_la * ((-RG_C) * r)) * (-_sig(-lam_v))
        dba_ref[...] += _colsum(d_ra)
        dbx_ref[...] += _colsum(d_ia)
        drab = d_ra.astype(BF16)
        diab = d_ia.astype(BF16)
        dxc_cols = []
        for j in range(N_RG_BLOCK):
            sl = slice(RG_BLOCK * j, RG_BLOCK * (j + 1))
            dxc_cols.append(_dot_nt(drab[:, sl], wa_ref[j]) + _dot_nt(diab[:, sl], wx_ref[j]))
            dwa_ref[j] += _dot_tn(xb[:, sl], drab[:, sl])
            dwx_ref[j] += _dot_tn(xb[:, sl], diab[:, sl])
        dxc = dH * m * gi + jnp.concatenate(dxc_cols, axis=1)

        dcb_ref[...] += _colsum(dxc)
        dext[0:tm, :] = dxc
        rext[0:SUBLANES, :] = jnp.where(first_tile, 0.0, rxp_ref[...])
        rext[SUBLANES:SUBLANES + tm, :] = rx_ref[...]
        drx = jnp.zeros((tm, D), F32)
        for k in range(CONV_WIDTH):
            drx = drx + dext[pl.ds(CONV_WIDTH - 1 - k, tm), :] * cw_ref[k:k + 1, :]
            dcw_ref[k:k + 1, :] += _colsum(dxc * rext[pl.ds(SUBLANES - (CONV_WIDTH - 1) + k, tm), :])
        dext[tm:tm + SUBLANES, :] = dext[0:SUBLANES, :]
        dab_ref[:, 0:D] = drx.astype(BF16)
        dab_ref[:, D:2 * D] = dgate.astype(BF16)

        @pl.when(first_tile)
        def _():
            for j in range(N_RG_BLOCK):
                for h in range(HEADS_PER_BLOCK):
                    sl = slice(HEAD_DIM * h, HEAD_DIM * (h + 1))
                    pair, side = divmod(HEADS_PER_BLOCK * j + h, 2)
                    lanes = slice(HEAD_DIM * side, HEAD_DIM * (side + 1))
                    dwah_ref[pair, :, lanes] = dwa_ref[j, sl, sl]
                    dwxh_ref[pair, :, lanes] = dwx_ref[j, sl, sl]

    def rev(col):
        return lambda i: (n - 1 - i, col)

    def prev8(col):
        return lambda i: (jnp.maximum((n - 1 - i) * per8 - 1, 0), col)

    tile = pl.BlockSpec((tm, D), rev(0))
    vec = pl.BlockSpec((1, D), lambda i: (0, 0))
    heads_in = pl.BlockSpec(rg_wa.shape, lambda i: (0, 0, 0, 0))
    head_pairs = (rg_wa.shape[1] // 2, HEAD_DIM, 2 * HEAD_DIM)
    heads_out = pl.BlockSpec(head_pairs, lambda i: (0, 0, 0))
    cw = pl.BlockSpec((CONV_WIDTH, D), lambda i: (0, 0))
    blocks_f32 = pltpu.VMEM((N_RG_BLOCK, RG_BLOCK, RG_BLOCK), F32)
    return _call(
        body, name="rglru_bwd", grid=(n,),
        in_specs=[tile, _ANY, pl.BlockSpec((tm, D), rev(0)), pl.BlockSpec((SUBLANES, D), prev8(0)),
                  pl.BlockSpec((tm, D), rev(1)), tile, pl.BlockSpec((SUBLANES, D), prev8(0)), tile,
                  pl.BlockSpec((tm, N_GATES * D), rev(0)), cw, heads_in, heads_in, vec],
        out_specs=[pl.BlockSpec((tm, 2 * D), rev(2)), cw, vec, heads_out, heads_out, vec, vec, vec],
        out_shape=[S((T, IN_COLS), BF16), S((CONV_WIDTH, D), F32), S((1, D), F32),
                   S(head_pairs, F32), S(head_pairs, F32), S((1, D), F32), S((1, D), F32), S((1, D), F32)],
        scratch_shapes=[pltpu.VMEM((tm + SUBLANES, D), F32), pltpu.VMEM((tm + SUBLANES, D), F32),
                        pltpu.VMEM((tm + SUBLANES, D), F32), pltpu.VMEM((SUBLANES, D), F32),
                        pltpu.VMEM((SUBLANES, D), F32)] + [pltpu.VMEM((D // LANES, tm, LANES), F32)] * 3
        + [_RG_BLOCKS_BF16] * 2 + [blocks_f32] * 2,
        args=(dya, dproj, proj, proj, proj, hseq, hseq, xc, gates, conv_w, rg_wa, rg_wx, lam), sides=sides,
        aliases={1: 0})


def _inproj_dh(dproj, w_in, dh, first, count, name, sides=()):
    T = dproj.shape[0]
    tm = min(TM_MM, T)

    def body(*refs):
        dp_ref, w_ref, dh_ref = refs[-3:]
        dh = jnp.zeros((tm, D), F32)
        for p in range(N_PIECE):
            shard, part = divmod((p + DPROJ_ROTATION) % N_PIECE, IN_SHARD // PIECE)
            dh = dh + _dot_nt(dp_ref[:, PIECE * p:PIECE * (p + 1)], w_ref[shard, :, PIECE * part:PIECE * (part + 1)])
        dh_ref[...] = dh

    earlier = [] if dh is None else [dh]
    return _call(
        body, name=name, grid=(count,),
        in_specs=[_ANY] * len(earlier) + [pl.BlockSpec((tm, IN_COLS), lambda i: (first + i, 0)),
                                         _resident((N_SHARD, D, IN_SHARD))],
        out_specs=[pl.BlockSpec((tm, D), lambda i: (first + i, 0))],
        out_shape=[S((T, D), F32)],
        args=(*earlier, dproj, w_in), sides=sides, aliases={0: 0} if earlier else None)


def _inproj_norm_bwd(dh, x, g, dx1):
    T = x.shape[0]
    tm = min(TM_MM, T)

    def body(dh_ref, x_ref, g_ref, dx1_ref, dx_ref, dgm_ref):
        @pl.when(pl.program_id(0) == 0)
        def _():
            dgm_ref[...] = jnp.zeros_like(dgm_ref)

        dx, dg_rows = _rms_bwd(dh_ref[...], x_ref[...], g_ref[...])
        dgm_ref[...] += _colsum(dg_rows)
        dx_ref[...] = dx1_ref[...] + dx

    tile = pl.BlockSpec((tm, D), lambda i: (i, 0))
    vec = pl.BlockSpec((1, D), lambda i: (0, 0))
    return pl.pallas_call(
        body, name="inproj_norm_bwd", grid=(T // tm,),
        in_specs=[tile, tile, vec, tile],
        out_specs=[tile, vec],
        out_shape=[S((T, D), F32), S((1, D), F32)],
        compiler_params=_params(1),
    )(dh, x, g, dx1)


def _inproj_wgrad(h, dproj, sides=()):
    T = h.shape[0]
    tk = min(TK_WGRAD, T)
    per = IN_SHARD // PIECE

    def body(h_ref, *refs):
        pieces, o_ref = refs[:per], refs[per]

        @pl.when(pl.program_id(1) == 0)
        def _():
            o_ref[...] = jnp.zeros_like(o_ref)

        o_ref[0] += _dot_tn(h_ref[...], jnp.concatenate([p[...] for p in pieces], axis=1))

    def piece(i):
        return pl.BlockSpec((tk, PIECE), lambda j, k: (k, (per * j + i + N_PIECE - DPROJ_ROTATION) % N_PIECE))

    (out,), side_outs = _call(
        body, name="inproj_wgrad", grid=(N_SHARD, T // tk),
        in_specs=[pl.BlockSpec((tk, D), lambda j, k: (k, 0))] + [piece(i) for i in range(per)],
        out_specs=[pl.BlockSpec((1, D, IN_SHARD), lambda j, k: (j, 0, 0))],
        out_shape=[S((N_SHARD, D, IN_SHARD), F32)], args=(h,) + (dproj,) * per, sides=sides)
    return out, side_outs


def _row_tile(rows):
    for t in range(256, 0, -SUBLANES):
        if rows % t == 0:
            return t
    raise ValueError(rows)


def _add_halves(core, g, theirs, name):
    _, r, cols = g.shape
    half = r // 2
    tr = _row_tile(half)
    nb = half // tr

    def body(core_ref, g_ref, t_ref, o32_ref, o16_ref):
        s = g_ref[...] + t_ref[...]
        o32_ref[...] = s
        o16_ref[...] = s.astype(BF16)

    blk = pl.BlockSpec((1, tr, cols), lambda s, i, core_ref: (s, i, 0))
    gs = pltpu.PrefetchScalarGridSpec(
        num_scalar_prefetch=1, grid=(N_SHARD, nb),
        in_specs=[pl.BlockSpec((1, tr, cols), lambda s, i, core_ref: (s, core_ref[0] * nb + i, 0)), blk],
        out_specs=[blk, blk])
    return pl.pallas_call(
        body, name=name, grid_spec=gs,
        out_shape=[S((N_SHARD, half, cols), F32), S((N_SHARD, half, cols), BF16)],
        compiler_params=_params(2),
    )(core, g, theirs)


def _sum_shards(chip, own, others, name):
    _, half, cols = own.shape
    tr = _row_tile(half)

    def body(chip_ref, own_ref, oth_ref, o_ref):
        acc = own_ref[0]
        for j in range(3):
            acc = acc + oth_ref[j].astype(F32)
        o_ref[...] = acc

    gs = pltpu.PrefetchScalarGridSpec(
        num_scalar_prefetch=1, grid=(half // tr,),
        in_specs=[pl.BlockSpec((1, tr, cols), lambda i, chip_ref: (chip_ref[0], i, 0)),
                  pl.BlockSpec((3, tr, cols), lambda i, chip_ref: (0, i, 0))],
        out_specs=pl.BlockSpec((tr, cols), lambda i, chip_ref: (i, 0)))
    return pl.pallas_call(
        body, name=name, grid_spec=gs, out_shape=S((half, cols), F32), compiler_params=_params(1),
    )(chip, own, others)


def _adamw(w, g, m, v):
    m = ADAM_B1 * m + (1.0 - ADAM_B1) * g
    v = ADAM_B2 * v + (1.0 - ADAM_B2) * (g * g)
    m_hat = m / (1.0 - ADAM_B1 ** ADAM_STEP)
    v_hat = v / (1.0 - ADAM_B2 ** ADAM_STEP)
    delta = -ADAM_LR * (m_hat / (jnp.sqrt(v_hat) + ADAM_EPS) + ADAM_WD * w)
    return delta, m, v


def _adamw_shard(core, mine, theirs, w, m, v, name):
    r, cols = w.shape
    half = r // 2
    tr = _row_tile(half)
    nb = half // tr

    def body(core_ref, mine_ref, theirs_ref, w_ref, m_ref, v_ref, g_ref, d_ref, mo_ref, vo_ref):
        g = jnp.where(pl.program_id(0) == core_ref[0], mine_ref[...], theirs_ref[...])
        g_ref[...] = g
        d_ref[...], mo_ref[...], vo_ref[...] = _adamw(w_ref[...], g, m_ref[...], v_ref[...])

    hblk = pl.BlockSpec((tr, cols), lambda h, i, core_ref: (i, 0))
    blk = pl.BlockSpec((tr, cols), lambda h, i, core_ref: (h * nb + i, 0))
    gs = pltpu.PrefetchScalarGridSpec(num_scalar_prefetch=1, grid=(2, nb),
                                      in_specs=[hblk, hblk, blk, blk, blk], out_specs=[blk] * 4)
    return pl.pallas_call(
        body, name=name, grid_spec=gs, out_shape=[S((r, cols), F32)] * 4, compiler_params=_params(2),
    )(core, mine, theirs, w, m, v)


def _adamw_whole(w, g, m, v, name):
    def body(w_ref, g_ref, m_ref, v_ref, d_ref, mo_ref, vo_ref):
        d_ref[...], mo_ref[...], vo_ref[...] = _adamw(w_ref[...], g_ref[...], m_ref[...], v_ref[...])

    return pl.pallas_call(body, name=name, out_shape=[S(w.shape, F32)] * 3)(w, g, m, v)


_VEC_ROWS = ("norm_mix_g", "conv_b", "rg_lambda", "sgu_ln_g", "sgu_ln_b", "norm_ffn_g", "norm_final_g", "rg_ba",
             "rg_bx")
_CONV_ROW = len(_VEC_ROWS)
_LOSS_ROW = _CONV_ROW + CONV_WIDTH
_VEC_PAD = -(_LOSS_ROW + 1) % SUBLANES
_HEAD_BIASES = ("rg_ba", "rg_bx")
_TENSORS = ("sgu_bs", "sgu_ws", "rg_wa", "rg_wx")
_HEAD_PAIRS = ("rg_wa", "rg_wx")


def _small_sum_adamw(parts, w, m, v):
    names = [n for n in _VEC_ROWS] + list(_TENSORS)
    n_parts = len(parts)

    def total(ref):
        acc = ref[0]
        for k in range(1, N_DEVICE):
            acc = acc + ref[k]
        return acc

    def body(*refs):
        part_refs, refs = refs[:n_parts], refs[n_parts:]
        w_refs, m_refs, v_refs = (dict(zip(names, refs[k * len(names):(k + 1) * len(names)])) for k in range(3))
        outs = refs[3 * len(names):]
        out_refs = {n: outs[4 * k:4 * k + 4] for k, n in enumerate(names)}
        conv_ref, loss_ref = outs[4 * len(names):]
        vec = total(part_refs[0])
        grads = {n: total(p) for n, p in zip(_TENSORS, part_refs[1:])}
        for n in _HEAD_PAIRS:
            pairs = grads[n]
            grads[n] = jnp.stack([pairs[k // 2, :, HEAD_DIM * (k % 2):HEAD_DIM * (k % 2 + 1)]
                                  for k in range(2 * pairs.shape[0])], axis=0)
        grads = {n: g[None] for n, g in grads.items()}
        for row, n in enumerate(_VEC_ROWS):
            g = vec[row:row + 1, :]
            if n in _HEAD_BIASES:
                g = jnp.concatenate([g[:, HEAD_DIM * h:HEAD_DIM * (h + 1)] for h in range(D // HEAD_DIM)], axis=0)[None]
            grads[n] = g
        for n in names:
            g_ref, d_ref, mo_ref, vo_ref = out_refs[n]
            g_ref[...] = grads[n]
            d_ref[...], mo_ref[...], vo_ref[...] = _adamw(w_refs[n][...], grads[n], m_refs[n][...], v_refs[n][...])
        conv_ref[...] = vec[_CONV_ROW:_CONV_ROW + CONV_WIDTH, :]
        loss_ref[...] = vec[_LOSS_ROW:_LOSS_ROW + 1, 0:1]

    res = pl.pallas_call(
        body, name="small_sum_adamw",
        out_shape=[S(w[n].shape, F32) for n in names for _ in range(4)] + [S((CONV_WIDTH, D), F32), S((1, 1), F32)],
        compiler_params=pltpu.CompilerParams(vmem_limit_bytes=VMEM_LIMIT),
    )(*parts, *[w[n] for n in names], *[m[n] for n in names], *[v[n] for n in names])
    return {n: tuple(res[4 * k:4 * k + 4]) for k, n in enumerate(names)}, res[-2], res[-1]


_BIG = ("w_in", "w_proj_a", "w_proj_b", "w_out", "w_gate_up", "w_down")
_WEIGHTS = ("norm_mix_g", "w_in", "conv_w", "conv_b", "rg_wa", "rg_ba", "rg_wx", "rg_bx", "rg_lambda", "sgu_ln_g",
            "sgu_ln_b", "sgu_ws", "sgu_bs", "w_proj_a", "w_proj_b", "w_out", "norm_ffn_g", "w_gate_up", "w_down",
            "norm_final_g")


def kernel(x, norm_mix_g, w_in, conv_w, conv_b, rg_wa, rg_ba, rg_wx, rg_bx, rg_lambda, sgu_ln_g, sgu_ln_b, sgu_ws, sgu_bs, w_proj_a, w_proj_b, w_out, norm_ffn_g, w_gate_up, w_down, norm_final_g, loss_target, m_norm_mix_g, m_w_in, m_conv_w, m_conv_b, m_rg_wa, m_rg_ba, m_rg_wx, m_rg_bx, m_rg_lambda, m_sgu_ln_g, m_sgu_ln_b, m_sgu_ws, m_sgu_bs, m_w_proj_a, m_w_proj_b, m_w_out, m_norm_ffn_g, m_w_gate_up, m_w_down, m_norm_final_g, v_norm_mix_g, v_w_in, v_conv_w, v_conv_b, v_rg_wa, v_rg_ba, v_rg_wx, v_rg_bx, v_rg_lambda, v_sgu_ln_g, v_sgu_ln_b, v_sgu_ws, v_sgu_bs, v_w_proj_a, v_w_proj_b, v_w_out, v_norm_ffn_g, v_w_gate_up, v_w_down, v_norm_final_g):
    args = dict(locals())
    w = {n: args[n] for n in _WEIGHTS}
    mom = {n: args["m_" + n] for n in _WEIGHTS}
    var = {n: args["v_" + n] for n in _WEIGHTS}
    xi, yi, ci = _position()
    core = ci.astype(jnp.int32).reshape(1)
    chip = (2 * xi + yi).astype(jnp.int32).reshape(1)

    bf = {n: w[n][0].astype(BF16) for n in _BIG}
    final_g = w["norm_final_g"].reshape(1, D)
    ba, bx = w["rg_ba"].reshape(1, D), w["rg_bx"].reshape(1, D)
    lam, ln_g, ln_b = w["rg_lambda"], w["sgu_ln_g"], w["sgu_ln_b"]
    x0, target = x[0], loss_target[0]

    def shard_major(g):
        return g.reshape(N_SHARD, g.shape[0] // N_SHARD, g.shape[1])

    def chip_sums(names, grads, theirs):
        return [_add_halves(core, g, t, "add_halves_" + n) for n, g, t in zip(names, grads, theirs)]

    def my_halves(names, sums, arrived):
        return [_sum_shards(chip, p32, a, "sum_shards_" + n) for n, (p32, _), a in zip(names, sums, arrived)]

    (proj, h), ((w_in_a,), (conv_a,)) = _inproj_own(
        chip, x0, w["norm_mix_g"], bf["w_in"], sides=[_gather_half_side([bf["w_in"]]), _gather_side([w["conv_w"][0]])])
    conv_cols = conv_a.shape[-1]
    conv_full = jnp.swapaxes(conv_a, 0, 1).reshape(CONV_WIDTH, D)
    (proj,), ((w_pa_a, w_pb_a, w_out_a),) = _inproj_rest(
        chip, proj, h, w_in_a, sides=[_gather_half_side([bf["w_proj_a"], bf["w_proj_b"], bf["w_out"]])])
    (ya, xc, hseq, gates), ((w_gu_a, w_down_a),) = _rglru_fwd(
        proj, conv_full, w["conv_b"], w["rg_wa"], w["rg_wx"], ba, bx, lam,
        sides=[_gather_half_side([bf["w_gate_up"], bf["w_down"]])])
    wpa, wpb, wout, wdown = w_pa_a.reshape(D, D), w_pb_a.reshape(D, D), w_out_a.reshape(D, D), w_down_a.reshape(D_FF, D)
    yb, pa, pb, mb, x1 = _sgu_merge_fwd(x0, proj, ya, ln_g, ln_b, w["sgu_ws"], w["sgu_bs"], wpa, wpb, wout)
    h2, gu, act, dx2, dx2b, loss, d_final_g = _ffn_fwd_loss(x1, w["norm_ffn_g"], w_gu_a, wdown, final_g, target)

    dgu, dx1, dx1b, d_ffn_g = _ffn_bwd(dx2, dx2b, gu, wdown, w_gu_a, x1, w["norm_ffn_g"])
    ffn = ("w_gate_up", "w_down")
    g_ffn = [_matmul_tn(h2, dgu, FF_SHARD, True, "wgrad_gate_up")[0],
             shard_major(_matmul_tn(act, dx2b, D // 2, False, "wgrad_down")[0])]
    (dpa, dpb, dya, dproj, d_ws, d_bs, d_lng, d_lnb), (theirs_ffn,) = _merge_sgu_bwd(
        dx1b, proj, pa, pb, ln_g, ln_b, w["sgu_ws"], w["sgu_bs"], wpa, wpb, wout, sides=[_halves_side(g_ffn)])
    sums_ffn = chip_sums(ffn, g_ffn, theirs_ffn)
    mix = ("w_proj_a", "w_proj_b", "w_out")
    g_mix = [shard_major(_matmul_tn(ya, dpa, D, False, "wgrad_proj_a")[0]),
             shard_major(_matmul_tn(yb, dpb, D, False, "wgrad_proj_b")[0]),
             shard_major(_matmul_tn(mb, dx1b, D, False, "wgrad_out")[0])]
    (dproj, d_cw, d_cb, d_wa, d_wx, d_ba, d_bx, d_lam), (arrived_ffn, theirs_mix, sgu_parts) = _rglru_bwd(
        dya, dproj, proj, hseq, xc, gates, conv_full, w["rg_wa"], w["rg_wx"], lam,
        sides=[_scatter_side([p16 for _, p16 in sums_ffn]), _halves_side(g_mix), _everyone_side([d_bs, d_ws])])
    mine_ffn = my_halves(ffn, sums_ffn, arrived_ffn)
    sums_mix = chip_sums(mix, g_mix, theirs_mix)
    g_in, ((wa_parts, wx_parts), other_ffn, arrived_mix) = _inproj_wgrad(
        h, dproj, sides=[_everyone_side([d_wa, d_wx]), _swap_side(mine_ffn),
                         _scatter_side([p16 for _, p16 in sums_mix])])
    mine_mix = my_halves(mix, sums_mix, arrived_mix)
    n_tiles = x0.shape[0] // min(TM_MM, x0.shape[0])
    n_first = max(1, n_tiles * 3 // 8)
    (dh,), (theirs_in, other_mix) = _inproj_dh(dproj, w_in_a, None, 0, n_first, "inproj_dh_a",
                                               sides=[_halves_side([g_in]), _swap_side(mine_mix)])
    sums_in = chip_sums(("w_in",), [g_in], theirs_in)
    (dh,), (arrived_in,) = _inproj_dh(dproj, w_in_a, dh, n_first, n_tiles - n_first, "inproj_dh_b",
                                      sides=[_scatter_side([p16 for _, p16 in sums_in])])
    mine_in = my_halves(("w_in",), sums_in, arrived_in)
    grad_x, d_mix_g = _inproj_norm_bwd(dh, x0, w["norm_mix_g"], dx1)
    rows = {"norm_mix_g": d_mix_g, "conv_b": d_cb, "rg_lambda": d_lam, "sgu_ln_g": d_lng, "sgu_ln_b": d_lnb,
            "norm_ffn_g": d_ffn_g, "norm_final_g": d_final_g, "rg_ba": d_ba, "rg_bx": d_bx}
    vec = jnp.concatenate([rows[n] for n in _VEC_ROWS]
                          + [d_cw, jnp.pad(loss, ((0, _VEC_PAD), (0, D - 1)))], axis=0)
    other_in, (vec_parts,) = _comm_only([_swap_side(mine_in), _everyone_side([vec])], "swap_w_in")
    small_parts = [vec_parts] + sgu_parts + [wa_parts, wx_parts]

    out = {}
    for n, gm, go in zip(ffn + mix + ("w_in",), mine_ffn + mine_mix + mine_in, other_ffn + other_mix + other_in):
        g, d, mo, vo = _adamw_shard(core, gm, go, w[n][0], mom[n][0], var[n][0], "adamw_" + n)
        out[n] = tuple(a[None] for a in (g, d, mo, vo))
    as_row = lambda t: {n: a.reshape(1, D) if n == "norm_final_g" else a for n, a in t.items()}
    small_out, conv_sum, loss_sum = _small_sum_adamw(small_parts, as_row(w), as_row(mom), as_row(var))
    out.update(small_out)
    out["norm_final_g"] = tuple(a.reshape(D) for a in small_out["norm_final_g"])
    conv_g = lax.dynamic_slice_in_dim(conv_sum, chip[0] * conv_cols, conv_cols, axis=1)
    d, mo, vo = _adamw_whole(w["conv_w"][0], conv_g, mom["conv_w"][0], var["conv_w"][0], "adamw_conv_w")
    out["conv_w"] = tuple(a[None] for a in (conv_g, d, mo, vo))

    return (loss_sum[0, 0], grad_x[None], *[out[n][0] for n in _WEIGHTS], *[out[n][1] for n in _WEIGHTS],
            *[out[n][2] for n in _WEIGHTS], *[out[n][3] for n in _WEIGHTS])
```

```python
import functools

import jax
import jax.numpy as jnp
from jax import lax
from jax.experimental import pallas as pl
from jax.experimental.pallas import tpu as pltpu

F32 = jnp.float32
BF16 = jnp.bfloat16
S = jax.ShapeDtypeStruct

D = 1024
N_SHARD = 4
IN_COLS = 6 * D
IN_SHARD = IN_COLS // N_SHARD
D_FF = 2816
FF_SHARD = 2 * D_FF // N_SHARD
RG_BLOCK = 256
N_RG_BLOCK = D // RG_BLOCK
CHUNK = 128
N_GROUP = 8
CONV_WIDTH = 4
RG_C = 8.0
EPS = 1e-6
ADAM_LR, ADAM_B1, ADAM_B2, ADAM_EPS, ADAM_WD, ADAM_STEP = 0.001, 0.9, 0.999, 1e-08, 0.01, 10

V7X_VMEM_BYTES = 64 * 1024 * 1024
VMEM_LIMIT = V7X_VMEM_BYTES * 3 // 4
SUBLANES = 8
MESH = pl.DeviceIdType.MESH

TM_MM = 512
TM_SCAN = 256
TM_FF = 256
TK_WGRAD = 1024


def _params(n_axes):
    return pltpu.CompilerParams(dimension_semantics=("arbitrary",) * n_axes, vmem_limit_bytes=VMEM_LIMIT)


def _resident(shape):
    nd = len(shape)
    return pl.BlockSpec(shape, lambda *_: (0,) * nd, pipeline_mode=pl.Buffered(1))


def _sig(x):
    return 1.0 / (1.0 + jnp.exp(-x))


_GELU_K2 = 2.0 * 0.7978845608028654
_GELU_C = 0.044715


def _gelu(x):
    return x * _sig(x * (_GELU_K2 + (_GELU_K2 * _GELU_C) * (x * x)))


def _gelu_and_grad(x):
    x2 = x * x
    s = _sig(x * (_GELU_K2 + (_GELU_K2 * _GELU_C) * x2))
    g = x * s
    return g, s + g * (1.0 - s) * (_GELU_K2 + (3.0 * _GELU_K2 * _GELU_C) * x2)


_EXPM1_SERIES = tuple(1.0 / f for f in (5040.0, 720.0, 120.0, 24.0, 6.0, 2.0, 1.0))


def _one_minus_exp(x):
    p = _EXPM1_SERIES[0]
    for coef in _EXPM1_SERIES[1:]:
        p = p * x + coef
    return jnp.where(x > -0.125, -x * p, 1.0 - jnp.exp(x))


def _softplus_neg(lam):
    z = -lam
    e = jnp.exp(-jnp.abs(z))
    u = 1.0 + e
    log1p = jnp.where(u == 1.0, e, jnp.log(u) * e / (u - 1.0))
    return jnp.maximum(z, 0.0) + log1p


def _rms_stats(x):
    return lax.rsqrt(jnp.mean(x * x, axis=-1, keepdims=True) + EPS)


def _rms_bwd(dy, x, g):
    rstd = _rms_stats(x)
    xhat = x * rstd
    dxhat = dy * g
    dx = rstd * (dxhat - xhat * jnp.mean(dxhat * xhat, axis=-1, keepdims=True))
    return dx, dy * xhat


def _colsum(x):
    return jnp.sum(x, axis=0, keepdims=True)


def _shift_down(x, d, fill):
    n = x.shape[0]
    if d % SUBLANES == 0:
        return jnp.concatenate([jnp.full((d, x.shape[1]), fill, x.dtype), x[:n - d]], axis=0)
    row = lax.broadcasted_iota(jnp.int32, x.shape, 0)
    return jnp.where(row < d, fill, pltpu.roll(x, d, 0))


def _shift_up(x, d, fill):
    n = x.shape[0]
    if d % SUBLANES == 0:
        return jnp.concatenate([x[d:], jnp.full((d, x.shape[1]), fill, x.dtype)], axis=0)
    row = lax.broadcasted_iota(jnp.int32, x.shape, 0)
    return jnp.where(row >= n - d, fill, pltpu.roll(x, n - d, 0))


def _scan(a, b, shift):
    d = 1
    while d < a.shape[0]:
        b = a * shift(b, d, 0.0) + b
        a = a * shift(a, d, 1.0)
        d *= 2
    return a, b


LANES = 128


def _scan_tile(a, b, outside, a_s, b_s, h_s, reverse):
    tm = a.shape[0]
    groups = tm // SUBLANES
    order = list(range(SUBLANES - 1, -1, -1) if reverse else range(SUBLANES))
    shift = _shift_up if reverse else _shift_down
    edge = groups - 1 if reverse else 0
    for j in range(D // LANES):
        a_s[j] = a[:, LANES * j:LANES * (j + 1)]
        b_s[j] = b[:, LANES * j:LANES * (j + 1)]
    for j in range(D // LANES):
        def slab(ref, k):
            return ref[j, pl.ds(k, groups, stride=SUBLANES), :]

        ga, gb = slab(a_s, order[0]), slab(b_s, order[0])
        for k in order[1:]:
            ak = slab(a_s, k)
            gb = ak * gb + slab(b_s, k)
            ga = ak * ga
        ga, gb = _scan(ga, gb, shift)
        h_out = outside[:, LANES * j:LANES * (j + 1)]
        group_end = ga * h_out + gb
        row = lax.broadcasted_iota(jnp.int32, (groups, LANES), 0)
        h = jnp.where(row == edge, h_out, shift(group_end, 1, 0.0))
        for k in order:
            h = slab(a_s, k) * h + slab(b_s, k)
            h_s[j, pl.ds(k, groups, stride=SUBLANES), :] = h
    return jnp.concatenate([h_s[j] for j in range(D // LANES)], axis=1)


def _dot(a, b):
    return jnp.dot(a, b, preferred_element_type=F32)


def _dot_nt(a, b):
    return lax.dot_general(a, b, (((1,), (1,)), ((), ())), preferred_element_type=F32)


def _dot_tn(a, b):
    return lax.dot_general(a, b, (((0,), (0,)), ((), ())), preferred_element_type=F32)


_ANY = pl.BlockSpec(memory_space=pl.ANY)


def _position():
    return lax.axis_index("x"), lax.axis_index("y"), lax.axis_index("c")


def _other_chips(x, y):
    return [(1 - x, y), (x, 1 - y), (1 - x, 1 - y)]


class _Side:
    def __init__(self, inputs, out_shapes, n_sems, make, continues=()):
        self.inputs, self.out_shapes, self.n_sems, self.make = list(inputs), list(out_shapes), n_sems, make
        self.continues = list(continues)


MID_STEP = 0.625


def _call(body, *, name, grid, in_specs, out_specs, out_shape, args, scratch_shapes=(), sides=(), aliases=None,
          scalars=None, mid=MID_STEP):
    n_in, n_out, n_scr = len(in_specs), len(out_specs), len(scratch_shapes)
    n_scalar = 0 if scalars is None else 1
    side_in = [len(s.inputs) + len(s.continues) for s in sides]
    side_out = [len(s.out_shapes) for s in sides]
    all_aliases = {k + n_scalar: v for k, v in (aliases or {}).items()}
    for idx, s in enumerate(sides):
        for k in range(len(s.continues)):
            operand = n_scalar + n_in + sum(side_in[:idx]) + len(s.inputs) + k
            all_aliases[operand] = n_out + sum(side_out[:idx]) + k

    def wrapped(*refs):
        refs = list(refs)
        take = lambda k: [refs.pop(0) for _ in range(k)]
        ins = take(n_scalar) + take(n_in)
        sins = [take(k) for k in side_in]
        outs = take(n_out)
        souts = [take(k) for k in side_out]
        scr = take(n_scr)
        sems = [take(3) for _ in sides]
        def run(phase):
            for s, si, so, sem in zip(sides, sins, souts, sems):
                for thunk in s.make(si[:len(s.inputs)], so, *sem)[phase]:
                    thunk()

        if sides:
            n_steps = functools.reduce(lambda a, b: a * b, grid)
            step = functools.reduce(lambda a, b: a + b, [
                pl.program_id(a) * functools.reduce(lambda p, q: p * q, grid[a + 1:], 1) for a in range(len(grid))])
            pl.when(step == 0)(lambda: run(0))
        body(*ins, *outs, *scr)
        if sides:
            pl.when(step == int(mid * (n_steps - 1)))(lambda: run(1))
            pl.when(step == n_steps - 1)(lambda: run(2))

    grid_spec = pltpu.PrefetchScalarGridSpec(
        num_scalar_prefetch=n_scalar, grid=grid,
        in_specs=list(in_specs) + [_ANY] * sum(side_in),
        out_specs=list(out_specs) + [_ANY] * sum(side_out),
        scratch_shapes=list(scratch_shapes) + [pltpu.SemaphoreType.DMA((s.n_sems,)) for s in sides for _ in range(3)])
    res = pl.pallas_call(
        wrapped, name=name, grid_spec=grid_spec,
        out_shape=list(out_shape) + [o for s in sides for o in s.out_shapes],
        input_output_aliases=all_aliases,
        compiler_params=_params(len(grid)),
    )(*([scalars] if n_scalar else []), *args, *[a for s in sides for a in s.inputs + s.continues])
    main, rest, per_side = list(res[:n_out]), list(res[n_out:]), []
    for k in side_out:
        per_side.append(rest[:k])
        rest = rest[k:]
    return main, per_side


def _comm_only(sides, name):
    def body():
        pass

    return _call(body, name=name, grid=(1,), in_specs=[], out_specs=[], out_shape=[], args=[], sides=sides)[1]


def _remote(src, dst, send, recv, k, device):
    return pltpu.make_async_remote_copy(src_ref=src, dst_ref=dst, send_sem=send.at[k], recv_sem=recv.at[k],
                                        device_id=device, device_id_type=MESH)


def _both_ways(copy, keys):
    return [lambda k=k: copy(k).start() for k in keys], [], [lambda k=k: copy(k).wait() for k in keys]


def _gather_side(shards):
    n = len(shards)

    def make(ins, outs, send, recv, local):
        x, y, c = _position()
        mine = 2 * x + y
        chips = _other_chips(x, y)
        pairs = [(w, j) for w in range(n) for j in range(3)]

        def own(w):
            return pltpu.make_async_copy(ins[w], outs[w].at[mine], local.at[w])

        def push(w, j):
            return _remote(ins[w], outs[w].at[mine], send, recv, 3 * w + j, (*chips[j], c))

        def arrival(w, j):
            px, py = chips[j]
            return _remote(ins[w], outs[w].at[2 * px + py], send, recv, 3 * w + j, (px, py, c))

        starts = [lambda w=w: own(w).start() for w in range(n)] + [lambda w=w, j=j: push(w, j).start() for w, j in pairs]
        waits = ([lambda w=w, j=j: arrival(w, j).wait_recv() for w, j in pairs]
                 + [lambda w=w, j=j: push(w, j).wait_send() for w, j in pairs]
                 + [lambda w=w: own(w).wait() for w in range(n)])
        return starts, [], waits

    return _Side(shards, [S((N_SHARD,) + s.shape, s.dtype) for s in shards], 3 * n, make)


def _gather_half_side(shards, relations=(0, 1, 2), into=None):
    n = len(shards)

    def make(ins, outs, send, recv, local):
        x, y, c = _position()
        mine = 2 * x + y
        chips = _other_chips(x, y)
        pairs = [(w, j) for w in range(n) for j in relations]

        def rows(w, core):
            half = ins[w].shape[0] // 2
            return pl.ds(core * half, half)

        def own(w):
            return pltpu.make_async_copy(ins[w], outs[w].at[mine], local.at[w])

        def push(w, j):
            return _remote(ins[w].at[rows(w, c), :], outs[w].at[mine, rows(w, c), :], send, recv, 3 * w + j,
                           (*chips[j], c))

        def landed(w, j, core):
            px, py = chips[j]
            return outs[w].at[2 * px + py, rows(w, core), :]

        def arrival(w, j):
            return _remote(ins[w].at[rows(w, c), :], landed(w, j, c), send, recv, 3 * w + j, (*chips[j], c))

        def passed(w, j, core):
            return _remote(landed(w, j, core), landed(w, j, core), send, recv, 3 * n + 3 * w + j, (x, y, 1 - c))

        owns = range(n) if into is None else ()
        starts = [lambda w=w: own(w).start() for w in owns] + [lambda w=w, j=j: push(w, j).start() for w, j in pairs]
        mids = [t for w, j in pairs for t in (lambda w=w, j=j: arrival(w, j).wait_recv(),
                                              lambda w=w, j=j: passed(w, j, c).start())]
        waits = ([lambda w=w, j=j: passed(w, j, 1 - c).wait_recv() for w, j in pairs]
                 + [lambda w=w, j=j: passed(w, j, c).wait_send() for w, j in pairs]
                 + [lambda w=w, j=j: push(w, j).wait_send() for w, j in pairs]
                 + [lambda w=w: own(w).wait() for w in owns])
        return starts, mids, waits

    return _Side(shards, [S((N_SHARD,) + s.shape, s.dtype) for s in shards], 6 * n, make, continues=into or ())


def _halves_side(grads):
    n = len(grads)

    def make(ins, outs, send, recv, local):
        x, y, c = _position()

        def copy(w):
            half = ins[w].shape[1] // 2
            return _remote(ins[w].at[:, pl.ds((1 - c) * half, half), :], outs[w], send, recv, w, (x, y, 1 - c))

        return _both_ways(copy, range(n))

    return _Side(grads, [S((N_SHARD, g.shape[1] // 2, g.shape[2]), F32) for g in grads], n, make)


def _scatter_side(partials):
    n = len(partials)

    def make(ins, outs, send, recv, local):
        x, y, c = _position()
        chips = _other_chips(x, y)

        def copy(k):
            w, j = divmod(k, 3)
            px, py = chips[j]
            return _remote(ins[w].at[2 * px + py], outs[w].at[j], send, recv, k, (px, py, c))

        return _both_ways(copy, range(3 * n))

    return _Side(partials, [S((3,) + p.shape[1:], p.dtype) for p in partials], 3 * n, make)


def _swap_side(halves):
    n = len(halves)

    def make(ins, outs, send, recv, local):
        x, y, c = _position()
        return _both_ways(lambda w: _remote(ins[w], outs[w], send, recv, w, (x, y, 1 - c)), range(n))

    return _Side(halves, [S(h.shape, h.dtype) for h in halves], n, make)


N_DEVICE = 8


def _everyone_side(arrays):
    n = len(arrays)
    peers = N_DEVICE - 1

    def make(ins, outs, send, recv, local):
        x, y, c = _position()
        mine = 4 * x + 2 * y + c
        pairs = [(w, k) for w in range(n) for k in range(1, N_DEVICE)]

        def peer(k):
            return (1 - x if k & 4 else x, 1 - y if k & 2 else y, 1 - c if k & 1 else c)

        def own(w):
            return pltpu.make_async_copy(ins[w], outs[w].at[mine], local.at[w])

        def push(w, k):
            return _remote(ins[w], outs[w].at[mine], send, recv, peers * w + k - 1, peer(k))

        def arrival(w, k):
            px, py, pc = peer(k)
            return _remote(ins[w], outs[w].at[4 * px + 2 * py + pc], send, recv, peers * w + k - 1, (px, py, pc))

        starts = [lambda w=w: own(w).start() for w in range(n)] + [lambda w=w, k=k: push(w, k).start() for w, k in pairs]
        waits = ([lambda w=w, k=k: arrival(w, k).wait_recv() for w, k in pairs]
                 + [lambda w=w, k=k: push(w, k).wait_send() for w, k in pairs]
                 + [lambda w=w: own(w).wait() for w in range(n)])
        return starts, [], waits

    return _Side(arrays, [S((N_DEVICE,) + a.shape, a.dtype) for a in arrays], peers * n, make)


def _inproj_own(chip, x, g, w_shard, sides=()):
    T = x.shape[0]
    tm = min(TM_MM, T)

    def body(chip_ref, x_ref, g_ref, w_ref, proj_ref, h_ref):
        xv = x_ref[...]
        h = (xv * _rms_stats(xv) * g_ref[...]).astype(BF16)
        h_ref[...] = h
        proj_ref[...] = _dot(h, w_ref[...])

    return _call(
        body, name="inproj_own", grid=(T // tm,),
        in_specs=[pl.BlockSpec((tm, D), lambda i, c: (i, 0)), pl.BlockSpec((1, D), lambda i, c: (0, 0)),
                  pl.BlockSpec((D, IN_SHARD), lambda i, c: (0, 0), pipeline_mode=pl.Buffered(1))],
        out_specs=[pl.BlockSpec((tm, IN_SHARD), lambda i, c: (i, c[0])), pl.BlockSpec((tm, D), lambda i, c: (i, 0))],
        out_shape=[S((T, IN_COLS), F32), S((T, D), BF16)],
        args=(x, g, w_shard), sides=sides, scalars=chip, mid=1.0)


def _inproj_rest(chip, proj, h, w_in, first, count, name, sides=(), mid=MID_STEP):
    T = h.shape[0]
    tm = min(TM_MM, T)

    def body(chip_ref, _, h_ref, w_ref, proj_ref):
        proj_ref[...] = _dot(h_ref[...], w_ref[0])

    def other(p, c):
        return jnp.bitwise_xor(c[0], first + p)

    return _call(
        body, name=name, grid=(count, T // tm),
        in_specs=[_ANY, pl.BlockSpec((tm, D), lambda p, i, c: (i, 0)),
                  pl.BlockSpec((1, D, IN_SHARD), lambda p, i, c: (other(p, c), 0, 0))],
        out_specs=[pl.BlockSpec((tm, IN_SHARD), lambda p, i, c: (i, other(p, c)))],
        out_shape=[S((T, IN_COLS), F32)],
        args=(proj, h, w_in), sides=sides, scalars=chip, aliases={0: 0}, mid=mid)


def _rg_gates(xc, wa_ref, wx_ref, ba, bx, sp):
    xb = xc.astype(BF16)
    blocks = [xb[:, RG_BLOCK * j:RG_BLOCK * (j + 1)] for j in range(N_RG_BLOCK)]
    r = _sig(jnp.concatenate([_dot(blocks[j], wa_ref[j]) for j in range(N_RG_BLOCK)], axis=1) + ba)
    gi = _sig(jnp.concatenate([_dot(blocks[j], wx_ref[j]) for j in range(N_RG_BLOCK)], axis=1) + bx)
    log_a = (-RG_C) * r * sp
    a = jnp.exp(log_a)
    m = jnp.sqrt(_one_minus_exp(2.0 * log_a))
    return xb, r, gi, a, m


N_GATES = 4
HEADS_PER_BLOCK = 4
HEAD_DIM = RG_BLOCK // HEADS_PER_BLOCK
_RG_BLOCKS_BF16 = pltpu.VMEM((N_RG_BLOCK, RG_BLOCK, RG_BLOCK), BF16)


def _fill_blockdiag(heads_ref, blocks):
    blocks[...] = jnp.zeros_like(blocks)
    for j in range(N_RG_BLOCK):
        for h in range(HEADS_PER_BLOCK):
            sl = slice(HEAD_DIM * h, HEAD_DIM * (h + 1))
            blocks[j, sl, sl] = heads_ref[0, HEADS_PER_BLOCK * j + h].astype(BF16)


def _rglru_fwd(proj, conv_w, conv_b, rg_wa, rg_wx, ba, bx, lam, sides=()):
    T = proj.shape[0]
    tm = min(TM_SCAN, T)

    def body(rx_ref, gate_ref, cw_ref, cb_ref, wah_ref, wxh_ref, ba_ref, bx_ref, lam_ref,
             ya_ref, xc_ref, h_ref, gates_ref, ext, hc, a_s, b_s, h_s, wa_ref, wx_ref):
        @pl.when(pl.program_id(0) == 0)
        def _():
            ext[0:SUBLANES, :] = jnp.zeros((SUBLANES, D), F32)
            hc[...] = jnp.zeros((SUBLANES, D), F32)
            _fill_blockdiag(wah_ref, wa_ref)
            _fill_blockdiag(wxh_ref, wx_ref)

        ext[SUBLANES:SUBLANES + tm, :] = rx_ref[...]
        xc = cb_ref[...]
        for k in range(CONV_WIDTH):
            xc = xc + ext[pl.ds(SUBLANES - (CONV_WIDTH - 1) + k, tm), :] * cw_ref[k:k + 1, :]
        ext[0:SUBLANES, :] = ext[tm:tm + SUBLANES, :]
        xc_ref[...] = xc
        _, r, gi, a, m = _rg_gates(xc, wa_ref, wx_ref, ba_ref[...], bx_ref[...], _softplus_neg(lam_ref[...]))
        for k, val in enumerate((r, gi, a, m)):
            gates_ref[:, D * k:D * (k + 1)] = val
        h = _scan_tile(a, m * (gi * xc), hc[0:1, :], a_s, b_s, h_s, reverse=False)
        hc[...] = jnp.broadcast_to(h[tm - 1:tm, :], (SUBLANES, D))
        h_ref[...] = h
        ya_ref[...] = (_gelu(gate_ref[...]) * h).astype(BF16)

    vec = pl.BlockSpec((1, D), lambda i: (0, 0))
    heads = pl.BlockSpec(rg_wa.shape, lambda i: (0, 0, 0, 0))
    tile = pl.BlockSpec((tm, D), lambda i: (i, 0))
    return _call(
        body, name="rglru_fwd", grid=(T // tm,),
        in_specs=[pl.BlockSpec((tm, D), lambda i: (i, 0)), pl.BlockSpec((tm, D), lambda i: (i, 1)),
                  pl.BlockSpec((CONV_WIDTH, D), lambda i: (0, 0)), vec, heads, heads, vec, vec, vec],
        out_specs=[tile, tile, tile, pl.BlockSpec((tm, N_GATES * D), lambda i: (i, 0))],
        out_shape=[S((T, D), BF16), S((T, D), F32), S((T, D), F32), S((T, N_GATES * D), F32)],
        scratch_shapes=[pltpu.VMEM((tm + SUBLANES, D), F32), pltpu.VMEM((SUBLANES, D), F32)]
        + [pltpu.VMEM((D // LANES, tm, LANES), F32)] * 3 + [_RG_BLOCKS_BF16] * 2,
        args=(proj, proj, conv_w, conv_b, rg_wa, rg_wx, ba, bx, lam), sides=sides)


def _layer_norm_stats(v):
    mu = jnp.mean(v, axis=-1, keepdims=True)
    vc = v - mu
    rstd = lax.rsqrt(jnp.mean(vc * vc, axis=-1, keepdims=True) + EPS)
    return vc * rstd, rstd


def _sgu_mix(w_ref, vnb, bst_ref, n_chunk):
    cols = []
    for g in range(N_GROUP):
        vg = vnb[:, CHUNK * g:CHUNK * (g + 1)].reshape(n_chunk, CHUNK, CHUNK)
        wb = jnp.broadcast_to(w_ref[g][None], (n_chunk, CHUNK, CHUNK))
        mg = lax.dot_general(wb, vg, (((2,), (1,)), ((0,), (0,))), preferred_element_type=F32)
        mg = mg + bst_ref[:, g:g + 1][None]
        cols.append(mg.reshape(n_chunk * CHUNK, CHUNK))
    return jnp.concatenate(cols, axis=1)


def _causal_mask():
    return (lax.broadcasted_iota(jnp.int32, (CHUNK, CHUNK), 0) >= lax.broadcasted_iota(jnp.int32, (CHUNK, CHUNK), 1))


def _fill_sgu_weights(ws_ref, bs_ref, w_tril, bs_t, w_tril_t=None):
    keep = _causal_mask()
    for g in range(N_GROUP):
        wg = jnp.where(keep, ws_ref[0, g], 0.0)
        w_tril[g] = wg.astype(BF16)
        if w_tril_t is not None:
            w_tril_t[g] = wg.T.astype(BF16)
    bs_t[...] = bs_ref[0].T


_SGU_W_BF16 = pltpu.VMEM((N_GROUP, CHUNK, CHUNK), BF16)
_SGU_BT = pltpu.VMEM((CHUNK, N_GROUP), F32)


def _sgu_merge_fwd(x, proj, ya, ln_g, ln_b, sgu_ws, sgu_bs, wpa, wpb, wout):
    T = x.shape[0]
    tm = min(TM_FF, T)
    n_chunk = tm // CHUNK

    def body(x_ref, uv_ref, gab_ref, ya_ref, g_ref, b_ref, ws_ref, bs_ref, wpa_ref, wpb_ref, wout_ref,
             yb_ref, pa_ref, pb_ref, mb_ref, x1_ref, w_ref, bst_ref):
        @pl.when(pl.program_id(0) == 0)
        def _():
            _fill_sgu_weights(ws_ref, bs_ref, w_ref, bst_ref)

        vhat, _ = _layer_norm_stats(_gelu(uv_ref[:, D:2 * D]))
        vnb = (vhat * g_ref[...] + b_ref[...]).astype(BF16)
        yb = (_gelu(uv_ref[:, 0:D]) * _sgu_mix(w_ref, vnb, bst_ref, n_chunk)).astype(BF16)
        yb_ref[...] = yb
        pa = _dot(ya_ref[...], wpa_ref[...])
        pb = _dot(yb, wpb_ref[...])
        pa_ref[...] = pa.astype(BF16)
        pb_ref[...] = pb.astype(BF16)
        mb = (_sig(gab_ref[:, 0:D]) * pa + _sig(gab_ref[:, D:2 * D]) * pb).astype(BF16)
        mb_ref[...] = mb
        x1_ref[...] = x_ref[...] + _dot(mb, wout_ref[...])

    tile = pl.BlockSpec((tm, D), lambda i: (i, 0))
    vec = pl.BlockSpec((1, D), lambda i: (0, 0))
    w = _resident((D, D))
    return pl.pallas_call(
        body, name="sgu_merge_fwd", grid=(T // tm,),
        in_specs=[tile, pl.BlockSpec((tm, 2 * D), lambda i: (i, 1)), pl.BlockSpec((tm, 2 * D), lambda i: (i, 2)), tile,
                  vec, vec, pl.BlockSpec(sgu_ws.shape, lambda i: (0, 0, 0, 0)),
                  pl.BlockSpec(sgu_bs.shape, lambda i: (0, 0, 0)), w, w, w],
        out_specs=[tile, tile, tile, tile, tile],
        out_shape=[S((T, D), BF16), S((T, D), BF16), S((T, D), BF16), S((T, D), BF16), S((T, D), F32)],
        scratch_shapes=[_SGU_W_BF16, _SGU_BT],
        compiler_params=_params(1),
    )(x, proj, proj, ya, ln_g, ln_b, sgu_ws, sgu_bs, wpa, wpb, wout)


def _ffn_fwd_loss(x1, g, w_gu, w_down, g_final, target):
    T = x1.shape[0]
    tm = min(TM_FF, T)

    def body(x_ref, g_ref, wgu_ref, wd_ref, gf_ref, t_ref,
             h2_ref, gu_ref, act_ref, dx2_ref, dx2b_ref, loss_ref, dg_ref):
        @pl.when(pl.program_id(0) == 0)
        def _():
            loss_ref[...] = jnp.zeros_like(loss_ref)
            dg_ref[...] = jnp.zeros_like(dg_ref)

        xv = x_ref[...]
        h2 = (xv * _rms_stats(xv) * g_ref[...]).astype(BF16)
        h2_ref[...] = h2
        x2 = xv
        for k in range(N_SHARD // 2):
            cols = slice(FF_SHARD * k, FF_SHARD * (k + 1))
            gate = _dot(h2, wgu_ref[k])
            up = _dot(h2, wgu_ref[k + N_SHARD // 2])
            gu_ref[:, cols] = gate.astype(BF16)
            gu_ref[:, D_FF + FF_SHARD * k:D_FF + FF_SHARD * (k + 1)] = up.astype(BF16)
            act = (gate * _sig(gate) * up).astype(BF16)
            act_ref[:, cols] = act
            x2 = x2 + _dot(act, wd_ref[cols, :])
        gf = gf_ref[...]
        err = x2 * _rms_stats(x2) * gf - t_ref[...]
        loss_ref[...] += 0.5 * jnp.sum(jnp.mean(err * err, axis=-1, keepdims=True), axis=0, keepdims=True)
        dx2, dg_rows = _rms_bwd(err * (1.0 / D), x2, gf)
        dg_ref[...] += _colsum(dg_rows)
        dx2_ref[...] = dx2
        dx2b_ref[...] = dx2.astype(BF16)

    tile = pl.BlockSpec((tm, D), lambda i: (i, 0))
    vec = pl.BlockSpec((1, D), lambda i: (0, 0))
    return pl.pallas_call(
        body, name="ffn_fwd_loss", grid=(T // tm,),
        in_specs=[tile, vec, _resident((N_SHARD, D, FF_SHARD)), _resident((D_FF, D)), vec, tile],
        out_specs=[tile, pl.BlockSpec((tm, 2 * D_FF), lambda i: (i, 0)), pl.BlockSpec((tm, D_FF), lambda i: (i, 0)),
                   tile, tile, pl.BlockSpec((1, 1), lambda i: (0, 0)), vec],
        out_shape=[S((T, D), BF16), S((T, 2 * D_FF), BF16), S((T, D_FF), BF16), S((T, D), F32), S((T, D), BF16),
                   S((1, 1), F32), S((1, D), F32)],
        compiler_params=_params(1),
    )(x1, g, w_gu, w_down, g_final, target)


def _ffn_bwd(dx2, dx2b, gu, w_down, w_gu, x1, g):
    T = x1.shape[0]
    tm = min(TM_FF, T)

    def body(dx2_ref, dx2b_ref, gu_ref, wd_ref, wgu_ref, x_ref, g_ref, dgu_ref, dx1_ref, dx1b_ref, dg_ref):
        @pl.when(pl.program_id(0) == 0)
        def _():
            dg_ref[...] = jnp.zeros_like(dg_ref)

        dxb = dx2b_ref[...]
        dh2 = jnp.zeros((tm, D), F32)
        for k in range(N_SHARD // 2):
            cols = slice(FF_SHARD * k, FF_SHARD * (k + 1))
            up_cols = slice(D_FF + FF_SHARD * k, D_FF + FF_SHARD * (k + 1))
            dact = _dot_nt(dxb, wd_ref[cols, :])
            gate = gu_ref[:, cols].astype(F32)
            sg = _sig(gate)
            dgate = (dact * gu_ref[:, up_cols].astype(F32) * (sg * (1.0 + gate * (1.0 - sg)))).astype(BF16)
            dup = (dact * (gate * sg)).astype(BF16)
            dgu_ref[:, cols] = dgate
            dgu_ref[:, up_cols] = dup
            dh2 = dh2 + _dot_nt(dgate, wgu_ref[k]) + _dot_nt(dup, wgu_ref[k + N_SHARD // 2])
        dx, dg_rows = _rms_bwd(dh2, x_ref[...], g_ref[...])
        dg_ref[...] += _colsum(dg_rows)
        dx1 = dx2_ref[...] + dx
        dx1_ref[...] = dx1
        dx1b_ref[...] = dx1.astype(BF16)

    tile = pl.BlockSpec((tm, D), lambda i: (i, 0))
    wide = pl.BlockSpec((tm, 2 * D_FF), lambda i: (i, 0))
    vec = pl.BlockSpec((1, D), lambda i: (0, 0))
    return pl.pallas_call(
        body, name="ffn_bwd", grid=(T // tm,),
        in_specs=[tile, tile, wide, _resident((D_FF, D)), _resident((N_SHARD, D, FF_SHARD)), tile, vec],
        out_specs=[wide, tile, tile, vec],
        out_shape=[S((T, 2 * D_FF), BF16), S((T, D), F32), S((T, D), BF16), S((1, D), F32)],
        compiler_params=_params(1),
    )(dx2, dx2b, gu, w_down, w_gu, x1, g)


def _matmul_tn(a, b, tn, shard_major, name, sides=()):
    T, M = a.shape
    N = b.shape[1]
    tk = min(TK_WGRAD, T)

    def body(a_ref, b_ref, o_ref):
        @pl.when(pl.program_id(1) == 0)
        def _():
            o_ref[...] = jnp.zeros_like(o_ref)

        acc = _dot_tn(a_ref[...], b_ref[...])
        if shard_major:
            o_ref[0] += acc
        else:
            o_ref[...] += acc

    if shard_major:
        out_spec, out_shape = pl.BlockSpec((1, M, tn), lambda j, k: (j, 0, 0)), S((N // tn, M, tn), F32)
    else:
        out_spec, out_shape = pl.BlockSpec((M, tn), lambda j, k: (0, j)), S((M, N), F32)
    (out,), side_outs = _call(
        body, name=name, grid=(N // tn, T // tk),
        in_specs=[pl.BlockSpec((tk, M), lambda j, k: (k, 0)), pl.BlockSpec((tk, tn), lambda j, k: (k, j))],
        out_specs=[out_spec], out_shape=[out_shape], args=(a, b), sides=sides)
    return out, side_outs


PIECE = IN_SHARD // 3
N_PIECE = IN_COLS // PIECE
DPROJ_ROTATION = 2 * D // PIECE


def _merge_sgu_bwd(dx1b, proj, pa, pb, ln_g, ln_b, sgu_ws, sgu_bs, wpa, wpb, wout, sides=()):
    T = dx1b.shape[0]
    tm = min(TM_FF, T)
    n_chunk = tm // CHUNK

    def body(dx_ref, uv_ref, gab_ref, pa_ref, pb_ref, g_ref, b_ref, ws_ref, bs_ref, wpa_ref, wpb_ref, wout_ref,
             dpa_ref, dpb_ref, dya_ref, dp_ref, dw_ref, dbs_ref, dg_ref, db_ref, w_ref, wt_ref, bst_ref):
        @pl.when(pl.program_id(0) == 0)
        def _():
            for ref in (dw_ref, dbs_ref, dg_ref, db_ref):
                ref[...] = jnp.zeros_like(ref)
            _fill_sgu_weights(ws_ref, bs_ref, w_ref, bst_ref, wt_ref)

        dm = _dot_nt(dx_ref[...], wout_ref[...])
        sa = _sig(gab_ref[:, 0:D])
        sb = _sig(gab_ref[:, D:2 * D])
        dpa = (dm * sa).astype(BF16)
        dpb = (dm * sb).astype(BF16)
        dpa_ref[...] = dpa
        dpb_ref[...] = dpb
        dp_ref[:, 2 * D:3 * D] = (dm * pa_ref[...].astype(F32) * (sa * (1.0 - sa))).astype(BF16)
        dp_ref[:, 3 * D:4 * D] = (dm * pb_ref[...].astype(F32) * (sb * (1.0 - sb))).astype(BF16)
        dya_ref[...] = _dot_nt(dpa, wpa_ref[...]).astype(BF16)
        dyb_v = _dot_nt(dpb, wpb_ref[...])

        gu, dgu = _gelu_and_grad(uv_ref[:, 0:D])
        gv, dgv = _gelu_and_grad(uv_ref[:, D:2 * D])
        vhat, rstd = _layer_norm_stats(gv)
        lng = g_ref[...]
        vnb = (vhat * lng + b_ref[...]).astype(BF16)
        mixed = _sgu_mix(w_ref, vnb, bst_ref, n_chunk)
        dp_ref[:, 0:D] = (dyb_v * mixed * dgu).astype(BF16)
        dmix = dyb_v * gu
        dmb = dmix.astype(BF16)
        keep = _causal_mask()
        dvn_cols, dbs_rows = [], []
        for g in range(N_GROUP):
            sl = slice(CHUNK * g, CHUNK * (g + 1))
            dmg = dmb[:, sl].reshape(n_chunk, CHUNK, CHUNK)
            vg = vnb[:, sl].reshape(n_chunk, CHUNK, CHUNK)
            wtb = jnp.broadcast_to(wt_ref[g][None], (n_chunk, CHUNK, CHUNK))
            dvn = lax.dot_general(wtb, dmg, (((2,), (1,)), ((0,), (0,))), preferred_element_type=F32)
            dvn_cols.append(dvn.reshape(tm, CHUNK))
            dw = lax.dot_general(dmg, vg, (((2,), (2,)), ((0,), (0,))), preferred_element_type=F32)
            dw_ref[g] += jnp.where(keep, jnp.sum(dw, axis=0), 0.0)
            per_token = jnp.sum(dmix[:, sl], axis=1)
            dbs_rows.append(jnp.sum(per_token.reshape(n_chunk, CHUNK), axis=0, keepdims=True))
        dbs_ref[...] += jnp.concatenate(dbs_rows, axis=0)
        dvn = jnp.concatenate(dvn_cols, axis=1)
        dg_ref[...] += _colsum(dvn * vhat)
        db_ref[...] += _colsum(dvn)
        dvhat = dvn * lng
        dgv_in = rstd * (dvhat - jnp.mean(dvhat, axis=-1, keepdims=True)
                         - vhat * jnp.mean(dvhat * vhat, axis=-1, keepdims=True))
        dp_ref[:, D:2 * D] = (dgv_in * dgv).astype(BF16)

    tile = pl.BlockSpec((tm, D), lambda i: (i, 0))
    vec = pl.BlockSpec((1, D), lambda i: (0, 0))
    w = _resident((D, D))
    wsp = pl.BlockSpec((N_GROUP, CHUNK, CHUNK), lambda i: (0, 0, 0))
    return _call(
        body, name="merge_sgu_bwd", grid=(T // tm,),
        in_specs=[tile, pl.BlockSpec((tm, 2 * D), lambda i: (i, 1)), pl.BlockSpec((tm, 2 * D), lambda i: (i, 2)),
                  tile, tile, vec, vec, pl.BlockSpec(sgu_ws.shape, lambda i: (0, 0, 0, 0)),
                  pl.BlockSpec(sgu_bs.shape, lambda i: (0, 0, 0)), w, w, w],
        out_specs=[tile, tile, tile, pl.BlockSpec((tm, 4 * D), lambda i: (i, 0)), wsp,
                   pl.BlockSpec((N_GROUP, CHUNK), lambda i: (0, 0)), vec, vec],
        out_shape=[S((T, D), BF16), S((T, D), BF16), S((T, D), BF16), S((T, IN_COLS), BF16),
                   S((N_GROUP, CHUNK, CHUNK), F32), S((N_GROUP, CHUNK), F32), S((1, D), F32), S((1, D), F32)],
        scratch_shapes=[_SGU_W_BF16, _SGU_W_BF16, _SGU_BT],
        args=(dx1b, proj, proj, pa, pb, ln_g, ln_b, sgu_ws, sgu_bs, wpa, wpb, wout), sides=sides)


def _rglru_bwd(dya, dproj, proj, hseq, xc, gates, conv_w, rg_wa, rg_wx, lam, sides=()):
    T = dya.shape[0]
    tm = min(TM_SCAN, T)
    n = T // tm
    per8 = tm // SUBLANES

    def body(dya_ref, _, rx_ref, rxp_ref, gate_ref, h_ref, hp_ref, xc_ref, gates_ref, cw_ref, wah_ref, wxh_ref,
             lam_ref, dab_ref, dcw_ref, dcb_ref, dwah_ref, dwxh_ref, dba_ref, dbx_ref, dlam_ref,
             hext, rext, dext, carry_a, carry_dh, a_s, b_s, h_s, wa_ref, wx_ref, dwa_ref, dwx_ref):
        i = pl.program_id(0)
        first_tile = i == n - 1

        @pl.when(i == 0)
        def _():
            for ref in (dcw_ref, dcb_ref, dwa_ref, dwx_ref, dba_ref, dbx_ref, dlam_ref, carry_a, carry_dh):
                ref[...] = jnp.zeros_like(ref)
            dext[tm:tm + SUBLANES, :] = jnp.zeros((SUBLANES, D), F32)
            _fill_blockdiag(wah_ref, wa_ref)
            _fill_blockdiag(wxh_ref, wx_ref)

        gel, dgel = _gelu_and_grad(gate_ref[...])
        dya_v = dya_ref[...].astype(F32)
        hseq_v = h_ref[...]
        dgate = dya_v * hseq_v * dgel
        xcv = xc_ref[...]
        lam_v = lam_ref[...]
        sp = _softplus_neg(lam_v)
        xb = xcv.astype(BF16)
        r, gi, a, m = (gates_ref[:, D * k:D * (k + 1)] for k in range(N_GATES))

        row = lax.broadcasted_iota(jnp.int32, (tm, D), 0)
        c = jnp.where(row == tm - 1, carry_a[0:1, :], _shift_up(a, 1, 0.0))
        dH = _scan_tile(c, dya_v * gel, carry_dh[0:1, :], a_s, b_s, h_s, reverse=True)
        carry_a[...] = jnp.broadcast_to(a[0:1, :], (SUBLANES, D))
        carry_dh[...] = jnp.broadcast_to(dH[0:1, :], (SUBLANES, D))

        hext[0:SUBLANES, :] = jnp.where(first_tile, 0.0, hp_ref[...])
        hext[SUBLANES:SUBLANES + tm, :] = hseq_v
        h_prev = hext[pl.ds(SUBLANES - 1, tm), :]

        d_m = dH * (gi * xcv)
        d_la = dH * h_prev * a - d_m * (a * a) / m
        d_ia = dH * m * xcv * (gi * (1.0 - gi))
        d_ra = d_la * ((-RG_C) * sp) * (r * (1.0 - r))
        dlam_ref[...] += _colsum(d_la * ((-RG_C) * r)) * (-_sig(-lam_v))
        dba_ref[...] += _colsum(d_ra)
        dbx_ref[...] += _colsum(d_ia)
        drab = d_ra.astype(BF16)
        diab = d_ia.astype(BF16)
        dxc_cols = []
        for j in range(N_RG_BLOCK):
            sl = slice(RG_BLOCK * j, RG_BLOCK * (j + 1))
            dxc_cols.append(_dot_nt(drab[:, sl], wa_ref[j]) + _dot_nt(diab[:, sl], wx_ref[j]))
            dwa_ref[j] += _dot_tn(xb[:, sl], drab[:, sl])
            dwx_ref[j] += _dot_tn(xb[:, sl], diab[:, sl])
        dxc = dH * m * gi + jnp.concatenate(dxc_cols, axis=1)

        dcb_ref[...] += _colsum(dxc)
        dext[0:tm, :] = dxc
        rext[0:SUBLANES, :] = jnp.where(first_tile, 0.0, rxp_ref[...])
        rext[SUBLANES:SUBLANES + tm, :] = rx_ref[...]
        drx = jnp.zeros((tm, D), F32)
        for k in range(CONV_WIDTH):
            drx = drx + dext[pl.ds(CONV_WIDTH - 1 - k, tm), :] * cw_ref[k:k + 1, :]
            dcw_ref[k:k + 1, :] += _colsum(dxc * rext[pl.ds(SUBLANES - (CONV_WIDTH - 1) + k, tm), :])
        dext[tm:tm + SUBLANES, :] = dext[0:SUBLANES, :]
        dab_ref[:, 0:D] = drx.astype(BF16)
        dab_ref[:, D:2 * D] = dgate.astype(BF16)

        @pl.when(first_tile)
        def _():
            for j in range(N_RG_BLOCK):
                for h in range(HEADS_PER_BLOCK):
                    sl = slice(HEAD_DIM * h, HEAD_DIM * (h + 1))
                    pair, side = divmod(HEADS_PER_BLOCK * j + h, 2)
                    lanes = slice(HEAD_DIM * side, HEAD_DIM * (side + 1))
                    dwah_ref[pair, :, lanes] = dwa_ref[j, sl, sl]
                    dwxh_ref[pair, :, lanes] = dwx_ref[j, sl, sl]

    def rev(col):
        return lambda i: (n - 1 - i, col)

    def prev8(col):
        return lambda i: (jnp.maximum((n - 1 - i) * per8 - 1, 0), col)

    tile = pl.BlockSpec((tm, D), rev(0))
    vec = pl.BlockSpec((1, D), lambda i: (0, 0))
    heads_in = pl.BlockSpec(rg_wa.shape, lambda i: (0, 0, 0, 0))
    head_pairs = (rg_wa.shape[1] // 2, HEAD_DIM, 2 * HEAD_DIM)
    heads_out = pl.BlockSpec(head_pairs, lambda i: (0, 0, 0))
    cw = pl.BlockSpec((CONV_WIDTH, D), lambda i: (0, 0))
    blocks_f32 = pltpu.VMEM((N_RG_BLOCK, RG_BLOCK, RG_BLOCK), F32)
    return _call(
        body, name="rglru_bwd", grid=(n,),
        in_specs=[tile, _ANY, pl.BlockSpec((tm, D), rev(0)), pl.BlockSpec((SUBLANES, D), prev8(0)),
                  pl.BlockSpec((tm, D), rev(1)), tile, pl.BlockSpec((SUBLANES, D), prev8(0)), tile,
                  pl.BlockSpec((tm, N_GATES * D), rev(0)), cw, heads_in, heads_in, vec],
        out_specs=[pl.BlockSpec((tm, 2 * D), rev(2)), cw, vec, heads_out, heads_out, vec, vec, vec],
        out_shape=[S((T, IN_COLS), BF16), S((CONV_WIDTH, D), F32), S((1, D), F32),
                   S(head_pairs, F32), S(head_pairs, F32), S((1, D), F32), S((1, D), F32), S((1, D), F32)],
        scratch_shapes=[pltpu.VMEM((tm + SUBLANES, D), F32), pltpu.VMEM((tm + SUBLANES, D), F32),
                        pltpu.VMEM((tm + SUBLANES, D), F32), pltpu.VMEM((SUBLANES, D), F32),
                        pltpu.VMEM((SUBLANES, D), F32)] + [pltpu.VMEM((D // LANES, tm, LANES), F32)] * 3
        + [_RG_BLOCKS_BF16] * 2 + [blocks_f32] * 2,
        args=(dya, dproj, proj, proj, proj, hseq, hseq, xc, gates, conv_w, rg_wa, rg_wx, lam), sides=sides,
        aliases={1: 0})


def _inproj_dh(dproj, w_in, dh, first, count, name, sides=()):
    T = dproj.shape[0]
    tm = min(TM_MM, T)

    def body(*refs):
        dp_ref, w_ref, dh_ref = refs[-3:]
        dh = jnp.zeros((tm, D), F32)
        for p in range(N_PIECE):
            shard, part = divmod((p + DPROJ_ROTATION) % N_PIECE, IN_SHARD // PIECE)
            dh = dh + _dot_nt(dp_ref[:, PIECE * p:PIECE * (p + 1)], w_ref[shard, :, PIECE * part:PIECE * (part + 1)])
        dh_ref[...] = dh

    earlier = [] if dh is None else [dh]
    return _call(
        body, name=name, grid=(count,),
        in_specs=[_ANY] * len(earlier) + [pl.BlockSpec((tm, IN_COLS), lambda i: (first + i, 0)),
                                         _resident((N_SHARD, D, IN_SHARD))],
        out_specs=[pl.BlockSpec((tm, D), lambda i: (first + i, 0))],
        out_shape=[S((T, D), F32)],
        args=(*earlier, dproj, w_in), sides=sides, aliases={0: 0} if earlier else None)


def _inproj_norm_bwd(dh, x, g, dx1):
    T = x.shape[0]
    tm = min(TM_MM, T)

    def body(dh_ref, x_ref, g_ref, dx1_ref, dx_ref, dgm_ref):
        @pl.when(pl.program_id(0) == 0)
        def _():
            dgm_ref[...] = jnp.zeros_like(dgm_ref)

        dx, dg_rows = _rms_bwd(dh_ref[...], x_ref[...], g_ref[...])
        dgm_ref[...] += _colsum(dg_rows)
        dx_ref[...] = dx1_ref[...] + dx

    tile = pl.BlockSpec((tm, D), lambda i: (i, 0))
    vec = pl.BlockSpec((1, D), lambda i: (0, 0))
    return pl.pallas_call(
        body, name="inproj_norm_bwd", grid=(T // tm,),
        in_specs=[tile, tile, vec, tile],
        out_specs=[tile, vec],
        out_shape=[S((T, D), F32), S((1, D), F32)],
        compiler_params=_params(1),
    )(dh, x, g, dx1)


def _inproj_wgrad(h, dproj, sides=()):
    T = h.shape[0]
    tk = min(TK_WGRAD, T)
    per = IN_SHARD // PIECE

    def body(h_ref, *refs):
        pieces, o_ref = refs[:per], refs[per]

        @pl.when(pl.program_id(1) == 0)
        def _():
            o_ref[...] = jnp.zeros_like(o_ref)

        o_ref[0] += _dot_tn(h_ref[...], jnp.concatenate([p[...] for p in pieces], axis=1))

    def piece(i):
        return pl.BlockSpec((tk, PIECE), lambda j, k: (k, (per * j + i + N_PIECE - DPROJ_ROTATION) % N_PIECE))

    (out,), side_outs = _call(
        body, name="inproj_wgrad", grid=(N_SHARD, T // tk),
        in_specs=[pl.BlockSpec((tk, D), lambda j, k: (k, 0))] + [piece(i) for i in range(per)],
        out_specs=[pl.BlockSpec((1, D, IN_SHARD), lambda j, k: (j, 0, 0))],
        out_shape=[S((N_SHARD, D, IN_SHARD), F32)], args=(h,) + (dproj,) * per, sides=sides)
    return out, side_outs


def _row_tile(rows):
    for t in range(256, 0, -SUBLANES):
        if rows % t == 0:
            return t
    raise ValueError(rows)


def _add_halves(core, g, theirs, name):
    _, r, cols = g.shape
    half = r // 2
    tr = _row_tile(half)
    nb = half // tr

    def body(core_ref, g_ref, t_ref, o_ref):
        o_ref[...] = (g_ref[...] + t_ref[...]).astype(BF16)

    blk = pl.BlockSpec((1, tr, cols), lambda s, i, core_ref: (s, i, 0))
    gs = pltpu.PrefetchScalarGridSpec(
        num_scalar_prefetch=1, grid=(N_SHARD, nb),
        in_specs=[pl.BlockSpec((1, tr, cols), lambda s, i, core_ref: (s, core_ref[0] * nb + i, 0)), blk],
        out_specs=blk)
    return pl.pallas_call(
        body, name=name, grid_spec=gs, out_shape=S((N_SHARD, half, cols), BF16), compiler_params=_params(2),
    )(core, g, theirs)


def _sum_shards(chip, own, others, name):
    _, half, cols = own.shape
    tr = _row_tile(half)

    def body(chip_ref, own_ref, oth_ref, o_ref):
        acc = own_ref[0].astype(F32)
        for j in range(3):
            acc = acc + oth_ref[j].astype(F32)
        o_ref[...] = acc

    gs = pltpu.PrefetchScalarGridSpec(
        num_scalar_prefetch=1, grid=(half // tr,),
        in_specs=[pl.BlockSpec((1, tr, cols), lambda i, chip_ref: (chip_ref[0], i, 0)),
                  pl.BlockSpec((3, tr, cols), lambda i, chip_ref: (0, i, 0))],
        out_specs=pl.BlockSpec((tr, cols), lambda i, chip_ref: (i, 0)))
    return pl.pallas_call(
        body, name=name, grid_spec=gs, out_shape=S((half, cols), F32), compiler_params=_params(1),
    )(chip, own, others)


def _adamw(w, g, m, v):
    m = ADAM_B1 * m + (1.0 - ADAM_B1) * g
    v = ADAM_B2 * v + (1.0 - ADAM_B2) * (g * g)
    m_hat = m / (1.0 - ADAM_B1 ** ADAM_STEP)
    v_hat = v / (1.0 - ADAM_B2 ** ADAM_STEP)
    delta = -ADAM_LR * (m_hat / (jnp.sqrt(v_hat) + ADAM_EPS) + ADAM_WD * w)
    return delta, m, v


def _adamw_shard(core, mine, theirs, w, m, v, name):
    r, cols = w.shape
    half = r // 2
    tr = _row_tile(half)
    nb = half // tr

    def body(core_ref, mine_ref, theirs_ref, w_ref, m_ref, v_ref, g_ref, d_ref, mo_ref, vo_ref):
        g = jnp.where(pl.program_id(0) == core_ref[0], mine_ref[...], theirs_ref[...])
        g_ref[...] = g
        d_ref[...], mo_ref[...], vo_ref[...] = _adamw(w_ref[...], g, m_ref[...], v_ref[...])

    hblk = pl.BlockSpec((tr, cols), lambda h, i, core_ref: (i, 0))
    blk = pl.BlockSpec((tr, cols), lambda h, i, core_ref: (h * nb + i, 0))
    gs = pltpu.PrefetchScalarGridSpec(num_scalar_prefetch=1, grid=(2, nb),
                                      in_specs=[hblk, hblk, blk, blk, blk], out_specs=[blk] * 4)
    return pl.pallas_call(
        body, name=name, grid_spec=gs, out_shape=[S((r, cols), F32)] * 4, compiler_params=_params(2),
    )(core, mine, theirs, w, m, v)


def _adamw_whole(w, g, m, v, name):
    def body(w_ref, g_ref, m_ref, v_ref, d_ref, mo_ref, vo_ref):
        d_ref[...], mo_ref[...], vo_ref[...] = _adamw(w_ref[...], g_ref[...], m_ref[...], v_ref[...])

    return pl.pallas_call(body, name=name, out_shape=[S(w.shape, F32)] * 3)(w, g, m, v)


_VEC_ROWS = ("norm_mix_g", "conv_b", "rg_lambda", "sgu_ln_g", "sgu_ln_b", "norm_ffn_g", "norm_final_g", "rg_ba",
             "rg_bx")
_CONV_ROW = len(_VEC_ROWS)
_LOSS_ROW = _CONV_ROW + CONV_WIDTH
_VEC_PAD = -(_LOSS_ROW + 1) % SUBLANES
_HEAD_BIASES = ("rg_ba", "rg_bx")
_TENSORS = ("sgu_bs", "sgu_ws", "rg_wa", "rg_wx")
_HEAD_PAIRS = ("rg_wa", "rg_wx")


def _small_sum_adamw(parts, w, m, v):
    names = [n for n in _VEC_ROWS] + list(_TENSORS)
    n_parts = len(parts)

    def total(ref):
        acc = ref[0]
        for k in range(1, N_DEVICE):
            acc = acc + ref[k]
        return acc

    def body(*refs):
        part_refs, refs = refs[:n_parts], refs[n_parts:]
        w_refs, m_refs, v_refs = (dict(zip(names, refs[k * len(names):(k + 1) * len(names)])) for k in range(3))
        outs = refs[3 * len(names):]
        out_refs = {n: outs[4 * k:4 * k + 4] for k, n in enumerate(names)}
        conv_ref, loss_ref = outs[4 * len(names):]
        vec = total(part_refs[0])
        grads = {n: total(p) for n, p in zip(_TENSORS, part_refs[1:])}
        for n in _HEAD_PAIRS:
            pairs = grads[n]
            grads[n] = jnp.stack([pairs[k // 2, :, HEAD_DIM * (k % 2):HEAD_DIM * (k % 2 + 1)]
                                  for k in range(2 * pairs.shape[0])], axis=0)
        grads = {n: g[None] for n, g in grads.items()}
        for row, n in enumerate(_VEC_ROWS):
            g = vec[row:row + 1, :]
            if n in _HEAD_BIASES:
                g = jnp.concatenate([g[:, HEAD_DIM * h:HEAD_DIM * (h + 1)] for h in range(D // HEAD_DIM)], axis=0)[None]
            grads[n] = g
        for n in names:
            g_ref, d_ref, mo_ref, vo_ref = out_refs[n]
            g_ref[...] = grads[n]
            d_ref[...], mo_ref[...], vo_ref[...] = _adamw(w_refs[n][...], grads[n], m_refs[n][...], v_refs[n][...])
        conv_ref[...] = vec[_CONV_ROW:_CONV_ROW + CONV_WIDTH, :]
        loss_ref[...] = vec[_LOSS_ROW:_LOSS_ROW + 1, 0:1]

    res = pl.pallas_call(
        body, name="small_sum_adamw",
        out_shape=[S(w[n].shape, F32) for n in names for _ in range(4)] + [S((CONV_WIDTH, D), F32), S((1, 1), F32)],
        compiler_params=pltpu.CompilerParams(vmem_limit_bytes=VMEM_LIMIT),
    )(*parts, *[w[n] for n in names], *[m[n] for n in names], *[v[n] for n in names])
    return {n: tuple(res[4 * k:4 * k + 4]) for k, n in enumerate(names)}, res[-2], res[-1]


_BIG = ("w_in", "w_proj_a", "w_proj_b", "w_out", "w_gate_up", "w_down")
_WEIGHTS = ("norm_mix_g", "w_in", "conv_w", "conv_b", "rg_wa", "rg_ba", "rg_wx", "rg_bx", "rg_lambda", "sgu_ln_g",
            "sgu_ln_b", "sgu_ws", "sgu_bs", "w_proj_a", "w_proj_b", "w_out", "norm_ffn_g", "w_gate_up", "w_down",
            "norm_final_g")


def kernel(x, norm_mix_g, w_in, conv_w, conv_b, rg_wa, rg_ba, rg_wx, rg_bx, rg_lambda, sgu_ln_g, sgu_ln_b, sgu_ws, sgu_bs, w_proj_a, w_proj_b, w_out, norm_ffn_g, w_gate_up, w_down, norm_final_g, loss_target, m_norm_mix_g, m_w_in, m_conv_w, m_conv_b, m_rg_wa, m_rg_ba, m_rg_wx, m_rg_bx, m_rg_lambda, m_sgu_ln_g, m_sgu_ln_b, m_sgu_ws, m_sgu_bs, m_w_proj_a, m_w_proj_b, m_w_out, m_norm_ffn_g, m_w_gate_up, m_w_down, m_norm_final_g, v_norm_mix_g, v_w_in, v_conv_w, v_conv_b, v_rg_wa, v_rg_ba, v_rg_wx, v_rg_bx, v_rg_lambda, v_sgu_ln_g, v_sgu_ln_b, v_sgu_ws, v_sgu_bs, v_w_proj_a, v_w_proj_b, v_w_out, v_norm_ffn_g, v_w_gate_up, v_w_down, v_norm_final_g):
    args = dict(locals())
    w = {n: args[n] for n in _WEIGHTS}
    mom = {n: args["m_" + n] for n in _WEIGHTS}
    var = {n: args["v_" + n] for n in _WEIGHTS}
    xi, yi, ci = _position()
    core = ci.astype(jnp.int32).reshape(1)
    chip = (2 * xi + yi).astype(jnp.int32).reshape(1)

    bf = {n: w[n][0].astype(BF16) for n in _BIG}
    final_g = w["norm_final_g"].reshape(1, D)
    ba, bx = w["rg_ba"].reshape(1, D), w["rg_bx"].reshape(1, D)
    lam, ln_g, ln_b = w["rg_lambda"], w["sgu_ln_g"], w["sgu_ln_b"]
    x0, target = x[0], loss_target[0]

    def shard_major(g):
        return g.reshape(N_SHARD, g.shape[0] // N_SHARD, g.shape[1])

    def chip_sums(names, grads, theirs):
        return [_add_halves(core, g, t, "add_halves_" + n) for n, g, t in zip(names, grads, theirs)]

    def my_halves(names, sums, arrived):
        return [_sum_shards(chip, p, a, "sum_shards_" + n) for n, p, a in zip(names, sums, arrived)]

    (proj, h), ((w_in_a,), (conv_a,)) = _inproj_own(
        chip, x0, w["norm_mix_g"], bf["w_in"],
        sides=[_gather_half_side([bf["w_in"]], relations=(0, 1)), _gather_side([w["conv_w"][0]])])
    conv_cols = conv_a.shape[-1]
    conv_full = jnp.swapaxes(conv_a, 0, 1).reshape(CONV_WIDTH, D)
    (proj,), ((w_in_a,), (w_pa_a, w_pb_a, w_out_a)) = _inproj_rest(
        chip, proj, h, w_in_a, 1, 2, "inproj_near", mid=1.0,
        sides=[_gather_half_side([bf["w_in"]], relations=(2,), into=[w_in_a]),
               _gather_half_side([bf["w_proj_a"], bf["w_proj_b"], bf["w_out"]])])
    (proj,), _ = _inproj_rest(chip, proj, h, w_in_a, 3, 1, "inproj_far")
    (ya, xc, hseq, gates), ((w_gu_a, w_down_a),) = _rglru_fwd(
        proj, conv_full, w["conv_b"], w["rg_wa"], w["rg_wx"], ba, bx, lam,
        sides=[_gather_half_side([bf["w_gate_up"], bf["w_down"]])])
    wpa, wpb, wout, wdown = w_pa_a.reshape(D, D), w_pb_a.reshape(D, D), w_out_a.reshape(D, D), w_down_a.reshape(D_FF, D)
    yb, pa, pb, mb, x1 = _sgu_merge_fwd(x0, proj, ya, ln_g, ln_b, w["sgu_ws"], w["sgu_bs"], wpa, wpb, wout)
    h2, gu, act, dx2, dx2b, loss, d_final_g = _ffn_fwd_loss(x1, w["norm_ffn_g"], w_gu_a, wdown, final_g, target)

    dgu, dx1, dx1b, d_ffn_g = _ffn_bwd(dx2, dx2b, gu, wdown, w_gu_a, x1, w["norm_ffn_g"])
    ffn = ("w_gate_up", "w_down")
    g_ffn = [_matmul_tn(h2, dgu, FF_SHARD, True, "wgrad_gate_up")[0],
             shard_major(_matmul_tn(act, dx2b, D // 2, False, "wgrad_down")[0])]
    (dpa, dpb, dya, dproj, d_ws, d_bs, d_lng, d_lnb), (theirs_ffn,) = _merge_sgu_bwd(
        dx1b, proj, pa, pb, ln_g, ln_b, w["sgu_ws"], w["sgu_bs"], wpa, wpb, wout, sides=[_halves_side(g_ffn)])
    sums_ffn = chip_sums(ffn, g_ffn, theirs_ffn)
    mix = ("w_proj_a", "w_proj_b", "w_out")
    g_mix = [shard_major(_matmul_tn(ya, dpa, D, False, "wgrad_proj_a")[0]),
             shard_major(_matmul_tn(yb, dpb, D, False, "wgrad_proj_b")[0]),
             shard_major(_matmul_tn(mb, dx1b, D, False, "wgrad_out")[0])]
    (dproj, d_cw, d_cb, d_wa, d_wx, d_ba, d_bx, d_lam), (arrived_ffn, theirs_mix, sgu_parts) = _rglru_bwd(
        dya, dproj, proj, hseq, xc, gates, conv_full, w["rg_wa"], w["rg_wx"], lam,
        sides=[_scatter_side(sums_ffn), _halves_side(g_mix), _everyone_side([d_bs, d_ws])])
    mine_ffn = my_halves(ffn, sums_ffn, arrived_ffn)
    sums_mix = chip_sums(mix, g_mix, theirs_mix)
    g_in, ((wa_parts, wx_parts), other_ffn, arrived_mix) = _inproj_wgrad(
        h, dproj, sides=[_everyone_side([d_wa, d_wx]), _swap_side(mine_ffn),
                         _scatter_side(sums_mix)])
    mine_mix = my_halves(mix, sums_mix, arrived_mix)
    n_tiles = x0.shape[0] // min(TM_MM, x0.shape[0])
    n_first = max(1, n_tiles * 3 // 8)
    (dh,), (theirs_in, other_mix) = _inproj_dh(dproj, w_in_a, None, 0, n_first, "inproj_dh_a",
                                               sides=[_halves_side([g_in]), _swap_side(mine_mix)])
    sums_in = chip_sums(("w_in",), [g_in], theirs_in)
    (dh,), (arrived_in,) = _inproj_dh(dproj, w_in_a, dh, n_first, n_tiles - n_first, "inproj_dh_b",
                                      sides=[_scatter_side(sums_in)])
    mine_in = my_halves(("w_in",), sums_in, arrived_in)
    grad_x, d_mix_g = _inproj_norm_bwd(dh, x0, w["norm_mix_g"], dx1)
    rows = {"norm_mix_g": d_mix_g, "conv_b": d_cb, "rg_lambda": d_lam, "sgu_ln_g": d_lng, "sgu_ln_b": d_lnb,
            "norm_ffn_g": d_ffn_g, "norm_final_g": d_final_g, "rg_ba": d_ba, "rg_bx": d_bx}
    vec = jnp.concatenate([rows[n] for n in _VEC_ROWS]
                          + [d_cw, jnp.pad(loss, ((0, _VEC_PAD), (0, D - 1)))], axis=0)
    other_in, (vec_parts,) = _comm_only([_swap_side(mine_in), _everyone_side([vec])], "swap_w_in")
    small_parts = [vec_parts] + sgu_parts + [wa_parts, wx_parts]

    out = {}
    for n, gm, go in zip(ffn + mix + ("w_in",), mine_ffn + mine_mix + mine_in, other_ffn + other_mix + other_in):
        g, d, mo, vo = _adamw_shard(core, gm, go, w[n][0], mom[n][0], var[n][0], "adamw_" + n)
        out[n] = tuple(a[None] for a in (g, d, mo, vo))
    as_row = lambda t: {n: a.reshape(1, D) if n == "norm_final_g" else a for n, a in t.items()}
    small_out, conv_sum, loss_sum = _small_sum_adamw(small_parts, as_row(w), as_row(mom), as_row(var))
    out.update(small_out)
    out["norm_final_g"] = tuple(a.reshape(D) for a in small_out["norm_final_g"])
    conv_g = lax.dynamic_slice_in_dim(conv_sum, chip[0] * conv_cols, conv_cols, axis=1)
    d, mo, vo = _adamw_whole(w["conv_w"][0], conv_g, mom["conv_w"][0], var["conv_w"][0], "adamw_conv_w")
    out["conv_w"] = tuple(a[None] for a in (conv_g, d, mo, vo))

    return (loss_sum[0, 0], grad_x[None], *[out[n][0] for n in _WEIGHTS], *[out[n][1] for n in _WEIGHTS],
            *[out[n][2] for n in _WEIGHTS], *[out[n][3] for n in _WEIGHTS])
```

```python
import functools

import jax
import jax.numpy as jnp
from jax import lax
from jax.experimental import pallas as pl
from jax.experimental.pallas import tpu as pltpu

F32 = jnp.float32
BF16 = jnp.bfloat16
S = jax.ShapeDtypeStruct

D = 1024
N_SHARD = 4
IN_COLS = 6 * D
IN_SHARD = IN_COLS // N_SHARD
D_FF = 2816
FF_SHARD = 2 * D_FF // N_SHARD
RG_BLOCK = 256
N_RG_BLOCK = D // RG_BLOCK
CHUNK = 128
N_GROUP = 8
CONV_WIDTH = 4
RG_C = 8.0
EPS = 1e-6
ADAM_LR, ADAM_B1, ADAM_B2, ADAM_EPS, ADAM_WD, ADAM_STEP = 0.001, 0.9, 0.999, 1e-08, 0.01, 10

V7X_VMEM_BYTES = 64 * 1024 * 1024
VMEM_LIMIT = V7X_VMEM_BYTES * 3 // 4
SUBLANES = 8
MESH = pl.DeviceIdType.MESH

TM_MM = 512
TM_SCAN = 256
TM_FF = 256
TK_WGRAD = 1024


def _params(n_axes):
    return pltpu.CompilerParams(dimension_semantics=("arbitrary",) * n_axes, vmem_limit_bytes=VMEM_LIMIT)


def _resident(shape):
    nd = len(shape)
    return pl.BlockSpec(shape, lambda *_: (0,) * nd, pipeline_mode=pl.Buffered(1))


def _sig(x):
    return 1.0 / (1.0 + jnp.exp(-x))


_GELU_K2 = 2.0 * 0.7978845608028654
_GELU_C = 0.044715


def _gelu(x):
    return x * _sig(x * (_GELU_K2 + (_GELU_K2 * _GELU_C) * (x * x)))


def _gelu_and_grad(x):
    x2 = x * x
    s = _sig(x * (_GELU_K2 + (_GELU_K2 * _GELU_C) * x2))
    g = x * s
    return g, s + g * (1.0 - s) * (_GELU_K2 + (3.0 * _GELU_K2 * _GELU_C) * x2)


_EXPM1_SERIES = tuple(1.0 / f for f in (5040.0, 720.0, 120.0, 24.0, 6.0, 2.0, 1.0))


def _one_minus_exp(x):
    p = _EXPM1_SERIES[0]
    for coef in _EXPM1_SERIES[1:]:
        p = p * x + coef
    return jnp.where(x > -0.125, -x * p, 1.0 - jnp.exp(x))


def _softplus_neg(lam):
    z = -lam
    e = jnp.exp(-jnp.abs(z))
    u = 1.0 + e
    log1p = jnp.where(u == 1.0, e, jnp.log(u) * e / (u - 1.0))
    return jnp.maximum(z, 0.0) + log1p


def _rms_stats(x):
    return lax.rsqrt(jnp.mean(x * x, axis=-1, keepdims=True) + EPS)


def _rms_bwd(dy, x, g):
    rstd = _rms_stats(x)
    xhat = x * rstd
    dxhat = dy * g
    dx = rstd * (dxhat - xhat * jnp.mean(dxhat * xhat, axis=-1, keepdims=True))
    return dx, dy * xhat


def _colsum(x):
    return jnp.sum(x, axis=0, keepdims=True)


def _shift_down(x, d, fill):
    n = x.shape[0]
    if d % SUBLANES == 0:
        return jnp.concatenate([jnp.full((d, x.shape[1]), fill, x.dtype), x[:n - d]], axis=0)
    row = lax.broadcasted_iota(jnp.int32, x.shape, 0)
    return jnp.where(row < d, fill, pltpu.roll(x, d, 0))


def _shift_up(x, d, fill):
    n = x.shape[0]
    if d % SUBLANES == 0:
        return jnp.concatenate([x[d:], jnp.full((d, x.shape[1]), fill, x.dtype)], axis=0)
    row = lax.broadcasted_iota(jnp.int32, x.shape, 0)
    return jnp.where(row >= n - d, fill, pltpu.roll(x, n - d, 0))


def _scan(a, b, shift):
    d = 1
    while d < a.shape[0]:
        b = a * shift(b, d, 0.0) + b
        a = a * shift(a, d, 1.0)
        d *= 2
    return a, b


LANES = 128


def _scan_tile(a, b, outside, a_s, b_s, h_s, reverse):
    tm = a.shape[0]
    groups = tm // SUBLANES
    order = list(range(SUBLANES - 1, -1, -1) if reverse else range(SUBLANES))
    shift = _shift_up if reverse else _shift_down
    edge = groups - 1 if reverse else 0
    for j in range(D // LANES):
        a_s[j] = a[:, LANES * j:LANES * (j + 1)]
        b_s[j] = b[:, LANES * j:LANES * (j + 1)]
    for j in range(D // LANES):
        def slab(ref, k):
            return ref[j, pl.ds(k, groups, stride=SUBLANES), :]

        ga, gb = slab(a_s, order[0]), slab(b_s, order[0])
        for k in order[1:]:
            ak = slab(a_s, k)
            gb = ak * gb + slab(b_s, k)
            ga = ak * ga
        ga, gb = _scan(ga, gb, shift)
        h_out = outside[:, LANES * j:LANES * (j + 1)]
        group_end = ga * h_out + gb
        row = lax.broadcasted_iota(jnp.int32, (groups, LANES), 0)
        h = jnp.where(row == edge, h_out, shift(group_end, 1, 0.0))
        for k in order:
            h = slab(a_s, k) * h + slab(b_s, k)
            h_s[j, pl.ds(k, groups, stride=SUBLANES), :] = h
    return jnp.concatenate([h_s[j] for j in range(D // LANES)], axis=1)


def _dot(a, b):
    return jnp.dot(a, b, preferred_element_type=F32)


def _dot_nt(a, b):
    return lax.dot_general(a, b, (((1,), (1,)), ((), ())), preferred_element_type=F32)


def _dot_tn(a, b):
    return lax.dot_general(a, b, (((0,), (0,)), ((), ())), preferred_element_type=F32)


_ANY = pl.BlockSpec(memory_space=pl.ANY)


def _position():
    return lax.axis_index("x"), lax.axis_index("y"), lax.axis_index("c")


def _other_chips(x, y):
    return [(1 - x, y), (x, 1 - y), (1 - x, 1 - y)]


class _Side:
    def __init__(self, inputs, out_shapes, n_sems, make, continues=()):
        self.inputs, self.out_shapes, self.n_sems, self.make = list(inputs), list(out_shapes), n_sems, make
        self.continues = list(continues)


MID_STEP = 0.625
LATE_MID_STEP = 0.875


def _call(body, *, name, grid, in_specs, out_specs, out_shape, args, scratch_shapes=(), sides=(), aliases=None,
          scalars=None, mid=MID_STEP):
    n_in, n_out, n_scr = len(in_specs), len(out_specs), len(scratch_shapes)
    n_scalar = 0 if scalars is None else 1
    side_in = [len(s.inputs) + len(s.continues) for s in sides]
    side_out = [len(s.out_shapes) for s in sides]
    all_aliases = {k + n_scalar: v for k, v in (aliases or {}).items()}
    for idx, s in enumerate(sides):
        for k in range(len(s.continues)):
            operand = n_scalar + n_in + sum(side_in[:idx]) + len(s.inputs) + k
            all_aliases[operand] = n_out + sum(side_out[:idx]) + k

    def wrapped(*refs):
        refs = list(refs)
        take = lambda k: [refs.pop(0) for _ in range(k)]
        ins = take(n_scalar) + take(n_in)
        sins = [take(k) for k in side_in]
        outs = take(n_out)
        souts = [take(k) for k in side_out]
        scr = take(n_scr)
        sems = [take(3) for _ in sides]
        def run(phase):
            for s, si, so, sem in zip(sides, sins, souts, sems):
                for thunk in s.make(si[:len(s.inputs)], so, *sem)[phase]:
                    thunk()

        if sides:
            n_steps = functools.reduce(lambda a, b: a * b, grid)
            step = functools.reduce(lambda a, b: a + b, [
                pl.program_id(a) * functools.reduce(lambda p, q: p * q, grid[a + 1:], 1) for a in range(len(grid))])
            pl.when(step == 0)(lambda: run(0))
        body(*ins, *outs, *scr)
        if sides:
            pl.when(step == int(mid * (n_steps - 1)))(lambda: run(1))
            pl.when(step == n_steps - 1)(lambda: run(2))

    grid_spec = pltpu.PrefetchScalarGridSpec(
        num_scalar_prefetch=n_scalar, grid=grid,
        in_specs=list(in_specs) + [_ANY] * sum(side_in),
        out_specs=list(out_specs) + [_ANY] * sum(side_out),
        scratch_shapes=list(scratch_shapes) + [pltpu.SemaphoreType.DMA((s.n_sems,)) for s in sides for _ in range(3)])
    res = pl.pallas_call(
        wrapped, name=name, grid_spec=grid_spec,
        out_shape=list(out_shape) + [o for s in sides for o in s.out_shapes],
        input_output_aliases=all_aliases,
        compiler_params=_params(len(grid)),
    )(*([scalars] if n_scalar else []), *args, *[a for s in sides for a in s.inputs + s.continues])
    main, rest, per_side = list(res[:n_out]), list(res[n_out:]), []
    for k in side_out:
        per_side.append(rest[:k])
        rest = rest[k:]
    return main, per_side


def _comm_only(sides, name):
    def body():
        pass

    return _call(body, name=name, grid=(1,), in_specs=[], out_specs=[], out_shape=[], args=[], sides=sides)[1]


def _remote(src, dst, send, recv, k, device):
    return pltpu.make_async_remote_copy(src_ref=src, dst_ref=dst, send_sem=send.at[k], recv_sem=recv.at[k],
                                        device_id=device, device_id_type=MESH)


def _both_ways(copy, keys):
    return [lambda k=k: copy(k).start() for k in keys], [], [lambda k=k: copy(k).wait() for k in keys]


def _gather_side(shards):
    n = len(shards)

    def make(ins, outs, send, recv, local):
        x, y, c = _position()
        mine = 2 * x + y
        chips = _other_chips(x, y)
        pairs = [(w, j) for w in range(n) for j in range(3)]

        def own(w):
            return pltpu.make_async_copy(ins[w], outs[w].at[mine], local.at[w])

        def push(w, j):
            return _remote(ins[w], outs[w].at[mine], send, recv, 3 * w + j, (*chips[j], c))

        def arrival(w, j):
            px, py = chips[j]
            return _remote(ins[w], outs[w].at[2 * px + py], send, recv, 3 * w + j, (px, py, c))

        starts = [lambda w=w: own(w).start() for w in range(n)] + [lambda w=w, j=j: push(w, j).start() for w, j in pairs]
        waits = ([lambda w=w, j=j: arrival(w, j).wait_recv() for w, j in pairs]
                 + [lambda w=w, j=j: push(w, j).wait_send() for w, j in pairs]
                 + [lambda w=w: own(w).wait() for w in range(n)])
        return starts, [], waits

    return _Side(shards, [S((N_SHARD,) + s.shape, s.dtype) for s in shards], 3 * n, make)


def _gather_half_side(shards, relations=(0, 1, 2), into=None):
    n = len(shards)

    def make(ins, outs, send, recv, local):
        x, y, c = _position()
        mine = 2 * x + y
        chips = _other_chips(x, y)
        pairs = [(w, j) for w in range(n) for j in relations]

        def rows(w, core):
            half = ins[w].shape[0] // 2
            return pl.ds(core * half, half)

        def own(w):
            return pltpu.make_async_copy(ins[w], outs[w].at[mine], local.at[w])

        def push(w, j):
            return _remote(ins[w].at[rows(w, c), :], outs[w].at[mine, rows(w, c), :], send, recv, 3 * w + j,
                           (*chips[j], c))

        def landed(w, j, core):
            px, py = chips[j]
            return outs[w].at[2 * px + py, rows(w, core), :]

        def arrival(w, j):
            return _remote(ins[w].at[rows(w, c), :], landed(w, j, c), send, recv, 3 * w + j, (*chips[j], c))

        def passed(w, j, core):
            return _remote(landed(w, j, core), landed(w, j, core), send, recv, 3 * n + 3 * w + j, (x, y, 1 - c))

        owns = range(n) if into is None else ()
        starts = [lambda w=w: own(w).start() for w in owns] + [lambda w=w, j=j: push(w, j).start() for w, j in pairs]
        mids = [t for w, j in pairs for t in (lambda w=w, j=j: arrival(w, j).wait_recv(),
                                              lambda w=w, j=j: passed(w, j, c).start())]
        waits = ([lambda w=w, j=j: passed(w, j, 1 - c).wait_recv() for w, j in pairs]
                 + [lambda w=w, j=j: passed(w, j, c).wait_send() for w, j in pairs]
                 + [lambda w=w, j=j: push(w, j).wait_send() for w, j in pairs]
                 + [lambda w=w: own(w).wait() for w in owns])
        return starts, mids, waits

    return _Side(shards, [S((N_SHARD,) + s.shape, s.dtype) for s in shards], 6 * n, make, continues=into or ())


def _halves_side(grads):
    n = len(grads)

    def make(ins, outs, send, recv, local):
        x, y, c = _position()

        def copy(w):
            half = ins[w].shape[1] // 2
            return _remote(ins[w].at[:, pl.ds((1 - c) * half, half), :], outs[w], send, recv, w, (x, y, 1 - c))

        return _both_ways(copy, range(n))

    return _Side(grads, [S((N_SHARD, g.shape[1] // 2, g.shape[2]), F32) for g in grads], n, make)


def _scatter_side(partials):
    n = len(partials)

    def make(ins, outs, send, recv, local):
        x, y, c = _position()
        chips = _other_chips(x, y)

        def copy(k):
            w, j = divmod(k, 3)
            px, py = chips[j]
            return _remote(ins[w].at[2 * px + py], outs[w].at[j], send, recv, k, (px, py, c))

        return _both_ways(copy, range(3 * n))

    return _Side(partials, [S((3,) + p.shape[1:], p.dtype) for p in partials], 3 * n, make)


def _swap_side(halves):
    n = len(halves)

    def make(ins, outs, send, recv, local):
        x, y, c = _position()
        return _both_ways(lambda w: _remote(ins[w], outs[w], send, recv, w, (x, y, 1 - c)), range(n))

    return _Side(halves, [S(h.shape, h.dtype) for h in halves], n, make)


N_DEVICE = 8


def _everyone_side(arrays):
    n = len(arrays)
    peers = N_DEVICE - 1

    def make(ins, outs, send, recv, local):
        x, y, c = _position()
        mine = 4 * x + 2 * y + c
        pairs = [(w, k) for w in range(n) for k in range(1, N_DEVICE)]

        def peer(k):
            return (1 - x if k & 4 else x, 1 - y if k & 2 else y, 1 - c if k & 1 else c)

        def own(w):
            return pltpu.make_async_copy(ins[w], outs[w].at[mine], local.at[w])

        def push(w, k):
            return _remote(ins[w], outs[w].at[mine], send, recv, peers * w + k - 1, peer(k))

        def arrival(w, k):
            px, py, pc = peer(k)
            return _remote(ins[w], outs[w].at[4 * px + 2 * py + pc], send, recv, peers * w + k - 1, (px, py, pc))

        starts = [lambda w=w: own(w).start() for w in range(n)] + [lambda w=w, k=k: push(w, k).start() for w, k in pairs]
        waits = ([lambda w=w, k=k: arrival(w, k).wait_recv() for w, k in pairs]
                 + [lambda w=w, k=k: push(w, k).wait_send() for w, k in pairs]
                 + [lambda w=w: own(w).wait() for w in range(n)])
        return starts, [], waits

    return _Side(arrays, [S((N_DEVICE,) + a.shape, a.dtype) for a in arrays], peers * n, make)


def _inproj_own(chip, x, g, w_shard, sides=()):
    T = x.shape[0]
    tm = min(TM_MM, T)

    def body(chip_ref, x_ref, g_ref, w_ref, proj_ref, h_ref):
        xv = x_ref[...]
        h = (xv * _rms_stats(xv) * g_ref[...]).astype(BF16)
        h_ref[...] = h
        proj_ref[...] = _dot(h, w_ref[...])

    return _call(
        body, name="inproj_own", grid=(T // tm,),
        in_specs=[pl.BlockSpec((tm, D), lambda i, c: (i, 0)), pl.BlockSpec((1, D), lambda i, c: (0, 0)),
                  pl.BlockSpec((D, IN_SHARD), lambda i, c: (0, 0), pipeline_mode=pl.Buffered(1))],
        out_specs=[pl.BlockSpec((tm, IN_SHARD), lambda i, c: (i, c[0])), pl.BlockSpec((tm, D), lambda i, c: (i, 0))],
        out_shape=[S((T, IN_COLS), F32), S((T, D), BF16)],
        args=(x, g, w_shard), sides=sides, scalars=chip, mid=LATE_MID_STEP)


def _inproj_rest(chip, proj, h, w_in, first, count, name, sides=(), mid=MID_STEP):
    T = h.shape[0]
    tm = min(TM_MM, T)

    def body(chip_ref, _, h_ref, w_ref, proj_ref):
        proj_ref[...] = _dot(h_ref[...], w_ref[0])

    def other(p, c):
        return jnp.bitwise_xor(c[0], first + p)

    return _call(
        body, name=name, grid=(count, T // tm),
        in_specs=[_ANY, pl.BlockSpec((tm, D), lambda p, i, c: (i, 0)),
                  pl.BlockSpec((1, D, IN_SHARD), lambda p, i, c: (other(p, c), 0, 0))],
        out_specs=[pl.BlockSpec((tm, IN_SHARD), lambda p, i, c: (i, other(p, c)))],
        out_shape=[S((T, IN_COLS), F32)],
        args=(proj, h, w_in), sides=sides, scalars=chip, aliases={0: 0}, mid=mid)


def _rg_gates(xc, wa_ref, wx_ref, ba, bx, sp):
    xb = xc.astype(BF16)
    blocks = [xb[:, RG_BLOCK * j:RG_BLOCK * (j + 1)] for j in range(N_RG_BLOCK)]
    r = _sig(jnp.concatenate([_dot(blocks[j], wa_ref[j]) for j in range(N_RG_BLOCK)], axis=1) + ba)
    gi = _sig(jnp.concatenate([_dot(blocks[j], wx_ref[j]) for j in range(N_RG_BLOCK)], axis=1) + bx)
    log_a = (-RG_C) * r * sp
    a = jnp.exp(log_a)
    m = jnp.sqrt(_one_minus_exp(2.0 * log_a))
    return xb, r, gi, a, m


N_GATES = 4
HEADS_PER_BLOCK = 4
HEAD_DIM = RG_BLOCK // HEADS_PER_BLOCK
_RG_BLOCKS_BF16 = pltpu.VMEM((N_RG_BLOCK, RG_BLOCK, RG_BLOCK), BF16)


def _fill_blockdiag(heads_ref, blocks):
    blocks[...] = jnp.zeros_like(blocks)
    for j in range(N_RG_BLOCK):
        for h in range(HEADS_PER_BLOCK):
            sl = slice(HEAD_DIM * h, HEAD_DIM * (h + 1))
            blocks[j, sl, sl] = heads_ref[0, HEADS_PER_BLOCK * j + h].astype(BF16)


def _rglru_fwd(proj, conv_w, conv_b, rg_wa, rg_wx, ba, bx, lam, sides=()):
    T = proj.shape[0]
    tm = min(TM_SCAN, T)

    def body(rx_ref, gate_ref, cw_ref, cb_ref, wah_ref, wxh_ref, ba_ref, bx_ref, lam_ref,
             ya_ref, xc_ref, h_ref, gates_ref, ext, hc, a_s, b_s, h_s, wa_ref, wx_ref):
        @pl.when(pl.program_id(0) == 0)
        def _():
            ext[0:SUBLANES, :] = jnp.zeros((SUBLANES, D), F32)
            hc[...] = jnp.zeros((SUBLANES, D), F32)
            _fill_blockdiag(wah_ref, wa_ref)
            _fill_blockdiag(wxh_ref, wx_ref)

        ext[SUBLANES:SUBLANES + tm, :] = rx_ref[...]
        xc = cb_ref[...]
        for k in range(CONV_WIDTH):
            xc = xc + ext[pl.ds(SUBLANES - (CONV_WIDTH - 1) + k, tm), :] * cw_ref[k:k + 1, :]
        ext[0:SUBLANES, :] = ext[tm:tm + SUBLANES, :]
        xc_ref[...] = xc
        _, r, gi, a, m = _rg_gates(xc, wa_ref, wx_ref, ba_ref[...], bx_ref[...], _softplus_neg(lam_ref[...]))
        for k, val in enumerate((r, gi, a, m)):
            gates_ref[:, D * k:D * (k + 1)] = val
        h = _scan_tile(a, m * (gi * xc), hc[0:1, :], a_s, b_s, h_s, reverse=False)
        hc[...] = jnp.broadcast_to(h[tm - 1:tm, :], (SUBLANES, D))
        h_ref[...] = h
        ya_ref[...] = (_gelu(gate_ref[...]) * h).astype(BF16)

    vec = pl.BlockSpec((1, D), lambda i: (0, 0))
    heads = pl.BlockSpec(rg_wa.shape, lambda i: (0, 0, 0, 0))
    tile = pl.BlockSpec((tm, D), lambda i: (i, 0))
    return _call(
        body, name="rglru_fwd", grid=(T // tm,),
        in_specs=[pl.BlockSpec((tm, D), lambda i: (i, 0)), pl.BlockSpec((tm, D), lambda i: (i, 1)),
                  pl.BlockSpec((CONV_WIDTH, D), lambda i: (0, 0)), vec, heads, heads, vec, vec, vec],
        out_specs=[tile, tile, tile, pl.BlockSpec((tm, N_GATES * D), lambda i: (i, 0))],
        out_shape=[S((T, D), BF16), S((T, D), F32), S((T, D), F32), S((T, N_GATES * D), F32)],
        scratch_shapes=[pltpu.VMEM((tm + SUBLANES, D), F32), pltpu.VMEM((SUBLANES, D), F32)]
        + [pltpu.VMEM((D // LANES, tm, LANES), F32)] * 3 + [_RG_BLOCKS_BF16] * 2,
        args=(proj, proj, conv_w, conv_b, rg_wa, rg_wx, ba, bx, lam), sides=sides, mid=LATE_MID_STEP)


def _layer_norm_stats(v):
    mu = jnp.mean(v, axis=-1, keepdims=True)
    vc = v - mu
    rstd = lax.rsqrt(jnp.mean(vc * vc, axis=-1, keepdims=True) + EPS)
    return vc * rstd, rstd


def _sgu_mix(w_ref, vnb, bst_ref, n_chunk):
    cols = []
    for g in range(N_GROUP):
        vg = vnb[:, CHUNK * g:CHUNK * (g + 1)].reshape(n_chunk, CHUNK, CHUNK)
        wb = jnp.broadcast_to(w_ref[g][None], (n_chunk, CHUNK, CHUNK))
        mg = lax.dot_general(wb, vg, (((2,), (1,)), ((0,), (0,))), preferred_element_type=F32)
        mg = mg + bst_ref[:, g:g + 1][None]
        cols.append(mg.reshape(n_chunk * CHUNK, CHUNK))
    return jnp.concatenate(cols, axis=1)


def _causal_mask():
    return (lax.broadcasted_iota(jnp.int32, (CHUNK, CHUNK), 0) >= lax.broadcasted_iota(jnp.int32, (CHUNK, CHUNK), 1))


def _fill_sgu_weights(ws_ref, bs_ref, w_tril, bs_t, w_tril_t=None):
    keep = _causal_mask()
    for g in range(N_GROUP):
        wg = jnp.where(keep, ws_ref[0, g], 0.0)
        w_tril[g] = wg.astype(BF16)
        if w_tril_t is not None:
            w_tril_t[g] = wg.T.astype(BF16)
    bs_t[...] = bs_ref[0].T


_SGU_W_BF16 = pltpu.VMEM((N_GROUP, CHUNK, CHUNK), BF16)
_SGU_BT = pltpu.VMEM((CHUNK, N_GROUP), F32)


def _sgu_merge_fwd(x, proj, ya, ln_g, ln_b, sgu_ws, sgu_bs, wpa, wpb, wout):
    T = x.shape[0]
    tm = min(TM_FF, T)
    n_chunk = tm // CHUNK

    def body(x_ref, uv_ref, gab_ref, ya_ref, g_ref, b_ref, ws_ref, bs_ref, wpa_ref, wpb_ref, wout_ref,
             yb_ref, pa_ref, pb_ref, mb_ref, x1_ref, w_ref, bst_ref):
        @pl.when(pl.program_id(0) == 0)
        def _():
            _fill_sgu_weights(ws_ref, bs_ref, w_ref, bst_ref)

        vhat, _ = _layer_norm_stats(_gelu(uv_ref[:, D:2 * D]))
        vnb = (vhat * g_ref[...] + b_ref[...]).astype(BF16)
        yb = (_gelu(uv_ref[:, 0:D]) * _sgu_mix(w_ref, vnb, bst_ref, n_chunk)).astype(BF16)
        yb_ref[...] = yb
        pa = _dot(ya_ref[...], wpa_ref[...])
        pb = _dot(yb, wpb_ref[...])
        pa_ref[...] = pa.astype(BF16)
        pb_ref[...] = pb.astype(BF16)
        mb = (_sig(gab_ref[:, 0:D]) * pa + _sig(gab_ref[:, D:2 * D]) * pb).astype(BF16)
        mb_ref[...] = mb
        x1_ref[...] = x_ref[...] + _dot(mb, wout_ref[...])

    tile = pl.BlockSpec((tm, D), lambda i: (i, 0))
    vec = pl.BlockSpec((1, D), lambda i: (0, 0))
    w = _resident((D, D))
    return pl.pallas_call(
        body, name="sgu_merge_fwd", grid=(T // tm,),
        in_specs=[tile, pl.BlockSpec((tm, 2 * D), lambda i: (i, 1)), pl.BlockSpec((tm, 2 * D), lambda i: (i, 2)), tile,
                  vec, vec, pl.BlockSpec(sgu_ws.shape, lambda i: (0, 0, 0, 0)),
                  pl.BlockSpec(sgu_bs.shape, lambda i: (0, 0, 0)), w, w, w],
        out_specs=[tile, tile, tile, tile, tile],
        out_shape=[S((T, D), BF16), S((T, D), BF16), S((T, D), BF16), S((T, D), BF16), S((T, D), F32)],
        scratch_shapes=[_SGU_W_BF16, _SGU_BT],
        compiler_params=_params(1),
    )(x, proj, proj, ya, ln_g, ln_b, sgu_ws, sgu_bs, wpa, wpb, wout)


def _ffn_fwd_loss(x1, g, w_gu, w_down, g_final, target):
    T = x1.shape[0]
    tm = min(TM_FF, T)

    def body(x_ref, g_ref, wgu_ref, wd_ref, gf_ref, t_ref,
             h2_ref, gu_ref, act_ref, dx2_ref, dx2b_ref, loss_ref, dg_ref):
        @pl.when(pl.program_id(0) == 0)
        def _():
            loss_ref[...] = jnp.zeros_like(loss_ref)
            dg_ref[...] = jnp.zeros_like(dg_ref)

        xv = x_ref[...]
        h2 = (xv * _rms_stats(xv) * g_ref[...]).astype(BF16)
        h2_ref[...] = h2
        x2 = xv
        for k in range(N_SHARD // 2):
            cols = slice(FF_SHARD * k, FF_SHARD * (k + 1))
            gate = _dot(h2, wgu_ref[k])
            up = _dot(h2, wgu_ref[k + N_SHARD // 2])
            gu_ref[:, cols] = gate.astype(BF16)
            gu_ref[:, D_FF + FF_SHARD * k:D_FF + FF_SHARD * (k + 1)] = up.astype(BF16)
            act = (gate * _sig(gate) * up).astype(BF16)
            act_ref[:, cols] = act
            x2 = x2 + _dot(act, wd_ref[cols, :])
        gf = gf_ref[...]
        err = x2 * _rms_stats(x2) * gf - t_ref[...]
        loss_ref[...] += 0.5 * jnp.sum(jnp.mean(err * err, axis=-1, keepdims=True), axis=0, keepdims=True)
        dx2, dg_rows = _rms_bwd(err * (1.0 / D), x2, gf)
        dg_ref[...] += _colsum(dg_rows)
        dx2_ref[...] = dx2
        dx2b_ref[...] = dx2.astype(BF16)

    tile = pl.BlockSpec((tm, D), lambda i: (i, 0))
    vec = pl.BlockSpec((1, D), lambda i: (0, 0))
    return pl.pallas_call(
        body, name="ffn_fwd_loss", grid=(T // tm,),
        in_specs=[tile, vec, _resident((N_SHARD, D, FF_SHARD)), _resident((D_FF, D)), vec, tile],
        out_specs=[tile, pl.BlockSpec((tm, 2 * D_FF), lambda i: (i, 0)), pl.BlockSpec((tm, D_FF), lambda i: (i, 0)),
                   tile, tile, pl.BlockSpec((1, 1), lambda i: (0, 0)), vec],
        out_shape=[S((T, D), BF16), S((T, 2 * D_FF), BF16), S((T, D_FF), BF16), S((T, D), F32), S((T, D), BF16),
                   S((1, 1), F32), S((1, D), F32)],
        compiler_params=_params(1),
    )(x1, g, w_gu, w_down, g_final, target)


def _ffn_bwd(dx2, dx2b, gu, w_down, w_gu, x1, g):
    T = x1.shape[0]
    tm = min(TM_FF, T)

    def body(dx2_ref, dx2b_ref, gu_ref, wd_ref, wgu_ref, x_ref, g_ref, dgu_ref, dx1_ref, dx1b_ref, dg_ref):
        @pl.when(pl.program_id(0) == 0)
        def _():
            dg_ref[...] = jnp.zeros_like(dg_ref)

        dxb = dx2b_ref[...]
        dh2 = jnp.zeros((tm, D), F32)
        for k in range(N_SHARD // 2):
            cols = slice(FF_SHARD * k, FF_SHARD * (k + 1))
            up_cols = slice(D_FF + FF_SHARD * k, D_FF + FF_SHARD * (k + 1))
            dact = _dot_nt(dxb, wd_ref[cols, :])
            gate = gu_ref[:, cols].astype(F32)
            sg = _sig(gate)
            dgate = (dact * gu_ref[:, up_cols].astype(F32) * (sg * (1.0 + gate * (1.0 - sg)))).astype(BF16)
            dup = (dact * (gate * sg)).astype(BF16)
            dgu_ref[:, cols] = dgate
            dgu_ref[:, up_cols] = dup
            dh2 = dh2 + _dot_nt(dgate, wgu_ref[k]) + _dot_nt(dup, wgu_ref[k + N_SHARD // 2])
        dx, dg_rows = _rms_bwd(dh2, x_ref[...], g_ref[...])
        dg_ref[...] += _colsum(dg_rows)
        dx1 = dx2_ref[...] + dx
        dx1_ref[...] = dx1
        dx1b_ref[...] = dx1.astype(BF16)

    tile = pl.BlockSpec((tm, D), lambda i: (i, 0))
    wide = pl.BlockSpec((tm, 2 * D_FF), lambda i: (i, 0))
    vec = pl.BlockSpec((1, D), lambda i: (0, 0))
    return pl.pallas_call(
        body, name="ffn_bwd", grid=(T // tm,),
        in_specs=[tile, tile, wide, _resident((D_FF, D)), _resident((N_SHARD, D, FF_SHARD)), tile, vec],
        out_specs=[wide, tile, tile, vec],
        out_shape=[S((T, 2 * D_FF), BF16), S((T, D), F32), S((T, D), BF16), S((1, D), F32)],
        compiler_params=_params(1),
    )(dx2, dx2b, gu, w_down, w_gu, x1, g)


def _matmul_tn(a, b, tn, shard_major, name, sides=()):
    T, M = a.shape
    N = b.shape[1]
    tk = min(TK_WGRAD, T)

    def body(a_ref, b_ref, o_ref):
        @pl.when(pl.program_id(1) == 0)
        def _():
            o_ref[...] = jnp.zeros_like(o_ref)

        acc = _dot_tn(a_ref[...], b_ref[...])
        if shard_major:
            o_ref[0] += acc
        else:
            o_ref[...] += acc

    if shard_major:
        out_spec, out_shape = pl.BlockSpec((1, M, tn), lambda j, k: (j, 0, 0)), S((N // tn, M, tn), F32)
    else:
        out_spec, out_shape = pl.BlockSpec((M, tn), lambda j, k: (0, j)), S((M, N), F32)
    (out,), side_outs = _call(
        body, name=name, grid=(N // tn, T // tk),
        in_specs=[pl.BlockSpec((tk, M), lambda j, k: (k, 0)), pl.BlockSpec((tk, tn), lambda j, k: (k, j))],
        out_specs=[out_spec], out_shape=[out_shape], args=(a, b), sides=sides)
    return out, side_outs


PIECE = IN_SHARD // 3
N_PIECE = IN_COLS // PIECE
DPROJ_ROTATION = 2 * D // PIECE


def _merge_sgu_bwd(dx1b, proj, pa, pb, ln_g, ln_b, sgu_ws, sgu_bs, wpa, wpb, wout, sides=()):
    T = dx1b.shape[0]
    tm = min(TM_FF, T)
    n_chunk = tm // CHUNK

    def body(dx_ref, uv_ref, gab_ref, pa_ref, pb_ref, g_ref, b_ref, ws_ref, bs_ref, wpa_ref, wpb_ref, wout_ref,
             dpa_ref, dpb_ref, dya_ref, dp_ref, dw_ref, dbs_ref, dg_ref, db_ref, w_ref, wt_ref, bst_ref):
        @pl.when(pl.program_id(0) == 0)
        def _():
            for ref in (dw_ref, dbs_ref, dg_ref, db_ref):
                ref[...] = jnp.zeros_like(ref)
            _fill_sgu_weights(ws_ref, bs_ref, w_ref, bst_ref, wt_ref)

        dm = _dot_nt(dx_ref[...], wout_ref[...])
        sa = _sig(gab_ref[:, 0:D])
        sb = _sig(gab_ref[:, D:2 * D])
        dpa = (dm * sa).astype(BF16)
        dpb = (dm * sb).astype(BF16)
        dpa_ref[...] = dpa
        dpb_ref[...] = dpb
        dp_ref[:, 2 * D:3 * D] = (dm * pa_ref[...].astype(F32) * (sa * (1.0 - sa))).astype(BF16)
        dp_ref[:, 3 * D:4 * D] = (dm * pb_ref[...].astype(F32) * (sb * (1.0 - sb))).astype(BF16)
        dya_ref[...] = _dot_nt(dpa, wpa_ref[...]).astype(BF16)
        dyb_v = _dot_nt(dpb, wpb_ref[...])

        gu, dgu = _gelu_and_grad(uv_ref[:, 0:D])
        gv, dgv = _gelu_and_grad(uv_ref[:, D:2 * D])
        vhat, rstd = _layer_norm_stats(gv)
        lng = g_ref[...]
        vnb = (vhat * lng + b_ref[...]).astype(BF16)
        mixed = _sgu_mix(w_ref, vnb, bst_ref, n_chunk)
        dp_ref[:, 0:D] = (dyb_v * mixed * dgu).astype(BF16)
        dmix = dyb_v * gu
        dmb = dmix.astype(BF16)
        keep = _causal_mask()
        dvn_cols, dbs_rows = [], []
        for g in range(N_GROUP):
            sl = slice(CHUNK * g, CHUNK * (g + 1))
            dmg = dmb[:, sl].reshape(n_chunk, CHUNK, CHUNK)
            vg = vnb[:, sl].reshape(n_chunk, CHUNK, CHUNK)
            wtb = jnp.broadcast_to(wt_ref[g][None], (n_chunk, CHUNK, CHUNK))
            dvn = lax.dot_general(wtb, dmg, (((2,), (1,)), ((0,), (0,))), preferred_element_type=F32)
            dvn_cols.append(dvn.reshape(tm, CHUNK))
            dw = lax.dot_general(dmg, vg, (((2,), (2,)), ((0,), (0,))), preferred_element_type=F32)
            dw_ref[g] += jnp.where(keep, jnp.sum(dw, axis=0), 0.0)
            per_token = jnp.sum(dmix[:, sl], axis=1)
            dbs_rows.append(jnp.sum(per_token.reshape(n_chunk, CHUNK), axis=0, keepdims=True))
        dbs_ref[...] += jnp.concatenate(dbs_rows, axis=0)
        dvn = jnp.concatenate(dvn_cols, axis=1)
        dg_ref[...] += _colsum(dvn * vhat)
        db_ref[...] += _colsum(dvn)
        dvhat = dvn * lng
        dgv_in = rstd * (dvhat - jnp.mean(dvhat, axis=-1, keepdims=True)
                         - vhat * jnp.mean(dvhat * vhat, axis=-1, keepdims=True))
        dp_ref[:, D:2 * D] = (dgv_in * dgv).astype(BF16)

    tile = pl.BlockSpec((tm, D), lambda i: (i, 0))
    vec = pl.BlockSpec((1, D), lambda i: (0, 0))
    w = _resident((D, D))
    wsp = pl.BlockSpec((N_GROUP, CHUNK, CHUNK), lambda i: (0, 0, 0))
    return _call(
        body, name="merge_sgu_bwd", grid=(T // tm,),
        in_specs=[tile, pl.BlockSpec((tm, 2 * D), lambda i: (i, 1)), pl.BlockSpec((tm, 2 * D), lambda i: (i, 2)),
                  tile, tile, vec, vec, pl.BlockSpec(sgu_ws.shape, lambda i: (0, 0, 0, 0)),
                  pl.BlockSpec(sgu_bs.shape, lambda i: (0, 0, 0)), w, w, w],
        out_specs=[tile, tile, tile, pl.BlockSpec((tm, 4 * D), lambda i: (i, 0)), wsp,
                   pl.BlockSpec((N_GROUP, CHUNK), lambda i: (0, 0)), vec, vec],
        out_shape=[S((T, D), BF16), S((T, D), BF16), S((T, D), BF16), S((T, IN_COLS), BF16),
                   S((N_GROUP, CHUNK, CHUNK), F32), S((N_GROUP, CHUNK), F32), S((1, D), F32), S((1, D), F32)],
        scratch_shapes=[_SGU_W_BF16, _SGU_W_BF16, _SGU_BT],
        args=(dx1b, proj, proj, pa, pb, ln_g, ln_b, sgu_ws, sgu_bs, wpa, wpb, wout), sides=sides)


def _rglru_bwd(dya, dproj, proj, hseq, xc, gates, conv_w, rg_wa, rg_wx, lam, sides=()):
    T = dya.shape[0]
    tm = min(TM_SCAN, T)
    n = T // tm
    per8 = tm // SUBLANES

    def body(dya_ref, _, rx_ref, rxp_ref, gate_ref, h_ref, hp_ref, xc_ref, gates_ref, cw_ref, wah_ref, wxh_ref,
             lam_ref, dab_ref, dcw_ref, dcb_ref, dwah_ref, dwxh_ref, dba_ref, dbx_ref, dlam_ref,
             hext, rext, dext, carry_a, carry_dh, a_s, b_s, h_s, wa_ref, wx_ref, dwa_ref, dwx_ref):
        i = pl.program_id(0)
        first_tile = i == n - 1

        @pl.when(i == 0)
        def _():
            for ref in (dcw_ref, dcb_ref, dwa_ref, dwx_ref, dba_ref, dbx_ref, dlam_ref, carry_a, carry_dh):
                ref[...] = jnp.zeros_like(ref)
            dext[tm:tm + SUBLANES, :] = jnp.zeros((SUBLANES, D), F32)
            _fill_blockdiag(wah_ref, wa_ref)
            _fill_blockdiag(wxh_ref, wx_ref)

        gel, dgel = _gelu_and_grad(gate_ref[...])
        dya_v = dya_ref[...].astype(F32)
        hseq_v = h_ref[...]
        dgate = dya_v * hseq_v * dgel
        xcv = xc_ref[...]
        lam_v = lam_ref[...]
        sp = _softplus_neg(lam_v)
        xb = xcv.astype(BF16)
        r, gi, a, m = (gates_ref[:, D * k:D * (k + 1)] for k in range(N_GATES))

        row = lax.broadcasted_iota(jnp.int32, (tm, D), 0)
        c = jnp.where(row == tm - 1, carry_a[0:1, :], _shift_up(a, 1, 0.0))
        dH = _scan_tile(c, dya_v * gel, carry_dh[0:1, :], a_s, b_s, h_s, reverse=True)
        carry_a[...] = jnp.broadcast_to(a[0:1, :], (SUBLANES, D))
        carry_dh[...] = jnp.broadcast_to(dH[0:1, :], (SUBLANES, D))

        hext[0:SUBLANES, :] = jnp.where(first_tile, 0.0, hp_ref[...])
        hext[SUBLANES:SUBLANES + tm, :] = hseq_v
        h_prev = hext[pl.ds(SUBLANES - 1, tm), :]

        d_m = dH * (gi * xcv)
        d_la = dH * h_prev * a - d_m * (a * a) / m
        d_ia = dH * m * xcv * (gi * (1.0 - gi))
        d_ra = d_la * ((-RG_C) * sp) * (r * (1.0 - r))
        dlam_ref[...] += _colsum(d_la * ((-RG_C) * r)) * (-_sig(-lam_v))
        dba_ref[...] += _colsum(d_ra)
        dbx_ref[...] += _colsum(d_ia)
        drab = d_ra.astype(BF16)
        diab = d_ia.astype(BF16)
        dxc_cols = []
        for j in range(N_RG_BLOCK):
            sl = slice(RG_BLOCK * j, RG_BLOCK * (j + 1))
            dxc_cols.append(_dot_nt(drab[:, sl], wa_ref[j]) + _dot_nt(diab[:, sl], wx_ref[j]))
            dwa_ref[j] += _dot_tn(xb[:, sl], drab[:, sl])
            dwx_ref[j] += _dot_tn(xb[:, sl], diab[:, sl])
        dxc = dH * m * gi + jnp.concatenate(dxc_cols, axis=1)

        dcb_ref[...] += _colsum(dxc)
        dext[0:tm, :] = dxc
        rext[0:SUBLANES, :] = jnp.where(first_tile, 0.0, rxp_ref[...])
        rext[SUBLANES:SUBLANES + tm, :] = rx_ref[...]
        drx = jnp.zeros((tm, D), F32)
        for k in range(CONV_WIDTH):
            drx = drx + dext[pl.ds(CONV_WIDTH - 1 - k, tm), :] * cw_ref[k:k + 1, :]
            dcw_ref[k:k + 1, :] += _colsum(dxc * rext[pl.ds(SUBLANES - (CONV_WIDTH - 1) + k, tm), :])
        dext[tm:tm + SUBLANES, :] = dext[0:SUBLANES, :]
        dab_ref[:, 0:D] = drx.astype(BF16)
        dab_ref[:, D:2 * D] = dgate.astype(BF16)

        @pl.when(first_tile)
        def _():
            for j in range(N_RG_BLOCK):
                for h in range(HEADS_PER_BLOCK):
                    sl = slice(HEAD_DIM * h, HEAD_DIM * (h + 1))
                    pair, side = divmod(HEADS_PER_BLOCK * j + h, 2)
                    lanes = slice(HEAD_DIM * side, HEAD_DIM * (side + 1))
                    dwah_ref[pair, :, lanes] = dwa_ref[j, sl, sl]
                    dwxh_ref[pair, :, lanes] = dwx_ref[j, sl, sl]

    def rev(col):
        return lambda i: (n - 1 - i, col)

    def prev8(col):
        return lambda i: (jnp.maximum((n - 1 - i) * per8 - 1, 0), col)

    tile = pl.BlockSpec((tm, D), rev(0))
    vec = pl.BlockSpec((1, D), lambda i: (0, 0))
    heads_in = pl.BlockSpec(rg_wa.shape, lambda i: (0, 0, 0, 0))
    head_pairs = (rg_wa.shape[1] // 2, HEAD_DIM, 2 * HEAD_DIM)
    heads_out = pl.BlockSpec(head_pairs, lambda i: (0, 0, 0))
    cw = pl.BlockSpec((CONV_WIDTH, D), lambda i: (0, 0))
    blocks_f32 = pltpu.VMEM((N_RG_BLOCK, RG_BLOCK, RG_BLOCK), F32)
    return _call(
        body, name="rglru_bwd", grid=(n,),
        in_specs=[tile, _ANY, pl.BlockSpec((tm, D), rev(0)), pl.BlockSpec((SUBLANES, D), prev8(0)),
                  pl.BlockSpec((tm, D), rev(1)), tile, pl.BlockSpec((SUBLANES, D), prev8(0)), tile,
                  pl.BlockSpec((tm, N_GATES * D), rev(0)), cw, heads_in, heads_in, vec],
        out_specs=[pl.BlockSpec((tm, 2 * D), rev(2)), cw, vec, heads_out, heads_out, vec, vec, vec],
        out_shape=[S((T, IN_COLS), BF16), S((CONV_WIDTH, D), F32), S((1, D), F32),
                   S(head_pairs, F32), S(head_pairs, F32), S((1, D), F32), S((1, D), F32), S((1, D), F32)],
        scratch_shapes=[pltpu.VMEM((tm + SUBLANES, D), F32), pltpu.VMEM((tm + SUBLANES, D), F32),
                        pltpu.VMEM((tm + SUBLANES, D), F32), pltpu.VMEM((SUBLANES, D), F32),
                        pltpu.VMEM((SUBLANES, D), F32)] + [pltpu.VMEM((D // LANES, tm, LANES), F32)] * 3
        + [_RG_BLOCKS_BF16] * 2 + [blocks_f32] * 2,
        args=(dya, dproj, proj, proj, proj, hseq, hseq, xc, gates, conv_w, rg_wa, rg_wx, lam), sides=sides,
        aliases={1: 0})


def _inproj_dh(dproj, w_in, dh, first, count, name, sides=()):
    T = dproj.shape[0]
    tm = min(TM_MM, T)

    def body(*refs):
        dp_ref, w_ref, dh_ref = refs[-3:]
        dh = jnp.zeros((tm, D), F32)
        for p in range(N_PIECE):
            shard, part = divmod((p + DPROJ_ROTATION) % N_PIECE, IN_SHARD // PIECE)
            dh = dh + _dot_nt(dp_ref[:, PIECE * p:PIECE * (p + 1)], w_ref[shard, :, PIECE * part:PIECE * (part + 1)])
        dh_ref[...] = dh

    earlier = [] if dh is None else [dh]
    return _call(
        body, name=name, grid=(count,),
        in_specs=[_ANY] * len(earlier) + [pl.BlockSpec((tm, IN_COLS), lambda i: (first + i, 0)),
                                         _resident((N_SHARD, D, IN_SHARD))],
        out_specs=[pl.BlockSpec((tm, D), lambda i: (first + i, 0))],
        out_shape=[S((T, D), F32)],
        args=(*earlier, dproj, w_in), sides=sides, aliases={0: 0} if earlier else None)


def _inproj_norm_bwd(dh, x, g, dx1):
    T = x.shape[0]
    tm = min(TM_MM, T)

    def body(dh_ref, x_ref, g_ref, dx1_ref, dx_ref, dgm_ref):
        @pl.when(pl.program_id(0) == 0)
        def _():
            dgm_ref[...] = jnp.zeros_like(dgm_ref)

        dx, dg_rows = _rms_bwd(dh_ref[...], x_ref[...], g_ref[...])
        dgm_ref[...] += _colsum(dg_rows)
        dx_ref[...] = dx1_ref[...] + dx

    tile = pl.BlockSpec((tm, D), lambda i: (i, 0))
    vec = pl.BlockSpec((1, D), lambda i: (0, 0))
    return pl.pallas_call(
        body, name="inproj_norm_bwd", grid=(T // tm,),
        in_specs=[tile, tile, vec, tile],
        out_specs=[tile, vec],
        out_shape=[S((T, D), F32), S((1, D), F32)],
        compiler_params=_params(1),
    )(dh, x, g, dx1)


def _inproj_wgrad(h, dproj, sides=()):
    T = h.shape[0]
    tk = min(TK_WGRAD, T)
    per = IN_SHARD // PIECE

    def body(h_ref, *refs):
        pieces, o_ref = refs[:per], refs[per]

        @pl.when(pl.program_id(1) == 0)
        def _():
            o_ref[...] = jnp.zeros_like(o_ref)

        o_ref[0] += _dot_tn(h_ref[...], jnp.concatenate([p[...] for p in pieces], axis=1))

    def piece(i):
        return pl.BlockSpec((tk, PIECE), lambda j, k: (k, (per * j + i + N_PIECE - DPROJ_ROTATION) % N_PIECE))

    (out,), side_outs = _call(
        body, name="inproj_wgrad", grid=(N_SHARD, T // tk),
        in_specs=[pl.BlockSpec((tk, D), lambda j, k: (k, 0))] + [piece(i) for i in range(per)],
        out_specs=[pl.BlockSpec((1, D, IN_SHARD), lambda j, k: (j, 0, 0))],
        out_shape=[S((N_SHARD, D, IN_SHARD), F32)], args=(h,) + (dproj,) * per, sides=sides)
    return out, side_outs


def _row_tile(rows):
    for t in range(256, 0, -SUBLANES):
        if rows % t == 0:
            return t
    raise ValueError(rows)


def _add_halves(core, g, theirs, name):
    _, r, cols = g.shape
    half = r // 2
    tr = _row_tile(half)
    nb = half // tr

    def body(core_ref, g_ref, t_ref, o_ref):
        o_ref[...] = (g_ref[...] + t_ref[...]).astype(BF16)

    blk = pl.BlockSpec((1, tr, cols), lambda s, i, core_ref: (s, i, 0))
    gs = pltpu.PrefetchScalarGridSpec(
        num_scalar_prefetch=1, grid=(N_SHARD, nb),
        in_specs=[pl.BlockSpec((1, tr, cols), lambda s, i, core_ref: (s, core_ref[0] * nb + i, 0)), blk],
        out_specs=blk)
    return pl.pallas_call(
        body, name=name, grid_spec=gs, out_shape=S((N_SHARD, half, cols), BF16), compiler_params=_params(2),
    )(core, g, theirs)


def _sum_shards(chip, own, others, name):
    _, half, cols = own.shape
    tr = _row_tile(half)

    def body(chip_ref, own_ref, oth_ref, o_ref):
        acc = own_ref[0].astype(F32)
        for j in range(3):
            acc = acc + oth_ref[j].astype(F32)
        o_ref[...] = acc

    gs = pltpu.PrefetchScalarGridSpec(
        num_scalar_prefetch=1, grid=(half // tr,),
        in_specs=[pl.BlockSpec((1, tr, cols), lambda i, chip_ref: (chip_ref[0], i, 0)),
                  pl.BlockSpec((3, tr, cols), lambda i, chip_ref: (0, i, 0))],
        out_specs=pl.BlockSpec((tr, cols), lambda i, chip_ref: (i, 0)))
    return pl.pallas_call(
        body, name=name, grid_spec=gs, out_shape=S((half, cols), F32), compiler_params=_params(1),
    )(chip, own, others)


def _adamw(w, g, m, v):
    m = ADAM_B1 * m + (1.0 - ADAM_B1) * g
    v = ADAM_B2 * v + (1.0 - ADAM_B2) * (g * g)
    m_hat = m / (1.0 - ADAM_B1 ** ADAM_STEP)
    v_hat = v / (1.0 - ADAM_B2 ** ADAM_STEP)
    delta = -ADAM_LR * (m_hat / (jnp.sqrt(v_hat) + ADAM_EPS) + ADAM_WD * w)
    return delta, m, v


def _adamw_shard(core, mine, theirs, w, m, v, name):
    r, cols = w.shape
    half = r // 2
    tr = _row_tile(half)
    nb = half // tr

    def body(core_ref, mine_ref, theirs_ref, w_ref, m_ref, v_ref, g_ref, d_ref, mo_ref, vo_ref):
        g = jnp.where(pl.program_id(0) == core_ref[0], mine_ref[...], theirs_ref[...])
        g_ref[...] = g
        d_ref[...], mo_ref[...], vo_ref[...] = _adamw(w_ref[...], g, m_ref[...], v_ref[...])

    hblk = pl.BlockSpec((tr, cols), lambda h, i, core_ref: (i, 0))
    blk = pl.BlockSpec((tr, cols), lambda h, i, core_ref: (h * nb + i, 0))
    gs = pltpu.PrefetchScalarGridSpec(num_scalar_prefetch=1, grid=(2, nb),
                                      in_specs=[hblk, hblk, blk, blk, blk], out_specs=[blk] * 4)
    return pl.pallas_call(
        body, name=name, grid_spec=gs, out_shape=[S((r, cols), F32)] * 4, compiler_params=_params(2),
    )(core, mine, theirs, w, m, v)


def _adamw_whole(w, g, m, v, name):
    def body(w_ref, g_ref, m_ref, v_ref, d_ref, mo_ref, vo_ref):
        d_ref[...], mo_ref[...], vo_ref[...] = _adamw(w_ref[...], g_ref[...], m_ref[...], v_ref[...])

    return pl.pallas_call(body, name=name, out_shape=[S(w.shape, F32)] * 3)(w, g, m, v)


_VEC_ROWS = ("norm_mix_g", "conv_b", "rg_lambda", "sgu_ln_g", "sgu_ln_b", "norm_ffn_g", "norm_final_g", "rg_ba",
             "rg_bx")
_CONV_ROW = len(_VEC_ROWS)
_LOSS_ROW = _CONV_ROW + CONV_WIDTH
_VEC_PAD = -(_LOSS_ROW + 1) % SUBLANES
_HEAD_BIASES = ("rg_ba", "rg_bx")
_TENSORS = ("sgu_bs", "sgu_ws", "rg_wa", "rg_wx")
_HEAD_PAIRS = ("rg_wa", "rg_wx")


def _small_sum_adamw(parts, w, m, v):
    names = [n for n in _VEC_ROWS] + list(_TENSORS)
    n_parts = len(parts)

    def total(ref):
        acc = ref[0]
        for k in range(1, N_DEVICE):
            acc = acc + ref[k]
        return acc

    def body(*refs):
        part_refs, refs = refs[:n_parts], refs[n_parts:]
        w_refs, m_refs, v_refs = (dict(zip(names, refs[k * len(names):(k + 1) * len(names)])) for k in range(3))
        outs = refs[3 * len(names):]
        out_refs = {n: outs[4 * k:4 * k + 4] for k, n in enumerate(names)}
        conv_ref, loss_ref = outs[4 * len(names):]
        vec = total(part_refs[0])
        grads = {n: total(p) for n, p in zip(_TENSORS, part_refs[1:])}
        for n in _HEAD_PAIRS:
            pairs = grads[n]
            grads[n] = jnp.stack([pairs[k // 2, :, HEAD_DIM * (k % 2):HEAD_DIM * (k % 2 + 1)]
                                  for k in range(2 * pairs.shape[0])], axis=0)
        grads = {n: g[None] for n, g in grads.items()}
        for row, n in enumerate(_VEC_ROWS):
            g = vec[row:row + 1, :]
            if n in _HEAD_BIASES:
                g = jnp.concatenate([g[:, HEAD_DIM * h:HEAD_DIM * (h + 1)] for h in range(D // HEAD_DIM)], axis=0)[None]
            grads[n] = g
        for n in names:
            g_ref, d_ref, mo_ref, vo_ref = out_refs[n]
            g_ref[...] = grads[n]
            d_ref[...], mo_ref[...], vo_ref[...] = _adamw(w_refs[n][...], grads[n], m_refs[n][...], v_refs[n][...])
        conv_ref[...] = vec[_CONV_ROW:_CONV_ROW + CONV_WIDTH, :]
        loss_ref[...] = vec[_LOSS_ROW:_LOSS_ROW + 1, 0:1]

    res = pl.pallas_call(
        body, name="small_sum_adamw",
        out_shape=[S(w[n].shape, F32) for n in names for _ in range(4)] + [S((CONV_WIDTH, D), F32), S((1, 1), F32)],
        compiler_params=pltpu.CompilerParams(vmem_limit_bytes=VMEM_LIMIT),
    )(*parts, *[w[n] for n in names], *[m[n] for n in names], *[v[n] for n in names])
    return {n: tuple(res[4 * k:4 * k + 4]) for k, n in enumerate(names)}, res[-2], res[-1]


_BIG = ("w_in", "w_proj_a", "w_proj_b", "w_out", "w_gate_up", "w_down")
_WEIGHTS = ("norm_mix_g", "w_in", "conv_w", "conv_b", "rg_wa", "rg_ba", "rg_wx", "rg_bx", "rg_lambda", "sgu_ln_g",
            "sgu_ln_b", "sgu_ws", "sgu_bs", "w_proj_a", "w_proj_b", "w_out", "norm_ffn_g", "w_gate_up", "w_down",
            "norm_final_g")


def kernel(x, norm_mix_g, w_in, conv_w, conv_b, rg_wa, rg_ba, rg_wx, rg_bx, rg_lambda, sgu_ln_g, sgu_ln_b, sgu_ws, sgu_bs, w_proj_a, w_proj_b, w_out, norm_ffn_g, w_gate_up, w_down, norm_final_g, loss_target, m_norm_mix_g, m_w_in, m_conv_w, m_conv_b, m_rg_wa, m_rg_ba, m_rg_wx, m_rg_bx, m_rg_lambda, m_sgu_ln_g, m_sgu_ln_b, m_sgu_ws, m_sgu_bs, m_w_proj_a, m_w_proj_b, m_w_out, m_norm_ffn_g, m_w_gate_up, m_w_down, m_norm_final_g, v_norm_mix_g, v_w_in, v_conv_w, v_conv_b, v_rg_wa, v_rg_ba, v_rg_wx, v_rg_bx, v_rg_lambda, v_sgu_ln_g, v_sgu_ln_b, v_sgu_ws, v_sgu_bs, v_w_proj_a, v_w_proj_b, v_w_out, v_norm_ffn_g, v_w_gate_up, v_w_down, v_norm_final_g):
    args = dict(locals())
    w = {n: args[n] for n in _WEIGHTS}
    mom = {n: args["m_" + n] for n in _WEIGHTS}
    var = {n: args["v_" + n] for n in _WEIGHTS}
    xi, yi, ci = _position()
    core = ci.astype(jnp.int32).reshape(1)
    chip = (2 * xi + yi).astype(jnp.int32).reshape(1)

    bf = {n: w[n][0].astype(BF16) for n in _BIG}
    final_g = w["norm_final_g"].reshape(1, D)
    ba, bx = w["rg_ba"].reshape(1, D), w["rg_bx"].reshape(1, D)
    lam, ln_g, ln_b = w["rg_lambda"], w["sgu_ln_g"], w["sgu_ln_b"]
    x0, target = x[0], loss_target[0]

    def shard_major(g):
        return g.reshape(N_SHARD, g.shape[0] // N_SHARD, g.shape[1])

    def chip_sums(names, grads, theirs):
        return [_add_halves(core, g, t, "add_halves_" + n) for n, g, t in zip(names, grads, theirs)]

    def my_halves(names, sums, arrived):
        return [_sum_shards(chip, p, a, "sum_shards_" + n) for n, p, a in zip(names, sums, arrived)]

    (proj, h), ((w_in_a,), (conv_a,)) = _inproj_own(
        chip, x0, w["norm_mix_g"], bf["w_in"],
        sides=[_gather_half_side([bf["w_in"]], relations=(0, 1)), _gather_side([w["conv_w"][0]])])
    conv_cols = conv_a.shape[-1]
    conv_full = jnp.swapaxes(conv_a, 0, 1).reshape(CONV_WIDTH, D)
    (proj,), ((w_in_a,),) = _inproj_rest(
        chip, proj, h, w_in_a, 1, 2, "inproj_near",
        sides=[_gather_half_side([bf["w_in"]], relations=(2,), into=[w_in_a])])
    (proj,), _ = _inproj_rest(chip, proj, h, w_in_a, 3, 1, "inproj_far")
    (ya, xc, hseq, gates), ((w_pa_a, w_pb_a, w_out_a, w_gu_a, w_down_a),) = _rglru_fwd(
        proj, conv_full, w["conv_b"], w["rg_wa"], w["rg_wx"], ba, bx, lam,
        sides=[_gather_half_side([bf[n] for n in ("w_proj_a", "w_proj_b", "w_out", "w_gate_up", "w_down")])])
    wpa, wpb, wout, wdown = w_pa_a.reshape(D, D), w_pb_a.reshape(D, D), w_out_a.reshape(D, D), w_down_a.reshape(D_FF, D)
    yb, pa, pb, mb, x1 = _sgu_merge_fwd(x0, proj, ya, ln_g, ln_b, w["sgu_ws"], w["sgu_bs"], wpa, wpb, wout)
    h2, gu, act, dx2, dx2b, loss, d_final_g = _ffn_fwd_loss(x1, w["norm_ffn_g"], w_gu_a, wdown, final_g, target)

    dgu, dx1, dx1b, d_ffn_g = _ffn_bwd(dx2, dx2b, gu, wdown, w_gu_a, x1, w["norm_ffn_g"])
    ffn = ("w_gate_up", "w_down")
    g_ffn = [_matmul_tn(h2, dgu, FF_SHARD, True, "wgrad_gate_up")[0],
             shard_major(_matmul_tn(act, dx2b, D // 2, False, "wgrad_down")[0])]
    (dpa, dpb, dya, dproj, d_ws, d_bs, d_lng, d_lnb), (theirs_ffn,) = _merge_sgu_bwd(
        dx1b, proj, pa, pb, ln_g, ln_b, w["sgu_ws"], w["sgu_bs"], wpa, wpb, wout, sides=[_halves_side(g_ffn)])
    sums_ffn = chip_sums(ffn, g_ffn, theirs_ffn)
    mix = ("w_proj_a", "w_proj_b", "w_out")
    g_mix = [shard_major(_matmul_tn(ya, dpa, D, False, "wgrad_proj_a")[0]),
             shard_major(_matmul_tn(yb, dpb, D, False, "wgrad_proj_b")[0]),
             shard_major(_matmul_tn(mb, dx1b, D, False, "wgrad_out")[0])]
    (dproj, d_cw, d_cb, d_wa, d_wx, d_ba, d_bx, d_lam), (arrived_ffn, theirs_mix, sgu_parts) = _rglru_bwd(
        dya, dproj, proj, hseq, xc, gates, conv_full, w["rg_wa"], w["rg_wx"], lam,
        sides=[_scatter_side(sums_ffn), _halves_side(g_mix), _everyone_side([d_bs, d_ws])])
    mine_ffn = my_halves(ffn, sums_ffn, arrived_ffn)
    sums_mix = chip_sums(mix, g_mix, theirs_mix)
    g_in, ((wa_parts, wx_parts), other_ffn, arrived_mix) = _inproj_wgrad(
        h, dproj, sides=[_everyone_side([d_wa, d_wx]), _swap_side(mine_ffn),
                         _scatter_side(sums_mix)])
    mine_mix = my_halves(mix, sums_mix, arrived_mix)
    n_tiles = x0.shape[0] // min(TM_MM, x0.shape[0])
    n_first = max(1, n_tiles * 3 // 8)
    (dh,), (theirs_in, other_mix) = _inproj_dh(dproj, w_in_a, None, 0, n_first, "inproj_dh_a",
                                               sides=[_halves_side([g_in]), _swap_side(mine_mix)])
    sums_in = chip_sums(("w_in",), [g_in], theirs_in)
    (dh,), (arrived_in,) = _inproj_dh(dproj, w_in_a, dh, n_first, n_tiles - n_first, "inproj_dh_b",
                                      sides=[_scatter_side(sums_in)])
    mine_in = my_halves(("w_in",), sums_in, arrived_in)
    grad_x, d_mix_g = _inproj_norm_bwd(dh, x0, w["norm_mix_g"], dx1)
    rows = {"norm_mix_g": d_mix_g, "conv_b": d_cb, "rg_lambda": d_lam, "sgu_ln_g": d_lng, "sgu_ln_b": d_lnb,
            "norm_ffn_g": d_ffn_g, "norm_final_g": d_final_g, "rg_ba": d_ba, "rg_bx": d_bx}
    vec = jnp.concatenate([rows[n] for n in _VEC_ROWS]
                          + [d_cw, jnp.pad(loss, ((0, _VEC_PAD), (0, D - 1)))], axis=0)
    other_in, (vec_parts,) = _comm_only([_swap_side(mine_in), _everyone_side([vec])], "swap_w_in")
    small_parts = [vec_parts] + sgu_parts + [wa_parts, wx_parts]

    out = {}
    for n, gm, go in zip(ffn + mix + ("w_in",), mine_ffn + mine_mix + mine_in, other_ffn + other_mix + other_in):
        g, d, mo, vo = _adamw_shard(core, gm, go, w[n][0], mom[n][0], var[n][0], "adamw_" + n)
        out[n] = tuple(a[None] for a in (g, d, mo, vo))
    as_row = lambda t: {n: a.reshape(1, D) if n == "norm_final_g" else a for n, a in t.items()}
    small_out, conv_sum, loss_sum = _small_sum_adamw(small_parts, as_row(w), as_row(mom), as_row(var))
    out.update(small_out)
    out["norm_final_g"] = tuple(a.reshape(D) for a in small_out["norm_final_g"])
    conv_g = lax.dynamic_slice_in_dim(conv_sum, chip[0] * conv_cols, conv_cols, axis=1)
    d, mo, vo = _adamw_whole(w["conv_w"][0], conv_g, mom["conv_w"][0], var["conv_w"][0], "adamw_conv_w")
    out["conv_w"] = tuple(a[None] for a in (conv_g, d, mo, vo))

    return (loss_sum[0, 0], grad_x[None], *[out[n][0] for n in _WEIGHTS], *[out[n][1] for n in _WEIGHTS],
            *[out[n][2] for n in _WEIGHTS], *[out[n][3] for n in _WEIGHTS])
```

```python
import functools

import jax
import jax.numpy as jnp
from jax import lax
from jax.experimental import pallas as pl
from jax.experimental.pallas import tpu as pltpu

F32 = jnp.float32
BF16 = jnp.bfloat16
S = jax.ShapeDtypeStruct

D = 1024
N_SHARD = 4
IN_COLS = 6 * D
IN_SHARD = IN_COLS // N_SHARD
D_FF = 2816
FF_SHARD = 2 * D_FF // N_SHARD
RG_BLOCK = 256
N_RG_BLOCK = D // RG_BLOCK
CHUNK = 128
N_GROUP = 8
CONV_WIDTH = 4
RG_C = 8.0
EPS = 1e-6
ADAM_LR, ADAM_B1, ADAM_B2, ADAM_EPS, ADAM_WD, ADAM_STEP = 0.001, 0.9, 0.999, 1e-08, 0.01, 10

V7X_VMEM_BYTES = 64 * 1024 * 1024
VMEM_LIMIT = V7X_VMEM_BYTES * 3 // 4
SUBLANES = 8
MESH = pl.DeviceIdType.MESH

TM_MM = 512
TM_SCAN = 256
TM_FF = 256
TK_WGRAD = 1024


def _params(n_axes):
    return pltpu.CompilerParams(dimension_semantics=("arbitrary",) * n_axes, vmem_limit_bytes=VMEM_LIMIT)


def _resident(shape):
    nd = len(shape)
    return pl.BlockSpec(shape, lambda *_: (0,) * nd, pipeline_mode=pl.Buffered(1))


def _sig(x):
    return 1.0 / (1.0 + jnp.exp(-x))


_GELU_K2 = 2.0 * 0.7978845608028654
_GELU_C = 0.044715


def _gelu(x):
    return x * _sig(x * (_GELU_K2 + (_GELU_K2 * _GELU_C) * (x * x)))


def _gelu_and_grad(x):
    x2 = x * x
    s = _sig(x * (_GELU_K2 + (_GELU_K2 * _GELU_C) * x2))
    g = x * s
    return g, s + g * (1.0 - s) * (_GELU_K2 + (3.0 * _GELU_K2 * _GELU_C) * x2)


_EXPM1_SERIES = tuple(1.0 / f for f in (5040.0, 720.0, 120.0, 24.0, 6.0, 2.0, 1.0))


def _one_minus_exp(x):
    p = _EXPM1_SERIES[0]
    for coef in _EXPM1_SERIES[1:]:
        p = p * x + coef
    return jnp.where(x > -0.125, -x * p, 1.0 - jnp.exp(x))


def _softplus_neg(lam):
    z = -lam
    e = jnp.exp(-jnp.abs(z))
    u = 1.0 + e
    log1p = jnp.where(u == 1.0, e, jnp.log(u) * e / (u - 1.0))
    return jnp.maximum(z, 0.0) + log1p


def _rms_stats(x):
    return lax.rsqrt(jnp.mean(x * x, axis=-1, keepdims=True) + EPS)


def _rms_bwd(dy, x, g):
    rstd = _rms_stats(x)
    xhat = x * rstd
    dxhat = dy * g
    dx = rstd * (dxhat - xhat * jnp.mean(dxhat * xhat, axis=-1, keepdims=True))
    return dx, dy * xhat


def _colsum(x):
    return jnp.sum(x, axis=0, keepdims=True)


def _shift_down(x, d, fill):
    n = x.shape[0]
    if d % SUBLANES == 0:
        return jnp.concatenate([jnp.full((d, x.shape[1]), fill, x.dtype), x[:n - d]], axis=0)
    row = lax.broadcasted_iota(jnp.int32, x.shape, 0)
    return jnp.where(row < d, fill, pltpu.roll(x, d, 0))


def _shift_up(x, d, fill):
    n = x.shape[0]
    if d % SUBLANES == 0:
        return jnp.concatenate([x[d:], jnp.full((d, x.shape[1]), fill, x.dtype)], axis=0)
    row = lax.broadcasted_iota(jnp.int32, x.shape, 0)
    return jnp.where(row >= n - d, fill, pltpu.roll(x, n - d, 0))


def _scan(a, b, shift):
    d = 1
    while d < a.shape[0]:
        b = a * shift(b, d, 0.0) + b
        a = a * shift(a, d, 1.0)
        d *= 2
    return a, b


LANES = 128


def _scan_tile(a, b, outside, a_s, b_s, h_s, reverse):
    tm = a.shape[0]
    groups = tm // SUBLANES
    order = list(range(SUBLANES - 1, -1, -1) if reverse else range(SUBLANES))
    shift = _shift_up if reverse else _shift_down
    edge = groups - 1 if reverse else 0
    for j in range(D // LANES):
        a_s[j] = a[:, LANES * j:LANES * (j + 1)]
        b_s[j] = b[:, LANES * j:LANES * (j + 1)]
    for j in range(D // LANES):
        def slab(ref, k):
            return ref[j, pl.ds(k, groups, stride=SUBLANES), :]

        ga, gb = slab(a_s, order[0]), slab(b_s, order[0])
        for k in order[1:]:
            ak = slab(a_s, k)
            gb = ak * gb + slab(b_s, k)
            ga = ak * ga
        ga, gb = _scan(ga, gb, shift)
        h_out = outside[:, LANES * j:LANES * (j + 1)]
        group_end = ga * h_out + gb
        row = lax.broadcasted_iota(jnp.int32, (groups, LANES), 0)
        h = jnp.where(row == edge, h_out, shift(group_end, 1, 0.0))
        for k in order:
            h = slab(a_s, k) * h + slab(b_s, k)
            h_s[j, pl.ds(k, groups, stride=SUBLANES), :] = h
    return jnp.concatenate([h_s[j] for j in range(D // LANES)], axis=1)


def _dot(a, b):
    return jnp.dot(a, b, preferred_element_type=F32)


def _dot_nt(a, b):
    return lax.dot_general(a, b, (((1,), (1,)), ((), ())), preferred_element_type=F32)


def _dot_tn(a, b):
    return lax.dot_general(a, b, (((0,), (0,)), ((), ())), preferred_element_type=F32)


_ANY = pl.BlockSpec(memory_space=pl.ANY)


def _position():
    return lax.axis_index("x"), lax.axis_index("y"), lax.axis_index("c")


def _other_chips(x, y):
    return [(1 - x, y), (x, 1 - y), (1 - x, 1 - y)]


class _Side:
    def __init__(self, inputs, out_shapes, n_sems, make, continues=()):
        self.inputs, self.out_shapes, self.n_sems, self.make = list(inputs), list(out_shapes), n_sems, make
        self.continues = list(continues)


MID_STEP = 0.625
LATE_MID_STEP = 0.875


def _call(body, *, name, grid, in_specs, out_specs, out_shape, args, scratch_shapes=(), sides=(), aliases=None,
          scalars=None, mid=MID_STEP):
    n_in, n_out, n_scr = len(in_specs), len(out_specs), len(scratch_shapes)
    n_scalar = 0 if scalars is None else 1
    side_in = [len(s.inputs) + len(s.continues) for s in sides]
    side_out = [len(s.out_shapes) for s in sides]
    all_aliases = {k + n_scalar: v for k, v in (aliases or {}).items()}
    for idx, s in enumerate(sides):
        for k in range(len(s.continues)):
            operand = n_scalar + n_in + sum(side_in[:idx]) + len(s.inputs) + k
            all_aliases[operand] = n_out + sum(side_out[:idx]) + k

    def wrapped(*refs):
        refs = list(refs)
        take = lambda k: [refs.pop(0) for _ in range(k)]
        ins = take(n_scalar) + take(n_in)
        sins = [take(k) for k in side_in]
        outs = take(n_out)
        souts = [take(k) for k in side_out]
        scr = take(n_scr)
        sems = [take(3) for _ in sides]
        def run(phase):
            for s, si, so, sem in zip(sides, sins, souts, sems):
                for thunk in s.make(si[:len(s.inputs)], so, *sem)[phase]:
                    thunk()

        if sides:
            n_steps = functools.reduce(lambda a, b: a * b, grid)
            step = functools.reduce(lambda a, b: a + b, [
                pl.program_id(a) * functools.reduce(lambda p, q: p * q, grid[a + 1:], 1) for a in range(len(grid))])
            pl.when(step == 0)(lambda: run(0))
        body(*ins, *outs, *scr)
        if sides:
            pl.when(step == int(mid * (n_steps - 1)))(lambda: run(1))
            pl.when(step == n_steps - 1)(lambda: run(2))

    grid_spec = pltpu.PrefetchScalarGridSpec(
        num_scalar_prefetch=n_scalar, grid=grid,
        in_specs=list(in_specs) + [_ANY] * sum(side_in),
        out_specs=list(out_specs) + [_ANY] * sum(side_out),
        scratch_shapes=list(scratch_shapes) + [pltpu.SemaphoreType.DMA((s.n_sems,)) for s in sides for _ in range(3)])
    res = pl.pallas_call(
        wrapped, name=name, grid_spec=grid_spec,
        out_shape=list(out_shape) + [o for s in sides for o in s.out_shapes],
        input_output_aliases=all_aliases,
        compiler_params=_params(len(grid)),
    )(*([scalars] if n_scalar else []), *args, *[a for s in sides for a in s.inputs + s.continues])
    main, rest, per_side = list(res[:n_out]), list(res[n_out:]), []
    for k in side_out:
        per_side.append(rest[:k])
        rest = rest[k:]
    return main, per_side


def _comm_only(sides, name):
    def body():
        pass

    return _call(body, name=name, grid=(1,), in_specs=[], out_specs=[], out_shape=[], args=[], sides=sides)[1]


def _remote(src, dst, send, recv, k, device):
    return pltpu.make_async_remote_copy(src_ref=src, dst_ref=dst, send_sem=send.at[k], recv_sem=recv.at[k],
                                        device_id=device, device_id_type=MESH)


def _both_ways(copy, keys):
    return [lambda k=k: copy(k).start() for k in keys], [], [lambda k=k: copy(k).wait() for k in keys]


def _gather_side(shards):
    n = len(shards)

    def make(ins, outs, send, recv, local):
        x, y, c = _position()
        mine = 2 * x + y
        chips = _other_chips(x, y)
        pairs = [(w, j) for w in range(n) for j in range(3)]

        def own(w):
            return pltpu.make_async_copy(ins[w], outs[w].at[mine], local.at[w])

        def push(w, j):
            return _remote(ins[w], outs[w].at[mine], send, recv, 3 * w + j, (*chips[j], c))

        def arrival(w, j):
            px, py = chips[j]
            return _remote(ins[w], outs[w].at[2 * px + py], send, recv, 3 * w + j, (px, py, c))

        starts = [lambda w=w: own(w).start() for w in range(n)] + [lambda w=w, j=j: push(w, j).start() for w, j in pairs]
        waits = ([lambda w=w, j=j: arrival(w, j).wait_recv() for w, j in pairs]
                 + [lambda w=w, j=j: push(w, j).wait_send() for w, j in pairs]
                 + [lambda w=w: own(w).wait() for w in range(n)])
        return starts, [], waits

    return _Side(shards, [S((N_SHARD,) + s.shape, s.dtype) for s in shards], 3 * n, make)


def _gather_half_side(shards, relations=(0, 1, 2), into=None):
    n = len(shards)

    def make(ins, outs, send, recv, local):
        x, y, c = _position()
        mine = 2 * x + y
        chips = _other_chips(x, y)
        pairs = [(w, j) for w in range(n) for j in relations]

        def rows(w, core):
            half = ins[w].shape[0] // 2
            return pl.ds(core * half, half)

        def own(w):
            return pltpu.make_async_copy(ins[w], outs[w].at[mine], local.at[w])

        def push(w, j):
            return _remote(ins[w].at[rows(w, c), :], outs[w].at[mine, rows(w, c), :], send, recv, 3 * w + j,
                           (*chips[j], c))

        def landed(w, j, core):
            px, py = chips[j]
            return outs[w].at[2 * px + py, rows(w, core), :]

        def arrival(w, j):
            return _remote(ins[w].at[rows(w, c), :], landed(w, j, c), send, recv, 3 * w + j, (*chips[j], c))

        def passed(w, j, core):
            return _remote(landed(w, j, core), landed(w, j, core), send, recv, 3 * n + 3 * w + j, (x, y, 1 - c))

        owns = range(n) if into is None else ()
        starts = [lambda w=w: own(w).start() for w in owns] + [lambda w=w, j=j: push(w, j).start() for w, j in pairs]
        mids = [t for w, j in pairs for t in (lambda w=w, j=j: arrival(w, j).wait_recv(),
                                              lambda w=w, j=j: passed(w, j, c).start())]
        waits = ([lambda w=w, j=j: passed(w, j, 1 - c).wait_recv() for w, j in pairs]
                 + [lambda w=w, j=j: passed(w, j, c).wait_send() for w, j in pairs]
                 + [lambda w=w, j=j: push(w, j).wait_send() for w, j in pairs]
                 + [lambda w=w: own(w).wait() for w in owns])
        return starts, mids, waits

    return _Side(shards, [S((N_SHARD,) + s.shape, s.dtype) for s in shards], 6 * n, make, continues=into or ())


def _halves_side(grads):
    n = len(grads)

    def make(ins, outs, send, recv, local):
        x, y, c = _position()

        def copy(w):
            half = ins[w].shape[1] // 2
            return _remote(ins[w].at[:, pl.ds((1 - c) * half, half), :], outs[w], send, recv, w, (x, y, 1 - c))

        return _both_ways(copy, range(n))

    return _Side(grads, [S((N_SHARD, g.shape[1] // 2, g.shape[2]), F32) for g in grads], n, make)


def _scatter_side(partials):
    n = len(partials)

    def make(ins, outs, send, recv, local):
        x, y, c = _position()
        chips = _other_chips(x, y)

        def copy(k):
            w, j = divmod(k, 3)
            px, py = chips[j]
            return _remote(ins[w].at[2 * px + py], outs[w].at[j], send, recv, k, (px, py, c))

        return _both_ways(copy, range(3 * n))

    return _Side(partials, [S((3,) + p.shape[1:], p.dtype) for p in partials], 3 * n, make)


def _swap_side(halves):
    n = len(halves)

    def make(ins, outs, send, recv, local):
        x, y, c = _position()
        return _both_ways(lambda w: _remote(ins[w], outs[w], send, recv, w, (x, y, 1 - c)), range(n))

    return _Side(halves, [S(h.shape, h.dtype) for h in halves], n, make)


N_DEVICE = 8


def _everyone_side(arrays):
    n = len(arrays)
    peers = N_DEVICE - 1

    def make(ins, outs, send, recv, local):
        x, y, c = _position()
        mine = 4 * x + 2 * y + c
        pairs = [(w, k) for w in range(n) for k in range(1, N_DEVICE)]

        def peer(k):
            return (1 - x if k & 4 else x, 1 - y if k & 2 else y, 1 - c if k & 1 else c)

        def own(w):
            return pltpu.make_async_copy(ins[w], outs[w].at[mine], local.at[w])

        def push(w, k):
            return _remote(ins[w], outs[w].at[mine], send, recv, peers * w + k - 1, peer(k))

        def arrival(w, k):
            px, py, pc = peer(k)
            return _remote(ins[w], outs[w].at[4 * px + 2 * py + pc], send, recv, peers * w + k - 1, (px, py, pc))

        starts = [lambda w=w: own(w).start() for w in range(n)] + [lambda w=w, k=k: push(w, k).start() for w, k in pairs]
        waits = ([lambda w=w, k=k: arrival(w, k).wait_recv() for w, k in pairs]
                 + [lambda w=w, k=k: push(w, k).wait_send() for w, k in pairs]
                 + [lambda w=w: own(w).wait() for w in range(n)])
        return starts, [], waits

    return _Side(arrays, [S((N_DEVICE,) + a.shape, a.dtype) for a in arrays], peers * n, make)


def _inproj_own(chip, x, g, w_shard, sides=()):
    T = x.shape[0]
    tm = min(TM_MM, T)

    def body(chip_ref, x_ref, g_ref, w_ref, proj_ref, h_ref):
        xv = x_ref[...]
        h = (xv * _rms_stats(xv) * g_ref[...]).astype(BF16)
        h_ref[...] = h
        proj_ref[...] = _dot(h, w_ref[...]).astype(BF16)

    return _call(
        body, name="inproj_own", grid=(T // tm,),
        in_specs=[pl.BlockSpec((tm, D), lambda i, c: (i, 0)), pl.BlockSpec((1, D), lambda i, c: (0, 0)),
                  pl.BlockSpec((D, IN_SHARD), lambda i, c: (0, 0), pipeline_mode=pl.Buffered(1))],
        out_specs=[pl.BlockSpec((tm, IN_SHARD), lambda i, c: (i, c[0])), pl.BlockSpec((tm, D), lambda i, c: (i, 0))],
        out_shape=[S((T, IN_COLS), BF16), S((T, D), BF16)],
        args=(x, g, w_shard), sides=sides, scalars=chip, mid=LATE_MID_STEP)


def _inproj_rest(chip, proj, h, w_in, first, count, name, sides=(), mid=MID_STEP):
    T = h.shape[0]
    tm = min(TM_MM, T)

    def body(chip_ref, _, h_ref, w_ref, proj_ref):
        proj_ref[...] = _dot(h_ref[...], w_ref[0]).astype(BF16)

    def other(p, c):
        return jnp.bitwise_xor(c[0], first + p)

    return _call(
        body, name=name, grid=(count, T // tm),
        in_specs=[_ANY, pl.BlockSpec((tm, D), lambda p, i, c: (i, 0)),
                  pl.BlockSpec((1, D, IN_SHARD), lambda p, i, c: (other(p, c), 0, 0))],
        out_specs=[pl.BlockSpec((tm, IN_SHARD), lambda p, i, c: (i, other(p, c)))],
        out_shape=[S((T, IN_COLS), BF16)],
        args=(proj, h, w_in), sides=sides, scalars=chip, aliases={0: 0}, mid=mid)


def _rg_gates(xc, wa_ref, wx_ref, ba, bx, sp):
    xb = xc.astype(BF16)
    blocks = [xb[:, RG_BLOCK * j:RG_BLOCK * (j + 1)] for j in range(N_RG_BLOCK)]
    r = _sig(jnp.concatenate([_dot(blocks[j], wa_ref[j]) for j in range(N_RG_BLOCK)], axis=1) + ba)
    gi = _sig(jnp.concatenate([_dot(blocks[j], wx_ref[j]) for j in range(N_RG_BLOCK)], axis=1) + bx)
    log_a = (-RG_C) * r * sp
    a = jnp.exp(log_a)
    m = jnp.sqrt(_one_minus_exp(2.0 * log_a))
    return xb, r, gi, a, m


N_GATES = 4
HEADS_PER_BLOCK = 4
HEAD_DIM = RG_BLOCK // HEADS_PER_BLOCK
_RG_BLOCKS_BF16 = pltpu.VMEM((N_RG_BLOCK, RG_BLOCK, RG_BLOCK), BF16)


def _fill_blockdiag(heads_ref, blocks):
    blocks[...] = jnp.zeros_like(blocks)
    for j in range(N_RG_BLOCK):
        for h in range(HEADS_PER_BLOCK):
            sl = slice(HEAD_DIM * h, HEAD_DIM * (h + 1))
            blocks[j, sl, sl] = heads_ref[0, HEADS_PER_BLOCK * j + h].astype(BF16)


def _rglru_fwd(proj, conv_w, conv_b, rg_wa, rg_wx, ba, bx, lam, sides=()):
    T = proj.shape[0]
    tm = min(TM_SCAN, T)

    def body(rx_ref, gate_ref, cw_ref, cb_ref, wah_ref, wxh_ref, ba_ref, bx_ref, lam_ref,
             ya_ref, xc_ref, h_ref, gates_ref, ext, hc, a_s, b_s, h_s, wa_ref, wx_ref):
        @pl.when(pl.program_id(0) == 0)
        def _():
            ext[0:SUBLANES, :] = jnp.zeros((SUBLANES, D), F32)
            hc[...] = jnp.zeros((SUBLANES, D), F32)
            _fill_blockdiag(wah_ref, wa_ref)
            _fill_blockdiag(wxh_ref, wx_ref)

        ext[SUBLANES:SUBLANES + tm, :] = rx_ref[...].astype(F32)
        xc = cb_ref[...]
        for k in range(CONV_WIDTH):
            xc = xc + ext[pl.ds(SUBLANES - (CONV_WIDTH - 1) + k, tm), :] * cw_ref[k:k + 1, :]
        ext[0:SUBLANES, :] = ext[tm:tm + SUBLANES, :]
        xc_ref[...] = xc
        _, r, gi, a, m = _rg_gates(xc, wa_ref, wx_ref, ba_ref[...], bx_ref[...], _softplus_neg(lam_ref[...]))
        for k, val in enumerate((r, gi, a, m)):
            gates_ref[:, D * k:D * (k + 1)] = val
        h = _scan_tile(a, m * (gi * xc), hc[0:1, :], a_s, b_s, h_s, reverse=False)
        hc[...] = jnp.broadcast_to(h[tm - 1:tm, :], (SUBLANES, D))
        h_ref[...] = h
        ya_ref[...] = (_gelu(gate_ref[...].astype(F32)) * h).astype(BF16)

    vec = pl.BlockSpec((1, D), lambda i: (0, 0))
    heads = pl.BlockSpec(rg_wa.shape, lambda i: (0, 0, 0, 0))
    tile = pl.BlockSpec((tm, D), lambda i: (i, 0))
    return _call(
        body, name="rglru_fwd", grid=(T // tm,),
        in_specs=[pl.BlockSpec((tm, D), lambda i: (i, 0)), pl.BlockSpec((tm, D), lambda i: (i, 1)),
                  pl.BlockSpec((CONV_WIDTH, D), lambda i: (0, 0)), vec, heads, heads, vec, vec, vec],
        out_specs=[tile, tile, tile, pl.BlockSpec((tm, N_GATES * D), lambda i: (i, 0))],
        out_shape=[S((T, D), BF16), S((T, D), F32), S((T, D), F32), S((T, N_GATES * D), F32)],
        scratch_shapes=[pltpu.VMEM((tm + SUBLANES, D), F32), pltpu.VMEM((SUBLANES, D), F32)]
        + [pltpu.VMEM((D // LANES, tm, LANES), F32)] * 3 + [_RG_BLOCKS_BF16] * 2,
        args=(proj, proj, conv_w, conv_b, rg_wa, rg_wx, ba, bx, lam), sides=sides, mid=LATE_MID_STEP)


def _layer_norm_stats(v):
    mu = jnp.mean(v, axis=-1, keepdims=True)
    vc = v - mu
    rstd = lax.rsqrt(jnp.mean(vc * vc, axis=-1, keepdims=True) + EPS)
    return vc * rstd, rstd


def _sgu_mix(w_ref, vnb, bst_ref, n_chunk):
    cols = []
    for g in range(N_GROUP):
        vg = vnb[:, CHUNK * g:CHUNK * (g + 1)].reshape(n_chunk, CHUNK, CHUNK)
        wb = jnp.broadcast_to(w_ref[g][None], (n_chunk, CHUNK, CHUNK))
        mg = lax.dot_general(wb, vg, (((2,), (1,)), ((0,), (0,))), preferred_element_type=F32)
        mg = mg + bst_ref[:, g:g + 1][None]
        cols.append(mg.reshape(n_chunk * CHUNK, CHUNK))
    return jnp.concatenate(cols, axis=1)


def _causal_mask():
    return (lax.broadcasted_iota(jnp.int32, (CHUNK, CHUNK), 0) >= lax.broadcasted_iota(jnp.int32, (CHUNK, CHUNK), 1))


def _fill_sgu_weights(ws_ref, bs_ref, w_tril, bs_t, w_tril_t=None):
    keep = _causal_mask()
    for g in range(N_GROUP):
        wg = jnp.where(keep, ws_ref[0, g], 0.0)
        w_tril[g] = wg.astype(BF16)
        if w_tril_t is not None:
            w_tril_t[g] = wg.T.astype(BF16)
    bs_t[...] = bs_ref[0].T


_SGU_W_BF16 = pltpu.VMEM((N_GROUP, CHUNK, CHUNK), BF16)
_SGU_BT = pltpu.VMEM((CHUNK, N_GROUP), F32)


def _sgu_merge_fwd(x, proj, ya, ln_g, ln_b, sgu_ws, sgu_bs, wpa, wpb, wout):
    T = x.shape[0]
    tm = min(TM_FF, T)
    n_chunk = tm // CHUNK

    def body(x_ref, uv_ref, gab_ref, ya_ref, g_ref, b_ref, ws_ref, bs_ref, wpa_ref, wpb_ref, wout_ref,
             yb_ref, pa_ref, pb_ref, mb_ref, x1_ref, w_ref, bst_ref):
        @pl.when(pl.program_id(0) == 0)
        def _():
            _fill_sgu_weights(ws_ref, bs_ref, w_ref, bst_ref)

        vhat, _ = _layer_norm_stats(_gelu(uv_ref[:, D:2 * D].astype(F32)))
        vnb = (vhat * g_ref[...] + b_ref[...]).astype(BF16)
        yb = (_gelu(uv_ref[:, 0:D].astype(F32)) * _sgu_mix(w_ref, vnb, bst_ref, n_chunk)).astype(BF16)
        yb_ref[...] = yb
        pa = _dot(ya_ref[...], wpa_ref[...])
        pb = _dot(yb, wpb_ref[...])
        pa_ref[...] = pa.astype(BF16)
        pb_ref[...] = pb.astype(BF16)
        mb = (_sig(gab_ref[:, 0:D].astype(F32)) * pa + _sig(gab_ref[:, D:2 * D].astype(F32)) * pb).astype(BF16)
        mb_ref[...] = mb
        x1_ref[...] = x_ref[...] + _dot(mb, wout_ref[...])

    tile = pl.BlockSpec((tm, D), lambda i: (i, 0))
    vec = pl.BlockSpec((1, D), lambda i: (0, 0))
    w = _resident((D, D))
    return pl.pallas_call(
        body, name="sgu_merge_fwd", grid=(T // tm,),
        in_specs=[tile, pl.BlockSpec((tm, 2 * D), lambda i: (i, 1)), pl.BlockSpec((tm, 2 * D), lambda i: (i, 2)), tile,
                  vec, vec, pl.BlockSpec(sgu_ws.shape, lambda i: (0, 0, 0, 0)),
                  pl.BlockSpec(sgu_bs.shape, lambda i: (0, 0, 0)), w, w, w],
        out_specs=[tile, tile, tile, tile, tile],
        out_shape=[S((T, D), BF16), S((T, D), BF16), S((T, D), BF16), S((T, D), BF16), S((T, D), F32)],
        scratch_shapes=[_SGU_W_BF16, _SGU_BT],
        compiler_params=_params(1),
    )(x, proj, proj, ya, ln_g, ln_b, sgu_ws, sgu_bs, wpa, wpb, wout)


def _ffn_fwd_loss(x1, g, w_gu, w_down, g_final, target):
    T = x1.shape[0]
    tm = min(TM_FF, T)

    def body(x_ref, g_ref, wgu_ref, wd_ref, gf_ref, t_ref,
             h2_ref, gu_ref, act_ref, dx2_ref, dx2b_ref, loss_ref, dg_ref):
        @pl.when(pl.program_id(0) == 0)
        def _():
            loss_ref[...] = jnp.zeros_like(loss_ref)
            dg_ref[...] = jnp.zeros_like(dg_ref)

        xv = x_ref[...]
        h2 = (xv * _rms_stats(xv) * g_ref[...]).astype(BF16)
        h2_ref[...] = h2
        x2 = xv
        for k in range(N_SHARD // 2):
            cols = slice(FF_SHARD * k, FF_SHARD * (k + 1))
            gate = _dot(h2, wgu_ref[k])
            up = _dot(h2, wgu_ref[k + N_SHARD // 2])
            gu_ref[:, cols] = gate.astype(BF16)
            gu_ref[:, D_FF + FF_SHARD * k:D_FF + FF_SHARD * (k + 1)] = up.astype(BF16)
            act = (gate * _sig(gate) * up).astype(BF16)
            act_ref[:, cols] = act
            x2 = x2 + _dot(act, wd_ref[cols, :])
        gf = gf_ref[...]
        err = x2 * _rms_stats(x2) * gf - t_ref[...]
        loss_ref[...] += 0.5 * jnp.sum(jnp.mean(err * err, axis=-1, keepdims=True), axis=0, keepdims=True)
        dx2, dg_rows = _rms_bwd(err * (1.0 / D), x2, gf)
        dg_ref[...] += _colsum(dg_rows)
        dx2_ref[...] = dx2
        dx2b_ref[...] = dx2.astype(BF16)

    tile = pl.BlockSpec((tm, D), lambda i: (i, 0))
    vec = pl.BlockSpec((1, D), lambda i: (0, 0))
    return pl.pallas_call(
        body, name="ffn_fwd_loss", grid=(T // tm,),
        in_specs=[tile, vec, _resident((N_SHARD, D, FF_SHARD)), _resident((D_FF, D)), vec, tile],
        out_specs=[tile, pl.BlockSpec((tm, 2 * D_FF), lambda i: (i, 0)), pl.BlockSpec((tm, D_FF), lambda i: (i, 0)),
                   tile, tile, pl.BlockSpec((1, 1), lambda i: (0, 0)), vec],
        out_shape=[S((T, D), BF16), S((T, 2 * D_FF), BF16), S((T, D_FF), BF16), S((T, D), F32), S((T, D), BF16),
                   S((1, 1), F32), S((1, D), F32)],
        compiler_params=_params(1),
    )(x1, g, w_gu, w_down, g_final, target)


def _ffn_bwd(dx2, dx2b, gu, w_down, w_gu, x1, g):
    T = x1.shape[0]
    tm = min(TM_FF, T)

    def body(dx2_ref, dx2b_ref, gu_ref, wd_ref, wgu_ref, x_ref, g_ref, dgu_ref, dx1_ref, dx1b_ref, dg_ref):
        @pl.when(pl.program_id(0) == 0)
        def _():
            dg_ref[...] = jnp.zeros_like(dg_ref)

        dxb = dx2b_ref[...]
        dh2 = jnp.zeros((tm, D), F32)
        for k in range(N_SHARD // 2):
            cols = slice(FF_SHARD * k, FF_SHARD * (k + 1))
            up_cols = slice(D_FF + FF_SHARD * k, D_FF + FF_SHARD * (k + 1))
            dact = _dot_nt(dxb, wd_ref[cols, :])
            gate = gu_ref[:, cols].astype(F32)
            sg = _sig(gate)
            dgate = (dact * gu_ref[:, up_cols].astype(F32) * (sg * (1.0 + gate * (1.0 - sg)))).astype(BF16)
            dup = (dact * (gate * sg)).astype(BF16)
            dgu_ref[:, cols] = dgate
            dgu_ref[:, up_cols] = dup
            dh2 = dh2 + _dot_nt(dgate, wgu_ref[k]) + _dot_nt(dup, wgu_ref[k + N_SHARD // 2])
        dx, dg_rows = _rms_bwd(dh2, x_ref[...], g_ref[...])
        dg_ref[...] += _colsum(dg_rows)
        dx1 = dx2_ref[...] + dx
        dx1_ref[...] = dx1
        dx1b_ref[...] = dx1.astype(BF16)

    tile = pl.BlockSpec((tm, D), lambda i: (i, 0))
    wide = pl.BlockSpec((tm, 2 * D_FF), lambda i: (i, 0))
    vec = pl.BlockSpec((1, D), lambda i: (0, 0))
    return pl.pallas_call(
        body, name="ffn_bwd", grid=(T // tm,),
        in_specs=[tile, tile, wide, _resident((D_FF, D)), _resident((N_SHARD, D, FF_SHARD)), tile, vec],
        out_specs=[wide, tile, tile, vec],
        out_shape=[S((T, 2 * D_FF), BF16), S((T, D), F32), S((T, D), BF16), S((1, D), F32)],
        compiler_params=_params(1),
    )(dx2, dx2b, gu, w_down, w_gu, x1, g)


def _matmul_tn(a, b, tn, shard_major, name, sides=()):
    T, M = a.shape
    N = b.shape[1]
    tk = min(TK_WGRAD, T)

    def body(a_ref, b_ref, o_ref):
        @pl.when(pl.program_id(1) == 0)
        def _():
            o_ref[...] = jnp.zeros_like(o_ref)

        acc = _dot_tn(a_ref[...], b_ref[...])
        if shard_major:
            o_ref[0] += acc
        else:
            o_ref[...] += acc

    if shard_major:
        out_spec, out_shape = pl.BlockSpec((1, M, tn), lambda j, k: (j, 0, 0)), S((N // tn, M, tn), F32)
    else:
        out_spec, out_shape = pl.BlockSpec((M, tn), lambda j, k: (0, j)), S((M, N), F32)
    (out,), side_outs = _call(
        body, name=name, grid=(N // tn, T // tk),
        in_specs=[pl.BlockSpec((tk, M), lambda j, k: (k, 0)), pl.BlockSpec((tk, tn), lambda j, k: (k, j))],
        out_specs=[out_spec], out_shape=[out_shape], args=(a, b), sides=sides)
    return out, side_outs


PIECE = IN_SHARD // 3
N_PIECE = IN_COLS // PIECE
DPROJ_ROTATION = 2 * D // PIECE


def _merge_sgu_bwd(dx1b, proj, pa, pb, ln_g, ln_b, sgu_ws, sgu_bs, wpa, wpb, wout, sides=()):
    T = dx1b.shape[0]
    tm = min(TM_FF, T)
    n_chunk = tm // CHUNK

    def body(dx_ref, uv_ref, gab_ref, pa_ref, pb_ref, g_ref, b_ref, ws_ref, bs_ref, wpa_ref, wpb_ref, wout_ref,
             dpa_ref, dpb_ref, dya_ref, dp_ref, dw_ref, dbs_ref, dg_ref, db_ref, w_ref, wt_ref, bst_ref):
        @pl.when(pl.program_id(0) == 0)
        def _():
            for ref in (dw_ref, dbs_ref, dg_ref, db_ref):
                ref[...] = jnp.zeros_like(ref)
            _fill_sgu_weights(ws_ref, bs_ref, w_ref, bst_ref, wt_ref)

        dm = _dot_nt(dx_ref[...], wout_ref[...])
        sa = _sig(gab_ref[:, 0:D].astype(F32))
        sb = _sig(gab_ref[:, D:2 * D].astype(F32))
        dpa = (dm * sa).astype(BF16)
        dpb = (dm * sb).astype(BF16)
        dpa_ref[...] = dpa
        dpb_ref[...] = dpb
        dp_ref[:, 2 * D:3 * D] = (dm * pa_ref[...].astype(F32) * (sa * (1.0 - sa))).astype(BF16)
        dp_ref[:, 3 * D:4 * D] = (dm * pb_ref[...].astype(F32) * (sb * (1.0 - sb))).astype(BF16)
        dya_ref[...] = _dot_nt(dpa, wpa_ref[...]).astype(BF16)
        dyb_v = _dot_nt(dpb, wpb_ref[...])

        gu, dgu = _gelu_and_grad(uv_ref[:, 0:D].astype(F32))
        gv, dgv = _gelu_and_grad(uv_ref[:, D:2 * D].astype(F32))
        vhat, rstd = _layer_norm_stats(gv)
        lng = g_ref[...]
        vnb = (vhat * lng + b_ref[...]).astype(BF16)
        mixed = _sgu_mix(w_ref, vnb, bst_ref, n_chunk)
        dp_ref[:, 0:D] = (dyb_v * mixed * dgu).astype(BF16)
        dmix = dyb_v * gu
        dmb = dmix.astype(BF16)
        keep = _causal_mask()
        dvn_cols, dbs_rows = [], []
        for g in range(N_GROUP):
            sl = slice(CHUNK * g, CHUNK * (g + 1))
            dmg = dmb[:, sl].reshape(n_chunk, CHUNK, CHUNK)
            vg = vnb[:, sl].reshape(n_chunk, CHUNK, CHUNK)
            wtb = jnp.broadcast_to(wt_ref[g][None], (n_chunk, CHUNK, CHUNK))
            dvn = lax.dot_general(wtb, dmg, (((2,), (1,)), ((0,), (0,))), preferred_element_type=F32)
            dvn_cols.append(dvn.reshape(tm, CHUNK))
            dw = lax.dot_general(dmg, vg, (((2,), (2,)), ((0,), (0,))), preferred_element_type=F32)
            dw_ref[g] += jnp.where(keep, jnp.sum(dw, axis=0), 0.0)
            per_token = jnp.sum(dmix[:, sl], axis=1)
            dbs_rows.append(jnp.sum(per_token.reshape(n_chunk, CHUNK), axis=0, keepdims=True))
        dbs_ref[...] += jnp.concatenate(dbs_rows, axis=0)
        dvn = jnp.concatenate(dvn_cols, axis=1)
        dg_ref[...] += _colsum(dvn * vhat)
        db_ref[...] += _colsum(dvn)
        dvhat = dvn * lng
        dgv_in = rstd * (dvhat - jnp.mean(dvhat, axis=-1, keepdims=True)
                         - vhat * jnp.mean(dvhat * vhat, axis=-1, keepdims=True))
        dp_ref[:, D:2 * D] = (dgv_in * dgv).astype(BF16)

    tile = pl.BlockSpec((tm, D), lambda i: (i, 0))
    vec = pl.BlockSpec((1, D), lambda i: (0, 0))
    w = _resident((D, D))
    wsp = pl.BlockSpec((N_GROUP, CHUNK, CHUNK), lambda i: (0, 0, 0))
    return _call(
        body, name="merge_sgu_bwd", grid=(T // tm,),
        in_specs=[tile, pl.BlockSpec((tm, 2 * D), lambda i: (i, 1)), pl.BlockSpec((tm, 2 * D), lambda i: (i, 2)),
                  tile, tile, vec, vec, pl.BlockSpec(sgu_ws.shape, lambda i: (0, 0, 0, 0)),
                  pl.BlockSpec(sgu_bs.shape, lambda i: (0, 0, 0)), w, w, w],
        out_specs=[tile, tile, tile, pl.BlockSpec((tm, 4 * D), lambda i: (i, 0)), wsp,
                   pl.BlockSpec((N_GROUP, CHUNK), lambda i: (0, 0)), vec, vec],
        out_shape=[S((T, D), BF16), S((T, D), BF16), S((T, D), BF16), S((T, IN_COLS), BF16),
                   S((N_GROUP, CHUNK, CHUNK), F32), S((N_GROUP, CHUNK), F32), S((1, D), F32), S((1, D), F32)],
        scratch_shapes=[_SGU_W_BF16, _SGU_W_BF16, _SGU_BT],
        args=(dx1b, proj, proj, pa, pb, ln_g, ln_b, sgu_ws, sgu_bs, wpa, wpb, wout), sides=sides)


def _rglru_bwd(dya, dproj, proj, hseq, xc, gates, conv_w, rg_wa, rg_wx, lam, sides=()):
    T = dya.shape[0]
    tm = min(TM_SCAN, T)
    n = T // tm
    per8 = tm // SUBLANES

    def body(dya_ref, _, rx_ref, rxp_ref, gate_ref, h_ref, hp_ref, xc_ref, gates_ref, cw_ref, wah_ref, wxh_ref,
             lam_ref, dab_ref, dcw_ref, dcb_ref, dwah_ref, dwxh_ref, dba_ref, dbx_ref, dlam_ref,
             hext, rext, dext, carry_a, carry_dh, a_s, b_s, h_s, wa_ref, wx_ref, dwa_ref, dwx_ref):
        i = pl.program_id(0)
        first_tile = i == n - 1

        @pl.when(i == 0)
        def _():
            for ref in (dcw_ref, dcb_ref, dwa_ref, dwx_ref, dba_ref, dbx_ref, dlam_ref, carry_a, carry_dh):
                ref[...] = jnp.zeros_like(ref)
            dext[tm:tm + SUBLANES, :] = jnp.zeros((SUBLANES, D), F32)
            _fill_blockdiag(wah_ref, wa_ref)
            _fill_blockdiag(wxh_ref, wx_ref)

        gel, dgel = _gelu_and_grad(gate_ref[...].astype(F32))
        dya_v = dya_ref[...].astype(F32)
        hseq_v = h_ref[...]
        dgate = dya_v * hseq_v * dgel
        xcv = xc_ref[...]
        lam_v = lam_ref[...]
        sp = _softplus_neg(lam_v)
        xb = xcv.astype(BF16)
        r, gi, a, m = (gates_ref[:, D * k:D * (k + 1)] for k in range(N_GATES))

        row = lax.broadcasted_iota(jnp.int32, (tm, D), 0)
        c = jnp.where(row == tm - 1, carry_a[0:1, :], _shift_up(a, 1, 0.0))
        dH = _scan_tile(c, dya_v * gel, carry_dh[0:1, :], a_s, b_s, h_s, reverse=True)
        carry_a[...] = jnp.broadcast_to(a[0:1, :], (SUBLANES, D))
        carry_dh[...] = jnp.broadcast_to(dH[0:1, :], (SUBLANES, D))

        hext[0:SUBLANES, :] = jnp.where(first_tile, 0.0, hp_ref[...])
        hext[SUBLANES:SUBLANES + tm, :] = hseq_v
        h_prev = hext[pl.ds(SUBLANES - 1, tm), :]

        d_m = dH * (gi * xcv)
        d_la = dH * h_prev * a - d_m * (a * a) / m
        d_ia = dH * m * xcv * (gi * (1.0 - gi))
        d_ra = d_la * ((-RG_C) * sp) * (r * (1.0 - r))
        dlam_ref[...] += _colsum(d_la * ((-RG_C) * r)) * (-_sig(-lam_v))
        dba_ref[...] += _colsum(d_ra)
        dbx_ref[...] += _colsum(d_ia)
        drab = d_ra.astype(BF16)
        diab = d_ia.astype(BF16)
        dxc_cols = []
        for j in range(N_RG_BLOCK):
            sl = slice(RG_BLOCK * j, RG_BLOCK * (j + 1))
            dxc_cols.append(_dot_nt(drab[:, sl], wa_ref[j]) + _dot_nt(diab[:, sl], wx_ref[j]))
            dwa_ref[j] += _dot_tn(xb[:, sl], drab[:, sl])
            dwx_ref[j] += _dot_tn(xb[:, sl], diab[:, sl])
        dxc = dH * m * gi + jnp.concatenate(dxc_cols, axis=1)

        dcb_ref[...] += _colsum(dxc)
        dext[0:tm, :] = dxc
        rext[0:SUBLANES, :] = jnp.where(first_tile, 0.0, rxp_ref[SUBLANES:2 * SUBLANES, :].astype(F32))
        rext[SUBLANES:SUBLANES + tm, :] = rx_ref[...].astype(F32)
        drx = jnp.zeros((tm, D), F32)
        for k in range(CONV_WIDTH):
            drx = drx + dext[pl.ds(CONV_WIDTH - 1 - k, tm), :] * cw_ref[k:k + 1, :]
            dcw_ref[k:k + 1, :] += _colsum(dxc * rext[pl.ds(SUBLANES - (CONV_WIDTH - 1) + k, tm), :])
        dext[tm:tm + SUBLANES, :] = dext[0:SUBLANES, :]
        dab_ref[:, 0:D] = drx.astype(BF16)
        dab_ref[:, D:2 * D] = dgate.astype(BF16)

        @pl.when(first_tile)
        def _():
            for j in range(N_RG_BLOCK):
                for h in range(HEADS_PER_BLOCK):
                    sl = slice(HEAD_DIM * h, HEAD_DIM * (h + 1))
                    pair, side = divmod(HEADS_PER_BLOCK * j + h, 2)
                    lanes = slice(HEAD_DIM * side, HEAD_DIM * (side + 1))
                    dwah_ref[pair, :, lanes] = dwa_ref[j, sl, sl]
                    dwxh_ref[pair, :, lanes] = dwx_ref[j, sl, sl]

    def rev(col):
        return lambda i: (n - 1 - i, col)

    def prev8(col):
        return lambda i: (jnp.maximum((n - 1 - i) * per8 - 1, 0), col)

    def prev16(col):
        return lambda i: (jnp.maximum((n - 1 - i) * (per8 // 2) - 1, 0), col)

    tile = pl.BlockSpec((tm, D), rev(0))
    vec = pl.BlockSpec((1, D), lambda i: (0, 0))
    heads_in = pl.BlockSpec(rg_wa.shape, lambda i: (0, 0, 0, 0))
    head_pairs = (rg_wa.shape[1] // 2, HEAD_DIM, 2 * HEAD_DIM)
    heads_out = pl.BlockSpec(head_pairs, lambda i: (0, 0, 0))
    cw = pl.BlockSpec((CONV_WIDTH, D), lambda i: (0, 0))
    blocks_f32 = pltpu.VMEM((N_RG_BLOCK, RG_BLOCK, RG_BLOCK), F32)
    return _call(
        body, name="rglru_bwd", grid=(n,),
        in_specs=[tile, _ANY, pl.BlockSpec((tm, D), rev(0)), pl.BlockSpec((2 * SUBLANES, D), prev16(0)),
                  pl.BlockSpec((tm, D), rev(1)), tile, pl.BlockSpec((SUBLANES, D), prev8(0)), tile,
                  pl.BlockSpec((tm, N_GATES * D), rev(0)), cw, heads_in, heads_in, vec],
        out_specs=[pl.BlockSpec((tm, 2 * D), rev(2)), cw, vec, heads_out, heads_out, vec, vec, vec],
        out_shape=[S((T, IN_COLS), BF16), S((CONV_WIDTH, D), F32), S((1, D), F32),
                   S(head_pairs, F32), S(head_pairs, F32), S((1, D), F32), S((1, D), F32), S((1, D), F32)],
        scratch_shapes=[pltpu.VMEM((tm + SUBLANES, D), F32), pltpu.VMEM((tm + SUBLANES, D), F32),
                        pltpu.VMEM((tm + SUBLANES, D), F32), pltpu.VMEM((SUBLANES, D), F32),
                        pltpu.VMEM((SUBLANES, D), F32)] + [pltpu.VMEM((D // LANES, tm, LANES), F32)] * 3
        + [_RG_BLOCKS_BF16] * 2 + [blocks_f32] * 2,
        args=(dya, dproj, proj, proj, proj, hseq, hseq, xc, gates, conv_w, rg_wa, rg_wx, lam), sides=sides,
        aliases={1: 0})


def _inproj_dh(dproj, w_in, dh, first, count, name, sides=()):
    T = dproj.shape[0]
    tm = min(TM_MM, T)

    def body(*refs):
        dp_ref, w_ref, dh_ref = refs[-3:]
        dh = jnp.zeros((tm, D), F32)
        for p in range(N_PIECE):
            shard, part = divmod((p + DPROJ_ROTATION) % N_PIECE, IN_SHARD // PIECE)
            dh = dh + _dot_nt(dp_ref[:, PIECE * p:PIECE * (p + 1)], w_ref[shard, :, PIECE * part:PIECE * (part + 1)])
        dh_ref[...] = dh

    earlier = [] if dh is None else [dh]
    return _call(
        body, name=name, grid=(count,),
        in_specs=[_ANY] * len(earlier) + [pl.BlockSpec((tm, IN_COLS), lambda i: (first + i, 0)),
                                         _resident((N_SHARD, D, IN_SHARD))],
        out_specs=[pl.BlockSpec((tm, D), lambda i: (first + i, 0))],
        out_shape=[S((T, D), F32)],
        args=(*earlier, dproj, w_in), sides=sides, aliases={0: 0} if earlier else None)


def _inproj_norm_bwd(dh, x, g, dx1):
    T = x.shape[0]
    tm = min(TM_MM, T)

    def body(dh_ref, x_ref, g_ref, dx1_ref, dx_ref, dgm_ref):
        @pl.when(pl.program_id(0) == 0)
        def _():
            dgm_ref[...] = jnp.zeros_like(dgm_ref)

        dx, dg_rows = _rms_bwd(dh_ref[...], x_ref[...], g_ref[...])
        dgm_ref[...] += _colsum(dg_rows)
        dx_ref[...] = dx1_ref[...] + dx

    tile = pl.BlockSpec((tm, D), lambda i: (i, 0))
    vec = pl.BlockSpec((1, D), lambda i: (0, 0))
    return pl.pallas_call(
        body, name="inproj_norm_bwd", grid=(T // tm,),
        in_specs=[tile, tile, vec, tile],
        out_specs=[tile, vec],
        out_shape=[S((T, D), F32), S((1, D), F32)],
        compiler_params=_params(1),
    )(dh, x, g, dx1)


def _inproj_wgrad(h, dproj, sides=()):
    T = h.shape[0]
    tk = min(TK_WGRAD, T)
    per = IN_SHARD // PIECE

    def body(h_ref, *refs):
        pieces, o_ref = refs[:per], refs[per]

        @pl.when(pl.program_id(1) == 0)
        def _():
            o_ref[...] = jnp.zeros_like(o_ref)

        o_ref[0] += _dot_tn(h_ref[...], jnp.concatenate([p[...] for p in pieces], axis=1))

    def piece(i):
        return pl.BlockSpec((tk, PIECE), lambda j, k: (k, (per * j + i + N_PIECE - DPROJ_ROTATION) % N_PIECE))

    (out,), side_outs = _call(
        body, name="inproj_wgrad", grid=(N_SHARD, T // tk),
        in_specs=[pl.BlockSpec((tk, D), lambda j, k: (k, 0))] + [piece(i) for i in range(per)],
        out_specs=[pl.BlockSpec((1, D, IN_SHARD), lambda j, k: (j, 0, 0))],
        out_shape=[S((N_SHARD, D, IN_SHARD), F32)], args=(h,) + (dproj,) * per, sides=sides)
    return out, side_outs


def _row_tile(rows):
    for t in range(256, 0, -SUBLANES):
        if rows % t == 0:
            return t
    raise ValueError(rows)


def _add_halves(core, g, theirs, name):
    _, r, cols = g.shape
    half = r // 2
    tr = _row_tile(half)
    nb = half // tr

    def body(core_ref, g_ref, t_ref, o_ref):
        o_ref[...] = (g_ref[...] + t_ref[...]).astype(BF16)

    blk = pl.BlockSpec((1, tr, cols), lambda s, i, core_ref: (s, i, 0))
    gs = pltpu.PrefetchScalarGridSpec(
        num_scalar_prefetch=1, grid=(N_SHARD, nb),
        in_specs=[pl.BlockSpec((1, tr, cols), lambda s, i, core_ref: (s, core_ref[0] * nb + i, 0)), blk],
        out_specs=blk)
    return pl.pallas_call(
        body, name=name, grid_spec=gs, out_shape=S((N_SHARD, half, cols), BF16), compiler_params=_params(2),
    )(core, g, theirs)


def _sum_shards(chip, own, others, name):
    _, half, cols = own.shape
    tr = _row_tile(half)

    def body(chip_ref, own_ref, oth_ref, o_ref):
        acc = own_ref[0].astype(F32)
        for j in range(3):
            acc = acc + oth_ref[j].astype(F32)
        o_ref[...] = acc

    gs = pltpu.PrefetchScalarGridSpec(
        num_scalar_prefetch=1, grid=(half // tr,),
        in_specs=[pl.BlockSpec((1, tr, cols), lambda i, chip_ref: (chip_ref[0], i, 0)),
                  pl.BlockSpec((3, tr, cols), lambda i, chip_ref: (0, i, 0))],
        out_specs=pl.BlockSpec((tr, cols), lambda i, chip_ref: (i, 0)))
    return pl.pallas_call(
        body, name=name, grid_spec=gs, out_shape=S((half, cols), F32), compiler_params=_params(1),
    )(chip, own, others)


def _adamw(w, g, m, v):
    m = ADAM_B1 * m + (1.0 - ADAM_B1) * g
    v = ADAM_B2 * v + (1.0 - ADAM_B2) * (g * g)
    m_hat = m / (1.0 - ADAM_B1 ** ADAM_STEP)
    v_hat = v / (1.0 - ADAM_B2 ** ADAM_STEP)
    delta = -ADAM_LR * (m_hat / (jnp.sqrt(v_hat) + ADAM_EPS) + ADAM_WD * w)
    return delta, m, v


def _adamw_shard(core, mine, theirs, w, m, v, name):
    r, cols = w.shape
    half = r // 2
    tr = _row_tile(half)
    nb = half // tr

    def body(core_ref, mine_ref, theirs_ref, w_ref, m_ref, v_ref, g_ref, d_ref, mo_ref, vo_ref):
        g = jnp.where(pl.program_id(0) == core_ref[0], mine_ref[...], theirs_ref[...])
        g_ref[...] = g
        d_ref[...], mo_ref[...], vo_ref[...] = _adamw(w_ref[...], g, m_ref[...], v_ref[...])

    hblk = pl.BlockSpec((tr, cols), lambda h, i, core_ref: (i, 0))
    blk = pl.BlockSpec((tr, cols), lambda h, i, core_ref: (h * nb + i, 0))
    gs = pltpu.PrefetchScalarGridSpec(num_scalar_prefetch=1, grid=(2, nb),
                                      in_specs=[hblk, hblk, blk, blk, blk], out_specs=[blk] * 4)
    return pl.pallas_call(
        body, name=name, grid_spec=gs, out_shape=[S((r, cols), F32)] * 4, compiler_params=_params(2),
    )(core, mine, theirs, w, m, v)


def _adamw_whole(w, g, m, v, name):
    def body(w_ref, g_ref, m_ref, v_ref, d_ref, mo_ref, vo_ref):
        d_ref[...], mo_ref[...], vo_ref[...] = _adamw(w_ref[...], g_ref[...], m_ref[...], v_ref[...])

    return pl.pallas_call(body, name=name, out_shape=[S(w.shape, F32)] * 3)(w, g, m, v)


_VEC_ROWS = ("norm_mix_g", "conv_b", "rg_lambda", "sgu_ln_g", "sgu_ln_b", "norm_ffn_g", "norm_final_g", "rg_ba",
             "rg_bx")
_CONV_ROW = len(_VEC_ROWS)
_LOSS_ROW = _CONV_ROW + CONV_WIDTH
_VEC_PAD = -(_LOSS_ROW + 1) % SUBLANES
_HEAD_BIASES = ("rg_ba", "rg_bx")
_TENSORS = ("sgu_bs", "sgu_ws", "rg_wa", "rg_wx")
_HEAD_PAIRS = ("rg_wa", "rg_wx")


def _small_sum_adamw(parts, w, m, v):
    names = [n for n in _VEC_ROWS] + list(_TENSORS)
    n_parts = len(parts)

    def total(ref):
        acc = ref[0]
        for k in range(1, N_DEVICE):
            acc = acc + ref[k]
        return acc

    def body(*refs):
        part_refs, refs = refs[:n_parts], refs[n_parts:]
        w_refs, m_refs, v_refs = (dict(zip(names, refs[k * len(names):(k + 1) * len(names)])) for k in range(3))
        outs = refs[3 * len(names):]
        out_refs = {n: outs[4 * k:4 * k + 4] for k, n in enumerate(names)}
        conv_ref, loss_ref = outs[4 * len(names):]
        vec = total(part_refs[0])
        grads = {n: total(p) for n, p in zip(_TENSORS, part_refs[1:])}
        for n in _HEAD_PAIRS:
            pairs = grads[n]
            grads[n] = jnp.stack([pairs[k // 2, :, HEAD_DIM * (k % 2):HEAD_DIM * (k % 2 + 1)]
                                  for k in range(2 * pairs.shape[0])], axis=0)
        grads = {n: g[None] for n, g in grads.items()}
        for row, n in enumerate(_VEC_ROWS):
            g = vec[row:row + 1, :]
            if n in _HEAD_BIASES:
                g = jnp.concatenate([g[:, HEAD_DIM * h:HEAD_DIM * (h + 1)] for h in range(D // HEAD_DIM)], axis=0)[None]
            grads[n] = g
        for n in names:
            g_ref, d_ref, mo_ref, vo_ref = out_refs[n]
            g_ref[...] = grads[n]
            d_ref[...], mo_ref[...], vo_ref[...] = _adamw(w_refs[n][...], grads[n], m_refs[n][...], v_refs[n][...])
        conv_ref[...] = vec[_CONV_ROW:_CONV_ROW + CONV_WIDTH, :]
        loss_ref[...] = vec[_LOSS_ROW:_LOSS_ROW + 1, 0:1]

    res = pl.pallas_call(
        body, name="small_sum_adamw",
        out_shape=[S(w[n].shape, F32) for n in names for _ in range(4)] + [S((CONV_WIDTH, D), F32), S((1, 1), F32)],
        compiler_params=pltpu.CompilerParams(vmem_limit_bytes=VMEM_LIMIT),
    )(*parts, *[w[n] for n in names], *[m[n] for n in names], *[v[n] for n in names])
    return {n: tuple(res[4 * k:4 * k + 4]) for k, n in enumerate(names)}, res[-2], res[-1]


_BIG = ("w_in", "w_proj_a", "w_proj_b", "w_out", "w_gate_up", "w_down")
_WEIGHTS = ("norm_mix_g", "w_in", "conv_w", "conv_b", "rg_wa", "rg_ba", "rg_wx", "rg_bx", "rg_lambda", "sgu_ln_g",
            "sgu_ln_b", "sgu_ws", "sgu_bs", "w_proj_a", "w_proj_b", "w_out", "norm_ffn_g", "w_gate_up", "w_down",
            "norm_final_g")


def kernel(x, norm_mix_g, w_in, conv_w, conv_b, rg_wa, rg_ba, rg_wx, rg_bx, rg_lambda, sgu_ln_g, sgu_ln_b, sgu_ws, sgu_bs, w_proj_a, w_proj_b, w_out, norm_ffn_g, w_gate_up, w_down, norm_final_g, loss_target, m_norm_mix_g, m_w_in, m_conv_w, m_conv_b, m_rg_wa, m_rg_ba, m_rg_wx, m_rg_bx, m_rg_lambda, m_sgu_ln_g, m_sgu_ln_b, m_sgu_ws, m_sgu_bs, m_w_proj_a, m_w_proj_b, m_w_out, m_norm_ffn_g, m_w_gate_up, m_w_down, m_norm_final_g, v_norm_mix_g, v_w_in, v_conv_w, v_conv_b, v_rg_wa, v_rg_ba, v_rg_wx, v_rg_bx, v_rg_lambda, v_sgu_ln_g, v_sgu_ln_b, v_sgu_ws, v_sgu_bs, v_w_proj_a, v_w_proj_b, v_w_out, v_norm_ffn_g, v_w_gate_up, v_w_down, v_norm_final_g):
    args = dict(locals())
    w = {n: args[n] for n in _WEIGHTS}
    mom = {n: args["m_" + n] for n in _WEIGHTS}
    var = {n: args["v_" + n] for n in _WEIGHTS}
    xi, yi, ci = _position()
    core = ci.astype(jnp.int32).reshape(1)
    chip = (2 * xi + yi).astype(jnp.int32).reshape(1)

    bf = {n: w[n][0].astype(BF16) for n in _BIG}
    final_g = w["norm_final_g"].reshape(1, D)
    ba, bx = w["rg_ba"].reshape(1, D), w["rg_bx"].reshape(1, D)
    lam, ln_g, ln_b = w["rg_lambda"], w["sgu_ln_g"], w["sgu_ln_b"]
    x0, target = x[0], loss_target[0]

    def shard_major(g):
        return g.reshape(N_SHARD, g.shape[0] // N_SHARD, g.shape[1])

    def chip_sums(names, grads, theirs):
        return [_add_halves(core, g, t, "add_halves_" + n) for n, g, t in zip(names, grads, theirs)]

    def my_halves(names, sums, arrived):
        return [_sum_shards(chip, p, a, "sum_shards_" + n) for n, p, a in zip(names, sums, arrived)]

    (proj, h), ((w_in_a,), (conv_a,)) = _inproj_own(
        chip, x0, w["norm_mix_g"], bf["w_in"],
        sides=[_gather_half_side([bf["w_in"]], relations=(0, 1)), _gather_side([w["conv_w"][0]])])
    conv_cols = conv_a.shape[-1]
    conv_full = jnp.swapaxes(conv_a, 0, 1).reshape(CONV_WIDTH, D)
    (proj,), ((w_in_a,),) = _inproj_rest(
        chip, proj, h, w_in_a, 1, 2, "inproj_near",
        sides=[_gather_half_side([bf["w_in"]], relations=(2,), into=[w_in_a])])
    (proj,), _ = _inproj_rest(chip, proj, h, w_in_a, 3, 1, "inproj_far")
    (ya, xc, hseq, gates), ((w_pa_a, w_pb_a, w_out_a, w_gu_a, w_down_a),) = _rglru_fwd(
        proj, conv_full, w["conv_b"], w["rg_wa"], w["rg_wx"], ba, bx, lam,
        sides=[_gather_half_side([bf[n] for n in ("w_proj_a", "w_proj_b", "w_out", "w_gate_up", "w_down")])])
    wpa, wpb, wout, wdown = w_pa_a.reshape(D, D), w_pb_a.reshape(D, D), w_out_a.reshape(D, D), w_down_a.reshape(D_FF, D)
    yb, pa, pb, mb, x1 = _sgu_merge_fwd(x0, proj, ya, ln_g, ln_b, w["sgu_ws"], w["sgu_bs"], wpa, wpb, wout)
    h2, gu, act, dx2, dx2b, loss, d_final_g = _ffn_fwd_loss(x1, w["norm_ffn_g"], w_gu_a, wdown, final_g, target)

    dgu, dx1, dx1b, d_ffn_g = _ffn_bwd(dx2, dx2b, gu, wdown, w_gu_a, x1, w["norm_ffn_g"])
    ffn = ("w_gate_up", "w_down")
    g_ffn = [_matmul_tn(h2, dgu, FF_SHARD, True, "wgrad_gate_up")[0],
             shard_major(_matmul_tn(act, dx2b, D // 2, False, "wgrad_down")[0])]
    (dpa, dpb, dya, dproj, d_ws, d_bs, d_lng, d_lnb), (theirs_ffn,) = _merge_sgu_bwd(
        dx1b, proj, pa, pb, ln_g, ln_b, w["sgu_ws"], w["sgu_bs"], wpa, wpb, wout, sides=[_halves_side(g_ffn)])
    sums_ffn = chip_sums(ffn, g_ffn, theirs_ffn)
    mix = ("w_proj_a", "w_proj_b", "w_out")
    g_mix = [shard_major(_matmul_tn(ya, dpa, D, False, "wgrad_proj_a")[0]),
             shard_major(_matmul_tn(yb, dpb, D, False, "wgrad_proj_b")[0]),
             shard_major(_matmul_tn(mb, dx1b, D, False, "wgrad_out")[0])]
    (dproj, d_cw, d_cb, d_wa, d_wx, d_ba, d_bx, d_lam), (arrived_ffn, theirs_mix, sgu_parts) = _rglru_bwd(
        dya, dproj, proj, hseq, xc, gates, conv_full, w["rg_wa"], w["rg_wx"], lam,
        sides=[_scatter_side(sums_ffn), _halves_side(g_mix), _everyone_side([d_bs, d_ws])])
    mine_ffn = my_halves(ffn, sums_ffn, arrived_ffn)
    sums_mix = chip_sums(mix, g_mix, theirs_mix)
    g_in, ((wa_parts, wx_parts), other_ffn, arrived_mix) = _inproj_wgrad(
        h, dproj, sides=[_everyone_side([d_wa, d_wx]), _swap_side(mine_ffn),
                         _scatter_side(sums_mix)])
    mine_mix = my_halves(mix, sums_mix, arrived_mix)
    n_tiles = x0.shape[0] // min(TM_MM, x0.shape[0])
    n_first = max(1, n_tiles * 3 // 8)
    (dh,), (theirs_in, other_mix) = _inproj_dh(dproj, w_in_a, None, 0, n_first, "inproj_dh_a",
                                               sides=[_halves_side([g_in]), _swap_side(mine_mix)])
    sums_in = chip_sums(("w_in",), [g_in], theirs_in)
    (dh,), (arrived_in,) = _inproj_dh(dproj, w_in_a, dh, n_first, n_tiles - n_first, "inproj_dh_b",
                                      sides=[_scatter_side(sums_in)])
    mine_in = my_halves(("w_in",), sums_in, arrived_in)
    grad_x, d_mix_g = _inproj_norm_bwd(dh, x0, w["norm_mix_g"], dx1)
    rows = {"norm_mix_g": d_mix_g, "conv_b": d_cb, "rg_lambda": d_lam, "sgu_ln_g": d_lng, "sgu_ln_b": d_lnb,
            "norm_ffn_g": d_ffn_g, "norm_final_g": d_final_g, "rg_ba": d_ba, "rg_bx": d_bx}
    vec = jnp.concatenate([rows[n] for n in _VEC_ROWS]
                          + [d_cw, jnp.pad(loss, ((0, _VEC_PAD), (0, D - 1)))], axis=0)
    other_in, (vec_parts,) = _comm_only([_swap_side(mine_in), _everyone_side([vec])], "swap_w_in")
    small_parts = [vec_parts] + sgu_parts + [wa_parts, wx_parts]

    out = {}
    for n, gm, go in zip(ffn + mix + ("w_in",), mine_ffn + mine_mix + mine_in, other_ffn + other_mix + other_in):
        g, d, mo, vo = _adamw_shard(core, gm, go, w[n][0], mom[n][0], var[n][0], "adamw_" + n)
        out[n] = tuple(a[None] for a in (g, d, mo, vo))
    as_row = lambda t: {n: a.reshape(1, D) if n == "norm_final_g" else a for n, a in t.items()}
    small_out, conv_sum, loss_sum = _small_sum_adamw(small_parts, as_row(w), as_row(mom), as_row(var))
    out.update(small_out)
    out["norm_final_g"] = tuple(a.reshape(D) for a in small_out["norm_final_g"])
    conv_g = lax.dynamic_slice_in_dim(conv_sum, chip[0] * conv_cols, conv_cols, axis=1)
    d, mo, vo = _adamw_whole(w["conv_w"][0], conv_g, mom["conv_w"][0], var["conv_w"][0], "adamw_conv_w")
    out["conv_w"] = tuple(a[None] for a in (conv_g, d, mo, vo))

    return (loss_sum[0, 0], grad_x[None], *[out[n][0] for n in _WEIGHTS], *[out[n][1] for n in _WEIGHTS],
            *[out[n][2] for n in _WEIGHTS], *[out[n][3] for n in _WEIGHTS])
```

```python
import functools

import jax
import jax.numpy as jnp
from jax import lax
from jax.experimental import pallas as pl
from jax.experimental.pallas import tpu as pltpu

F32 = jnp.float32
BF16 = jnp.bfloat16
S = jax.ShapeDtypeStruct

D = 1024
N_SHARD = 4
IN_COLS = 6 * D
IN_SHARD = IN_COLS // N_SHARD
D_FF = 2816
FF_SHARD = 2 * D_FF // N_SHARD
RG_BLOCK = 256
N_RG_BLOCK = D // RG_BLOCK
CHUNK = 128
N_GROUP = 8
CONV_WIDTH = 4
RG_C = 8.0
EPS = 1e-6
ADAM_LR, ADAM_B1, ADAM_B2, ADAM_EPS, ADAM_WD, ADAM_STEP = 0.001, 0.9, 0.999, 1e-08, 0.01, 10

V7X_VMEM_BYTES = 64 * 1024 * 1024
VMEM_LIMIT = V7X_VMEM_BYTES * 3 // 4
SUBLANES = 8
MESH = pl.DeviceIdType.MESH

TM_MM = 512
TM_SCAN = 256
TM_FF = 256
TK_WGRAD = 1024


def _params(n_axes):
    return pltpu.CompilerParams(dimension_semantics=("arbitrary",) * n_axes, vmem_limit_bytes=VMEM_LIMIT)


def _resident(shape):
    nd = len(shape)
    return pl.BlockSpec(shape, lambda *_: (0,) * nd, pipeline_mode=pl.Buffered(1))


def _sig(x):
    return 1.0 / (1.0 + jnp.exp(-x))


_GELU_K2 = 2.0 * 0.7978845608028654
_GELU_C = 0.044715


def _gelu(x):
    return x * _sig(x * (_GELU_K2 + (_GELU_K2 * _GELU_C) * (x * x)))


def _gelu_and_grad(x):
    x2 = x * x
    s = _sig(x * (_GELU_K2 + (_GELU_K2 * _GELU_C) * x2))
    g = x * s
    return g, s + g * (1.0 - s) * (_GELU_K2 + (3.0 * _GELU_K2 * _GELU_C) * x2)


_EXPM1_SERIES = tuple(1.0 / f for f in (5040.0, 720.0, 120.0, 24.0, 6.0, 2.0, 1.0))


def _one_minus_exp(x):
    p = _EXPM1_SERIES[0]
    for coef in _EXPM1_SERIES[1:]:
        p = p * x + coef
    return jnp.where(x > -0.125, -x * p, 1.0 - jnp.exp(x))


def _softplus_neg(lam):
    z = -lam
    e = jnp.exp(-jnp.abs(z))
    u = 1.0 + e
    log1p = jnp.where(u == 1.0, e, jnp.log(u) * e / (u - 1.0))
    return jnp.maximum(z, 0.0) + log1p


def _rms_stats(x):
    return lax.rsqrt(jnp.mean(x * x, axis=-1, keepdims=True) + EPS)


def _rms_bwd(dy, x, g):
    rstd = _rms_stats(x)
    xhat = x * rstd
    dxhat = dy * g
    dx = rstd * (dxhat - xhat * jnp.mean(dxhat * xhat, axis=-1, keepdims=True))
    return dx, dy * xhat


def _colsum(x):
    return jnp.sum(x, axis=0, keepdims=True)


def _shift_down(x, d, fill):
    n = x.shape[0]
    if d % SUBLANES == 0:
        return jnp.concatenate([jnp.full((d, x.shape[1]), fill, x.dtype), x[:n - d]], axis=0)
    row = lax.broadcasted_iota(jnp.int32, x.shape, 0)
    return jnp.where(row < d, fill, pltpu.roll(x, d, 0))


def _shift_up(x, d, fill):
    n = x.shape[0]
    if d % SUBLANES == 0:
        return jnp.concatenate([x[d:], jnp.full((d, x.shape[1]), fill, x.dtype)], axis=0)
    row = lax.broadcasted_iota(jnp.int32, x.shape, 0)
    return jnp.where(row >= n - d, fill, pltpu.roll(x, n - d, 0))


def _scan(a, b, shift):
    d = 1
    while d < a.shape[0]:
        b = a * shift(b, d, 0.0) + b
        a = a * shift(a, d, 1.0)
        d *= 2
    return a, b


LANES = 128


def _scan_tile(a, b, outside, a_s, b_s, h_s, reverse):
    tm = a.shape[0]
    groups = tm // SUBLANES
    order = list(range(SUBLANES - 1, -1, -1) if reverse else range(SUBLANES))
    shift = _shift_up if reverse else _shift_down
    edge = groups - 1 if reverse else 0
    for j in range(D // LANES):
        a_s[j] = a[:, LANES * j:LANES * (j + 1)]
        b_s[j] = b[:, LANES * j:LANES * (j + 1)]
    for j in range(D // LANES):
        def slab(ref, k):
            return ref[j, pl.ds(k, groups, stride=SUBLANES), :]

        ga, gb = slab(a_s, order[0]), slab(b_s, order[0])
        for k in order[1:]:
            ak = slab(a_s, k)
            gb = ak * gb + slab(b_s, k)
            ga = ak * ga
        ga, gb = _scan(ga, gb, shift)
        h_out = outside[:, LANES * j:LANES * (j + 1)]
        group_end = ga * h_out + gb
        row = lax.broadcasted_iota(jnp.int32, (groups, LANES), 0)
        h = jnp.where(row == edge, h_out, shift(group_end, 1, 0.0))
        for k in order:
            h = slab(a_s, k) * h + slab(b_s, k)
            h_s[j, pl.ds(k, groups, stride=SUBLANES), :] = h
    return jnp.concatenate([h_s[j] for j in range(D // LANES)], axis=1)


def _dot(a, b):
    return jnp.dot(a, b, preferred_element_type=F32)


def _dot_nt(a, b):
    return lax.dot_general(a, b, (((1,), (1,)), ((), ())), preferred_element_type=F32)


def _dot_tn(a, b):
    return lax.dot_general(a, b, (((0,), (0,)), ((), ())), preferred_element_type=F32)


_ANY = pl.BlockSpec(memory_space=pl.ANY)


def _position():
    return lax.axis_index("x"), lax.axis_index("y"), lax.axis_index("c")


def _other_chips(x, y):
    return [(1 - x, y), (x, 1 - y), (1 - x, 1 - y)]


class _Side:
    def __init__(self, inputs, out_shapes, n_sems, make, continues=()):
        self.inputs, self.out_shapes, self.n_sems, self.make = list(inputs), list(out_shapes), n_sems, make
        self.continues = list(continues)


MID_STEP = 0.625
LATE_MID_STEP = 0.875


def _call(body, *, name, grid, in_specs, out_specs, out_shape, args, scratch_shapes=(), sides=(), aliases=None,
          scalars=None, mid=MID_STEP):
    n_in, n_out, n_scr = len(in_specs), len(out_specs), len(scratch_shapes)
    n_scalar = 0 if scalars is None else 1
    side_in = [len(s.inputs) + len(s.continues) for s in sides]
    side_out = [len(s.out_shapes) for s in sides]
    all_aliases = {k + n_scalar: v for k, v in (aliases or {}).items()}
    for idx, s in enumerate(sides):
        for k in range(len(s.continues)):
            operand = n_scalar + n_in + sum(side_in[:idx]) + len(s.inputs) + k
            all_aliases[operand] = n_out + sum(side_out[:idx]) + k

    def wrapped(*refs):
        refs = list(refs)
        take = lambda k: [refs.pop(0) for _ in range(k)]
        ins = take(n_scalar) + take(n_in)
        sins = [take(k) for k in side_in]
        outs = take(n_out)
        souts = [take(k) for k in side_out]
        scr = take(n_scr)
        sems = [take(3) for _ in sides]
        def run(phase):
            for s, si, so, sem in zip(sides, sins, souts, sems):
                for thunk in s.make(si[:len(s.inputs)], so, *sem)[phase]:
                    thunk()

        if sides:
            n_steps = functools.reduce(lambda a, b: a * b, grid)
            step = functools.reduce(lambda a, b: a + b, [
                pl.program_id(a) * functools.reduce(lambda p, q: p * q, grid[a + 1:], 1) for a in range(len(grid))])
            pl.when(step == 0)(lambda: run(0))
        body(*ins, *outs, *scr)
        if sides:
            pl.when(step == int(mid * (n_steps - 1)))(lambda: run(1))
            pl.when(step == n_steps - 1)(lambda: run(2))

    grid_spec = pltpu.PrefetchScalarGridSpec(
        num_scalar_prefetch=n_scalar, grid=grid,
        in_specs=list(in_specs) + [_ANY] * sum(side_in),
        out_specs=list(out_specs) + [_ANY] * sum(side_out),
        scratch_shapes=list(scratch_shapes) + [pltpu.SemaphoreType.DMA((s.n_sems,)) for s in sides for _ in range(3)])
    res = pl.pallas_call(
        wrapped, name=name, grid_spec=grid_spec,
        out_shape=list(out_shape) + [o for s in sides for o in s.out_shapes],
        input_output_aliases=all_aliases,
        compiler_params=_params(len(grid)),
    )(*([scalars] if n_scalar else []), *args, *[a for s in sides for a in s.inputs + s.continues])
    main, rest, per_side = list(res[:n_out]), list(res[n_out:]), []
    for k in side_out:
        per_side.append(rest[:k])
        rest = rest[k:]
    return main, per_side


def _comm_only(sides, name):
    def body():
        pass

    return _call(body, name=name, grid=(1,), in_specs=[], out_specs=[], out_shape=[], args=[], sides=sides)[1]


def _remote(src, dst, send, recv, k, device):
    return pltpu.make_async_remote_copy(src_ref=src, dst_ref=dst, send_sem=send.at[k], recv_sem=recv.at[k],
                                        device_id=device, device_id_type=MESH)


def _both_ways(copy, keys):
    return [lambda k=k: copy(k).start() for k in keys], [], [lambda k=k: copy(k).wait() for k in keys]


def _gather_side(shards):
    n = len(shards)

    def make(ins, outs, send, recv, local):
        x, y, c = _position()
        mine = 2 * x + y
        chips = _other_chips(x, y)
        pairs = [(w, j) for w in range(n) for j in range(3)]

        def own(w):
            return pltpu.make_async_copy(ins[w], outs[w].at[mine], local.at[w])

        def push(w, j):
            return _remote(ins[w], outs[w].at[mine], send, recv, 3 * w + j, (*chips[j], c))

        def arrival(w, j):
            px, py = chips[j]
            return _remote(ins[w], outs[w].at[2 * px + py], send, recv, 3 * w + j, (px, py, c))

        starts = [lambda w=w: own(w).start() for w in range(n)] + [lambda w=w, j=j: push(w, j).start() for w, j in pairs]
        waits = ([lambda w=w, j=j: arrival(w, j).wait_recv() for w, j in pairs]
                 + [lambda w=w, j=j: push(w, j).wait_send() for w, j in pairs]
                 + [lambda w=w: own(w).wait() for w in range(n)])
        return starts, [], waits

    return _Side(shards, [S((N_SHARD,) + s.shape, s.dtype) for s in shards], 3 * n, make)


def _gather_half_side(shards, relations=(0, 1, 2), into=None):
    n = len(shards)

    def make(ins, outs, send, recv, local):
        x, y, c = _position()
        mine = 2 * x + y
        chips = _other_chips(x, y)
        pairs = [(w, j) for w in range(n) for j in relations]

        def rows(w, core):
            half = ins[w].shape[0] // 2
            return pl.ds(core * half, half)

        def own(w):
            return pltpu.make_async_copy(ins[w], outs[w].at[mine], local.at[w])

        def push(w, j):
            return _remote(ins[w].at[rows(w, c), :], outs[w].at[mine, rows(w, c), :], send, recv, 3 * w + j,
                           (*chips[j], c))

        def landed(w, j, core):
            px, py = chips[j]
            return outs[w].at[2 * px + py, rows(w, core), :]

        def arrival(w, j):
            return _remote(ins[w].at[rows(w, c), :], landed(w, j, c), send, recv, 3 * w + j, (*chips[j], c))

        def passed(w, j, core):
            return _remote(landed(w, j, core), landed(w, j, core), send, recv, 3 * n + 3 * w + j, (x, y, 1 - c))

        owns = range(n) if into is None else ()
        starts = [lambda w=w: own(w).start() for w in owns] + [lambda w=w, j=j: push(w, j).start() for w, j in pairs]
        mids = [t for w, j in pairs for t in (lambda w=w, j=j: arrival(w, j).wait_recv(),
                                              lambda w=w, j=j: passed(w, j, c).start())]
        waits = ([lambda w=w, j=j: passed(w, j, 1 - c).wait_recv() for w, j in pairs]
                 + [lambda w=w, j=j: passed(w, j, c).wait_send() for w, j in pairs]
                 + [lambda w=w, j=j: push(w, j).wait_send() for w, j in pairs]
                 + [lambda w=w: own(w).wait() for w in owns])
        return starts, mids, waits

    return _Side(shards, [S((N_SHARD,) + s.shape, s.dtype) for s in shards], 6 * n, make, continues=into or ())


def _halves_side(grads):
    n = len(grads)

    def make(ins, outs, send, recv, local):
        x, y, c = _position()

        def copy(w):
            half = ins[w].shape[1] // 2
            return _remote(ins[w].at[:, pl.ds((1 - c) * half, half), :], outs[w], send, recv, w, (x, y, 1 - c))

        return _both_ways(copy, range(n))

    return _Side(grads, [S((N_SHARD, g.shape[1] // 2, g.shape[2]), F32) for g in grads], n, make)


def _scatter_side(partials):
    n = len(partials)

    def make(ins, outs, send, recv, local):
        x, y, c = _position()
        chips = _other_chips(x, y)

        def copy(k):
            w, j = divmod(k, 3)
            px, py = chips[j]
            return _remote(ins[w].at[2 * px + py], outs[w].at[j], send, recv, k, (px, py, c))

        return _both_ways(copy, range(3 * n))

    return _Side(partials, [S((3,) + p.shape[1:], p.dtype) for p in partials], 3 * n, make)


def _swap_side(halves):
    n = len(halves)

    def make(ins, outs, send, recv, local):
        x, y, c = _position()
        return _both_ways(lambda w: _remote(ins[w], outs[w], send, recv, w, (x, y, 1 - c)), range(n))

    return _Side(halves, [S(h.shape, h.dtype) for h in halves], n, make)


N_DEVICE = 8


def _everyone_side(arrays):
    n = len(arrays)
    peers = N_DEVICE - 1

    def make(ins, outs, send, recv, local):
        x, y, c = _position()
        mine = 4 * x + 2 * y + c
        pairs = [(w, k) for w in range(n) for k in range(1, N_DEVICE)]

        def peer(k):
            return (1 - x if k & 4 else x, 1 - y if k & 2 else y, 1 - c if k & 1 else c)

        def own(w):
            return pltpu.make_async_copy(ins[w], outs[w].at[mine], local.at[w])

        def push(w, k):
            return _remote(ins[w], outs[w].at[mine], send, recv, peers * w + k - 1, peer(k))

        def arrival(w, k):
            px, py, pc = peer(k)
            return _remote(ins[w], outs[w].at[4 * px + 2 * py + pc], send, recv, peers * w + k - 1, (px, py, pc))

        starts = [lambda w=w: own(w).start() for w in range(n)] + [lambda w=w, k=k: push(w, k).start() for w, k in pairs]
        waits = ([lambda w=w, k=k: arrival(w, k).wait_recv() for w, k in pairs]
                 + [lambda w=w, k=k: push(w, k).wait_send() for w, k in pairs]
                 + [lambda w=w: own(w).wait() for w in range(n)])
        return starts, [], waits

    return _Side(arrays, [S((N_DEVICE,) + a.shape, a.dtype) for a in arrays], peers * n, make)


def _inproj_own(chip, x, g, w_shard, sides=()):
    T = x.shape[0]
    tm = min(TM_MM, T)

    def body(chip_ref, x_ref, g_ref, w_ref, proj_ref, h_ref):
        xv = x_ref[...]
        h = (xv * _rms_stats(xv) * g_ref[...]).astype(BF16)
        h_ref[...] = h
        proj_ref[...] = _dot(h, w_ref[...]).astype(BF16)

    return _call(
        body, name="inproj_own", grid=(T // tm,),
        in_specs=[pl.BlockSpec((tm, D), lambda i, c: (i, 0)), pl.BlockSpec((1, D), lambda i, c: (0, 0)),
                  pl.BlockSpec((D, IN_SHARD), lambda i, c: (0, 0), pipeline_mode=pl.Buffered(1))],
        out_specs=[pl.BlockSpec((tm, IN_SHARD), lambda i, c: (i, c[0])), pl.BlockSpec((tm, D), lambda i, c: (i, 0))],
        out_shape=[S((T, IN_COLS), BF16), S((T, D), BF16)],
        args=(x, g, w_shard), sides=sides, scalars=chip, mid=LATE_MID_STEP)


def _inproj_rest(chip, proj, h, w_in, first, count, name, sides=(), mid=MID_STEP):
    T = h.shape[0]
    tm = min(TM_MM, T)

    def body(chip_ref, _, h_ref, w_ref, proj_ref):
        proj_ref[...] = _dot(h_ref[...], w_ref[0]).astype(BF16)

    def other(p, c):
        return jnp.bitwise_xor(c[0], first + p)

    return _call(
        body, name=name, grid=(count, T // tm),
        in_specs=[_ANY, pl.BlockSpec((tm, D), lambda p, i, c: (i, 0)),
                  pl.BlockSpec((1, D, IN_SHARD), lambda p, i, c: (other(p, c), 0, 0))],
        out_specs=[pl.BlockSpec((tm, IN_SHARD), lambda p, i, c: (i, other(p, c)))],
        out_shape=[S((T, IN_COLS), BF16)],
        args=(proj, h, w_in), sides=sides, scalars=chip, aliases={0: 0}, mid=mid)


def _rg_gates(xc, wa_ref, wx_ref, ba, bx, sp):
    xb = xc.astype(BF16)
    blocks = [xb[:, RG_BLOCK * j:RG_BLOCK * (j + 1)] for j in range(N_RG_BLOCK)]
    r = _sig(jnp.concatenate([_dot(blocks[j], wa_ref[j]) for j in range(N_RG_BLOCK)], axis=1) + ba)
    gi = _sig(jnp.concatenate([_dot(blocks[j], wx_ref[j]) for j in range(N_RG_BLOCK)], axis=1) + bx)
    log_a = (-RG_C) * r * sp
    a = jnp.exp(log_a)
    m = jnp.sqrt(_one_minus_exp(2.0 * log_a))
    return xb, r, gi, a, m


N_GATES = 4
HEADS_PER_BLOCK = 4
HEAD_DIM = RG_BLOCK // HEADS_PER_BLOCK
_RG_BLOCKS_BF16 = pltpu.VMEM((N_RG_BLOCK, RG_BLOCK, RG_BLOCK), BF16)


def _fill_blockdiag(heads_ref, blocks):
    blocks[...] = jnp.zeros_like(blocks)
    for j in range(N_RG_BLOCK):
        for h in range(HEADS_PER_BLOCK):
            sl = slice(HEAD_DIM * h, HEAD_DIM * (h + 1))
            blocks[j, sl, sl] = heads_ref[0, HEADS_PER_BLOCK * j + h].astype(BF16)


def _rglru_fwd(proj, conv_w, conv_b, rg_wa, rg_wx, ba, bx, lam, sides=()):
    T = proj.shape[0]
    tm = min(TM_SCAN, T)

    def body(rx_ref, gate_ref, cw_ref, cb_ref, wah_ref, wxh_ref, ba_ref, bx_ref, lam_ref,
             ya_ref, xc_ref, h_ref, gates_ref, ext, hc, a_s, b_s, h_s, wa_ref, wx_ref):
        @pl.when(pl.program_id(0) == 0)
        def _():
            ext[0:SUBLANES, :] = jnp.zeros((SUBLANES, D), F32)
            hc[...] = jnp.zeros((SUBLANES, D), F32)
            _fill_blockdiag(wah_ref, wa_ref)
            _fill_blockdiag(wxh_ref, wx_ref)

        ext[SUBLANES:SUBLANES + tm, :] = rx_ref[...].astype(F32)
        xc = cb_ref[...]
        for k in range(CONV_WIDTH):
            xc = xc + ext[pl.ds(SUBLANES - (CONV_WIDTH - 1) + k, tm), :] * cw_ref[k:k + 1, :]
        ext[0:SUBLANES, :] = ext[tm:tm + SUBLANES, :]
        xc_ref[...] = xc
        _, r, gi, a, m = _rg_gates(xc, wa_ref, wx_ref, ba_ref[...], bx_ref[...], _softplus_neg(lam_ref[...]))
        for k, val in enumerate((r, gi, a, m)):
            gates_ref[:, D * k:D * (k + 1)] = val
        h = _scan_tile(a, m * (gi * xc), hc[0:1, :], a_s, b_s, h_s, reverse=False)
        hc[...] = jnp.broadcast_to(h[tm - 1:tm, :], (SUBLANES, D))
        h_ref[...] = h
        ya_ref[...] = (_gelu(gate_ref[...].astype(F32)) * h).astype(BF16)

    vec = pl.BlockSpec((1, D), lambda i: (0, 0))
    heads = pl.BlockSpec(rg_wa.shape, lambda i: (0, 0, 0, 0))
    tile = pl.BlockSpec((tm, D), lambda i: (i, 0))
    return _call(
        body, name="rglru_fwd", grid=(T // tm,),
        in_specs=[pl.BlockSpec((tm, D), lambda i: (i, 0)), pl.BlockSpec((tm, D), lambda i: (i, 1)),
                  pl.BlockSpec((CONV_WIDTH, D), lambda i: (0, 0)), vec, heads, heads, vec, vec, vec],
        out_specs=[tile, tile, tile, pl.BlockSpec((tm, N_GATES * D), lambda i: (i, 0))],
        out_shape=[S((T, D), BF16), S((T, D), F32), S((T, D), F32), S((T, N_GATES * D), F32)],
        scratch_shapes=[pltpu.VMEM((tm + SUBLANES, D), F32), pltpu.VMEM((SUBLANES, D), F32)]
        + [pltpu.VMEM((D // LANES, tm, LANES), F32)] * 3 + [_RG_BLOCKS_BF16] * 2,
        args=(proj, proj, conv_w, conv_b, rg_wa, rg_wx, ba, bx, lam), sides=sides, mid=LATE_MID_STEP)


def _layer_norm_stats(v):
    mu = jnp.mean(v, axis=-1, keepdims=True)
    vc = v - mu
    rstd = lax.rsqrt(jnp.mean(vc * vc, axis=-1, keepdims=True) + EPS)
    return vc * rstd, rstd


def _sgu_mix(w_ref, vnb, bst_ref, n_chunk):
    cols = []
    for g in range(N_GROUP):
        vg = vnb[:, CHUNK * g:CHUNK * (g + 1)].reshape(n_chunk, CHUNK, CHUNK)
        wb = jnp.broadcast_to(w_ref[g][None], (n_chunk, CHUNK, CHUNK))
        mg = lax.dot_general(wb, vg, (((2,), (1,)), ((0,), (0,))), preferred_element_type=F32)
        mg = mg + bst_ref[:, g:g + 1][None]
        cols.append(mg.reshape(n_chunk * CHUNK, CHUNK))
    return jnp.concatenate(cols, axis=1)


def _causal_mask():
    return (lax.broadcasted_iota(jnp.int32, (CHUNK, CHUNK), 0) >= lax.broadcasted_iota(jnp.int32, (CHUNK, CHUNK), 1))


def _fill_sgu_weights(ws_ref, bs_ref, w_tril, bs_t, w_tril_t=None):
    keep = _causal_mask()
    for g in range(N_GROUP):
        wg = jnp.where(keep, ws_ref[0, g], 0.0)
        w_tril[g] = wg.astype(BF16)
        if w_tril_t is not None:
            w_tril_t[g] = wg.T.astype(BF16)
    bs_t[...] = bs_ref[0].T


_SGU_W_BF16 = pltpu.VMEM((N_GROUP, CHUNK, CHUNK), BF16)
_SGU_BT = pltpu.VMEM((CHUNK, N_GROUP), F32)


def _sgu_merge_fwd(x, proj, ya, ln_g, ln_b, sgu_ws, sgu_bs, wpa, wpb, wout):
    T = x.shape[0]
    tm = min(TM_FF, T)
    n_chunk = tm // CHUNK

    def body(x_ref, uv_ref, gab_ref, ya_ref, g_ref, b_ref, ws_ref, bs_ref, wpa_ref, wpb_ref, wout_ref,
             yb_ref, pa_ref, pb_ref, mb_ref, x1_ref, w_ref, bst_ref):
        @pl.when(pl.program_id(0) == 0)
        def _():
            _fill_sgu_weights(ws_ref, bs_ref, w_ref, bst_ref)

        vhat, _ = _layer_norm_stats(_gelu(uv_ref[:, D:2 * D].astype(F32)))
        vnb = (vhat * g_ref[...] + b_ref[...]).astype(BF16)
        yb = (_gelu(uv_ref[:, 0:D].astype(F32)) * _sgu_mix(w_ref, vnb, bst_ref, n_chunk)).astype(BF16)
        yb_ref[...] = yb
        pa = _dot(ya_ref[...], wpa_ref[...])
        pb = _dot(yb, wpb_ref[...])
        pa_ref[...] = pa.astype(BF16)
        pb_ref[...] = pb.astype(BF16)
        mb = (_sig(gab_ref[:, 0:D].astype(F32)) * pa + _sig(gab_ref[:, D:2 * D].astype(F32)) * pb).astype(BF16)
        mb_ref[...] = mb
        x1_ref[...] = x_ref[...] + _dot(mb, wout_ref[...])

    tile = pl.BlockSpec((tm, D), lambda i: (i, 0))
    vec = pl.BlockSpec((1, D), lambda i: (0, 0))
    w = _resident((D, D))
    return pl.pallas_call(
        body, name="sgu_merge_fwd", grid=(T // tm,),
        in_specs=[tile, pl.BlockSpec((tm, 2 * D), lambda i: (i, 1)), pl.BlockSpec((tm, 2 * D), lambda i: (i, 2)), tile,
                  vec, vec, pl.BlockSpec(sgu_ws.shape, lambda i: (0, 0, 0, 0)),
                  pl.BlockSpec(sgu_bs.shape, lambda i: (0, 0, 0)), w, w, w],
        out_specs=[tile, tile, tile, tile, tile],
        out_shape=[S((T, D), BF16), S((T, D), BF16), S((T, D), BF16), S((T, D), BF16), S((T, D), F32)],
        scratch_shapes=[_SGU_W_BF16, _SGU_BT],
        compiler_params=_params(1),
    )(x, proj, proj, ya, ln_g, ln_b, sgu_ws, sgu_bs, wpa, wpb, wout)


def _ffn_fwd_loss(x1, g, w_gu, w_down, g_final, target):
    T = x1.shape[0]
    tm = min(TM_FF, T)

    def body(x_ref, g_ref, wgu_ref, wd_ref, gf_ref, t_ref,
             h2_ref, gu_ref, act_ref, dx2_ref, dx2b_ref, loss_ref, dg_ref):
        @pl.when(pl.program_id(0) == 0)
        def _():
            loss_ref[...] = jnp.zeros_like(loss_ref)
            dg_ref[...] = jnp.zeros_like(dg_ref)

        xv = x_ref[...]
        h2 = (xv * _rms_stats(xv) * g_ref[...]).astype(BF16)
        h2_ref[...] = h2
        x2 = xv
        for k in range(N_SHARD // 2):
            cols = slice(FF_SHARD * k, FF_SHARD * (k + 1))
            gate = _dot(h2, wgu_ref[k])
            up = _dot(h2, wgu_ref[k + N_SHARD // 2])
            gu_ref[:, cols] = gate.astype(BF16)
            gu_ref[:, D_FF + FF_SHARD * k:D_FF + FF_SHARD * (k + 1)] = up.astype(BF16)
            act = (gate * _sig(gate) * up).astype(BF16)
            act_ref[:, cols] = act
            x2 = x2 + _dot(act, wd_ref[cols, :])
        gf = gf_ref[...]
        err = x2 * _rms_stats(x2) * gf - t_ref[...]
        loss_ref[...] += 0.5 * jnp.sum(jnp.mean(err * err, axis=-1, keepdims=True), axis=0, keepdims=True)
        dx2, dg_rows = _rms_bwd(err * (1.0 / D), x2, gf)
        dg_ref[...] += _colsum(dg_rows)
        dx2_ref[...] = dx2
        dx2b_ref[...] = dx2.astype(BF16)

    tile = pl.BlockSpec((tm, D), lambda i: (i, 0))
    vec = pl.BlockSpec((1, D), lambda i: (0, 0))
    return pl.pallas_call(
        body, name="ffn_fwd_loss", grid=(T // tm,),
        in_specs=[tile, vec, _resident((N_SHARD, D, FF_SHARD)), _resident((D_FF, D)), vec, tile],
        out_specs=[tile, pl.BlockSpec((tm, 2 * D_FF), lambda i: (i, 0)), pl.BlockSpec((tm, D_FF), lambda i: (i, 0)),
                   tile, tile, pl.BlockSpec((1, 1), lambda i: (0, 0)), vec],
        out_shape=[S((T, D), BF16), S((T, 2 * D_FF), BF16), S((T, D_FF), BF16), S((T, D), F32), S((T, D), BF16),
                   S((1, 1), F32), S((1, D), F32)],
        compiler_params=_params(1),
    )(x1, g, w_gu, w_down, g_final, target)


def _ffn_bwd(dx2, dx2b, gu, w_down, w_gu, x1, g):
    T = x1.shape[0]
    tm = min(TM_FF, T)

    def body(dx2_ref, dx2b_ref, gu_ref, wd_ref, wgu_ref, x_ref, g_ref, dgu_ref, dx1_ref, dx1b_ref, dg_ref):
        @pl.when(pl.program_id(0) == 0)
        def _():
            dg_ref[...] = jnp.zeros_like(dg_ref)

        dxb = dx2b_ref[...]
        dh2 = jnp.zeros((tm, D), F32)
        for k in range(N_SHARD // 2):
            cols = slice(FF_SHARD * k, FF_SHARD * (k + 1))
            up_cols = slice(D_FF + FF_SHARD * k, D_FF + FF_SHARD * (k + 1))
            dact = _dot_nt(dxb, wd_ref[cols, :])
            gate = gu_ref[:, cols].astype(F32)
            sg = _sig(gate)
            dgate = (dact * gu_ref[:, up_cols].astype(F32) * (sg * (1.0 + gate * (1.0 - sg)))).astype(BF16)
            dup = (dact * (gate * sg)).astype(BF16)
            dgu_ref[:, cols] = dgate
            dgu_ref[:, up_cols] = dup
            dh2 = dh2 + _dot_nt(dgate, wgu_ref[k]) + _dot_nt(dup, wgu_ref[k + N_SHARD // 2])
        dx, dg_rows = _rms_bwd(dh2, x_ref[...], g_ref[...])
        dg_ref[...] += _colsum(dg_rows)
        dx1 = dx2_ref[...] + dx
        dx1_ref[...] = dx1
        dx1b_ref[...] = dx1.astype(BF16)

    tile = pl.BlockSpec((tm, D), lambda i: (i, 0))
    wide = pl.BlockSpec((tm, 2 * D_FF), lambda i: (i, 0))
    vec = pl.BlockSpec((1, D), lambda i: (0, 0))
    return pl.pallas_call(
        body, name="ffn_bwd", grid=(T // tm,),
        in_specs=[tile, tile, wide, _resident((D_FF, D)), _resident((N_SHARD, D, FF_SHARD)), tile, vec],
        out_specs=[wide, tile, tile, vec],
        out_shape=[S((T, 2 * D_FF), BF16), S((T, D), F32), S((T, D), BF16), S((1, D), F32)],
        compiler_params=_params(1),
    )(dx2, dx2b, gu, w_down, w_gu, x1, g)


def _matmul_tn(a, b, tn, shard_major, name, sides=()):
    T, M = a.shape
    N = b.shape[1]
    tk = min(TK_WGRAD, T)

    def body(a_ref, b_ref, o_ref):
        @pl.when(pl.program_id(1) == 0)
        def _():
            o_ref[...] = jnp.zeros_like(o_ref)

        acc = _dot_tn(a_ref[...], b_ref[...])
        if shard_major:
            o_ref[0] += acc
        else:
            o_ref[...] += acc

    if shard_major:
        out_spec, out_shape = pl.BlockSpec((1, M, tn), lambda j, k: (j, 0, 0)), S((N // tn, M, tn), F32)
    else:
        out_spec, out_shape = pl.BlockSpec((M, tn), lambda j, k: (0, j)), S((M, N), F32)
    (out,), side_outs = _call(
        body, name=name, grid=(N // tn, T // tk),
        in_specs=[pl.BlockSpec((tk, M), lambda j, k: (k, 0)), pl.BlockSpec((tk, tn), lambda j, k: (k, j))],
        out_specs=[out_spec], out_shape=[out_shape], args=(a, b), sides=sides)
    return out, side_outs


PIECE = IN_SHARD // 3
N_PIECE = IN_COLS // PIECE
DPROJ_ROTATION = 2 * D // PIECE


def _merge_sgu_bwd(dx1b, proj, pa, pb, ln_g, ln_b, sgu_ws, sgu_bs, wpa, wpb, wout, sides=()):
    T = dx1b.shape[0]
    tm = min(TM_FF, T)
    n_chunk = tm // CHUNK

    def body(dx_ref, uv_ref, gab_ref, pa_ref, pb_ref, g_ref, b_ref, ws_ref, bs_ref, wpa_ref, wpb_ref, wout_ref,
             dpa_ref, dpb_ref, dya_ref, dp_ref, dw_ref, dbs_ref, dg_ref, db_ref, w_ref, wt_ref, bst_ref):
        @pl.when(pl.program_id(0) == 0)
        def _():
            for ref in (dw_ref, dbs_ref, dg_ref, db_ref):
                ref[...] = jnp.zeros_like(ref)
            _fill_sgu_weights(ws_ref, bs_ref, w_ref, bst_ref, wt_ref)

        dm = _dot_nt(dx_ref[...], wout_ref[...])
        sa = _sig(gab_ref[:, 0:D].astype(F32))
        sb = _sig(gab_ref[:, D:2 * D].astype(F32))
        dpa = (dm * sa).astype(BF16)
        dpb = (dm * sb).astype(BF16)
        dpa_ref[...] = dpa
        dpb_ref[...] = dpb
        dp_ref[:, 2 * D:3 * D] = (dm * pa_ref[...].astype(F32) * (sa * (1.0 - sa))).astype(BF16)
        dp_ref[:, 3 * D:4 * D] = (dm * pb_ref[...].astype(F32) * (sb * (1.0 - sb))).astype(BF16)
        dya_ref[...] = _dot_nt(dpa, wpa_ref[...]).astype(BF16)
        dyb_v = _dot_nt(dpb, wpb_ref[...])

        gu, dgu = _gelu_and_grad(uv_ref[:, 0:D].astype(F32))
        gv, dgv = _gelu_and_grad(uv_ref[:, D:2 * D].astype(F32))
        vhat, rstd = _layer_norm_stats(gv)
        lng = g_ref[...]
        vnb = (vhat * lng + b_ref[...]).astype(BF16)
        mixed = _sgu_mix(w_ref, vnb, bst_ref, n_chunk)
        dp_ref[:, 0:D] = (dyb_v * mixed * dgu).astype(BF16)
        dmix = dyb_v * gu
        dmb = dmix.astype(BF16)
        keep = _causal_mask()
        dvn_cols, dbs_rows = [], []
        for g in range(N_GROUP):
            sl = slice(CHUNK * g, CHUNK * (g + 1))
            dmg = dmb[:, sl].reshape(n_chunk, CHUNK, CHUNK)
            vg = vnb[:, sl].reshape(n_chunk, CHUNK, CHUNK)
            wtb = jnp.broadcast_to(wt_ref[g][None], (n_chunk, CHUNK, CHUNK))
            dvn = lax.dot_general(wtb, dmg, (((2,), (1,)), ((0,), (0,))), preferred_element_type=F32)
            dvn_cols.append(dvn.reshape(tm, CHUNK))
            dw = lax.dot_general(dmg, vg, (((2,), (2,)), ((0,), (0,))), preferred_element_type=F32)
            dw_ref[g] += jnp.where(keep, jnp.sum(dw, axis=0), 0.0)
            per_token = jnp.sum(dmix[:, sl], axis=1)
            dbs_rows.append(jnp.sum(per_token.reshape(n_chunk, CHUNK), axis=0, keepdims=True))
        dbs_ref[...] += jnp.concatenate(dbs_rows, axis=0)
        dvn = jnp.concatenate(dvn_cols, axis=1)
        dg_ref[...] += _colsum(dvn * vhat)
        db_ref[...] += _colsum(dvn)
        dvhat = dvn * lng
        dgv_in = rstd * (dvhat - jnp.mean(dvhat, axis=-1, keepdims=True)
                         - vhat * jnp.mean(dvhat * vhat, axis=-1, keepdims=True))
        dp_ref[:, D:2 * D] = (dgv_in * dgv).astype(BF16)

    tile = pl.BlockSpec((tm, D), lambda i: (i, 0))
    vec = pl.BlockSpec((1, D), lambda i: (0, 0))
    w = _resident((D, D))
    wsp = pl.BlockSpec((N_GROUP, CHUNK, CHUNK), lambda i: (0, 0, 0))
    return _call(
        body, name="merge_sgu_bwd", grid=(T // tm,),
        in_specs=[tile, pl.BlockSpec((tm, 2 * D), lambda i: (i, 1)), pl.BlockSpec((tm, 2 * D), lambda i: (i, 2)),
                  tile, tile, vec, vec, pl.BlockSpec(sgu_ws.shape, lambda i: (0, 0, 0, 0)),
                  pl.BlockSpec(sgu_bs.shape, lambda i: (0, 0, 0)), w, w, w],
        out_specs=[tile, tile, tile, pl.BlockSpec((tm, 4 * D), lambda i: (i, 0)), wsp,
                   pl.BlockSpec((N_GROUP, CHUNK), lambda i: (0, 0)), vec, vec],
        out_shape=[S((T, D), BF16), S((T, D), BF16), S((T, D), BF16), S((T, IN_COLS), BF16),
                   S((N_GROUP, CHUNK, CHUNK), F32), S((N_GROUP, CHUNK), F32), S((1, D), F32), S((1, D), F32)],
        scratch_shapes=[_SGU_W_BF16, _SGU_W_BF16, _SGU_BT],
        args=(dx1b, proj, proj, pa, pb, ln_g, ln_b, sgu_ws, sgu_bs, wpa, wpb, wout), sides=sides)


def _rglru_bwd(dya, dproj, proj, hseq, xc, gates, conv_w, rg_wa, rg_wx, lam, sides=()):
    T = dya.shape[0]
    tm = min(TM_SCAN, T)
    n = T // tm
    per8 = tm // SUBLANES

    def body(dya_ref, _, rx_ref, rxp_ref, gate_ref, h_ref, hp_ref, xc_ref, gates_ref, cw_ref, wah_ref, wxh_ref,
             lam_ref, dab_ref, dcw_ref, dcb_ref, dwah_ref, dwxh_ref, dba_ref, dbx_ref, dlam_ref,
             hext, rext, dext, carry_a, carry_dh, a_s, b_s, h_s, wa_ref, wx_ref, dwa_ref, dwx_ref):
        i = pl.program_id(0)
        first_tile = i == n - 1

        @pl.when(i == 0)
        def _():
            for ref in (dcw_ref, dcb_ref, dwa_ref, dwx_ref, dba_ref, dbx_ref, dlam_ref, carry_a, carry_dh):
                ref[...] = jnp.zeros_like(ref)
            dext[tm:tm + SUBLANES, :] = jnp.zeros((SUBLANES, D), F32)
            _fill_blockdiag(wah_ref, wa_ref)
            _fill_blockdiag(wxh_ref, wx_ref)

        gel, dgel = _gelu_and_grad(gate_ref[...].astype(F32))
        dya_v = dya_ref[...].astype(F32)
        hseq_v = h_ref[...]
        dgate = dya_v * hseq_v * dgel
        xcv = xc_ref[...]
        lam_v = lam_ref[...]
        sp = _softplus_neg(lam_v)
        xb = xcv.astype(BF16)
        r, gi, a, m = (gates_ref[:, D * k:D * (k + 1)] for k in range(N_GATES))

        row = lax.broadcasted_iota(jnp.int32, (tm, D), 0)
        c = jnp.where(row == tm - 1, carry_a[0:1, :], _shift_up(a, 1, 0.0))
        dH = _scan_tile(c, dya_v * gel, carry_dh[0:1, :], a_s, b_s, h_s, reverse=True)
        carry_a[...] = jnp.broadcast_to(a[0:1, :], (SUBLANES, D))
        carry_dh[...] = jnp.broadcast_to(dH[0:1, :], (SUBLANES, D))

        hext[0:SUBLANES, :] = jnp.where(first_tile, 0.0, hp_ref[...])
        hext[SUBLANES:SUBLANES + tm, :] = hseq_v
        h_prev = hext[pl.ds(SUBLANES - 1, tm), :]

        d_m = dH * (gi * xcv)
        d_la = dH * h_prev * a - d_m * (a * a) / m
        d_ia = dH * m * xcv * (gi * (1.0 - gi))
        d_ra = d_la * ((-RG_C) * sp) * (r * (1.0 - r))
        dlam_ref[...] += _colsum(d_la * ((-RG_C) * r)) * (-_sig(-lam_v))
        dba_ref[...] += _colsum(d_ra)
        dbx_ref[...] += _colsum(d_ia)
        drab = d_ra.astype(BF16)
        diab = d_ia.astype(BF16)
        dxc_cols = []
        for j in range(N_RG_BLOCK):
            sl = slice(RG_BLOCK * j, RG_BLOCK * (j + 1))
            dxc_cols.append(_dot_nt(drab[:, sl], wa_ref[j]) + _dot_nt(diab[:, sl], wx_ref[j]))
            dwa_ref[j] += _dot_tn(xb[:, sl], drab[:, sl])
            dwx_ref[j] += _dot_tn(xb[:, sl], diab[:, sl])
        dxc = dH * m * gi + jnp.concatenate(dxc_cols, axis=1)

        dcb_ref[...] += _colsum(dxc)
        dext[0:tm, :] = dxc
        rext[0:SUBLANES, :] = jnp.where(first_tile, 0.0, rxp_ref[SUBLANES:2 * SUBLANES, :].astype(F32))
        rext[SUBLANES:SUBLANES + tm, :] = rx_ref[...].astype(F32)
        drx = jnp.zeros((tm, D), F32)
        for k in range(CONV_WIDTH):
            drx = drx + dext[pl.ds(CONV_WIDTH - 1 - k, tm), :] * cw_ref[k:k + 1, :]
            dcw_ref[k:k + 1, :] += _colsum(dxc * rext[pl.ds(SUBLANES - (CONV_WIDTH - 1) + k, tm), :])
        dext[tm:tm + SUBLANES, :] = dext[0:SUBLANES, :]
        dab_ref[:, 0:D] = drx.astype(BF16)
        dab_ref[:, D:2 * D] = dgate.astype(BF16)

        @pl.when(first_tile)
        def _():
            for j in range(N_RG_BLOCK):
                for h in range(HEADS_PER_BLOCK):
                    sl = slice(HEAD_DIM * h, HEAD_DIM * (h + 1))
                    pair, side = divmod(HEADS_PER_BLOCK * j + h, 2)
                    lanes = slice(HEAD_DIM * side, HEAD_DIM * (side + 1))
                    dwah_ref[pair, :, lanes] = dwa_ref[j, sl, sl]
                    dwxh_ref[pair, :, lanes] = dwx_ref[j, sl, sl]

    def rev(col):
        return lambda i: (n - 1 - i, col)

    def prev8(col):
        return lambda i: (jnp.maximum((n - 1 - i) * per8 - 1, 0), col)

    def prev16(col):
        return lambda i: (jnp.maximum((n - 1 - i) * (per8 // 2) - 1, 0), col)

    tile = pl.BlockSpec((tm, D), rev(0))
    vec = pl.BlockSpec((1, D), lambda i: (0, 0))
    heads_in = pl.BlockSpec(rg_wa.shape, lambda i: (0, 0, 0, 0))
    head_pairs = (rg_wa.shape[1] // 2, HEAD_DIM, 2 * HEAD_DIM)
    heads_out = pl.BlockSpec(head_pairs, lambda i: (0, 0, 0))
    cw = pl.BlockSpec((CONV_WIDTH, D), lambda i: (0, 0))
    blocks_f32 = pltpu.VMEM((N_RG_BLOCK, RG_BLOCK, RG_BLOCK), F32)
    return _call(
        body, name="rglru_bwd", grid=(n,),
        in_specs=[tile, _ANY, pl.BlockSpec((tm, D), rev(0)), pl.BlockSpec((2 * SUBLANES, D), prev16(0)),
                  pl.BlockSpec((tm, D), rev(1)), tile, pl.BlockSpec((SUBLANES, D), prev8(0)), tile,
                  pl.BlockSpec((tm, N_GATES * D), rev(0)), cw, heads_in, heads_in, vec],
        out_specs=[pl.BlockSpec((tm, 2 * D), rev(2)), cw, vec, heads_out, heads_out, vec, vec, vec],
        out_shape=[S((T, IN_COLS), BF16), S((CONV_WIDTH, D), F32), S((1, D), F32),
                   S(head_pairs, F32), S(head_pairs, F32), S((1, D), F32), S((1, D), F32), S((1, D), F32)],
        scratch_shapes=[pltpu.VMEM((tm + SUBLANES, D), F32), pltpu.VMEM((tm + SUBLANES, D), F32),
                        pltpu.VMEM((tm + SUBLANES, D), F32), pltpu.VMEM((SUBLANES, D), F32),
                        pltpu.VMEM((SUBLANES, D), F32)] + [pltpu.VMEM((D // LANES, tm, LANES), F32)] * 3
        + [_RG_BLOCKS_BF16] * 2 + [blocks_f32] * 2,
        args=(dya, dproj, proj, proj, proj, hseq, hseq, xc, gates, conv_w, rg_wa, rg_wx, lam), sides=sides,
        aliases={1: 0})


def _inproj_dh(dproj, w_in, dh, first, count, name, sides=()):
    T = dproj.shape[0]
    tm = min(TM_MM, T)

    def body(*refs):
        dp_ref, w_ref, dh_ref = refs[-3:]
        dh = jnp.zeros((tm, D), F32)
        for p in range(N_PIECE):
            shard, part = divmod((p + DPROJ_ROTATION) % N_PIECE, IN_SHARD // PIECE)
            dh = dh + _dot_nt(dp_ref[:, PIECE * p:PIECE * (p + 1)], w_ref[shard, :, PIECE * part:PIECE * (part + 1)])
        dh_ref[...] = dh

    earlier = [] if dh is None else [dh]
    return _call(
        body, name=name, grid=(count,),
        in_specs=[_ANY] * len(earlier) + [pl.BlockSpec((tm, IN_COLS), lambda i: (first + i, 0)),
                                         _resident((N_SHARD, D, IN_SHARD))],
        out_specs=[pl.BlockSpec((tm, D), lambda i: (first + i, 0))],
        out_shape=[S((T, D), F32)],
        args=(*earlier, dproj, w_in), sides=sides, aliases={0: 0} if earlier else None)


def _inproj_norm_bwd(dh, x, g, dx1):
    T = x.shape[0]
    tm = min(TM_MM, T)

    def body(dh_ref, x_ref, g_ref, dx1_ref, dx_ref, dgm_ref):
        @pl.when(pl.program_id(0) == 0)
        def _():
            dgm_ref[...] = jnp.zeros_like(dgm_ref)

        dx, dg_rows = _rms_bwd(dh_ref[...], x_ref[...], g_ref[...])
        dgm_ref[...] += _colsum(dg_rows)
        dx_ref[...] = dx1_ref[...] + dx

    tile = pl.BlockSpec((tm, D), lambda i: (i, 0))
    vec = pl.BlockSpec((1, D), lambda i: (0, 0))
    return pl.pallas_call(
        body, name="inproj_norm_bwd", grid=(T // tm,),
        in_specs=[tile, tile, vec, tile],
        out_specs=[tile, vec],
        out_shape=[S((T, D), F32), S((1, D), F32)],
        compiler_params=_params(1),
    )(dh, x, g, dx1)


def _inproj_wgrad(h, dproj, sides=()):
    T = h.shape[0]
    tk = min(TK_WGRAD, T)
    per = IN_SHARD // PIECE

    def body(h_ref, *refs):
        pieces, o_ref = refs[:per], refs[per]

        @pl.when(pl.program_id(1) == 0)
        def _():
            o_ref[...] = jnp.zeros_like(o_ref)

        o_ref[0] += _dot_tn(h_ref[...], jnp.concatenate([p[...] for p in pieces], axis=1))

    def piece(i):
        return pl.BlockSpec((tk, PIECE), lambda j, k: (k, (per * j + i + N_PIECE - DPROJ_ROTATION) % N_PIECE))

    (out,), side_outs = _call(
        body, name="inproj_wgrad", grid=(N_SHARD, T // tk),
        in_specs=[pl.BlockSpec((tk, D), lambda j, k: (k, 0))] + [piece(i) for i in range(per)],
        out_specs=[pl.BlockSpec((1, D, IN_SHARD), lambda j, k: (j, 0, 0))],
        out_shape=[S((N_SHARD, D, IN_SHARD), F32)], args=(h,) + (dproj,) * per, sides=sides)
    return out, side_outs


def _row_tile(rows):
    for t in range(256, 0, -SUBLANES):
        if rows % t == 0:
            return t
    raise ValueError(rows)


def _add_halves(core, grads, theirs, name):
    n = len(grads)
    _, r, cols = grads[0].shape
    half = r // 2
    tr = _row_tile(half)
    nb = half // tr

    def body(core_ref, *refs):
        for g_ref, t_ref, o_ref in zip(refs[:n], refs[n:2 * n], refs[2 * n:]):
            o_ref[...] = (g_ref[...] + t_ref[...]).astype(BF16)

    blk = pl.BlockSpec((1, tr, cols), lambda s, i, core_ref: (s, i, 0))
    mine = pl.BlockSpec((1, tr, cols), lambda s, i, core_ref: (s, core_ref[0] * nb + i, 0))
    gs = pltpu.PrefetchScalarGridSpec(num_scalar_prefetch=1, grid=(N_SHARD, nb),
                                      in_specs=[mine] * n + [blk] * n, out_specs=[blk] * n)
    return pl.pallas_call(
        body, name=name, grid_spec=gs, out_shape=[S((N_SHARD, half, cols), BF16)] * n, compiler_params=_params(2),
    )(core, *grads, *theirs)


def _sum_shards(chip, owns, others, name):
    n = len(owns)
    _, half, cols = owns[0].shape
    tr = _row_tile(half)

    def body(chip_ref, *refs):
        for own_ref, oth_ref, o_ref in zip(refs[:n], refs[n:2 * n], refs[2 * n:]):
            acc = own_ref[0].astype(F32)
            for j in range(3):
                acc = acc + oth_ref[j].astype(F32)
            o_ref[...] = acc

    gs = pltpu.PrefetchScalarGridSpec(
        num_scalar_prefetch=1, grid=(half // tr,),
        in_specs=[pl.BlockSpec((1, tr, cols), lambda i, chip_ref: (chip_ref[0], i, 0))] * n
        + [pl.BlockSpec((3, tr, cols), lambda i, chip_ref: (0, i, 0))] * n,
        out_specs=[pl.BlockSpec((tr, cols), lambda i, chip_ref: (i, 0))] * n)
    return pl.pallas_call(
        body, name=name, grid_spec=gs, out_shape=[S((half, cols), F32)] * n, compiler_params=_params(1),
    )(chip, *owns, *others)


def _adamw(w, g, m, v):
    m = ADAM_B1 * m + (1.0 - ADAM_B1) * g
    v = ADAM_B2 * v + (1.0 - ADAM_B2) * (g * g)
    m_hat = m / (1.0 - ADAM_B1 ** ADAM_STEP)
    v_hat = v / (1.0 - ADAM_B2 ** ADAM_STEP)
    delta = -ADAM_LR * (m_hat / (jnp.sqrt(v_hat) + ADAM_EPS) + ADAM_WD * w)
    return delta, m, v


def _adamw_shard(core, mine, theirs, w, m, v, name):
    n = len(w)
    r, cols = w[0].shape
    half = r // 2
    tr = _row_tile(half)
    nb = half // tr

    def body(core_ref, *refs):
        groups = [refs[k * n:(k + 1) * n] for k in range(9)]
        for mine_ref, theirs_ref, w_ref, m_ref, v_ref, g_ref, d_ref, mo_ref, vo_ref in zip(*groups):
            g = jnp.where(pl.program_id(0) == core_ref[0], mine_ref[...], theirs_ref[...])
            g_ref[...] = g
            d_ref[...], mo_ref[...], vo_ref[...] = _adamw(w_ref[...], g, m_ref[...], v_ref[...])

    hblk = pl.BlockSpec((tr, cols), lambda h, i, core_ref: (i, 0))
    blk = pl.BlockSpec((tr, cols), lambda h, i, core_ref: (h * nb + i, 0))
    gs = pltpu.PrefetchScalarGridSpec(num_scalar_prefetch=1, grid=(2, nb),
                                      in_specs=[hblk] * (2 * n) + [blk] * (3 * n), out_specs=[blk] * (4 * n))
    res = pl.pallas_call(
        body, name=name, grid_spec=gs, out_shape=[S((r, cols), F32)] * (4 * n), compiler_params=_params(2),
    )(core, *mine, *theirs, *w, *m, *v)
    return [tuple(res[k * n + j] for k in range(4)) for j in range(n)]


def _adamw_whole(w, g, m, v, name):
    def body(w_ref, g_ref, m_ref, v_ref, d_ref, mo_ref, vo_ref):
        d_ref[...], mo_ref[...], vo_ref[...] = _adamw(w_ref[...], g_ref[...], m_ref[...], v_ref[...])

    return pl.pallas_call(body, name=name, out_shape=[S(w.shape, F32)] * 3)(w, g, m, v)


_VEC_ROWS = ("norm_mix_g", "conv_b", "rg_lambda", "sgu_ln_g", "sgu_ln_b", "norm_ffn_g", "norm_final_g", "rg_ba",
             "rg_bx")
_CONV_ROW = len(_VEC_ROWS)
_LOSS_ROW = _CONV_ROW + CONV_WIDTH
_VEC_PAD = -(_LOSS_ROW + 1) % SUBLANES
_HEAD_BIASES = ("rg_ba", "rg_bx")
_TENSORS = ("sgu_bs", "sgu_ws", "rg_wa", "rg_wx")
_HEAD_PAIRS = ("rg_wa", "rg_wx")


def _small_sum_adamw(parts, w, m, v):
    names = [n for n in _VEC_ROWS] + list(_TENSORS)
    n_parts = len(parts)

    def total(ref):
        acc = ref[0]
        for k in range(1, N_DEVICE):
            acc = acc + ref[k]
        return acc

    def body(*refs):
        part_refs, refs = refs[:n_parts], refs[n_parts:]
        w_refs, m_refs, v_refs = (dict(zip(names, refs[k * len(names):(k + 1) * len(names)])) for k in range(3))
        outs = refs[3 * len(names):]
        out_refs = {n: outs[4 * k:4 * k + 4] for k, n in enumerate(names)}
        conv_ref, loss_ref = outs[4 * len(names):]
        vec = total(part_refs[0])
        grads = {n: total(p) for n, p in zip(_TENSORS, part_refs[1:])}
        for n in _HEAD_PAIRS:
            pairs = grads[n]
            grads[n] = jnp.stack([pairs[k // 2, :, HEAD_DIM * (k % 2):HEAD_DIM * (k % 2 + 1)]
                                  for k in range(2 * pairs.shape[0])], axis=0)
        grads = {n: g[None] for n, g in grads.items()}
        for row, n in enumerate(_VEC_ROWS):
            g = vec[row:row + 1, :]
            if n in _HEAD_BIASES:
                g = jnp.concatenate([g[:, HEAD_DIM * h:HEAD_DIM * (h + 1)] for h in range(D // HEAD_DIM)], axis=0)[None]
            grads[n] = g
        for n in names:
            g_ref, d_ref, mo_ref, vo_ref = out_refs[n]
            g_ref[...] = grads[n]
            d_ref[...], mo_ref[...], vo_ref[...] = _adamw(w_refs[n][...], grads[n], m_refs[n][...], v_refs[n][...])
        conv_ref[...] = vec[_CONV_ROW:_CONV_ROW + CONV_WIDTH, :]
        loss_ref[...] = vec[_LOSS_ROW:_LOSS_ROW + 1, 0:1]

    res = pl.pallas_call(
        body, name="small_sum_adamw",
        out_shape=[S(w[n].shape, F32) for n in names for _ in range(4)] + [S((CONV_WIDTH, D), F32), S((1, 1), F32)],
        compiler_params=pltpu.CompilerParams(vmem_limit_bytes=VMEM_LIMIT),
    )(*parts, *[w[n] for n in names], *[m[n] for n in names], *[v[n] for n in names])
    return {n: tuple(res[4 * k:4 * k + 4]) for k, n in enumerate(names)}, res[-2], res[-1]


_BIG = ("w_in", "w_proj_a", "w_proj_b", "w_out", "w_gate_up", "w_down")
_WEIGHTS = ("norm_mix_g", "w_in", "conv_w", "conv_b", "rg_wa", "rg_ba", "rg_wx", "rg_bx", "rg_lambda", "sgu_ln_g",
            "sgu_ln_b", "sgu_ws", "sgu_bs", "w_proj_a", "w_proj_b", "w_out", "norm_ffn_g", "w_gate_up", "w_down",
            "norm_final_g")


def kernel(x, norm_mix_g, w_in, conv_w, conv_b, rg_wa, rg_ba, rg_wx, rg_bx, rg_lambda, sgu_ln_g, sgu_ln_b, sgu_ws, sgu_bs, w_proj_a, w_proj_b, w_out, norm_ffn_g, w_gate_up, w_down, norm_final_g, loss_target, m_norm_mix_g, m_w_in, m_conv_w, m_conv_b, m_rg_wa, m_rg_ba, m_rg_wx, m_rg_bx, m_rg_lambda, m_sgu_ln_g, m_sgu_ln_b, m_sgu_ws, m_sgu_bs, m_w_proj_a, m_w_proj_b, m_w_out, m_norm_ffn_g, m_w_gate_up, m_w_down, m_norm_final_g, v_norm_mix_g, v_w_in, v_conv_w, v_conv_b, v_rg_wa, v_rg_ba, v_rg_wx, v_rg_bx, v_rg_lambda, v_sgu_ln_g, v_sgu_ln_b, v_sgu_ws, v_sgu_bs, v_w_proj_a, v_w_proj_b, v_w_out, v_norm_ffn_g, v_w_gate_up, v_w_down, v_norm_final_g):
    args = dict(locals())
    w = {n: args[n] for n in _WEIGHTS}
    mom = {n: args["m_" + n] for n in _WEIGHTS}
    var = {n: args["v_" + n] for n in _WEIGHTS}
    xi, yi, ci = _position()
    core = ci.astype(jnp.int32).reshape(1)
    chip = (2 * xi + yi).astype(jnp.int32).reshape(1)

    bf = {n: w[n][0].astype(BF16) for n in _BIG}
    final_g = w["norm_final_g"].reshape(1, D)
    ba, bx = w["rg_ba"].reshape(1, D), w["rg_bx"].reshape(1, D)
    lam, ln_g, ln_b = w["rg_lambda"], w["sgu_ln_g"], w["sgu_ln_b"]
    x0, target = x[0], loss_target[0]

    def shard_major(g):
        return g.reshape(N_SHARD, g.shape[0] // N_SHARD, g.shape[1])

    def per_shape(names, fn, *lists):
        if len({a.shape for a in lists[0]}) == 1:
            return fn("_".join(names), *lists)
        return [r for k, n in enumerate(names) for r in fn(n, *[[a[k]] for a in lists])]

    def chip_sums(names, grads, theirs):
        return per_shape(names, lambda s, g, t: _add_halves(core, g, t, "add_halves_" + s), grads, theirs)

    def my_halves(names, sums, arrived):
        return per_shape(names, lambda s, p, a: _sum_shards(chip, p, a, "sum_shards_" + s), sums, arrived)

    (proj, h), ((w_in_a,), (conv_a,)) = _inproj_own(
        chip, x0, w["norm_mix_g"], bf["w_in"],
        sides=[_gather_half_side([bf["w_in"]], relations=(0, 1)), _gather_side([w["conv_w"][0]])])
    conv_cols = conv_a.shape[-1]
    conv_full = jnp.swapaxes(conv_a, 0, 1).reshape(CONV_WIDTH, D)
    (proj,), ((w_in_a,),) = _inproj_rest(
        chip, proj, h, w_in_a, 1, 2, "inproj_near",
        sides=[_gather_half_side([bf["w_in"]], relations=(2,), into=[w_in_a])])
    (proj,), _ = _inproj_rest(chip, proj, h, w_in_a, 3, 1, "inproj_far")
    (ya, xc, hseq, gates), ((w_pa_a, w_pb_a, w_out_a, w_gu_a, w_down_a),) = _rglru_fwd(
        proj, conv_full, w["conv_b"], w["rg_wa"], w["rg_wx"], ba, bx, lam,
        sides=[_gather_half_side([bf[n] for n in ("w_proj_a", "w_proj_b", "w_out", "w_gate_up", "w_down")])])
    wpa, wpb, wout, wdown = w_pa_a.reshape(D, D), w_pb_a.reshape(D, D), w_out_a.reshape(D, D), w_down_a.reshape(D_FF, D)
    yb, pa, pb, mb, x1 = _sgu_merge_fwd(x0, proj, ya, ln_g, ln_b, w["sgu_ws"], w["sgu_bs"], wpa, wpb, wout)
    h2, gu, act, dx2, dx2b, loss, d_final_g = _ffn_fwd_loss(x1, w["norm_ffn_g"], w_gu_a, wdown, final_g, target)

    dgu, dx1, dx1b, d_ffn_g = _ffn_bwd(dx2, dx2b, gu, wdown, w_gu_a, x1, w["norm_ffn_g"])
    ffn = ("w_gate_up", "w_down")
    g_ffn = [_matmul_tn(h2, dgu, FF_SHARD, True, "wgrad_gate_up")[0],
             shard_major(_matmul_tn(act, dx2b, D // 2, False, "wgrad_down")[0])]
    (dpa, dpb, dya, dproj, d_ws, d_bs, d_lng, d_lnb), (theirs_ffn,) = _merge_sgu_bwd(
        dx1b, proj, pa, pb, ln_g, ln_b, w["sgu_ws"], w["sgu_bs"], wpa, wpb, wout, sides=[_halves_side(g_ffn)])
    sums_ffn = chip_sums(ffn, g_ffn, theirs_ffn)
    mix = ("w_proj_a", "w_proj_b", "w_out")
    g_mix = [shard_major(_matmul_tn(ya, dpa, D, False, "wgrad_proj_a")[0]),
             shard_major(_matmul_tn(yb, dpb, D, False, "wgrad_proj_b")[0]),
             shard_major(_matmul_tn(mb, dx1b, D, False, "wgrad_out")[0])]
    (dproj, d_cw, d_cb, d_wa, d_wx, d_ba, d_bx, d_lam), (arrived_ffn, theirs_mix, sgu_parts) = _rglru_bwd(
        dya, dproj, proj, hseq, xc, gates, conv_full, w["rg_wa"], w["rg_wx"], lam,
        sides=[_scatter_side(sums_ffn), _halves_side(g_mix), _everyone_side([d_bs, d_ws])])
    mine_ffn = my_halves(ffn, sums_ffn, arrived_ffn)
    sums_mix = chip_sums(mix, g_mix, theirs_mix)
    g_in, ((wa_parts, wx_parts), other_ffn, arrived_mix) = _inproj_wgrad(
        h, dproj, sides=[_everyone_side([d_wa, d_wx]), _swap_side(mine_ffn),
                         _scatter_side(sums_mix)])
    mine_mix = my_halves(mix, sums_mix, arrived_mix)
    n_tiles = x0.shape[0] // min(TM_MM, x0.shape[0])
    n_first = max(1, n_tiles * 3 // 8)
    (dh,), (theirs_in, other_mix) = _inproj_dh(dproj, w_in_a, None, 0, n_first, "inproj_dh_a",
                                               sides=[_halves_side([g_in]), _swap_side(mine_mix)])
    sums_in = chip_sums(("w_in",), [g_in], theirs_in)
    (dh,), (arrived_in,) = _inproj_dh(dproj, w_in_a, dh, n_first, n_tiles - n_first, "inproj_dh_b",
                                      sides=[_scatter_side(sums_in)])
    mine_in = my_halves(("w_in",), sums_in, arrived_in)
    grad_x, d_mix_g = _inproj_norm_bwd(dh, x0, w["norm_mix_g"], dx1)
    rows = {"norm_mix_g": d_mix_g, "conv_b": d_cb, "rg_lambda": d_lam, "sgu_ln_g": d_lng, "sgu_ln_b": d_lnb,
            "norm_ffn_g": d_ffn_g, "norm_final_g": d_final_g, "rg_ba": d_ba, "rg_bx": d_bx}
    vec = jnp.concatenate([rows[n] for n in _VEC_ROWS]
                          + [d_cw, jnp.pad(loss, ((0, _VEC_PAD), (0, D - 1)))], axis=0)
    other_in, (vec_parts,) = _comm_only([_swap_side(mine_in), _everyone_side([vec])], "swap_w_in")
    small_parts = [vec_parts] + sgu_parts + [wa_parts, wx_parts]

    out = {}
    for names, gm, go in ((ffn, mine_ffn, other_ffn), (mix, mine_mix, other_mix), (("w_in",), mine_in, other_in)):
        results = per_shape(
            names, lambda s, *lists: _adamw_shard(core, *lists, "adamw_" + s),
            gm, go, [w[n][0] for n in names], [mom[n][0] for n in names], [var[n][0] for n in names])
        for n, res in zip(names, results):
            out[n] = tuple(a[None] for a in res)
    as_row = lambda t: {n: a.reshape(1, D) if n == "norm_final_g" else a for n, a in t.items()}
    small_out, conv_sum, loss_sum = _small_sum_adamw(small_parts, as_row(w), as_row(mom), as_row(var))
    out.update(small_out)
    out["norm_final_g"] = tuple(a.reshape(D) for a in small_out["norm_final_g"])
    conv_g = lax.dynamic_slice_in_dim(conv_sum, chip[0] * conv_cols, conv_cols, axis=1)
    d, mo, vo = _adamw_whole(w["conv_w"][0], conv_g, mom["conv_w"][0], var["conv_w"][0], "adamw_conv_w")
    out["conv_w"] = tuple(a[None] for a in (conv_g, d, mo, vo))

    return (loss_sum[0, 0], grad_x[None], *[out[n][0] for n in _WEIGHTS], *[out[n][1] for n in _WEIGHTS],
            *[out[n][2] for n in _WEIGHTS], *[out[n][3] for n in _WEIGHTS])
```

```python
import functools

import jax
import jax.numpy as jnp
from jax import lax
from jax.experimental import pallas as pl
from jax.experimental.pallas import tpu as pltpu

F32 = jnp.float32
BF16 = jnp.bfloat16
S = jax.ShapeDtypeStruct

D = 1024
N_SHARD = 4
IN_COLS = 6 * D
IN_SHARD = IN_COLS // N_SHARD
D_FF = 2816
FF_SHARD = 2 * D_FF // N_SHARD
RG_BLOCK = 256
N_RG_BLOCK = D // RG_BLOCK
CHUNK = 128
N_GROUP = 8
CONV_WIDTH = 4
RG_C = 8.0
EPS = 1e-6
ADAM_LR, ADAM_B1, ADAM_B2, ADAM_EPS, ADAM_WD, ADAM_STEP = 0.001, 0.9, 0.999, 1e-08, 0.01, 10

V7X_VMEM_BYTES = 64 * 1024 * 1024
VMEM_LIMIT = V7X_VMEM_BYTES * 3 // 4
SUBLANES = 8
MESH = pl.DeviceIdType.MESH

TM_MM = 512
TM_SCAN = 256
TM_FF = 256
TK_WGRAD = 1024


def _params(n_axes):
    return pltpu.CompilerParams(dimension_semantics=("arbitrary",) * n_axes, vmem_limit_bytes=VMEM_LIMIT)


def _resident(shape):
    nd = len(shape)
    return pl.BlockSpec(shape, lambda *_: (0,) * nd, pipeline_mode=pl.Buffered(1))


def _sig(x):
    return 1.0 / (1.0 + jnp.exp(-x))


_GELU_K2 = 2.0 * 0.7978845608028654
_GELU_C = 0.044715


def _gelu(x):
    return x * _sig(x * (_GELU_K2 + (_GELU_K2 * _GELU_C) * (x * x)))


def _gelu_and_grad(x):
    x2 = x * x
    s = _sig(x * (_GELU_K2 + (_GELU_K2 * _GELU_C) * x2))
    g = x * s
    return g, s + g * (1.0 - s) * (_GELU_K2 + (3.0 * _GELU_K2 * _GELU_C) * x2)


_EXPM1_SERIES = tuple(1.0 / f for f in (5040.0, 720.0, 120.0, 24.0, 6.0, 2.0, 1.0))


def _one_minus_exp(x):
    p = _EXPM1_SERIES[0]
    for coef in _EXPM1_SERIES[1:]:
        p = p * x + coef
    return jnp.where(x > -0.125, -x * p, 1.0 - jnp.exp(x))


def _softplus_neg(lam):
    z = -lam
    e = jnp.exp(-jnp.abs(z))
    u = 1.0 + e
    log1p = jnp.where(u == 1.0, e, jnp.log(u) * e / (u - 1.0))
    return jnp.maximum(z, 0.0) + log1p


def _rms_stats(x):
    return lax.rsqrt(jnp.mean(x * x, axis=-1, keepdims=True) + EPS)


def _rms_bwd(dy, x, g):
    rstd = _rms_stats(x)
    xhat = x * rstd
    dxhat = dy * g
    dx = rstd * (dxhat - xhat * jnp.mean(dxhat * xhat, axis=-1, keepdims=True))
    return dx, dy * xhat


def _colsum(x):
    return jnp.sum(x, axis=0, keepdims=True)


def _shift_down(x, d, fill):
    n = x.shape[0]
    if d % SUBLANES == 0:
        return jnp.concatenate([jnp.full((d, x.shape[1]), fill, x.dtype), x[:n - d]], axis=0)
    row = lax.broadcasted_iota(jnp.int32, x.shape, 0)
    return jnp.where(row < d, fill, pltpu.roll(x, d, 0))


def _shift_up(x, d, fill):
    n = x.shape[0]
    if d % SUBLANES == 0:
        return jnp.concatenate([x[d:], jnp.full((d, x.shape[1]), fill, x.dtype)], axis=0)
    row = lax.broadcasted_iota(jnp.int32, x.shape, 0)
    return jnp.where(row >= n - d, fill, pltpu.roll(x, n - d, 0))


def _scan(a, b, shift):
    d = 1
    while d < a.shape[0]:
        b = a * shift(b, d, 0.0) + b
        a = a * shift(a, d, 1.0)
        d *= 2
    return a, b


LANES = 128


def _scan_tile(a, b, outside, a_s, b_s, h_s, reverse):
    tm = a.shape[0]
    groups = tm // SUBLANES
    order = list(range(SUBLANES - 1, -1, -1) if reverse else range(SUBLANES))
    shift = _shift_up if reverse else _shift_down
    edge = groups - 1 if reverse else 0
    for j in range(D // LANES):
        a_s[j] = a[:, LANES * j:LANES * (j + 1)]
        b_s[j] = b[:, LANES * j:LANES * (j + 1)]
    for j in range(D // LANES):
        def slab(ref, k):
            return ref[j, pl.ds(k, groups, stride=SUBLANES), :]

        ga, gb = slab(a_s, order[0]), slab(b_s, order[0])
        for k in order[1:]:
            ak = slab(a_s, k)
            gb = ak * gb + slab(b_s, k)
            ga = ak * ga
        ga, gb = _scan(ga, gb, shift)
        h_out = outside[:, LANES * j:LANES * (j + 1)]
        group_end = ga * h_out + gb
        row = lax.broadcasted_iota(jnp.int32, (groups, LANES), 0)
        h = jnp.where(row == edge, h_out, shift(group_end, 1, 0.0))
        for k in order:
            h = slab(a_s, k) * h + slab(b_s, k)
            h_s[j, pl.ds(k, groups, stride=SUBLANES), :] = h
    return jnp.concatenate([h_s[j] for j in range(D // LANES)], axis=1)


def _dot(a, b):
    return jnp.dot(a, b, preferred_element_type=F32)


def _dot_nt(a, b):
    return lax.dot_general(a, b, (((1,), (1,)), ((), ())), preferred_element_type=F32)


def _dot_tn(a, b):
    return lax.dot_general(a, b, (((0,), (0,)), ((), ())), preferred_element_type=F32)


_ANY = pl.BlockSpec(memory_space=pl.ANY)


def _position():
    return lax.axis_index("x"), lax.axis_index("y"), lax.axis_index("c")


def _other_chips(x, y):
    return [(1 - x, y), (x, 1 - y), (1 - x, 1 - y)]


class _Side:
    def __init__(self, inputs, out_shapes, n_sems, make, continues=()):
        self.inputs, self.out_shapes, self.n_sems, self.make = list(inputs), list(out_shapes), n_sems, make
        self.continues = list(continues)


MID_STEP = 0.625
LATE_MID_STEP = 0.875


def _call(body, *, name, grid, in_specs, out_specs, out_shape, args, scratch_shapes=(), sides=(), aliases=None,
          scalars=None, mid=MID_STEP):
    n_in, n_out, n_scr = len(in_specs), len(out_specs), len(scratch_shapes)
    n_scalar = 0 if scalars is None else 1
    side_in = [len(s.inputs) + len(s.continues) for s in sides]
    side_out = [len(s.out_shapes) for s in sides]
    all_aliases = {k + n_scalar: v for k, v in (aliases or {}).items()}
    for idx, s in enumerate(sides):
        for k in range(len(s.continues)):
            operand = n_scalar + n_in + sum(side_in[:idx]) + len(s.inputs) + k
            all_aliases[operand] = n_out + sum(side_out[:idx]) + k

    def wrapped(*refs):
        refs = list(refs)
        take = lambda k: [refs.pop(0) for _ in range(k)]
        ins = take(n_scalar) + take(n_in)
        sins = [take(k) for k in side_in]
        outs = take(n_out)
        souts = [take(k) for k in side_out]
        scr = take(n_scr)
        sems = [take(3) for _ in sides]
        def run(phase):
            for s, si, so, sem in zip(sides, sins, souts, sems):
                for thunk in s.make(si[:len(s.inputs)], so, *sem)[phase]:
                    thunk()

        if sides:
            n_steps = functools.reduce(lambda a, b: a * b, grid)
            step = functools.reduce(lambda a, b: a + b, [
                pl.program_id(a) * functools.reduce(lambda p, q: p * q, grid[a + 1:], 1) for a in range(len(grid))])
            pl.when(step == 0)(lambda: run(0))
        body(*ins, *outs, *scr)
        if sides:
            pl.when(step == int(mid * (n_steps - 1)))(lambda: run(1))
            pl.when(step == n_steps - 1)(lambda: run(2))

    grid_spec = pltpu.PrefetchScalarGridSpec(
        num_scalar_prefetch=n_scalar, grid=grid,
        in_specs=list(in_specs) + [_ANY] * sum(side_in),
        out_specs=list(out_specs) + [_ANY] * sum(side_out),
        scratch_shapes=list(scratch_shapes) + [pltpu.SemaphoreType.DMA((s.n_sems,)) for s in sides for _ in range(3)])
    res = pl.pallas_call(
        wrapped, name=name, grid_spec=grid_spec,
        out_shape=list(out_shape) + [o for s in sides for o in s.out_shapes],
        input_output_aliases=all_aliases,
        compiler_params=_params(len(grid)),
    )(*([scalars] if n_scalar else []), *args, *[a for s in sides for a in s.inputs + s.continues])
    main, rest, per_side = list(res[:n_out]), list(res[n_out:]), []
    for k in side_out:
        per_side.append(rest[:k])
        rest = rest[k:]
    return main, per_side


def _comm_only(sides, name):
    def body():
        pass

    return _call(body, name=name, grid=(1,), in_specs=[], out_specs=[], out_shape=[], args=[], sides=sides)[1]


def _remote(src, dst, send, recv, k, device):
    return pltpu.make_async_remote_copy(src_ref=src, dst_ref=dst, send_sem=send.at[k], recv_sem=recv.at[k],
                                        device_id=device, device_id_type=MESH)


def _both_ways(copy, keys):
    return [lambda k=k: copy(k).start() for k in keys], [], [lambda k=k: copy(k).wait() for k in keys]


def _gather_side(shards):
    n = len(shards)

    def make(ins, outs, send, recv, local):
        x, y, c = _position()
        mine = 2 * x + y
        chips = _other_chips(x, y)
        pairs = [(w, j) for w in range(n) for j in range(3)]

        def own(w):
            return pltpu.make_async_copy(ins[w], outs[w].at[mine], local.at[w])

        def push(w, j):
            return _remote(ins[w], outs[w].at[mine], send, recv, 3 * w + j, (*chips[j], c))

        def arrival(w, j):
            px, py = chips[j]
            return _remote(ins[w], outs[w].at[2 * px + py], send, recv, 3 * w + j, (px, py, c))

        starts = [lambda w=w: own(w).start() for w in range(n)] + [lambda w=w, j=j: push(w, j).start() for w, j in pairs]
        waits = ([lambda w=w, j=j: arrival(w, j).wait_recv() for w, j in pairs]
                 + [lambda w=w, j=j: push(w, j).wait_send() for w, j in pairs]
                 + [lambda w=w: own(w).wait() for w in range(n)])
        return starts, [], waits

    return _Side(shards, [S((N_SHARD,) + s.shape, s.dtype) for s in shards], 3 * n, make)


def _gather_half_side(shards, relations=(0, 1, 2), into=None):
    n = len(shards)

    def make(ins, outs, send, recv, local):
        x, y, c = _position()
        mine = 2 * x + y
        chips = _other_chips(x, y)
        pairs = [(w, j) for w in range(n) for j in relations]

        def rows(w, core):
            half = ins[w].shape[0] // 2
            return pl.ds(core * half, half)

        def own(w):
            return pltpu.make_async_copy(ins[w], outs[w].at[mine], local.at[w])

        def push(w, j):
            return _remote(ins[w].at[rows(w, c), :], outs[w].at[mine, rows(w, c), :], send, recv, 3 * w + j,
                           (*chips[j], c))

        def landed(w, j, core):
            px, py = chips[j]
            return outs[w].at[2 * px + py, rows(w, core), :]

        def arrival(w, j):
            return _remote(ins[w].at[rows(w, c), :], landed(w, j, c), send, recv, 3 * w + j, (*chips[j], c))

        def passed(w, j, core):
            return _remote(landed(w, j, core), landed(w, j, core), send, recv, 3 * n + 3 * w + j, (x, y, 1 - c))

        owns = range(n) if into is None else ()
        starts = [lambda w=w: own(w).start() for w in owns] + [lambda w=w, j=j: push(w, j).start() for w, j in pairs]
        mids = [t for w, j in pairs for t in (lambda w=w, j=j: arrival(w, j).wait_recv(),
                                              lambda w=w, j=j: passed(w, j, c).start())]
        waits = ([lambda w=w, j=j: passed(w, j, 1 - c).wait_recv() for w, j in pairs]
                 + [lambda w=w, j=j: passed(w, j, c).wait_send() for w, j in pairs]
                 + [lambda w=w, j=j: push(w, j).wait_send() for w, j in pairs]
                 + [lambda w=w: own(w).wait() for w in owns])
        return starts, mids, waits

    return _Side(shards, [S((N_SHARD,) + s.shape, s.dtype) for s in shards], 6 * n, make, continues=into or ())


def _halves_side(grads):
    n = len(grads)

    def make(ins, outs, send, recv, local):
        x, y, c = _position()

        def copy(w):
            half = ins[w].shape[1] // 2
            return _remote(ins[w].at[:, pl.ds((1 - c) * half, half), :], outs[w], send, recv, w, (x, y, 1 - c))

        return _both_ways(copy, range(n))

    return _Side(grads, [S((N_SHARD, g.shape[1] // 2, g.shape[2]), F32) for g in grads], n, make)


def _scatter_side(partials):
    n = len(partials)

    def make(ins, outs, send, recv, local):
        x, y, c = _position()
        chips = _other_chips(x, y)

        def copy(k):
            w, j = divmod(k, 3)
            px, py = chips[j]
            return _remote(ins[w].at[2 * px + py], outs[w].at[j], send, recv, k, (px, py, c))

        return _both_ways(copy, range(3 * n))

    return _Side(partials, [S((3,) + p.shape[1:], p.dtype) for p in partials], 3 * n, make)


def _swap_side(halves):
    n = len(halves)

    def make(ins, outs, send, recv, local):
        x, y, c = _position()
        return _both_ways(lambda w: _remote(ins[w], outs[w], send, recv, w, (x, y, 1 - c)), range(n))

    return _Side(halves, [S(h.shape, h.dtype) for h in halves], n, make)


N_DEVICE = 8


def _everyone_side(arrays):
    n = len(arrays)
    peers = N_DEVICE - 1

    def make(ins, outs, send, recv, local):
        x, y, c = _position()
        mine = 4 * x + 2 * y + c
        pairs = [(w, k) for w in range(n) for k in range(1, N_DEVICE)]

        def peer(k):
            return (1 - x if k & 4 else x, 1 - y if k & 2 else y, 1 - c if k & 1 else c)

        def own(w):
            return pltpu.make_async_copy(ins[w], outs[w].at[mine], local.at[w])

        def push(w, k):
            return _remote(ins[w], outs[w].at[mine], send, recv, peers * w + k - 1, peer(k))

        def arrival(w, k):
            px, py, pc = peer(k)
            return _remote(ins[w], outs[w].at[4 * px + 2 * py + pc], send, recv, peers * w + k - 1, (px, py, pc))

        starts = [lambda w=w: own(w).start() for w in range(n)] + [lambda w=w, k=k: push(w, k).start() for w, k in pairs]
        waits = ([lambda w=w, k=k: arrival(w, k).wait_recv() for w, k in pairs]
                 + [lambda w=w, k=k: push(w, k).wait_send() for w, k in pairs]
                 + [lambda w=w: own(w).wait() for w in range(n)])
        return starts, [], waits

    return _Side(arrays, [S((N_DEVICE,) + a.shape, a.dtype) for a in arrays], peers * n, make)


def _inproj_own(chip, x, g, w_shard, sides=()):
    T = x.shape[0]
    tm = min(TM_MM, T)

    def body(chip_ref, x_ref, g_ref, w_ref, proj_ref, h_ref):
        xv = x_ref[...]
        h = (xv * _rms_stats(xv) * g_ref[...]).astype(BF16)
        h_ref[...] = h
        proj_ref[...] = _dot(h, w_ref[...]).astype(BF16)

    return _call(
        body, name="inproj_own", grid=(T // tm,),
        in_specs=[pl.BlockSpec((tm, D), lambda i, c: (i, 0)), pl.BlockSpec((1, D), lambda i, c: (0, 0)),
                  pl.BlockSpec((D, IN_SHARD), lambda i, c: (0, 0), pipeline_mode=pl.Buffered(1))],
        out_specs=[pl.BlockSpec((tm, IN_SHARD), lambda i, c: (i, c[0])), pl.BlockSpec((tm, D), lambda i, c: (i, 0))],
        out_shape=[S((T, IN_COLS), BF16), S((T, D), BF16)],
        args=(x, g, w_shard), sides=sides, scalars=chip, mid=LATE_MID_STEP)


def _inproj_rest(chip, proj, h, w_in, first, count, name, sides=(), mid=MID_STEP):
    T = h.shape[0]
    tm = min(TM_MM, T)

    def body(chip_ref, _, h_ref, w_ref, proj_ref):
        proj_ref[...] = _dot(h_ref[...], w_ref[0]).astype(BF16)

    def other(p, c):
        return jnp.bitwise_xor(c[0], first + p)

    return _call(
        body, name=name, grid=(count, T // tm),
        in_specs=[_ANY, pl.BlockSpec((tm, D), lambda p, i, c: (i, 0)),
                  pl.BlockSpec((1, D, IN_SHARD), lambda p, i, c: (other(p, c), 0, 0))],
        out_specs=[pl.BlockSpec((tm, IN_SHARD), lambda p, i, c: (i, other(p, c)))],
        out_shape=[S((T, IN_COLS), BF16)],
        args=(proj, h, w_in), sides=sides, scalars=chip, aliases={0: 0}, mid=mid)


def _rg_gates(xc, wa_ref, wx_ref, ba, bx, sp):
    xb = xc.astype(BF16)
    blocks = [xb[:, RG_BLOCK * j:RG_BLOCK * (j + 1)] for j in range(N_RG_BLOCK)]
    r = _sig(jnp.concatenate([_dot(blocks[j], wa_ref[j]) for j in range(N_RG_BLOCK)], axis=1) + ba)
    gi = _sig(jnp.concatenate([_dot(blocks[j], wx_ref[j]) for j in range(N_RG_BLOCK)], axis=1) + bx)
    log_a = (-RG_C) * r * sp
    a = jnp.exp(log_a)
    m = jnp.sqrt(_one_minus_exp(2.0 * log_a))
    return xb, r, gi, a, m


N_GATES = 4
HEADS_PER_BLOCK = 4
HEAD_DIM = RG_BLOCK // HEADS_PER_BLOCK
_RG_BLOCKS_BF16 = pltpu.VMEM((N_RG_BLOCK, RG_BLOCK, RG_BLOCK), BF16)


def _fill_blockdiag(heads_ref, blocks):
    blocks[...] = jnp.zeros_like(blocks)
    for j in range(N_RG_BLOCK):
        for h in range(HEADS_PER_BLOCK):
            sl = slice(HEAD_DIM * h, HEAD_DIM * (h + 1))
            blocks[j, sl, sl] = heads_ref[0, HEADS_PER_BLOCK * j + h].astype(BF16)


def _rglru_fwd(proj, conv_w, conv_b, rg_wa, rg_wx, ba, bx, lam, sides=()):
    T = proj.shape[0]
    tm = min(TM_SCAN, T)

    def body(rx_ref, gate_ref, cw_ref, cb_ref, wah_ref, wxh_ref, ba_ref, bx_ref, lam_ref,
             ya_ref, xc_ref, h_ref, gates_ref, ext, hc, a_s, b_s, h_s, wa_ref, wx_ref):
        @pl.when(pl.program_id(0) == 0)
        def _():
            ext[0:SUBLANES, :] = jnp.zeros((SUBLANES, D), F32)
            hc[...] = jnp.zeros((SUBLANES, D), F32)
            _fill_blockdiag(wah_ref, wa_ref)
            _fill_blockdiag(wxh_ref, wx_ref)

        ext[SUBLANES:SUBLANES + tm, :] = rx_ref[...].astype(F32)
        xc = cb_ref[...]
        for k in range(CONV_WIDTH):
            xc = xc + ext[pl.ds(SUBLANES - (CONV_WIDTH - 1) + k, tm), :] * cw_ref[k:k + 1, :]
        ext[0:SUBLANES, :] = ext[tm:tm + SUBLANES, :]
        xc_ref[...] = xc
        _, r, gi, a, m = _rg_gates(xc, wa_ref, wx_ref, ba_ref[...], bx_ref[...], _softplus_neg(lam_ref[...]))
        for k, val in enumerate((r, gi, a, m)):
            gates_ref[:, D * k:D * (k + 1)] = val
        h = _scan_tile(a, m * (gi * xc), hc[0:1, :], a_s, b_s, h_s, reverse=False)
        hc[...] = jnp.broadcast_to(h[tm - 1:tm, :], (SUBLANES, D))
        h_ref[...] = h
        ya_ref[...] = (_gelu(gate_ref[...].astype(F32)) * h).astype(BF16)

    vec = pl.BlockSpec((1, D), lambda i: (0, 0))
    heads = pl.BlockSpec(rg_wa.shape, lambda i: (0, 0, 0, 0))
    tile = pl.BlockSpec((tm, D), lambda i: (i, 0))
    return _call(
        body, name="rglru_fwd", grid=(T // tm,),
        in_specs=[pl.BlockSpec((tm, D), lambda i: (i, 0)), pl.BlockSpec((tm, D), lambda i: (i, 1)),
                  pl.BlockSpec((CONV_WIDTH, D), lambda i: (0, 0)), vec, heads, heads, vec, vec, vec],
        out_specs=[tile, tile, tile, pl.BlockSpec((tm, N_GATES * D), lambda i: (i, 0))],
        out_shape=[S((T, D), BF16), S((T, D), F32), S((T, D), F32), S((T, N_GATES * D), F32)],
        scratch_shapes=[pltpu.VMEM((tm + SUBLANES, D), F32), pltpu.VMEM((SUBLANES, D), F32)]
        + [pltpu.VMEM((D // LANES, tm, LANES), F32)] * 3 + [_RG_BLOCKS_BF16] * 2,
        args=(proj, proj, conv_w, conv_b, rg_wa, rg_wx, ba, bx, lam), sides=sides, mid=LATE_MID_STEP)


def _layer_norm_stats(v):
    mu = jnp.mean(v, axis=-1, keepdims=True)
    vc = v - mu
    rstd = lax.rsqrt(jnp.mean(vc * vc, axis=-1, keepdims=True) + EPS)
    return vc * rstd, rstd


def _sgu_mix(w_ref, vnb, bst_ref, n_chunk):
    cols = []
    for g in range(N_GROUP):
        vg = vnb[:, CHUNK * g:CHUNK * (g + 1)].reshape(n_chunk, CHUNK, CHUNK)
        wb = jnp.broadcast_to(w_ref[g][None], (n_chunk, CHUNK, CHUNK))
        mg = lax.dot_general(wb, vg, (((2,), (1,)), ((0,), (0,))), preferred_element_type=F32)
        mg = mg + bst_ref[:, g:g + 1][None]
        cols.append(mg.reshape(n_chunk * CHUNK, CHUNK))
    return jnp.concatenate(cols, axis=1)


def _causal_mask():
    return (lax.broadcasted_iota(jnp.int32, (CHUNK, CHUNK), 0) >= lax.broadcasted_iota(jnp.int32, (CHUNK, CHUNK), 1))


def _fill_sgu_weights(ws_ref, bs_ref, w_tril, bs_t, w_tril_t=None):
    keep = _causal_mask()
    for g in range(N_GROUP):
        wg = jnp.where(keep, ws_ref[0, g], 0.0)
        w_tril[g] = wg.astype(BF16)
        if w_tril_t is not None:
            w_tril_t[g] = wg.T.astype(BF16)
    bs_t[...] = bs_ref[0].T


_SGU_W_BF16 = pltpu.VMEM((N_GROUP, CHUNK, CHUNK), BF16)
_SGU_BT = pltpu.VMEM((CHUNK, N_GROUP), F32)


def _sgu_merge_fwd(x, proj, ya, ln_g, ln_b, sgu_ws, sgu_bs, wpa, wpb, wout):
    T = x.shape[0]
    tm = min(TM_FF, T)
    n_chunk = tm // CHUNK

    def body(x_ref, uv_ref, gab_ref, ya_ref, g_ref, b_ref, ws_ref, bs_ref, wpa_ref, wpb_ref, wout_ref,
             yb_ref, pa_ref, pb_ref, mb_ref, x1_ref, w_ref, bst_ref):
        @pl.when(pl.program_id(0) == 0)
        def _():
            _fill_sgu_weights(ws_ref, bs_ref, w_ref, bst_ref)

        vhat, _ = _layer_norm_stats(_gelu(uv_ref[:, D:2 * D].astype(F32)))
        vnb = (vhat * g_ref[...] + b_ref[...]).astype(BF16)
        yb = (_gelu(uv_ref[:, 0:D].astype(F32)) * _sgu_mix(w_ref, vnb, bst_ref, n_chunk)).astype(BF16)
        yb_ref[...] = yb
        pa = _dot(ya_ref[...], wpa_ref[...])
        pb = _dot(yb, wpb_ref[...])
        pa_ref[...] = pa.astype(BF16)
        pb_ref[...] = pb.astype(BF16)
        mb = (_sig(gab_ref[:, 0:D].astype(F32)) * pa + _sig(gab_ref[:, D:2 * D].astype(F32)) * pb).astype(BF16)
        mb_ref[...] = mb
        x1_ref[...] = x_ref[...] + _dot(mb, wout_ref[...])

    tile = pl.BlockSpec((tm, D), lambda i: (i, 0))
    vec = pl.BlockSpec((1, D), lambda i: (0, 0))
    w = _resident((D, D))
    return pl.pallas_call(
        body, name="sgu_merge_fwd", grid=(T // tm,),
        in_specs=[tile, pl.BlockSpec((tm, 2 * D), lambda i: (i, 1)), pl.BlockSpec((tm, 2 * D), lambda i: (i, 2)), tile,
                  vec, vec, pl.BlockSpec(sgu_ws.shape, lambda i: (0, 0, 0, 0)),
                  pl.BlockSpec(sgu_bs.shape, lambda i: (0, 0, 0)), w, w, w],
        out_specs=[tile, tile, tile, tile, tile],
        out_shape=[S((T, D), BF16), S((T, D), BF16), S((T, D), BF16), S((T, D), BF16), S((T, D), F32)],
        scratch_shapes=[_SGU_W_BF16, _SGU_BT],
        compiler_params=_params(1),
    )(x, proj, proj, ya, ln_g, ln_b, sgu_ws, sgu_bs, wpa, wpb, wout)


def _ffn_fwd_loss(x1, g, w_gu, w_down, g_final, target):
    T = x1.shape[0]
    tm = min(TM_FF, T)

    def body(x_ref, g_ref, wgu_ref, wd_ref, gf_ref, t_ref,
             h2_ref, gu_ref, act_ref, dx2_ref, dx2b_ref, loss_ref, dg_ref):
        @pl.when(pl.program_id(0) == 0)
        def _():
            loss_ref[...] = jnp.zeros_like(loss_ref)
            dg_ref[...] = jnp.zeros_like(dg_ref)

        xv = x_ref[...]
        h2 = (xv * _rms_stats(xv) * g_ref[...]).astype(BF16)
        h2_ref[...] = h2
        x2 = xv
        for k in range(N_SHARD // 2):
            cols = slice(FF_SHARD * k, FF_SHARD * (k + 1))
            gate = _dot(h2, wgu_ref[k])
            up = _dot(h2, wgu_ref[k + N_SHARD // 2])
            gu_ref[:, cols] = gate.astype(BF16)
            gu_ref[:, D_FF + FF_SHARD * k:D_FF + FF_SHARD * (k + 1)] = up.astype(BF16)
            act = (gate * _sig(gate) * up).astype(BF16)
            act_ref[:, cols] = act
            x2 = x2 + _dot(act, wd_ref[cols, :])
        gf = gf_ref[...]
        err = x2 * _rms_stats(x2) * gf - t_ref[...]
        loss_ref[...] += 0.5 * jnp.sum(jnp.mean(err * err, axis=-1, keepdims=True), axis=0, keepdims=True)
        dx2, dg_rows = _rms_bwd(err * (1.0 / D), x2, gf)
        dg_ref[...] += _colsum(dg_rows)
        dx2_ref[...] = dx2
        dx2b_ref[...] = dx2.astype(BF16)

    tile = pl.BlockSpec((tm, D), lambda i: (i, 0))
    vec = pl.BlockSpec((1, D), lambda i: (0, 0))
    return pl.pallas_call(
        body, name="ffn_fwd_loss", grid=(T // tm,),
        in_specs=[tile, vec, _resident((N_SHARD, D, FF_SHARD)), _resident((D_FF, D)), vec, tile],
        out_specs=[tile, pl.BlockSpec((tm, 2 * D_FF), lambda i: (i, 0)), pl.BlockSpec((tm, D_FF), lambda i: (i, 0)),
                   tile, tile, pl.BlockSpec((1, 1), lambda i: (0, 0)), vec],
        out_shape=[S((T, D), BF16), S((T, 2 * D_FF), BF16), S((T, D_FF), BF16), S((T, D), F32), S((T, D), BF16),
                   S((1, 1), F32), S((1, D), F32)],
        compiler_params=_params(1),
    )(x1, g, w_gu, w_down, g_final, target)


def _ffn_bwd(dx2, dx2b, gu, w_down, w_gu, x1, g):
    T = x1.shape[0]
    tm = min(TM_FF, T)

    def body(dx2_ref, dx2b_ref, gu_ref, wd_ref, wgu_ref, x_ref, g_ref, dgu_ref, dx1_ref, dx1b_ref, dg_ref):
        @pl.when(pl.program_id(0) == 0)
        def _():
            dg_ref[...] = jnp.zeros_like(dg_ref)

        dxb = dx2b_ref[...]
        dh2 = jnp.zeros((tm, D), F32)
        for k in range(N_SHARD // 2):
            cols = slice(FF_SHARD * k, FF_SHARD * (k + 1))
            up_cols = slice(D_FF + FF_SHARD * k, D_FF + FF_SHARD * (k + 1))
            dact = _dot_nt(dxb, wd_ref[cols, :])
            gate = gu_ref[:, cols].astype(F32)
            sg = _sig(gate)
            dgate = (dact * gu_ref[:, up_cols].astype(F32) * (sg * (1.0 + gate * (1.0 - sg)))).astype(BF16)
            dup = (dact * (gate * sg)).astype(BF16)
            dgu_ref[:, cols] = dgate
            dgu_ref[:, up_cols] = dup
            dh2 = dh2 + _dot_nt(dgate, wgu_ref[k]) + _dot_nt(dup, wgu_ref[k + N_SHARD // 2])
        dx, dg_rows = _rms_bwd(dh2, x_ref[...], g_ref[...])
        dg_ref[...] += _colsum(dg_rows)
        dx1 = dx2_ref[...] + dx
        dx1_ref[...] = dx1
        dx1b_ref[...] = dx1.astype(BF16)

    tile = pl.BlockSpec((tm, D), lambda i: (i, 0))
    wide = pl.BlockSpec((tm, 2 * D_FF), lambda i: (i, 0))
    vec = pl.BlockSpec((1, D), lambda i: (0, 0))
    return pl.pallas_call(
        body, name="ffn_bwd", grid=(T // tm,),
        in_specs=[tile, tile, wide, _resident((D_FF, D)), _resident((N_SHARD, D, FF_SHARD)), tile, vec],
        out_specs=[wide, tile, tile, vec],
        out_shape=[S((T, 2 * D_FF), BF16), S((T, D), F32), S((T, D), BF16), S((1, D), F32)],
        compiler_params=_params(1),
    )(dx2, dx2b, gu, w_down, w_gu, x1, g)


def _matmul_tn(a, b, tn, shard_major, name, sides=()):
    T, M = a.shape
    N = b.shape[1]
    tk = min(TK_WGRAD, T)

    def body(a_ref, b_ref, o_ref):
        @pl.when(pl.program_id(1) == 0)
        def _():
            o_ref[...] = jnp.zeros_like(o_ref)

        acc = _dot_tn(a_ref[...], b_ref[...])
        if shard_major:
            o_ref[0] += acc
        else:
            o_ref[...] += acc

    if shard_major:
        out_spec, out_shape = pl.BlockSpec((1, M, tn), lambda j, k: (j, 0, 0)), S((N // tn, M, tn), F32)
    else:
        out_spec, out_shape = pl.BlockSpec((M, tn), lambda j, k: (0, j)), S((M, N), F32)
    (out,), side_outs = _call(
        body, name=name, grid=(N // tn, T // tk),
        in_specs=[pl.BlockSpec((tk, M), lambda j, k: (k, 0)), pl.BlockSpec((tk, tn), lambda j, k: (k, j))],
        out_specs=[out_spec], out_shape=[out_shape], args=(a, b), sides=sides)
    return out, side_outs


PIECE = IN_SHARD // 3
N_PIECE = IN_COLS // PIECE
DPROJ_ROTATION = 2 * D // PIECE


def _merge_sgu_bwd(dx1b, proj, pa, pb, ya, yb, mb, ln_g, ln_b, sgu_ws, sgu_bs, wpa, wpb, wout, sides=()):
    T = dx1b.shape[0]
    tm = min(TM_FF, T)
    n_chunk = tm // CHUNK
    n_steps = T // tm

    def body(dx_ref, uv_ref, gab_ref, pa_ref, pb_ref, ya_ref, yb_ref, mb_ref, g_ref, b_ref, ws_ref, bs_ref,
             wpa_ref, wpb_ref, wout_ref,
             dya_ref, dp_ref, dw_ref, dbs_ref, dg_ref, db_ref, gpa_ref, gpb_ref, gout_ref,
             w_ref, wt_ref, bst_ref, acc_pa, acc_pb, acc_out):
        @pl.when(pl.program_id(0) == 0)
        def _():
            for ref in (dw_ref, dbs_ref, dg_ref, db_ref, acc_pa, acc_pb, acc_out):
                ref[...] = jnp.zeros_like(ref)
            _fill_sgu_weights(ws_ref, bs_ref, w_ref, bst_ref, wt_ref)

        dxb = dx_ref[...]
        dm = _dot_nt(dxb, wout_ref[...])
        sa = _sig(gab_ref[:, 0:D].astype(F32))
        sb = _sig(gab_ref[:, D:2 * D].astype(F32))
        dpa = (dm * sa).astype(BF16)
        dpb = (dm * sb).astype(BF16)
        acc_pa[...] += _dot_tn(ya_ref[...], dpa)
        acc_pb[...] += _dot_tn(yb_ref[...], dpb)
        acc_out[...] += _dot_tn(mb_ref[...], dxb)
        dp_ref[:, 2 * D:3 * D] = (dm * pa_ref[...].astype(F32) * (sa * (1.0 - sa))).astype(BF16)
        dp_ref[:, 3 * D:4 * D] = (dm * pb_ref[...].astype(F32) * (sb * (1.0 - sb))).astype(BF16)
        dya_ref[...] = _dot_nt(dpa, wpa_ref[...]).astype(BF16)
        dyb_v = _dot_nt(dpb, wpb_ref[...])

        gu, dgu = _gelu_and_grad(uv_ref[:, 0:D].astype(F32))
        gv, dgv = _gelu_and_grad(uv_ref[:, D:2 * D].astype(F32))
        vhat, rstd = _layer_norm_stats(gv)
        lng = g_ref[...]
        vnb = (vhat * lng + b_ref[...]).astype(BF16)
        mixed = _sgu_mix(w_ref, vnb, bst_ref, n_chunk)
        dp_ref[:, 0:D] = (dyb_v * mixed * dgu).astype(BF16)
        dmix = dyb_v * gu
        dmb = dmix.astype(BF16)
        keep = _causal_mask()
        dvn_cols, dbs_rows = [], []
        for g in range(N_GROUP):
            sl = slice(CHUNK * g, CHUNK * (g + 1))
            dmg = dmb[:, sl].reshape(n_chunk, CHUNK, CHUNK)
            vg = vnb[:, sl].reshape(n_chunk, CHUNK, CHUNK)
            wtb = jnp.broadcast_to(wt_ref[g][None], (n_chunk, CHUNK, CHUNK))
            dvn = lax.dot_general(wtb, dmg, (((2,), (1,)), ((0,), (0,))), preferred_element_type=F32)
            dvn_cols.append(dvn.reshape(tm, CHUNK))
            dw = lax.dot_general(dmg, vg, (((2,), (2,)), ((0,), (0,))), preferred_element_type=F32)
            dw_ref[g] += jnp.where(keep, jnp.sum(dw, axis=0), 0.0)
            per_token = jnp.sum(dmix[:, sl], axis=1)
            dbs_rows.append(jnp.sum(per_token.reshape(n_chunk, CHUNK), axis=0, keepdims=True))
        dbs_ref[...] += jnp.concatenate(dbs_rows, axis=0)
        dvn = jnp.concatenate(dvn_cols, axis=1)
        dg_ref[...] += _colsum(dvn * vhat)
        db_ref[...] += _colsum(dvn)
        dvhat = dvn * lng
        dgv_in = rstd * (dvhat - jnp.mean(dvhat, axis=-1, keepdims=True)
                         - vhat * jnp.mean(dvhat * vhat, axis=-1, keepdims=True))
        dp_ref[:, D:2 * D] = (dgv_in * dgv).astype(BF16)

        @pl.when(pl.program_id(0) == n_steps - 1)
        def _():
            for acc, out in ((acc_pa, gpa_ref), (acc_pb, gpb_ref), (acc_out, gout_ref)):
                pltpu.sync_copy(acc, out)

    tile = pl.BlockSpec((tm, D), lambda i: (i, 0))
    vec = pl.BlockSpec((1, D), lambda i: (0, 0))
    w = _resident((D, D))
    wsp = pl.BlockSpec((N_GROUP, CHUNK, CHUNK), lambda i: (0, 0, 0))
    acc = pltpu.VMEM((D, D), F32)
    return _call(
        body, name="merge_sgu_bwd", grid=(n_steps,),
        in_specs=[tile, pl.BlockSpec((tm, 2 * D), lambda i: (i, 1)), pl.BlockSpec((tm, 2 * D), lambda i: (i, 2)),
                  tile, tile, tile, tile, tile, vec, vec, pl.BlockSpec(sgu_ws.shape, lambda i: (0, 0, 0, 0)),
                  pl.BlockSpec(sgu_bs.shape, lambda i: (0, 0, 0)), w, w, w],
        out_specs=[tile, pl.BlockSpec((tm, 4 * D), lambda i: (i, 0)), wsp,
                   pl.BlockSpec((N_GROUP, CHUNK), lambda i: (0, 0)), vec, vec, _ANY, _ANY, _ANY],
        out_shape=[S((T, D), BF16), S((T, IN_COLS), BF16), S((N_GROUP, CHUNK, CHUNK), F32), S((N_GROUP, CHUNK), F32),
                   S((1, D), F32), S((1, D), F32), S((D, D), F32), S((D, D), F32), S((D, D), F32)],
        scratch_shapes=[_SGU_W_BF16, _SGU_W_BF16, _SGU_BT, acc, acc, acc],
        args=(dx1b, proj, proj, pa, pb, ya, yb, mb, ln_g, ln_b, sgu_ws, sgu_bs, wpa, wpb, wout), sides=sides)


def _rglru_bwd(dya, dproj, proj, hseq, xc, gates, conv_w, rg_wa, rg_wx, lam, sides=()):
    T = dya.shape[0]
    tm = min(TM_SCAN, T)
    n = T // tm
    per8 = tm // SUBLANES

    def body(dya_ref, _, rx_ref, rxp_ref, gate_ref, h_ref, hp_ref, xc_ref, gates_ref, cw_ref, wah_ref, wxh_ref,
             lam_ref, dab_ref, dcw_ref, dcb_ref, dwah_ref, dwxh_ref, dba_ref, dbx_ref, dlam_ref,
             hext, rext, dext, carry_a, carry_dh, a_s, b_s, h_s, wa_ref, wx_ref, dwa_ref, dwx_ref):
        i = pl.program_id(0)
        first_tile = i == n - 1

        @pl.when(i == 0)
        def _():
            for ref in (dcw_ref, dcb_ref, dwa_ref, dwx_ref, dba_ref, dbx_ref, dlam_ref, carry_a, carry_dh):
                ref[...] = jnp.zeros_like(ref)
            dext[tm:tm + SUBLANES, :] = jnp.zeros((SUBLANES, D), F32)
            _fill_blockdiag(wah_ref, wa_ref)
            _fill_blockdiag(wxh_ref, wx_ref)

        gel, dgel = _gelu_and_grad(gate_ref[...].astype(F32))
        dya_v = dya_ref[...].astype(F32)
        hseq_v = h_ref[...]
        dgate = dya_v * hseq_v * dgel
        xcv = xc_ref[...]
        lam_v = lam_ref[...]
        sp = _softplus_neg(lam_v)
        xb = xcv.astype(BF16)
        r, gi, a, m = (gates_ref[:, D * k:D * (k + 1)] for k in range(N_GATES))

        row = lax.broadcasted_iota(jnp.int32, (tm, D), 0)
        c = jnp.where(row == tm - 1, carry_a[0:1, :], _shift_up(a, 1, 0.0))
        dH = _scan_tile(c, dya_v * gel, carry_dh[0:1, :], a_s, b_s, h_s, reverse=True)
        carry_a[...] = jnp.broadcast_to(a[0:1, :], (SUBLANES, D))
        carry_dh[...] = jnp.broadcast_to(dH[0:1, :], (SUBLANES, D))

        hext[0:SUBLANES, :] = jnp.where(first_tile, 0.0, hp_ref[...])
        hext[SUBLANES:SUBLANES + tm, :] = hseq_v
        h_prev = hext[pl.ds(SUBLANES - 1, tm), :]

        d_m = dH * (gi * xcv)
        d_la = dH * h_prev * a - d_m * (a * a) / m
        d_ia = dH * m * xcv * (gi * (1.0 - gi))
        d_ra = d_la * ((-RG_C) * sp) * (r * (1.0 - r))
        dlam_ref[...] += _colsum(d_la * ((-RG_C) * r)) * (-_sig(-lam_v))
        dba_ref[...] += _colsum(d_ra)
        dbx_ref[...] += _colsum(d_ia)
        drab = d_ra.astype(BF16)
        diab = d_ia.astype(BF16)
        dxc_cols = []
        for j in range(N_RG_BLOCK):
            sl = slice(RG_BLOCK * j, RG_BLOCK * (j + 1))
            dxc_cols.append(_dot_nt(drab[:, sl], wa_ref[j]) + _dot_nt(diab[:, sl], wx_ref[j]))
            dwa_ref[j] += _dot_tn(xb[:, sl], drab[:, sl])
            dwx_ref[j] += _dot_tn(xb[:, sl], diab[:, sl])
        dxc = dH * m * gi + jnp.concatenate(dxc_cols, axis=1)

        dcb_ref[...] += _colsum(dxc)
        dext[0:tm, :] = dxc
        rext[0:SUBLANES, :] = jnp.where(first_tile, 0.0, rxp_ref[SUBLANES:2 * SUBLANES, :].astype(F32))
        rext[SUBLANES:SUBLANES + tm, :] = rx_ref[...].astype(F32)
        drx = jnp.zeros((tm, D), F32)
        for k in range(CONV_WIDTH):
            drx = drx + dext[pl.ds(CONV_WIDTH - 1 - k, tm), :] * cw_ref[k:k + 1, :]
            dcw_ref[k:k + 1, :] += _colsum(dxc * rext[pl.ds(SUBLANES - (CONV_WIDTH - 1) + k, tm), :])
        dext[tm:tm + SUBLANES, :] = dext[0:SUBLANES, :]
        dab_ref[:, 0:D] = drx.astype(BF16)
        dab_ref[:, D:2 * D] = dgate.astype(BF16)

        @pl.when(first_tile)
        def _():
            for j in range(N_RG_BLOCK):
                for h in range(HEADS_PER_BLOCK):
                    sl = slice(HEAD_DIM * h, HEAD_DIM * (h + 1))
                    pair, side = divmod(HEADS_PER_BLOCK * j + h, 2)
                    lanes = slice(HEAD_DIM * side, HEAD_DIM * (side + 1))
                    dwah_ref[pair, :, lanes] = dwa_ref[j, sl, sl]
                    dwxh_ref[pair, :, lanes] = dwx_ref[j, sl, sl]

    def rev(col):
        return lambda i: (n - 1 - i, col)

    def prev8(col):
        return lambda i: (jnp.maximum((n - 1 - i) * per8 - 1, 0), col)

    def prev16(col):
        return lambda i: (jnp.maximum((n - 1 - i) * (per8 // 2) - 1, 0), col)

    tile = pl.BlockSpec((tm, D), rev(0))
    vec = pl.BlockSpec((1, D), lambda i: (0, 0))
    heads_in = pl.BlockSpec(rg_wa.shape, lambda i: (0, 0, 0, 0))
    head_pairs = (rg_wa.shape[1] // 2, HEAD_DIM, 2 * HEAD_DIM)
    heads_out = pl.BlockSpec(head_pairs, lambda i: (0, 0, 0))
    cw = pl.BlockSpec((CONV_WIDTH, D), lambda i: (0, 0))
    blocks_f32 = pltpu.VMEM((N_RG_BLOCK, RG_BLOCK, RG_BLOCK), F32)
    return _call(
        body, name="rglru_bwd", grid=(n,),
        in_specs=[tile, _ANY, pl.BlockSpec((tm, D), rev(0)), pl.BlockSpec((2 * SUBLANES, D), prev16(0)),
                  pl.BlockSpec((tm, D), rev(1)), tile, pl.BlockSpec((SUBLANES, D), prev8(0)), tile,
                  pl.BlockSpec((tm, N_GATES * D), rev(0)), cw, heads_in, heads_in, vec],
        out_specs=[pl.BlockSpec((tm, 2 * D), rev(2)), cw, vec, heads_out, heads_out, vec, vec, vec],
        out_shape=[S((T, IN_COLS), BF16), S((CONV_WIDTH, D), F32), S((1, D), F32),
                   S(head_pairs, F32), S(head_pairs, F32), S((1, D), F32), S((1, D), F32), S((1, D), F32)],
        scratch_shapes=[pltpu.VMEM((tm + SUBLANES, D), F32), pltpu.VMEM((tm + SUBLANES, D), F32),
                        pltpu.VMEM((tm + SUBLANES, D), F32), pltpu.VMEM((SUBLANES, D), F32),
                        pltpu.VMEM((SUBLANES, D), F32)] + [pltpu.VMEM((D // LANES, tm, LANES), F32)] * 3
        + [_RG_BLOCKS_BF16] * 2 + [blocks_f32] * 2,
        args=(dya, dproj, proj, proj, proj, hseq, hseq, xc, gates, conv_w, rg_wa, rg_wx, lam), sides=sides,
        aliases={1: 0})


def _inproj_dh(dproj, w_in, dh, first, count, name, sides=()):
    T = dproj.shape[0]
    tm = min(TM_MM, T)

    def body(*refs):
        dp_ref, w_ref, dh_ref = refs[-3:]
        dh = jnp.zeros((tm, D), F32)
        for p in range(N_PIECE):
            shard, part = divmod((p + DPROJ_ROTATION) % N_PIECE, IN_SHARD // PIECE)
            dh = dh + _dot_nt(dp_ref[:, PIECE * p:PIECE * (p + 1)], w_ref[shard, :, PIECE * part:PIECE * (part + 1)])
        dh_ref[...] = dh

    earlier = [] if dh is None else [dh]
    return _call(
        body, name=name, grid=(count,),
        in_specs=[_ANY] * len(earlier) + [pl.BlockSpec((tm, IN_COLS), lambda i: (first + i, 0)),
                                         _resident((N_SHARD, D, IN_SHARD))],
        out_specs=[pl.BlockSpec((tm, D), lambda i: (first + i, 0))],
        out_shape=[S((T, D), F32)],
        args=(*earlier, dproj, w_in), sides=sides, aliases={0: 0} if earlier else None)


def _inproj_norm_bwd(dh, x, g, dx1):
    T = x.shape[0]
    tm = min(TM_MM, T)

    def body(dh_ref, x_ref, g_ref, dx1_ref, dx_ref, dgm_ref):
        @pl.when(pl.program_id(0) == 0)
        def _():
            dgm_ref[...] = jnp.zeros_like(dgm_ref)

        dx, dg_rows = _rms_bwd(dh_ref[...], x_ref[...], g_ref[...])
        dgm_ref[...] += _colsum(dg_rows)
        dx_ref[...] = dx1_ref[...] + dx

    tile = pl.BlockSpec((tm, D), lambda i: (i, 0))
    vec = pl.BlockSpec((1, D), lambda i: (0, 0))
    return pl.pallas_call(
        body, name="inproj_norm_bwd", grid=(T // tm,),
        in_specs=[tile, tile, vec, tile],
        out_specs=[tile, vec],
        out_shape=[S((T, D), F32), S((1, D), F32)],
        compiler_params=_params(1),
    )(dh, x, g, dx1)


def _inproj_wgrad(h, dproj, sides=()):
    T = h.shape[0]
    tk = min(TK_WGRAD, T)
    per = IN_SHARD // PIECE

    def body(h_ref, *refs):
        pieces, o_ref = refs[:per], refs[per]

        @pl.when(pl.program_id(1) == 0)
        def _():
            o_ref[...] = jnp.zeros_like(o_ref)

        o_ref[0] += _dot_tn(h_ref[...], jnp.concatenate([p[...] for p in pieces], axis=1))

    def piece(i):
        return pl.BlockSpec((tk, PIECE), lambda j, k: (k, (per * j + i + N_PIECE - DPROJ_ROTATION) % N_PIECE))

    (out,), side_outs = _call(
        body, name="inproj_wgrad", grid=(N_SHARD, T // tk),
        in_specs=[pl.BlockSpec((tk, D), lambda j, k: (k, 0))] + [piece(i) for i in range(per)],
        out_specs=[pl.BlockSpec((1, D, IN_SHARD), lambda j, k: (j, 0, 0))],
        out_shape=[S((N_SHARD, D, IN_SHARD), F32)], args=(h,) + (dproj,) * per, sides=sides)
    return out, side_outs


def _row_tile(rows):
    for t in range(256, 0, -SUBLANES):
        if rows % t == 0:
            return t
    raise ValueError(rows)


def _add_halves(core, grads, theirs, name):
    n = len(grads)
    _, r, cols = grads[0].shape
    half = r // 2
    tr = _row_tile(half)
    nb = half // tr

    def body(core_ref, *refs):
        for g_ref, t_ref, o_ref in zip(refs[:n], refs[n:2 * n], refs[2 * n:]):
            o_ref[...] = (g_ref[...] + t_ref[...]).astype(BF16)

    blk = pl.BlockSpec((1, tr, cols), lambda s, i, core_ref: (s, i, 0))
    mine = pl.BlockSpec((1, tr, cols), lambda s, i, core_ref: (s, core_ref[0] * nb + i, 0))
    gs = pltpu.PrefetchScalarGridSpec(num_scalar_prefetch=1, grid=(N_SHARD, nb),
                                      in_specs=[mine] * n + [blk] * n, out_specs=[blk] * n)
    return pl.pallas_call(
        body, name=name, grid_spec=gs, out_shape=[S((N_SHARD, half, cols), BF16)] * n, compiler_params=_params(2),
    )(core, *grads, *theirs)


def _sum_shards(chip, owns, others, name):
    n = len(owns)
    _, half, cols = owns[0].shape
    tr = _row_tile(half)

    def body(chip_ref, *refs):
        for own_ref, oth_ref, o_ref in zip(refs[:n], refs[n:2 * n], refs[2 * n:]):
            acc = own_ref[0].astype(F32)
            for j in range(3):
                acc = acc + oth_ref[j].astype(F32)
            o_ref[...] = acc

    gs = pltpu.PrefetchScalarGridSpec(
        num_scalar_prefetch=1, grid=(half // tr,),
        in_specs=[pl.BlockSpec((1, tr, cols), lambda i, chip_ref: (chip_ref[0], i, 0))] * n
        + [pl.BlockSpec((3, tr, cols), lambda i, chip_ref: (0, i, 0))] * n,
        out_specs=[pl.BlockSpec((tr, cols), lambda i, chip_ref: (i, 0))] * n)
    return pl.pallas_call(
        body, name=name, grid_spec=gs, out_shape=[S((half, cols), F32)] * n, compiler_params=_params(1),
    )(chip, *owns, *others)


def _adamw(w, g, m, v):
    m = ADAM_B1 * m + (1.0 - ADAM_B1) * g
    v = ADAM_B2 * v + (1.0 - ADAM_B2) * (g * g)
    m_hat = m / (1.0 - ADAM_B1 ** ADAM_STEP)
    v_hat = v / (1.0 - ADAM_B2 ** ADAM_STEP)
    delta = -ADAM_LR * (m_hat / (jnp.sqrt(v_hat) + ADAM_EPS) + ADAM_WD * w)
    return delta, m, v


def _adamw_shard(core, mine, theirs, w, m, v, name):
    n = len(w)
    r, cols = w[0].shape
    half = r // 2
    tr = _row_tile(half)
    nb = half // tr

    def body(core_ref, *refs):
        groups = [refs[k * n:(k + 1) * n] for k in range(9)]
        for mine_ref, theirs_ref, w_ref, m_ref, v_ref, g_ref, d_ref, mo_ref, vo_ref in zip(*groups):
            g = jnp.where(pl.program_id(0) == core_ref[0], mine_ref[...], theirs_ref[...])
            g_ref[...] = g
            d_ref[...], mo_ref[...], vo_ref[...] = _adamw(w_ref[...], g, m_ref[...], v_ref[...])

    hblk = pl.BlockSpec((tr, cols), lambda h, i, core_ref: (i, 0))
    blk = pl.BlockSpec((tr, cols), lambda h, i, core_ref: (h * nb + i, 0))
    gs = pltpu.PrefetchScalarGridSpec(num_scalar_prefetch=1, grid=(2, nb),
                                      in_specs=[hblk] * (2 * n) + [blk] * (3 * n), out_specs=[blk] * (4 * n))
    res = pl.pallas_call(
        body, name=name, grid_spec=gs, out_shape=[S((r, cols), F32)] * (4 * n), compiler_params=_params(2),
    )(core, *mine, *theirs, *w, *m, *v)
    return [tuple(res[k * n + j] for k in range(4)) for j in range(n)]


def _adamw_whole(w, g, m, v, name):
    def body(w_ref, g_ref, m_ref, v_ref, d_ref, mo_ref, vo_ref):
        d_ref[...], mo_ref[...], vo_ref[...] = _adamw(w_ref[...], g_ref[...], m_ref[...], v_ref[...])

    return pl.pallas_call(body, name=name, out_shape=[S(w.shape, F32)] * 3)(w, g, m, v)


_VEC_ROWS = ("norm_mix_g", "conv_b", "rg_lambda", "sgu_ln_g", "sgu_ln_b", "norm_ffn_g", "norm_final_g", "rg_ba",
             "rg_bx")
_CONV_ROW = len(_VEC_ROWS)
_LOSS_ROW = _CONV_ROW + CONV_WIDTH
_VEC_PAD = -(_LOSS_ROW + 1) % SUBLANES
_HEAD_BIASES = ("rg_ba", "rg_bx")
_TENSORS = ("sgu_bs", "sgu_ws", "rg_wa", "rg_wx")
_HEAD_PAIRS = ("rg_wa", "rg_wx")


def _small_sum_adamw(parts, w, m, v):
    names = [n for n in _VEC_ROWS] + list(_TENSORS)
    n_parts = len(parts)

    def total(ref):
        acc = ref[0]
        for k in range(1, N_DEVICE):
            acc = acc + ref[k]
        return acc

    def body(*refs):
        part_refs, refs = refs[:n_parts], refs[n_parts:]
        w_refs, m_refs, v_refs = (dict(zip(names, refs[k * len(names):(k + 1) * len(names)])) for k in range(3))
        outs = refs[3 * len(names):]
        out_refs = {n: outs[4 * k:4 * k + 4] for k, n in enumerate(names)}
        conv_ref, loss_ref = outs[4 * len(names):]
        vec = total(part_refs[0])
        grads = {n: total(p) for n, p in zip(_TENSORS, part_refs[1:])}
        for n in _HEAD_PAIRS:
            pairs = grads[n]
            grads[n] = jnp.stack([pairs[k // 2, :, HEAD_DIM * (k % 2):HEAD_DIM * (k % 2 + 1)]
                                  for k in range(2 * pairs.shape[0])], axis=0)
        grads = {n: g[None] for n, g in grads.items()}
        for row, n in enumerate(_VEC_ROWS):
            g = vec[row:row + 1, :]
            if n in _HEAD_BIASES:
                g = jnp.concatenate([g[:, HEAD_DIM * h:HEAD_DIM * (h + 1)] for h in range(D // HEAD_DIM)], axis=0)[None]
            grads[n] = g
        for n in names:
            g_ref, d_ref, mo_ref, vo_ref = out_refs[n]
            g_ref[...] = grads[n]
            d_ref[...], mo_ref[...], vo_ref[...] = _adamw(w_refs[n][...], grads[n], m_refs[n][...], v_refs[n][...])
        conv_ref[...] = vec[_CONV_ROW:_CONV_ROW + CONV_WIDTH, :]
        loss_ref[...] = vec[_LOSS_ROW:_LOSS_ROW + 1, 0:1]

    res = pl.pallas_call(
        body, name="small_sum_adamw",
        out_shape=[S(w[n].shape, F32) for n in names for _ in range(4)] + [S((CONV_WIDTH, D), F32), S((1, 1), F32)],
        compiler_params=pltpu.CompilerParams(vmem_limit_bytes=VMEM_LIMIT),
    )(*parts, *[w[n] for n in names], *[m[n] for n in names], *[v[n] for n in names])
    return {n: tuple(res[4 * k:4 * k + 4]) for k, n in enumerate(names)}, res[-2], res[-1]


_BIG = ("w_in", "w_proj_a", "w_proj_b", "w_out", "w_gate_up", "w_down")
_WEIGHTS = ("norm_mix_g", "w_in", "conv_w", "conv_b", "rg_wa", "rg_ba", "rg_wx", "rg_bx", "rg_lambda", "sgu_ln_g",
            "sgu_ln_b", "sgu_ws", "sgu_bs", "w_proj_a", "w_proj_b", "w_out", "norm_ffn_g", "w_gate_up", "w_down",
            "norm_final_g")


def kernel(x, norm_mix_g, w_in, conv_w, conv_b, rg_wa, rg_ba, rg_wx, rg_bx, rg_lambda, sgu_ln_g, sgu_ln_b, sgu_ws, sgu_bs, w_proj_a, w_proj_b, w_out, norm_ffn_g, w_gate_up, w_down, norm_final_g, loss_target, m_norm_mix_g, m_w_in, m_conv_w, m_conv_b, m_rg_wa, m_rg_ba, m_rg_wx, m_rg_bx, m_rg_lambda, m_sgu_ln_g, m_sgu_ln_b, m_sgu_ws, m_sgu_bs, m_w_proj_a, m_w_proj_b, m_w_out, m_norm_ffn_g, m_w_gate_up, m_w_down, m_norm_final_g, v_norm_mix_g, v_w_in, v_conv_w, v_conv_b, v_rg_wa, v_rg_ba, v_rg_wx, v_rg_bx, v_rg_lambda, v_sgu_ln_g, v_sgu_ln_b, v_sgu_ws, v_sgu_bs, v_w_proj_a, v_w_proj_b, v_w_out, v_norm_ffn_g, v_w_gate_up, v_w_down, v_norm_final_g):
    args = dict(locals())
    w = {n: args[n] for n in _WEIGHTS}
    mom = {n: args["m_" + n] for n in _WEIGHTS}
    var = {n: args["v_" + n] for n in _WEIGHTS}
    xi, yi, ci = _position()
    core = ci.astype(jnp.int32).reshape(1)
    chip = (2 * xi + yi).astype(jnp.int32).reshape(1)

    bf = {n: w[n][0].astype(BF16) for n in _BIG}
    final_g = w["norm_final_g"].reshape(1, D)
    ba, bx = w["rg_ba"].reshape(1, D), w["rg_bx"].reshape(1, D)
    lam, ln_g, ln_b = w["rg_lambda"], w["sgu_ln_g"], w["sgu_ln_b"]
    x0, target = x[0], loss_target[0]

    def shard_major(g):
        return g.reshape(N_SHARD, g.shape[0] // N_SHARD, g.shape[1])

    def per_shape(names, fn, *lists):
        if len({a.shape for a in lists[0]}) == 1:
            return fn("_".join(names), *lists)
        return [r for k, n in enumerate(names) for r in fn(n, *[[a[k]] for a in lists])]

    def chip_sums(names, grads, theirs):
        return per_shape(names, lambda s, g, t: _add_halves(core, g, t, "add_halves_" + s), grads, theirs)

    def my_halves(names, sums, arrived):
        return per_shape(names, lambda s, p, a: _sum_shards(chip, p, a, "sum_shards_" + s), sums, arrived)

    (proj, h), ((w_in_a,), (conv_a,)) = _inproj_own(
        chip, x0, w["norm_mix_g"], bf["w_in"],
        sides=[_gather_half_side([bf["w_in"]], relations=(0, 1)), _gather_side([w["conv_w"][0]])])
    conv_cols = conv_a.shape[-1]
    conv_full = jnp.swapaxes(conv_a, 0, 1).reshape(CONV_WIDTH, D)
    (proj,), ((w_in_a,),) = _inproj_rest(
        chip, proj, h, w_in_a, 1, 2, "inproj_near",
        sides=[_gather_half_side([bf["w_in"]], relations=(2,), into=[w_in_a])])
    (proj,), _ = _inproj_rest(chip, proj, h, w_in_a, 3, 1, "inproj_far")
    (ya, xc, hseq, gates), ((w_pa_a, w_pb_a, w_out_a, w_gu_a, w_down_a),) = _rglru_fwd(
        proj, conv_full, w["conv_b"], w["rg_wa"], w["rg_wx"], ba, bx, lam,
        sides=[_gather_half_side([bf[n] for n in ("w_proj_a", "w_proj_b", "w_out", "w_gate_up", "w_down")])])
    wpa, wpb, wout, wdown = w_pa_a.reshape(D, D), w_pb_a.reshape(D, D), w_out_a.reshape(D, D), w_down_a.reshape(D_FF, D)
    yb, pa, pb, mb, x1 = _sgu_merge_fwd(x0, proj, ya, ln_g, ln_b, w["sgu_ws"], w["sgu_bs"], wpa, wpb, wout)
    h2, gu, act, dx2, dx2b, loss, d_final_g = _ffn_fwd_loss(x1, w["norm_ffn_g"], w_gu_a, wdown, final_g, target)

    dgu, dx1, dx1b, d_ffn_g = _ffn_bwd(dx2, dx2b, gu, wdown, w_gu_a, x1, w["norm_ffn_g"])
    ffn = ("w_gate_up", "w_down")
    g_ffn = [_matmul_tn(h2, dgu, FF_SHARD, True, "wgrad_gate_up")[0],
             shard_major(_matmul_tn(act, dx2b, D // 2, False, "wgrad_down")[0])]
    (dya, dproj, d_ws, d_bs, d_lng, d_lnb, g_pa, g_pb, g_out), (theirs_ffn,) = _merge_sgu_bwd(
        dx1b, proj, pa, pb, ya, yb, mb, ln_g, ln_b, w["sgu_ws"], w["sgu_bs"], wpa, wpb, wout,
        sides=[_halves_side(g_ffn)])
    sums_ffn = chip_sums(ffn, g_ffn, theirs_ffn)
    mix = ("w_proj_a", "w_proj_b", "w_out")
    g_mix = [shard_major(g) for g in (g_pa, g_pb, g_out)]
    (dproj, d_cw, d_cb, d_wa, d_wx, d_ba, d_bx, d_lam), (arrived_ffn, theirs_mix, sgu_parts) = _rglru_bwd(
        dya, dproj, proj, hseq, xc, gates, conv_full, w["rg_wa"], w["rg_wx"], lam,
        sides=[_scatter_side(sums_ffn), _halves_side(g_mix), _everyone_side([d_bs, d_ws])])
    mine_ffn = my_halves(ffn, sums_ffn, arrived_ffn)
    sums_mix = chip_sums(mix, g_mix, theirs_mix)
    g_in, ((wa_parts, wx_parts), other_ffn, arrived_mix) = _inproj_wgrad(
        h, dproj, sides=[_everyone_side([d_wa, d_wx]), _swap_side(mine_ffn),
                         _scatter_side(sums_mix)])
    mine_mix = my_halves(mix, sums_mix, arrived_mix)
    n_tiles = x0.shape[0] // min(TM_MM, x0.shape[0])
    n_first = max(1, n_tiles * 3 // 8)
    (dh,), (theirs_in, other_mix) = _inproj_dh(dproj, w_in_a, None, 0, n_first, "inproj_dh_a",
                                               sides=[_halves_side([g_in]), _swap_side(mine_mix)])
    sums_in = chip_sums(("w_in",), [g_in], theirs_in)
    (dh,), (arrived_in,) = _inproj_dh(dproj, w_in_a, dh, n_first, n_tiles - n_first, "inproj_dh_b",
                                      sides=[_scatter_side(sums_in)])
    mine_in = my_halves(("w_in",), sums_in, arrived_in)
    grad_x, d_mix_g = _inproj_norm_bwd(dh, x0, w["norm_mix_g"], dx1)
    rows = {"norm_mix_g": d_mix_g, "conv_b": d_cb, "rg_lambda": d_lam, "sgu_ln_g": d_lng, "sgu_ln_b": d_lnb,
            "norm_ffn_g": d_ffn_g, "norm_final_g": d_final_g, "rg_ba": d_ba, "rg_bx": d_bx}
    vec = jnp.concatenate([rows[n] for n in _VEC_ROWS]
                          + [d_cw, jnp.pad(loss, ((0, _VEC_PAD), (0, D - 1)))], axis=0)
    other_in, (vec_parts,) = _comm_only([_swap_side(mine_in), _everyone_side([vec])], "swap_w_in")
    small_parts = [vec_parts] + sgu_parts + [wa_parts, wx_parts]

    out = {}
    for names, gm, go in ((ffn, mine_ffn, other_ffn), (mix, mine_mix, other_mix), (("w_in",), mine_in, other_in)):
        results = per_shape(
            names, lambda s, *lists: _adamw_shard(core, *lists, "adamw_" + s),
            gm, go, [w[n][0] for n in names], [mom[n][0] for n in names], [var[n][0] for n in names])
        for n, res in zip(names, results):
            out[n] = tuple(a[None] for a in res)
    as_row = lambda t: {n: a.reshape(1, D) if n == "norm_final_g" else a for n, a in t.items()}
    small_out, conv_sum, loss_sum = _small_sum_adamw(small_parts, as_row(w), as_row(mom), as_row(var))
    out.update(small_out)
    out["norm_final_g"] = tuple(a.reshape(D) for a in small_out["norm_final_g"])
    conv_g = lax.dynamic_slice_in_dim(conv_sum, chip[0] * conv_cols, conv_cols, axis=1)
    d, mo, vo = _adamw_whole(w["conv_w"][0], conv_g, mom["conv_w"][0], var["conv_w"][0], "adamw_conv_w")
    out["conv_w"] = tuple(a[None] for a in (conv_g, d, mo, vo))

    return (loss_sum[0, 0], grad_x[None], *[out[n][0] for n in _WEIGHTS], *[out[n][1] for n in _WEIGHTS],
            *[out[n][2] for n in _WEIGHTS], *[out[n][3] for n in _WEIGHTS])
```

```python
import functools

import jax
import jax.numpy as jnp
from jax import lax
from jax.experimental import pallas as pl
from jax.experimental.pallas import tpu as pltpu

F32 = jnp.float32
BF16 = jnp.bfloat16
S = jax.ShapeDtypeStruct

D = 1024
N_SHARD = 4
IN_COLS = 6 * D
IN_SHARD = IN_COLS // N_SHARD
D_FF = 2816
FF_SHARD = 2 * D_FF // N_SHARD
RG_BLOCK = 256
N_RG_BLOCK = D // RG_BLOCK
CHUNK = 128
N_GROUP = 8
CONV_WIDTH = 4
RG_C = 8.0
EPS = 1e-6
ADAM_LR, ADAM_B1, ADAM_B2, ADAM_EPS, ADAM_WD, ADAM_STEP = 0.001, 0.9, 0.999, 1e-08, 0.01, 10

V7X_VMEM_BYTES = 64 * 1024 * 1024
VMEM_LIMIT = V7X_VMEM_BYTES * 3 // 4
SUBLANES = 8
MESH = pl.DeviceIdType.MESH

TM_MM = 512
TM_SCAN = 256
TM_FF = 256
TK_WGRAD = 2048


def _params(n_axes):
    return pltpu.CompilerParams(dimension_semantics=("arbitrary",) * n_axes, vmem_limit_bytes=VMEM_LIMIT)


def _resident(shape):
    nd = len(shape)
    return pl.BlockSpec(shape, lambda *_: (0,) * nd, pipeline_mode=pl.Buffered(1))


def _sig(x):
    return 1.0 / (1.0 + jnp.exp(-x))


_GELU_K2 = 2.0 * 0.7978845608028654
_GELU_C = 0.044715


def _gelu(x):
    return x * _sig(x * (_GELU_K2 + (_GELU_K2 * _GELU_C) * (x * x)))


def _gelu_and_grad(x):
    x2 = x * x
    s = _sig(x * (_GELU_K2 + (_GELU_K2 * _GELU_C) * x2))
    g = x * s
    return g, s + g * (1.0 - s) * (_GELU_K2 + (3.0 * _GELU_K2 * _GELU_C) * x2)


_EXPM1_SERIES = tuple(1.0 / f for f in (5040.0, 720.0, 120.0, 24.0, 6.0, 2.0, 1.0))


def _one_minus_exp(x):
    p = _EXPM1_SERIES[0]
    for coef in _EXPM1_SERIES[1:]:
        p = p * x + coef
    return jnp.where(x > -0.125, -x * p, 1.0 - jnp.exp(x))


def _softplus_neg(lam):
    z = -lam
    e = jnp.exp(-jnp.abs(z))
    u = 1.0 + e
    log1p = jnp.where(u == 1.0, e, jnp.log(u) * e / (u - 1.0))
    return jnp.maximum(z, 0.0) + log1p


def _rms_stats(x):
    return lax.rsqrt(jnp.mean(x * x, axis=-1, keepdims=True) + EPS)


def _rms_bwd(dy, x, g):
    rstd = _rms_stats(x)
    xhat = x * rstd
    dxhat = dy * g
    dx = rstd * (dxhat - xhat * jnp.mean(dxhat * xhat, axis=-1, keepdims=True))
    return dx, dy * xhat


def _colsum(x):
    return jnp.sum(x, axis=0, keepdims=True)


def _shift_down(x, d, fill):
    n = x.shape[0]
    if d % SUBLANES == 0:
        return jnp.concatenate([jnp.full((d, x.shape[1]), fill, x.dtype), x[:n - d]], axis=0)
    row = lax.broadcasted_iota(jnp.int32, x.shape, 0)
    return jnp.where(row < d, fill, pltpu.roll(x, d, 0))


def _shift_up(x, d, fill):
    n = x.shape[0]
    if d % SUBLANES == 0:
        return jnp.concatenate([x[d:], jnp.full((d, x.shape[1]), fill, x.dtype)], axis=0)
    row = lax.broadcasted_iota(jnp.int32, x.shape, 0)
    return jnp.where(row >= n - d, fill, pltpu.roll(x, n - d, 0))


def _scan(a, b, shift):
    d = 1
    while d < a.shape[0]:
        b = a * shift(b, d, 0.0) + b
        a = a * shift(a, d, 1.0)
        d *= 2
    return a, b


LANES = 128


def _scan_tile(a, b, outside, a_s, b_s, h_s, reverse):
    tm = a.shape[0]
    groups = tm // SUBLANES
    order = list(range(SUBLANES - 1, -1, -1) if reverse else range(SUBLANES))
    shift = _shift_up if reverse else _shift_down
    edge = groups - 1 if reverse else 0
    for j in range(D // LANES):
        a_s[j] = a[:, LANES * j:LANES * (j + 1)]
        b_s[j] = b[:, LANES * j:LANES * (j + 1)]
    for j in range(D // LANES):
        def slab(ref, k):
            return ref[j, pl.ds(k, groups, stride=SUBLANES), :]

        ga, gb = slab(a_s, order[0]), slab(b_s, order[0])
        for k in order[1:]:
            ak = slab(a_s, k)
            gb = ak * gb + slab(b_s, k)
            ga = ak * ga
        ga, gb = _scan(ga, gb, shift)
        h_out = outside[:, LANES * j:LANES * (j + 1)]
        group_end = ga * h_out + gb
        row = lax.broadcasted_iota(jnp.int32, (groups, LANES), 0)
        h = jnp.where(row == edge, h_out, shift(group_end, 1, 0.0))
        for k in order:
            h = slab(a_s, k) * h + slab(b_s, k)
            h_s[j, pl.ds(k, groups, stride=SUBLANES), :] = h
    return jnp.concatenate([h_s[j] for j in range(D // LANES)], axis=1)


def _dot(a, b):
    return jnp.dot(a, b, preferred_element_type=F32)


def _dot_nt(a, b):
    return lax.dot_general(a, b, (((1,), (1,)), ((), ())), preferred_element_type=F32)


def _dot_tn(a, b):
    return lax.dot_general(a, b, (((0,), (0,)), ((), ())), preferred_element_type=F32)


_ANY = pl.BlockSpec(memory_space=pl.ANY)


def _position():
    return lax.axis_index("x"), lax.axis_index("y"), lax.axis_index("c")


def _other_chips(x, y):
    return [(1 - x, y), (x, 1 - y), (1 - x, 1 - y)]


class _Side:
    def __init__(self, inputs, out_shapes, n_sems, make, continues=()):
        self.inputs, self.out_shapes, self.n_sems, self.make = list(inputs), list(out_shapes), n_sems, make
        self.continues = list(continues)


MID_STEP = 0.625
LATE_MID_STEP = 0.875


def _call(body, *, name, grid, in_specs, out_specs, out_shape, args, scratch_shapes=(), sides=(), aliases=None,
          scalars=None, mid=MID_STEP):
    n_in, n_out, n_scr = len(in_specs), len(out_specs), len(scratch_shapes)
    n_scalar = 0 if scalars is None else 1
    side_in = [len(s.inputs) + len(s.continues) for s in sides]
    side_out = [len(s.out_shapes) for s in sides]
    all_aliases = {k + n_scalar: v for k, v in (aliases or {}).items()}
    for idx, s in enumerate(sides):
        for k in range(len(s.continues)):
            operand = n_scalar + n_in + sum(side_in[:idx]) + len(s.inputs) + k
            all_aliases[operand] = n_out + sum(side_out[:idx]) + k

    def wrapped(*refs):
        refs = list(refs)
        take = lambda k: [refs.pop(0) for _ in range(k)]
        ins = take(n_scalar) + take(n_in)
        sins = [take(k) for k in side_in]
        outs = take(n_out)
        souts = [take(k) for k in side_out]
        scr = take(n_scr)
        sems = [take(3) for _ in sides]
        def run(phase):
            for s, si, so, sem in zip(sides, sins, souts, sems):
                for thunk in s.make(si[:len(s.inputs)], so, *sem)[phase]:
                    thunk()

        if sides:
            n_steps = functools.reduce(lambda a, b: a * b, grid)
            step = functools.reduce(lambda a, b: a + b, [
                pl.program_id(a) * functools.reduce(lambda p, q: p * q, grid[a + 1:], 1) for a in range(len(grid))])
            pl.when(step == 0)(lambda: run(0))
        body(*ins, *outs, *scr)
        if sides:
            pl.when(step == int(mid * (n_steps - 1)))(lambda: run(1))
            pl.when(step == n_steps - 1)(lambda: run(2))

    grid_spec = pltpu.PrefetchScalarGridSpec(
        num_scalar_prefetch=n_scalar, grid=grid,
        in_specs=list(in_specs) + [_ANY] * sum(side_in),
        out_specs=list(out_specs) + [_ANY] * sum(side_out),
        scratch_shapes=list(scratch_shapes) + [pltpu.SemaphoreType.DMA((s.n_sems,)) for s in sides for _ in range(3)])
    res = pl.pallas_call(
        wrapped, name=name, grid_spec=grid_spec,
        out_shape=list(out_shape) + [o for s in sides for o in s.out_shapes],
        input_output_aliases=all_aliases,
        compiler_params=_params(len(grid)),
    )(*([scalars] if n_scalar else []), *args, *[a for s in sides for a in s.inputs + s.continues])
    main, rest, per_side = list(res[:n_out]), list(res[n_out:]), []
    for k in side_out:
        per_side.append(rest[:k])
        rest = rest[k:]
    return main, per_side


def _comm_only(sides, name):
    def body():
        pass

    return _call(body, name=name, grid=(1,), in_specs=[], out_specs=[], out_shape=[], args=[], sides=sides)[1]


def _remote(src, dst, send, recv, k, device):
    return pltpu.make_async_remote_copy(src_ref=src, dst_ref=dst, send_sem=send.at[k], recv_sem=recv.at[k],
                                        device_id=device, device_id_type=MESH)


def _both_ways(copy, keys):
    return [lambda k=k: copy(k).start() for k in keys], [], [lambda k=k: copy(k).wait() for k in keys]


def _gather_side(shards):
    n = len(shards)

    def make(ins, outs, send, recv, local):
        x, y, c = _position()
        mine = 2 * x + y
        chips = _other_chips(x, y)
        pairs = [(w, j) for w in range(n) for j in range(3)]

        def own(w):
            return pltpu.make_async_copy(ins[w], outs[w].at[mine], local.at[w])

        def push(w, j):
            return _remote(ins[w], outs[w].at[mine], send, recv, 3 * w + j, (*chips[j], c))

        def arrival(w, j):
            px, py = chips[j]
            return _remote(ins[w], outs[w].at[2 * px + py], send, recv, 3 * w + j, (px, py, c))

        starts = [lambda w=w: own(w).start() for w in range(n)] + [lambda w=w, j=j: push(w, j).start() for w, j in pairs]
        waits = ([lambda w=w, j=j: arrival(w, j).wait_recv() for w, j in pairs]
                 + [lambda w=w, j=j: push(w, j).wait_send() for w, j in pairs]
                 + [lambda w=w: own(w).wait() for w in range(n)])
        return starts, [], waits

    return _Side(shards, [S((N_SHARD,) + s.shape, s.dtype) for s in shards], 3 * n, make)


def _gather_half_side(shards, relations=(0, 1, 2), into=None):
    n = len(shards)

    def make(ins, outs, send, recv, local):
        x, y, c = _position()
        mine = 2 * x + y
        chips = _other_chips(x, y)
        pairs = [(w, j) for w in range(n) for j in relations]

        def rows(w, core):
            half = ins[w].shape[0] // 2
            return pl.ds(core * half, half)

        def own(w):
            return pltpu.make_async_copy(ins[w], outs[w].at[mine], local.at[w])

        def push(w, j):
            return _remote(ins[w].at[rows(w, c), :], outs[w].at[mine, rows(w, c), :], send, recv, 3 * w + j,
                           (*chips[j], c))

        def landed(w, j, core):
            px, py = chips[j]
            return outs[w].at[2 * px + py, rows(w, core), :]

        def arrival(w, j):
            return _remote(ins[w].at[rows(w, c), :], landed(w, j, c), send, recv, 3 * w + j, (*chips[j], c))

        def passed(w, j, core):
            return _remote(landed(w, j, core), landed(w, j, core), send, recv, 3 * n + 3 * w + j, (x, y, 1 - c))

        owns = range(n) if into is None else ()
        starts = [lambda w=w: own(w).start() for w in owns] + [lambda w=w, j=j: push(w, j).start() for w, j in pairs]
        mids = [t for w, j in pairs for t in (lambda w=w, j=j: arrival(w, j).wait_recv(),
                                              lambda w=w, j=j: passed(w, j, c).start())]
        waits = ([lambda w=w, j=j: passed(w, j, 1 - c).wait_recv() for w, j in pairs]
                 + [lambda w=w, j=j: passed(w, j, c).wait_send() for w, j in pairs]
                 + [lambda w=w, j=j: push(w, j).wait_send() for w, j in pairs]
                 + [lambda w=w: own(w).wait() for w in owns])
        return starts, mids, waits

    return _Side(shards, [S((N_SHARD,) + s.shape, s.dtype) for s in shards], 6 * n, make, continues=into or ())


def _halves_side(grads):
    n = len(grads)

    def make(ins, outs, send, recv, local):
        x, y, c = _position()

        def copy(w):
            half = ins[w].shape[1] // 2
            return _remote(ins[w].at[:, pl.ds((1 - c) * half, half), :], outs[w], send, recv, w, (x, y, 1 - c))

        return _both_ways(copy, range(n))

    return _Side(grads, [S((N_SHARD, g.shape[1] // 2, g.shape[2]), F32) for g in grads], n, make)


def _scatter_side(partials):
    n = len(partials)

    def make(ins, outs, send, recv, local):
        x, y, c = _position()
        chips = _other_chips(x, y)

        def copy(k):
            w, j = divmod(k, 3)
            px, py = chips[j]
            return _remote(ins[w].at[2 * px + py], outs[w].at[j], send, recv, k, (px, py, c))

        return _both_ways(copy, range(3 * n))

    return _Side(partials, [S((3,) + p.shape[1:], p.dtype) for p in partials], 3 * n, make)


def _swap_side(halves):
    n = len(halves)

    def make(ins, outs, send, recv, local):
        x, y, c = _position()
        return _both_ways(lambda w: _remote(ins[w], outs[w], send, recv, w, (x, y, 1 - c)), range(n))

    return _Side(halves, [S(h.shape, h.dtype) for h in halves], n, make)


N_DEVICE = 8


def _everyone_side(arrays):
    n = len(arrays)
    peers = N_DEVICE - 1

    def make(ins, outs, send, recv, local):
        x, y, c = _position()
        mine = 4 * x + 2 * y + c
        pairs = [(w, k) for w in range(n) for k in range(1, N_DEVICE)]

        def peer(k):
            return (1 - x if k & 4 else x, 1 - y if k & 2 else y, 1 - c if k & 1 else c)

        def own(w):
            return pltpu.make_async_copy(ins[w], outs[w].at[mine], local.at[w])

        def push(w, k):
            return _remote(ins[w], outs[w].at[mine], send, recv, peers * w + k - 1, peer(k))

        def arrival(w, k):
            px, py, pc = peer(k)
            return _remote(ins[w], outs[w].at[4 * px + 2 * py + pc], send, recv, peers * w + k - 1, (px, py, pc))

        starts = [lambda w=w: own(w).start() for w in range(n)] + [lambda w=w, k=k: push(w, k).start() for w, k in pairs]
        waits = ([lambda w=w, k=k: arrival(w, k).wait_recv() for w, k in pairs]
                 + [lambda w=w, k=k: push(w, k).wait_send() for w, k in pairs]
                 + [lambda w=w: own(w).wait() for w in range(n)])
        return starts, [], waits

    return _Side(arrays, [S((N_DEVICE,) + a.shape, a.dtype) for a in arrays], peers * n, make)


def _inproj_own(chip, x, g, w_shard, sides=()):
    T = x.shape[0]
    tm = min(TM_MM, T)

    def body(chip_ref, x_ref, g_ref, w_ref, proj_ref, h_ref):
        xv = x_ref[...]
        h = (xv * _rms_stats(xv) * g_ref[...]).astype(BF16)
        h_ref[...] = h
        proj_ref[...] = _dot(h, w_ref[...]).astype(BF16)

    return _call(
        body, name="inproj_own", grid=(T // tm,),
        in_specs=[pl.BlockSpec((tm, D), lambda i, c: (i, 0)), pl.BlockSpec((1, D), lambda i, c: (0, 0)),
                  pl.BlockSpec((D, IN_SHARD), lambda i, c: (0, 0), pipeline_mode=pl.Buffered(1))],
        out_specs=[pl.BlockSpec((tm, IN_SHARD), lambda i, c: (i, c[0])), pl.BlockSpec((tm, D), lambda i, c: (i, 0))],
        out_shape=[S((T, IN_COLS), BF16), S((T, D), BF16)],
        args=(x, g, w_shard), sides=sides, scalars=chip, mid=LATE_MID_STEP)


def _inproj_rest(chip, proj, h, w_in, first, count, name, sides=(), mid=MID_STEP):
    T = h.shape[0]
    tm = min(TM_MM, T)

    def body(chip_ref, _, h_ref, w_ref, proj_ref):
        proj_ref[...] = _dot(h_ref[...], w_ref[0]).astype(BF16)

    def other(p, c):
        return jnp.bitwise_xor(c[0], first + p)

    return _call(
        body, name=name, grid=(count, T // tm),
        in_specs=[_ANY, pl.BlockSpec((tm, D), lambda p, i, c: (i, 0)),
                  pl.BlockSpec((1, D, IN_SHARD), lambda p, i, c: (other(p, c), 0, 0))],
        out_specs=[pl.BlockSpec((tm, IN_SHARD), lambda p, i, c: (i, other(p, c)))],
        out_shape=[S((T, IN_COLS), BF16)],
        args=(proj, h, w_in), sides=sides, scalars=chip, aliases={0: 0}, mid=mid)


def _rg_gates(xc, wa_ref, wx_ref, ba, bx, sp):
    xb = xc.astype(BF16)
    blocks = [xb[:, RG_BLOCK * j:RG_BLOCK * (j + 1)] for j in range(N_RG_BLOCK)]
    r = _sig(jnp.concatenate([_dot(blocks[j], wa_ref[j]) for j in range(N_RG_BLOCK)], axis=1) + ba)
    gi = _sig(jnp.concatenate([_dot(blocks[j], wx_ref[j]) for j in range(N_RG_BLOCK)], axis=1) + bx)
    log_a = (-RG_C) * r * sp
    a = jnp.exp(log_a)
    m = jnp.sqrt(_one_minus_exp(2.0 * log_a))
    return xb, r, gi, a, m


N_GATES = 4
HEADS_PER_BLOCK = 4
HEAD_DIM = RG_BLOCK // HEADS_PER_BLOCK
_RG_BLOCKS_BF16 = pltpu.VMEM((N_RG_BLOCK, RG_BLOCK, RG_BLOCK), BF16)


def _fill_blockdiag(heads_ref, blocks):
    blocks[...] = jnp.zeros_like(blocks)
    for j in range(N_RG_BLOCK):
        for h in range(HEADS_PER_BLOCK):
            sl = slice(HEAD_DIM * h, HEAD_DIM * (h + 1))
            blocks[j, sl, sl] = heads_ref[0, HEADS_PER_BLOCK * j + h].astype(BF16)


def _rglru_fwd(proj, conv_w, conv_b, rg_wa, rg_wx, ba, bx, lam, sides=()):
    T = proj.shape[0]
    tm = min(TM_SCAN, T)

    def body(rx_ref, gate_ref, cw_ref, cb_ref, wah_ref, wxh_ref, ba_ref, bx_ref, lam_ref,
             ya_ref, xc_ref, h_ref, gates_ref, ext, hc, a_s, b_s, h_s, wa_ref, wx_ref):
        @pl.when(pl.program_id(0) == 0)
        def _():
            ext[0:SUBLANES, :] = jnp.zeros((SUBLANES, D), F32)
            hc[...] = jnp.zeros((SUBLANES, D), F32)
            _fill_blockdiag(wah_ref, wa_ref)
            _fill_blockdiag(wxh_ref, wx_ref)

        ext[SUBLANES:SUBLANES + tm, :] = rx_ref[...].astype(F32)
        xc = cb_ref[...]
        for k in range(CONV_WIDTH):
            xc = xc + ext[pl.ds(SUBLANES - (CONV_WIDTH - 1) + k, tm), :] * cw_ref[k:k + 1, :]
        ext[0:SUBLANES, :] = ext[tm:tm + SUBLANES, :]
        xc_ref[...] = xc
        _, r, gi, a, m = _rg_gates(xc, wa_ref, wx_ref, ba_ref[...], bx_ref[...], _softplus_neg(lam_ref[...]))
        for k, val in enumerate((r, gi, a, m)):
            gates_ref[:, D * k:D * (k + 1)] = val
        h = _scan_tile(a, m * (gi * xc), hc[0:1, :], a_s, b_s, h_s, reverse=False)
        hc[...] = jnp.broadcast_to(h[tm - 1:tm, :], (SUBLANES, D))
        h_ref[...] = h
        ya_ref[...] = (_gelu(gate_ref[...].astype(F32)) * h).astype(BF16)

    vec = pl.BlockSpec((1, D), lambda i: (0, 0))
    heads = pl.BlockSpec(rg_wa.shape, lambda i: (0, 0, 0, 0))
    tile = pl.BlockSpec((tm, D), lambda i: (i, 0))
    return _call(
        body, name="rglru_fwd", grid=(T // tm,),
        in_specs=[pl.BlockSpec((tm, D), lambda i: (i, 0)), pl.BlockSpec((tm, D), lambda i: (i, 1)),
                  pl.BlockSpec((CONV_WIDTH, D), lambda i: (0, 0)), vec, heads, heads, vec, vec, vec],
        out_specs=[tile, tile, tile, pl.BlockSpec((tm, N_GATES * D), lambda i: (i, 0))],
        out_shape=[S((T, D), BF16), S((T, D), F32), S((T, D), F32), S((T, N_GATES * D), F32)],
        scratch_shapes=[pltpu.VMEM((tm + SUBLANES, D), F32), pltpu.VMEM((SUBLANES, D), F32)]
        + [pltpu.VMEM((D // LANES, tm, LANES), F32)] * 3 + [_RG_BLOCKS_BF16] * 2,
        args=(proj, proj, conv_w, conv_b, rg_wa, rg_wx, ba, bx, lam), sides=sides, mid=LATE_MID_STEP)


def _layer_norm_stats(v):
    mu = jnp.mean(v, axis=-1, keepdims=True)
    vc = v - mu
    rstd = lax.rsqrt(jnp.mean(vc * vc, axis=-1, keepdims=True) + EPS)
    return vc * rstd, rstd


def _sgu_mix(w_ref, vnb, bst_ref, n_chunk):
    cols = []
    for g in range(N_GROUP):
        vg = vnb[:, CHUNK * g:CHUNK * (g + 1)].reshape(n_chunk, CHUNK, CHUNK)
        wb = jnp.broadcast_to(w_ref[g][None], (n_chunk, CHUNK, CHUNK))
        mg = lax.dot_general(wb, vg, (((2,), (1,)), ((0,), (0,))), preferred_element_type=F32)
        mg = mg + bst_ref[:, g:g + 1][None]
        cols.append(mg.reshape(n_chunk * CHUNK, CHUNK))
    return jnp.concatenate(cols, axis=1)


def _causal_mask():
    return (lax.broadcasted_iota(jnp.int32, (CHUNK, CHUNK), 0) >= lax.broadcasted_iota(jnp.int32, (CHUNK, CHUNK), 1))


def _fill_sgu_weights(ws_ref, bs_ref, w_tril, bs_t, w_tril_t=None):
    keep = _causal_mask()
    for g in range(N_GROUP):
        wg = jnp.where(keep, ws_ref[0, g], 0.0)
        w_tril[g] = wg.astype(BF16)
        if w_tril_t is not None:
            w_tril_t[g] = wg.T.astype(BF16)
    bs_t[...] = bs_ref[0].T


_SGU_W_BF16 = pltpu.VMEM((N_GROUP, CHUNK, CHUNK), BF16)
_SGU_BT = pltpu.VMEM((CHUNK, N_GROUP), F32)


def _sgu_merge_fwd(x, proj, ya, ln_g, ln_b, sgu_ws, sgu_bs, wpa, wpb, wout):
    T = x.shape[0]
    tm = min(TM_FF, T)
    n_chunk = tm // CHUNK

    def body(x_ref, uv_ref, gab_ref, ya_ref, g_ref, b_ref, ws_ref, bs_ref, wpa_ref, wpb_ref, wout_ref,
             yb_ref, pa_ref, pb_ref, mb_ref, x1_ref, w_ref, bst_ref):
        @pl.when(pl.program_id(0) == 0)
        def _():
            _fill_sgu_weights(ws_ref, bs_ref, w_ref, bst_ref)

        vhat, _ = _layer_norm_stats(_gelu(uv_ref[:, D:2 * D].astype(F32)))
        vnb = (vhat * g_ref[...] + b_ref[...]).astype(BF16)
        yb = (_gelu(uv_ref[:, 0:D].astype(F32)) * _sgu_mix(w_ref, vnb, bst_ref, n_chunk)).astype(BF16)
        yb_ref[...] = yb
        pa = _dot(ya_ref[...], wpa_ref[...])
        pb = _dot(yb, wpb_ref[...])
        pa_ref[...] = pa.astype(BF16)
        pb_ref[...] = pb.astype(BF16)
        mb = (_sig(gab_ref[:, 0:D].astype(F32)) * pa + _sig(gab_ref[:, D:2 * D].astype(F32)) * pb).astype(BF16)
        mb_ref[...] = mb
        x1_ref[...] = x_ref[...] + _dot(mb, wout_ref[...])

    tile = pl.BlockSpec((tm, D), lambda i: (i, 0))
    vec = pl.BlockSpec((1, D), lambda i: (0, 0))
    w = _resident((D, D))
    return pl.pallas_call(
        body, name="sgu_merge_fwd", grid=(T // tm,),
        in_specs=[tile, pl.BlockSpec((tm, 2 * D), lambda i: (i, 1)), pl.BlockSpec((tm, 2 * D), lambda i: (i, 2)), tile,
                  vec, vec, pl.BlockSpec(sgu_ws.shape, lambda i: (0, 0, 0, 0)),
                  pl.BlockSpec(sgu_bs.shape, lambda i: (0, 0, 0)), w, w, w],
        out_specs=[tile, tile, tile, tile, tile],
        out_shape=[S((T, D), BF16), S((T, D), BF16), S((T, D), BF16), S((T, D), BF16), S((T, D), F32)],
        scratch_shapes=[_SGU_W_BF16, _SGU_BT],
        compiler_params=_params(1),
    )(x, proj, proj, ya, ln_g, ln_b, sgu_ws, sgu_bs, wpa, wpb, wout)


def _ffn_fwd_loss(x1, g, w_gu, w_down, g_final, target):
    T = x1.shape[0]
    tm = min(TM_FF, T)

    def body(x_ref, g_ref, wgu_ref, wd_ref, gf_ref, t_ref,
             h2_ref, gu_ref, act_ref, dx2_ref, dx2b_ref, loss_ref, dg_ref):
        @pl.when(pl.program_id(0) == 0)
        def _():
            loss_ref[...] = jnp.zeros_like(loss_ref)
            dg_ref[...] = jnp.zeros_like(dg_ref)

        xv = x_ref[...]
        h2 = (xv * _rms_stats(xv) * g_ref[...]).astype(BF16)
        h2_ref[...] = h2
        x2 = xv
        for k in range(N_SHARD // 2):
            cols = slice(FF_SHARD * k, FF_SHARD * (k + 1))
            gate = _dot(h2, wgu_ref[k])
            up = _dot(h2, wgu_ref[k + N_SHARD // 2])
            gu_ref[:, cols] = gate.astype(BF16)
            gu_ref[:, D_FF + FF_SHARD * k:D_FF + FF_SHARD * (k + 1)] = up.astype(BF16)
            act = (gate * _sig(gate) * up).astype(BF16)
            act_ref[:, cols] = act
            x2 = x2 + _dot(act, wd_ref[cols, :])
        gf = gf_ref[...]
        err = x2 * _rms_stats(x2) * gf - t_ref[...]
        loss_ref[...] += 0.5 * jnp.sum(jnp.mean(err * err, axis=-1, keepdims=True), axis=0, keepdims=True)
        dx2, dg_rows = _rms_bwd(err * (1.0 / D), x2, gf)
        dg_ref[...] += _colsum(dg_rows)
        dx2_ref[...] = dx2
        dx2b_ref[...] = dx2.astype(BF16)

    tile = pl.BlockSpec((tm, D), lambda i: (i, 0))
    vec = pl.BlockSpec((1, D), lambda i: (0, 0))
    return pl.pallas_call(
        body, name="ffn_fwd_loss", grid=(T // tm,),
        in_specs=[tile, vec, _resident((N_SHARD, D, FF_SHARD)), _resident((D_FF, D)), vec, tile],
        out_specs=[tile, pl.BlockSpec((tm, 2 * D_FF), lambda i: (i, 0)), pl.BlockSpec((tm, D_FF), lambda i: (i, 0)),
                   tile, tile, pl.BlockSpec((1, 1), lambda i: (0, 0)), vec],
        out_shape=[S((T, D), BF16), S((T, 2 * D_FF), BF16), S((T, D_FF), BF16), S((T, D), F32), S((T, D), BF16),
                   S((1, 1), F32), S((1, D), F32)],
        compiler_params=_params(1),
    )(x1, g, w_gu, w_down, g_final, target)


def _ffn_bwd(dx2, dx2b, gu, w_down, w_gu, x1, g):
    T = x1.shape[0]
    tm = min(TM_FF, T)

    def body(dx2_ref, dx2b_ref, gu_ref, wd_ref, wgu_ref, x_ref, g_ref, dgu_ref, dx1_ref, dx1b_ref, dg_ref):
        @pl.when(pl.program_id(0) == 0)
        def _():
            dg_ref[...] = jnp.zeros_like(dg_ref)

        dxb = dx2b_ref[...]
        dh2 = jnp.zeros((tm, D), F32)
        for k in range(N_SHARD // 2):
            cols = slice(FF_SHARD * k, FF_SHARD * (k + 1))
            up_cols = slice(D_FF + FF_SHARD * k, D_FF + FF_SHARD * (k + 1))
            dact = _dot_nt(dxb, wd_ref[cols, :])
            gate = gu_ref[:, cols].astype(F32)
            sg = _sig(gate)
            dgate = (dact * gu_ref[:, up_cols].astype(F32) * (sg * (1.0 + gate * (1.0 - sg)))).astype(BF16)
            dup = (dact * (gate * sg)).astype(BF16)
            dgu_ref[:, cols] = dgate
            dgu_ref[:, up_cols] = dup
            dh2 = dh2 + _dot_nt(dgate, wgu_ref[k]) + _dot_nt(dup, wgu_ref[k + N_SHARD // 2])
        dx, dg_rows = _rms_bwd(dh2, x_ref[...], g_ref[...])
        dg_ref[...] += _colsum(dg_rows)
        dx1 = dx2_ref[...] + dx
        dx1_ref[...] = dx1
        dx1b_ref[...] = dx1.astype(BF16)

    tile = pl.BlockSpec((tm, D), lambda i: (i, 0))
    wide = pl.BlockSpec((tm, 2 * D_FF), lambda i: (i, 0))
    vec = pl.BlockSpec((1, D), lambda i: (0, 0))
    return pl.pallas_call(
        body, name="ffn_bwd", grid=(T // tm,),
        in_specs=[tile, tile, wide, _resident((D_FF, D)), _resident((N_SHARD, D, FF_SHARD)), tile, vec],
        out_specs=[wide, tile, tile, vec],
        out_shape=[S((T, 2 * D_FF), BF16), S((T, D), F32), S((T, D), BF16), S((1, D), F32)],
        compiler_params=_params(1),
    )(dx2, dx2b, gu, w_down, w_gu, x1, g)


def _matmul_tn(a, b, tn, shard_major, name, sides=()):
    T, M = a.shape
    N = b.shape[1]
    tk = min(TK_WGRAD, T)

    def body(a_ref, b_ref, o_ref):
        @pl.when(pl.program_id(1) == 0)
        def _():
            o_ref[...] = jnp.zeros_like(o_ref)

        acc = _dot_tn(a_ref[...], b_ref[...])
        if shard_major:
            o_ref[0] += acc
        else:
            o_ref[...] += acc

    if shard_major:
        out_spec, out_shape = pl.BlockSpec((1, M, tn), lambda j, k: (j, 0, 0)), S((N // tn, M, tn), F32)
    else:
        out_spec, out_shape = pl.BlockSpec((M, tn), lambda j, k: (0, j)), S((M, N), F32)
    (out,), side_outs = _call(
        body, name=name, grid=(N // tn, T // tk),
        in_specs=[pl.BlockSpec((tk, M), lambda j, k: (k, 0)), pl.BlockSpec((tk, tn), lambda j, k: (k, j))],
        out_specs=[out_spec], out_shape=[out_shape], args=(a, b), sides=sides)
    return out, side_outs


PIECE = IN_SHARD // 3
N_PIECE = IN_COLS // PIECE
DPROJ_ROTATION = 2 * D // PIECE


def _merge_sgu_bwd(dx1b, proj, pa, pb, ya, yb, mb, ln_g, ln_b, sgu_ws, sgu_bs, wpa, wpb, wout, sides=()):
    T = dx1b.shape[0]
    tm = min(TM_FF, T)
    n_chunk = tm // CHUNK
    n_steps = T // tm

    def body(dx_ref, uv_ref, gab_ref, pa_ref, pb_ref, ya_ref, yb_ref, mb_ref, g_ref, b_ref, ws_ref, bs_ref,
             wpa_ref, wpb_ref, wout_ref,
             dya_ref, dp_ref, dw_ref, dbs_ref, dg_ref, db_ref, gpa_ref, gpb_ref, gout_ref,
             w_ref, wt_ref, bst_ref, acc_pa, acc_pb, acc_out):
        @pl.when(pl.program_id(0) == 0)
        def _():
            for ref in (dw_ref, dbs_ref, dg_ref, db_ref, acc_pa, acc_pb, acc_out):
                ref[...] = jnp.zeros_like(ref)
            _fill_sgu_weights(ws_ref, bs_ref, w_ref, bst_ref, wt_ref)

        dxb = dx_ref[...]
        dm = _dot_nt(dxb, wout_ref[...])
        sa = _sig(gab_ref[:, 0:D].astype(F32))
        sb = _sig(gab_ref[:, D:2 * D].astype(F32))
        dpa = (dm * sa).astype(BF16)
        dpb = (dm * sb).astype(BF16)
        acc_pa[...] += _dot_tn(ya_ref[...], dpa)
        acc_pb[...] += _dot_tn(yb_ref[...], dpb)
        acc_out[...] += _dot_tn(mb_ref[...], dxb)
        dp_ref[:, 2 * D:3 * D] = (dm * pa_ref[...].astype(F32) * (sa * (1.0 - sa))).astype(BF16)
        dp_ref[:, 3 * D:4 * D] = (dm * pb_ref[...].astype(F32) * (sb * (1.0 - sb))).astype(BF16)
        dya_ref[...] = _dot_nt(dpa, wpa_ref[...]).astype(BF16)
        dyb_v = _dot_nt(dpb, wpb_ref[...])

        gu, dgu = _gelu_and_grad(uv_ref[:, 0:D].astype(F32))
        gv, dgv = _gelu_and_grad(uv_ref[:, D:2 * D].astype(F32))
        vhat, rstd = _layer_norm_stats(gv)
        lng = g_ref[...]
        vnb = (vhat * lng + b_ref[...]).astype(BF16)
        mixed = _sgu_mix(w_ref, vnb, bst_ref, n_chunk)
        dp_ref[:, 0:D] = (dyb_v * mixed * dgu).astype(BF16)
        dmix = dyb_v * gu
        dmb = dmix.astype(BF16)
        keep = _causal_mask()
        dvn_cols, dbs_rows = [], []
        for g in range(N_GROUP):
            sl = slice(CHUNK * g, CHUNK * (g + 1))
            dmg = dmb[:, sl].reshape(n_chunk, CHUNK, CHUNK)
            vg = vnb[:, sl].reshape(n_chunk, CHUNK, CHUNK)
            wtb = jnp.broadcast_to(wt_ref[g][None], (n_chunk, CHUNK, CHUNK))
            dvn = lax.dot_general(wtb, dmg, (((2,), (1,)), ((0,), (0,))), preferred_element_type=F32)
            dvn_cols.append(dvn.reshape(tm, CHUNK))
            dw = lax.dot_general(dmg, vg, (((2,), (2,)), ((0,), (0,))), preferred_element_type=F32)
            dw_ref[g] += jnp.where(keep, jnp.sum(dw, axis=0), 0.0)
            per_token = jnp.sum(dmix[:, sl], axis=1)
            dbs_rows.append(jnp.sum(per_token.reshape(n_chunk, CHUNK), axis=0, keepdims=True))
        dbs_ref[...] += jnp.concatenate(dbs_rows, axis=0)
        dvn = jnp.concatenate(dvn_cols, axis=1)
        dg_ref[...] += _colsum(dvn * vhat)
        db_ref[...] += _colsum(dvn)
        dvhat = dvn * lng
        dgv_in = rstd * (dvhat - jnp.mean(dvhat, axis=-1, keepdims=True)
                         - vhat * jnp.mean(dvhat * vhat, axis=-1, keepdims=True))
        dp_ref[:, D:2 * D] = (dgv_in * dgv).astype(BF16)

        @pl.when(pl.program_id(0) == n_steps - 1)
        def _():
            for acc, out in ((acc_pa, gpa_ref), (acc_pb, gpb_ref), (acc_out, gout_ref)):
                pltpu.sync_copy(acc, out)

    tile = pl.BlockSpec((tm, D), lambda i: (i, 0))
    vec = pl.BlockSpec((1, D), lambda i: (0, 0))
    w = _resident((D, D))
    wsp = pl.BlockSpec((N_GROUP, CHUNK, CHUNK), lambda i: (0, 0, 0))
    acc = pltpu.VMEM((D, D), F32)
    return _call(
        body, name="merge_sgu_bwd", grid=(n_steps,),
        in_specs=[tile, pl.BlockSpec((tm, 2 * D), lambda i: (i, 1)), pl.BlockSpec((tm, 2 * D), lambda i: (i, 2)),
                  tile, tile, tile, tile, tile, vec, vec, pl.BlockSpec(sgu_ws.shape, lambda i: (0, 0, 0, 0)),
                  pl.BlockSpec(sgu_bs.shape, lambda i: (0, 0, 0)), w, w, w],
        out_specs=[tile, pl.BlockSpec((tm, 4 * D), lambda i: (i, 0)), wsp,
                   pl.BlockSpec((N_GROUP, CHUNK), lambda i: (0, 0)), vec, vec, _ANY, _ANY, _ANY],
        out_shape=[S((T, D), BF16), S((T, IN_COLS), BF16), S((N_GROUP, CHUNK, CHUNK), F32), S((N_GROUP, CHUNK), F32),
                   S((1, D), F32), S((1, D), F32), S((D, D), F32), S((D, D), F32), S((D, D), F32)],
        scratch_shapes=[_SGU_W_BF16, _SGU_W_BF16, _SGU_BT, acc, acc, acc],
        args=(dx1b, proj, proj, pa, pb, ya, yb, mb, ln_g, ln_b, sgu_ws, sgu_bs, wpa, wpb, wout), sides=sides)


def _rglru_bwd(dya, dproj, proj, hseq, xc, gates, conv_w, rg_wa, rg_wx, lam, sides=()):
    T = dya.shape[0]
    tm = min(TM_SCAN, T)
    n = T // tm
    per8 = tm // SUBLANES

    def body(dya_ref, _, rx_ref, rxp_ref, gate_ref, h_ref, hp_ref, xc_ref, gates_ref, cw_ref, wah_ref, wxh_ref,
             lam_ref, dab_ref, dcw_ref, dcb_ref, dwah_ref, dwxh_ref, dba_ref, dbx_ref, dlam_ref,
             hext, rext, dext, carry_a, carry_dh, a_s, b_s, h_s, wa_ref, wx_ref, dwa_ref, dwx_ref):
        i = pl.program_id(0)
        first_tile = i == n - 1

        @pl.when(i == 0)
        def _():
            for ref in (dcw_ref, dcb_ref, dwa_ref, dwx_ref, dba_ref, dbx_ref, dlam_ref, carry_a, carry_dh):
                ref[...] = jnp.zeros_like(ref)
            dext[tm:tm + SUBLANES, :] = jnp.zeros((SUBLANES, D), F32)
            _fill_blockdiag(wah_ref, wa_ref)
            _fill_blockdiag(wxh_ref, wx_ref)

        gel, dgel = _gelu_and_grad(gate_ref[...].astype(F32))
        dya_v = dya_ref[...].astype(F32)
        hseq_v = h_ref[...]
        dgate = dya_v * hseq_v * dgel
        xcv = xc_ref[...]
        lam_v = lam_ref[...]
        sp = _softplus_neg(lam_v)
        xb = xcv.astype(BF16)
        r, gi, a, m = (gates_ref[:, D * k:D * (k + 1)] for k in range(N_GATES))

        row = lax.broadcasted_iota(jnp.int32, (tm, D), 0)
        c = jnp.where(row == tm - 1, carry_a[0:1, :], _shift_up(a, 1, 0.0))
        dH = _scan_tile(c, dya_v * gel, carry_dh[0:1, :], a_s, b_s, h_s, reverse=True)
        carry_a[...] = jnp.broadcast_to(a[0:1, :], (SUBLANES, D))
        carry_dh[...] = jnp.broadcast_to(dH[0:1, :], (SUBLANES, D))

        hext[0:SUBLANES, :] = jnp.where(first_tile, 0.0, hp_ref[...])
        hext[SUBLANES:SUBLANES + tm, :] = hseq_v
        h_prev = hext[pl.ds(SUBLANES - 1, tm), :]

        d_m = dH * (gi * xcv)
        d_la = dH * h_prev * a - d_m * (a * a) / m
        d_ia = dH * m * xcv * (gi * (1.0 - gi))
        d_ra = d_la * ((-RG_C) * sp) * (r * (1.0 - r))
        dlam_ref[...] += _colsum(d_la * ((-RG_C) * r)) * (-_sig(-lam_v))
        dba_ref[...] += _colsum(d_ra)
        dbx_ref[...] += _colsum(d_ia)
        drab = d_ra.astype(BF16)
        diab = d_ia.astype(BF16)
        dxc_cols = []
        for j in range(N_RG_BLOCK):
            sl = slice(RG_BLOCK * j, RG_BLOCK * (j + 1))
            dxc_cols.append(_dot_nt(drab[:, sl], wa_ref[j]) + _dot_nt(diab[:, sl], wx_ref[j]))
            dwa_ref[j] += _dot_tn(xb[:, sl], drab[:, sl])
            dwx_ref[j] += _dot_tn(xb[:, sl], diab[:, sl])
        dxc = dH * m * gi + jnp.concatenate(dxc_cols, axis=1)

        dcb_ref[...] += _colsum(dxc)
        dext[0:tm, :] = dxc
        rext[0:SUBLANES, :] = jnp.where(first_tile, 0.0, rxp_ref[SUBLANES:2 * SUBLANES, :].astype(F32))
        rext[SUBLANES:SUBLANES + tm, :] = rx_ref[...].astype(F32)
        drx = jnp.zeros((tm, D), F32)
        for k in range(CONV_WIDTH):
            drx = drx + dext[pl.ds(CONV_WIDTH - 1 - k, tm), :] * cw_ref[k:k + 1, :]
            dcw_ref[k:k + 1, :] += _colsum(dxc * rext[pl.ds(SUBLANES - (CONV_WIDTH - 1) + k, tm), :])
        dext[tm:tm + SUBLANES, :] = dext[0:SUBLANES, :]
        dab_ref[:, 0:D] = drx.astype(BF16)
        dab_ref[:, D:2 * D] = dgate.astype(BF16)

        @pl.when(first_tile)
        def _():
            for j in range(N_RG_BLOCK):
                for h in range(HEADS_PER_BLOCK):
                    sl = slice(HEAD_DIM * h, HEAD_DIM * (h + 1))
                    pair, side = divmod(HEADS_PER_BLOCK * j + h, 2)
                    lanes = slice(HEAD_DIM * side, HEAD_DIM * (side + 1))
                    dwah_ref[pair, :, lanes] = dwa_ref[j, sl, sl]
                    dwxh_ref[pair, :, lanes] = dwx_ref[j, sl, sl]

    def rev(col):
        return lambda i: (n - 1 - i, col)

    def prev8(col):
        return lambda i: (jnp.maximum((n - 1 - i) * per8 - 1, 0), col)

    def prev16(col):
        return lambda i: (jnp.maximum((n - 1 - i) * (per8 // 2) - 1, 0), col)

    tile = pl.BlockSpec((tm, D), rev(0))
    vec = pl.BlockSpec((1, D), lambda i: (0, 0))
    heads_in = pl.BlockSpec(rg_wa.shape, lambda i: (0, 0, 0, 0))
    head_pairs = (rg_wa.shape[1] // 2, HEAD_DIM, 2 * HEAD_DIM)
    heads_out = pl.BlockSpec(head_pairs, lambda i: (0, 0, 0))
    cw = pl.BlockSpec((CONV_WIDTH, D), lambda i: (0, 0))
    blocks_f32 = pltpu.VMEM((N_RG_BLOCK, RG_BLOCK, RG_BLOCK), F32)
    return _call(
        body, name="rglru_bwd", grid=(n,),
        in_specs=[tile, _ANY, pl.BlockSpec((tm, D), rev(0)), pl.BlockSpec((2 * SUBLANES, D), prev16(0)),
                  pl.BlockSpec((tm, D), rev(1)), tile, pl.BlockSpec((SUBLANES, D), prev8(0)), tile,
                  pl.BlockSpec((tm, N_GATES * D), rev(0)), cw, heads_in, heads_in, vec],
        out_specs=[pl.BlockSpec((tm, 2 * D), rev(2)), cw, vec, heads_out, heads_out, vec, vec, vec],
        out_shape=[S((T, IN_COLS), BF16), S((CONV_WIDTH, D), F32), S((1, D), F32),
                   S(head_pairs, F32), S(head_pairs, F32), S((1, D), F32), S((1, D), F32), S((1, D), F32)],
        scratch_shapes=[pltpu.VMEM((tm + SUBLANES, D), F32), pltpu.VMEM((tm + SUBLANES, D), F32),
                        pltpu.VMEM((tm + SUBLANES, D), F32), pltpu.VMEM((SUBLANES, D), F32),
                        pltpu.VMEM((SUBLANES, D), F32)] + [pltpu.VMEM((D // LANES, tm, LANES), F32)] * 3
        + [_RG_BLOCKS_BF16] * 2 + [blocks_f32] * 2,
        args=(dya, dproj, proj, proj, proj, hseq, hseq, xc, gates, conv_w, rg_wa, rg_wx, lam), sides=sides,
        aliases={1: 0})


def _inproj_dh(dproj, w_in, dh, first, count, name, sides=()):
    T = dproj.shape[0]
    tm = min(TM_MM, T)

    def body(*refs):
        dp_ref, w_ref, dh_ref = refs[-3:]
        dh = jnp.zeros((tm, D), F32)
        for p in range(N_PIECE):
            shard, part = divmod((p + DPROJ_ROTATION) % N_PIECE, IN_SHARD // PIECE)
            dh = dh + _dot_nt(dp_ref[:, PIECE * p:PIECE * (p + 1)], w_ref[shard, :, PIECE * part:PIECE * (part + 1)])
        dh_ref[...] = dh

    earlier = [] if dh is None else [dh]
    return _call(
        body, name=name, grid=(count,),
        in_specs=[_ANY] * len(earlier) + [pl.BlockSpec((tm, IN_COLS), lambda i: (first + i, 0)),
                                         _resident((N_SHARD, D, IN_SHARD))],
        out_specs=[pl.BlockSpec((tm, D), lambda i: (first + i, 0))],
        out_shape=[S((T, D), F32)],
        args=(*earlier, dproj, w_in), sides=sides, aliases={0: 0} if earlier else None)


def _inproj_norm_bwd(dh, x, g, dx1):
    T = x.shape[0]
    tm = min(TM_MM, T)

    def body(dh_ref, x_ref, g_ref, dx1_ref, dx_ref, dgm_ref):
        @pl.when(pl.program_id(0) == 0)
        def _():
            dgm_ref[...] = jnp.zeros_like(dgm_ref)

        dx, dg_rows = _rms_bwd(dh_ref[...], x_ref[...], g_ref[...])
        dgm_ref[...] += _colsum(dg_rows)
        dx_ref[...] = dx1_ref[...] + dx

    tile = pl.BlockSpec((tm, D), lambda i: (i, 0))
    vec = pl.BlockSpec((1, D), lambda i: (0, 0))
    return pl.pallas_call(
        body, name="inproj_norm_bwd", grid=(T // tm,),
        in_specs=[tile, tile, vec, tile],
        out_specs=[tile, vec],
        out_shape=[S((T, D), F32), S((1, D), F32)],
        compiler_params=_params(1),
    )(dh, x, g, dx1)


def _inproj_wgrad(h, dproj, sides=()):
    T = h.shape[0]
    tk = min(TK_WGRAD, T)
    per = IN_SHARD // PIECE

    def body(h_ref, *refs):
        pieces, o_ref = refs[:per], refs[per]

        @pl.when(pl.program_id(1) == 0)
        def _():
            o_ref[...] = jnp.zeros_like(o_ref)

        o_ref[0] += _dot_tn(h_ref[...], jnp.concatenate([p[...] for p in pieces], axis=1))

    def piece(i):
        return pl.BlockSpec((tk, PIECE), lambda j, k: (k, (per * j + i + N_PIECE - DPROJ_ROTATION) % N_PIECE))

    (out,), side_outs = _call(
        body, name="inproj_wgrad", grid=(N_SHARD, T // tk),
        in_specs=[pl.BlockSpec((tk, D), lambda j, k: (k, 0))] + [piece(i) for i in range(per)],
        out_specs=[pl.BlockSpec((1, D, IN_SHARD), lambda j, k: (j, 0, 0))],
        out_shape=[S((N_SHARD, D, IN_SHARD), F32)], args=(h,) + (dproj,) * per, sides=sides)
    return out, side_outs


def _row_tile(rows):
    for t in range(256, 0, -SUBLANES):
        if rows % t == 0:
            return t
    raise ValueError(rows)


def _add_halves(core, grads, theirs, name):
    n = len(grads)
    _, r, cols = grads[0].shape
    half = r // 2
    tr = _row_tile(half)
    nb = half // tr

    def body(core_ref, *refs):
        for g_ref, t_ref, o_ref in zip(refs[:n], refs[n:2 * n], refs[2 * n:]):
            o_ref[...] = (g_ref[...] + t_ref[...]).astype(BF16)

    blk = pl.BlockSpec((1, tr, cols), lambda s, i, core_ref: (s, i, 0))
    mine = pl.BlockSpec((1, tr, cols), lambda s, i, core_ref: (s, core_ref[0] * nb + i, 0))
    gs = pltpu.PrefetchScalarGridSpec(num_scalar_prefetch=1, grid=(N_SHARD, nb),
                                      in_specs=[mine] * n + [blk] * n, out_specs=[blk] * n)
    return pl.pallas_call(
        body, name=name, grid_spec=gs, out_shape=[S((N_SHARD, half, cols), BF16)] * n, compiler_params=_params(2),
    )(core, *grads, *theirs)


def _sum_shards(chip, owns, others, name):
    n = len(owns)
    _, half, cols = owns[0].shape
    tr = _row_tile(half)

    def body(chip_ref, *refs):
        for own_ref, oth_ref, o_ref in zip(refs[:n], refs[n:2 * n], refs[2 * n:]):
            acc = own_ref[0].astype(F32)
            for j in range(3):
                acc = acc + oth_ref[j].astype(F32)
            o_ref[...] = acc

    gs = pltpu.PrefetchScalarGridSpec(
        num_scalar_prefetch=1, grid=(half // tr,),
        in_specs=[pl.BlockSpec((1, tr, cols), lambda i, chip_ref: (chip_ref[0], i, 0))] * n
        + [pl.BlockSpec((3, tr, cols), lambda i, chip_ref: (0, i, 0))] * n,
        out_specs=[pl.BlockSpec((tr, cols), lambda i, chip_ref: (i, 0))] * n)
    return pl.pallas_call(
        body, name=name, grid_spec=gs, out_shape=[S((half, cols), F32)] * n, compiler_params=_params(1),
    )(chip, *owns, *others)


def _adamw(w, g, m, v):
    m = ADAM_B1 * m + (1.0 - ADAM_B1) * g
    v = ADAM_B2 * v + (1.0 - ADAM_B2) * (g * g)
    m_hat = m / (1.0 - ADAM_B1 ** ADAM_STEP)
    v_hat = v / (1.0 - ADAM_B2 ** ADAM_STEP)
    delta = -ADAM_LR * (m_hat / (jnp.sqrt(v_hat) + ADAM_EPS) + ADAM_WD * w)
    return delta, m, v


def _adamw_shard(core, mine, theirs, w, m, v, name):
    n = len(w)
    r, cols = w[0].shape
    half = r // 2
    tr = _row_tile(half)
    nb = half // tr

    def body(core_ref, *refs):
        groups = [refs[k * n:(k + 1) * n] for k in range(9)]
        for mine_ref, theirs_ref, w_ref, m_ref, v_ref, g_ref, d_ref, mo_ref, vo_ref in zip(*groups):
            g = jnp.where(pl.program_id(0) == core_ref[0], mine_ref[...], theirs_ref[...])
            g_ref[...] = g
            d_ref[...], mo_ref[...], vo_ref[...] = _adamw(w_ref[...], g, m_ref[...], v_ref[...])

    hblk = pl.BlockSpec((tr, cols), lambda h, i, core_ref: (i, 0))
    blk = pl.BlockSpec((tr, cols), lambda h, i, core_ref: (h * nb + i, 0))
    gs = pltpu.PrefetchScalarGridSpec(num_scalar_prefetch=1, grid=(2, nb),
                                      in_specs=[hblk] * (2 * n) + [blk] * (3 * n), out_specs=[blk] * (4 * n))
    res = pl.pallas_call(
        body, name=name, grid_spec=gs, out_shape=[S((r, cols), F32)] * (4 * n), compiler_params=_params(2),
    )(core, *mine, *theirs, *w, *m, *v)
    return [tuple(res[k * n + j] for k in range(4)) for j in range(n)]


def _adamw_whole(w, g, m, v, name):
    def body(w_ref, g_ref, m_ref, v_ref, d_ref, mo_ref, vo_ref):
        d_ref[...], mo_ref[...], vo_ref[...] = _adamw(w_ref[...], g_ref[...], m_ref[...], v_ref[...])

    return pl.pallas_call(body, name=name, out_shape=[S(w.shape, F32)] * 3)(w, g, m, v)


_VEC_ROWS = ("norm_mix_g", "conv_b", "rg_lambda", "sgu_ln_g", "sgu_ln_b", "norm_ffn_g", "norm_final_g", "rg_ba",
             "rg_bx")
_CONV_ROW = len(_VEC_ROWS)
_LOSS_ROW = _CONV_ROW + CONV_WIDTH
_VEC_PAD = -(_LOSS_ROW + 1) % SUBLANES
_HEAD_BIASES = ("rg_ba", "rg_bx")
_TENSORS = ("sgu_bs", "sgu_ws", "rg_wa", "rg_wx")
_HEAD_PAIRS = ("rg_wa", "rg_wx")


def _small_sum_adamw(parts, w, m, v):
    names = [n for n in _VEC_ROWS] + list(_TENSORS)
    n_parts = len(parts)

    def total(ref):
        acc = ref[0]
        for k in range(1, N_DEVICE):
            acc = acc + ref[k]
        return acc

    def body(*refs):
        part_refs, refs = refs[:n_parts], refs[n_parts:]
        w_refs, m_refs, v_refs = (dict(zip(names, refs[k * len(names):(k + 1) * len(names)])) for k in range(3))
        outs = refs[3 * len(names):]
        out_refs = {n: outs[4 * k:4 * k + 4] for k, n in enumerate(names)}
        conv_ref, loss_ref = outs[4 * len(names):]
        vec = total(part_refs[0])
        grads = {n: total(p) for n, p in zip(_TENSORS, part_refs[1:])}
        for n in _HEAD_PAIRS:
            pairs = grads[n]
            grads[n] = jnp.stack([pairs[k // 2, :, HEAD_DIM * (k % 2):HEAD_DIM * (k % 2 + 1)]
                                  for k in range(2 * pairs.shape[0])], axis=0)
        grads = {n: g[None] for n, g in grads.items()}
        for row, n in enumerate(_VEC_ROWS):
            g = vec[row:row + 1, :]
            if n in _HEAD_BIASES:
                g = jnp.concatenate([g[:, HEAD_DIM * h:HEAD_DIM * (h + 1)] for h in range(D // HEAD_DIM)], axis=0)[None]
            grads[n] = g
        for n in names:
            g_ref, d_ref, mo_ref, vo_ref = out_refs[n]
            g_ref[...] = grads[n]
            d_ref[...], mo_ref[...], vo_ref[...] = _adamw(w_refs[n][...], grads[n], m_refs[n][...], v_refs[n][...])
        conv_ref[...] = vec[_CONV_ROW:_CONV_ROW + CONV_WIDTH, :]
        loss_ref[...] = vec[_LOSS_ROW:_LOSS_ROW + 1, 0:1]

    res = pl.pallas_call(
        body, name="small_sum_adamw",
        out_shape=[S(w[n].shape, F32) for n in names for _ in range(4)] + [S((CONV_WIDTH, D), F32), S((1, 1), F32)],
        compiler_params=pltpu.CompilerParams(vmem_limit_bytes=VMEM_LIMIT),
    )(*parts, *[w[n] for n in names], *[m[n] for n in names], *[v[n] for n in names])
    return {n: tuple(res[4 * k:4 * k + 4]) for k, n in enumerate(names)}, res[-2], res[-1]


_BIG = ("w_in", "w_proj_a", "w_proj_b", "w_out", "w_gate_up", "w_down")
_WEIGHTS = ("norm_mix_g", "w_in", "conv_w", "conv_b", "rg_wa", "rg_ba", "rg_wx", "rg_bx", "rg_lambda", "sgu_ln_g",
            "sgu_ln_b", "sgu_ws", "sgu_bs", "w_proj_a", "w_proj_b", "w_out", "norm_ffn_g", "w_gate_up", "w_down",
            "norm_final_g")


def kernel(x, norm_mix_g, w_in, conv_w, conv_b, rg_wa, rg_ba, rg_wx, rg_bx, rg_lambda, sgu_ln_g, sgu_ln_b, sgu_ws, sgu_bs, w_proj_a, w_proj_b, w_out, norm_ffn_g, w_gate_up, w_down, norm_final_g, loss_target, m_norm_mix_g, m_w_in, m_conv_w, m_conv_b, m_rg_wa, m_rg_ba, m_rg_wx, m_rg_bx, m_rg_lambda, m_sgu_ln_g, m_sgu_ln_b, m_sgu_ws, m_sgu_bs, m_w_proj_a, m_w_proj_b, m_w_out, m_norm_ffn_g, m_w_gate_up, m_w_down, m_norm_final_g, v_norm_mix_g, v_w_in, v_conv_w, v_conv_b, v_rg_wa, v_rg_ba, v_rg_wx, v_rg_bx, v_rg_lambda, v_sgu_ln_g, v_sgu_ln_b, v_sgu_ws, v_sgu_bs, v_w_proj_a, v_w_proj_b, v_w_out, v_norm_ffn_g, v_w_gate_up, v_w_down, v_norm_final_g):
    args = dict(locals())
    w = {n: args[n] for n in _WEIGHTS}
    mom = {n: args["m_" + n] for n in _WEIGHTS}
    var = {n: args["v_" + n] for n in _WEIGHTS}
    xi, yi, ci = _position()
    core = ci.astype(jnp.int32).reshape(1)
    chip = (2 * xi + yi).astype(jnp.int32).reshape(1)

    bf = {n: w[n][0].astype(BF16) for n in _BIG}
    final_g = w["norm_final_g"].reshape(1, D)
    ba, bx = w["rg_ba"].reshape(1, D), w["rg_bx"].reshape(1, D)
    lam, ln_g, ln_b = w["rg_lambda"], w["sgu_ln_g"], w["sgu_ln_b"]
    x0, target = x[0], loss_target[0]

    def shard_major(g):
        return g.reshape(N_SHARD, g.shape[0] // N_SHARD, g.shape[1])

    def per_shape(names, fn, *lists):
        if len({a.shape for a in lists[0]}) == 1:
            return fn("_".join(names), *lists)
        return [r for k, n in enumerate(names) for r in fn(n, *[[a[k]] for a in lists])]

    def chip_sums(names, grads, theirs):
        return per_shape(names, lambda s, g, t: _add_halves(core, g, t, "add_halves_" + s), grads, theirs)

    def my_halves(names, sums, arrived):
        return per_shape(names, lambda s, p, a: _sum_shards(chip, p, a, "sum_shards_" + s), sums, arrived)

    (proj, h), ((w_in_a,), (conv_a,)) = _inproj_own(
        chip, x0, w["norm_mix_g"], bf["w_in"],
        sides=[_gather_half_side([bf["w_in"]], relations=(0, 1)), _gather_side([w["conv_w"][0]])])
    conv_cols = conv_a.shape[-1]
    conv_full = jnp.swapaxes(conv_a, 0, 1).reshape(CONV_WIDTH, D)
    (proj,), ((w_in_a,),) = _inproj_rest(
        chip, proj, h, w_in_a, 1, 2, "inproj_near",
        sides=[_gather_half_side([bf["w_in"]], relations=(2,), into=[w_in_a])])
    (proj,), _ = _inproj_rest(chip, proj, h, w_in_a, 3, 1, "inproj_far")
    (ya, xc, hseq, gates), ((w_pa_a, w_pb_a, w_out_a, w_gu_a, w_down_a),) = _rglru_fwd(
        proj, conv_full, w["conv_b"], w["rg_wa"], w["rg_wx"], ba, bx, lam,
        sides=[_gather_half_side([bf[n] for n in ("w_proj_a", "w_proj_b", "w_out", "w_gate_up", "w_down")])])
    wpa, wpb, wout, wdown = w_pa_a.reshape(D, D), w_pb_a.reshape(D, D), w_out_a.reshape(D, D), w_down_a.reshape(D_FF, D)
    yb, pa, pb, mb, x1 = _sgu_merge_fwd(x0, proj, ya, ln_g, ln_b, w["sgu_ws"], w["sgu_bs"], wpa, wpb, wout)
    h2, gu, act, dx2, dx2b, loss, d_final_g = _ffn_fwd_loss(x1, w["norm_ffn_g"], w_gu_a, wdown, final_g, target)

    dgu, dx1, dx1b, d_ffn_g = _ffn_bwd(dx2, dx2b, gu, wdown, w_gu_a, x1, w["norm_ffn_g"])
    ffn = ("w_gate_up", "w_down")
    g_ffn = [_matmul_tn(h2, dgu, FF_SHARD, True, "wgrad_gate_up")[0],
             shard_major(_matmul_tn(act, dx2b, D // 2, False, "wgrad_down")[0])]
    (dya, dproj, d_ws, d_bs, d_lng, d_lnb, g_pa, g_pb, g_out), (theirs_ffn,) = _merge_sgu_bwd(
        dx1b, proj, pa, pb, ya, yb, mb, ln_g, ln_b, w["sgu_ws"], w["sgu_bs"], wpa, wpb, wout,
        sides=[_halves_side(g_ffn)])
    sums_ffn = chip_sums(ffn, g_ffn, theirs_ffn)
    mix = ("w_proj_a", "w_proj_b", "w_out")
    g_mix = [shard_major(g) for g in (g_pa, g_pb, g_out)]
    (dproj, d_cw, d_cb, d_wa, d_wx, d_ba, d_bx, d_lam), (arrived_ffn, theirs_mix, sgu_parts) = _rglru_bwd(
        dya, dproj, proj, hseq, xc, gates, conv_full, w["rg_wa"], w["rg_wx"], lam,
        sides=[_scatter_side(sums_ffn), _halves_side(g_mix), _everyone_side([d_bs, d_ws])])
    mine_ffn = my_halves(ffn, sums_ffn, arrived_ffn)
    sums_mix = chip_sums(mix, g_mix, theirs_mix)
    g_in, ((wa_parts, wx_parts), other_ffn, arrived_mix) = _inproj_wgrad(
        h, dproj, sides=[_everyone_side([d_wa, d_wx]), _swap_side(mine_ffn),
                         _scatter_side(sums_mix)])
    mine_mix = my_halves(mix, sums_mix, arrived_mix)
    n_tiles = x0.shape[0] // min(TM_MM, x0.shape[0])
    n_first = max(1, n_tiles * 3 // 8)
    (dh,), (theirs_in, other_mix) = _inproj_dh(dproj, w_in_a, None, 0, n_first, "inproj_dh_a",
                                               sides=[_halves_side([g_in]), _swap_side(mine_mix)])
    sums_in = chip_sums(("w_in",), [g_in], theirs_in)
    (dh,), (arrived_in,) = _inproj_dh(dproj, w_in_a, dh, n_first, n_tiles - n_first, "inproj_dh_b",
                                      sides=[_scatter_side(sums_in)])
    mine_in = my_halves(("w_in",), sums_in, arrived_in)
    grad_x, d_mix_g = _inproj_norm_bwd(dh, x0, w["norm_mix_g"], dx1)
    rows = {"norm_mix_g": d_mix_g, "conv_b": d_cb, "rg_lambda": d_lam, "sgu_ln_g": d_lng, "sgu_ln_b": d_lnb,
            "norm_ffn_g": d_ffn_g, "norm_final_g": d_final_g, "rg_ba": d_ba, "rg_bx": d_bx}
    vec = jnp.concatenate([rows[n] for n in _VEC_ROWS]
                          + [d_cw, jnp.pad(loss, ((0, _VEC_PAD), (0, D - 1)))], axis=0)
    other_in, (vec_parts,) = _comm_only([_swap_side(mine_in), _everyone_side([vec])], "swap_w_in")
    small_parts = [vec_parts] + sgu_parts + [wa_parts, wx_parts]

    out = {}
    for names, gm, go in ((ffn, mine_ffn, other_ffn), (mix, mine_mix, other_mix), (("w_in",), mine_in, other_in)):
        results = per_shape(
            names, lambda s, *lists: _adamw_shard(core, *lists, "adamw_" + s),
            gm, go, [w[n][0] for n in names], [mom[n][0] for n in names], [var[n][0] for n in names])
        for n, res in zip(names, results):
            out[n] = tuple(a[None] for a in res)
    as_row = lambda t: {n: a.reshape(1, D) if n == "norm_final_g" else a for n, a in t.items()}
    small_out, conv_sum, loss_sum = _small_sum_adamw(small_parts, as_row(w), as_row(mom), as_row(var))
    out.update(small_out)
    out["norm_final_g"] = tuple(a.reshape(D) for a in small_out["norm_final_g"])
    conv_g = lax.dynamic_slice_in_dim(conv_sum, chip[0] * conv_cols, conv_cols, axis=1)
    d, mo, vo = _adamw_whole(w["conv_w"][0], conv_g, mom["conv_w"][0], var["conv_w"][0], "adamw_conv_w")
    out["conv_w"] = tuple(a[None] for a in (conv_g, d, mo, vo))

    return (loss_sum[0, 0], grad_x[None], *[out[n][0] for n in _WEIGHTS], *[out[n][1] for n in _WEIGHTS],
            *[out[n][2] for n in _WEIGHTS], *[out[n][3] for n in _WEIGHTS])
```

```python
import functools

import jax
import jax.numpy as jnp
from jax import lax
from jax.experimental import pallas as pl
from jax.experimental.pallas import tpu as pltpu

F32 = jnp.float32
BF16 = jnp.bfloat16
S = jax.ShapeDtypeStruct

D = 1024
N_SHARD = 4
IN_COLS = 6 * D
IN_SHARD = IN_COLS // N_SHARD
D_FF = 2816
FF_SHARD = 2 * D_FF // N_SHARD
RG_BLOCK = 256
N_RG_BLOCK = D // RG_BLOCK
CHUNK = 128
N_GROUP = 8
CONV_WIDTH = 4
RG_C = 8.0
EPS = 1e-6
ADAM_LR, ADAM_B1, ADAM_B2, ADAM_EPS, ADAM_WD, ADAM_STEP = 0.001, 0.9, 0.999, 1e-08, 0.01, 10

V7X_VMEM_BYTES = 64 * 1024 * 1024
VMEM_LIMIT = V7X_VMEM_BYTES * 3 // 4
SUBLANES = 8
MESH = pl.DeviceIdType.MESH

TM_MM = 512
TM_SCAN = 256
TM_FF = 256
TK_WGRAD = 2048


def _params(n_axes):
    return pltpu.CompilerParams(dimension_semantics=("arbitrary",) * n_axes, vmem_limit_bytes=VMEM_LIMIT)


def _resident(shape):
    nd = len(shape)
    return pl.BlockSpec(shape, lambda *_: (0,) * nd, pipeline_mode=pl.Buffered(1))


def _sig(x):
    return 1.0 / (1.0 + jnp.exp(-x))


_GELU_K2 = 2.0 * 0.7978845608028654
_GELU_C = 0.044715


def _gelu(x):
    return x * _sig(x * (_GELU_K2 + (_GELU_K2 * _GELU_C) * (x * x)))


def _gelu_and_grad(x):
    x2 = x * x
    s = _sig(x * (_GELU_K2 + (_GELU_K2 * _GELU_C) * x2))
    g = x * s
    return g, s + g * (1.0 - s) * (_GELU_K2 + (3.0 * _GELU_K2 * _GELU_C) * x2)


_EXPM1_SERIES = tuple(1.0 / f for f in (5040.0, 720.0, 120.0, 24.0, 6.0, 2.0, 1.0))


def _one_minus_exp(x):
    p = _EXPM1_SERIES[0]
    for coef in _EXPM1_SERIES[1:]:
        p = p * x + coef
    return jnp.where(x > -0.125, -x * p, 1.0 - jnp.exp(x))


def _softplus_neg(lam):
    z = -lam
    e = jnp.exp(-jnp.abs(z))
    u = 1.0 + e
    log1p = jnp.where(u == 1.0, e, jnp.log(u) * e / (u - 1.0))
    return jnp.maximum(z, 0.0) + log1p


def _rms_stats(x):
    return lax.rsqrt(jnp.mean(x * x, axis=-1, keepdims=True) + EPS)


def _rms_bwd(dy, x, g):
    rstd = _rms_stats(x)
    xhat = x * rstd
    dxhat = dy * g
    dx = rstd * (dxhat - xhat * jnp.mean(dxhat * xhat, axis=-1, keepdims=True))
    return dx, dy * xhat


def _colsum(x):
    return jnp.sum(x, axis=0, keepdims=True)


def _shift_down(x, d, fill):
    n = x.shape[0]
    if d % SUBLANES == 0:
        return jnp.concatenate([jnp.full((d, x.shape[1]), fill, x.dtype), x[:n - d]], axis=0)
    row = lax.broadcasted_iota(jnp.int32, x.shape, 0)
    return jnp.where(row < d, fill, pltpu.roll(x, d, 0))


def _shift_up(x, d, fill):
    n = x.shape[0]
    if d % SUBLANES == 0:
        return jnp.concatenate([x[d:], jnp.full((d, x.shape[1]), fill, x.dtype)], axis=0)
    row = lax.broadcasted_iota(jnp.int32, x.shape, 0)
    return jnp.where(row >= n - d, fill, pltpu.roll(x, n - d, 0))


def _scan(a, b, shift):
    d = 1
    while d < a.shape[0]:
        b = a * shift(b, d, 0.0) + b
        a = a * shift(a, d, 1.0)
        d *= 2
    return a, b


LANES = 128


def _scan_tile(a, b, outside, a_s, b_s, h_s, reverse):
    tm = a.shape[0]
    groups = tm // SUBLANES
    order = list(range(SUBLANES - 1, -1, -1) if reverse else range(SUBLANES))
    shift = _shift_up if reverse else _shift_down
    edge = groups - 1 if reverse else 0
    for j in range(D // LANES):
        a_s[j] = a[:, LANES * j:LANES * (j + 1)]
        b_s[j] = b[:, LANES * j:LANES * (j + 1)]
    for j in range(D // LANES):
        def slab(ref, k):
            return ref[j, pl.ds(k, groups, stride=SUBLANES), :]

        ga, gb = slab(a_s, order[0]), slab(b_s, order[0])
        for k in order[1:]:
            ak = slab(a_s, k)
            gb = ak * gb + slab(b_s, k)
            ga = ak * ga
        ga, gb = _scan(ga, gb, shift)
        h_out = outside[:, LANES * j:LANES * (j + 1)]
        group_end = ga * h_out + gb
        row = lax.broadcasted_iota(jnp.int32, (groups, LANES), 0)
        h = jnp.where(row == edge, h_out, shift(group_end, 1, 0.0))
        for k in order:
            h = slab(a_s, k) * h + slab(b_s, k)
            h_s[j, pl.ds(k, groups, stride=SUBLANES), :] = h
    return jnp.concatenate([h_s[j] for j in range(D // LANES)], axis=1)


def _dot(a, b):
    return jnp.dot(a, b, preferred_element_type=F32)


def _dot_nt(a, b):
    return lax.dot_general(a, b, (((1,), (1,)), ((), ())), preferred_element_type=F32)


def _dot_tn(a, b):
    return lax.dot_general(a, b, (((0,), (0,)), ((), ())), preferred_element_type=F32)


_ANY = pl.BlockSpec(memory_space=pl.ANY)


def _position():
    return lax.axis_index("x"), lax.axis_index("y"), lax.axis_index("c")


def _other_chips(x, y):
    return [(1 - x, y), (x, 1 - y), (1 - x, 1 - y)]


class _Side:
    def __init__(self, inputs, out_shapes, n_sems, make, continues=()):
        self.inputs, self.out_shapes, self.n_sems, self.make = list(inputs), list(out_shapes), n_sems, make
        self.continues = list(continues)


MID_STEP = 0.625
LATE_MID_STEP = 0.875


def _call(body, *, name, grid, in_specs, out_specs, out_shape, args, scratch_shapes=(), sides=(), aliases=None,
          scalars=None, mid=MID_STEP):
    n_in, n_out, n_scr = len(in_specs), len(out_specs), len(scratch_shapes)
    n_scalar = 0 if scalars is None else 1
    side_in = [len(s.inputs) + len(s.continues) for s in sides]
    side_out = [len(s.out_shapes) for s in sides]
    all_aliases = {k + n_scalar: v for k, v in (aliases or {}).items()}
    for idx, s in enumerate(sides):
        for k in range(len(s.continues)):
            operand = n_scalar + n_in + sum(side_in[:idx]) + len(s.inputs) + k
            all_aliases[operand] = n_out + sum(side_out[:idx]) + k

    def wrapped(*refs):
        refs = list(refs)
        take = lambda k: [refs.pop(0) for _ in range(k)]
        ins = take(n_scalar) + take(n_in)
        sins = [take(k) for k in side_in]
        outs = take(n_out)
        souts = [take(k) for k in side_out]
        scr = take(n_scr)
        sems = [take(3) for _ in sides]
        def run(phase):
            for s, si, so, sem in zip(sides, sins, souts, sems):
                for thunk in s.make(si[:len(s.inputs)], so, *sem)[phase]:
                    thunk()

        if sides:
            n_steps = functools.reduce(lambda a, b: a * b, grid)
            step = functools.reduce(lambda a, b: a + b, [
                pl.program_id(a) * functools.reduce(lambda p, q: p * q, grid[a + 1:], 1) for a in range(len(grid))])
            pl.when(step == 0)(lambda: run(0))
        body(*ins, *outs, *scr)
        if sides:
            pl.when(step == int(mid * (n_steps - 1)))(lambda: run(1))
            pl.when(step == n_steps - 1)(lambda: run(2))

    grid_spec = pltpu.PrefetchScalarGridSpec(
        num_scalar_prefetch=n_scalar, grid=grid,
        in_specs=list(in_specs) + [_ANY] * sum(side_in),
        out_specs=list(out_specs) + [_ANY] * sum(side_out),
        scratch_shapes=list(scratch_shapes) + [pltpu.SemaphoreType.DMA((s.n_sems,)) for s in sides for _ in range(3)])
    res = pl.pallas_call(
        wrapped, name=name, grid_spec=grid_spec,
        out_shape=list(out_shape) + [o for s in sides for o in s.out_shapes],
        input_output_aliases=all_aliases,
        compiler_params=_params(len(grid)),
    )(*([scalars] if n_scalar else []), *args, *[a for s in sides for a in s.inputs + s.continues])
    main, rest, per_side = list(res[:n_out]), list(res[n_out:]), []
    for k in side_out:
        per_side.append(rest[:k])
        rest = rest[k:]
    return main, per_side


def _comm_only(sides, name):
    def body():
        pass

    return _call(body, name=name, grid=(1,), in_specs=[], out_specs=[], out_shape=[], args=[], sides=sides)[1]


def _remote(src, dst, send, recv, k, device):
    return pltpu.make_async_remote_copy(src_ref=src, dst_ref=dst, send_sem=send.at[k], recv_sem=recv.at[k],
                                        device_id=device, device_id_type=MESH)


def _both_ways(copy, keys):
    return [lambda k=k: copy(k).start() for k in keys], [], [lambda k=k: copy(k).wait() for k in keys]


def _gather_side(shards):
    n = len(shards)

    def make(ins, outs, send, recv, local):
        x, y, c = _position()
        mine = 2 * x + y
        chips = _other_chips(x, y)
        pairs = [(w, j) for w in range(n) for j in range(3)]

        def own(w):
            return pltpu.make_async_copy(ins[w], outs[w].at[mine], local.at[w])

        def push(w, j):
            return _remote(ins[w], outs[w].at[mine], send, recv, 3 * w + j, (*chips[j], c))

        def arrival(w, j):
            px, py = chips[j]
            return _remote(ins[w], outs[w].at[2 * px + py], send, recv, 3 * w + j, (px, py, c))

        starts = [lambda w=w: own(w).start() for w in range(n)] + [lambda w=w, j=j: push(w, j).start() for w, j in pairs]
        waits = ([lambda w=w, j=j: arrival(w, j).wait_recv() for w, j in pairs]
                 + [lambda w=w, j=j: push(w, j).wait_send() for w, j in pairs]
                 + [lambda w=w: own(w).wait() for w in range(n)])
        return starts, [], waits

    return _Side(shards, [S((N_SHARD,) + s.shape, s.dtype) for s in shards], 3 * n, make)


def _gather_half_side(shards, relations=(0, 1, 2), into=None):
    n = len(shards)

    def make(ins, outs, send, recv, local):
        x, y, c = _position()
        mine = 2 * x + y
        chips = _other_chips(x, y)
        pairs = [(w, j) for w in range(n) for j in relations]

        def rows(w, core):
            half = ins[w].shape[0] // 2
            return pl.ds(core * half, half)

        def own(w):
            return pltpu.make_async_copy(ins[w], outs[w].at[mine], local.at[w])

        def push(w, j):
            return _remote(ins[w].at[rows(w, c), :], outs[w].at[mine, rows(w, c), :], send, recv, 3 * w + j,
                           (*chips[j], c))

        def landed(w, j, core):
            px, py = chips[j]
            return outs[w].at[2 * px + py, rows(w, core), :]

        def arrival(w, j):
            return _remote(ins[w].at[rows(w, c), :], landed(w, j, c), send, recv, 3 * w + j, (*chips[j], c))

        def passed(w, j, core):
            return _remote(landed(w, j, core), landed(w, j, core), send, recv, 3 * n + 3 * w + j, (x, y, 1 - c))

        owns = range(n) if into is None else ()
        starts = [lambda w=w: own(w).start() for w in owns] + [lambda w=w, j=j: push(w, j).start() for w, j in pairs]
        mids = [t for w, j in pairs for t in (lambda w=w, j=j: arrival(w, j).wait_recv(),
                                              lambda w=w, j=j: passed(w, j, c).start())]
        waits = ([lambda w=w, j=j: passed(w, j, 1 - c).wait_recv() for w, j in pairs]
                 + [lambda w=w, j=j: passed(w, j, c).wait_send() for w, j in pairs]
                 + [lambda w=w, j=j: push(w, j).wait_send() for w, j in pairs]
                 + [lambda w=w: own(w).wait() for w in owns])
        return starts, mids, waits

    return _Side(shards, [S((N_SHARD,) + s.shape, s.dtype) for s in shards], 6 * n, make, continues=into or ())


def _halves_side(grads):
    n = len(grads)

    def make(ins, outs, send, recv, local):
        x, y, c = _position()

        def copy(w):
            half = ins[w].shape[1] // 2
            return _remote(ins[w].at[:, pl.ds((1 - c) * half, half), :], outs[w], send, recv, w, (x, y, 1 - c))

        return _both_ways(copy, range(n))

    return _Side(grads, [S((N_SHARD, g.shape[1] // 2, g.shape[2]), F32) for g in grads], n, make)


def _scatter_side(partials):
    n = len(partials)

    def make(ins, outs, send, recv, local):
        x, y, c = _position()
        chips = _other_chips(x, y)

        def copy(k):
            w, j = divmod(k, 3)
            px, py = chips[j]
            return _remote(ins[w].at[2 * px + py], outs[w].at[j], send, recv, k, (px, py, c))

        return _both_ways(copy, range(3 * n))

    return _Side(partials, [S((3,) + p.shape[1:], p.dtype) for p in partials], 3 * n, make)


def _swap_side(halves):
    n = len(halves)

    def make(ins, outs, send, recv, local):
        x, y, c = _position()
        return _both_ways(lambda w: _remote(ins[w], outs[w], send, recv, w, (x, y, 1 - c)), range(n))

    return _Side(halves, [S(h.shape, h.dtype) for h in halves], n, make)


N_DEVICE = 8


def _everyone_side(arrays):
    n = len(arrays)
    peers = N_DEVICE - 1

    def make(ins, outs, send, recv, local):
        x, y, c = _position()
        mine = 4 * x + 2 * y + c
        pairs = [(w, k) for w in range(n) for k in range(1, N_DEVICE)]

        def peer(k):
            return (1 - x if k & 4 else x, 1 - y if k & 2 else y, 1 - c if k & 1 else c)

        def own(w):
            return pltpu.make_async_copy(ins[w], outs[w].at[mine], local.at[w])

        def push(w, k):
            return _remote(ins[w], outs[w].at[mine], send, recv, peers * w + k - 1, peer(k))

        def arrival(w, k):
            px, py, pc = peer(k)
            return _remote(ins[w], outs[w].at[4 * px + 2 * py + pc], send, recv, peers * w + k - 1, (px, py, pc))

        starts = [lambda w=w: own(w).start() for w in range(n)] + [lambda w=w, k=k: push(w, k).start() for w, k in pairs]
        waits = ([lambda w=w, k=k: arrival(w, k).wait_recv() for w, k in pairs]
                 + [lambda w=w, k=k: push(w, k).wait_send() for w, k in pairs]
                 + [lambda w=w: own(w).wait() for w in range(n)])
        return starts, [], waits

    return _Side(arrays, [S((N_DEVICE,) + a.shape, a.dtype) for a in arrays], peers * n, make)


def _inproj_own(chip, x, g, w_shard, sides=()):
    T = x.shape[0]
    tm = min(TM_MM, T)

    def body(chip_ref, x_ref, g_ref, w_ref, proj_ref, h_ref):
        xv = x_ref[...]
        h = (xv * _rms_stats(xv) * g_ref[...]).astype(BF16)
        h_ref[...] = h
        proj_ref[...] = _dot(h, w_ref[...]).astype(BF16)

    return _call(
        body, name="inproj_own", grid=(T // tm,),
        in_specs=[pl.BlockSpec((tm, D), lambda i, c: (i, 0)), pl.BlockSpec((1, D), lambda i, c: (0, 0)),
                  pl.BlockSpec((D, IN_SHARD), lambda i, c: (0, 0), pipeline_mode=pl.Buffered(1))],
        out_specs=[pl.BlockSpec((tm, IN_SHARD), lambda i, c: (i, c[0])), pl.BlockSpec((tm, D), lambda i, c: (i, 0))],
        out_shape=[S((T, IN_COLS), BF16), S((T, D), BF16)],
        args=(x, g, w_shard), sides=sides, scalars=chip, mid=LATE_MID_STEP)


def _inproj_rest(chip, proj, h, w_in, first, count, name, sides=(), mid=MID_STEP):
    T = h.shape[0]
    tm = min(TM_MM, T)

    def body(chip_ref, _, h_ref, w_ref, proj_ref):
        proj_ref[...] = _dot(h_ref[...], w_ref[0]).astype(BF16)

    def other(p, c):
        return jnp.bitwise_xor(c[0], first + p)

    return _call(
        body, name=name, grid=(count, T // tm),
        in_specs=[_ANY, pl.BlockSpec((tm, D), lambda p, i, c: (i, 0)),
                  pl.BlockSpec((1, D, IN_SHARD), lambda p, i, c: (other(p, c), 0, 0))],
        out_specs=[pl.BlockSpec((tm, IN_SHARD), lambda p, i, c: (i, other(p, c)))],
        out_shape=[S((T, IN_COLS), BF16)],
        args=(proj, h, w_in), sides=sides, scalars=chip, aliases={0: 0}, mid=mid)


def _rg_gates(xc, wa_ref, wx_ref, ba, bx, sp):
    xb = xc.astype(BF16)
    blocks = [xb[:, RG_BLOCK * j:RG_BLOCK * (j + 1)] for j in range(N_RG_BLOCK)]
    r = _sig(jnp.concatenate([_dot(blocks[j], wa_ref[j]) for j in range(N_RG_BLOCK)], axis=1) + ba)
    gi = _sig(jnp.concatenate([_dot(blocks[j], wx_ref[j]) for j in range(N_RG_BLOCK)], axis=1) + bx)
    log_a = (-RG_C) * r * sp
    a = jnp.exp(log_a)
    m = jnp.sqrt(_one_minus_exp(2.0 * log_a))
    return xb, r, gi, a, m


N_GATES = 4
HEADS_PER_BLOCK = 4
HEAD_DIM = RG_BLOCK // HEADS_PER_BLOCK
_RG_BLOCKS_BF16 = pltpu.VMEM((N_RG_BLOCK, RG_BLOCK, RG_BLOCK), BF16)


def _fill_blockdiag(heads_ref, blocks):
    blocks[...] = jnp.zeros_like(blocks)
    for j in range(N_RG_BLOCK):
        for h in range(HEADS_PER_BLOCK):
            sl = slice(HEAD_DIM * h, HEAD_DIM * (h + 1))
            blocks[j, sl, sl] = heads_ref[0, HEADS_PER_BLOCK * j + h].astype(BF16)


def _rglru_fwd(proj, conv_w, conv_b, rg_wa, rg_wx, ba, bx, lam, sides=()):
    T = proj.shape[0]
    tm = min(TM_SCAN, T)

    def body(rx_ref, gate_ref, cw_ref, cb_ref, wah_ref, wxh_ref, ba_ref, bx_ref, lam_ref,
             ya_ref, xc_ref, h_ref, gates_ref, ext, hc, a_s, b_s, h_s, wa_ref, wx_ref):
        @pl.when(pl.program_id(0) == 0)
        def _():
            ext[0:SUBLANES, :] = jnp.zeros((SUBLANES, D), F32)
            hc[...] = jnp.zeros((SUBLANES, D), F32)
            _fill_blockdiag(wah_ref, wa_ref)
            _fill_blockdiag(wxh_ref, wx_ref)

        ext[SUBLANES:SUBLANES + tm, :] = rx_ref[...].astype(F32)
        xc = cb_ref[...]
        for k in range(CONV_WIDTH):
            xc = xc + ext[pl.ds(SUBLANES - (CONV_WIDTH - 1) + k, tm), :] * cw_ref[k:k + 1, :]
        ext[0:SUBLANES, :] = ext[tm:tm + SUBLANES, :]
        xc_ref[...] = xc
        _, r, gi, a, m = _rg_gates(xc, wa_ref, wx_ref, ba_ref[...], bx_ref[...], _softplus_neg(lam_ref[...]))
        for k, val in enumerate((r, gi, a, m)):
            gates_ref[:, D * k:D * (k + 1)] = val
        h = _scan_tile(a, m * (gi * xc), hc[0:1, :], a_s, b_s, h_s, reverse=False)
        hc[...] = jnp.broadcast_to(h[tm - 1:tm, :], (SUBLANES, D))
        h_ref[...] = h
        ya_ref[...] = (_gelu(gate_ref[...].astype(F32)) * h).astype(BF16)

    vec = pl.BlockSpec((1, D), lambda i: (0, 0))
    heads = pl.BlockSpec(rg_wa.shape, lambda i: (0, 0, 0, 0))
    tile = pl.BlockSpec((tm, D), lambda i: (i, 0))
    return _call(
        body, name="rglru_fwd", grid=(T // tm,),
        in_specs=[pl.BlockSpec((tm, D), lambda i: (i, 0)), pl.BlockSpec((tm, D), lambda i: (i, 1)),
                  pl.BlockSpec((CONV_WIDTH, D), lambda i: (0, 0)), vec, heads, heads, vec, vec, vec],
        out_specs=[tile, tile, tile, pl.BlockSpec((tm, N_GATES * D), lambda i: (i, 0))],
        out_shape=[S((T, D), BF16), S((T, D), F32), S((T, D), F32), S((T, N_GATES * D), F32)],
        scratch_shapes=[pltpu.VMEM((tm + SUBLANES, D), F32), pltpu.VMEM((SUBLANES, D), F32)]
        + [pltpu.VMEM((D // LANES, tm, LANES), F32)] * 3 + [_RG_BLOCKS_BF16] * 2,
        args=(proj, proj, conv_w, conv_b, rg_wa, rg_wx, ba, bx, lam), sides=sides, mid=LATE_MID_STEP)


def _layer_norm_stats(v):
    mu = jnp.mean(v, axis=-1, keepdims=True)
    vc = v - mu
    rstd = lax.rsqrt(jnp.mean(vc * vc, axis=-1, keepdims=True) + EPS)
    return vc * rstd, rstd


def _sgu_mix(w_ref, vnb, bst_ref, n_chunk):
    cols = []
    for g in range(N_GROUP):
        vg = vnb[:, CHUNK * g:CHUNK * (g + 1)].reshape(n_chunk, CHUNK, CHUNK)
        wb = jnp.broadcast_to(w_ref[g][None], (n_chunk, CHUNK, CHUNK))
        mg = lax.dot_general(wb, vg, (((2,), (1,)), ((0,), (0,))), preferred_element_type=F32)
        mg = mg + bst_ref[:, g:g + 1][None]
        cols.append(mg.reshape(n_chunk * CHUNK, CHUNK))
    return jnp.concatenate(cols, axis=1)


def _causal_mask():
    return (lax.broadcasted_iota(jnp.int32, (CHUNK, CHUNK), 0) >= lax.broadcasted_iota(jnp.int32, (CHUNK, CHUNK), 1))


def _fill_sgu_weights(ws_ref, bs_ref, w_tril, bs_t, w_tril_t=None):
    keep = _causal_mask()
    for g in range(N_GROUP):
        wg = jnp.where(keep, ws_ref[0, g], 0.0)
        w_tril[g] = wg.astype(BF16)
        if w_tril_t is not None:
            w_tril_t[g] = wg.T.astype(BF16)
    bs_t[...] = bs_ref[0].T


_SGU_W_BF16 = pltpu.VMEM((N_GROUP, CHUNK, CHUNK), BF16)
_SGU_BT = pltpu.VMEM((CHUNK, N_GROUP), F32)


def _sgu_merge_fwd(x, proj, ya, ln_g, ln_b, sgu_ws, sgu_bs, wpa, wpb, wout):
    T = x.shape[0]
    tm = min(TM_MM, T)
    n_chunk = tm // CHUNK

    def body(x_ref, uv_ref, gab_ref, ya_ref, g_ref, b_ref, ws_ref, bs_ref, wpa_ref, wpb_ref, wout_ref,
             yb_ref, pa_ref, pb_ref, mb_ref, x1_ref, w_ref, bst_ref):
        @pl.when(pl.program_id(0) == 0)
        def _():
            _fill_sgu_weights(ws_ref, bs_ref, w_ref, bst_ref)

        vhat, _ = _layer_norm_stats(_gelu(uv_ref[:, D:2 * D].astype(F32)))
        vnb = (vhat * g_ref[...] + b_ref[...]).astype(BF16)
        yb = (_gelu(uv_ref[:, 0:D].astype(F32)) * _sgu_mix(w_ref, vnb, bst_ref, n_chunk)).astype(BF16)
        yb_ref[...] = yb
        pa = _dot(ya_ref[...], wpa_ref[...])
        pb = _dot(yb, wpb_ref[...])
        pa_ref[...] = pa.astype(BF16)
        pb_ref[...] = pb.astype(BF16)
        mb = (_sig(gab_ref[:, 0:D].astype(F32)) * pa + _sig(gab_ref[:, D:2 * D].astype(F32)) * pb).astype(BF16)
        mb_ref[...] = mb
        x1_ref[...] = x_ref[...] + _dot(mb, wout_ref[...])

    tile = pl.BlockSpec((tm, D), lambda i: (i, 0))
    vec = pl.BlockSpec((1, D), lambda i: (0, 0))
    w = _resident((D, D))
    return pl.pallas_call(
        body, name="sgu_merge_fwd", grid=(T // tm,),
        in_specs=[tile, pl.BlockSpec((tm, 2 * D), lambda i: (i, 1)), pl.BlockSpec((tm, 2 * D), lambda i: (i, 2)), tile,
                  vec, vec, pl.BlockSpec(sgu_ws.shape, lambda i: (0, 0, 0, 0)),
                  pl.BlockSpec(sgu_bs.shape, lambda i: (0, 0, 0)), w, w, w],
        out_specs=[tile, tile, tile, tile, tile],
        out_shape=[S((T, D), BF16), S((T, D), BF16), S((T, D), BF16), S((T, D), BF16), S((T, D), F32)],
        scratch_shapes=[_SGU_W_BF16, _SGU_BT],
        compiler_params=_params(1),
    )(x, proj, proj, ya, ln_g, ln_b, sgu_ws, sgu_bs, wpa, wpb, wout)


def _ffn_fwd_loss(x1, g, w_gu, w_down, g_final, target):
    T = x1.shape[0]
    tm = min(TM_FF, T)

    def body(x_ref, g_ref, wgu_ref, wd_ref, gf_ref, t_ref,
             h2_ref, gu_ref, act_ref, dx2_ref, dx2b_ref, loss_ref, dg_ref):
        @pl.when(pl.program_id(0) == 0)
        def _():
            loss_ref[...] = jnp.zeros_like(loss_ref)
            dg_ref[...] = jnp.zeros_like(dg_ref)

        xv = x_ref[...]
        h2 = (xv * _rms_stats(xv) * g_ref[...]).astype(BF16)
        h2_ref[...] = h2
        x2 = xv
        for k in range(N_SHARD // 2):
            cols = slice(FF_SHARD * k, FF_SHARD * (k + 1))
            gate = _dot(h2, wgu_ref[k])
            up = _dot(h2, wgu_ref[k + N_SHARD // 2])
            gu_ref[:, cols] = gate.astype(BF16)
            gu_ref[:, D_FF + FF_SHARD * k:D_FF + FF_SHARD * (k + 1)] = up.astype(BF16)
            act = (gate * _sig(gate) * up).astype(BF16)
            act_ref[:, cols] = act
            x2 = x2 + _dot(act, wd_ref[cols, :])
        gf = gf_ref[...]
        err = x2 * _rms_stats(x2) * gf - t_ref[...]
        loss_ref[...] += 0.5 * jnp.sum(jnp.mean(err * err, axis=-1, keepdims=True), axis=0, keepdims=True)
        dx2, dg_rows = _rms_bwd(err * (1.0 / D), x2, gf)
        dg_ref[...] += _colsum(dg_rows)
        dx2_ref[...] = dx2
        dx2b_ref[...] = dx2.astype(BF16)

    tile = pl.BlockSpec((tm, D), lambda i: (i, 0))
    vec = pl.BlockSpec((1, D), lambda i: (0, 0))
    return pl.pallas_call(
        body, name="ffn_fwd_loss", grid=(T // tm,),
        in_specs=[tile, vec, _resident((N_SHARD, D, FF_SHARD)), _resident((D_FF, D)), vec, tile],
        out_specs=[tile, pl.BlockSpec((tm, 2 * D_FF), lambda i: (i, 0)), pl.BlockSpec((tm, D_FF), lambda i: (i, 0)),
                   tile, tile, pl.BlockSpec((1, 1), lambda i: (0, 0)), vec],
        out_shape=[S((T, D), BF16), S((T, 2 * D_FF), BF16), S((T, D_FF), BF16), S((T, D), F32), S((T, D), BF16),
                   S((1, 1), F32), S((1, D), F32)],
        compiler_params=_params(1),
    )(x1, g, w_gu, w_down, g_final, target)


def _ffn_bwd(dx2, dx2b, gu, w_down, w_gu, x1, g):
    T = x1.shape[0]
    tm = min(TM_FF, T)

    def body(dx2_ref, dx2b_ref, gu_ref, wd_ref, wgu_ref, x_ref, g_ref, dgu_ref, dx1_ref, dx1b_ref, dg_ref):
        @pl.when(pl.program_id(0) == 0)
        def _():
            dg_ref[...] = jnp.zeros_like(dg_ref)

        dxb = dx2b_ref[...]
        dh2 = jnp.zeros((tm, D), F32)
        for k in range(N_SHARD // 2):
            cols = slice(FF_SHARD * k, FF_SHARD * (k + 1))
            up_cols = slice(D_FF + FF_SHARD * k, D_FF + FF_SHARD * (k + 1))
            dact = _dot_nt(dxb, wd_ref[cols, :])
            gate = gu_ref[:, cols].astype(F32)
            sg = _sig(gate)
            dgate = (dact * gu_ref[:, up_cols].astype(F32) * (sg * (1.0 + gate * (1.0 - sg)))).astype(BF16)
            dup = (dact * (gate * sg)).astype(BF16)
            dgu_ref[:, cols] = dgate
            dgu_ref[:, up_cols] = dup
            dh2 = dh2 + _dot_nt(dgate, wgu_ref[k]) + _dot_nt(dup, wgu_ref[k + N_SHARD // 2])
        dx, dg_rows = _rms_bwd(dh2, x_ref[...], g_ref[...])
        dg_ref[...] += _colsum(dg_rows)
        dx1 = dx2_ref[...] + dx
        dx1_ref[...] = dx1
        dx1b_ref[...] = dx1.astype(BF16)

    tile = pl.BlockSpec((tm, D), lambda i: (i, 0))
    wide = pl.BlockSpec((tm, 2 * D_FF), lambda i: (i, 0))
    vec = pl.BlockSpec((1, D), lambda i: (0, 0))
    return pl.pallas_call(
        body, name="ffn_bwd", grid=(T // tm,),
        in_specs=[tile, tile, wide, _resident((D_FF, D)), _resident((N_SHARD, D, FF_SHARD)), tile, vec],
        out_specs=[wide, tile, tile, vec],
        out_shape=[S((T, 2 * D_FF), BF16), S((T, D), F32), S((T, D), BF16), S((1, D), F32)],
        compiler_params=_params(1),
    )(dx2, dx2b, gu, w_down, w_gu, x1, g)


def _matmul_tn(a, b, tn, shard_major, name, sides=()):
    T, M = a.shape
    N = b.shape[1]
    tk = min(TK_WGRAD, T)

    def body(a_ref, b_ref, o_ref):
        @pl.when(pl.program_id(1) == 0)
        def _():
            o_ref[...] = jnp.zeros_like(o_ref)

        acc = _dot_tn(a_ref[...], b_ref[...])
        if shard_major:
            o_ref[0] += acc
        else:
            o_ref[...] += acc

    if shard_major:
        out_spec, out_shape = pl.BlockSpec((1, M, tn), lambda j, k: (j, 0, 0)), S((N // tn, M, tn), F32)
    else:
        out_spec, out_shape = pl.BlockSpec((M, tn), lambda j, k: (0, j)), S((M, N), F32)
    (out,), side_outs = _call(
        body, name=name, grid=(N // tn, T // tk),
        in_specs=[pl.BlockSpec((tk, M), lambda j, k: (k, 0)), pl.BlockSpec((tk, tn), lambda j, k: (k, j))],
        out_specs=[out_spec], out_shape=[out_shape], args=(a, b), sides=sides)
    return out, side_outs


PIECE = IN_SHARD // 3
N_PIECE = IN_COLS // PIECE
DPROJ_ROTATION = 2 * D // PIECE


def _merge_sgu_bwd(dx1b, proj, pa, pb, ya, yb, mb, ln_g, ln_b, sgu_ws, sgu_bs, wpa, wpb, wout, sides=()):
    T = dx1b.shape[0]
    tm = min(TM_FF, T)
    n_chunk = tm // CHUNK
    n_steps = T // tm

    def body(dx_ref, uv_ref, gab_ref, pa_ref, pb_ref, ya_ref, yb_ref, mb_ref, g_ref, b_ref, ws_ref, bs_ref,
             wpa_ref, wpb_ref, wout_ref,
             dya_ref, dp_ref, dw_ref, dbs_ref, dg_ref, db_ref, gpa_ref, gpb_ref, gout_ref,
             w_ref, wt_ref, bst_ref, acc_pa, acc_pb, acc_out):
        @pl.when(pl.program_id(0) == 0)
        def _():
            for ref in (dw_ref, dbs_ref, dg_ref, db_ref, acc_pa, acc_pb, acc_out):
                ref[...] = jnp.zeros_like(ref)
            _fill_sgu_weights(ws_ref, bs_ref, w_ref, bst_ref, wt_ref)

        dxb = dx_ref[...]
        dm = _dot_nt(dxb, wout_ref[...])
        sa = _sig(gab_ref[:, 0:D].astype(F32))
        sb = _sig(gab_ref[:, D:2 * D].astype(F32))
        dpa = (dm * sa).astype(BF16)
        dpb = (dm * sb).astype(BF16)
        acc_pa[...] += _dot_tn(ya_ref[...], dpa)
        acc_pb[...] += _dot_tn(yb_ref[...], dpb)
        acc_out[...] += _dot_tn(mb_ref[...], dxb)
        dp_ref[:, 2 * D:3 * D] = (dm * pa_ref[...].astype(F32) * (sa * (1.0 - sa))).astype(BF16)
        dp_ref[:, 3 * D:4 * D] = (dm * pb_ref[...].astype(F32) * (sb * (1.0 - sb))).astype(BF16)
        dya_ref[...] = _dot_nt(dpa, wpa_ref[...]).astype(BF16)
        dyb_v = _dot_nt(dpb, wpb_ref[...])

        gu, dgu = _gelu_and_grad(uv_ref[:, 0:D].astype(F32))
        gv, dgv = _gelu_and_grad(uv_ref[:, D:2 * D].astype(F32))
        vhat, rstd = _layer_norm_stats(gv)
        lng = g_ref[...]
        vnb = (vhat * lng + b_ref[...]).astype(BF16)
        mixed = _sgu_mix(w_ref, vnb, bst_ref, n_chunk)
        dp_ref[:, 0:D] = (dyb_v * mixed * dgu).astype(BF16)
        dmix = dyb_v * gu
        dmb = dmix.astype(BF16)
        keep = _causal_mask()
        dvn_cols, dbs_rows = [], []
        for g in range(N_GROUP):
            sl = slice(CHUNK * g, CHUNK * (g + 1))
            dmg = dmb[:, sl].reshape(n_chunk, CHUNK, CHUNK)
            vg = vnb[:, sl].reshape(n_chunk, CHUNK, CHUNK)
            wtb = jnp.broadcast_to(wt_ref[g][None], (n_chunk, CHUNK, CHUNK))
            dvn = lax.dot_general(wtb, dmg, (((2,), (1,)), ((0,), (0,))), preferred_element_type=F32)
            dvn_cols.append(dvn.reshape(tm, CHUNK))
            dw = lax.dot_general(dmg, vg, (((2,), (2,)), ((0,), (0,))), preferred_element_type=F32)
            dw_ref[g] += jnp.where(keep, jnp.sum(dw, axis=0), 0.0)
            per_token = jnp.sum(dmix[:, sl], axis=1)
            dbs_rows.append(jnp.sum(per_token.reshape(n_chunk, CHUNK), axis=0, keepdims=True))
        dbs_ref[...] += jnp.concatenate(dbs_rows, axis=0)
        dvn = jnp.concatenate(dvn_cols, axis=1)
        dg_ref[...] += _colsum(dvn * vhat)
        db_ref[...] += _colsum(dvn)
        dvhat = dvn * lng
        dgv_in = rstd * (dvhat - jnp.mean(dvhat, axis=-1, keepdims=True)
                         - vhat * jnp.mean(dvhat * vhat, axis=-1, keepdims=True))
        dp_ref[:, D:2 * D] = (dgv_in * dgv).astype(BF16)

        @pl.when(pl.program_id(0) == n_steps - 1)
        def _():
            for acc, out in ((acc_pa, gpa_ref), (acc_pb, gpb_ref), (acc_out, gout_ref)):
                pltpu.sync_copy(acc, out)

    tile = pl.BlockSpec((tm, D), lambda i: (i, 0))
    vec = pl.BlockSpec((1, D), lambda i: (0, 0))
    w = _resident((D, D))
    wsp = pl.BlockSpec((N_GROUP, CHUNK, CHUNK), lambda i: (0, 0, 0))
    acc = pltpu.VMEM((D, D), F32)
    return _call(
        body, name="merge_sgu_bwd", grid=(n_steps,),
        in_specs=[tile, pl.BlockSpec((tm, 2 * D), lambda i: (i, 1)), pl.BlockSpec((tm, 2 * D), lambda i: (i, 2)),
                  tile, tile, tile, tile, tile, vec, vec, pl.BlockSpec(sgu_ws.shape, lambda i: (0, 0, 0, 0)),
                  pl.BlockSpec(sgu_bs.shape, lambda i: (0, 0, 0)), w, w, w],
        out_specs=[tile, pl.BlockSpec((tm, 4 * D), lambda i: (i, 0)), wsp,
                   pl.BlockSpec((N_GROUP, CHUNK), lambda i: (0, 0)), vec, vec, _ANY, _ANY, _ANY],
        out_shape=[S((T, D), BF16), S((T, IN_COLS), BF16), S((N_GROUP, CHUNK, CHUNK), F32), S((N_GROUP, CHUNK), F32),
                   S((1, D), F32), S((1, D), F32), S((D, D), F32), S((D, D), F32), S((D, D), F32)],
        scratch_shapes=[_SGU_W_BF16, _SGU_W_BF16, _SGU_BT, acc, acc, acc],
        args=(dx1b, proj, proj, pa, pb, ya, yb, mb, ln_g, ln_b, sgu_ws, sgu_bs, wpa, wpb, wout), sides=sides)


def _rglru_bwd(dya, dproj, proj, hseq, xc, gates, conv_w, rg_wa, rg_wx, lam, sides=()):
    T = dya.shape[0]
    tm = min(TM_SCAN, T)
    n = T // tm
    per8 = tm // SUBLANES

    def body(dya_ref, _, rx_ref, rxp_ref, gate_ref, h_ref, hp_ref, xc_ref, gates_ref, cw_ref, wah_ref, wxh_ref,
             lam_ref, dab_ref, dcw_ref, dcb_ref, dwah_ref, dwxh_ref, dba_ref, dbx_ref, dlam_ref,
             hext, rext, dext, carry_a, carry_dh, a_s, b_s, h_s, wa_ref, wx_ref, dwa_ref, dwx_ref):
        i = pl.program_id(0)
        first_tile = i == n - 1

        @pl.when(i == 0)
        def _():
            for ref in (dcw_ref, dcb_ref, dwa_ref, dwx_ref, dba_ref, dbx_ref, dlam_ref, carry_a, carry_dh):
                ref[...] = jnp.zeros_like(ref)
            dext[tm:tm + SUBLANES, :] = jnp.zeros((SUBLANES, D), F32)
            _fill_blockdiag(wah_ref, wa_ref)
            _fill_blockdiag(wxh_ref, wx_ref)

        gel, dgel = _gelu_and_grad(gate_ref[...].astype(F32))
        dya_v = dya_ref[...].astype(F32)
        hseq_v = h_ref[...]
        dgate = dya_v * hseq_v * dgel
        xcv = xc_ref[...]
        lam_v = lam_ref[...]
        sp = _softplus_neg(lam_v)
        xb = xcv.astype(BF16)
        r, gi, a, m = (gates_ref[:, D * k:D * (k + 1)] for k in range(N_GATES))

        row = lax.broadcasted_iota(jnp.int32, (tm, D), 0)
        c = jnp.where(row == tm - 1, carry_a[0:1, :], _shift_up(a, 1, 0.0))
        dH = _scan_tile(c, dya_v * gel, carry_dh[0:1, :], a_s, b_s, h_s, reverse=True)
        carry_a[...] = jnp.broadcast_to(a[0:1, :], (SUBLANES, D))
        carry_dh[...] = jnp.broadcast_to(dH[0:1, :], (SUBLANES, D))

        hext[0:SUBLANES, :] = jnp.where(first_tile, 0.0, hp_ref[...])
        hext[SUBLANES:SUBLANES + tm, :] = hseq_v
        h_prev = hext[pl.ds(SUBLANES - 1, tm), :]

        d_m = dH * (gi * xcv)
        d_la = dH * h_prev * a - d_m * (a * a) / m
        d_ia = dH * m * xcv * (gi * (1.0 - gi))
        d_ra = d_la * ((-RG_C) * sp) * (r * (1.0 - r))
        dlam_ref[...] += _colsum(d_la * ((-RG_C) * r)) * (-_sig(-lam_v))
        dba_ref[...] += _colsum(d_ra)
        dbx_ref[...] += _colsum(d_ia)
        drab = d_ra.astype(BF16)
        diab = d_ia.astype(BF16)
        dxc_cols = []
        for j in range(N_RG_BLOCK):
            sl = slice(RG_BLOCK * j, RG_BLOCK * (j + 1))
            dxc_cols.append(_dot_nt(drab[:, sl], wa_ref[j]) + _dot_nt(diab[:, sl], wx_ref[j]))
            dwa_ref[j] += _dot_tn(xb[:, sl], drab[:, sl])
            dwx_ref[j] += _dot_tn(xb[:, sl], diab[:, sl])
        dxc = dH * m * gi + jnp.concatenate(dxc_cols, axis=1)

        dcb_ref[...] += _colsum(dxc)
        dext[0:tm, :] = dxc
        rext[0:SUBLANES, :] = jnp.where(first_tile, 0.0, rxp_ref[SUBLANES:2 * SUBLANES, :].astype(F32))
        rext[SUBLANES:SUBLANES + tm, :] = rx_ref[...].astype(F32)
        drx = jnp.zeros((tm, D), F32)
        for k in range(CONV_WIDTH):
            drx = drx + dext[pl.ds(CONV_WIDTH - 1 - k, tm), :] * cw_ref[k:k + 1, :]
            dcw_ref[k:k + 1, :] += _colsum(dxc * rext[pl.ds(SUBLANES - (CONV_WIDTH - 1) + k, tm), :])
        dext[tm:tm + SUBLANES, :] = dext[0:SUBLANES, :]
        dab_ref[:, 0:D] = drx.astype(BF16)
        dab_ref[:, D:2 * D] = dgate.astype(BF16)

        @pl.when(first_tile)
        def _():
            for j in range(N_RG_BLOCK):
                for h in range(HEADS_PER_BLOCK):
                    sl = slice(HEAD_DIM * h, HEAD_DIM * (h + 1))
                    pair, side = divmod(HEADS_PER_BLOCK * j + h, 2)
                    lanes = slice(HEAD_DIM * side, HEAD_DIM * (side + 1))
                    dwah_ref[pair, :, lanes] = dwa_ref[j, sl, sl]
                    dwxh_ref[pair, :, lanes] = dwx_ref[j, sl, sl]

    def rev(col):
        return lambda i: (n - 1 - i, col)

    def prev8(col):
        return lambda i: (jnp.maximum((n - 1 - i) * per8 - 1, 0), col)

    def prev16(col):
        return lambda i: (jnp.maximum((n - 1 - i) * (per8 // 2) - 1, 0), col)

    tile = pl.BlockSpec((tm, D), rev(0))
    vec = pl.BlockSpec((1, D), lambda i: (0, 0))
    heads_in = pl.BlockSpec(rg_wa.shape, lambda i: (0, 0, 0, 0))
    head_pairs = (rg_wa.shape[1] // 2, HEAD_DIM, 2 * HEAD_DIM)
    heads_out = pl.BlockSpec(head_pairs, lambda i: (0, 0, 0))
    cw = pl.BlockSpec((CONV_WIDTH, D), lambda i: (0, 0))
    blocks_f32 = pltpu.VMEM((N_RG_BLOCK, RG_BLOCK, RG_BLOCK), F32)
    return _call(
        body, name="rglru_bwd", grid=(n,),
        in_specs=[tile, _ANY, pl.BlockSpec((tm, D), rev(0)), pl.BlockSpec((2 * SUBLANES, D), prev16(0)),
                  pl.BlockSpec((tm, D), rev(1)), tile, pl.BlockSpec((SUBLANES, D), prev8(0)), tile,
                  pl.BlockSpec((tm, N_GATES * D), rev(0)), cw, heads_in, heads_in, vec],
        out_specs=[pl.BlockSpec((tm, 2 * D), rev(2)), cw, vec, heads_out, heads_out, vec, vec, vec],
        out_shape=[S((T, IN_COLS), BF16), S((CONV_WIDTH, D), F32), S((1, D), F32),
                   S(head_pairs, F32), S(head_pairs, F32), S((1, D), F32), S((1, D), F32), S((1, D), F32)],
        scratch_shapes=[pltpu.VMEM((tm + SUBLANES, D), F32), pltpu.VMEM((tm + SUBLANES, D), F32),
                        pltpu.VMEM((tm + SUBLANES, D), F32), pltpu.VMEM((SUBLANES, D), F32),
                        pltpu.VMEM((SUBLANES, D), F32)] + [pltpu.VMEM((D // LANES, tm, LANES), F32)] * 3
        + [_RG_BLOCKS_BF16] * 2 + [blocks_f32] * 2,
        args=(dya, dproj, proj, proj, proj, hseq, hseq, xc, gates, conv_w, rg_wa, rg_wx, lam), sides=sides,
        aliases={1: 0})


def _inproj_dh(dproj, w_in, dh, first, count, name, sides=()):
    T = dproj.shape[0]
    tm = min(TM_MM, T)

    def body(*refs):
        dp_ref, w_ref, dh_ref = refs[-3:]
        dh = jnp.zeros((tm, D), F32)
        for p in range(N_PIECE):
            shard, part = divmod((p + DPROJ_ROTATION) % N_PIECE, IN_SHARD // PIECE)
            dh = dh + _dot_nt(dp_ref[:, PIECE * p:PIECE * (p + 1)], w_ref[shard, :, PIECE * part:PIECE * (part + 1)])
        dh_ref[...] = dh

    earlier = [] if dh is None else [dh]
    return _call(
        body, name=name, grid=(count,),
        in_specs=[_ANY] * len(earlier) + [pl.BlockSpec((tm, IN_COLS), lambda i: (first + i, 0)),
                                         _resident((N_SHARD, D, IN_SHARD))],
        out_specs=[pl.BlockSpec((tm, D), lambda i: (first + i, 0))],
        out_shape=[S((T, D), F32)],
        args=(*earlier, dproj, w_in), sides=sides, aliases={0: 0} if earlier else None)


def _inproj_norm_bwd(dh, x, g, dx1):
    T = x.shape[0]
    tm = min(TM_MM, T)

    def body(dh_ref, x_ref, g_ref, dx1_ref, dx_ref, dgm_ref):
        @pl.when(pl.program_id(0) == 0)
        def _():
            dgm_ref[...] = jnp.zeros_like(dgm_ref)

        dx, dg_rows = _rms_bwd(dh_ref[...], x_ref[...], g_ref[...])
        dgm_ref[...] += _colsum(dg_rows)
        dx_ref[...] = dx1_ref[...] + dx

    tile = pl.BlockSpec((tm, D), lambda i: (i, 0))
    vec = pl.BlockSpec((1, D), lambda i: (0, 0))
    return pl.pallas_call(
        body, name="inproj_norm_bwd", grid=(T // tm,),
        in_specs=[tile, tile, vec, tile],
        out_specs=[tile, vec],
        out_shape=[S((T, D), F32), S((1, D), F32)],
        compiler_params=_params(1),
    )(dh, x, g, dx1)


def _inproj_wgrad(h, dproj, sides=()):
    T = h.shape[0]
    tk = min(TK_WGRAD, T)
    per = IN_SHARD // PIECE

    def body(h_ref, *refs):
        pieces, o_ref = refs[:per], refs[per]

        @pl.when(pl.program_id(1) == 0)
        def _():
            o_ref[...] = jnp.zeros_like(o_ref)

        o_ref[0] += _dot_tn(h_ref[...], jnp.concatenate([p[...] for p in pieces], axis=1))

    def piece(i):
        return pl.BlockSpec((tk, PIECE), lambda j, k: (k, (per * j + i + N_PIECE - DPROJ_ROTATION) % N_PIECE))

    (out,), side_outs = _call(
        body, name="inproj_wgrad", grid=(N_SHARD, T // tk),
        in_specs=[pl.BlockSpec((tk, D), lambda j, k: (k, 0))] + [piece(i) for i in range(per)],
        out_specs=[pl.BlockSpec((1, D, IN_SHARD), lambda j, k: (j, 0, 0))],
        out_shape=[S((N_SHARD, D, IN_SHARD), F32)], args=(h,) + (dproj,) * per, sides=sides)
    return out, side_outs


def _row_tile(rows):
    for t in range(256, 0, -SUBLANES):
        if rows % t == 0:
            return t
    raise ValueError(rows)


def _add_halves(core, grads, theirs, name):
    n = len(grads)
    _, r, cols = grads[0].shape
    half = r // 2
    tr = _row_tile(half)
    nb = half // tr

    def body(core_ref, *refs):
        for g_ref, t_ref, o_ref in zip(refs[:n], refs[n:2 * n], refs[2 * n:]):
            o_ref[...] = (g_ref[...] + t_ref[...]).astype(BF16)

    blk = pl.BlockSpec((1, tr, cols), lambda s, i, core_ref: (s, i, 0))
    mine = pl.BlockSpec((1, tr, cols), lambda s, i, core_ref: (s, core_ref[0] * nb + i, 0))
    gs = pltpu.PrefetchScalarGridSpec(num_scalar_prefetch=1, grid=(N_SHARD, nb),
                                      in_specs=[mine] * n + [blk] * n, out_specs=[blk] * n)
    return pl.pallas_call(
        body, name=name, grid_spec=gs, out_shape=[S((N_SHARD, half, cols), BF16)] * n, compiler_params=_params(2),
    )(core, *grads, *theirs)


def _sum_shards(chip, owns, others, name):
    n = len(owns)
    _, half, cols = owns[0].shape
    tr = _row_tile(half)

    def body(chip_ref, *refs):
        for own_ref, oth_ref, o_ref in zip(refs[:n], refs[n:2 * n], refs[2 * n:]):
            acc = own_ref[0].astype(F32)
            for j in range(3):
                acc = acc + oth_ref[j].astype(F32)
            o_ref[...] = acc

    gs = pltpu.PrefetchScalarGridSpec(
        num_scalar_prefetch=1, grid=(half // tr,),
        in_specs=[pl.BlockSpec((1, tr, cols), lambda i, chip_ref: (chip_ref[0], i, 0))] * n
        + [pl.BlockSpec((3, tr, cols), lambda i, chip_ref: (0, i, 0))] * n,
        out_specs=[pl.BlockSpec((tr, cols), lambda i, chip_ref: (i, 0))] * n)
    return pl.pallas_call(
        body, name=name, grid_spec=gs, out_shape=[S((half, cols), F32)] * n, compiler_params=_params(1),
    )(chip, *owns, *others)


def _adamw(w, g, m, v):
    m = ADAM_B1 * m + (1.0 - ADAM_B1) * g
    v = ADAM_B2 * v + (1.0 - ADAM_B2) * (g * g)
    m_hat = m / (1.0 - ADAM_B1 ** ADAM_STEP)
    v_hat = v / (1.0 - ADAM_B2 ** ADAM_STEP)
    delta = -ADAM_LR * (m_hat / (jnp.sqrt(v_hat) + ADAM_EPS) + ADAM_WD * w)
    return delta, m, v


def _adamw_shard(core, mine, theirs, w, m, v, name):
    n = len(w)
    r, cols = w[0].shape
    half = r // 2
    tr = _row_tile(half)
    nb = half // tr

    def body(core_ref, *refs):
        groups = [refs[k * n:(k + 1) * n] for k in range(9)]
        for mine_ref, theirs_ref, w_ref, m_ref, v_ref, g_ref, d_ref, mo_ref, vo_ref in zip(*groups):
            g = jnp.where(pl.program_id(0) == core_ref[0], mine_ref[...], theirs_ref[...])
            g_ref[...] = g
            d_ref[...], mo_ref[...], vo_ref[...] = _adamw(w_ref[...], g, m_ref[...], v_ref[...])

    hblk = pl.BlockSpec((tr, cols), lambda h, i, core_ref: (i, 0))
    blk = pl.BlockSpec((tr, cols), lambda h, i, core_ref: (h * nb + i, 0))
    gs = pltpu.PrefetchScalarGridSpec(num_scalar_prefetch=1, grid=(2, nb),
                                      in_specs=[hblk] * (2 * n) + [blk] * (3 * n), out_specs=[blk] * (4 * n))
    res = pl.pallas_call(
        body, name=name, grid_spec=gs, out_shape=[S((r, cols), F32)] * (4 * n), compiler_params=_params(2),
    )(core, *mine, *theirs, *w, *m, *v)
    return [tuple(res[k * n + j] for k in range(4)) for j in range(n)]


def _adamw_whole(w, g, m, v, name):
    def body(w_ref, g_ref, m_ref, v_ref, d_ref, mo_ref, vo_ref):
        d_ref[...], mo_ref[...], vo_ref[...] = _adamw(w_ref[...], g_ref[...], m_ref[...], v_ref[...])

    return pl.pallas_call(body, name=name, out_shape=[S(w.shape, F32)] * 3)(w, g, m, v)


_VEC_ROWS = ("norm_mix_g", "conv_b", "rg_lambda", "sgu_ln_g", "sgu_ln_b", "norm_ffn_g", "norm_final_g", "rg_ba",
             "rg_bx")
_CONV_ROW = len(_VEC_ROWS)
_LOSS_ROW = _CONV_ROW + CONV_WIDTH
_VEC_PAD = -(_LOSS_ROW + 1) % SUBLANES
_HEAD_BIASES = ("rg_ba", "rg_bx")
_TENSORS = ("sgu_bs", "sgu_ws", "rg_wa", "rg_wx")
_HEAD_PAIRS = ("rg_wa", "rg_wx")


def _small_sum_adamw(parts, w, m, v):
    names = [n for n in _VEC_ROWS] + list(_TENSORS)
    n_parts = len(parts)

    def total(ref):
        acc = ref[0]
        for k in range(1, N_DEVICE):
            acc = acc + ref[k]
        return acc

    def body(*refs):
        part_refs, refs = refs[:n_parts], refs[n_parts:]
        w_refs, m_refs, v_refs = (dict(zip(names, refs[k * len(names):(k + 1) * len(names)])) for k in range(3))
        outs = refs[3 * len(names):]
        out_refs = {n: outs[4 * k:4 * k + 4] for k, n in enumerate(names)}
        conv_ref, loss_ref = outs[4 * len(names):]
        vec = total(part_refs[0])
        grads = {n: total(p) for n, p in zip(_TENSORS, part_refs[1:])}
        for n in _HEAD_PAIRS:
            pairs = grads[n]
            grads[n] = jnp.stack([pairs[k // 2, :, HEAD_DIM * (k % 2):HEAD_DIM * (k % 2 + 1)]
                                  for k in range(2 * pairs.shape[0])], axis=0)
        grads = {n: g[None] for n, g in grads.items()}
        for row, n in enumerate(_VEC_ROWS):
            g = vec[row:row + 1, :]
            if n in _HEAD_BIASES:
                g = jnp.concatenate([g[:, HEAD_DIM * h:HEAD_DIM * (h + 1)] for h in range(D // HEAD_DIM)], axis=0)[None]
            grads[n] = g
        for n in names:
            g_ref, d_ref, mo_ref, vo_ref = out_refs[n]
            g_ref[...] = grads[n]
            d_ref[...], mo_ref[...], vo_ref[...] = _adamw(w_refs[n][...], grads[n], m_refs[n][...], v_refs[n][...])
        conv_ref[...] = vec[_CONV_ROW:_CONV_ROW + CONV_WIDTH, :]
        loss_ref[...] = vec[_LOSS_ROW:_LOSS_ROW + 1, 0:1]

    res = pl.pallas_call(
        body, name="small_sum_adamw",
        out_shape=[S(w[n].shape, F32) for n in names for _ in range(4)] + [S((CONV_WIDTH, D), F32), S((1, 1), F32)],
        compiler_params=pltpu.CompilerParams(vmem_limit_bytes=VMEM_LIMIT),
    )(*parts, *[w[n] for n in names], *[m[n] for n in names], *[v[n] for n in names])
    return {n: tuple(res[4 * k:4 * k + 4]) for k, n in enumerate(names)}, res[-2], res[-1]


_BIG = ("w_in", "w_proj_a", "w_proj_b", "w_out", "w_gate_up", "w_down")
_WEIGHTS = ("norm_mix_g", "w_in", "conv_w", "conv_b", "rg_wa", "rg_ba", "rg_wx", "rg_bx", "rg_lambda", "sgu_ln_g",
            "sgu_ln_b", "sgu_ws", "sgu_bs", "w_proj_a", "w_proj_b", "w_out", "norm_ffn_g", "w_gate_up", "w_down",
            "norm_final_g")


def kernel(x, norm_mix_g, w_in, conv_w, conv_b, rg_wa, rg_ba, rg_wx, rg_bx, rg_lambda, sgu_ln_g, sgu_ln_b, sgu_ws, sgu_bs, w_proj_a, w_proj_b, w_out, norm_ffn_g, w_gate_up, w_down, norm_final_g, loss_target, m_norm_mix_g, m_w_in, m_conv_w, m_conv_b, m_rg_wa, m_rg_ba, m_rg_wx, m_rg_bx, m_rg_lambda, m_sgu_ln_g, m_sgu_ln_b, m_sgu_ws, m_sgu_bs, m_w_proj_a, m_w_proj_b, m_w_out, m_norm_ffn_g, m_w_gate_up, m_w_down, m_norm_final_g, v_norm_mix_g, v_w_in, v_conv_w, v_conv_b, v_rg_wa, v_rg_ba, v_rg_wx, v_rg_bx, v_rg_lambda, v_sgu_ln_g, v_sgu_ln_b, v_sgu_ws, v_sgu_bs, v_w_proj_a, v_w_proj_b, v_w_out, v_norm_ffn_g, v_w_gate_up, v_w_down, v_norm_final_g):
    args = dict(locals())
    w = {n: args[n] for n in _WEIGHTS}
    mom = {n: args["m_" + n] for n in _WEIGHTS}
    var = {n: args["v_" + n] for n in _WEIGHTS}
    xi, yi, ci = _position()
    core = ci.astype(jnp.int32).reshape(1)
    chip = (2 * xi + yi).astype(jnp.int32).reshape(1)

    bf = {n: w[n][0].astype(BF16) for n in _BIG}
    final_g = w["norm_final_g"].reshape(1, D)
    ba, bx = w["rg_ba"].reshape(1, D), w["rg_bx"].reshape(1, D)
    lam, ln_g, ln_b = w["rg_lambda"], w["sgu_ln_g"], w["sgu_ln_b"]
    x0, target = x[0], loss_target[0]

    def shard_major(g):
        return g.reshape(N_SHARD, g.shape[0] // N_SHARD, g.shape[1])

    def per_shape(names, fn, *lists):
        if len({a.shape for a in lists[0]}) == 1:
            return fn("_".join(names), *lists)
        return [r for k, n in enumerate(names) for r in fn(n, *[[a[k]] for a in lists])]

    def chip_sums(names, grads, theirs):
        return per_shape(names, lambda s, g, t: _add_halves(core, g, t, "add_halves_" + s), grads, theirs)

    def my_halves(names, sums, arrived):
        return per_shape(names, lambda s, p, a: _sum_shards(chip, p, a, "sum_shards_" + s), sums, arrived)

    (proj, h), ((w_in_a,), (conv_a,)) = _inproj_own(
        chip, x0, w["norm_mix_g"], bf["w_in"],
        sides=[_gather_half_side([bf["w_in"]], relations=(0, 1)), _gather_side([w["conv_w"][0]])])
    conv_cols = conv_a.shape[-1]
    conv_full = jnp.swapaxes(conv_a, 0, 1).reshape(CONV_WIDTH, D)
    (proj,), ((w_in_a,),) = _inproj_rest(
        chip, proj, h, w_in_a, 1, 2, "inproj_near",
        sides=[_gather_half_side([bf["w_in"]], relations=(2,), into=[w_in_a])])
    (proj,), _ = _inproj_rest(chip, proj, h, w_in_a, 3, 1, "inproj_far")
    (ya, xc, hseq, gates), ((w_pa_a, w_pb_a, w_out_a, w_gu_a, w_down_a),) = _rglru_fwd(
        proj, conv_full, w["conv_b"], w["rg_wa"], w["rg_wx"], ba, bx, lam,
        sides=[_gather_half_side([bf[n] for n in ("w_proj_a", "w_proj_b", "w_out", "w_gate_up", "w_down")])])
    wpa, wpb, wout, wdown = w_pa_a.reshape(D, D), w_pb_a.reshape(D, D), w_out_a.reshape(D, D), w_down_a.reshape(D_FF, D)
    yb, pa, pb, mb, x1 = _sgu_merge_fwd(x0, proj, ya, ln_g, ln_b, w["sgu_ws"], w["sgu_bs"], wpa, wpb, wout)
    h2, gu, act, dx2, dx2b, loss, d_final_g = _ffn_fwd_loss(x1, w["norm_ffn_g"], w_gu_a, wdown, final_g, target)

    dgu, dx1, dx1b, d_ffn_g = _ffn_bwd(dx2, dx2b, gu, wdown, w_gu_a, x1, w["norm_ffn_g"])
    ffn = ("w_gate_up", "w_down")
    g_ffn = [_matmul_tn(h2, dgu, FF_SHARD, True, "wgrad_gate_up")[0],
             shard_major(_matmul_tn(act, dx2b, D // 2, False, "wgrad_down")[0])]
    (dya, dproj, d_ws, d_bs, d_lng, d_lnb, g_pa, g_pb, g_out), (theirs_ffn,) = _merge_sgu_bwd(
        dx1b, proj, pa, pb, ya, yb, mb, ln_g, ln_b, w["sgu_ws"], w["sgu_bs"], wpa, wpb, wout,
        sides=[_halves_side(g_ffn)])
    sums_ffn = chip_sums(ffn, g_ffn, theirs_ffn)
    mix = ("w_proj_a", "w_proj_b", "w_out")
    g_mix = [shard_major(g) for g in (g_pa, g_pb, g_out)]
    (dproj, d_cw, d_cb, d_wa, d_wx, d_ba, d_bx, d_lam), (arrived_ffn, theirs_mix, sgu_parts) = _rglru_bwd(
        dya, dproj, proj, hseq, xc, gates, conv_full, w["rg_wa"], w["rg_wx"], lam,
        sides=[_scatter_side(sums_ffn), _halves_side(g_mix), _everyone_side([d_bs, d_ws])])
    mine_ffn = my_halves(ffn, sums_ffn, arrived_ffn)
    sums_mix = chip_sums(mix, g_mix, theirs_mix)
    g_in, ((wa_parts, wx_parts), other_ffn, arrived_mix) = _inproj_wgrad(
        h, dproj, sides=[_everyone_side([d_wa, d_wx]), _swap_side(mine_ffn),
                         _scatter_side(sums_mix)])
    mine_mix = my_halves(mix, sums_mix, arrived_mix)
    n_tiles = x0.shape[0] // min(TM_MM, x0.shape[0])
    n_first = max(1, n_tiles * 3 // 8)
    (dh,), (theirs_in, other_mix) = _inproj_dh(dproj, w_in_a, None, 0, n_first, "inproj_dh_a",
                                               sides=[_halves_side([g_in]), _swap_side(mine_mix)])
    sums_in = chip_sums(("w_in",), [g_in], theirs_in)
    (dh,), (arrived_in,) = _inproj_dh(dproj, w_in_a, dh, n_first, n_tiles - n_first, "inproj_dh_b",
                                      sides=[_scatter_side(sums_in)])
    mine_in = my_halves(("w_in",), sums_in, arrived_in)
    grad_x, d_mix_g = _inproj_norm_bwd(dh, x0, w["norm_mix_g"], dx1)
    rows = {"norm_mix_g": d_mix_g, "conv_b": d_cb, "rg_lambda": d_lam, "sgu_ln_g": d_lng, "sgu_ln_b": d_lnb,
            "norm_ffn_g": d_ffn_g, "norm_final_g": d_final_g, "rg_ba": d_ba, "rg_bx": d_bx}
    vec = jnp.concatenate([rows[n] for n in _VEC_ROWS]
                          + [d_cw, jnp.pad(loss, ((0, _VEC_PAD), (0, D - 1)))], axis=0)
    other_in, (vec_parts,) = _comm_only([_swap_side(mine_in), _everyone_side([vec])], "swap_w_in")
    small_parts = [vec_parts] + sgu_parts + [wa_parts, wx_parts]

    out = {}
    for names, gm, go in ((ffn, mine_ffn, other_ffn), (mix, mine_mix, other_mix), (("w_in",), mine_in, other_in)):
        results = per_shape(
            names, lambda s, *lists: _adamw_shard(core, *lists, "adamw_" + s),
            gm, go, [w[n][0] for n in names], [mom[n][0] for n in names], [var[n][0] for n in names])
        for n, res in zip(names, results):
            out[n] = tuple(a[None] for a in res)
    as_row = lambda t: {n: a.reshape(1, D) if n == "norm_final_g" else a for n, a in t.items()}
    small_out, conv_sum, loss_sum = _small_sum_adamw(small_parts, as_row(w), as_row(mom), as_row(var))
    out.update(small_out)
    out["norm_final_g"] = tuple(a.reshape(D) for a in small_out["norm_final_g"])
    conv_g = lax.dynamic_slice_in_dim(conv_sum, chip[0] * conv_cols, conv_cols, axis=1)
    d, mo, vo = _adamw_whole(w["conv_w"][0], conv_g, mom["conv_w"][0], var["conv_w"][0], "adamw_conv_w")
    out["conv_w"] = tuple(a[None] for a in (conv_g, d, mo, vo))

    return (loss_sum[0, 0], grad_x[None], *[out[n][0] for n in _WEIGHTS], *[out[n][1] for n in _WEIGHTS],
            *[out[n][2] for n in _WEIGHTS], *[out[n][3] for n in _WEIGHTS])
```

```python
import functools

import jax
import jax.numpy as jnp
from jax import lax
from jax.experimental import pallas as pl
from jax.experimental.pallas import tpu as pltpu

F32 = jnp.float32
BF16 = jnp.bfloat16
S = jax.ShapeDtypeStruct

D = 1024
N_SHARD = 4
IN_COLS = 6 * D
IN_SHARD = IN_COLS // N_SHARD
D_FF = 2816
FF_SHARD = 2 * D_FF // N_SHARD
RG_BLOCK = 256
N_RG_BLOCK = D // RG_BLOCK
CHUNK = 128
N_GROUP = 8
CONV_WIDTH = 4
RG_C = 8.0
EPS = 1e-6
ADAM_LR, ADAM_B1, ADAM_B2, ADAM_EPS, ADAM_WD, ADAM_STEP = 0.001, 0.9, 0.999, 1e-08, 0.01, 10

V7X_VMEM_BYTES = 64 * 1024 * 1024
VMEM_LIMIT = V7X_VMEM_BYTES * 3 // 4
SUBLANES = 8
MESH = pl.DeviceIdType.MESH

TM_MM = 512
TM_SCAN = 256
TM_FF = 256
TK_WGRAD = 2048


def _params(n_axes):
    return pltpu.CompilerParams(dimension_semantics=("arbitrary",) * n_axes, vmem_limit_bytes=VMEM_LIMIT)


def _resident(shape):
    nd = len(shape)
    return pl.BlockSpec(shape, lambda *_: (0,) * nd, pipeline_mode=pl.Buffered(1))


def _sig(x):
    return 1.0 / (1.0 + jnp.exp(-x))


_GELU_K2 = 2.0 * 0.7978845608028654
_GELU_C = 0.044715


def _gelu(x):
    return x * _sig(x * (_GELU_K2 + (_GELU_K2 * _GELU_C) * (x * x)))


def _gelu_and_grad(x):
    x2 = x * x
    s = _sig(x * (_GELU_K2 + (_GELU_K2 * _GELU_C) * x2))
    g = x * s
    return g, s + g * (1.0 - s) * (_GELU_K2 + (3.0 * _GELU_K2 * _GELU_C) * x2)


_EXPM1_SERIES = tuple(1.0 / f for f in (5040.0, 720.0, 120.0, 24.0, 6.0, 2.0, 1.0))


def _one_minus_exp(x):
    p = _EXPM1_SERIES[0]
    for coef in _EXPM1_SERIES[1:]:
        p = p * x + coef
    return jnp.where(x > -0.125, -x * p, 1.0 - jnp.exp(x))


def _softplus_neg(lam):
    z = -lam
    e = jnp.exp(-jnp.abs(z))
    u = 1.0 + e
    log1p = jnp.where(u == 1.0, e, jnp.log(u) * e / (u - 1.0))
    return jnp.maximum(z, 0.0) + log1p


def _rms_stats(x):
    return lax.rsqrt(jnp.mean(x * x, axis=-1, keepdims=True) + EPS)


def _rms_bwd(dy, x, g):
    rstd = _rms_stats(x)
    xhat = x * rstd
    dxhat = dy * g
    dx = rstd * (dxhat - xhat * jnp.mean(dxhat * xhat, axis=-1, keepdims=True))
    return dx, dy * xhat


def _colsum(x):
    return jnp.sum(x, axis=0, keepdims=True)


def _shift_down(x, d, fill):
    n = x.shape[0]
    if d % SUBLANES == 0:
        return jnp.concatenate([jnp.full((d, x.shape[1]), fill, x.dtype), x[:n - d]], axis=0)
    row = lax.broadcasted_iota(jnp.int32, x.shape, 0)
    return jnp.where(row < d, fill, pltpu.roll(x, d, 0))


def _shift_up(x, d, fill):
    n = x.shape[0]
    if d % SUBLANES == 0:
        return jnp.concatenate([x[d:], jnp.full((d, x.shape[1]), fill, x.dtype)], axis=0)
    row = lax.broadcasted_iota(jnp.int32, x.shape, 0)
    return jnp.where(row >= n - d, fill, pltpu.roll(x, n - d, 0))


def _scan(a, b, shift):
    d = 1
    while d < a.shape[0]:
        b = a * shift(b, d, 0.0) + b
        a = a * shift(a, d, 1.0)
        d *= 2
    return a, b


LANES = 128


def _scan_tile(a, b, outside, a_s, b_s, h_s, reverse):
    tm = a.shape[0]
    groups = tm // SUBLANES
    order = list(range(SUBLANES - 1, -1, -1) if reverse else range(SUBLANES))
    shift = _shift_up if reverse else _shift_down
    edge = groups - 1 if reverse else 0
    for j in range(D // LANES):
        a_s[j] = a[:, LANES * j:LANES * (j + 1)]
        b_s[j] = b[:, LANES * j:LANES * (j + 1)]
    for j in range(D // LANES):
        def slab(ref, k):
            return ref[j, pl.ds(k, groups, stride=SUBLANES), :]

        ga, gb = slab(a_s, order[0]), slab(b_s, order[0])
        for k in order[1:]:
            ak = slab(a_s, k)
            gb = ak * gb + slab(b_s, k)
            ga = ak * ga
        ga, gb = _scan(ga, gb, shift)
        h_out = outside[:, LANES * j:LANES * (j + 1)]
        group_end = ga * h_out + gb
        row = lax.broadcasted_iota(jnp.int32, (groups, LANES), 0)
        h = jnp.where(row == edge, h_out, shift(group_end, 1, 0.0))
        for k in order:
            h = slab(a_s, k) * h + slab(b_s, k)
            h_s[j, pl.ds(k, groups, stride=SUBLANES), :] = h
    return jnp.concatenate([h_s[j] for j in range(D // LANES)], axis=1)


def _dot(a, b):
    return jnp.dot(a, b, preferred_element_type=F32)


def _dot_nt(a, b):
    return lax.dot_general(a, b, (((1,), (1,)), ((), ())), preferred_element_type=F32)


def _dot_tn(a, b):
    return lax.dot_general(a, b, (((0,), (0,)), ((), ())), preferred_element_type=F32)


_ANY = pl.BlockSpec(memory_space=pl.ANY)


def _position():
    return lax.axis_index("x"), lax.axis_index("y"), lax.axis_index("c")


def _other_chips(x, y):
    return [(1 - x, y), (x, 1 - y), (1 - x, 1 - y)]


class _Side:
    def __init__(self, inputs, out_shapes, n_sems, make, continues=()):
        self.inputs, self.out_shapes, self.n_sems, self.make = list(inputs), list(out_shapes), n_sems, make
        self.continues = list(continues)


MID_STEP = 0.625
LATE_MID_STEP = 0.875


def _call(body, *, name, grid, in_specs, out_specs, out_shape, args, scratch_shapes=(), sides=(), aliases=None,
          scalars=None, mid=MID_STEP):
    n_in, n_out, n_scr = len(in_specs), len(out_specs), len(scratch_shapes)
    n_scalar = 0 if scalars is None else 1
    side_in = [len(s.inputs) + len(s.continues) for s in sides]
    side_out = [len(s.out_shapes) for s in sides]
    all_aliases = {k + n_scalar: v for k, v in (aliases or {}).items()}
    for idx, s in enumerate(sides):
        for k in range(len(s.continues)):
            operand = n_scalar + n_in + sum(side_in[:idx]) + len(s.inputs) + k
            all_aliases[operand] = n_out + sum(side_out[:idx]) + k

    def wrapped(*refs):
        refs = list(refs)
        take = lambda k: [refs.pop(0) for _ in range(k)]
        ins = take(n_scalar) + take(n_in)
        sins = [take(k) for k in side_in]
        outs = take(n_out)
        souts = [take(k) for k in side_out]
        scr = take(n_scr)
        sems = [take(3) for _ in sides]
        def run(phase):
            for s, si, so, sem in zip(sides, sins, souts, sems):
                for thunk in s.make(si[:len(s.inputs)], so, *sem)[phase]:
                    thunk()

        if sides:
            n_steps = functools.reduce(lambda a, b: a * b, grid)
            step = functools.reduce(lambda a, b: a + b, [
                pl.program_id(a) * functools.reduce(lambda p, q: p * q, grid[a + 1:], 1) for a in range(len(grid))])
            pl.when(step == 0)(lambda: run(0))
        body(*ins, *outs, *scr)
        if sides:
            pl.when(step == int(mid * (n_steps - 1)))(lambda: run(1))
            pl.when(step == n_steps - 1)(lambda: run(2))

    grid_spec = pltpu.PrefetchScalarGridSpec(
        num_scalar_prefetch=n_scalar, grid=grid,
        in_specs=list(in_specs) + [_ANY] * sum(side_in),
        out_specs=list(out_specs) + [_ANY] * sum(side_out),
        scratch_shapes=list(scratch_shapes) + [pltpu.SemaphoreType.DMA((s.n_sems,)) for s in sides for _ in range(3)])
    res = pl.pallas_call(
        wrapped, name=name, grid_spec=grid_spec,
        out_shape=list(out_shape) + [o for s in sides for o in s.out_shapes],
        input_output_aliases=all_aliases,
        compiler_params=_params(len(grid)),
    )(*([scalars] if n_scalar else []), *args, *[a for s in sides for a in s.inputs + s.continues])
    main, rest, per_side = list(res[:n_out]), list(res[n_out:]), []
    for k in side_out:
        per_side.append(rest[:k])
        rest = rest[k:]
    return main, per_side


def _comm_only(sides, name):
    def body():
        pass

    return _call(body, name=name, grid=(1,), in_specs=[], out_specs=[], out_shape=[], args=[], sides=sides)[1]


def _remote(src, dst, send, recv, k, device):
    return pltpu.make_async_remote_copy(src_ref=src, dst_ref=dst, send_sem=send.at[k], recv_sem=recv.at[k],
                                        device_id=device, device_id_type=MESH)


def _both_ways(copy, keys):
    return [lambda k=k: copy(k).start() for k in keys], [], [lambda k=k: copy(k).wait() for k in keys]


def _gather_side(shards):
    n = len(shards)

    def make(ins, outs, send, recv, local):
        x, y, c = _position()
        mine = 2 * x + y
        chips = _other_chips(x, y)
        pairs = [(w, j) for w in range(n) for j in range(3)]

        def own(w):
            return pltpu.make_async_copy(ins[w], outs[w].at[mine], local.at[w])

        def push(w, j):
            return _remote(ins[w], outs[w].at[mine], send, recv, 3 * w + j, (*chips[j], c))

        def arrival(w, j):
            px, py = chips[j]
            return _remote(ins[w], outs[w].at[2 * px + py], send, recv, 3 * w + j, (px, py, c))

        starts = [lambda w=w: own(w).start() for w in range(n)] + [lambda w=w, j=j: push(w, j).start() for w, j in pairs]
        waits = ([lambda w=w, j=j: arrival(w, j).wait_recv() for w, j in pairs]
                 + [lambda w=w, j=j: push(w, j).wait_send() for w, j in pairs]
                 + [lambda w=w: own(w).wait() for w in range(n)])
        return starts, [], waits

    return _Side(shards, [S((N_SHARD,) + s.shape, s.dtype) for s in shards], 3 * n, make)


def _gather_half_side(shards, relations=(0, 1, 2), into=None):
    n = len(shards)

    def make(ins, outs, send, recv, local):
        x, y, c = _position()
        mine = 2 * x + y
        chips = _other_chips(x, y)
        pairs = [(w, j) for w in range(n) for j in relations]

        def rows(w, core):
            half = ins[w].shape[0] // 2
            return pl.ds(core * half, half)

        def own(w):
            return pltpu.make_async_copy(ins[w], outs[w].at[mine], local.at[w])

        def push(w, j):
            return _remote(ins[w].at[rows(w, c), :], outs[w].at[mine, rows(w, c), :], send, recv, 3 * w + j,
                           (*chips[j], c))

        def landed(w, j, core):
            px, py = chips[j]
            return outs[w].at[2 * px + py, rows(w, core), :]

        def arrival(w, j):
            return _remote(ins[w].at[rows(w, c), :], landed(w, j, c), send, recv, 3 * w + j, (*chips[j], c))

        def passed(w, j, core):
            return _remote(landed(w, j, core), landed(w, j, core), send, recv, 3 * n + 3 * w + j, (x, y, 1 - c))

        owns = range(n) if into is None else ()
        starts = [lambda w=w: own(w).start() for w in owns] + [lambda w=w, j=j: push(w, j).start() for w, j in pairs]
        mids = [t for w, j in pairs for t in (lambda w=w, j=j: arrival(w, j).wait_recv(),
                                              lambda w=w, j=j: passed(w, j, c).start())]
        waits = ([lambda w=w, j=j: passed(w, j, 1 - c).wait_recv() for w, j in pairs]
                 + [lambda w=w, j=j: passed(w, j, c).wait_send() for w, j in pairs]
                 + [lambda w=w, j=j: push(w, j).wait_send() for w, j in pairs]
                 + [lambda w=w: own(w).wait() for w in owns])
        return starts, mids, waits

    return _Side(shards, [S((N_SHARD,) + s.shape, s.dtype) for s in shards], 6 * n, make, continues=into or ())


def _halves_side(grads):
    n = len(grads)

    def make(ins, outs, send, recv, local):
        x, y, c = _position()

        def copy(w):
            half = ins[w].shape[1] // 2
            return _remote(ins[w].at[:, pl.ds((1 - c) * half, half), :], outs[w], send, recv, w, (x, y, 1 - c))

        return _both_ways(copy, range(n))

    return _Side(grads, [S((N_SHARD, g.shape[1] // 2, g.shape[2]), F32) for g in grads], n, make)


def _scatter_side(partials):
    n = len(partials)

    def make(ins, outs, send, recv, local):
        x, y, c = _position()
        chips = _other_chips(x, y)

        def copy(k):
            w, j = divmod(k, 3)
            px, py = chips[j]
            return _remote(ins[w].at[2 * px + py], outs[w].at[j], send, recv, k, (px, py, c))

        return _both_ways(copy, range(3 * n))

    return _Side(partials, [S((3,) + p.shape[1:], p.dtype) for p in partials], 3 * n, make)


def _swap_side(halves):
    n = len(halves)

    def make(ins, outs, send, recv, local):
        x, y, c = _position()
        return _both_ways(lambda w: _remote(ins[w], outs[w], send, recv, w, (x, y, 1 - c)), range(n))

    return _Side(halves, [S(h.shape, h.dtype) for h in halves], n, make)


N_DEVICE = 8


def _everyone_side(arrays):
    n = len(arrays)
    peers = N_DEVICE - 1

    def make(ins, outs, send, recv, local):
        x, y, c = _position()
        mine = 4 * x + 2 * y + c
        pairs = [(w, k) for w in range(n) for k in range(1, N_DEVICE)]

        def peer(k):
            return (1 - x if k & 4 else x, 1 - y if k & 2 else y, 1 - c if k & 1 else c)

        def own(w):
            return pltpu.make_async_copy(ins[w], outs[w].at[mine], local.at[w])

        def push(w, k):
            return _remote(ins[w], outs[w].at[mine], send, recv, peers * w + k - 1, peer(k))

        def arrival(w, k):
            px, py, pc = peer(k)
            return _remote(ins[w], outs[w].at[4 * px + 2 * py + pc], send, recv, peers * w + k - 1, (px, py, pc))

        starts = [lambda w=w: own(w).start() for w in range(n)] + [lambda w=w, k=k: push(w, k).start() for w, k in pairs]
        waits = ([lambda w=w, k=k: arrival(w, k).wait_recv() for w, k in pairs]
                 + [lambda w=w, k=k: push(w, k).wait_send() for w, k in pairs]
                 + [lambda w=w: own(w).wait() for w in range(n)])
        return starts, [], waits

    return _Side(arrays, [S((N_DEVICE,) + a.shape, a.dtype) for a in arrays], peers * n, make)


def _inproj_own(chip, x, g, w_shard, sides=()):
    T = x.shape[0]
    tm = min(TM_MM, T)

    def body(chip_ref, x_ref, g_ref, w_ref, proj_ref, h_ref):
        xv = x_ref[...]
        h = (xv * _rms_stats(xv) * g_ref[...]).astype(BF16)
        h_ref[...] = h
        proj_ref[...] = _dot(h, w_ref[...]).astype(BF16)

    return _call(
        body, name="inproj_own", grid=(T // tm,),
        in_specs=[pl.BlockSpec((tm, D), lambda i, c: (i, 0)), pl.BlockSpec((1, D), lambda i, c: (0, 0)),
                  pl.BlockSpec((D, IN_SHARD), lambda i, c: (0, 0), pipeline_mode=pl.Buffered(1))],
        out_specs=[pl.BlockSpec((tm, IN_SHARD), lambda i, c: (i, c[0])), pl.BlockSpec((tm, D), lambda i, c: (i, 0))],
        out_shape=[S((T, IN_COLS), BF16), S((T, D), BF16)],
        args=(x, g, w_shard), sides=sides, scalars=chip, mid=LATE_MID_STEP)


def _inproj_rest(chip, proj, h, w_in, first, count, name, sides=(), mid=MID_STEP):
    T = h.shape[0]
    tm = min(TM_MM, T)

    def body(chip_ref, _, h_ref, w_ref, proj_ref):
        proj_ref[...] = _dot(h_ref[...], w_ref[0]).astype(BF16)

    def other(p, c):
        return jnp.bitwise_xor(c[0], first + p)

    return _call(
        body, name=name, grid=(count, T // tm),
        in_specs=[_ANY, pl.BlockSpec((tm, D), lambda p, i, c: (i, 0)),
                  pl.BlockSpec((1, D, IN_SHARD), lambda p, i, c: (other(p, c), 0, 0))],
        out_specs=[pl.BlockSpec((tm, IN_SHARD), lambda p, i, c: (i, other(p, c)))],
        out_shape=[S((T, IN_COLS), BF16)],
        args=(proj, h, w_in), sides=sides, scalars=chip, aliases={0: 0}, mid=mid)


def _rg_gates(xc, wa_ref, wx_ref, ba, bx, sp):
    xb = xc.astype(BF16)
    blocks = [xb[:, RG_BLOCK * j:RG_BLOCK * (j + 1)] for j in range(N_RG_BLOCK)]
    r = _sig(jnp.concatenate([_dot(blocks[j], wa_ref[j]) for j in range(N_RG_BLOCK)], axis=1) + ba)
    gi = _sig(jnp.concatenate([_dot(blocks[j], wx_ref[j]) for j in range(N_RG_BLOCK)], axis=1) + bx)
    log_a = (-RG_C) * r * sp
    a = jnp.exp(log_a)
    m = jnp.sqrt(_one_minus_exp(2.0 * log_a))
    return xb, r, gi, a, m


N_GATES = 4
HEADS_PER_BLOCK = 4
HEAD_DIM = RG_BLOCK // HEADS_PER_BLOCK
_RG_BLOCKS_BF16 = pltpu.VMEM((N_RG_BLOCK, RG_BLOCK, RG_BLOCK), BF16)


def _fill_blockdiag(heads_ref, blocks):
    blocks[...] = jnp.zeros_like(blocks)
    for j in range(N_RG_BLOCK):
        for h in range(HEADS_PER_BLOCK):
            sl = slice(HEAD_DIM * h, HEAD_DIM * (h + 1))
            blocks[j, sl, sl] = heads_ref[0, HEADS_PER_BLOCK * j + h].astype(BF16)


def _rglru_fwd(proj, conv_w, conv_b, rg_wa, rg_wx, ba, bx, lam, sides=()):
    T = proj.shape[0]
    tm = min(TM_SCAN, T)

    def body(rx_ref, gate_ref, cw_ref, cb_ref, wah_ref, wxh_ref, ba_ref, bx_ref, lam_ref,
             ya_ref, xc_ref, h_ref, gates_ref, ext, hc, a_s, b_s, h_s, wa_ref, wx_ref):
        @pl.when(pl.program_id(0) == 0)
        def _():
            ext[0:SUBLANES, :] = jnp.zeros((SUBLANES, D), F32)
            hc[...] = jnp.zeros((SUBLANES, D), F32)
            _fill_blockdiag(wah_ref, wa_ref)
            _fill_blockdiag(wxh_ref, wx_ref)

        ext[SUBLANES:SUBLANES + tm, :] = rx_ref[...].astype(F32)
        xc = cb_ref[...]
        for k in range(CONV_WIDTH):
            xc = xc + ext[pl.ds(SUBLANES - (CONV_WIDTH - 1) + k, tm), :] * cw_ref[k:k + 1, :]
        ext[0:SUBLANES, :] = ext[tm:tm + SUBLANES, :]
        xc_ref[...] = xc
        _, r, gi, a, m = _rg_gates(xc, wa_ref, wx_ref, ba_ref[...], bx_ref[...], _softplus_neg(lam_ref[...]))
        for k, val in enumerate((r, gi, a, m)):
            gates_ref[:, D * k:D * (k + 1)] = val
        h = _scan_tile(a, m * (gi * xc), hc[0:1, :], a_s, b_s, h_s, reverse=False)
        hc[...] = jnp.broadcast_to(h[tm - 1:tm, :], (SUBLANES, D))
        h_ref[...] = h
        ya_ref[...] = (_gelu(gate_ref[...].astype(F32)) * h).astype(BF16)

    vec = pl.BlockSpec((1, D), lambda i: (0, 0))
    heads = pl.BlockSpec(rg_wa.shape, lambda i: (0, 0, 0, 0))
    tile = pl.BlockSpec((tm, D), lambda i: (i, 0))
    return _call(
        body, name="rglru_fwd", grid=(T // tm,),
        in_specs=[pl.BlockSpec((tm, D), lambda i: (i, 0)), pl.BlockSpec((tm, D), lambda i: (i, 1)),
                  pl.BlockSpec((CONV_WIDTH, D), lambda i: (0, 0)), vec, heads, heads, vec, vec, vec],
        out_specs=[tile, tile, tile, pl.BlockSpec((tm, N_GATES * D), lambda i: (i, 0))],
        out_shape=[S((T, D), BF16), S((T, D), F32), S((T, D), F32), S((T, N_GATES * D), F32)],
        scratch_shapes=[pltpu.VMEM((tm + SUBLANES, D), F32), pltpu.VMEM((SUBLANES, D), F32)]
        + [pltpu.VMEM((D // LANES, tm, LANES), F32)] * 3 + [_RG_BLOCKS_BF16] * 2,
        args=(proj, proj, conv_w, conv_b, rg_wa, rg_wx, ba, bx, lam), sides=sides, mid=LATE_MID_STEP)


def _layer_norm_stats(v):
    mu = jnp.mean(v, axis=-1, keepdims=True)
    vc = v - mu
    rstd = lax.rsqrt(jnp.mean(vc * vc, axis=-1, keepdims=True) + EPS)
    return vc * rstd, rstd


def _sgu_mix(w_ref, vnb, bst_ref, n_chunk):
    cols = []
    for g in range(N_GROUP):
        vg = vnb[:, CHUNK * g:CHUNK * (g + 1)].reshape(n_chunk, CHUNK, CHUNK)
        wb = jnp.broadcast_to(w_ref[g][None], (n_chunk, CHUNK, CHUNK))
        mg = lax.dot_general(wb, vg, (((2,), (1,)), ((0,), (0,))), preferred_element_type=F32)
        mg = mg + bst_ref[:, g:g + 1][None]
        cols.append(mg.reshape(n_chunk * CHUNK, CHUNK))
    return jnp.concatenate(cols, axis=1)


def _causal_mask():
    return (lax.broadcasted_iota(jnp.int32, (CHUNK, CHUNK), 0) >= lax.broadcasted_iota(jnp.int32, (CHUNK, CHUNK), 1))


def _fill_sgu_weights(ws_ref, bs_ref, w_tril, bs_t, w_tril_t=None):
    keep = _causal_mask()
    for g in range(N_GROUP):
        wg = jnp.where(keep, ws_ref[0, g], 0.0)
        w_tril[g] = wg.astype(BF16)
        if w_tril_t is not None:
            w_tril_t[g] = wg.T.astype(BF16)
    bs_t[...] = bs_ref[0].T


_SGU_W_BF16 = pltpu.VMEM((N_GROUP, CHUNK, CHUNK), BF16)
_SGU_BT = pltpu.VMEM((CHUNK, N_GROUP), F32)


def _sgu_merge_fwd(x, proj, ya, ln_g, ln_b, sgu_ws, sgu_bs, wpa, wpb, wout):
    T = x.shape[0]
    tm = min(TM_MM, T)
    n_chunk = tm // CHUNK

    def body(x_ref, uv_ref, gab_ref, ya_ref, g_ref, b_ref, ws_ref, bs_ref, wpa_ref, wpb_ref, wout_ref,
             yb_ref, pa_ref, pb_ref, mb_ref, x1_ref, w_ref, bst_ref):
        @pl.when(pl.program_id(0) == 0)
        def _():
            _fill_sgu_weights(ws_ref, bs_ref, w_ref, bst_ref)

        vhat, _ = _layer_norm_stats(_gelu(uv_ref[:, D:2 * D].astype(F32)))
        vnb = (vhat * g_ref[...] + b_ref[...]).astype(BF16)
        yb = (_gelu(uv_ref[:, 0:D].astype(F32)) * _sgu_mix(w_ref, vnb, bst_ref, n_chunk)).astype(BF16)
        yb_ref[...] = yb
        pa = _dot(ya_ref[...], wpa_ref[...])
        pb = _dot(yb, wpb_ref[...])
        pa_ref[...] = pa.astype(BF16)
        pb_ref[...] = pb.astype(BF16)
        mb = (_sig(gab_ref[:, 0:D].astype(F32)) * pa + _sig(gab_ref[:, D:2 * D].astype(F32)) * pb).astype(BF16)
        mb_ref[...] = mb
        x1_ref[...] = x_ref[...] + _dot(mb, wout_ref[...])

    tile = pl.BlockSpec((tm, D), lambda i: (i, 0))
    vec = pl.BlockSpec((1, D), lambda i: (0, 0))
    w = _resident((D, D))
    return pl.pallas_call(
        body, name="sgu_merge_fwd", grid=(T // tm,),
        in_specs=[tile, pl.BlockSpec((tm, 2 * D), lambda i: (i, 1)), pl.BlockSpec((tm, 2 * D), lambda i: (i, 2)), tile,
                  vec, vec, pl.BlockSpec(sgu_ws.shape, lambda i: (0, 0, 0, 0)),
                  pl.BlockSpec(sgu_bs.shape, lambda i: (0, 0, 0)), w, w, w],
        out_specs=[tile, tile, tile, tile, tile],
        out_shape=[S((T, D), BF16), S((T, D), BF16), S((T, D), BF16), S((T, D), BF16), S((T, D), F32)],
        scratch_shapes=[_SGU_W_BF16, _SGU_BT],
        compiler_params=_params(1),
    )(x, proj, proj, ya, ln_g, ln_b, sgu_ws, sgu_bs, wpa, wpb, wout)


def _ffn_fwd_loss(x1, g, w_gu, w_down, g_final, target):
    T = x1.shape[0]
    tm = min(TM_FF, T)

    def body(x_ref, g_ref, wgu_ref, wd_ref, gf_ref, t_ref,
             h2_ref, gu_ref, act_ref, dx2_ref, dx2b_ref, loss_ref, dg_ref):
        @pl.when(pl.program_id(0) == 0)
        def _():
            loss_ref[...] = jnp.zeros_like(loss_ref)
            dg_ref[...] = jnp.zeros_like(dg_ref)

        xv = x_ref[...]
        h2 = (xv * _rms_stats(xv) * g_ref[...]).astype(BF16)
        h2_ref[...] = h2
        x2 = xv
        for k in range(N_SHARD // 2):
            cols = slice(FF_SHARD * k, FF_SHARD * (k + 1))
            gate = _dot(h2, wgu_ref[k])
            up = _dot(h2, wgu_ref[k + N_SHARD // 2])
            gu_ref[:, cols] = gate.astype(BF16)
            gu_ref[:, D_FF + FF_SHARD * k:D_FF + FF_SHARD * (k + 1)] = up.astype(BF16)
            act = (gate * _sig(gate) * up).astype(BF16)
            act_ref[:, cols] = act
            x2 = x2 + _dot(act, wd_ref[cols, :])
        gf = gf_ref[...]
        err = x2 * _rms_stats(x2) * gf - t_ref[...]
        loss_ref[...] += 0.5 * jnp.sum(jnp.mean(err * err, axis=-1, keepdims=True), axis=0, keepdims=True)
        dx2, dg_rows = _rms_bwd(err * (1.0 / D), x2, gf)
        dg_ref[...] += _colsum(dg_rows)
        dx2_ref[...] = dx2
        dx2b_ref[...] = dx2.astype(BF16)

    tile = pl.BlockSpec((tm, D), lambda i: (i, 0))
    vec = pl.BlockSpec((1, D), lambda i: (0, 0))
    return pl.pallas_call(
        body, name="ffn_fwd_loss", grid=(T // tm,),
        in_specs=[tile, vec, _resident((N_SHARD, D, FF_SHARD)), _resident((D_FF, D)), vec, tile],
        out_specs=[tile, pl.BlockSpec((tm, 2 * D_FF), lambda i: (i, 0)), pl.BlockSpec((tm, D_FF), lambda i: (i, 0)),
                   tile, tile, pl.BlockSpec((1, 1), lambda i: (0, 0)), vec],
        out_shape=[S((T, D), BF16), S((T, 2 * D_FF), BF16), S((T, D_FF), BF16), S((T, D), F32), S((T, D), BF16),
                   S((1, 1), F32), S((1, D), F32)],
        compiler_params=_params(1),
    )(x1, g, w_gu, w_down, g_final, target)


def _ffn_bwd(dx2, dx2b, gu, w_down, w_gu, x1, g):
    T = x1.shape[0]
    tm = min(TM_FF, T)

    def body(dx2_ref, dx2b_ref, gu_ref, wd_ref, wgu_ref, x_ref, g_ref, dgu_ref, dx1_ref, dx1b_ref, dg_ref):
        @pl.when(pl.program_id(0) == 0)
        def _():
            dg_ref[...] = jnp.zeros_like(dg_ref)

        dxb = dx2b_ref[...]
        dh2 = jnp.zeros((tm, D), F32)
        for k in range(N_SHARD // 2):
            cols = slice(FF_SHARD * k, FF_SHARD * (k + 1))
            up_cols = slice(D_FF + FF_SHARD * k, D_FF + FF_SHARD * (k + 1))
            dact = _dot_nt(dxb, wd_ref[cols, :])
            gate = gu_ref[:, cols].astype(F32)
            sg = _sig(gate)
            dgate = (dact * gu_ref[:, up_cols].astype(F32) * (sg * (1.0 + gate * (1.0 - sg)))).astype(BF16)
            dup = (dact * (gate * sg)).astype(BF16)
            dgu_ref[:, cols] = dgate
            dgu_ref[:, up_cols] = dup
            dh2 = dh2 + _dot_nt(dgate, wgu_ref[k]) + _dot_nt(dup, wgu_ref[k + N_SHARD // 2])
        dx, dg_rows = _rms_bwd(dh2, x_ref[...], g_ref[...])
        dg_ref[...] += _colsum(dg_rows)
        dx1 = dx2_ref[...] + dx
        dx1_ref[...] = dx1
        dx1b_ref[...] = dx1.astype(BF16)

    tile = pl.BlockSpec((tm, D), lambda i: (i, 0))
    wide = pl.BlockSpec((tm, 2 * D_FF), lambda i: (i, 0))
    vec = pl.BlockSpec((1, D), lambda i: (0, 0))
    return pl.pallas_call(
        body, name="ffn_bwd", grid=(T // tm,),
        in_specs=[tile, tile, wide, _resident((D_FF, D)), _resident((N_SHARD, D, FF_SHARD)), tile, vec],
        out_specs=[wide, tile, tile, vec],
        out_shape=[S((T, 2 * D_FF), BF16), S((T, D), F32), S((T, D), BF16), S((1, D), F32)],
        compiler_params=_params(1),
    )(dx2, dx2b, gu, w_down, w_gu, x1, g)


def _matmul_tn(a, b, tn, shard_major, name, sides=()):
    T, M = a.shape
    N = b.shape[1]
    tk = min(TK_WGRAD, T)

    def body(a_ref, b_ref, o_ref):
        @pl.when(pl.program_id(1) == 0)
        def _():
            o_ref[...] = jnp.zeros_like(o_ref)

        acc = _dot_tn(a_ref[...], b_ref[...])
        if shard_major:
            o_ref[0] += acc
        else:
            o_ref[...] += acc

    if shard_major:
        out_spec, out_shape = pl.BlockSpec((1, M, tn), lambda j, k: (j, 0, 0)), S((N // tn, M, tn), F32)
    else:
        out_spec, out_shape = pl.BlockSpec((M, tn), lambda j, k: (0, j)), S((M, N), F32)
    (out,), side_outs = _call(
        body, name=name, grid=(N // tn, T // tk),
        in_specs=[pl.BlockSpec((tk, M), lambda j, k: (k, 0)), pl.BlockSpec((tk, tn), lambda j, k: (k, j))],
        out_specs=[out_spec], out_shape=[out_shape], args=(a, b), sides=sides)
    return out, side_outs


PIECE = IN_SHARD // 3
N_PIECE = IN_COLS // PIECE
DPROJ_ROTATION = 2 * D // PIECE


def _merge_sgu_bwd(dx1b, proj, pa, pb, ya, yb, mb, ln_g, ln_b, sgu_ws, sgu_bs, wpa, wpb, wout, sides=()):
    T = dx1b.shape[0]
    tm = min(TM_FF, T)
    n_chunk = tm // CHUNK
    n_steps = T // tm

    def body(dx_ref, uv_ref, gab_ref, pa_ref, pb_ref, ya_ref, yb_ref, mb_ref, g_ref, b_ref, ws_ref, bs_ref,
             wpa_ref, wpb_ref, wout_ref,
             dya_ref, dp_ref, dw_ref, dbs_ref, dg_ref, db_ref, gpa_ref, gpb_ref, gout_ref,
             w_ref, wt_ref, bst_ref, acc_pa, acc_pb, acc_out):
        @pl.when(pl.program_id(0) == 0)
        def _():
            for ref in (dw_ref, dbs_ref, dg_ref, db_ref, acc_pa, acc_pb, acc_out):
                ref[...] = jnp.zeros_like(ref)
            _fill_sgu_weights(ws_ref, bs_ref, w_ref, bst_ref, wt_ref)

        dxb = dx_ref[...]
        dm = _dot_nt(dxb, wout_ref[...])
        sa = _sig(gab_ref[:, 0:D].astype(F32))
        sb = _sig(gab_ref[:, D:2 * D].astype(F32))
        dpa = (dm * sa).astype(BF16)
        dpb = (dm * sb).astype(BF16)
        acc_pa[...] += _dot_tn(ya_ref[...], dpa)
        acc_pb[...] += _dot_tn(yb_ref[...], dpb)
        acc_out[...] += _dot_tn(mb_ref[...], dxb)
        dp_ref[:, 2 * D:3 * D] = (dm * pa_ref[...].astype(F32) * (sa * (1.0 - sa))).astype(BF16)
        dp_ref[:, 3 * D:4 * D] = (dm * pb_ref[...].astype(F32) * (sb * (1.0 - sb))).astype(BF16)
        dya_ref[...] = _dot_nt(dpa, wpa_ref[...]).astype(BF16)
        dyb_v = _dot_nt(dpb, wpb_ref[...])

        gu, dgu = _gelu_and_grad(uv_ref[:, 0:D].astype(F32))
        gv, dgv = _gelu_and_grad(uv_ref[:, D:2 * D].astype(F32))
        vhat, rstd = _layer_norm_stats(gv)
        lng = g_ref[...]
        vnb = (vhat * lng + b_ref[...]).astype(BF16)
        mixed = _sgu_mix(w_ref, vnb, bst_ref, n_chunk)
        dp_ref[:, 0:D] = (dyb_v * mixed * dgu).astype(BF16)
        dmix = dyb_v * gu
        dmb = dmix.astype(BF16)
        keep = _causal_mask()
        dvn_cols, dbs_rows = [], []
        for g in range(N_GROUP):
            sl = slice(CHUNK * g, CHUNK * (g + 1))
            dmg = dmb[:, sl].reshape(n_chunk, CHUNK, CHUNK)
            vg = vnb[:, sl].reshape(n_chunk, CHUNK, CHUNK)
            wtb = jnp.broadcast_to(wt_ref[g][None], (n_chunk, CHUNK, CHUNK))
            dvn = lax.dot_general(wtb, dmg, (((2,), (1,)), ((0,), (0,))), preferred_element_type=F32)
            dvn_cols.append(dvn.reshape(tm, CHUNK))
            dw = lax.dot_general(dmg, vg, (((2,), (2,)), ((0,), (0,))), preferred_element_type=F32)
            dw_ref[g] += jnp.where(keep, jnp.sum(dw, axis=0), 0.0)
            per_token = jnp.sum(dmix[:, sl], axis=1)
            dbs_rows.append(jnp.sum(per_token.reshape(n_chunk, CHUNK), axis=0, keepdims=True))
        dbs_ref[...] += jnp.concatenate(dbs_rows, axis=0)
        dvn = jnp.concatenate(dvn_cols, axis=1)
        dg_ref[...] += _colsum(dvn * vhat)
        db_ref[...] += _colsum(dvn)
        dvhat = dvn * lng
        dgv_in = rstd * (dvhat - jnp.mean(dvhat, axis=-1, keepdims=True)
                         - vhat * jnp.mean(dvhat * vhat, axis=-1, keepdims=True))
        dp_ref[:, D:2 * D] = (dgv_in * dgv).astype(BF16)

        @pl.when(pl.program_id(0) == n_steps - 1)
        def _():
            for acc, out in ((acc_pa, gpa_ref), (acc_pb, gpb_ref), (acc_out, gout_ref)):
                pltpu.sync_copy(acc, out)

    tile = pl.BlockSpec((tm, D), lambda i: (i, 0))
    vec = pl.BlockSpec((1, D), lambda i: (0, 0))
    w = _resident((D, D))
    wsp = pl.BlockSpec((N_GROUP, CHUNK, CHUNK), lambda i: (0, 0, 0))
    acc = pltpu.VMEM((D, D), F32)
    return _call(
        body, name="merge_sgu_bwd", grid=(n_steps,),
        in_specs=[tile, pl.BlockSpec((tm, 2 * D), lambda i: (i, 1)), pl.BlockSpec((tm, 2 * D), lambda i: (i, 2)),
                  tile, tile, tile, tile, tile, vec, vec, pl.BlockSpec(sgu_ws.shape, lambda i: (0, 0, 0, 0)),
                  pl.BlockSpec(sgu_bs.shape, lambda i: (0, 0, 0)), w, w, w],
        out_specs=[tile, pl.BlockSpec((tm, 4 * D), lambda i: (i, 0)), wsp,
                   pl.BlockSpec((N_GROUP, CHUNK), lambda i: (0, 0)), vec, vec, _ANY, _ANY, _ANY],
        out_shape=[S((T, D), BF16), S((T, IN_COLS), BF16), S((N_GROUP, CHUNK, CHUNK), F32), S((N_GROUP, CHUNK), F32),
                   S((1, D), F32), S((1, D), F32), S((D, D), F32), S((D, D), F32), S((D, D), F32)],
        scratch_shapes=[_SGU_W_BF16, _SGU_W_BF16, _SGU_BT, acc, acc, acc],
        args=(dx1b, proj, proj, pa, pb, ya, yb, mb, ln_g, ln_b, sgu_ws, sgu_bs, wpa, wpb, wout), sides=sides)


def _rglru_bwd(dya, dproj, proj, hseq, xc, gates, conv_w, rg_wa, rg_wx, lam, sides=()):
    T = dya.shape[0]
    tm = min(TM_SCAN, T)
    n = T // tm
    per8 = tm // SUBLANES

    def body(dya_ref, _, rx_ref, rxp_ref, gate_ref, h_ref, hp_ref, xc_ref, gates_ref, cw_ref, wah_ref, wxh_ref,
             lam_ref, dab_ref, dcw_ref, dcb_ref, dwah_ref, dwxh_ref, dba_ref, dbx_ref, dlam_ref,
             hext, rext, dext, carry_a, carry_dh, a_s, b_s, h_s, wa_ref, wx_ref, dwa_ref, dwx_ref):
        i = pl.program_id(0)
        first_tile = i == n - 1

        @pl.when(i == 0)
        def _():
            for ref in (dcw_ref, dcb_ref, dwa_ref, dwx_ref, dba_ref, dbx_ref, dlam_ref, carry_a, carry_dh):
                ref[...] = jnp.zeros_like(ref)
            dext[tm:tm + SUBLANES, :] = jnp.zeros((SUBLANES, D), F32)
            _fill_blockdiag(wah_ref, wa_ref)
            _fill_blockdiag(wxh_ref, wx_ref)

        gel, dgel = _gelu_and_grad(gate_ref[...].astype(F32))
        dya_v = dya_ref[...].astype(F32)
        hseq_v = h_ref[...]
        dgate = dya_v * hseq_v * dgel
        xcv = xc_ref[...]
        lam_v = lam_ref[...]
        sp = _softplus_neg(lam_v)
        xb = xcv.astype(BF16)
        r, gi, a, m = (gates_ref[:, D * k:D * (k + 1)] for k in range(N_GATES))

        row = lax.broadcasted_iota(jnp.int32, (tm, D), 0)
        c = jnp.where(row == tm - 1, carry_a[0:1, :], _shift_up(a, 1, 0.0))
        dH = _scan_tile(c, dya_v * gel, carry_dh[0:1, :], a_s, b_s, h_s, reverse=True)
        carry_a[...] = jnp.broadcast_to(a[0:1, :], (SUBLANES, D))
        carry_dh[...] = jnp.broadcast_to(dH[0:1, :], (SUBLANES, D))

        hext[0:SUBLANES, :] = jnp.where(first_tile, 0.0, hp_ref[...])
        hext[SUBLANES:SUBLANES + tm, :] = hseq_v
        h_prev = hext[pl.ds(SUBLANES - 1, tm), :]

        d_m = dH * (gi * xcv)
        d_la = dH * h_prev * a - d_m * (a * a) / m
        d_ia = dH * m * xcv * (gi * (1.0 - gi))
        d_ra = d_la * ((-RG_C) * sp) * (r * (1.0 - r))
        dlam_ref[...] += _colsum(d_la * ((-RG_C) * r)) * (-_sig(-lam_v))
        dba_ref[...] += _colsum(d_ra)
        dbx_ref[...] += _colsum(d_ia)
        drab = d_ra.astype(BF16)
        diab = d_ia.astype(BF16)
        dxc_cols = []
        for j in range(N_RG_BLOCK):
            sl = slice(RG_BLOCK * j, RG_BLOCK * (j + 1))
            dxc_cols.append(_dot_nt(drab[:, sl], wa_ref[j]) + _dot_nt(diab[:, sl], wx_ref[j]))
            dwa_ref[j] += _dot_tn(xb[:, sl], drab[:, sl])
            dwx_ref[j] += _dot_tn(xb[:, sl], diab[:, sl])
        dxc = dH * m * gi + jnp.concatenate(dxc_cols, axis=1)

        dcb_ref[...] += _colsum(dxc)
        dext[0:tm, :] = dxc
        rext[0:SUBLANES, :] = jnp.where(first_tile, 0.0, rxp_ref[SUBLANES:2 * SUBLANES, :].astype(F32))
        rext[SUBLANES:SUBLANES + tm, :] = rx_ref[...].astype(F32)
        drx = jnp.zeros((tm, D), F32)
        for k in range(CONV_WIDTH):
            drx = drx + dext[pl.ds(CONV_WIDTH - 1 - k, tm), :] * cw_ref[k:k + 1, :]
            dcw_ref[k:k + 1, :] += _colsum(dxc * rext[pl.ds(SUBLANES - (CONV_WIDTH - 1) + k, tm), :])
        dext[tm:tm + SUBLANES, :] = dext[0:SUBLANES, :]
        dab_ref[:, 0:D] = drx.astype(BF16)
        dab_ref[:, D:2 * D] = dgate.astype(BF16)

        @pl.when(first_tile)
        def _():
            for j in range(N_RG_BLOCK):
                for h in range(HEADS_PER_BLOCK):
                    sl = slice(HEAD_DIM * h, HEAD_DIM * (h + 1))
                    pair, side = divmod(HEADS_PER_BLOCK * j + h, 2)
                    lanes = slice(HEAD_DIM * side, HEAD_DIM * (side + 1))
                    dwah_ref[pair, :, lanes] = dwa_ref[j, sl, sl]
                    dwxh_ref[pair, :, lanes] = dwx_ref[j, sl, sl]

    def rev(col):
        return lambda i: (n - 1 - i, col)

    def prev8(col):
        return lambda i: (jnp.maximum((n - 1 - i) * per8 - 1, 0), col)

    def prev16(col):
        return lambda i: (jnp.maximum((n - 1 - i) * (per8 // 2) - 1, 0), col)

    tile = pl.BlockSpec((tm, D), rev(0))
    vec = pl.BlockSpec((1, D), lambda i: (0, 0))
    heads_in = pl.BlockSpec(rg_wa.shape, lambda i: (0, 0, 0, 0))
    head_pairs = (rg_wa.shape[1] // 2, HEAD_DIM, 2 * HEAD_DIM)
    heads_out = pl.BlockSpec(head_pairs, lambda i: (0, 0, 0))
    cw = pl.BlockSpec((CONV_WIDTH, D), lambda i: (0, 0))
    blocks_f32 = pltpu.VMEM((N_RG_BLOCK, RG_BLOCK, RG_BLOCK), F32)
    return _call(
        body, name="rglru_bwd", grid=(n,),
        in_specs=[tile, _ANY, pl.BlockSpec((tm, D), rev(0)), pl.BlockSpec((2 * SUBLANES, D), prev16(0)),
                  pl.BlockSpec((tm, D), rev(1)), tile, pl.BlockSpec((SUBLANES, D), prev8(0)), tile,
                  pl.BlockSpec((tm, N_GATES * D), rev(0)), cw, heads_in, heads_in, vec],
        out_specs=[pl.BlockSpec((tm, 2 * D), rev(2)), cw, vec, heads_out, heads_out, vec, vec, vec],
        out_shape=[S((T, IN_COLS), BF16), S((CONV_WIDTH, D), F32), S((1, D), F32),
                   S(head_pairs, F32), S(head_pairs, F32), S((1, D), F32), S((1, D), F32), S((1, D), F32)],
        scratch_shapes=[pltpu.VMEM((tm + SUBLANES, D), F32), pltpu.VMEM((tm + SUBLANES, D), F32),
                        pltpu.VMEM((tm + SUBLANES, D), F32), pltpu.VMEM((SUBLANES, D), F32),
                        pltpu.VMEM((SUBLANES, D), F32)] + [pltpu.VMEM((D // LANES, tm, LANES), F32)] * 3
        + [_RG_BLOCKS_BF16] * 2 + [blocks_f32] * 2,
        args=(dya, dproj, proj, proj, proj, hseq, hseq, xc, gates, conv_w, rg_wa, rg_wx, lam), sides=sides,
        aliases={1: 0})


def _inproj_dh(dproj, w_in, dh, first, count, name, sides=()):
    T = dproj.shape[0]
    tm = min(TM_MM, T)

    def body(*refs):
        dp_ref, w_ref, dh_ref = refs[-3:]
        dh = jnp.zeros((tm, D), F32)
        for p in range(N_PIECE):
            shard, part = divmod((p + DPROJ_ROTATION) % N_PIECE, IN_SHARD // PIECE)
            dh = dh + _dot_nt(dp_ref[:, PIECE * p:PIECE * (p + 1)], w_ref[shard, :, PIECE * part:PIECE * (part + 1)])
        dh_ref[...] = dh

    earlier = [] if dh is None else [dh]
    return _call(
        body, name=name, grid=(count,),
        in_specs=[_ANY] * len(earlier) + [pl.BlockSpec((tm, IN_COLS), lambda i: (first + i, 0)),
                                         _resident((N_SHARD, D, IN_SHARD))],
        out_specs=[pl.BlockSpec((tm, D), lambda i: (first + i, 0))],
        out_shape=[S((T, D), F32)],
        args=(*earlier, dproj, w_in), sides=sides, aliases={0: 0} if earlier else None)


def _inproj_norm_bwd(dh, x, g, dx1):
    T = x.shape[0]
    tm = min(TM_MM, T)

    def body(dh_ref, x_ref, g_ref, dx1_ref, dx_ref, dgm_ref):
        @pl.when(pl.program_id(0) == 0)
        def _():
            dgm_ref[...] = jnp.zeros_like(dgm_ref)

        dx, dg_rows = _rms_bwd(dh_ref[...], x_ref[...], g_ref[...])
        dgm_ref[...] += _colsum(dg_rows)
        dx_ref[...] = dx1_ref[...] + dx

    tile = pl.BlockSpec((tm, D), lambda i: (i, 0))
    vec = pl.BlockSpec((1, D), lambda i: (0, 0))
    return pl.pallas_call(
        body, name="inproj_norm_bwd", grid=(T // tm,),
        in_specs=[tile, tile, vec, tile],
        out_specs=[tile, vec],
        out_shape=[S((T, D), F32), S((1, D), F32)],
        compiler_params=_params(1),
    )(dh, x, g, dx1)


def _inproj_wgrad(h, dproj, sides=()):
    T = h.shape[0]
    tk = min(TK_WGRAD, T)
    per = IN_SHARD // PIECE

    def body(h_ref, *refs):
        pieces, o_ref = refs[:per], refs[per]

        @pl.when(pl.program_id(1) == 0)
        def _():
            o_ref[...] = jnp.zeros_like(o_ref)

        o_ref[0] += _dot_tn(h_ref[...], jnp.concatenate([p[...] for p in pieces], axis=1))

    def piece(i):
        return pl.BlockSpec((tk, PIECE), lambda j, k: (k, (per * j + i + N_PIECE - DPROJ_ROTATION) % N_PIECE))

    (out,), side_outs = _call(
        body, name="inproj_wgrad", grid=(N_SHARD, T // tk),
        in_specs=[pl.BlockSpec((tk, D), lambda j, k: (k, 0))] + [piece(i) for i in range(per)],
        out_specs=[pl.BlockSpec((1, D, IN_SHARD), lambda j, k: (j, 0, 0))],
        out_shape=[S((N_SHARD, D, IN_SHARD), F32)], args=(h,) + (dproj,) * per, sides=sides)
    return out, side_outs


def _row_tile(rows):
    for t in range(256, 0, -SUBLANES):
        if rows % t == 0:
            return t
    raise ValueError(rows)


def _add_halves(core, grads, theirs, name):
    n = len(grads)
    _, r, cols = grads[0].shape
    half = r // 2
    tr = _row_tile(half)
    nb = half // tr

    def body(core_ref, *refs):
        for g_ref, t_ref, o_ref in zip(refs[:n], refs[n:2 * n], refs[2 * n:]):
            o_ref[...] = (g_ref[...] + t_ref[...]).astype(BF16)

    blk = pl.BlockSpec((1, tr, cols), lambda s, i, core_ref: (s, i, 0))
    mine = pl.BlockSpec((1, tr, cols), lambda s, i, core_ref: (s, core_ref[0] * nb + i, 0))
    gs = pltpu.PrefetchScalarGridSpec(num_scalar_prefetch=1, grid=(N_SHARD, nb),
                                      in_specs=[mine] * n + [blk] * n, out_specs=[blk] * n)
    return pl.pallas_call(
        body, name=name, grid_spec=gs, out_shape=[S((N_SHARD, half, cols), BF16)] * n, compiler_params=_params(2),
    )(core, *grads, *theirs)


def _sum_shards(chip, owns, others, name):
    n = len(owns)
    _, half, cols = owns[0].shape
    tr = _row_tile(half)

    def body(chip_ref, *refs):
        for own_ref, oth_ref, o_ref in zip(refs[:n], refs[n:2 * n], refs[2 * n:]):
            acc = own_ref[0].astype(F32)
            for j in range(3):
                acc = acc + oth_ref[j].astype(F32)
            o_ref[...] = acc

    gs = pltpu.PrefetchScalarGridSpec(
        num_scalar_prefetch=1, grid=(half // tr,),
        in_specs=[pl.BlockSpec((1, tr, cols), lambda i, chip_ref: (chip_ref[0], i, 0))] * n
        + [pl.BlockSpec((3, tr, cols), lambda i, chip_ref: (0, i, 0))] * n,
        out_specs=[pl.BlockSpec((tr, cols), lambda i, chip_ref: (i, 0))] * n)
    return pl.pallas_call(
        body, name=name, grid_spec=gs, out_shape=[S((half, cols), F32)] * n, compiler_params=_params(1),
    )(chip, *owns, *others)


def _adamw(w, g, m, v):
    m = ADAM_B1 * m + (1.0 - ADAM_B1) * g
    v = ADAM_B2 * v + (1.0 - ADAM_B2) * (g * g)
    m_hat = m / (1.0 - ADAM_B1 ** ADAM_STEP)
    v_hat = v / (1.0 - ADAM_B2 ** ADAM_STEP)
    delta = -ADAM_LR * (m_hat / (jnp.sqrt(v_hat) + ADAM_EPS) + ADAM_WD * w)
    return delta, m, v


def _adamw_shard(core, mine, theirs, w, m, v, name):
    n = len(w)
    r, cols = w[0].shape
    half = r // 2
    tr = _row_tile(half)
    nb = half // tr

    def body(core_ref, *refs):
        groups = [refs[k * n:(k + 1) * n] for k in range(9)]
        for mine_ref, theirs_ref, w_ref, m_ref, v_ref, g_ref, d_ref, mo_ref, vo_ref in zip(*groups):
            g = jnp.where(pl.program_id(0) == core_ref[0], mine_ref[...], theirs_ref[...])
            g_ref[...] = g
            d_ref[...], mo_ref[...], vo_ref[...] = _adamw(w_ref[...], g, m_ref[...], v_ref[...])

    hblk = pl.BlockSpec((tr, cols), lambda h, i, core_ref: (i, 0))
    blk = pl.BlockSpec((tr, cols), lambda h, i, core_ref: (h * nb + i, 0))
    gs = pltpu.PrefetchScalarGridSpec(num_scalar_prefetch=1, grid=(2, nb),
                                      in_specs=[hblk] * (2 * n) + [blk] * (3 * n), out_specs=[blk] * (4 * n))
    res = pl.pallas_call(
        body, name=name, grid_spec=gs, out_shape=[S((r, cols), F32)] * (4 * n), compiler_params=_params(2),
    )(core, *mine, *theirs, *w, *m, *v)
    return [tuple(res[k * n + j] for k in range(4)) for j in range(n)]


def _adamw_whole(w, g, m, v, name):
    def body(w_ref, g_ref, m_ref, v_ref, d_ref, mo_ref, vo_ref):
        d_ref[...], mo_ref[...], vo_ref[...] = _adamw(w_ref[...], g_ref[...], m_ref[...], v_ref[...])

    return pl.pallas_call(body, name=name, out_shape=[S(w.shape, F32)] * 3)(w, g, m, v)


_VEC_ROWS = ("norm_mix_g", "conv_b", "rg_lambda", "sgu_ln_g", "sgu_ln_b", "norm_ffn_g", "norm_final_g", "rg_ba",
             "rg_bx")
_CONV_ROW = len(_VEC_ROWS)
_LOSS_ROW = _CONV_ROW + CONV_WIDTH
_VEC_PAD = -(_LOSS_ROW + 1) % SUBLANES
_HEAD_BIASES = ("rg_ba", "rg_bx")
_TENSORS = ("sgu_bs", "sgu_ws", "rg_wa", "rg_wx")
_HEAD_PAIRS = ("rg_wa", "rg_wx")


def _small_sum_adamw(parts, w, m, v):
    names = [n for n in _VEC_ROWS] + list(_TENSORS)
    n_parts = len(parts)

    def total(ref):
        acc = ref[0]
        for k in range(1, N_DEVICE):
            acc = acc + ref[k]
        return acc

    def body(*refs):
        part_refs, refs = refs[:n_parts], refs[n_parts:]
        w_refs, m_refs, v_refs = (dict(zip(names, refs[k * len(names):(k + 1) * len(names)])) for k in range(3))
        outs = refs[3 * len(names):]
        out_refs = {n: outs[4 * k:4 * k + 4] for k, n in enumerate(names)}
        conv_ref, loss_ref = outs[4 * len(names):]
        vec = total(part_refs[0])
        grads = {n: total(p) for n, p in zip(_TENSORS, part_refs[1:])}
        for n in _HEAD_PAIRS:
            pairs = grads[n]
            grads[n] = jnp.stack([pairs[k // 2, :, HEAD_DIM * (k % 2):HEAD_DIM * (k % 2 + 1)]
                                  for k in range(2 * pairs.shape[0])], axis=0)
        grads = {n: g[None] for n, g in grads.items()}
        for row, n in enumerate(_VEC_ROWS):
            g = vec[row:row + 1, :]
            if n in _HEAD_BIASES:
                g = jnp.concatenate([g[:, HEAD_DIM * h:HEAD_DIM * (h + 1)] for h in range(D // HEAD_DIM)], axis=0)[None]
            grads[n] = g
        for n in names:
            g_ref, d_ref, mo_ref, vo_ref = out_refs[n]
            g_ref[...] = grads[n]
            d_ref[...], mo_ref[...], vo_ref[...] = _adamw(w_refs[n][...], grads[n], m_refs[n][...], v_refs[n][...])
        conv_ref[...] = vec[_CONV_ROW:_CONV_ROW + CONV_WIDTH, :]
        loss_ref[...] = vec[_LOSS_ROW:_LOSS_ROW + 1, 0:1]

    res = pl.pallas_call(
        body, name="small_sum_adamw",
        out_shape=[S(w[n].shape, F32) for n in names for _ in range(4)] + [S((CONV_WIDTH, D), F32), S((1, 1), F32)],
        compiler_params=pltpu.CompilerParams(vmem_limit_bytes=VMEM_LIMIT),
    )(*parts, *[w[n] for n in names], *[m[n] for n in names], *[v[n] for n in names])
    return {n: tuple(res[4 * k:4 * k + 4]) for k, n in enumerate(names)}, res[-2], res[-1]


_BIG = ("w_in", "w_proj_a", "w_proj_b", "w_out", "w_gate_up", "w_down")
_WEIGHTS = ("norm_mix_g", "w_in", "conv_w", "conv_b", "rg_wa", "rg_ba", "rg_wx", "rg_bx", "rg_lambda", "sgu_ln_g",
            "sgu_ln_b", "sgu_ws", "sgu_bs", "w_proj_a", "w_proj_b", "w_out", "norm_ffn_g", "w_gate_up", "w_down",
            "norm_final_g")


def kernel(x, norm_mix_g, w_in, conv_w, conv_b, rg_wa, rg_ba, rg_wx, rg_bx, rg_lambda, sgu_ln_g, sgu_ln_b, sgu_ws, sgu_bs, w_proj_a, w_proj_b, w_out, norm_ffn_g, w_gate_up, w_down, norm_final_g, loss_target, m_norm_mix_g, m_w_in, m_conv_w, m_conv_b, m_rg_wa, m_rg_ba, m_rg_wx, m_rg_bx, m_rg_lambda, m_sgu_ln_g, m_sgu_ln_b, m_sgu_ws, m_sgu_bs, m_w_proj_a, m_w_proj_b, m_w_out, m_norm_ffn_g, m_w_gate_up, m_w_down, m_norm_final_g, v_norm_mix_g, v_w_in, v_conv_w, v_conv_b, v_rg_wa, v_rg_ba, v_rg_wx, v_rg_bx, v_rg_lambda, v_sgu_ln_g, v_sgu_ln_b, v_sgu_ws, v_sgu_bs, v_w_proj_a, v_w_proj_b, v_w_out, v_norm_ffn_g, v_w_gate_up, v_w_down, v_norm_final_g):
    args = dict(locals())
    w = {n: args[n] for n in _WEIGHTS}
    mom = {n: args["m_" + n] for n in _WEIGHTS}
    var = {n: args["v_" + n] for n in _WEIGHTS}
    xi, yi, ci = _position()
    core = ci.astype(jnp.int32).reshape(1)
    chip = (2 * xi + yi).astype(jnp.int32).reshape(1)

    bf = {n: w[n][0].astype(BF16) for n in _BIG}
    final_g = w["norm_final_g"].reshape(1, D)
    ba, bx = w["rg_ba"].reshape(1, D), w["rg_bx"].reshape(1, D)
    lam, ln_g, ln_b = w["rg_lambda"], w["sgu_ln_g"], w["sgu_ln_b"]
    x0, target = x[0], loss_target[0]

    def shard_major(g):
        return g.reshape(N_SHARD, g.shape[0] // N_SHARD, g.shape[1])

    def per_shape(names, fn, *lists):
        if len({a.shape for a in lists[0]}) == 1:
            return fn("_".join(names), *lists)
        return [r for k, n in enumerate(names) for r in fn(n, *[[a[k]] for a in lists])]

    def chip_sums(names, grads, theirs):
        return per_shape(names, lambda s, g, t: _add_halves(core, g, t, "add_halves_" + s), grads, theirs)

    def my_halves(names, sums, arrived):
        return per_shape(names, lambda s, p, a: _sum_shards(chip, p, a, "sum_shards_" + s), sums, arrived)

    (proj, h), ((w_in_a,), (conv_a,)) = _inproj_own(
        chip, x0, w["norm_mix_g"], bf["w_in"],
        sides=[_gather_half_side([bf["w_in"]], relations=(0, 1)), _gather_side([w["conv_w"][0]])])
    conv_cols = conv_a.shape[-1]
    conv_full = jnp.swapaxes(conv_a, 0, 1).reshape(CONV_WIDTH, D)
    (proj,), ((w_in_a,),) = _inproj_rest(
        chip, proj, h, w_in_a, 1, 2, "inproj_near",
        sides=[_gather_half_side([bf["w_in"]], relations=(2,), into=[w_in_a])])
    (proj,), _ = _inproj_rest(chip, proj, h, w_in_a, 3, 1, "inproj_far")
    (ya, xc, hseq, gates), ((w_pa_a, w_pb_a, w_out_a, w_gu_a, w_down_a),) = _rglru_fwd(
        proj, conv_full, w["conv_b"], w["rg_wa"], w["rg_wx"], ba, bx, lam,
        sides=[_gather_half_side([bf[n] for n in ("w_proj_a", "w_proj_b", "w_out", "w_gate_up", "w_down")])])
    wpa, wpb, wout, wdown = w_pa_a.reshape(D, D), w_pb_a.reshape(D, D), w_out_a.reshape(D, D), w_down_a.reshape(D_FF, D)
    yb, pa, pb, mb, x1 = _sgu_merge_fwd(x0, proj, ya, ln_g, ln_b, w["sgu_ws"], w["sgu_bs"], wpa, wpb, wout)
    h2, gu, act, dx2, dx2b, loss, d_final_g = _ffn_fwd_loss(x1, w["norm_ffn_g"], w_gu_a, wdown, final_g, target)

    dgu, dx1, dx1b, d_ffn_g = _ffn_bwd(dx2, dx2b, gu, wdown, w_gu_a, x1, w["norm_ffn_g"])
    ffn = ("w_gate_up", "w_down")
    g_ffn = [_matmul_tn(h2, dgu, FF_SHARD, True, "wgrad_gate_up")[0],
             shard_major(_matmul_tn(act, dx2b, D // 2, False, "wgrad_down")[0])]
    (dya, dproj, d_ws, d_bs, d_lng, d_lnb, g_pa, g_pb, g_out), (theirs_ffn,) = _merge_sgu_bwd(
        dx1b, proj, pa, pb, ya, yb, mb, ln_g, ln_b, w["sgu_ws"], w["sgu_bs"], wpa, wpb, wout,
        sides=[_halves_side(g_ffn)])
    sums_ffn = chip_sums(ffn, g_ffn, theirs_ffn)
    mix = ("w_proj_a", "w_proj_b", "w_out")
    g_mix = [shard_major(g) for g in (g_pa, g_pb, g_out)]
    (dproj, d_cw, d_cb, d_wa, d_wx, d_ba, d_bx, d_lam), (arrived_ffn, theirs_mix, sgu_parts) = _rglru_bwd(
        dya, dproj, proj, hseq, xc, gates, conv_full, w["rg_wa"], w["rg_wx"], lam,
        sides=[_scatter_side(sums_ffn), _halves_side(g_mix), _everyone_side([d_bs, d_ws])])
    mine_ffn = my_halves(ffn, sums_ffn, arrived_ffn)
    sums_mix = chip_sums(mix, g_mix, theirs_mix)
    g_in, ((wa_parts, wx_parts), other_ffn, arrived_mix) = _inproj_wgrad(
        h, dproj, sides=[_everyone_side([d_wa, d_wx]), _swap_side(mine_ffn),
                         _scatter_side(sums_mix)])
    mine_mix = my_halves(mix, sums_mix, arrived_mix)
    n_tiles = x0.shape[0] // min(TM_MM, x0.shape[0])
    n_first = max(1, n_tiles // 4)
    (dh,), (theirs_in, other_mix) = _inproj_dh(dproj, w_in_a, None, 0, n_first, "inproj_dh_a",
                                               sides=[_halves_side([g_in]), _swap_side(mine_mix)])
    sums_in = chip_sums(("w_in",), [g_in], theirs_in)
    (dh,), (arrived_in,) = _inproj_dh(dproj, w_in_a, dh, n_first, n_tiles - n_first, "inproj_dh_b",
                                      sides=[_scatter_side(sums_in)])
    mine_in = my_halves(("w_in",), sums_in, arrived_in)
    grad_x, d_mix_g = _inproj_norm_bwd(dh, x0, w["norm_mix_g"], dx1)
    rows = {"norm_mix_g": d_mix_g, "conv_b": d_cb, "rg_lambda": d_lam, "sgu_ln_g": d_lng, "sgu_ln_b": d_lnb,
            "norm_ffn_g": d_ffn_g, "norm_final_g": d_final_g, "rg_ba": d_ba, "rg_bx": d_bx}
    vec = jnp.concatenate([rows[n] for n in _VEC_ROWS]
                          + [d_cw, jnp.pad(loss, ((0, _VEC_PAD), (0, D - 1)))], axis=0)
    other_in, (vec_parts,) = _comm_only([_swap_side(mine_in), _everyone_side([vec])], "swap_w_in")
    small_parts = [vec_parts] + sgu_parts + [wa_parts, wx_parts]

    out = {}
    for names, gm, go in ((ffn, mine_ffn, other_ffn), (mix, mine_mix, other_mix), (("w_in",), mine_in, other_in)):
        results = per_shape(
            names, lambda s, *lists: _adamw_shard(core, *lists, "adamw_" + s),
            gm, go, [w[n][0] for n in names], [mom[n][0] for n in names], [var[n][0] for n in names])
        for n, res in zip(names, results):
            out[n] = tuple(a[None] for a in res)
    as_row = lambda t: {n: a.reshape(1, D) if n == "norm_final_g" else a for n, a in t.items()}
    small_out, conv_sum, loss_sum = _small_sum_adamw(small_parts, as_row(w), as_row(mom), as_row(var))
    out.update(small_out)
    out["norm_final_g"] = tuple(a.reshape(D) for a in small_out["norm_final_g"])
    conv_g = lax.dynamic_slice_in_dim(conv_sum, chip[0] * conv_cols, conv_cols, axis=1)
    d, mo, vo = _adamw_whole(w["conv_w"][0], conv_g, mom["conv_w"][0], var["conv_w"][0], "adamw_conv_w")
    out["conv_w"] = tuple(a[None] for a in (conv_g, d, mo, vo))

    return (loss_sum[0, 0], grad_x[None], *[out[n][0] for n in _WEIGHTS], *[out[n][1] for n in _WEIGHTS],
            *[out[n][2] for n in _WEIGHTS], *[out[n][3] for n in _WEIGHTS])
```

```python
import functools

import jax
import jax.numpy as jnp
from jax import lax
from jax.experimental import pallas as pl
from jax.experimental.pallas import tpu as pltpu

F32 = jnp.float32
BF16 = jnp.bfloat16
S = jax.ShapeDtypeStruct

D = 1024
N_SHARD = 4
IN_COLS = 6 * D
IN_SHARD = IN_COLS // N_SHARD
D_FF = 2816
FF_SHARD = 2 * D_FF // N_SHARD
RG_BLOCK = 256
N_RG_BLOCK = D // RG_BLOCK
CHUNK = 128
N_GROUP = 8
CONV_WIDTH = 4
RG_C = 8.0
EPS = 1e-6
ADAM_LR, ADAM_B1, ADAM_B2, ADAM_EPS, ADAM_WD, ADAM_STEP = 0.001, 0.9, 0.999, 1e-08, 0.01, 10

V7X_VMEM_BYTES = 64 * 1024 * 1024
VMEM_LIMIT = V7X_VMEM_BYTES * 3 // 4
SUBLANES = 8
MESH = pl.DeviceIdType.MESH

STREAM_SLOTS = 3
TM_MM = 512
TM_SCAN = 256
TM_FF = 256
TK_WGRAD = 2048


def _params(n_axes):
    return pltpu.CompilerParams(dimension_semantics=("arbitrary",) * n_axes, vmem_limit_bytes=VMEM_LIMIT)


def _resident(shape):
    nd = len(shape)
    return pl.BlockSpec(shape, lambda *_: (0,) * nd, pipeline_mode=pl.Buffered(1))


def _sig(x):
    return 1.0 / (1.0 + jnp.exp(-x))


_GELU_K2 = 2.0 * 0.7978845608028654
_GELU_C = 0.044715


def _gelu(x):
    return x * _sig(x * (_GELU_K2 + (_GELU_K2 * _GELU_C) * (x * x)))


def _gelu_and_grad(x):
    x2 = x * x
    s = _sig(x * (_GELU_K2 + (_GELU_K2 * _GELU_C) * x2))
    g = x * s
    return g, s + g * (1.0 - s) * (_GELU_K2 + (3.0 * _GELU_K2 * _GELU_C) * x2)


_EXPM1_SERIES = tuple(1.0 / f for f in (5040.0, 720.0, 120.0, 24.0, 6.0, 2.0, 1.0))


def _one_minus_exp(x):
    p = _EXPM1_SERIES[0]
    for coef in _EXPM1_SERIES[1:]:
        p = p * x + coef
    return jnp.where(x > -0.125, -x * p, 1.0 - jnp.exp(x))


def _softplus_neg(lam):
    z = -lam
    e = jnp.exp(-jnp.abs(z))
    u = 1.0 + e
    log1p = jnp.where(u == 1.0, e, jnp.log(u) * e / (u - 1.0))
    return jnp.maximum(z, 0.0) + log1p


def _rms_stats(x):
    return lax.rsqrt(jnp.mean(x * x, axis=-1, keepdims=True) + EPS)


def _rms_bwd(dy, x, g):
    rstd = _rms_stats(x)
    xhat = x * rstd
    dxhat = dy * g
    dx = rstd * (dxhat - xhat * jnp.mean(dxhat * xhat, axis=-1, keepdims=True))
    return dx, dy * xhat


def _colsum(x):
    return jnp.sum(x, axis=0, keepdims=True)


def _shift_down(x, d, fill):
    n = x.shape[0]
    if d % SUBLANES == 0:
        return jnp.concatenate([jnp.full((d, x.shape[1]), fill, x.dtype), x[:n - d]], axis=0)
    row = lax.broadcasted_iota(jnp.int32, x.shape, 0)
    return jnp.where(row < d, fill, pltpu.roll(x, d, 0))


def _shift_up(x, d, fill):
    n = x.shape[0]
    if d % SUBLANES == 0:
        return jnp.concatenate([x[d:], jnp.full((d, x.shape[1]), fill, x.dtype)], axis=0)
    row = lax.broadcasted_iota(jnp.int32, x.shape, 0)
    return jnp.where(row >= n - d, fill, pltpu.roll(x, n - d, 0))


def _scan(a, b, shift):
    d = 1
    while d < a.shape[0]:
        b = a * shift(b, d, 0.0) + b
        a = a * shift(a, d, 1.0)
        d *= 2
    return a, b


LANES = 128


def _scan_tile(a, b, outside, a_s, b_s, h_s, reverse):
    tm = a.shape[0]
    groups = tm // SUBLANES
    order = list(range(SUBLANES - 1, -1, -1) if reverse else range(SUBLANES))
    shift = _shift_up if reverse else _shift_down
    edge = groups - 1 if reverse else 0
    for j in range(D // LANES):
        a_s[j] = a[:, LANES * j:LANES * (j + 1)]
        b_s[j] = b[:, LANES * j:LANES * (j + 1)]
    for j in range(D // LANES):
        def slab(ref, k):
            return ref[j, pl.ds(k, groups, stride=SUBLANES), :]

        ga, gb = slab(a_s, order[0]), slab(b_s, order[0])
        for k in order[1:]:
            ak = slab(a_s, k)
            gb = ak * gb + slab(b_s, k)
            ga = ak * ga
        ga, gb = _scan(ga, gb, shift)
        h_out = outside[:, LANES * j:LANES * (j + 1)]
        group_end = ga * h_out + gb
        row = lax.broadcasted_iota(jnp.int32, (groups, LANES), 0)
        h = jnp.where(row == edge, h_out, shift(group_end, 1, 0.0))
        for k in order:
            h = slab(a_s, k) * h + slab(b_s, k)
            h_s[j, pl.ds(k, groups, stride=SUBLANES), :] = h
    return jnp.concatenate([h_s[j] for j in range(D // LANES)], axis=1)


def _dot(a, b):
    return jnp.dot(a, b, preferred_element_type=F32)


def _dot_nt(a, b):
    return lax.dot_general(a, b, (((1,), (1,)), ((), ())), preferred_element_type=F32)


def _dot_tn(a, b):
    return lax.dot_general(a, b, (((0,), (0,)), ((), ())), preferred_element_type=F32)


_ANY = pl.BlockSpec(memory_space=pl.ANY)


def _position():
    return lax.axis_index("x"), lax.axis_index("y"), lax.axis_index("c")


def _other_chips(x, y):
    return [(1 - x, y), (x, 1 - y), (1 - x, 1 - y)]


class _Side:
    def __init__(self, inputs, out_shapes, n_sems, make, continues=()):
        self.inputs, self.out_shapes, self.n_sems, self.make = list(inputs), list(out_shapes), n_sems, make
        self.continues = list(continues)


MID_STEP = 0.625
LATE_MID_STEP = 0.875


def _call(body, *, name, grid, in_specs, out_specs, out_shape, args, scratch_shapes=(), sides=(), aliases=None,
          scalars=None, mid=MID_STEP):
    n_in, n_out, n_scr = len(in_specs), len(out_specs), len(scratch_shapes)
    n_scalar = 0 if scalars is None else 1
    side_in = [len(s.inputs) + len(s.continues) for s in sides]
    side_out = [len(s.out_shapes) for s in sides]
    all_aliases = {k + n_scalar: v for k, v in (aliases or {}).items()}
    for idx, s in enumerate(sides):
        for k in range(len(s.continues)):
            operand = n_scalar + n_in + sum(side_in[:idx]) + len(s.inputs) + k
            all_aliases[operand] = n_out + sum(side_out[:idx]) + k

    def wrapped(*refs):
        refs = list(refs)
        take = lambda k: [refs.pop(0) for _ in range(k)]
        ins = take(n_scalar) + take(n_in)
        sins = [take(k) for k in side_in]
        outs = take(n_out)
        souts = [take(k) for k in side_out]
        scr = take(n_scr)
        sems = [take(3) for _ in sides]
        def run(phase):
            for s, si, so, sem in zip(sides, sins, souts, sems):
                for thunk in s.make(si[:len(s.inputs)], so, *sem)[phase]:
                    thunk()

        if sides:
            n_steps = functools.reduce(lambda a, b: a * b, grid)
            step = functools.reduce(lambda a, b: a + b, [
                pl.program_id(a) * functools.reduce(lambda p, q: p * q, grid[a + 1:], 1) for a in range(len(grid))])
            pl.when(step == 0)(lambda: run(0))
        body(*ins, *outs, *scr)
        if sides:
            pl.when(step == int(mid * (n_steps - 1)))(lambda: run(1))
            pl.when(step == n_steps - 1)(lambda: run(2))

    grid_spec = pltpu.PrefetchScalarGridSpec(
        num_scalar_prefetch=n_scalar, grid=grid,
        in_specs=list(in_specs) + [_ANY] * sum(side_in),
        out_specs=list(out_specs) + [_ANY] * sum(side_out),
        scratch_shapes=list(scratch_shapes) + [pltpu.SemaphoreType.DMA((s.n_sems,)) for s in sides for _ in range(3)])
    res = pl.pallas_call(
        wrapped, name=name, grid_spec=grid_spec,
        out_shape=list(out_shape) + [o for s in sides for o in s.out_shapes],
        input_output_aliases=all_aliases,
        compiler_params=_params(len(grid)),
    )(*([scalars] if n_scalar else []), *args, *[a for s in sides for a in s.inputs + s.continues])
    main, rest, per_side = list(res[:n_out]), list(res[n_out:]), []
    for k in side_out:
        per_side.append(rest[:k])
        rest = rest[k:]
    return main, per_side


def _comm_only(sides, name):
    def body():
        pass

    return _call(body, name=name, grid=(1,), in_specs=[], out_specs=[], out_shape=[], args=[], sides=sides)[1]


def _remote(src, dst, send, recv, k, device):
    return pltpu.make_async_remote_copy(src_ref=src, dst_ref=dst, send_sem=send.at[k], recv_sem=recv.at[k],
                                        device_id=device, device_id_type=MESH)


def _both_ways(copy, keys):
    return [lambda k=k: copy(k).start() for k in keys], [], [lambda k=k: copy(k).wait() for k in keys]


def _gather_side(shards):
    n = len(shards)

    def make(ins, outs, send, recv, local):
        x, y, c = _position()
        mine = 2 * x + y
        chips = _other_chips(x, y)
        pairs = [(w, j) for w in range(n) for j in range(3)]

        def own(w):
            return pltpu.make_async_copy(ins[w], outs[w].at[mine], local.at[w])

        def push(w, j):
            return _remote(ins[w], outs[w].at[mine], send, recv, 3 * w + j, (*chips[j], c))

        def arrival(w, j):
            px, py = chips[j]
            return _remote(ins[w], outs[w].at[2 * px + py], send, recv, 3 * w + j, (px, py, c))

        starts = [lambda w=w: own(w).start() for w in range(n)] + [lambda w=w, j=j: push(w, j).start() for w, j in pairs]
        waits = ([lambda w=w, j=j: arrival(w, j).wait_recv() for w, j in pairs]
                 + [lambda w=w, j=j: push(w, j).wait_send() for w, j in pairs]
                 + [lambda w=w: own(w).wait() for w in range(n)])
        return starts, [], waits

    return _Side(shards, [S((N_SHARD,) + s.shape, s.dtype) for s in shards], 3 * n, make)


def _gather_half_side(shards, relations=(0, 1, 2), into=None):
    n = len(shards)

    def make(ins, outs, send, recv, local):
        x, y, c = _position()
        mine = 2 * x + y
        chips = _other_chips(x, y)
        pairs = [(w, j) for w in range(n) for j in relations]

        def rows(w, core):
            half = ins[w].shape[0] // 2
            return pl.ds(core * half, half)

        def own(w):
            return pltpu.make_async_copy(ins[w], outs[w].at[mine], local.at[w])

        def push(w, j):
            return _remote(ins[w].at[rows(w, c), :], outs[w].at[mine, rows(w, c), :], send, recv, 3 * w + j,
                           (*chips[j], c))

        def landed(w, j, core):
            px, py = chips[j]
            return outs[w].at[2 * px + py, rows(w, core), :]

        def arrival(w, j):
            return _remote(ins[w].at[rows(w, c), :], landed(w, j, c), send, recv, 3 * w + j, (*chips[j], c))

        def passed(w, j, core):
            return _remote(landed(w, j, core), landed(w, j, core), send, recv, 3 * n + 3 * w + j, (x, y, 1 - c))

        owns = range(n) if into is None else ()
        starts = [lambda w=w: own(w).start() for w in owns] + [lambda w=w, j=j: push(w, j).start() for w, j in pairs]
        mids = [t for w, j in pairs for t in (lambda w=w, j=j: arrival(w, j).wait_recv(),
                                              lambda w=w, j=j: passed(w, j, c).start())]
        waits = ([lambda w=w, j=j: passed(w, j, 1 - c).wait_recv() for w, j in pairs]
                 + [lambda w=w, j=j: passed(w, j, c).wait_send() for w, j in pairs]
                 + [lambda w=w, j=j: push(w, j).wait_send() for w, j in pairs]
                 + [lambda w=w: own(w).wait() for w in owns])
        return starts, mids, waits

    return _Side(shards, [S((N_SHARD,) + s.shape, s.dtype) for s in shards], 6 * n, make, continues=into or ())


def _halves_side(grads):
    n = len(grads)

    def make(ins, outs, send, recv, local):
        x, y, c = _position()

        def copy(w):
            half = ins[w].shape[1] // 2
            return _remote(ins[w].at[:, pl.ds((1 - c) * half, half), :], outs[w], send, recv, w, (x, y, 1 - c))

        return _both_ways(copy, range(n))

    return _Side(grads, [S((N_SHARD, g.shape[1] // 2, g.shape[2]), F32) for g in grads], n, make)


def _scatter_side(partials):
    n = len(partials)

    def make(ins, outs, send, recv, local):
        x, y, c = _position()
        chips = _other_chips(x, y)

        def copy(k):
            w, j = divmod(k, 3)
            px, py = chips[j]
            return _remote(ins[w].at[2 * px + py], outs[w].at[j], send, recv, k, (px, py, c))

        return _both_ways(copy, range(3 * n))

    return _Side(partials, [S((3,) + p.shape[1:], p.dtype) for p in partials], 3 * n, make)


def _swap_side(halves):
    n = len(halves)

    def make(ins, outs, send, recv, local):
        x, y, c = _position()
        return _both_ways(lambda w: _remote(ins[w], outs[w], send, recv, w, (x, y, 1 - c)), range(n))

    return _Side(halves, [S(h.shape, h.dtype) for h in halves], n, make)


N_DEVICE = 8


def _everyone_side(arrays):
    n = len(arrays)
    peers = N_DEVICE - 1

    def make(ins, outs, send, recv, local):
        x, y, c = _position()
        mine = 4 * x + 2 * y + c
        pairs = [(w, k) for w in range(n) for k in range(1, N_DEVICE)]

        def peer(k):
            return (1 - x if k & 4 else x, 1 - y if k & 2 else y, 1 - c if k & 1 else c)

        def own(w):
            return pltpu.make_async_copy(ins[w], outs[w].at[mine], local.at[w])

        def push(w, k):
            return _remote(ins[w], outs[w].at[mine], send, recv, peers * w + k - 1, peer(k))

        def arrival(w, k):
            px, py, pc = peer(k)
            return _remote(ins[w], outs[w].at[4 * px + 2 * py + pc], send, recv, peers * w + k - 1, (px, py, pc))

        starts = [lambda w=w: own(w).start() for w in range(n)] + [lambda w=w, k=k: push(w, k).start() for w, k in pairs]
        waits = ([lambda w=w, k=k: arrival(w, k).wait_recv() for w, k in pairs]
                 + [lambda w=w, k=k: push(w, k).wait_send() for w, k in pairs]
                 + [lambda w=w: own(w).wait() for w in range(n)])
        return starts, [], waits

    return _Side(arrays, [S((N_DEVICE,) + a.shape, a.dtype) for a in arrays], peers * n, make)


def _inproj_own(chip, x, g, w_shard, sides=()):
    T = x.shape[0]
    tm = min(TM_MM, T)

    def body(chip_ref, x_ref, g_ref, w_ref, proj_ref, h_ref):
        xv = x_ref[...]
        h = (xv * _rms_stats(xv) * g_ref[...]).astype(BF16)
        h_ref[...] = h
        proj_ref[...] = _dot(h, w_ref[...]).astype(BF16)

    return _call(
        body, name="inproj_own", grid=(T // tm,),
        in_specs=[pl.BlockSpec((tm, D), lambda i, c: (i, 0)), pl.BlockSpec((1, D), lambda i, c: (0, 0)),
                  pl.BlockSpec((D, IN_SHARD), lambda i, c: (0, 0), pipeline_mode=pl.Buffered(1))],
        out_specs=[pl.BlockSpec((tm, IN_SHARD), lambda i, c: (i, c[0])), pl.BlockSpec((tm, D), lambda i, c: (i, 0))],
        out_shape=[S((T, IN_COLS), BF16), S((T, D), BF16)],
        args=(x, g, w_shard), sides=sides, scalars=chip, mid=LATE_MID_STEP)


def _inproj_rest(chip, proj, h, w_in, first, count, name, sides=(), mid=MID_STEP):
    T = h.shape[0]
    tm = min(TM_MM, T)

    def body(chip_ref, _, h_ref, w_ref, proj_ref):
        proj_ref[...] = _dot(h_ref[...], w_ref[0]).astype(BF16)

    def other(p, c):
        return jnp.bitwise_xor(c[0], first + p)

    return _call(
        body, name=name, grid=(count, T // tm),
        in_specs=[_ANY, pl.BlockSpec((tm, D), lambda p, i, c: (i, 0)),
                  pl.BlockSpec((1, D, IN_SHARD), lambda p, i, c: (other(p, c), 0, 0))],
        out_specs=[pl.BlockSpec((tm, IN_SHARD), lambda p, i, c: (i, other(p, c)))],
        out_shape=[S((T, IN_COLS), BF16)],
        args=(proj, h, w_in), sides=sides, scalars=chip, aliases={0: 0}, mid=mid)


def _rg_gates(xc, wa_ref, wx_ref, ba, bx, sp):
    xb = xc.astype(BF16)
    blocks = [xb[:, RG_BLOCK * j:RG_BLOCK * (j + 1)] for j in range(N_RG_BLOCK)]
    r = _sig(jnp.concatenate([_dot(blocks[j], wa_ref[j]) for j in range(N_RG_BLOCK)], axis=1) + ba)
    gi = _sig(jnp.concatenate([_dot(blocks[j], wx_ref[j]) for j in range(N_RG_BLOCK)], axis=1) + bx)
    log_a = (-RG_C) * r * sp
    a = jnp.exp(log_a)
    m = jnp.sqrt(_one_minus_exp(2.0 * log_a))
    return xb, r, gi, a, m


N_GATES = 4
HEADS_PER_BLOCK = 4
HEAD_DIM = RG_BLOCK // HEADS_PER_BLOCK
_RG_BLOCKS_BF16 = pltpu.VMEM((N_RG_BLOCK, RG_BLOCK, RG_BLOCK), BF16)


def _fill_blockdiag(heads_ref, blocks):
    blocks[...] = jnp.zeros_like(blocks)
    for j in range(N_RG_BLOCK):
        for h in range(HEADS_PER_BLOCK):
            sl = slice(HEAD_DIM * h, HEAD_DIM * (h + 1))
            blocks[j, sl, sl] = heads_ref[0, HEADS_PER_BLOCK * j + h].astype(BF16)


def _rglru_fwd(proj, conv_w, conv_b, rg_wa, rg_wx, ba, bx, lam, sides=()):
    T = proj.shape[0]
    tm = min(TM_SCAN, T)

    def body(rx_ref, gate_ref, cw_ref, cb_ref, wah_ref, wxh_ref, ba_ref, bx_ref, lam_ref,
             ya_ref, xc_ref, h_ref, gates_ref, ext, hc, a_s, b_s, h_s, wa_ref, wx_ref):
        @pl.when(pl.program_id(0) == 0)
        def _():
            ext[0:SUBLANES, :] = jnp.zeros((SUBLANES, D), F32)
            hc[...] = jnp.zeros((SUBLANES, D), F32)
            _fill_blockdiag(wah_ref, wa_ref)
            _fill_blockdiag(wxh_ref, wx_ref)

        ext[SUBLANES:SUBLANES + tm, :] = rx_ref[...].astype(F32)
        xc = cb_ref[...]
        for k in range(CONV_WIDTH):
            xc = xc + ext[pl.ds(SUBLANES - (CONV_WIDTH - 1) + k, tm), :] * cw_ref[k:k + 1, :]
        ext[0:SUBLANES, :] = ext[tm:tm + SUBLANES, :]
        xc_ref[...] = xc
        _, r, gi, a, m = _rg_gates(xc, wa_ref, wx_ref, ba_ref[...], bx_ref[...], _softplus_neg(lam_ref[...]))
        for k, val in enumerate((r, gi, a, m)):
            gates_ref[:, D * k:D * (k + 1)] = val
        h = _scan_tile(a, m * (gi * xc), hc[0:1, :], a_s, b_s, h_s, reverse=False)
        hc[...] = jnp.broadcast_to(h[tm - 1:tm, :], (SUBLANES, D))
        h_ref[...] = h
        ya_ref[...] = (_gelu(gate_ref[...].astype(F32)) * h).astype(BF16)

    vec = pl.BlockSpec((1, D), lambda i: (0, 0))
    heads = pl.BlockSpec(rg_wa.shape, lambda i: (0, 0, 0, 0))
    tile = pl.BlockSpec((tm, D), lambda i: (i, 0))
    return _call(
        body, name="rglru_fwd", grid=(T // tm,),
        in_specs=[pl.BlockSpec((tm, D), lambda i: (i, 0)), pl.BlockSpec((tm, D), lambda i: (i, 1)),
                  pl.BlockSpec((CONV_WIDTH, D), lambda i: (0, 0)), vec, heads, heads, vec, vec, vec],
        out_specs=[tile, tile, tile, pl.BlockSpec((tm, N_GATES * D), lambda i: (i, 0))],
        out_shape=[S((T, D), BF16), S((T, D), F32), S((T, D), F32), S((T, N_GATES * D), F32)],
        scratch_shapes=[pltpu.VMEM((tm + SUBLANES, D), F32), pltpu.VMEM((SUBLANES, D), F32)]
        + [pltpu.VMEM((D // LANES, tm, LANES), F32)] * 3 + [_RG_BLOCKS_BF16] * 2,
        args=(proj, proj, conv_w, conv_b, rg_wa, rg_wx, ba, bx, lam), sides=sides, mid=LATE_MID_STEP)


def _layer_norm_stats(v):
    mu = jnp.mean(v, axis=-1, keepdims=True)
    vc = v - mu
    rstd = lax.rsqrt(jnp.mean(vc * vc, axis=-1, keepdims=True) + EPS)
    return vc * rstd, rstd


def _sgu_mix(w_ref, vnb, bst_ref, n_chunk):
    cols = []
    for g in range(N_GROUP):
        vg = vnb[:, CHUNK * g:CHUNK * (g + 1)].reshape(n_chunk, CHUNK, CHUNK)
        wb = jnp.broadcast_to(w_ref[g][None], (n_chunk, CHUNK, CHUNK))
        mg = lax.dot_general(wb, vg, (((2,), (1,)), ((0,), (0,))), preferred_element_type=F32)
        mg = mg + bst_ref[:, g:g + 1][None]
        cols.append(mg.reshape(n_chunk * CHUNK, CHUNK))
    return jnp.concatenate(cols, axis=1)


def _causal_mask():
    return (lax.broadcasted_iota(jnp.int32, (CHUNK, CHUNK), 0) >= lax.broadcasted_iota(jnp.int32, (CHUNK, CHUNK), 1))


def _fill_sgu_weights(ws_ref, bs_ref, w_tril, bs_t, w_tril_t=None):
    keep = _causal_mask()
    for g in range(N_GROUP):
        wg = jnp.where(keep, ws_ref[0, g], 0.0)
        w_tril[g] = wg.astype(BF16)
        if w_tril_t is not None:
            w_tril_t[g] = wg.T.astype(BF16)
    bs_t[...] = bs_ref[0].T


_SGU_W_BF16 = pltpu.VMEM((N_GROUP, CHUNK, CHUNK), BF16)
_SGU_BT = pltpu.VMEM((CHUNK, N_GROUP), F32)


def _sgu_merge_fwd(x, proj, ya, ln_g, ln_b, sgu_ws, sgu_bs, wpa, wpb, wout):
    T = x.shape[0]
    tm = min(TM_MM, T)
    n_chunk = tm // CHUNK

    def body(x_ref, uv_ref, gab_ref, ya_ref, g_ref, b_ref, ws_ref, bs_ref, wpa_ref, wpb_ref, wout_ref,
             yb_ref, pa_ref, pb_ref, mb_ref, x1_ref, w_ref, bst_ref):
        @pl.when(pl.program_id(0) == 0)
        def _():
            _fill_sgu_weights(ws_ref, bs_ref, w_ref, bst_ref)

        vhat, _ = _layer_norm_stats(_gelu(uv_ref[:, D:2 * D].astype(F32)))
        vnb = (vhat * g_ref[...] + b_ref[...]).astype(BF16)
        yb = (_gelu(uv_ref[:, 0:D].astype(F32)) * _sgu_mix(w_ref, vnb, bst_ref, n_chunk)).astype(BF16)
        yb_ref[...] = yb
        pa = _dot(ya_ref[...], wpa_ref[...])
        pb = _dot(yb, wpb_ref[...])
        pa_ref[...] = pa.astype(BF16)
        pb_ref[...] = pb.astype(BF16)
        mb = (_sig(gab_ref[:, 0:D].astype(F32)) * pa + _sig(gab_ref[:, D:2 * D].astype(F32)) * pb).astype(BF16)
        mb_ref[...] = mb
        x1_ref[...] = x_ref[...] + _dot(mb, wout_ref[...])

    tile = pl.BlockSpec((tm, D), lambda i: (i, 0))
    vec = pl.BlockSpec((1, D), lambda i: (0, 0))
    w = _resident((D, D))
    return pl.pallas_call(
        body, name="sgu_merge_fwd", grid=(T // tm,),
        in_specs=[tile, pl.BlockSpec((tm, 2 * D), lambda i: (i, 1)), pl.BlockSpec((tm, 2 * D), lambda i: (i, 2)), tile,
                  vec, vec, pl.BlockSpec(sgu_ws.shape, lambda i: (0, 0, 0, 0)),
                  pl.BlockSpec(sgu_bs.shape, lambda i: (0, 0, 0)), w, w, w],
        out_specs=[tile, tile, tile, tile, tile],
        out_shape=[S((T, D), BF16), S((T, D), BF16), S((T, D), BF16), S((T, D), BF16), S((T, D), F32)],
        scratch_shapes=[_SGU_W_BF16, _SGU_BT],
        compiler_params=_params(1),
    )(x, proj, proj, ya, ln_g, ln_b, sgu_ws, sgu_bs, wpa, wpb, wout)


def _ffn_fwd_loss(x1, g, w_gu, w_down, g_final, target):
    T = x1.shape[0]
    tm = min(TM_FF, T)

    def body(x_ref, g_ref, wgu_ref, wd_ref, gf_ref, t_ref,
             h2_ref, gu_ref, act_ref, dx2_ref, dx2b_ref, loss_ref, dg_ref):
        @pl.when(pl.program_id(0) == 0)
        def _():
            loss_ref[...] = jnp.zeros_like(loss_ref)
            dg_ref[...] = jnp.zeros_like(dg_ref)

        xv = x_ref[...]
        h2 = (xv * _rms_stats(xv) * g_ref[...]).astype(BF16)
        h2_ref[...] = h2
        x2 = xv
        for k in range(N_SHARD // 2):
            cols = slice(FF_SHARD * k, FF_SHARD * (k + 1))
            gate = _dot(h2, wgu_ref[k])
            up = _dot(h2, wgu_ref[k + N_SHARD // 2])
            gu_ref[:, cols] = gate.astype(BF16)
            gu_ref[:, D_FF + FF_SHARD * k:D_FF + FF_SHARD * (k + 1)] = up.astype(BF16)
            act = (gate * _sig(gate) * up).astype(BF16)
            act_ref[:, cols] = act
            x2 = x2 + _dot(act, wd_ref[cols, :])
        gf = gf_ref[...]
        err = x2 * _rms_stats(x2) * gf - t_ref[...]
        loss_ref[...] += 0.5 * jnp.sum(jnp.mean(err * err, axis=-1, keepdims=True), axis=0, keepdims=True)
        dx2, dg_rows = _rms_bwd(err * (1.0 / D), x2, gf)
        dg_ref[...] += _colsum(dg_rows)
        dx2_ref[...] = dx2
        dx2b_ref[...] = dx2.astype(BF16)

    tile = pl.BlockSpec((tm, D), lambda i: (i, 0))
    vec = pl.BlockSpec((1, D), lambda i: (0, 0))
    return pl.pallas_call(
        body, name="ffn_fwd_loss", grid=(T // tm,),
        in_specs=[tile, vec, _resident((N_SHARD, D, FF_SHARD)), _resident((D_FF, D)), vec, tile],
        out_specs=[tile, pl.BlockSpec((tm, 2 * D_FF), lambda i: (i, 0)), pl.BlockSpec((tm, D_FF), lambda i: (i, 0)),
                   tile, tile, pl.BlockSpec((1, 1), lambda i: (0, 0)), vec],
        out_shape=[S((T, D), BF16), S((T, 2 * D_FF), BF16), S((T, D_FF), BF16), S((T, D), F32), S((T, D), BF16),
                   S((1, 1), F32), S((1, D), F32)],
        compiler_params=_params(1),
    )(x1, g, w_gu, w_down, g_final, target)


def _ffn_bwd(dx2, dx2b, gu, w_down, w_gu, x1, g):
    T = x1.shape[0]
    tm = min(TM_FF, T)

    def body(dx2_ref, dx2b_ref, gu_ref, wd_ref, wgu_ref, x_ref, g_ref, dgu_ref, dx1_ref, dx1b_ref, dg_ref):
        @pl.when(pl.program_id(0) == 0)
        def _():
            dg_ref[...] = jnp.zeros_like(dg_ref)

        dxb = dx2b_ref[...]
        dh2 = jnp.zeros((tm, D), F32)
        for k in range(N_SHARD // 2):
            cols = slice(FF_SHARD * k, FF_SHARD * (k + 1))
            up_cols = slice(D_FF + FF_SHARD * k, D_FF + FF_SHARD * (k + 1))
            dact = _dot_nt(dxb, wd_ref[cols, :])
            gate = gu_ref[:, cols].astype(F32)
            sg = _sig(gate)
            dgate = (dact * gu_ref[:, up_cols].astype(F32) * (sg * (1.0 + gate * (1.0 - sg)))).astype(BF16)
            dup = (dact * (gate * sg)).astype(BF16)
            dgu_ref[:, cols] = dgate
            dgu_ref[:, up_cols] = dup
            dh2 = dh2 + _dot_nt(dgate, wgu_ref[k]) + _dot_nt(dup, wgu_ref[k + N_SHARD // 2])
        dx, dg_rows = _rms_bwd(dh2, x_ref[...], g_ref[...])
        dg_ref[...] += _colsum(dg_rows)
        dx1 = dx2_ref[...] + dx
        dx1_ref[...] = dx1
        dx1b_ref[...] = dx1.astype(BF16)

    tile = pl.BlockSpec((tm, D), lambda i: (i, 0))
    wide = pl.BlockSpec((tm, 2 * D_FF), lambda i: (i, 0))
    vec = pl.BlockSpec((1, D), lambda i: (0, 0))
    return pl.pallas_call(
        body, name="ffn_bwd", grid=(T // tm,),
        in_specs=[tile, tile, wide, _resident((D_FF, D)), _resident((N_SHARD, D, FF_SHARD)), tile, vec],
        out_specs=[wide, tile, tile, vec],
        out_shape=[S((T, 2 * D_FF), BF16), S((T, D), F32), S((T, D), BF16), S((1, D), F32)],
        compiler_params=_params(1),
    )(dx2, dx2b, gu, w_down, w_gu, x1, g)


def _matmul_tn(a, b, tn, shard_major, name, sides=()):
    T, M = a.shape
    N = b.shape[1]
    tk = min(TK_WGRAD, T)

    def body(a_ref, b_ref, o_ref):
        @pl.when(pl.program_id(1) == 0)
        def _():
            o_ref[...] = jnp.zeros_like(o_ref)

        acc = _dot_tn(a_ref[...], b_ref[...])
        if shard_major:
            o_ref[0] += acc
        else:
            o_ref[...] += acc

    if shard_major:
        out_spec, out_shape = pl.BlockSpec((1, M, tn), lambda j, k: (j, 0, 0)), S((N // tn, M, tn), F32)
    else:
        out_spec, out_shape = pl.BlockSpec((M, tn), lambda j, k: (0, j)), S((M, N), F32)
    (out,), side_outs = _call(
        body, name=name, grid=(N // tn, T // tk),
        in_specs=[pl.BlockSpec((tk, M), lambda j, k: (k, 0)), pl.BlockSpec((tk, tn), lambda j, k: (k, j))],
        out_specs=[out_spec], out_shape=[out_shape], args=(a, b), sides=sides)
    return out, side_outs


PIECE = IN_SHARD // 3
N_PIECE = IN_COLS // PIECE
DPROJ_ROTATION = 2 * D // PIECE


def _merge_sgu_bwd(dx1b, proj, pa, pb, ya, yb, mb, ln_g, ln_b, sgu_ws, sgu_bs, wpa, wpb, wout, sides=()):
    T = dx1b.shape[0]
    tm = min(TM_FF, T)
    n_chunk = tm // CHUNK
    n_steps = T // tm

    def body(dx_ref, uv_ref, gab_ref, pa_ref, pb_ref, ya_ref, yb_ref, mb_ref, g_ref, b_ref, ws_ref, bs_ref,
             wpa_ref, wpb_ref, wout_ref,
             dya_ref, dp_ref, dw_ref, dbs_ref, dg_ref, db_ref, gpa_ref, gpb_ref, gout_ref,
             w_ref, wt_ref, bst_ref, acc_pa, acc_pb, acc_out):
        @pl.when(pl.program_id(0) == 0)
        def _():
            for ref in (dw_ref, dbs_ref, dg_ref, db_ref, acc_pa, acc_pb, acc_out):
                ref[...] = jnp.zeros_like(ref)
            _fill_sgu_weights(ws_ref, bs_ref, w_ref, bst_ref, wt_ref)

        dxb = dx_ref[...]
        dm = _dot_nt(dxb, wout_ref[...])
        sa = _sig(gab_ref[:, 0:D].astype(F32))
        sb = _sig(gab_ref[:, D:2 * D].astype(F32))
        dpa = (dm * sa).astype(BF16)
        dpb = (dm * sb).astype(BF16)
        acc_pa[...] += _dot_tn(ya_ref[...], dpa)
        acc_pb[...] += _dot_tn(yb_ref[...], dpb)
        acc_out[...] += _dot_tn(mb_ref[...], dxb)
        dp_ref[:, 2 * D:3 * D] = (dm * pa_ref[...].astype(F32) * (sa * (1.0 - sa))).astype(BF16)
        dp_ref[:, 3 * D:4 * D] = (dm * pb_ref[...].astype(F32) * (sb * (1.0 - sb))).astype(BF16)
        dya_ref[...] = _dot_nt(dpa, wpa_ref[...]).astype(BF16)
        dyb_v = _dot_nt(dpb, wpb_ref[...])

        gu, dgu = _gelu_and_grad(uv_ref[:, 0:D].astype(F32))
        gv, dgv = _gelu_and_grad(uv_ref[:, D:2 * D].astype(F32))
        vhat, rstd = _layer_norm_stats(gv)
        lng = g_ref[...]
        vnb = (vhat * lng + b_ref[...]).astype(BF16)
        mixed = _sgu_mix(w_ref, vnb, bst_ref, n_chunk)
        dp_ref[:, 0:D] = (dyb_v * mixed * dgu).astype(BF16)
        dmix = dyb_v * gu
        dmb = dmix.astype(BF16)
        keep = _causal_mask()
        dvn_cols, dbs_rows = [], []
        for g in range(N_GROUP):
            sl = slice(CHUNK * g, CHUNK * (g + 1))
            dmg = dmb[:, sl].reshape(n_chunk, CHUNK, CHUNK)
            vg = vnb[:, sl].reshape(n_chunk, CHUNK, CHUNK)
            wtb = jnp.broadcast_to(wt_ref[g][None], (n_chunk, CHUNK, CHUNK))
            dvn = lax.dot_general(wtb, dmg, (((2,), (1,)), ((0,), (0,))), preferred_element_type=F32)
            dvn_cols.append(dvn.reshape(tm, CHUNK))
            dw = lax.dot_general(dmg, vg, (((2,), (2,)), ((0,), (0,))), preferred_element_type=F32)
            dw_ref[g] += jnp.where(keep, jnp.sum(dw, axis=0), 0.0)
            per_token = jnp.sum(dmix[:, sl], axis=1)
            dbs_rows.append(jnp.sum(per_token.reshape(n_chunk, CHUNK), axis=0, keepdims=True))
        dbs_ref[...] += jnp.concatenate(dbs_rows, axis=0)
        dvn = jnp.concatenate(dvn_cols, axis=1)
        dg_ref[...] += _colsum(dvn * vhat)
        db_ref[...] += _colsum(dvn)
        dvhat = dvn * lng
        dgv_in = rstd * (dvhat - jnp.mean(dvhat, axis=-1, keepdims=True)
                         - vhat * jnp.mean(dvhat * vhat, axis=-1, keepdims=True))
        dp_ref[:, D:2 * D] = (dgv_in * dgv).astype(BF16)

        @pl.when(pl.program_id(0) == n_steps - 1)
        def _():
            for acc, out in ((acc_pa, gpa_ref), (acc_pb, gpb_ref), (acc_out, gout_ref)):
                pltpu.sync_copy(acc, out)

    tile = pl.BlockSpec((tm, D), lambda i: (i, 0))
    vec = pl.BlockSpec((1, D), lambda i: (0, 0))
    w = _resident((D, D))
    wsp = pl.BlockSpec((N_GROUP, CHUNK, CHUNK), lambda i: (0, 0, 0))
    acc = pltpu.VMEM((D, D), F32)
    return _call(
        body, name="merge_sgu_bwd", grid=(n_steps,),
        in_specs=[tile, pl.BlockSpec((tm, 2 * D), lambda i: (i, 1)), pl.BlockSpec((tm, 2 * D), lambda i: (i, 2)),
                  tile, tile, tile, tile, tile, vec, vec, pl.BlockSpec(sgu_ws.shape, lambda i: (0, 0, 0, 0)),
                  pl.BlockSpec(sgu_bs.shape, lambda i: (0, 0, 0)), w, w, w],
        out_specs=[tile, pl.BlockSpec((tm, 4 * D), lambda i: (i, 0)), wsp,
                   pl.BlockSpec((N_GROUP, CHUNK), lambda i: (0, 0)), vec, vec, _ANY, _ANY, _ANY],
        out_shape=[S((T, D), BF16), S((T, IN_COLS), BF16), S((N_GROUP, CHUNK, CHUNK), F32), S((N_GROUP, CHUNK), F32),
                   S((1, D), F32), S((1, D), F32), S((D, D), F32), S((D, D), F32), S((D, D), F32)],
        scratch_shapes=[_SGU_W_BF16, _SGU_W_BF16, _SGU_BT, acc, acc, acc],
        args=(dx1b, proj, proj, pa, pb, ya, yb, mb, ln_g, ln_b, sgu_ws, sgu_bs, wpa, wpb, wout), sides=sides)


def _rglru_bwd(dya, dproj, proj, hseq, xc, gates, conv_w, rg_wa, rg_wx, lam, sides=()):
    T = dya.shape[0]
    tm = min(TM_SCAN, T)
    n = T // tm
    per8 = tm // SUBLANES

    def body(dya_ref, _, rx_ref, rxp_ref, gate_ref, h_ref, hp_ref, xc_ref, gates_ref, cw_ref, wah_ref, wxh_ref,
             lam_ref, dab_ref, dcw_ref, dcb_ref, dwah_ref, dwxh_ref, dba_ref, dbx_ref, dlam_ref,
             hext, rext, dext, carry_a, carry_dh, a_s, b_s, h_s, wa_ref, wx_ref, dwa_ref, dwx_ref):
        i = pl.program_id(0)
        first_tile = i == n - 1

        @pl.when(i == 0)
        def _():
            for ref in (dcw_ref, dcb_ref, dwa_ref, dwx_ref, dba_ref, dbx_ref, dlam_ref, carry_a, carry_dh):
                ref[...] = jnp.zeros_like(ref)
            dext[tm:tm + SUBLANES, :] = jnp.zeros((SUBLANES, D), F32)
            _fill_blockdiag(wah_ref, wa_ref)
            _fill_blockdiag(wxh_ref, wx_ref)

        gel, dgel = _gelu_and_grad(gate_ref[...].astype(F32))
        dya_v = dya_ref[...].astype(F32)
        hseq_v = h_ref[...]
        dgate = dya_v * hseq_v * dgel
        xcv = xc_ref[...]
        lam_v = lam_ref[...]
        sp = _softplus_neg(lam_v)
        xb = xcv.astype(BF16)
        r, gi, a, m = (gates_ref[:, D * k:D * (k + 1)] for k in range(N_GATES))

        row = lax.broadcasted_iota(jnp.int32, (tm, D), 0)
        c = jnp.where(row == tm - 1, carry_a[0:1, :], _shift_up(a, 1, 0.0))
        dH = _scan_tile(c, dya_v * gel, carry_dh[0:1, :], a_s, b_s, h_s, reverse=True)
        carry_a[...] = jnp.broadcast_to(a[0:1, :], (SUBLANES, D))
        carry_dh[...] = jnp.broadcast_to(dH[0:1, :], (SUBLANES, D))

        hext[0:SUBLANES, :] = jnp.where(first_tile, 0.0, hp_ref[...])
        hext[SUBLANES:SUBLANES + tm, :] = hseq_v
        h_prev = hext[pl.ds(SUBLANES - 1, tm), :]

        d_m = dH * (gi * xcv)
        d_la = dH * h_prev * a - d_m * (a * a) / m
        d_ia = dH * m * xcv * (gi * (1.0 - gi))
        d_ra = d_la * ((-RG_C) * sp) * (r * (1.0 - r))
        dlam_ref[...] += _colsum(d_la * ((-RG_C) * r)) * (-_sig(-lam_v))
        dba_ref[...] += _colsum(d_ra)
        dbx_ref[...] += _colsum(d_ia)
        drab = d_ra.astype(BF16)
        diab = d_ia.astype(BF16)
        dxc_cols = []
        for j in range(N_RG_BLOCK):
            sl = slice(RG_BLOCK * j, RG_BLOCK * (j + 1))
            dxc_cols.append(_dot_nt(drab[:, sl], wa_ref[j]) + _dot_nt(diab[:, sl], wx_ref[j]))
            dwa_ref[j] += _dot_tn(xb[:, sl], drab[:, sl])
            dwx_ref[j] += _dot_tn(xb[:, sl], diab[:, sl])
        dxc = dH * m * gi + jnp.concatenate(dxc_cols, axis=1)

        dcb_ref[...] += _colsum(dxc)
        dext[0:tm, :] = dxc
        rext[0:SUBLANES, :] = jnp.where(first_tile, 0.0, rxp_ref[SUBLANES:2 * SUBLANES, :].astype(F32))
        rext[SUBLANES:SUBLANES + tm, :] = rx_ref[...].astype(F32)
        drx = jnp.zeros((tm, D), F32)
        for k in range(CONV_WIDTH):
            drx = drx + dext[pl.ds(CONV_WIDTH - 1 - k, tm), :] * cw_ref[k:k + 1, :]
            dcw_ref[k:k + 1, :] += _colsum(dxc * rext[pl.ds(SUBLANES - (CONV_WIDTH - 1) + k, tm), :])
        dext[tm:tm + SUBLANES, :] = dext[0:SUBLANES, :]
        dab_ref[:, 0:D] = drx.astype(BF16)
        dab_ref[:, D:2 * D] = dgate.astype(BF16)

        @pl.when(first_tile)
        def _():
            for j in range(N_RG_BLOCK):
                for h in range(HEADS_PER_BLOCK):
                    sl = slice(HEAD_DIM * h, HEAD_DIM * (h + 1))
                    pair, side = divmod(HEADS_PER_BLOCK * j + h, 2)
                    lanes = slice(HEAD_DIM * side, HEAD_DIM * (side + 1))
                    dwah_ref[pair, :, lanes] = dwa_ref[j, sl, sl]
                    dwxh_ref[pair, :, lanes] = dwx_ref[j, sl, sl]

    def rev(col):
        return lambda i: (n - 1 - i, col)

    def prev8(col):
        return lambda i: (jnp.maximum((n - 1 - i) * per8 - 1, 0), col)

    def prev16(col):
        return lambda i: (jnp.maximum((n - 1 - i) * (per8 // 2) - 1, 0), col)

    tile = pl.BlockSpec((tm, D), rev(0))
    vec = pl.BlockSpec((1, D), lambda i: (0, 0))
    heads_in = pl.BlockSpec(rg_wa.shape, lambda i: (0, 0, 0, 0))
    head_pairs = (rg_wa.shape[1] // 2, HEAD_DIM, 2 * HEAD_DIM)
    heads_out = pl.BlockSpec(head_pairs, lambda i: (0, 0, 0))
    cw = pl.BlockSpec((CONV_WIDTH, D), lambda i: (0, 0))
    blocks_f32 = pltpu.VMEM((N_RG_BLOCK, RG_BLOCK, RG_BLOCK), F32)
    return _call(
        body, name="rglru_bwd", grid=(n,),
        in_specs=[tile, _ANY, pl.BlockSpec((tm, D), rev(0)), pl.BlockSpec((2 * SUBLANES, D), prev16(0)),
                  pl.BlockSpec((tm, D), rev(1)), tile, pl.BlockSpec((SUBLANES, D), prev8(0)), tile,
                  pl.BlockSpec((tm, N_GATES * D), rev(0)), cw, heads_in, heads_in, vec],
        out_specs=[pl.BlockSpec((tm, 2 * D), rev(2)), cw, vec, heads_out, heads_out, vec, vec, vec],
        out_shape=[S((T, IN_COLS), BF16), S((CONV_WIDTH, D), F32), S((1, D), F32),
                   S(head_pairs, F32), S(head_pairs, F32), S((1, D), F32), S((1, D), F32), S((1, D), F32)],
        scratch_shapes=[pltpu.VMEM((tm + SUBLANES, D), F32), pltpu.VMEM((tm + SUBLANES, D), F32),
                        pltpu.VMEM((tm + SUBLANES, D), F32), pltpu.VMEM((SUBLANES, D), F32),
                        pltpu.VMEM((SUBLANES, D), F32)] + [pltpu.VMEM((D // LANES, tm, LANES), F32)] * 3
        + [_RG_BLOCKS_BF16] * 2 + [blocks_f32] * 2,
        args=(dya, dproj, proj, proj, proj, hseq, hseq, xc, gates, conv_w, rg_wa, rg_wx, lam), sides=sides,
        aliases={1: 0})


def _inproj_dh(dproj, w_in, dh, first, count, name, sides=()):
    T = dproj.shape[0]
    tm = min(TM_MM, T)

    def body(*refs):
        dp_ref, w_ref, dh_ref = refs[-3:]
        dh = jnp.zeros((tm, D), F32)
        for p in range(N_PIECE):
            shard, part = divmod((p + DPROJ_ROTATION) % N_PIECE, IN_SHARD // PIECE)
            dh = dh + _dot_nt(dp_ref[:, PIECE * p:PIECE * (p + 1)], w_ref[shard, :, PIECE * part:PIECE * (part + 1)])
        dh_ref[...] = dh

    earlier = [] if dh is None else [dh]
    return _call(
        body, name=name, grid=(count,),
        in_specs=[_ANY] * len(earlier) + [pl.BlockSpec((tm, IN_COLS), lambda i: (first + i, 0)),
                                         _resident((N_SHARD, D, IN_SHARD))],
        out_specs=[pl.BlockSpec((tm, D), lambda i: (first + i, 0))],
        out_shape=[S((T, D), F32)],
        args=(*earlier, dproj, w_in), sides=sides, aliases={0: 0} if earlier else None)


def _inproj_norm_bwd(dh, x, g, dx1):
    T = x.shape[0]
    tm = min(TM_MM, T)
    n = T // tm
    ahead = STREAM_SLOTS - 1

    def body(dh_hbm, x_hbm, g_ref, dx1_hbm, dx_ref, dgm_ref, bufs, sems):
        step = pl.program_id(0)
        sources = (dh_hbm, x_hbm, dx1_hbm)

        def fetch(tile, k):
            rows = pl.ds(pl.multiple_of(tile * tm, tm), tm)
            slot = tile % STREAM_SLOTS
            return pltpu.make_async_copy(sources[k].at[rows, :], bufs.at[k, slot], sems.at[k, slot])

        @pl.when(step == 0)
        def _():
            dgm_ref[...] = jnp.zeros_like(dgm_ref)
            for tile in range(min(ahead, n)):
                for k in range(len(sources)):
                    fetch(step + tile, k).start()

        @pl.when(step + ahead < n)
        def _():
            for k in range(len(sources)):
                fetch(step + ahead, k).start()

        for k in range(len(sources)):
            fetch(step, k).wait()
        slot = step % STREAM_SLOTS
        dx, dg_rows = _rms_bwd(bufs[0, slot], bufs[1, slot], g_ref[...])
        dgm_ref[...] += _colsum(dg_rows)
        dx_ref[...] = bufs[2, slot] + dx

    tile = pl.BlockSpec((tm, D), lambda i: (i, 0))
    vec = pl.BlockSpec((1, D), lambda i: (0, 0))
    return pl.pallas_call(
        body, name="inproj_norm_bwd", grid=(n,),
        in_specs=[_ANY, _ANY, vec, _ANY],
        out_specs=[tile, vec],
        out_shape=[S((T, D), F32), S((1, D), F32)],
        scratch_shapes=[pltpu.VMEM((3, STREAM_SLOTS, tm, D), F32), pltpu.SemaphoreType.DMA((3, STREAM_SLOTS))],
        compiler_params=_params(1),
    )(dh, x, g, dx1)


def _inproj_wgrad(h, dproj, sides=()):
    T = h.shape[0]
    tk = min(TK_WGRAD, T)
    per = IN_SHARD // PIECE

    def body(h_ref, *refs):
        pieces, o_ref = refs[:per], refs[per]

        @pl.when(pl.program_id(1) == 0)
        def _():
            o_ref[...] = jnp.zeros_like(o_ref)

        o_ref[0] += _dot_tn(h_ref[...], jnp.concatenate([p[...] for p in pieces], axis=1))

    def piece(i):
        return pl.BlockSpec((tk, PIECE), lambda j, k: (k, (per * j + i + N_PIECE - DPROJ_ROTATION) % N_PIECE))

    (out,), side_outs = _call(
        body, name="inproj_wgrad", grid=(N_SHARD, T // tk),
        in_specs=[pl.BlockSpec((tk, D), lambda j, k: (k, 0))] + [piece(i) for i in range(per)],
        out_specs=[pl.BlockSpec((1, D, IN_SHARD), lambda j, k: (j, 0, 0))],
        out_shape=[S((N_SHARD, D, IN_SHARD), F32)], args=(h,) + (dproj,) * per, sides=sides)
    return out, side_outs


def _row_tile(rows):
    for t in range(256, 0, -SUBLANES):
        if rows % t == 0:
            return t
    raise ValueError(rows)


def _add_halves(core, grads, theirs, name):
    n = len(grads)
    _, r, cols = grads[0].shape
    half = r // 2
    tr = _row_tile(half)
    nb = half // tr

    def body(core_ref, *refs):
        for g_ref, t_ref, o_ref in zip(refs[:n], refs[n:2 * n], refs[2 * n:]):
            o_ref[...] = (g_ref[...] + t_ref[...]).astype(BF16)

    blk = pl.BlockSpec((1, tr, cols), lambda s, i, core_ref: (s, i, 0))
    mine = pl.BlockSpec((1, tr, cols), lambda s, i, core_ref: (s, core_ref[0] * nb + i, 0))
    gs = pltpu.PrefetchScalarGridSpec(num_scalar_prefetch=1, grid=(N_SHARD, nb),
                                      in_specs=[mine] * n + [blk] * n, out_specs=[blk] * n)
    return pl.pallas_call(
        body, name=name, grid_spec=gs, out_shape=[S((N_SHARD, half, cols), BF16)] * n, compiler_params=_params(2),
    )(core, *grads, *theirs)


def _sum_shards(chip, owns, others, name):
    n = len(owns)
    _, half, cols = owns[0].shape
    tr = _row_tile(half)

    def body(chip_ref, *refs):
        for own_ref, oth_ref, o_ref in zip(refs[:n], refs[n:2 * n], refs[2 * n:]):
            acc = own_ref[0].astype(F32)
            for j in range(3):
                acc = acc + oth_ref[j].astype(F32)
            o_ref[...] = acc

    gs = pltpu.PrefetchScalarGridSpec(
        num_scalar_prefetch=1, grid=(half // tr,),
        in_specs=[pl.BlockSpec((1, tr, cols), lambda i, chip_ref: (chip_ref[0], i, 0))] * n
        + [pl.BlockSpec((3, tr, cols), lambda i, chip_ref: (0, i, 0))] * n,
        out_specs=[pl.BlockSpec((tr, cols), lambda i, chip_ref: (i, 0))] * n)
    return pl.pallas_call(
        body, name=name, grid_spec=gs, out_shape=[S((half, cols), F32)] * n, compiler_params=_params(1),
    )(chip, *owns, *others)


def _adamw(w, g, m, v):
    m = ADAM_B1 * m + (1.0 - ADAM_B1) * g
    v = ADAM_B2 * v + (1.0 - ADAM_B2) * (g * g)
    m_hat = m / (1.0 - ADAM_B1 ** ADAM_STEP)
    v_hat = v / (1.0 - ADAM_B2 ** ADAM_STEP)
    delta = -ADAM_LR * (m_hat / (jnp.sqrt(v_hat) + ADAM_EPS) + ADAM_WD * w)
    return delta, m, v


def _adamw_shard(core, mine, theirs, w, m, v, name):
    n = len(w)
    r, cols = w[0].shape
    half = r // 2
    tr = _row_tile(half)
    nb = half // tr

    def body(core_ref, *refs):
        groups = [refs[k * n:(k + 1) * n] for k in range(9)]
        for mine_ref, theirs_ref, w_ref, m_ref, v_ref, g_ref, d_ref, mo_ref, vo_ref in zip(*groups):
            g = jnp.where(pl.program_id(0) == core_ref[0], mine_ref[...], theirs_ref[...])
            g_ref[...] = g
            d_ref[...], mo_ref[...], vo_ref[...] = _adamw(w_ref[...], g, m_ref[...], v_ref[...])

    hblk = pl.BlockSpec((tr, cols), lambda h, i, core_ref: (i, 0))
    blk = pl.BlockSpec((tr, cols), lambda h, i, core_ref: (h * nb + i, 0))
    gs = pltpu.PrefetchScalarGridSpec(num_scalar_prefetch=1, grid=(2, nb),
                                      in_specs=[hblk] * (2 * n) + [blk] * (3 * n), out_specs=[blk] * (4 * n))
    res = pl.pallas_call(
        body, name=name, grid_spec=gs, out_shape=[S((r, cols), F32)] * (4 * n), compiler_params=_params(2),
    )(core, *mine, *theirs, *w, *m, *v)
    return [tuple(res[k * n + j] for k in range(4)) for j in range(n)]


def _adamw_whole(w, g, m, v, name):
    def body(w_ref, g_ref, m_ref, v_ref, d_ref, mo_ref, vo_ref):
        d_ref[...], mo_ref[...], vo_ref[...] = _adamw(w_ref[...], g_ref[...], m_ref[...], v_ref[...])

    return pl.pallas_call(body, name=name, out_shape=[S(w.shape, F32)] * 3)(w, g, m, v)


_VEC_ROWS = ("norm_mix_g", "conv_b", "rg_lambda", "sgu_ln_g", "sgu_ln_b", "norm_ffn_g", "norm_final_g", "rg_ba",
             "rg_bx")
_CONV_ROW = len(_VEC_ROWS)
_LOSS_ROW = _CONV_ROW + CONV_WIDTH
_VEC_PAD = -(_LOSS_ROW + 1) % SUBLANES
_HEAD_BIASES = ("rg_ba", "rg_bx")
_TENSORS = ("sgu_bs", "sgu_ws", "rg_wa", "rg_wx")
_HEAD_PAIRS = ("rg_wa", "rg_wx")


def _small_sum_adamw(parts, w, m, v):
    names = [n for n in _VEC_ROWS] + list(_TENSORS)
    n_parts = len(parts)

    def total(ref):
        acc = ref[0]
        for k in range(1, N_DEVICE):
            acc = acc + ref[k]
        return acc

    def body(*refs):
        part_refs, refs = refs[:n_parts], refs[n_parts:]
        w_refs, m_refs, v_refs = (dict(zip(names, refs[k * len(names):(k + 1) * len(names)])) for k in range(3))
        outs = refs[3 * len(names):]
        out_refs = {n: outs[4 * k:4 * k + 4] for k, n in enumerate(names)}
        conv_ref, loss_ref = outs[4 * len(names):]
        vec = total(part_refs[0])
        grads = {n: total(p) for n, p in zip(_TENSORS, part_refs[1:])}
        for n in _HEAD_PAIRS:
            pairs = grads[n]
            grads[n] = jnp.stack([pairs[k // 2, :, HEAD_DIM * (k % 2):HEAD_DIM * (k % 2 + 1)]
                                  for k in range(2 * pairs.shape[0])], axis=0)
        grads = {n: g[None] for n, g in grads.items()}
        for row, n in enumerate(_VEC_ROWS):
            g = vec[row:row + 1, :]
            if n in _HEAD_BIASES:
                g = jnp.concatenate([g[:, HEAD_DIM * h:HEAD_DIM * (h + 1)] for h in range(D // HEAD_DIM)], axis=0)[None]
            grads[n] = g
        for n in names:
            g_ref, d_ref, mo_ref, vo_ref = out_refs[n]
            g_ref[...] = grads[n]
            d_ref[...], mo_ref[...], vo_ref[...] = _adamw(w_refs[n][...], grads[n], m_refs[n][...], v_refs[n][...])
        conv_ref[...] = vec[_CONV_ROW:_CONV_ROW + CONV_WIDTH, :]
        loss_ref[...] = vec[_LOSS_ROW:_LOSS_ROW + 1, 0:1]

    res = pl.pallas_call(
        body, name="small_sum_adamw",
        out_shape=[S(w[n].shape, F32) for n in names for _ in range(4)] + [S((CONV_WIDTH, D), F32), S((1, 1), F32)],
        compiler_params=pltpu.CompilerParams(vmem_limit_bytes=VMEM_LIMIT),
    )(*parts, *[w[n] for n in names], *[m[n] for n in names], *[v[n] for n in names])
    return {n: tuple(res[4 * k:4 * k + 4]) for k, n in enumerate(names)}, res[-2], res[-1]


_BIG = ("w_in", "w_proj_a", "w_proj_b", "w_out", "w_gate_up", "w_down")
_WEIGHTS = ("norm_mix_g", "w_in", "conv_w", "conv_b", "rg_wa", "rg_ba", "rg_wx", "rg_bx", "rg_lambda", "sgu_ln_g",
            "sgu_ln_b", "sgu_ws", "sgu_bs", "w_proj_a", "w_proj_b", "w_out", "norm_ffn_g", "w_gate_up", "w_down",
            "norm_final_g")


def kernel(x, norm_mix_g, w_in, conv_w, conv_b, rg_wa, rg_ba, rg_wx, rg_bx, rg_lambda, sgu_ln_g, sgu_ln_b, sgu_ws, sgu_bs, w_proj_a, w_proj_b, w_out, norm_ffn_g, w_gate_up, w_down, norm_final_g, loss_target, m_norm_mix_g, m_w_in, m_conv_w, m_conv_b, m_rg_wa, m_rg_ba, m_rg_wx, m_rg_bx, m_rg_lambda, m_sgu_ln_g, m_sgu_ln_b, m_sgu_ws, m_sgu_bs, m_w_proj_a, m_w_proj_b, m_w_out, m_norm_ffn_g, m_w_gate_up, m_w_down, m_norm_final_g, v_norm_mix_g, v_w_in, v_conv_w, v_conv_b, v_rg_wa, v_rg_ba, v_rg_wx, v_rg_bx, v_rg_lambda, v_sgu_ln_g, v_sgu_ln_b, v_sgu_ws, v_sgu_bs, v_w_proj_a, v_w_proj_b, v_w_out, v_norm_ffn_g, v_w_gate_up, v_w_down, v_norm_final_g):
    args = dict(locals())
    w = {n: args[n] for n in _WEIGHTS}
    mom = {n: args["m_" + n] for n in _WEIGHTS}
    var = {n: args["v_" + n] for n in _WEIGHTS}
    xi, yi, ci = _position()
    core = ci.astype(jnp.int32).reshape(1)
    chip = (2 * xi + yi).astype(jnp.int32).reshape(1)

    bf = {n: w[n][0].astype(BF16) for n in _BIG}
    final_g = w["norm_final_g"].reshape(1, D)
    ba, bx = w["rg_ba"].reshape(1, D), w["rg_bx"].reshape(1, D)
    lam, ln_g, ln_b = w["rg_lambda"], w["sgu_ln_g"], w["sgu_ln_b"]
    x0, target = x[0], loss_target[0]

    def shard_major(g):
        return g.reshape(N_SHARD, g.shape[0] // N_SHARD, g.shape[1])

    def per_shape(names, fn, *lists):
        if len({a.shape for a in lists[0]}) == 1:
            return fn("_".join(names), *lists)
        return [r for k, n in enumerate(names) for r in fn(n, *[[a[k]] for a in lists])]

    def chip_sums(names, grads, theirs):
        return per_shape(names, lambda s, g, t: _add_halves(core, g, t, "add_halves_" + s), grads, theirs)

    def my_halves(names, sums, arrived):
        return per_shape(names, lambda s, p, a: _sum_shards(chip, p, a, "sum_shards_" + s), sums, arrived)

    (proj, h), ((w_in_a,), (conv_a,)) = _inproj_own(
        chip, x0, w["norm_mix_g"], bf["w_in"],
        sides=[_gather_half_side([bf["w_in"]], relations=(0, 1)), _gather_side([w["conv_w"][0]])])
    conv_cols = conv_a.shape[-1]
    conv_full = jnp.swapaxes(conv_a, 0, 1).reshape(CONV_WIDTH, D)
    (proj,), ((w_in_a,),) = _inproj_rest(
        chip, proj, h, w_in_a, 1, 2, "inproj_near",
        sides=[_gather_half_side([bf["w_in"]], relations=(2,), into=[w_in_a])])
    (proj,), _ = _inproj_rest(chip, proj, h, w_in_a, 3, 1, "inproj_far")
    (ya, xc, hseq, gates), ((w_pa_a, w_pb_a, w_out_a, w_gu_a, w_down_a),) = _rglru_fwd(
        proj, conv_full, w["conv_b"], w["rg_wa"], w["rg_wx"], ba, bx, lam,
        sides=[_gather_half_side([bf[n] for n in ("w_proj_a", "w_proj_b", "w_out", "w_gate_up", "w_down")])])
    wpa, wpb, wout, wdown = w_pa_a.reshape(D, D), w_pb_a.reshape(D, D), w_out_a.reshape(D, D), w_down_a.reshape(D_FF, D)
    yb, pa, pb, mb, x1 = _sgu_merge_fwd(x0, proj, ya, ln_g, ln_b, w["sgu_ws"], w["sgu_bs"], wpa, wpb, wout)
    h2, gu, act, dx2, dx2b, loss, d_final_g = _ffn_fwd_loss(x1, w["norm_ffn_g"], w_gu_a, wdown, final_g, target)

    dgu, dx1, dx1b, d_ffn_g = _ffn_bwd(dx2, dx2b, gu, wdown, w_gu_a, x1, w["norm_ffn_g"])
    ffn = ("w_gate_up", "w_down")
    g_ffn = [_matmul_tn(h2, dgu, FF_SHARD, True, "wgrad_gate_up")[0],
             shard_major(_matmul_tn(act, dx2b, D // 2, False, "wgrad_down")[0])]
    (dya, dproj, d_ws, d_bs, d_lng, d_lnb, g_pa, g_pb, g_out), (theirs_ffn,) = _merge_sgu_bwd(
        dx1b, proj, pa, pb, ya, yb, mb, ln_g, ln_b, w["sgu_ws"], w["sgu_bs"], wpa, wpb, wout,
        sides=[_halves_side(g_ffn)])
    sums_ffn = chip_sums(ffn, g_ffn, theirs_ffn)
    mix = ("w_proj_a", "w_proj_b", "w_out")
    g_mix = [shard_major(g) for g in (g_pa, g_pb, g_out)]
    (dproj, d_cw, d_cb, d_wa, d_wx, d_ba, d_bx, d_lam), (arrived_ffn, theirs_mix, sgu_parts) = _rglru_bwd(
        dya, dproj, proj, hseq, xc, gates, conv_full, w["rg_wa"], w["rg_wx"], lam,
        sides=[_scatter_side(sums_ffn), _halves_side(g_mix), _everyone_side([d_bs, d_ws])])
    mine_ffn = my_halves(ffn, sums_ffn, arrived_ffn)
    sums_mix = chip_sums(mix, g_mix, theirs_mix)
    g_in, ((wa_parts, wx_parts), other_ffn, arrived_mix) = _inproj_wgrad(
        h, dproj, sides=[_everyone_side([d_wa, d_wx]), _swap_side(mine_ffn),
                         _scatter_side(sums_mix)])
    mine_mix = my_halves(mix, sums_mix, arrived_mix)
    n_tiles = x0.shape[0] // min(TM_MM, x0.shape[0])
    n_first = max(1, n_tiles * 3 // 8)
    (dh,), (theirs_in, other_mix) = _inproj_dh(dproj, w_in_a, None, 0, n_first, "inproj_dh_a",
                                               sides=[_halves_side([g_in]), _swap_side(mine_mix)])
    sums_in = chip_sums(("w_in",), [g_in], theirs_in)
    (dh,), (arrived_in,) = _inproj_dh(dproj, w_in_a, dh, n_first, n_tiles - n_first, "inproj_dh_b",
                                      sides=[_scatter_side(sums_in)])
    mine_in = my_halves(("w_in",), sums_in, arrived_in)
    grad_x, d_mix_g = _inproj_norm_bwd(dh, x0, w["norm_mix_g"], dx1)
    rows = {"norm_mix_g": d_mix_g, "conv_b": d_cb, "rg_lambda": d_lam, "sgu_ln_g": d_lng, "sgu_ln_b": d_lnb,
            "norm_ffn_g": d_ffn_g, "norm_final_g": d_final_g, "rg_ba": d_ba, "rg_bx": d_bx}
    vec = jnp.concatenate([rows[n] for n in _VEC_ROWS]
                          + [d_cw, jnp.pad(loss, ((0, _VEC_PAD), (0, D - 1)))], axis=0)
    other_in, (vec_parts,) = _comm_only([_swap_side(mine_in), _everyone_side([vec])], "swap_w_in")
    small_parts = [vec_parts] + sgu_parts + [wa_parts, wx_parts]

    out = {}
    for names, gm, go in ((ffn, mine_ffn, other_ffn), (mix, mine_mix, other_mix), (("w_in",), mine_in, other_in)):
        results = per_shape(
            names, lambda s, *lists: _adamw_shard(core, *lists, "adamw_" + s),
            gm, go, [w[n][0] for n in names], [mom[n][0] for n in names], [var[n][0] for n in names])
        for n, res in zip(names, results):
            out[n] = tuple(a[None] for a in res)
    as_row = lambda t: {n: a.reshape(1, D) if n == "norm_final_g" else a for n, a in t.items()}
    small_out, conv_sum, loss_sum = _small_sum_adamw(small_parts, as_row(w), as_row(mom), as_row(var))
    out.update(small_out)
    out["norm_final_g"] = tuple(a.reshape(D) for a in small_out["norm_final_g"])
    conv_g = lax.dynamic_slice_in_dim(conv_sum, chip[0] * conv_cols, conv_cols, axis=1)
    d, mo, vo = _adamw_whole(w["conv_w"][0], conv_g, mom["conv_w"][0], var["conv_w"][0], "adamw_conv_w")
    out["conv_w"] = tuple(a[None] for a in (conv_g, d, mo, vo))

    return (loss_sum[0, 0], grad_x[None], *[out[n][0] for n in _WEIGHTS], *[out[n][1] for n in _WEIGHTS],
            *[out[n][2] for n in _WEIGHTS], *[out[n][3] for n in _WEIGHTS])
```

```python
import functools

import jax
import jax.numpy as jnp
from jax import lax
from jax.experimental import pallas as pl
from jax.experimental.pallas import tpu as pltpu

F32 = jnp.float32
BF16 = jnp.bfloat16
S = jax.ShapeDtypeStruct

D = 1024
N_SHARD = 4
IN_COLS = 6 * D
IN_SHARD = IN_COLS // N_SHARD
D_FF = 2816
FF_SHARD = 2 * D_FF // N_SHARD
RG_BLOCK = 256
N_RG_BLOCK = D // RG_BLOCK
CHUNK = 128
N_GROUP = 8
CONV_WIDTH = 4
RG_C = 8.0
EPS = 1e-6
ADAM_LR, ADAM_B1, ADAM_B2, ADAM_EPS, ADAM_WD, ADAM_STEP = 0.001, 0.9, 0.999, 1e-08, 0.01, 10

V7X_VMEM_BYTES = 64 * 1024 * 1024
VMEM_LIMIT = V7X_VMEM_BYTES * 3 // 4
SUBLANES = 8
MESH = pl.DeviceIdType.MESH

TM_MM = 512
TM_SCAN = 256
TM_FF = 256
TK_WGRAD = 2048


def _params(n_axes):
    return pltpu.CompilerParams(dimension_semantics=("arbitrary",) * n_axes, vmem_limit_bytes=VMEM_LIMIT)


def _resident(shape):
    nd = len(shape)
    return pl.BlockSpec(shape, lambda *_: (0,) * nd, pipeline_mode=pl.Buffered(1))


def _sig(x):
    return 1.0 / (1.0 + jnp.exp(-x))


_GELU_K2 = 2.0 * 0.7978845608028654
_GELU_C = 0.044715


def _gelu(x):
    return x * _sig(x * (_GELU_K2 + (_GELU_K2 * _GELU_C) * (x * x)))


def _gelu_and_grad(x):
    x2 = x * x
    s = _sig(x * (_GELU_K2 + (_GELU_K2 * _GELU_C) * x2))
    g = x * s
    return g, s + g * (1.0 - s) * (_GELU_K2 + (3.0 * _GELU_K2 * _GELU_C) * x2)


_EXPM1_SERIES = tuple(1.0 / f for f in (5040.0, 720.0, 120.0, 24.0, 6.0, 2.0, 1.0))


def _one_minus_exp(x):
    p = _EXPM1_SERIES[0]
    for coef in _EXPM1_SERIES[1:]:
        p = p * x + coef
    return jnp.where(x > -0.125, -x * p, 1.0 - jnp.exp(x))


def _softplus_neg(lam):
    z = -lam
    e = jnp.exp(-jnp.abs(z))
    u = 1.0 + e
    log1p = jnp.where(u == 1.0, e, jnp.log(u) * e / (u - 1.0))
    return jnp.maximum(z, 0.0) + log1p


def _rms_stats(x):
    return lax.rsqrt(jnp.mean(x * x, axis=-1, keepdims=True) + EPS)


def _rms_bwd(dy, x, g):
    rstd = _rms_stats(x)
    xhat = x * rstd
    dxhat = dy * g
    dx = rstd * (dxhat - xhat * jnp.mean(dxhat * xhat, axis=-1, keepdims=True))
    return dx, dy * xhat


def _colsum(x):
    return jnp.sum(x, axis=0, keepdims=True)


def _shift_down(x, d, fill):
    n = x.shape[0]
    if d % SUBLANES == 0:
        return jnp.concatenate([jnp.full((d, x.shape[1]), fill, x.dtype), x[:n - d]], axis=0)
    row = lax.broadcasted_iota(jnp.int32, x.shape, 0)
    return jnp.where(row < d, fill, pltpu.roll(x, d, 0))


def _shift_up(x, d, fill):
    n = x.shape[0]
    if d % SUBLANES == 0:
        return jnp.concatenate([x[d:], jnp.full((d, x.shape[1]), fill, x.dtype)], axis=0)
    row = lax.broadcasted_iota(jnp.int32, x.shape, 0)
    return jnp.where(row >= n - d, fill, pltpu.roll(x, n - d, 0))


def _scan(a, b, shift):
    d = 1
    while d < a.shape[0]:
        b = a * shift(b, d, 0.0) + b
        a = a * shift(a, d, 1.0)
        d *= 2
    return a, b


LANES = 128


def _scan_tile(a, b, outside, a_s, b_s, h_s, reverse):
    tm = a.shape[0]
    groups = tm // SUBLANES
    order = list(range(SUBLANES - 1, -1, -1) if reverse else range(SUBLANES))
    shift = _shift_up if reverse else _shift_down
    edge = groups - 1 if reverse else 0
    for j in range(D // LANES):
        a_s[j] = a[:, LANES * j:LANES * (j + 1)]
        b_s[j] = b[:, LANES * j:LANES * (j + 1)]
    for j in range(D // LANES):
        def slab(ref, k):
            return ref[j, pl.ds(k, groups, stride=SUBLANES), :]

        ga, gb = slab(a_s, order[0]), slab(b_s, order[0])
        for k in order[1:]:
            ak = slab(a_s, k)
            gb = ak * gb + slab(b_s, k)
            ga = ak * ga
        ga, gb = _scan(ga, gb, shift)
        h_out = outside[:, LANES * j:LANES * (j + 1)]
        group_end = ga * h_out + gb
        row = lax.broadcasted_iota(jnp.int32, (groups, LANES), 0)
        h = jnp.where(row == edge, h_out, shift(group_end, 1, 0.0))
        for k in order:
            h = slab(a_s, k) * h + slab(b_s, k)
            h_s[j, pl.ds(k, groups, stride=SUBLANES), :] = h
    return jnp.concatenate([h_s[j] for j in range(D // LANES)], axis=1)


def _dot(a, b):
    return jnp.dot(a, b, preferred_element_type=F32)


def _dot_nt(a, b):
    return lax.dot_general(a, b, (((1,), (1,)), ((), ())), preferred_element_type=F32)


def _dot_tn(a, b):
    return lax.dot_general(a, b, (((0,), (0,)), ((), ())), preferred_element_type=F32)


_ANY = pl.BlockSpec(memory_space=pl.ANY)


def _position():
    return lax.axis_index("x"), lax.axis_index("y"), lax.axis_index("c")


def _other_chips(x, y):
    return [(1 - x, y), (x, 1 - y), (1 - x, 1 - y)]


class _Side:
    def __init__(self, inputs, out_shapes, n_sems, make, continues=()):
        self.inputs, self.out_shapes, self.n_sems, self.make = list(inputs), list(out_shapes), n_sems, make
        self.continues = list(continues)


MID_STEP = 0.625
LATE_MID_STEP = 0.875


def _call(body, *, name, grid, in_specs, out_specs, out_shape, args, scratch_shapes=(), sides=(), aliases=None,
          scalars=None, mid=MID_STEP):
    n_in, n_out, n_scr = len(in_specs), len(out_specs), len(scratch_shapes)
    n_scalar = 0 if scalars is None else 1
    side_in = [len(s.inputs) + len(s.continues) for s in sides]
    side_out = [len(s.out_shapes) for s in sides]
    all_aliases = {k + n_scalar: v for k, v in (aliases or {}).items()}
    for idx, s in enumerate(sides):
        for k in range(len(s.continues)):
            operand = n_scalar + n_in + sum(side_in[:idx]) + len(s.inputs) + k
            all_aliases[operand] = n_out + sum(side_out[:idx]) + k

    def wrapped(*refs):
        refs = list(refs)
        take = lambda k: [refs.pop(0) for _ in range(k)]
        ins = take(n_scalar) + take(n_in)
        sins = [take(k) for k in side_in]
        outs = take(n_out)
        souts = [take(k) for k in side_out]
        scr = take(n_scr)
        sems = [take(3) for _ in sides]
        def run(phase):
            for s, si, so, sem in zip(sides, sins, souts, sems):
                for thunk in s.make(si[:len(s.inputs)], so, *sem)[phase]:
                    thunk()

        if sides:
            n_steps = functools.reduce(lambda a, b: a * b, grid)
            step = functools.reduce(lambda a, b: a + b, [
                pl.program_id(a) * functools.reduce(lambda p, q: p * q, grid[a + 1:], 1) for a in range(len(grid))])
            pl.when(step == 0)(lambda: run(0))
        body(*ins, *outs, *scr)
        if sides:
            pl.when(step == int(mid * (n_steps - 1)))(lambda: run(1))
            pl.when(step == n_steps - 1)(lambda: run(2))

    grid_spec = pltpu.PrefetchScalarGridSpec(
        num_scalar_prefetch=n_scalar, grid=grid,
        in_specs=list(in_specs) + [_ANY] * sum(side_in),
        out_specs=list(out_specs) + [_ANY] * sum(side_out),
        scratch_shapes=list(scratch_shapes) + [pltpu.SemaphoreType.DMA((s.n_sems,)) for s in sides for _ in range(3)])
    res = pl.pallas_call(
        wrapped, name=name, grid_spec=grid_spec,
        out_shape=list(out_shape) + [o for s in sides for o in s.out_shapes],
        input_output_aliases=all_aliases,
        compiler_params=_params(len(grid)),
    )(*([scalars] if n_scalar else []), *args, *[a for s in sides for a in s.inputs + s.continues])
    main, rest, per_side = list(res[:n_out]), list(res[n_out:]), []
    for k in side_out:
        per_side.append(rest[:k])
        rest = rest[k:]
    return main, per_side


def _comm_only(sides, name):
    def body():
        pass

    return _call(body, name=name, grid=(1,), in_specs=[], out_specs=[], out_shape=[], args=[], sides=sides)[1]


def _remote(src, dst, send, recv, k, device):
    return pltpu.make_async_remote_copy(src_ref=src, dst_ref=dst, send_sem=send.at[k], recv_sem=recv.at[k],
                                        device_id=device, device_id_type=MESH)


def _both_ways(copy, keys):
    return [lambda k=k: copy(k).start() for k in keys], [], [lambda k=k: copy(k).wait() for k in keys]


def _gather_side(shards):
    n = len(shards)

    def make(ins, outs, send, recv, local):
        x, y, c = _position()
        mine = 2 * x + y
        chips = _other_chips(x, y)
        pairs = [(w, j) for w in range(n) for j in range(3)]

        def own(w):
            return pltpu.make_async_copy(ins[w], outs[w].at[mine], local.at[w])

        def push(w, j):
            return _remote(ins[w], outs[w].at[mine], send, recv, 3 * w + j, (*chips[j], c))

        def arrival(w, j):
            px, py = chips[j]
            return _remote(ins[w], outs[w].at[2 * px + py], send, recv, 3 * w + j, (px, py, c))

        starts = [lambda w=w: own(w).start() for w in range(n)] + [lambda w=w, j=j: push(w, j).start() for w, j in pairs]
        waits = ([lambda w=w, j=j: arrival(w, j).wait_recv() for w, j in pairs]
                 + [lambda w=w, j=j: push(w, j).wait_send() for w, j in pairs]
                 + [lambda w=w: own(w).wait() for w in range(n)])
        return starts, [], waits

    return _Side(shards, [S((N_SHARD,) + s.shape, s.dtype) for s in shards], 3 * n, make)


def _gather_half_side(shards, relations=(0, 1, 2), into=None):
    n = len(shards)

    def make(ins, outs, send, recv, local):
        x, y, c = _position()
        mine = 2 * x + y
        chips = _other_chips(x, y)
        pairs = [(w, j) for w in range(n) for j in relations]

        def rows(w, core):
            half = ins[w].shape[0] // 2
            return pl.ds(core * half, half)

        def own(w):
            return pltpu.make_async_copy(ins[w], outs[w].at[mine], local.at[w])

        def push(w, j):
            return _remote(ins[w].at[rows(w, c), :], outs[w].at[mine, rows(w, c), :], send, recv, 3 * w + j,
                           (*chips[j], c))

        def landed(w, j, core):
            px, py = chips[j]
            return outs[w].at[2 * px + py, rows(w, core), :]

        def arrival(w, j):
            return _remote(ins[w].at[rows(w, c), :], landed(w, j, c), send, recv, 3 * w + j, (*chips[j], c))

        def passed(w, j, core):
            return _remote(landed(w, j, core), landed(w, j, core), send, recv, 3 * n + 3 * w + j, (x, y, 1 - c))

        owns = range(n) if into is None else ()
        starts = [lambda w=w: own(w).start() for w in owns] + [lambda w=w, j=j: push(w, j).start() for w, j in pairs]
        mids = [t for w, j in pairs for t in (lambda w=w, j=j: arrival(w, j).wait_recv(),
                                              lambda w=w, j=j: passed(w, j, c).start())]
        waits = ([lambda w=w, j=j: passed(w, j, 1 - c).wait_recv() for w, j in pairs]
                 + [lambda w=w, j=j: passed(w, j, c).wait_send() for w, j in pairs]
                 + [lambda w=w, j=j: push(w, j).wait_send() for w, j in pairs]
                 + [lambda w=w: own(w).wait() for w in owns])
        return starts, mids, waits

    return _Side(shards, [S((N_SHARD,) + s.shape, s.dtype) for s in shards], 6 * n, make, continues=into or ())


def _halves_side(grads):
    n = len(grads)

    def make(ins, outs, send, recv, local):
        x, y, c = _position()

        def copy(w):
            half = ins[w].shape[1] // 2
            return _remote(ins[w].at[:, pl.ds((1 - c) * half, half), :], outs[w], send, recv, w, (x, y, 1 - c))

        return _both_ways(copy, range(n))

    return _Side(grads, [S((N_SHARD, g.shape[1] // 2, g.shape[2]), F32) for g in grads], n, make)


def _scatter_side(partials):
    n = len(partials)

    def make(ins, outs, send, recv, local):
        x, y, c = _position()
        chips = _other_chips(x, y)

        def copy(k):
            w, j = divmod(k, 3)
            px, py = chips[j]
            return _remote(ins[w].at[2 * px + py], outs[w].at[j], send, recv, k, (px, py, c))

        return _both_ways(copy, range(3 * n))

    return _Side(partials, [S((3,) + p.shape[1:], p.dtype) for p in partials], 3 * n, make)


def _swap_side(halves):
    n = len(halves)

    def make(ins, outs, send, recv, local):
        x, y, c = _position()
        return _both_ways(lambda w: _remote(ins[w], outs[w], send, recv, w, (x, y, 1 - c)), range(n))

    return _Side(halves, [S(h.shape, h.dtype) for h in halves], n, make)


N_DEVICE = 8


def _everyone_side(arrays):
    n = len(arrays)
    peers = N_DEVICE - 1

    def make(ins, outs, send, recv, local):
        x, y, c = _position()
        mine = 4 * x + 2 * y + c
        pairs = [(w, k) for w in range(n) for k in range(1, N_DEVICE)]

        def peer(k):
            return (1 - x if k & 4 else x, 1 - y if k & 2 else y, 1 - c if k & 1 else c)

        def own(w):
            return pltpu.make_async_copy(ins[w], outs[w].at[mine], local.at[w])

        def push(w, k):
            return _remote(ins[w], outs[w].at[mine], send, recv, peers * w + k - 1, peer(k))

        def arrival(w, k):
            px, py, pc = peer(k)
            return _remote(ins[w], outs[w].at[4 * px + 2 * py + pc], send, recv, peers * w + k - 1, (px, py, pc))

        starts = [lambda w=w: own(w).start() for w in range(n)] + [lambda w=w, k=k: push(w, k).start() for w, k in pairs]
        waits = ([lambda w=w, k=k: arrival(w, k).wait_recv() for w, k in pairs]
                 + [lambda w=w, k=k: push(w, k).wait_send() for w, k in pairs]
                 + [lambda w=w: own(w).wait() for w in range(n)])
        return starts, [], waits

    return _Side(arrays, [S((N_DEVICE,) + a.shape, a.dtype) for a in arrays], peers * n, make)


def _inproj_own(chip, x, g, w_shard, sides=()):
    T = x.shape[0]
    tm = min(TM_MM, T)

    def body(chip_ref, x_ref, g_ref, w_ref, proj_ref, h_ref):
        xv = x_ref[...]
        h = (xv * _rms_stats(xv) * g_ref[...]).astype(BF16)
        h_ref[...] = h
        proj_ref[...] = _dot(h, w_ref[...]).astype(BF16)

    return _call(
        body, name="inproj_own", grid=(T // tm,),
        in_specs=[pl.BlockSpec((tm, D), lambda i, c: (i, 0)), pl.BlockSpec((1, D), lambda i, c: (0, 0)),
                  pl.BlockSpec((D, IN_SHARD), lambda i, c: (0, 0), pipeline_mode=pl.Buffered(1))],
        out_specs=[pl.BlockSpec((tm, IN_SHARD), lambda i, c: (i, c[0])), pl.BlockSpec((tm, D), lambda i, c: (i, 0))],
        out_shape=[S((T, IN_COLS), BF16), S((T, D), BF16)],
        args=(x, g, w_shard), sides=sides, scalars=chip, mid=LATE_MID_STEP)


def _inproj_rest(chip, proj, h, w_in, first, count, name, sides=(), mid=MID_STEP):
    T = h.shape[0]
    tm = min(TM_MM, T)

    def body(chip_ref, _, h_ref, w_ref, proj_ref):
        proj_ref[...] = _dot(h_ref[...], w_ref[0]).astype(BF16)

    def other(p, c):
        return jnp.bitwise_xor(c[0], first + p)

    return _call(
        body, name=name, grid=(count, T // tm),
        in_specs=[_ANY, pl.BlockSpec((tm, D), lambda p, i, c: (i, 0)),
                  pl.BlockSpec((1, D, IN_SHARD), lambda p, i, c: (other(p, c), 0, 0))],
        out_specs=[pl.BlockSpec((tm, IN_SHARD), lambda p, i, c: (i, other(p, c)))],
        out_shape=[S((T, IN_COLS), BF16)],
        args=(proj, h, w_in), sides=sides, scalars=chip, aliases={0: 0}, mid=mid)


def _rg_gates(xc, wa_ref, wx_ref, ba, bx, sp):
    xb = xc.astype(BF16)
    blocks = [xb[:, RG_BLOCK * j:RG_BLOCK * (j + 1)] for j in range(N_RG_BLOCK)]
    r = _sig(jnp.concatenate([_dot(blocks[j], wa_ref[j]) for j in range(N_RG_BLOCK)], axis=1) + ba)
    gi = _sig(jnp.concatenate([_dot(blocks[j], wx_ref[j]) for j in range(N_RG_BLOCK)], axis=1) + bx)
    log_a = (-RG_C) * r * sp
    a = jnp.exp(log_a)
    m = jnp.sqrt(_one_minus_exp(2.0 * log_a))
    return xb, r, gi, a, m


N_GATES = 4
HEADS_PER_BLOCK = 4
HEAD_DIM = RG_BLOCK // HEADS_PER_BLOCK
_RG_BLOCKS_BF16 = pltpu.VMEM((N_RG_BLOCK, RG_BLOCK, RG_BLOCK), BF16)


def _fill_blockdiag(heads_ref, blocks):
    blocks[...] = jnp.zeros_like(blocks)
    for j in range(N_RG_BLOCK):
        for h in range(HEADS_PER_BLOCK):
            sl = slice(HEAD_DIM * h, HEAD_DIM * (h + 1))
            blocks[j, sl, sl] = heads_ref[0, HEADS_PER_BLOCK * j + h].astype(BF16)


def _rglru_fwd(proj, conv_w, conv_b, rg_wa, rg_wx, ba, bx, lam, sides=()):
    T = proj.shape[0]
    tm = min(TM_SCAN, T)

    def body(rx_ref, gate_ref, cw_ref, cb_ref, wah_ref, wxh_ref, ba_ref, bx_ref, lam_ref,
             ya_ref, xc_ref, h_ref, gates_ref, ext, hc, a_s, b_s, h_s, wa_ref, wx_ref):
        @pl.when(pl.program_id(0) == 0)
        def _():
            ext[0:SUBLANES, :] = jnp.zeros((SUBLANES, D), F32)
            hc[...] = jnp.zeros((SUBLANES, D), F32)
            _fill_blockdiag(wah_ref, wa_ref)
            _fill_blockdiag(wxh_ref, wx_ref)

        ext[SUBLANES:SUBLANES + tm, :] = rx_ref[...].astype(F32)
        xc = cb_ref[...]
        for k in range(CONV_WIDTH):
            xc = xc + ext[pl.ds(SUBLANES - (CONV_WIDTH - 1) + k, tm), :] * cw_ref[k:k + 1, :]
        ext[0:SUBLANES, :] = ext[tm:tm + SUBLANES, :]
        xc_ref[...] = xc
        _, r, gi, a, m = _rg_gates(xc, wa_ref, wx_ref, ba_ref[...], bx_ref[...], _softplus_neg(lam_ref[...]))
        for k, val in enumerate((r, gi, a, m)):
            gates_ref[:, D * k:D * (k + 1)] = val
        h = _scan_tile(a, m * (gi * xc), hc[0:1, :], a_s, b_s, h_s, reverse=False)
        hc[...] = jnp.broadcast_to(h[tm - 1:tm, :], (SUBLANES, D))
        h_ref[...] = h
        ya_ref[...] = (_gelu(gate_ref[...].astype(F32)) * h).astype(BF16)

    vec = pl.BlockSpec((1, D), lambda i: (0, 0))
    heads = pl.BlockSpec(rg_wa.shape, lambda i: (0, 0, 0, 0))
    tile = pl.BlockSpec((tm, D), lambda i: (i, 0))
    return _call(
        body, name="rglru_fwd", grid=(T // tm,),
        in_specs=[pl.BlockSpec((tm, D), lambda i: (i, 0)), pl.BlockSpec((tm, D), lambda i: (i, 1)),
                  pl.BlockSpec((CONV_WIDTH, D), lambda i: (0, 0)), vec, heads, heads, vec, vec, vec],
        out_specs=[tile, tile, tile, pl.BlockSpec((tm, N_GATES * D), lambda i: (i, 0))],
        out_shape=[S((T, D), BF16), S((T, D), F32), S((T, D), F32), S((T, N_GATES * D), F32)],
        scratch_shapes=[pltpu.VMEM((tm + SUBLANES, D), F32), pltpu.VMEM((SUBLANES, D), F32)]
        + [pltpu.VMEM((D // LANES, tm, LANES), F32)] * 3 + [_RG_BLOCKS_BF16] * 2,
        args=(proj, proj, conv_w, conv_b, rg_wa, rg_wx, ba, bx, lam), sides=sides, mid=LATE_MID_STEP)


def _layer_norm_stats(v):
    mu = jnp.mean(v, axis=-1, keepdims=True)
    vc = v - mu
    rstd = lax.rsqrt(jnp.mean(vc * vc, axis=-1, keepdims=True) + EPS)
    return vc * rstd, rstd


def _sgu_mix(w_ref, vnb, bst_ref, n_chunk):
    cols = []
    for g in range(N_GROUP):
        vg = vnb[:, CHUNK * g:CHUNK * (g + 1)].reshape(n_chunk, CHUNK, CHUNK)
        wb = jnp.broadcast_to(w_ref[g][None], (n_chunk, CHUNK, CHUNK))
        mg = lax.dot_general(wb, vg, (((2,), (1,)), ((0,), (0,))), preferred_element_type=F32)
        mg = mg + bst_ref[:, g:g + 1][None]
        cols.append(mg.reshape(n_chunk * CHUNK, CHUNK))
    return jnp.concatenate(cols, axis=1)


def _causal_mask():
    return (lax.broadcasted_iota(jnp.int32, (CHUNK, CHUNK), 0) >= lax.broadcasted_iota(jnp.int32, (CHUNK, CHUNK), 1))


def _fill_sgu_weights(ws_ref, bs_ref, w_tril, bs_t, w_tril_t=None):
    keep = _causal_mask()
    for g in range(N_GROUP):
        wg = jnp.where(keep, ws_ref[0, g], 0.0)
        w_tril[g] = wg.astype(BF16)
        if w_tril_t is not None:
            w_tril_t[g] = wg.T.astype(BF16)
    bs_t[...] = bs_ref[0].T


_SGU_W_BF16 = pltpu.VMEM((N_GROUP, CHUNK, CHUNK), BF16)
_SGU_BT = pltpu.VMEM((CHUNK, N_GROUP), F32)


def _sgu_merge_fwd(x, proj, ya, ln_g, ln_b, sgu_ws, sgu_bs, wpa, wpb, wout):
    T = x.shape[0]
    tm = min(TM_MM, T)
    n_chunk = tm // CHUNK

    def body(x_ref, uv_ref, gab_ref, ya_ref, g_ref, b_ref, ws_ref, bs_ref, wpa_ref, wpb_ref, wout_ref,
             yb_ref, pa_ref, pb_ref, mb_ref, x1_ref, w_ref, bst_ref):
        @pl.when(pl.program_id(0) == 0)
        def _():
            _fill_sgu_weights(ws_ref, bs_ref, w_ref, bst_ref)

        vhat, _ = _layer_norm_stats(_gelu(uv_ref[:, D:2 * D].astype(F32)))
        vnb = (vhat * g_ref[...] + b_ref[...]).astype(BF16)
        yb = (_gelu(uv_ref[:, 0:D].astype(F32)) * _sgu_mix(w_ref, vnb, bst_ref, n_chunk)).astype(BF16)
        yb_ref[...] = yb
        pa = _dot(ya_ref[...], wpa_ref[...])
        pb = _dot(yb, wpb_ref[...])
        pa_ref[...] = pa.astype(BF16)
        pb_ref[...] = pb.astype(BF16)
        mb = (_sig(gab_ref[:, 0:D].astype(F32)) * pa + _sig(gab_ref[:, D:2 * D].astype(F32)) * pb).astype(BF16)
        mb_ref[...] = mb
        x1_ref[...] = x_ref[...] + _dot(mb, wout_ref[...])

    tile = pl.BlockSpec((tm, D), lambda i: (i, 0))
    vec = pl.BlockSpec((1, D), lambda i: (0, 0))
    w = _resident((D, D))
    return pl.pallas_call(
        body, name="sgu_merge_fwd", grid=(T // tm,),
        in_specs=[tile, pl.BlockSpec((tm, 2 * D), lambda i: (i, 1)), pl.BlockSpec((tm, 2 * D), lambda i: (i, 2)), tile,
                  vec, vec, pl.BlockSpec(sgu_ws.shape, lambda i: (0, 0, 0, 0)),
                  pl.BlockSpec(sgu_bs.shape, lambda i: (0, 0, 0)), w, w, w],
        out_specs=[tile, tile, tile, tile, tile],
        out_shape=[S((T, D), BF16), S((T, D), BF16), S((T, D), BF16), S((T, D), BF16), S((T, D), F32)],
        scratch_shapes=[_SGU_W_BF16, _SGU_BT],
        compiler_params=_params(1),
    )(x, proj, proj, ya, ln_g, ln_b, sgu_ws, sgu_bs, wpa, wpb, wout)


def _ffn_fwd_loss(x1, g, w_gu, w_down, g_final, target):
    T = x1.shape[0]
    tm = min(TM_FF, T)

    def body(x_ref, g_ref, wgu_ref, wd_ref, gf_ref, t_ref,
             h2_ref, gu_ref, act_ref, dx2_ref, dx2b_ref, loss_ref, dg_ref):
        @pl.when(pl.program_id(0) == 0)
        def _():
            loss_ref[...] = jnp.zeros_like(loss_ref)
            dg_ref[...] = jnp.zeros_like(dg_ref)

        xv = x_ref[...]
        h2 = (xv * _rms_stats(xv) * g_ref[...]).astype(BF16)
        h2_ref[...] = h2
        x2 = xv
        for k in range(N_SHARD // 2):
            cols = slice(FF_SHARD * k, FF_SHARD * (k + 1))
            gate = _dot(h2, wgu_ref[k])
            up = _dot(h2, wgu_ref[k + N_SHARD // 2])
            gu_ref[:, cols] = gate.astype(BF16)
            gu_ref[:, D_FF + FF_SHARD * k:D_FF + FF_SHARD * (k + 1)] = up.astype(BF16)
            act = (gate * _sig(gate) * up).astype(BF16)
            act_ref[:, cols] = act
            x2 = x2 + _dot(act, wd_ref[cols, :])
        gf = gf_ref[...]
        err = x2 * _rms_stats(x2) * gf - t_ref[...]
        loss_ref[...] += 0.5 * jnp.sum(jnp.mean(err * err, axis=-1, keepdims=True), axis=0, keepdims=True)
        dx2, dg_rows = _rms_bwd(err * (1.0 / D), x2, gf)
        dg_ref[...] += _colsum(dg_rows)
        dx2_ref[...] = dx2
        dx2b_ref[...] = dx2.astype(BF16)

    tile = pl.BlockSpec((tm, D), lambda i: (i, 0))
    vec = pl.BlockSpec((1, D), lambda i: (0, 0))
    return pl.pallas_call(
        body, name="ffn_fwd_loss", grid=(T // tm,),
        in_specs=[tile, vec, _resident((N_SHARD, D, FF_SHARD)), _resident((D_FF, D)), vec, tile],
        out_specs=[tile, pl.BlockSpec((tm, 2 * D_FF), lambda i: (i, 0)), pl.BlockSpec((tm, D_FF), lambda i: (i, 0)),
                   tile, tile, pl.BlockSpec((1, 1), lambda i: (0, 0)), vec],
        out_shape=[S((T, D), BF16), S((T, 2 * D_FF), BF16), S((T, D_FF), BF16), S((T, D), F32), S((T, D), BF16),
                   S((1, 1), F32), S((1, D), F32)],
        compiler_params=_params(1),
    )(x1, g, w_gu, w_down, g_final, target)


def _ffn_bwd(dx2, dx2b, gu, w_down, w_gu, x1, g, mb):
    T = x1.shape[0]
    tm = min(TM_FF, T)
    n_steps = T // tm

    def body(dx2_ref, dx2b_ref, gu_ref, wd_ref, wgu_ref, x_ref, g_ref, mb_ref,
             dgu_ref, dx1_ref, dx1b_ref, dg_ref, gout_ref, acc_out):
        @pl.when(pl.program_id(0) == 0)
        def _():
            dg_ref[...] = jnp.zeros_like(dg_ref)
            acc_out[...] = jnp.zeros_like(acc_out)

        dxb = dx2b_ref[...]
        dh2 = jnp.zeros((tm, D), F32)
        for k in range(N_SHARD // 2):
            cols = slice(FF_SHARD * k, FF_SHARD * (k + 1))
            up_cols = slice(D_FF + FF_SHARD * k, D_FF + FF_SHARD * (k + 1))
            dact = _dot_nt(dxb, wd_ref[cols, :])
            gate = gu_ref[:, cols].astype(F32)
            sg = _sig(gate)
            dgate = (dact * gu_ref[:, up_cols].astype(F32) * (sg * (1.0 + gate * (1.0 - sg)))).astype(BF16)
            dup = (dact * (gate * sg)).astype(BF16)
            dgu_ref[:, cols] = dgate
            dgu_ref[:, up_cols] = dup
            dh2 = dh2 + _dot_nt(dgate, wgu_ref[k]) + _dot_nt(dup, wgu_ref[k + N_SHARD // 2])
        dx, dg_rows = _rms_bwd(dh2, x_ref[...], g_ref[...])
        dg_ref[...] += _colsum(dg_rows)
        dx1 = dx2_ref[...] + dx
        dx1_ref[...] = dx1
        dx1b = dx1.astype(BF16)
        dx1b_ref[...] = dx1b
        acc_out[...] += _dot_tn(mb_ref[...], dx1b)

        @pl.when(pl.program_id(0) == n_steps - 1)
        def _():
            pltpu.sync_copy(acc_out, gout_ref)

    tile = pl.BlockSpec((tm, D), lambda i: (i, 0))
    wide = pl.BlockSpec((tm, 2 * D_FF), lambda i: (i, 0))
    vec = pl.BlockSpec((1, D), lambda i: (0, 0))
    return pl.pallas_call(
        body, name="ffn_bwd", grid=(n_steps,),
        in_specs=[tile, tile, wide, _resident((D_FF, D)), _resident((N_SHARD, D, FF_SHARD)), tile, vec, tile],
        out_specs=[wide, tile, tile, vec, _ANY],
        out_shape=[S((T, 2 * D_FF), BF16), S((T, D), F32), S((T, D), BF16), S((1, D), F32), S((D, D), F32)],
        scratch_shapes=[pltpu.VMEM((D, D), F32)],
        compiler_params=_params(1),
    )(dx2, dx2b, gu, w_down, w_gu, x1, g, mb)


def _matmul_tn(a, b, tn, shard_major, name, sides=()):
    T, M = a.shape
    N = b.shape[1]
    tk = min(TK_WGRAD, T)

    def body(a_ref, b_ref, o_ref):
        @pl.when(pl.program_id(1) == 0)
        def _():
            o_ref[...] = jnp.zeros_like(o_ref)

        acc = _dot_tn(a_ref[...], b_ref[...])
        if shard_major:
            o_ref[0] += acc
        else:
            o_ref[...] += acc

    if shard_major:
        out_spec, out_shape = pl.BlockSpec((1, M, tn), lambda j, k: (j, 0, 0)), S((N // tn, M, tn), F32)
    else:
        out_spec, out_shape = pl.BlockSpec((M, tn), lambda j, k: (0, j)), S((M, N), F32)
    (out,), side_outs = _call(
        body, name=name, grid=(N // tn, T // tk),
        in_specs=[pl.BlockSpec((tk, M), lambda j, k: (k, 0)), pl.BlockSpec((tk, tn), lambda j, k: (k, j))],
        out_specs=[out_spec], out_shape=[out_shape], args=(a, b), sides=sides)
    return out, side_outs


PIECE = IN_SHARD // 3
N_PIECE = IN_COLS // PIECE
DPROJ_ROTATION = 2 * D // PIECE


def _merge_sgu_bwd(dx1b, proj, pa, pb, ya, yb, ln_g, ln_b, sgu_ws, sgu_bs, wpa, wpb, wout, sides=()):
    T = dx1b.shape[0]
    tm = min(TM_FF, T)
    n_chunk = tm // CHUNK
    n_steps = T // tm

    def body(dx_ref, uv_ref, gab_ref, pa_ref, pb_ref, ya_ref, yb_ref, g_ref, b_ref, ws_ref, bs_ref,
             wpa_ref, wpb_ref, wout_ref,
             dya_ref, dp_ref, dw_ref, dbs_ref, dg_ref, db_ref, gpa_ref, gpb_ref,
             w_ref, wt_ref, bst_ref, acc_pa, acc_pb):
        @pl.when(pl.program_id(0) == 0)
        def _():
            for ref in (dw_ref, dbs_ref, dg_ref, db_ref, acc_pa, acc_pb):
                ref[...] = jnp.zeros_like(ref)
            _fill_sgu_weights(ws_ref, bs_ref, w_ref, bst_ref, wt_ref)

        dxb = dx_ref[...]
        dm = _dot_nt(dxb, wout_ref[...])
        sa = _sig(gab_ref[:, 0:D].astype(F32))
        sb = _sig(gab_ref[:, D:2 * D].astype(F32))
        dpa = (dm * sa).astype(BF16)
        dpb = (dm * sb).astype(BF16)
        acc_pa[...] += _dot_tn(ya_ref[...], dpa)
        acc_pb[...] += _dot_tn(yb_ref[...], dpb)
        dp_ref[:, 2 * D:3 * D] = (dm * pa_ref[...].astype(F32) * (sa * (1.0 - sa))).astype(BF16)
        dp_ref[:, 3 * D:4 * D] = (dm * pb_ref[...].astype(F32) * (sb * (1.0 - sb))).astype(BF16)
        dya_ref[...] = _dot_nt(dpa, wpa_ref[...]).astype(BF16)
        dyb_v = _dot_nt(dpb, wpb_ref[...])

        gu, dgu = _gelu_and_grad(uv_ref[:, 0:D].astype(F32))
        gv, dgv = _gelu_and_grad(uv_ref[:, D:2 * D].astype(F32))
        vhat, rstd = _layer_norm_stats(gv)
        lng = g_ref[...]
        vnb = (vhat * lng + b_ref[...]).astype(BF16)
        mixed = _sgu_mix(w_ref, vnb, bst_ref, n_chunk)
        dp_ref[:, 0:D] = (dyb_v * mixed * dgu).astype(BF16)
        dmix = dyb_v * gu
        dmb = dmix.astype(BF16)
        keep = _causal_mask()
        dvn_cols, dbs_rows = [], []
        for g in range(N_GROUP):
            sl = slice(CHUNK * g, CHUNK * (g + 1))
            dmg = dmb[:, sl].reshape(n_chunk, CHUNK, CHUNK)
            vg = vnb[:, sl].reshape(n_chunk, CHUNK, CHUNK)
            wtb = jnp.broadcast_to(wt_ref[g][None], (n_chunk, CHUNK, CHUNK))
            dvn = lax.dot_general(wtb, dmg, (((2,), (1,)), ((0,), (0,))), preferred_element_type=F32)
            dvn_cols.append(dvn.reshape(tm, CHUNK))
            dw = lax.dot_general(dmg, vg, (((2,), (2,)), ((0,), (0,))), preferred_element_type=F32)
            dw_ref[g] += jnp.where(keep, jnp.sum(dw, axis=0), 0.0)
            per_token = jnp.sum(dmix[:, sl], axis=1)
            dbs_rows.append(jnp.sum(per_token.reshape(n_chunk, CHUNK), axis=0, keepdims=True))
        dbs_ref[...] += jnp.concatenate(dbs_rows, axis=0)
        dvn = jnp.concatenate(dvn_cols, axis=1)
        dg_ref[...] += _colsum(dvn * vhat)
        db_ref[...] += _colsum(dvn)
        dvhat = dvn * lng
        dgv_in = rstd * (dvhat - jnp.mean(dvhat, axis=-1, keepdims=True)
                         - vhat * jnp.mean(dvhat * vhat, axis=-1, keepdims=True))
        dp_ref[:, D:2 * D] = (dgv_in * dgv).astype(BF16)

        @pl.when(pl.program_id(0) == n_steps - 1)
        def _():
            for acc, out in ((acc_pa, gpa_ref), (acc_pb, gpb_ref)):
                pltpu.sync_copy(acc, out)

    tile = pl.BlockSpec((tm, D), lambda i: (i, 0))
    vec = pl.BlockSpec((1, D), lambda i: (0, 0))
    w = _resident((D, D))
    wsp = pl.BlockSpec((N_GROUP, CHUNK, CHUNK), lambda i: (0, 0, 0))
    acc = pltpu.VMEM((D, D), F32)
    return _call(
        body, name="merge_sgu_bwd", grid=(n_steps,),
        in_specs=[tile, pl.BlockSpec((tm, 2 * D), lambda i: (i, 1)), pl.BlockSpec((tm, 2 * D), lambda i: (i, 2)),
                  tile, tile, tile, tile, vec, vec, pl.BlockSpec(sgu_ws.shape, lambda i: (0, 0, 0, 0)),
                  pl.BlockSpec(sgu_bs.shape, lambda i: (0, 0, 0)), w, w, w],
        out_specs=[tile, pl.BlockSpec((tm, 4 * D), lambda i: (i, 0)), wsp,
                   pl.BlockSpec((N_GROUP, CHUNK), lambda i: (0, 0)), vec, vec, _ANY, _ANY],
        out_shape=[S((T, D), BF16), S((T, IN_COLS), BF16), S((N_GROUP, CHUNK, CHUNK), F32), S((N_GROUP, CHUNK), F32),
                   S((1, D), F32), S((1, D), F32), S((D, D), F32), S((D, D), F32)],
        scratch_shapes=[_SGU_W_BF16, _SGU_W_BF16, _SGU_BT, acc, acc],
        args=(dx1b, proj, proj, pa, pb, ya, yb, ln_g, ln_b, sgu_ws, sgu_bs, wpa, wpb, wout), sides=sides)


def _rglru_bwd(dya, dproj, proj, hseq, xc, gates, conv_w, rg_wa, rg_wx, lam, sides=()):
    T = dya.shape[0]
    tm = min(TM_SCAN, T)
    n = T // tm
    per8 = tm // SUBLANES

    def body(dya_ref, _, rx_ref, rxp_ref, gate_ref, h_ref, hp_ref, xc_ref, gates_ref, cw_ref, wah_ref, wxh_ref,
             lam_ref, dab_ref, dcw_ref, dcb_ref, dwah_ref, dwxh_ref, dba_ref, dbx_ref, dlam_ref,
             hext, rext, dext, carry_a, carry_dh, a_s, b_s, h_s, wa_ref, wx_ref, dwa_ref, dwx_ref):
        i = pl.program_id(0)
        first_tile = i == n - 1

        @pl.when(i == 0)
        def _():
            for ref in (dcw_ref, dcb_ref, dwa_ref, dwx_ref, dba_ref, dbx_ref, dlam_ref, carry_a, carry_dh):
                ref[...] = jnp.zeros_like(ref)
            dext[tm:tm + SUBLANES, :] = jnp.zeros((SUBLANES, D), F32)
            _fill_blockdiag(wah_ref, wa_ref)
            _fill_blockdiag(wxh_ref, wx_ref)

        gel, dgel = _gelu_and_grad(gate_ref[...].astype(F32))
        dya_v = dya_ref[...].astype(F32)
        hseq_v = h_ref[...]
        dgate = dya_v * hseq_v * dgel
        xcv = xc_ref[...]
        lam_v = lam_ref[...]
        sp = _softplus_neg(lam_v)
        xb = xcv.astype(BF16)
        r, gi, a, m = (gates_ref[:, D * k:D * (k + 1)] for k in range(N_GATES))

        row = lax.broadcasted_iota(jnp.int32, (tm, D), 0)
        c = jnp.where(row == tm - 1, carry_a[0:1, :], _shift_up(a, 1, 0.0))
        dH = _scan_tile(c, dya_v * gel, carry_dh[0:1, :], a_s, b_s, h_s, reverse=True)
        carry_a[...] = jnp.broadcast_to(a[0:1, :], (SUBLANES, D))
        carry_dh[...] = jnp.broadcast_to(dH[0:1, :], (SUBLANES, D))

        hext[0:SUBLANES, :] = jnp.where(first_tile, 0.0, hp_ref[...])
        hext[SUBLANES:SUBLANES + tm, :] = hseq_v
        h_prev = hext[pl.ds(SUBLANES - 1, tm), :]

        d_m = dH * (gi * xcv)
        d_la = dH * h_prev * a - d_m * (a * a) / m
        d_ia = dH * m * xcv * (gi * (1.0 - gi))
        d_ra = d_la * ((-RG_C) * sp) * (r * (1.0 - r))
        dlam_ref[...] += _colsum(d_la * ((-RG_C) * r)) * (-_sig(-lam_v))
        dba_ref[...] += _colsum(d_ra)
        dbx_ref[...] += _colsum(d_ia)
        drab = d_ra.astype(BF16)
        diab = d_ia.astype(BF16)
        dxc_cols = []
        for j in range(N_RG_BLOCK):
            sl = slice(RG_BLOCK * j, RG_BLOCK * (j + 1))
            dxc_cols.append(_dot_nt(drab[:, sl], wa_ref[j]) + _dot_nt(diab[:, sl], wx_ref[j]))
            dwa_ref[j] += _dot_tn(xb[:, sl], drab[:, sl])
            dwx_ref[j] += _dot_tn(xb[:, sl], diab[:, sl])
        dxc = dH * m * gi + jnp.concatenate(dxc_cols, axis=1)

        dcb_ref[...] += _colsum(dxc)
        dext[0:tm, :] = dxc
        rext[0:SUBLANES, :] = jnp.where(first_tile, 0.0, rxp_ref[SUBLANES:2 * SUBLANES, :].astype(F32))
        rext[SUBLANES:SUBLANES + tm, :] = rx_ref[...].astype(F32)
        drx = jnp.zeros((tm, D), F32)
        for k in range(CONV_WIDTH):
            drx = drx + dext[pl.ds(CONV_WIDTH - 1 - k, tm), :] * cw_ref[k:k + 1, :]
            dcw_ref[k:k + 1, :] += _colsum(dxc * rext[pl.ds(SUBLANES - (CONV_WIDTH - 1) + k, tm), :])
        dext[tm:tm + SUBLANES, :] = dext[0:SUBLANES, :]
        dab_ref[:, 0:D] = drx.astype(BF16)
        dab_ref[:, D:2 * D] = dgate.astype(BF16)

        @pl.when(first_tile)
        def _():
            for j in range(N_RG_BLOCK):
                for h in range(HEADS_PER_BLOCK):
                    sl = slice(HEAD_DIM * h, HEAD_DIM * (h + 1))
                    pair, side = divmod(HEADS_PER_BLOCK * j + h, 2)
                    lanes = slice(HEAD_DIM * side, HEAD_DIM * (side + 1))
                    dwah_ref[pair, :, lanes] = dwa_ref[j, sl, sl]
                    dwxh_ref[pair, :, lanes] = dwx_ref[j, sl, sl]

    def rev(col):
        return lambda i: (n - 1 - i, col)

    def prev8(col):
        return lambda i: (jnp.maximum((n - 1 - i) * per8 - 1, 0), col)

    def prev16(col):
        return lambda i: (jnp.maximum((n - 1 - i) * (per8 // 2) - 1, 0), col)

    tile = pl.BlockSpec((tm, D), rev(0))
    vec = pl.BlockSpec((1, D), lambda i: (0, 0))
    heads_in = pl.BlockSpec(rg_wa.shape, lambda i: (0, 0, 0, 0))
    head_pairs = (rg_wa.shape[1] // 2, HEAD_DIM, 2 * HEAD_DIM)
    heads_out = pl.BlockSpec(head_pairs, lambda i: (0, 0, 0))
    cw = pl.BlockSpec((CONV_WIDTH, D), lambda i: (0, 0))
    blocks_f32 = pltpu.VMEM((N_RG_BLOCK, RG_BLOCK, RG_BLOCK), F32)
    return _call(
        body, name="rglru_bwd", grid=(n,),
        in_specs=[tile, _ANY, pl.BlockSpec((tm, D), rev(0)), pl.BlockSpec((2 * SUBLANES, D), prev16(0)),
                  pl.BlockSpec((tm, D), rev(1)), tile, pl.BlockSpec((SUBLANES, D), prev8(0)), tile,
                  pl.BlockSpec((tm, N_GATES * D), rev(0)), cw, heads_in, heads_in, vec],
        out_specs=[pl.BlockSpec((tm, 2 * D), rev(2)), cw, vec, heads_out, heads_out, vec, vec, vec],
        out_shape=[S((T, IN_COLS), BF16), S((CONV_WIDTH, D), F32), S((1, D), F32),
                   S(head_pairs, F32), S(head_pairs, F32), S((1, D), F32), S((1, D), F32), S((1, D), F32)],
        scratch_shapes=[pltpu.VMEM((tm + SUBLANES, D), F32), pltpu.VMEM((tm + SUBLANES, D), F32),
                        pltpu.VMEM((tm + SUBLANES, D), F32), pltpu.VMEM((SUBLANES, D), F32),
                        pltpu.VMEM((SUBLANES, D), F32)] + [pltpu.VMEM((D // LANES, tm, LANES), F32)] * 3
        + [_RG_BLOCKS_BF16] * 2 + [blocks_f32] * 2,
        args=(dya, dproj, proj, proj, proj, hseq, hseq, xc, gates, conv_w, rg_wa, rg_wx, lam), sides=sides,
        aliases={1: 0})


def _inproj_dh(dproj, w_in, dh, first, count, name, sides=()):
    T = dproj.shape[0]
    tm = min(TM_MM, T)

    def body(*refs):
        dp_ref, w_ref, dh_ref = refs[-3:]
        dh = jnp.zeros((tm, D), F32)
        for p in range(N_PIECE):
            shard, part = divmod((p + DPROJ_ROTATION) % N_PIECE, IN_SHARD // PIECE)
            dh = dh + _dot_nt(dp_ref[:, PIECE * p:PIECE * (p + 1)], w_ref[shard, :, PIECE * part:PIECE * (part + 1)])
        dh_ref[...] = dh

    earlier = [] if dh is None else [dh]
    return _call(
        body, name=name, grid=(count,),
        in_specs=[_ANY] * len(earlier) + [pl.BlockSpec((tm, IN_COLS), lambda i: (first + i, 0)),
                                         _resident((N_SHARD, D, IN_SHARD))],
        out_specs=[pl.BlockSpec((tm, D), lambda i: (first + i, 0))],
        out_shape=[S((T, D), F32)],
        args=(*earlier, dproj, w_in), sides=sides, aliases={0: 0} if earlier else None)


def _inproj_norm_bwd(dh, x, g, dx1):
    T = x.shape[0]
    tm = min(TM_MM, T)

    def body(dh_ref, x_ref, g_ref, dx1_ref, dx_ref, dgm_ref):
        @pl.when(pl.program_id(0) == 0)
        def _():
            dgm_ref[...] = jnp.zeros_like(dgm_ref)

        dx, dg_rows = _rms_bwd(dh_ref[...], x_ref[...], g_ref[...])
        dgm_ref[...] += _colsum(dg_rows)
        dx_ref[...] = dx1_ref[...] + dx

    tile = pl.BlockSpec((tm, D), lambda i: (i, 0))
    vec = pl.BlockSpec((1, D), lambda i: (0, 0))
    return pl.pallas_call(
        body, name="inproj_norm_bwd", grid=(T // tm,),
        in_specs=[tile, tile, vec, tile],
        out_specs=[tile, vec],
        out_shape=[S((T, D), F32), S((1, D), F32)],
        compiler_params=_params(1),
    )(dh, x, g, dx1)


def _inproj_wgrad(h, dproj, sides=()):
    T = h.shape[0]
    tk = min(TK_WGRAD, T)
    per = IN_SHARD // PIECE

    def body(h_ref, *refs):
        pieces, o_ref = refs[:per], refs[per]

        @pl.when(pl.program_id(1) == 0)
        def _():
            o_ref[...] = jnp.zeros_like(o_ref)

        o_ref[0] += _dot_tn(h_ref[...], jnp.concatenate([p[...] for p in pieces], axis=1))

    def piece(i):
        return pl.BlockSpec((tk, PIECE), lambda j, k: (k, (per * j + i + N_PIECE - DPROJ_ROTATION) % N_PIECE))

    (out,), side_outs = _call(
        body, name="inproj_wgrad", grid=(N_SHARD, T // tk),
        in_specs=[pl.BlockSpec((tk, D), lambda j, k: (k, 0))] + [piece(i) for i in range(per)],
        out_specs=[pl.BlockSpec((1, D, IN_SHARD), lambda j, k: (j, 0, 0))],
        out_shape=[S((N_SHARD, D, IN_SHARD), F32)], args=(h,) + (dproj,) * per, sides=sides)
    return out, side_outs


def _row_tile(rows):
    for t in range(256, 0, -SUBLANES):
        if rows % t == 0:
            return t
    raise ValueError(rows)


def _add_halves(core, grads, theirs, name):
    n = len(grads)
    _, r, cols = grads[0].shape
    half = r // 2
    tr = _row_tile(half)
    nb = half // tr

    def body(core_ref, *refs):
        for g_ref, t_ref, o_ref in zip(refs[:n], refs[n:2 * n], refs[2 * n:]):
            o_ref[...] = (g_ref[...] + t_ref[...]).astype(BF16)

    blk = pl.BlockSpec((1, tr, cols), lambda s, i, core_ref: (s, i, 0))
    mine = pl.BlockSpec((1, tr, cols), lambda s, i, core_ref: (s, core_ref[0] * nb + i, 0))
    gs = pltpu.PrefetchScalarGridSpec(num_scalar_prefetch=1, grid=(N_SHARD, nb),
                                      in_specs=[mine] * n + [blk] * n, out_specs=[blk] * n)
    return pl.pallas_call(
        body, name=name, grid_spec=gs, out_shape=[S((N_SHARD, half, cols), BF16)] * n, compiler_params=_params(2),
    )(core, *grads, *theirs)


def _sum_shards(chip, owns, others, name):
    n = len(owns)
    _, half, cols = owns[0].shape
    tr = _row_tile(half)

    def body(chip_ref, *refs):
        for own_ref, oth_ref, o_ref in zip(refs[:n], refs[n:2 * n], refs[2 * n:]):
            acc = own_ref[0].astype(F32)
            for j in range(3):
                acc = acc + oth_ref[j].astype(F32)
            o_ref[...] = acc

    gs = pltpu.PrefetchScalarGridSpec(
        num_scalar_prefetch=1, grid=(half // tr,),
        in_specs=[pl.BlockSpec((1, tr, cols), lambda i, chip_ref: (chip_ref[0], i, 0))] * n
        + [pl.BlockSpec((3, tr, cols), lambda i, chip_ref: (0, i, 0))] * n,
        out_specs=[pl.BlockSpec((tr, cols), lambda i, chip_ref: (i, 0))] * n)
    return pl.pallas_call(
        body, name=name, grid_spec=gs, out_shape=[S((half, cols), F32)] * n, compiler_params=_params(1),
    )(chip, *owns, *others)


def _adamw(w, g, m, v):
    m = ADAM_B1 * m + (1.0 - ADAM_B1) * g
    v = ADAM_B2 * v + (1.0 - ADAM_B2) * (g * g)
    m_hat = m / (1.0 - ADAM_B1 ** ADAM_STEP)
    v_hat = v / (1.0 - ADAM_B2 ** ADAM_STEP)
    delta = -ADAM_LR * (m_hat / (jnp.sqrt(v_hat) + ADAM_EPS) + ADAM_WD * w)
    return delta, m, v


def _adamw_shard(core, mine, theirs, w, m, v, name):
    n = len(w)
    r, cols = w[0].shape
    half = r // 2
    tr = _row_tile(half)
    nb = half // tr

    def body(core_ref, *refs):
        groups = [refs[k * n:(k + 1) * n] for k in range(9)]
        for mine_ref, theirs_ref, w_ref, m_ref, v_ref, g_ref, d_ref, mo_ref, vo_ref in zip(*groups):
            g = jnp.where(pl.program_id(0) == core_ref[0], mine_ref[...], theirs_ref[...])
            g_ref[...] = g
            d_ref[...], mo_ref[...], vo_ref[...] = _adamw(w_ref[...], g, m_ref[...], v_ref[...])

    hblk = pl.BlockSpec((tr, cols), lambda h, i, core_ref: (i, 0))
    blk = pl.BlockSpec((tr, cols), lambda h, i, core_ref: (h * nb + i, 0))
    gs = pltpu.PrefetchScalarGridSpec(num_scalar_prefetch=1, grid=(2, nb),
                                      in_specs=[hblk] * (2 * n) + [blk] * (3 * n), out_specs=[blk] * (4 * n))
    res = pl.pallas_call(
        body, name=name, grid_spec=gs, out_shape=[S((r, cols), F32)] * (4 * n), compiler_params=_params(2),
    )(core, *mine, *theirs, *w, *m, *v)
    return [tuple(res[k * n + j] for k in range(4)) for j in range(n)]


def _adamw_whole(w, g, m, v, name):
    def body(w_ref, g_ref, m_ref, v_ref, d_ref, mo_ref, vo_ref):
        d_ref[...], mo_ref[...], vo_ref[...] = _adamw(w_ref[...], g_ref[...], m_ref[...], v_ref[...])

    return pl.pallas_call(body, name=name, out_shape=[S(w.shape, F32)] * 3)(w, g, m, v)


_VEC_ROWS = ("norm_mix_g", "conv_b", "rg_lambda", "sgu_ln_g", "sgu_ln_b", "norm_ffn_g", "norm_final_g", "rg_ba",
             "rg_bx")
_CONV_ROW = len(_VEC_ROWS)
_LOSS_ROW = _CONV_ROW + CONV_WIDTH
_VEC_PAD = -(_LOSS_ROW + 1) % SUBLANES
_HEAD_BIASES = ("rg_ba", "rg_bx")
_TENSORS = ("sgu_bs", "sgu_ws", "rg_wa", "rg_wx")
_HEAD_PAIRS = ("rg_wa", "rg_wx")


def _small_sum_adamw(parts, w, m, v):
    names = [n for n in _VEC_ROWS] + list(_TENSORS)
    n_parts = len(parts)

    def total(ref):
        acc = ref[0]
        for k in range(1, N_DEVICE):
            acc = acc + ref[k]
        return acc

    def body(*refs):
        part_refs, refs = refs[:n_parts], refs[n_parts:]
        w_refs, m_refs, v_refs = (dict(zip(names, refs[k * len(names):(k + 1) * len(names)])) for k in range(3))
        outs = refs[3 * len(names):]
        out_refs = {n: outs[4 * k:4 * k + 4] for k, n in enumerate(names)}
        conv_ref, loss_ref = outs[4 * len(names):]
        vec = total(part_refs[0])
        grads = {n: total(p) for n, p in zip(_TENSORS, part_refs[1:])}
        for n in _HEAD_PAIRS:
            pairs = grads[n]
            grads[n] = jnp.stack([pairs[k // 2, :, HEAD_DIM * (k % 2):HEAD_DIM * (k % 2 + 1)]
                                  for k in range(2 * pairs.shape[0])], axis=0)
        grads = {n: g[None] for n, g in grads.items()}
        for row, n in enumerate(_VEC_ROWS):
            g = vec[row:row + 1, :]
            if n in _HEAD_BIASES:
                g = jnp.concatenate([g[:, HEAD_DIM * h:HEAD_DIM * (h + 1)] for h in range(D // HEAD_DIM)], axis=0)[None]
            grads[n] = g
        for n in names:
            g_ref, d_ref, mo_ref, vo_ref = out_refs[n]
            g_ref[...] = grads[n]
            d_ref[...], mo_ref[...], vo_ref[...] = _adamw(w_refs[n][...], grads[n], m_refs[n][...], v_refs[n][...])
        conv_ref[...] = vec[_CONV_ROW:_CONV_ROW + CONV_WIDTH, :]
        loss_ref[...] = vec[_LOSS_ROW:_LOSS_ROW + 1, 0:1]

    res = pl.pallas_call(
        body, name="small_sum_adamw",
        out_shape=[S(w[n].shape, F32) for n in names for _ in range(4)] + [S((CONV_WIDTH, D), F32), S((1, 1), F32)],
        compiler_params=pltpu.CompilerParams(vmem_limit_bytes=VMEM_LIMIT),
    )(*parts, *[w[n] for n in names], *[m[n] for n in names], *[v[n] for n in names])
    return {n: tuple(res[4 * k:4 * k + 4]) for k, n in enumerate(names)}, res[-2], res[-1]


_BIG = ("w_in", "w_proj_a", "w_proj_b", "w_out", "w_gate_up", "w_down")
_WEIGHTS = ("norm_mix_g", "w_in", "conv_w", "conv_b", "rg_wa", "rg_ba", "rg_wx", "rg_bx", "rg_lambda", "sgu_ln_g",
            "sgu_ln_b", "sgu_ws", "sgu_bs", "w_proj_a", "w_proj_b", "w_out", "norm_ffn_g", "w_gate_up", "w_down",
            "norm_final_g")


def kernel(x, norm_mix_g, w_in, conv_w, conv_b, rg_wa, rg_ba, rg_wx, rg_bx, rg_lambda, sgu_ln_g, sgu_ln_b, sgu_ws, sgu_bs, w_proj_a, w_proj_b, w_out, norm_ffn_g, w_gate_up, w_down, norm_final_g, loss_target, m_norm_mix_g, m_w_in, m_conv_w, m_conv_b, m_rg_wa, m_rg_ba, m_rg_wx, m_rg_bx, m_rg_lambda, m_sgu_ln_g, m_sgu_ln_b, m_sgu_ws, m_sgu_bs, m_w_proj_a, m_w_proj_b, m_w_out, m_norm_ffn_g, m_w_gate_up, m_w_down, m_norm_final_g, v_norm_mix_g, v_w_in, v_conv_w, v_conv_b, v_rg_wa, v_rg_ba, v_rg_wx, v_rg_bx, v_rg_lambda, v_sgu_ln_g, v_sgu_ln_b, v_sgu_ws, v_sgu_bs, v_w_proj_a, v_w_proj_b, v_w_out, v_norm_ffn_g, v_w_gate_up, v_w_down, v_norm_final_g):
    args = dict(locals())
    w = {n: args[n] for n in _WEIGHTS}
    mom = {n: args["m_" + n] for n in _WEIGHTS}
    var = {n: args["v_" + n] for n in _WEIGHTS}
    xi, yi, ci = _position()
    core = ci.astype(jnp.int32).reshape(1)
    chip = (2 * xi + yi).astype(jnp.int32).reshape(1)

    bf = {n: w[n][0].astype(BF16) for n in _BIG}
    final_g = w["norm_final_g"].reshape(1, D)
    ba, bx = w["rg_ba"].reshape(1, D), w["rg_bx"].reshape(1, D)
    lam, ln_g, ln_b = w["rg_lambda"], w["sgu_ln_g"], w["sgu_ln_b"]
    x0, target = x[0], loss_target[0]

    def shard_major(g):
        return g.reshape(N_SHARD, g.shape[0] // N_SHARD, g.shape[1])

    def per_shape(names, fn, *lists):
        if len({a.shape for a in lists[0]}) == 1:
            return fn("_".join(names), *lists)
        return [r for k, n in enumerate(names) for r in fn(n, *[[a[k]] for a in lists])]

    def chip_sums(names, grads, theirs):
        return per_shape(names, lambda s, g, t: _add_halves(core, g, t, "add_halves_" + s), grads, theirs)

    def my_halves(names, sums, arrived):
        return per_shape(names, lambda s, p, a: _sum_shards(chip, p, a, "sum_shards_" + s), sums, arrived)

    (proj, h), ((w_in_a,), (conv_a,)) = _inproj_own(
        chip, x0, w["norm_mix_g"], bf["w_in"],
        sides=[_gather_half_side([bf["w_in"]], relations=(0, 1)), _gather_side([w["conv_w"][0]])])
    conv_cols = conv_a.shape[-1]
    conv_full = jnp.swapaxes(conv_a, 0, 1).reshape(CONV_WIDTH, D)
    (proj,), ((w_in_a,),) = _inproj_rest(
        chip, proj, h, w_in_a, 1, 2, "inproj_near",
        sides=[_gather_half_side([bf["w_in"]], relations=(2,), into=[w_in_a])])
    (proj,), _ = _inproj_rest(chip, proj, h, w_in_a, 3, 1, "inproj_far")
    (ya, xc, hseq, gates), ((w_pa_a, w_pb_a, w_out_a, w_gu_a, w_down_a),) = _rglru_fwd(
        proj, conv_full, w["conv_b"], w["rg_wa"], w["rg_wx"], ba, bx, lam,
        sides=[_gather_half_side([bf[n] for n in ("w_proj_a", "w_proj_b", "w_out", "w_gate_up", "w_down")])])
    wpa, wpb, wout, wdown = w_pa_a.reshape(D, D), w_pb_a.reshape(D, D), w_out_a.reshape(D, D), w_down_a.reshape(D_FF, D)
    yb, pa, pb, mb, x1 = _sgu_merge_fwd(x0, proj, ya, ln_g, ln_b, w["sgu_ws"], w["sgu_bs"], wpa, wpb, wout)
    h2, gu, act, dx2, dx2b, loss, d_final_g = _ffn_fwd_loss(x1, w["norm_ffn_g"], w_gu_a, wdown, final_g, target)

    dgu, dx1, dx1b, d_ffn_g, g_out = _ffn_bwd(dx2, dx2b, gu, wdown, w_gu_a, x1, w["norm_ffn_g"], mb)
    ffn = ("w_gate_up", "w_down")
    g_ffn = [_matmul_tn(h2, dgu, FF_SHARD, True, "wgrad_gate_up")[0],
             shard_major(_matmul_tn(act, dx2b, D // 2, False, "wgrad_down")[0])]
    (dya, dproj, d_ws, d_bs, d_lng, d_lnb, g_pa, g_pb), (theirs_ffn,) = _merge_sgu_bwd(
        dx1b, proj, pa, pb, ya, yb, ln_g, ln_b, w["sgu_ws"], w["sgu_bs"], wpa, wpb, wout,
        sides=[_halves_side(g_ffn)])
    sums_ffn = chip_sums(ffn, g_ffn, theirs_ffn)
    mix = ("w_proj_a", "w_proj_b", "w_out")
    g_mix = [shard_major(g) for g in (g_pa, g_pb, g_out)]
    (dproj, d_cw, d_cb, d_wa, d_wx, d_ba, d_bx, d_lam), (arrived_ffn, theirs_mix, sgu_parts) = _rglru_bwd(
        dya, dproj, proj, hseq, xc, gates, conv_full, w["rg_wa"], w["rg_wx"], lam,
        sides=[_scatter_side(sums_ffn), _halves_side(g_mix), _everyone_side([d_bs, d_ws])])
    mine_ffn = my_halves(ffn, sums_ffn, arrived_ffn)
    sums_mix = chip_sums(mix, g_mix, theirs_mix)
    g_in, ((wa_parts, wx_parts), other_ffn, arrived_mix) = _inproj_wgrad(
        h, dproj, sides=[_everyone_side([d_wa, d_wx]), _swap_side(mine_ffn),
                         _scatter_side(sums_mix)])
    mine_mix = my_halves(mix, sums_mix, arrived_mix)
    n_tiles = x0.shape[0] // min(TM_MM, x0.shape[0])
    n_first = max(1, n_tiles * 3 // 8)
    (dh,), (theirs_in, other_mix) = _inproj_dh(dproj, w_in_a, None, 0, n_first, "inproj_dh_a",
                                               sides=[_halves_side([g_in]), _swap_side(mine_mix)])
    sums_in = chip_sums(("w_in",), [g_in], theirs_in)
    (dh,), (arrived_in,) = _inproj_dh(dproj, w_in_a, dh, n_first, n_tiles - n_first, "inproj_dh_b",
                                      sides=[_scatter_side(sums_in)])
    mine_in = my_halves(("w_in",), sums_in, arrived_in)
    grad_x, d_mix_g = _inproj_norm_bwd(dh, x0, w["norm_mix_g"], dx1)
    rows = {"norm_mix_g": d_mix_g, "conv_b": d_cb, "rg_lambda": d_lam, "sgu_ln_g": d_lng, "sgu_ln_b": d_lnb,
            "norm_ffn_g": d_ffn_g, "norm_final_g": d_final_g, "rg_ba": d_ba, "rg_bx": d_bx}
    vec = jnp.concatenate([rows[n] for n in _VEC_ROWS]
                          + [d_cw, jnp.pad(loss, ((0, _VEC_PAD), (0, D - 1)))], axis=0)
    other_in, (vec_parts,) = _comm_only([_swap_side(mine_in), _everyone_side([vec])], "swap_w_in")
    small_parts = [vec_parts] + sgu_parts + [wa_parts, wx_parts]

    out = {}
    for names, gm, go in ((ffn, mine_ffn, other_ffn), (mix, mine_mix, other_mix), (("w_in",), mine_in, other_in)):
        results = per_shape(
            names, lambda s, *lists: _adamw_shard(core, *lists, "adamw_" + s),
            gm, go, [w[n][0] for n in names], [mom[n][0] for n in names], [var[n][0] for n in names])
        for n, res in zip(names, results):
            out[n] = tuple(a[None] for a in res)
    as_row = lambda t: {n: a.reshape(1, D) if n == "norm_final_g" else a for n, a in t.items()}
    small_out, conv_sum, loss_sum = _small_sum_adamw(small_parts, as_row(w), as_row(mom), as_row(var))
    out.update(small_out)
    out["norm_final_g"] = tuple(a.reshape(D) for a in small_out["norm_final_g"])
    conv_g = lax.dynamic_slice_in_dim(conv_sum, chip[0] * conv_cols, conv_cols, axis=1)
    d, mo, vo = _adamw_whole(w["conv_w"][0], conv_g, mom["conv_w"][0], var["conv_w"][0], "adamw_conv_w")
    out["conv_w"] = tuple(a[None] for a in (conv_g, d, mo, vo))

    return (loss_sum[0, 0], grad_x[None], *[out[n][0] for n in _WEIGHTS], *[out[n][1] for n in _WEIGHTS],
            *[out[n][2] for n in _WEIGHTS], *[out[n][3] for n in _WEIGHTS])
```

```python
import functools

import jax
import jax.numpy as jnp
from jax import lax
from jax.experimental import pallas as pl
from jax.experimental.pallas import tpu as pltpu

F32 = jnp.float32
BF16 = jnp.bfloat16
S = jax.ShapeDtypeStruct

D = 1024
N_SHARD = 4
IN_COLS = 6 * D
IN_SHARD = IN_COLS // N_SHARD
D_FF = 2816
FF_SHARD = 2 * D_FF // N_SHARD
RG_BLOCK = 256
N_RG_BLOCK = D // RG_BLOCK
CHUNK = 128
N_GROUP = 8
CONV_WIDTH = 4
RG_C = 8.0
EPS = 1e-6
ADAM_LR, ADAM_B1, ADAM_B2, ADAM_EPS, ADAM_WD, ADAM_STEP = 0.001, 0.9, 0.999, 1e-08, 0.01, 10

V7X_VMEM_BYTES = 64 * 1024 * 1024
VMEM_LIMIT = V7X_VMEM_BYTES * 3 // 4
SUBLANES = 8
MESH = pl.DeviceIdType.MESH

TM_MM = 512
TM_SCAN = 256
TM_FF = 256
TK_WGRAD = 2048


def _params(n_axes, limit=VMEM_LIMIT):
    return pltpu.CompilerParams(dimension_semantics=("arbitrary",) * n_axes, vmem_limit_bytes=limit)


def _resident(shape):
    nd = len(shape)
    return pl.BlockSpec(shape, lambda *_: (0,) * nd, pipeline_mode=pl.Buffered(1))


def _sig(x):
    return 1.0 / (1.0 + jnp.exp(-x))


_GELU_K2 = 2.0 * 0.7978845608028654
_GELU_C = 0.044715


def _gelu(x):
    return x * _sig(x * (_GELU_K2 + (_GELU_K2 * _GELU_C) * (x * x)))


def _gelu_and_grad(x):
    x2 = x * x
    s = _sig(x * (_GELU_K2 + (_GELU_K2 * _GELU_C) * x2))
    g = x * s
    return g, s + g * (1.0 - s) * (_GELU_K2 + (3.0 * _GELU_K2 * _GELU_C) * x2)


_EXPM1_SERIES = tuple(1.0 / f for f in (5040.0, 720.0, 120.0, 24.0, 6.0, 2.0, 1.0))


def _one_minus_exp(x):
    p = _EXPM1_SERIES[0]
    for coef in _EXPM1_SERIES[1:]:
        p = p * x + coef
    return jnp.where(x > -0.125, -x * p, 1.0 - jnp.exp(x))


def _softplus_neg(lam):
    z = -lam
    e = jnp.exp(-jnp.abs(z))
    u = 1.0 + e
    log1p = jnp.where(u == 1.0, e, jnp.log(u) * e / (u - 1.0))
    return jnp.maximum(z, 0.0) + log1p


def _rms_stats(x):
    return lax.rsqrt(jnp.mean(x * x, axis=-1, keepdims=True) + EPS)


def _rms_bwd(dy, x, g):
    rstd = _rms_stats(x)
    xhat = x * rstd
    dxhat = dy * g
    dx = rstd * (dxhat - xhat * jnp.mean(dxhat * xhat, axis=-1, keepdims=True))
    return dx, dy * xhat


def _colsum(x):
    return jnp.sum(x, axis=0, keepdims=True)


def _shift_down(x, d, fill):
    n = x.shape[0]
    if d % SUBLANES == 0:
        return jnp.concatenate([jnp.full((d, x.shape[1]), fill, x.dtype), x[:n - d]], axis=0)
    row = lax.broadcasted_iota(jnp.int32, x.shape, 0)
    return jnp.where(row < d, fill, pltpu.roll(x, d, 0))


def _shift_up(x, d, fill):
    n = x.shape[0]
    if d % SUBLANES == 0:
        return jnp.concatenate([x[d:], jnp.full((d, x.shape[1]), fill, x.dtype)], axis=0)
    row = lax.broadcasted_iota(jnp.int32, x.shape, 0)
    return jnp.where(row >= n - d, fill, pltpu.roll(x, n - d, 0))


def _scan(a, b, shift):
    d = 1
    while d < a.shape[0]:
        b = a * shift(b, d, 0.0) + b
        a = a * shift(a, d, 1.0)
        d *= 2
    return a, b


LANES = 128


def _scan_tile(a, b, outside, a_s, b_s, h_s, reverse):
    tm = a.shape[0]
    groups = tm // SUBLANES
    order = list(range(SUBLANES - 1, -1, -1) if reverse else range(SUBLANES))
    shift = _shift_up if reverse else _shift_down
    edge = groups - 1 if reverse else 0
    for j in range(D // LANES):
        a_s[j] = a[:, LANES * j:LANES * (j + 1)]
        b_s[j] = b[:, LANES * j:LANES * (j + 1)]
    for j in range(D // LANES):
        def slab(ref, k):
            return ref[j, pl.ds(k, groups, stride=SUBLANES), :]

        ga, gb = slab(a_s, order[0]), slab(b_s, order[0])
        for k in order[1:]:
            ak = slab(a_s, k)
            gb = ak * gb + slab(b_s, k)
            ga = ak * ga
        ga, gb = _scan(ga, gb, shift)
        h_out = outside[:, LANES * j:LANES * (j + 1)]
        group_end = ga * h_out + gb
        row = lax.broadcasted_iota(jnp.int32, (groups, LANES), 0)
        h = jnp.where(row == edge, h_out, shift(group_end, 1, 0.0))
        for k in order:
            h = slab(a_s, k) * h + slab(b_s, k)
            h_s[j, pl.ds(k, groups, stride=SUBLANES), :] = h
    return jnp.concatenate([h_s[j] for j in range(D // LANES)], axis=1)


def _dot(a, b):
    return jnp.dot(a, b, preferred_element_type=F32)


def _dot_nt(a, b):
    return lax.dot_general(a, b, (((1,), (1,)), ((), ())), preferred_element_type=F32)


def _dot_tn(a, b):
    return lax.dot_general(a, b, (((0,), (0,)), ((), ())), preferred_element_type=F32)


_ANY = pl.BlockSpec(memory_space=pl.ANY)


def _position():
    return lax.axis_index("x"), lax.axis_index("y"), lax.axis_index("c")


def _other_chips(x, y):
    return [(1 - x, y), (x, 1 - y), (1 - x, 1 - y)]


class _Side:
    def __init__(self, inputs, out_shapes, n_sems, make, continues=()):
        self.inputs, self.out_shapes, self.n_sems, self.make = list(inputs), list(out_shapes), n_sems, make
        self.continues = list(continues)


MID_STEP = 0.625
LATE_MID_STEP = 0.875


def _call(body, *, name, grid, in_specs, out_specs, out_shape, args, scratch_shapes=(), sides=(), aliases=None,
          scalars=None, mid=MID_STEP):
    n_in, n_out, n_scr = len(in_specs), len(out_specs), len(scratch_shapes)
    n_scalar = 0 if scalars is None else 1
    side_in = [len(s.inputs) + len(s.continues) for s in sides]
    side_out = [len(s.out_shapes) for s in sides]
    all_aliases = {k + n_scalar: v for k, v in (aliases or {}).items()}
    for idx, s in enumerate(sides):
        for k in range(len(s.continues)):
            operand = n_scalar + n_in + sum(side_in[:idx]) + len(s.inputs) + k
            all_aliases[operand] = n_out + sum(side_out[:idx]) + k

    def wrapped(*refs):
        refs = list(refs)
        take = lambda k: [refs.pop(0) for _ in range(k)]
        ins = take(n_scalar) + take(n_in)
        sins = [take(k) for k in side_in]
        outs = take(n_out)
        souts = [take(k) for k in side_out]
        scr = take(n_scr)
        sems = [take(3) for _ in sides]
        def run(phase):
            for s, si, so, sem in zip(sides, sins, souts, sems):
                for thunk in s.make(si[:len(s.inputs)], so, *sem)[phase]:
                    thunk()

        if sides:
            n_steps = functools.reduce(lambda a, b: a * b, grid)
            step = functools.reduce(lambda a, b: a + b, [
                pl.program_id(a) * functools.reduce(lambda p, q: p * q, grid[a + 1:], 1) for a in range(len(grid))])
            pl.when(step == 0)(lambda: run(0))
        body(*ins, *outs, *scr)
        if sides:
            pl.when(step == int(mid * (n_steps - 1)))(lambda: run(1))
            pl.when(step == n_steps - 1)(lambda: run(2))

    grid_spec = pltpu.PrefetchScalarGridSpec(
        num_scalar_prefetch=n_scalar, grid=grid,
        in_specs=list(in_specs) + [_ANY] * sum(side_in),
        out_specs=list(out_specs) + [_ANY] * sum(side_out),
        scratch_shapes=list(scratch_shapes) + [pltpu.SemaphoreType.DMA((s.n_sems,)) for s in sides for _ in range(3)])
    res = pl.pallas_call(
        wrapped, name=name, grid_spec=grid_spec,
        out_shape=list(out_shape) + [o for s in sides for o in s.out_shapes],
        input_output_aliases=all_aliases,
        compiler_params=_params(len(grid)),
    )(*([scalars] if n_scalar else []), *args, *[a for s in sides for a in s.inputs + s.continues])
    main, rest, per_side = list(res[:n_out]), list(res[n_out:]), []
    for k in side_out:
        per_side.append(rest[:k])
        rest = rest[k:]
    return main, per_side


def _comm_only(sides, name):
    def body():
        pass

    return _call(body, name=name, grid=(1,), in_specs=[], out_specs=[], out_shape=[], args=[], sides=sides)[1]


def _remote(src, dst, send, recv, k, device):
    return pltpu.make_async_remote_copy(src_ref=src, dst_ref=dst, send_sem=send.at[k], recv_sem=recv.at[k],
                                        device_id=device, device_id_type=MESH)


def _both_ways(copy, keys):
    return [lambda k=k: copy(k).start() for k in keys], [], [lambda k=k: copy(k).wait() for k in keys]


def _gather_side(shards):
    n = len(shards)

    def make(ins, outs, send, recv, local):
        x, y, c = _position()
        mine = 2 * x + y
        chips = _other_chips(x, y)
        pairs = [(w, j) for w in range(n) for j in range(3)]

        def own(w):
            return pltpu.make_async_copy(ins[w], outs[w].at[mine], local.at[w])

        def push(w, j):
            return _remote(ins[w], outs[w].at[mine], send, recv, 3 * w + j, (*chips[j], c))

        def arrival(w, j):
            px, py = chips[j]
            return _remote(ins[w], outs[w].at[2 * px + py], send, recv, 3 * w + j, (px, py, c))

        starts = [lambda w=w: own(w).start() for w in range(n)] + [lambda w=w, j=j: push(w, j).start() for w, j in pairs]
        waits = ([lambda w=w, j=j: arrival(w, j).wait_recv() for w, j in pairs]
                 + [lambda w=w, j=j: push(w, j).wait_send() for w, j in pairs]
                 + [lambda w=w: own(w).wait() for w in range(n)])
        return starts, [], waits

    return _Side(shards, [S((N_SHARD,) + s.shape, s.dtype) for s in shards], 3 * n, make)


def _gather_half_side(shards, relations=(0, 1, 2), into=None):
    n = len(shards)

    def make(ins, outs, send, recv, local):
        x, y, c = _position()
        mine = 2 * x + y
        chips = _other_chips(x, y)
        pairs = [(w, j) for w in range(n) for j in relations]

        def rows(w, core):
            half = ins[w].shape[0] // 2
            return pl.ds(core * half, half)

        def own(w):
            return pltpu.make_async_copy(ins[w], outs[w].at[mine], local.at[w])

        def push(w, j):
            return _remote(ins[w].at[rows(w, c), :], outs[w].at[mine, rows(w, c), :], send, recv, 3 * w + j,
                           (*chips[j], c))

        def landed(w, j, core):
            px, py = chips[j]
            return outs[w].at[2 * px + py, rows(w, core), :]

        def arrival(w, j):
            return _remote(ins[w].at[rows(w, c), :], landed(w, j, c), send, recv, 3 * w + j, (*chips[j], c))

        def passed(w, j, core):
            return _remote(landed(w, j, core), landed(w, j, core), send, recv, 3 * n + 3 * w + j, (x, y, 1 - c))

        owns = range(n) if into is None else ()
        starts = [lambda w=w: own(w).start() for w in owns] + [lambda w=w, j=j: push(w, j).start() for w, j in pairs]
        mids = [t for w, j in pairs for t in (lambda w=w, j=j: arrival(w, j).wait_recv(),
                                              lambda w=w, j=j: passed(w, j, c).start())]
        waits = ([lambda w=w, j=j: passed(w, j, 1 - c).wait_recv() for w, j in pairs]
                 + [lambda w=w, j=j: passed(w, j, c).wait_send() for w, j in pairs]
                 + [lambda w=w, j=j: push(w, j).wait_send() for w, j in pairs]
                 + [lambda w=w: own(w).wait() for w in owns])
        return starts, mids, waits

    return _Side(shards, [S((N_SHARD,) + s.shape, s.dtype) for s in shards], 6 * n, make, continues=into or ())


def _halves_side(grads):
    n = len(grads)

    def make(ins, outs, send, recv, local):
        x, y, c = _position()

        def copy(w):
            half = ins[w].shape[1] // 2
            return _remote(ins[w].at[:, pl.ds((1 - c) * half, half), :], outs[w], send, recv, w, (x, y, 1 - c))

        return _both_ways(copy, range(n))

    return _Side(grads, [S((N_SHARD, g.shape[1] // 2, g.shape[2]), F32) for g in grads], n, make)


def _scatter_side(partials):
    n = len(partials)

    def make(ins, outs, send, recv, local):
        x, y, c = _position()
        chips = _other_chips(x, y)

        def copy(k):
            w, j = divmod(k, 3)
            px, py = chips[j]
            return _remote(ins[w].at[2 * px + py], outs[w].at[j], send, recv, k, (px, py, c))

        return _both_ways(copy, range(3 * n))

    return _Side(partials, [S((3,) + p.shape[1:], p.dtype) for p in partials], 3 * n, make)


def _swap_side(halves):
    n = len(halves)

    def make(ins, outs, send, recv, local):
        x, y, c = _position()
        return _both_ways(lambda w: _remote(ins[w], outs[w], send, recv, w, (x, y, 1 - c)), range(n))

    return _Side(halves, [S(h.shape, h.dtype) for h in halves], n, make)


N_DEVICE = 8


def _everyone_side(arrays):
    n = len(arrays)
    peers = N_DEVICE - 1

    def make(ins, outs, send, recv, local):
        x, y, c = _position()
        mine = 4 * x + 2 * y + c
        pairs = [(w, k) for w in range(n) for k in range(1, N_DEVICE)]

        def peer(k):
            return (1 - x if k & 4 else x, 1 - y if k & 2 else y, 1 - c if k & 1 else c)

        def own(w):
            return pltpu.make_async_copy(ins[w], outs[w].at[mine], local.at[w])

        def push(w, k):
            return _remote(ins[w], outs[w].at[mine], send, recv, peers * w + k - 1, peer(k))

        def arrival(w, k):
            px, py, pc = peer(k)
            return _remote(ins[w], outs[w].at[4 * px + 2 * py + pc], send, recv, peers * w + k - 1, (px, py, pc))

        starts = [lambda w=w: own(w).start() for w in range(n)] + [lambda w=w, k=k: push(w, k).start() for w, k in pairs]
        waits = ([lambda w=w, k=k: arrival(w, k).wait_recv() for w, k in pairs]
                 + [lambda w=w, k=k: push(w, k).wait_send() for w, k in pairs]
                 + [lambda w=w: own(w).wait() for w in range(n)])
        return starts, [], waits

    return _Side(arrays, [S((N_DEVICE,) + a.shape, a.dtype) for a in arrays], peers * n, make)


def _inproj_own(chip, x, g, w_shard, sides=()):
    T = x.shape[0]
    tm = min(TM_MM, T)

    def body(chip_ref, x_ref, g_ref, w_ref, proj_ref, h_ref):
        xv = x_ref[...]
        h = (xv * _rms_stats(xv) * g_ref[...]).astype(BF16)
        h_ref[...] = h
        proj_ref[...] = _dot(h, w_ref[...]).astype(BF16)

    return _call(
        body, name="inproj_own", grid=(T // tm,),
        in_specs=[pl.BlockSpec((tm, D), lambda i, c: (i, 0)), pl.BlockSpec((1, D), lambda i, c: (0, 0)),
                  pl.BlockSpec((D, IN_SHARD), lambda i, c: (0, 0), pipeline_mode=pl.Buffered(1))],
        out_specs=[pl.BlockSpec((tm, IN_SHARD), lambda i, c: (i, c[0])), pl.BlockSpec((tm, D), lambda i, c: (i, 0))],
        out_shape=[S((T, IN_COLS), BF16), S((T, D), BF16)],
        args=(x, g, w_shard), sides=sides, scalars=chip, mid=LATE_MID_STEP)


def _inproj_rest(chip, proj, h, w_in, first, count, name, sides=(), mid=MID_STEP):
    T = h.shape[0]
    tm = min(TM_MM, T)

    def body(chip_ref, _, h_ref, w_ref, proj_ref):
        proj_ref[...] = _dot(h_ref[...], w_ref[0]).astype(BF16)

    def other(p, c):
        return jnp.bitwise_xor(c[0], first + p)

    return _call(
        body, name=name, grid=(count, T // tm),
        in_specs=[_ANY, pl.BlockSpec((tm, D), lambda p, i, c: (i, 0)),
                  pl.BlockSpec((1, D, IN_SHARD), lambda p, i, c: (other(p, c), 0, 0))],
        out_specs=[pl.BlockSpec((tm, IN_SHARD), lambda p, i, c: (i, other(p, c)))],
        out_shape=[S((T, IN_COLS), BF16)],
        args=(proj, h, w_in), sides=sides, scalars=chip, aliases={0: 0}, mid=mid)


def _rg_gates(xc, wa_ref, wx_ref, ba, bx, sp):
    xb = xc.astype(BF16)
    blocks = [xb[:, RG_BLOCK * j:RG_BLOCK * (j + 1)] for j in range(N_RG_BLOCK)]
    r = _sig(jnp.concatenate([_dot(blocks[j], wa_ref[j]) for j in range(N_RG_BLOCK)], axis=1) + ba)
    gi = _sig(jnp.concatenate([_dot(blocks[j], wx_ref[j]) for j in range(N_RG_BLOCK)], axis=1) + bx)
    log_a = (-RG_C) * r * sp
    a = jnp.exp(log_a)
    m = jnp.sqrt(_one_minus_exp(2.0 * log_a))
    return xb, r, gi, a, m


N_GATES = 4
HEADS_PER_BLOCK = 4
HEAD_DIM = RG_BLOCK // HEADS_PER_BLOCK
_RG_BLOCKS_BF16 = pltpu.VMEM((N_RG_BLOCK, RG_BLOCK, RG_BLOCK), BF16)


def _fill_blockdiag(heads_ref, blocks):
    blocks[...] = jnp.zeros_like(blocks)
    for j in range(N_RG_BLOCK):
        for h in range(HEADS_PER_BLOCK):
            sl = slice(HEAD_DIM * h, HEAD_DIM * (h + 1))
            blocks[j, sl, sl] = heads_ref[0, HEADS_PER_BLOCK * j + h].astype(BF16)


def _rglru_fwd(proj, conv_w, conv_b, rg_wa, rg_wx, ba, bx, lam, sides=()):
    T = proj.shape[0]
    tm = min(TM_SCAN, T)

    def body(rx_ref, gate_ref, cw_ref, cb_ref, wah_ref, wxh_ref, ba_ref, bx_ref, lam_ref,
             ya_ref, xc_ref, h_ref, gates_ref, ext, hc, a_s, b_s, h_s, wa_ref, wx_ref):
        @pl.when(pl.program_id(0) == 0)
        def _():
            ext[0:SUBLANES, :] = jnp.zeros((SUBLANES, D), F32)
            hc[...] = jnp.zeros((SUBLANES, D), F32)
            _fill_blockdiag(wah_ref, wa_ref)
            _fill_blockdiag(wxh_ref, wx_ref)

        ext[SUBLANES:SUBLANES + tm, :] = rx_ref[...].astype(F32)
        xc = cb_ref[...]
        for k in range(CONV_WIDTH):
            xc = xc + ext[pl.ds(SUBLANES - (CONV_WIDTH - 1) + k, tm), :] * cw_ref[k:k + 1, :]
        ext[0:SUBLANES, :] = ext[tm:tm + SUBLANES, :]
        xc_ref[...] = xc
        _, r, gi, a, m = _rg_gates(xc, wa_ref, wx_ref, ba_ref[...], bx_ref[...], _softplus_neg(lam_ref[...]))
        for k, val in enumerate((r, gi, a, m)):
            gates_ref[:, D * k:D * (k + 1)] = val
        h = _scan_tile(a, m * (gi * xc), hc[0:1, :], a_s, b_s, h_s, reverse=False)
        hc[...] = jnp.broadcast_to(h[tm - 1:tm, :], (SUBLANES, D))
        h_ref[...] = h
        ya_ref[...] = (_gelu(gate_ref[...].astype(F32)) * h).astype(BF16)

    vec = pl.BlockSpec((1, D), lambda i: (0, 0))
    heads = pl.BlockSpec(rg_wa.shape, lambda i: (0, 0, 0, 0))
    tile = pl.BlockSpec((tm, D), lambda i: (i, 0))
    return _call(
        body, name="rglru_fwd", grid=(T // tm,),
        in_specs=[pl.BlockSpec((tm, D), lambda i: (i, 0)), pl.BlockSpec((tm, D), lambda i: (i, 1)),
                  pl.BlockSpec((CONV_WIDTH, D), lambda i: (0, 0)), vec, heads, heads, vec, vec, vec],
        out_specs=[tile, tile, tile, pl.BlockSpec((tm, N_GATES * D), lambda i: (i, 0))],
        out_shape=[S((T, D), BF16), S((T, D), F32), S((T, D), F32), S((T, N_GATES * D), F32)],
        scratch_shapes=[pltpu.VMEM((tm + SUBLANES, D), F32), pltpu.VMEM((SUBLANES, D), F32)]
        + [pltpu.VMEM((D // LANES, tm, LANES), F32)] * 3 + [_RG_BLOCKS_BF16] * 2,
        args=(proj, proj, conv_w, conv_b, rg_wa, rg_wx, ba, bx, lam), sides=sides, mid=LATE_MID_STEP)


def _layer_norm_stats(v):
    mu = jnp.mean(v, axis=-1, keepdims=True)
    vc = v - mu
    rstd = lax.rsqrt(jnp.mean(vc * vc, axis=-1, keepdims=True) + EPS)
    return vc * rstd, rstd


def _sgu_mix(w_ref, vnb, bst_ref, n_chunk):
    cols = []
    for g in range(N_GROUP):
        vg = vnb[:, CHUNK * g:CHUNK * (g + 1)].reshape(n_chunk, CHUNK, CHUNK)
        wb = jnp.broadcast_to(w_ref[g][None], (n_chunk, CHUNK, CHUNK))
        mg = lax.dot_general(wb, vg, (((2,), (1,)), ((0,), (0,))), preferred_element_type=F32)
        mg = mg + bst_ref[:, g:g + 1][None]
        cols.append(mg.reshape(n_chunk * CHUNK, CHUNK))
    return jnp.concatenate(cols, axis=1)


def _causal_mask():
    return (lax.broadcasted_iota(jnp.int32, (CHUNK, CHUNK), 0) >= lax.broadcasted_iota(jnp.int32, (CHUNK, CHUNK), 1))


def _fill_sgu_weights(ws_ref, bs_ref, w_tril, bs_t, w_tril_t=None):
    keep = _causal_mask()
    for g in range(N_GROUP):
        wg = jnp.where(keep, ws_ref[0, g], 0.0)
        w_tril[g] = wg.astype(BF16)
        if w_tril_t is not None:
            w_tril_t[g] = wg.T.astype(BF16)
    bs_t[...] = bs_ref[0].T


_SGU_W_BF16 = pltpu.VMEM((N_GROUP, CHUNK, CHUNK), BF16)
_SGU_BT = pltpu.VMEM((CHUNK, N_GROUP), F32)


def _sgu_merge_fwd(x, proj, ya, ln_g, ln_b, sgu_ws, sgu_bs, wpa, wpb, wout):
    T = x.shape[0]
    tm = min(TM_MM, T)
    n_chunk = tm // CHUNK

    def body(x_ref, uv_ref, gab_ref, ya_ref, g_ref, b_ref, ws_ref, bs_ref, wpa_ref, wpb_ref, wout_ref,
             yb_ref, pa_ref, pb_ref, mb_ref, x1_ref, w_ref, bst_ref):
        @pl.when(pl.program_id(0) == 0)
        def _():
            _fill_sgu_weights(ws_ref, bs_ref, w_ref, bst_ref)

        vhat, _ = _layer_norm_stats(_gelu(uv_ref[:, D:2 * D].astype(F32)))
        vnb = (vhat * g_ref[...] + b_ref[...]).astype(BF16)
        yb = (_gelu(uv_ref[:, 0:D].astype(F32)) * _sgu_mix(w_ref, vnb, bst_ref, n_chunk)).astype(BF16)
        yb_ref[...] = yb
        pa = _dot(ya_ref[...], wpa_ref[...])
        pb = _dot(yb, wpb_ref[...])
        pa_ref[...] = pa.astype(BF16)
        pb_ref[...] = pb.astype(BF16)
        mb = (_sig(gab_ref[:, 0:D].astype(F32)) * pa + _sig(gab_ref[:, D:2 * D].astype(F32)) * pb).astype(BF16)
        mb_ref[...] = mb
        x1_ref[...] = x_ref[...] + _dot(mb, wout_ref[...])

    tile = pl.BlockSpec((tm, D), lambda i: (i, 0))
    vec = pl.BlockSpec((1, D), lambda i: (0, 0))
    w = _resident((D, D))
    return pl.pallas_call(
        body, name="sgu_merge_fwd", grid=(T // tm,),
        in_specs=[tile, pl.BlockSpec((tm, 2 * D), lambda i: (i, 1)), pl.BlockSpec((tm, 2 * D), lambda i: (i, 2)), tile,
                  vec, vec, pl.BlockSpec(sgu_ws.shape, lambda i: (0, 0, 0, 0)),
                  pl.BlockSpec(sgu_bs.shape, lambda i: (0, 0, 0)), w, w, w],
        out_specs=[tile, tile, tile, tile, tile],
        out_shape=[S((T, D), BF16), S((T, D), BF16), S((T, D), BF16), S((T, D), BF16), S((T, D), F32)],
        scratch_shapes=[_SGU_W_BF16, _SGU_BT],
        compiler_params=_params(1),
    )(x, proj, proj, ya, ln_g, ln_b, sgu_ws, sgu_bs, wpa, wpb, wout)


def _ffn_fwd_loss(x1, g, w_gu, w_down, g_final, target):
    T = x1.shape[0]
    tm = min(TM_FF, T)

    def body(x_ref, g_ref, wgu_ref, wd_ref, gf_ref, t_ref,
             h2_ref, gu_ref, act_ref, dx2_ref, dx2b_ref, loss_ref, dg_ref):
        @pl.when(pl.program_id(0) == 0)
        def _():
            loss_ref[...] = jnp.zeros_like(loss_ref)
            dg_ref[...] = jnp.zeros_like(dg_ref)

        xv = x_ref[...]
        h2 = (xv * _rms_stats(xv) * g_ref[...]).astype(BF16)
        h2_ref[...] = h2
        x2 = xv
        for k in range(N_SHARD // 2):
            cols = slice(FF_SHARD * k, FF_SHARD * (k + 1))
            gate = _dot(h2, wgu_ref[k])
            up = _dot(h2, wgu_ref[k + N_SHARD // 2])
            gu_ref[:, cols] = gate.astype(BF16)
            gu_ref[:, D_FF + FF_SHARD * k:D_FF + FF_SHARD * (k + 1)] = up.astype(BF16)
            act = (gate * _sig(gate) * up).astype(BF16)
            act_ref[:, cols] = act
            x2 = x2 + _dot(act, wd_ref[cols, :])
        gf = gf_ref[...]
        err = x2 * _rms_stats(x2) * gf - t_ref[...]
        loss_ref[...] += 0.5 * jnp.sum(jnp.mean(err * err, axis=-1, keepdims=True), axis=0, keepdims=True)
        dx2, dg_rows = _rms_bwd(err * (1.0 / D), x2, gf)
        dg_ref[...] += _colsum(dg_rows)
        dx2_ref[...] = dx2
        dx2b_ref[...] = dx2.astype(BF16)

    tile = pl.BlockSpec((tm, D), lambda i: (i, 0))
    vec = pl.BlockSpec((1, D), lambda i: (0, 0))
    return pl.pallas_call(
        body, name="ffn_fwd_loss", grid=(T // tm,),
        in_specs=[tile, vec, _resident((N_SHARD, D, FF_SHARD)), _resident((D_FF, D)), vec, tile],
        out_specs=[tile, pl.BlockSpec((tm, 2 * D_FF), lambda i: (i, 0)), pl.BlockSpec((tm, D_FF), lambda i: (i, 0)),
                   tile, tile, pl.BlockSpec((1, 1), lambda i: (0, 0)), vec],
        out_shape=[S((T, D), BF16), S((T, 2 * D_FF), BF16), S((T, D_FF), BF16), S((T, D), F32), S((T, D), BF16),
                   S((1, 1), F32), S((1, D), F32)],
        compiler_params=_params(1),
    )(x1, g, w_gu, w_down, g_final, target)


def _ffn_bwd(dx2, dx2b, gu, w_down, w_gu, x1, g):
    T = x1.shape[0]
    tm = min(TM_FF, T)

    def body(dx2_ref, dx2b_ref, gu_ref, wd_ref, wgu_ref, x_ref, g_ref, dgu_ref, dx1_ref, dx1b_ref, dg_ref):
        @pl.when(pl.program_id(0) == 0)
        def _():
            dg_ref[...] = jnp.zeros_like(dg_ref)

        dxb = dx2b_ref[...]
        dh2 = jnp.zeros((tm, D), F32)
        for k in range(N_SHARD // 2):
            cols = slice(FF_SHARD * k, FF_SHARD * (k + 1))
            up_cols = slice(D_FF + FF_SHARD * k, D_FF + FF_SHARD * (k + 1))
            dact = _dot_nt(dxb, wd_ref[cols, :])
            gate = gu_ref[:, cols].astype(F32)
            sg = _sig(gate)
            dgate = (dact * gu_ref[:, up_cols].astype(F32) * (sg * (1.0 + gate * (1.0 - sg)))).astype(BF16)
            dup = (dact * (gate * sg)).astype(BF16)
            dgu_ref[:, cols] = dgate
            dgu_ref[:, up_cols] = dup
            dh2 = dh2 + _dot_nt(dgate, wgu_ref[k]) + _dot_nt(dup, wgu_ref[k + N_SHARD // 2])
        dx, dg_rows = _rms_bwd(dh2, x_ref[...], g_ref[...])
        dg_ref[...] += _colsum(dg_rows)
        dx1 = dx2_ref[...] + dx
        dx1_ref[...] = dx1
        dx1b_ref[...] = dx1.astype(BF16)

    tile = pl.BlockSpec((tm, D), lambda i: (i, 0))
    wide = pl.BlockSpec((tm, 2 * D_FF), lambda i: (i, 0))
    vec = pl.BlockSpec((1, D), lambda i: (0, 0))
    return pl.pallas_call(
        body, name="ffn_bwd", grid=(T // tm,),
        in_specs=[tile, tile, wide, _resident((D_FF, D)), _resident((N_SHARD, D, FF_SHARD)), tile, vec],
        out_specs=[wide, tile, tile, vec],
        out_shape=[S((T, 2 * D_FF), BF16), S((T, D), F32), S((T, D), BF16), S((1, D), F32)],
        compiler_params=_params(1),
    )(dx2, dx2b, gu, w_down, w_gu, x1, g)


def _matmul_tn(a, b, tn, shard_major, name, sides=()):
    T, M = a.shape
    N = b.shape[1]
    tk = min(TK_WGRAD, T)

    def body(a_ref, b_ref, o_ref):
        @pl.when(pl.program_id(1) == 0)
        def _():
            o_ref[...] = jnp.zeros_like(o_ref)

        acc = _dot_tn(a_ref[...], b_ref[...])
        if shard_major:
            o_ref[0] += acc
        else:
            o_ref[...] += acc

    if shard_major:
        out_spec, out_shape = pl.BlockSpec((1, M, tn), lambda j, k: (j, 0, 0)), S((N // tn, M, tn), F32)
    else:
        out_spec, out_shape = pl.BlockSpec((M, tn), lambda j, k: (0, j)), S((M, N), F32)
    (out,), side_outs = _call(
        body, name=name, grid=(N // tn, T // tk),
        in_specs=[pl.BlockSpec((tk, M), lambda j, k: (k, 0)), pl.BlockSpec((tk, tn), lambda j, k: (k, j))],
        out_specs=[out_spec], out_shape=[out_shape], args=(a, b), sides=sides)
    return out, side_outs


PIECE = IN_SHARD // 3
N_PIECE = IN_COLS // PIECE
DPROJ_ROTATION = 2 * D // PIECE


def _merge_sgu_bwd(dx1b, proj, pa, pb, ya, yb, mb, ln_g, ln_b, sgu_ws, sgu_bs, wpa, wpb, wout, sides=()):
    T = dx1b.shape[0]
    tm = min(TM_FF, T)
    n_chunk = tm // CHUNK
    n_steps = T // tm

    def body(dx_ref, uv_ref, gab_ref, pa_ref, pb_ref, ya_ref, yb_ref, mb_ref, g_ref, b_ref, ws_ref, bs_ref,
             wpa_ref, wpb_ref, wout_ref,
             dya_ref, dp_ref, dw_ref, dbs_ref, dg_ref, db_ref, gpa_ref, gpb_ref, gout_ref,
             w_ref, wt_ref, bst_ref, acc_pa, acc_pb, acc_out):
        @pl.when(pl.program_id(0) == 0)
        def _():
            for ref in (dw_ref, dbs_ref, dg_ref, db_ref, acc_pa, acc_pb, acc_out):
                ref[...] = jnp.zeros_like(ref)
            _fill_sgu_weights(ws_ref, bs_ref, w_ref, bst_ref, wt_ref)

        dxb = dx_ref[...]
        dm = _dot_nt(dxb, wout_ref[...])
        sa = _sig(gab_ref[:, 0:D].astype(F32))
        sb = _sig(gab_ref[:, D:2 * D].astype(F32))
        dpa = (dm * sa).astype(BF16)
        dpb = (dm * sb).astype(BF16)
        acc_pa[...] += _dot_tn(ya_ref[...], dpa)
        acc_pb[...] += _dot_tn(yb_ref[...], dpb)
        acc_out[...] += _dot_tn(mb_ref[...], dxb)
        dp_ref[:, 2 * D:3 * D] = (dm * pa_ref[...].astype(F32) * (sa * (1.0 - sa))).astype(BF16)
        dp_ref[:, 3 * D:4 * D] = (dm * pb_ref[...].astype(F32) * (sb * (1.0 - sb))).astype(BF16)
        dya_ref[...] = _dot_nt(dpa, wpa_ref[...]).astype(BF16)
        dyb_v = _dot_nt(dpb, wpb_ref[...])

        gu, dgu = _gelu_and_grad(uv_ref[:, 0:D].astype(F32))
        gv, dgv = _gelu_and_grad(uv_ref[:, D:2 * D].astype(F32))
        vhat, rstd = _layer_norm_stats(gv)
        lng = g_ref[...]
        vnb = (vhat * lng + b_ref[...]).astype(BF16)
        mixed = _sgu_mix(w_ref, vnb, bst_ref, n_chunk)
        dp_ref[:, 0:D] = (dyb_v * mixed * dgu).astype(BF16)
        dmix = dyb_v * gu
        dmb = dmix.astype(BF16)
        keep = _causal_mask()
        dvn_cols, dbs_rows = [], []
        for g in range(N_GROUP):
            sl = slice(CHUNK * g, CHUNK * (g + 1))
            dmg = dmb[:, sl].reshape(n_chunk, CHUNK, CHUNK)
            vg = vnb[:, sl].reshape(n_chunk, CHUNK, CHUNK)
            wtb = jnp.broadcast_to(wt_ref[g][None], (n_chunk, CHUNK, CHUNK))
            dvn = lax.dot_general(wtb, dmg, (((2,), (1,)), ((0,), (0,))), preferred_element_type=F32)
            dvn_cols.append(dvn.reshape(tm, CHUNK))
            dw = lax.dot_general(dmg, vg, (((2,), (2,)), ((0,), (0,))), preferred_element_type=F32)
            dw_ref[g] += jnp.where(keep, jnp.sum(dw, axis=0), 0.0)
            per_token = jnp.sum(dmix[:, sl], axis=1)
            dbs_rows.append(jnp.sum(per_token.reshape(n_chunk, CHUNK), axis=0, keepdims=True))
        dbs_ref[...] += jnp.concatenate(dbs_rows, axis=0)
        dvn = jnp.concatenate(dvn_cols, axis=1)
        dg_ref[...] += _colsum(dvn * vhat)
        db_ref[...] += _colsum(dvn)
        dvhat = dvn * lng
        dgv_in = rstd * (dvhat - jnp.mean(dvhat, axis=-1, keepdims=True)
                         - vhat * jnp.mean(dvhat * vhat, axis=-1, keepdims=True))
        dp_ref[:, D:2 * D] = (dgv_in * dgv).astype(BF16)

        @pl.when(pl.program_id(0) == n_steps - 1)
        def _():
            for acc, out in ((acc_pa, gpa_ref), (acc_pb, gpb_ref), (acc_out, gout_ref)):
                pltpu.sync_copy(acc, out)

    tile = pl.BlockSpec((tm, D), lambda i: (i, 0))
    vec = pl.BlockSpec((1, D), lambda i: (0, 0))
    w = _resident((D, D))
    wsp = pl.BlockSpec((N_GROUP, CHUNK, CHUNK), lambda i: (0, 0, 0))
    acc = pltpu.VMEM((D, D), F32)
    return _call(
        body, name="merge_sgu_bwd", grid=(n_steps,),
        in_specs=[tile, pl.BlockSpec((tm, 2 * D), lambda i: (i, 1)), pl.BlockSpec((tm, 2 * D), lambda i: (i, 2)),
                  tile, tile, tile, tile, tile, vec, vec, pl.BlockSpec(sgu_ws.shape, lambda i: (0, 0, 0, 0)),
                  pl.BlockSpec(sgu_bs.shape, lambda i: (0, 0, 0)), w, w, w],
        out_specs=[tile, pl.BlockSpec((tm, 4 * D), lambda i: (i, 0)), wsp,
                   pl.BlockSpec((N_GROUP, CHUNK), lambda i: (0, 0)), vec, vec, _ANY, _ANY, _ANY],
        out_shape=[S((T, D), BF16), S((T, IN_COLS), BF16), S((N_GROUP, CHUNK, CHUNK), F32), S((N_GROUP, CHUNK), F32),
                   S((1, D), F32), S((1, D), F32), S((D, D), F32), S((D, D), F32), S((D, D), F32)],
        scratch_shapes=[_SGU_W_BF16, _SGU_W_BF16, _SGU_BT, acc, acc, acc],
        args=(dx1b, proj, proj, pa, pb, ya, yb, mb, ln_g, ln_b, sgu_ws, sgu_bs, wpa, wpb, wout), sides=sides)


def _rglru_bwd(dya, dproj, proj, hseq, xc, gates, conv_w, rg_wa, rg_wx, lam, sides=()):
    T = dya.shape[0]
    tm = min(TM_SCAN, T)
    n = T // tm
    per8 = tm // SUBLANES

    def body(dya_ref, _, rx_ref, rxp_ref, gate_ref, h_ref, hp_ref, xc_ref, gates_ref, cw_ref, wah_ref, wxh_ref,
             lam_ref, dab_ref, dcw_ref, dcb_ref, dwah_ref, dwxh_ref, dba_ref, dbx_ref, dlam_ref,
             hext, rext, dext, carry_a, carry_dh, a_s, b_s, h_s, wa_ref, wx_ref, dwa_ref, dwx_ref):
        i = pl.program_id(0)
        first_tile = i == n - 1

        @pl.when(i == 0)
        def _():
            for ref in (dcw_ref, dcb_ref, dwa_ref, dwx_ref, dba_ref, dbx_ref, dlam_ref, carry_a, carry_dh):
                ref[...] = jnp.zeros_like(ref)
            dext[tm:tm + SUBLANES, :] = jnp.zeros((SUBLANES, D), F32)
            _fill_blockdiag(wah_ref, wa_ref)
            _fill_blockdiag(wxh_ref, wx_ref)

        gel, dgel = _gelu_and_grad(gate_ref[...].astype(F32))
        dya_v = dya_ref[...].astype(F32)
        hseq_v = h_ref[...]
        dgate = dya_v * hseq_v * dgel
        xcv = xc_ref[...]
        lam_v = lam_ref[...]
        sp = _softplus_neg(lam_v)
        xb = xcv.astype(BF16)
        r, gi, a, m = (gates_ref[:, D * k:D * (k + 1)] for k in range(N_GATES))

        row = lax.broadcasted_iota(jnp.int32, (tm, D), 0)
        c = jnp.where(row == tm - 1, carry_a[0:1, :], _shift_up(a, 1, 0.0))
        dH = _scan_tile(c, dya_v * gel, carry_dh[0:1, :], a_s, b_s, h_s, reverse=True)
        carry_a[...] = jnp.broadcast_to(a[0:1, :], (SUBLANES, D))
        carry_dh[...] = jnp.broadcast_to(dH[0:1, :], (SUBLANES, D))

        hext[0:SUBLANES, :] = jnp.where(first_tile, 0.0, hp_ref[...])
        hext[SUBLANES:SUBLANES + tm, :] = hseq_v
        h_prev = hext[pl.ds(SUBLANES - 1, tm), :]

        d_m = dH * (gi * xcv)
        d_la = dH * h_prev * a - d_m * (a * a) / m
        d_ia = dH * m * xcv * (gi * (1.0 - gi))
        d_ra = d_la * ((-RG_C) * sp) * (r * (1.0 - r))
        dlam_ref[...] += _colsum(d_la * ((-RG_C) * r)) * (-_sig(-lam_v))
        dba_ref[...] += _colsum(d_ra)
        dbx_ref[...] += _colsum(d_ia)
        drab = d_ra.astype(BF16)
        diab = d_ia.astype(BF16)
        dxc_cols = []
        for j in range(N_RG_BLOCK):
            sl = slice(RG_BLOCK * j, RG_BLOCK * (j + 1))
            dxc_cols.append(_dot_nt(drab[:, sl], wa_ref[j]) + _dot_nt(diab[:, sl], wx_ref[j]))
            dwa_ref[j] += _dot_tn(xb[:, sl], drab[:, sl])
            dwx_ref[j] += _dot_tn(xb[:, sl], diab[:, sl])
        dxc = dH * m * gi + jnp.concatenate(dxc_cols, axis=1)

        dcb_ref[...] += _colsum(dxc)
        dext[0:tm, :] = dxc
        rext[0:SUBLANES, :] = jnp.where(first_tile, 0.0, rxp_ref[SUBLANES:2 * SUBLANES, :].astype(F32))
        rext[SUBLANES:SUBLANES + tm, :] = rx_ref[...].astype(F32)
        drx = jnp.zeros((tm, D), F32)
        for k in range(CONV_WIDTH):
            drx = drx + dext[pl.ds(CONV_WIDTH - 1 - k, tm), :] * cw_ref[k:k + 1, :]
            dcw_ref[k:k + 1, :] += _colsum(dxc * rext[pl.ds(SUBLANES - (CONV_WIDTH - 1) + k, tm), :])
        dext[tm:tm + SUBLANES, :] = dext[0:SUBLANES, :]
        dab_ref[:, 0:D] = drx.astype(BF16)
        dab_ref[:, D:2 * D] = dgate.astype(BF16)

        @pl.when(first_tile)
        def _():
            for j in range(N_RG_BLOCK):
                for h in range(HEADS_PER_BLOCK):
                    sl = slice(HEAD_DIM * h, HEAD_DIM * (h + 1))
                    pair, side = divmod(HEADS_PER_BLOCK * j + h, 2)
                    lanes = slice(HEAD_DIM * side, HEAD_DIM * (side + 1))
                    dwah_ref[pair, :, lanes] = dwa_ref[j, sl, sl]
                    dwxh_ref[pair, :, lanes] = dwx_ref[j, sl, sl]

    def rev(col):
        return lambda i: (n - 1 - i, col)

    def prev8(col):
        return lambda i: (jnp.maximum((n - 1 - i) * per8 - 1, 0), col)

    def prev16(col):
        return lambda i: (jnp.maximum((n - 1 - i) * (per8 // 2) - 1, 0), col)

    tile = pl.BlockSpec((tm, D), rev(0))
    vec = pl.BlockSpec((1, D), lambda i: (0, 0))
    heads_in = pl.BlockSpec(rg_wa.shape, lambda i: (0, 0, 0, 0))
    head_pairs = (rg_wa.shape[1] // 2, HEAD_DIM, 2 * HEAD_DIM)
    heads_out = pl.BlockSpec(head_pairs, lambda i: (0, 0, 0))
    cw = pl.BlockSpec((CONV_WIDTH, D), lambda i: (0, 0))
    blocks_f32 = pltpu.VMEM((N_RG_BLOCK, RG_BLOCK, RG_BLOCK), F32)
    return _call(
        body, name="rglru_bwd", grid=(n,),
        in_specs=[tile, _ANY, pl.BlockSpec((tm, D), rev(0)), pl.BlockSpec((2 * SUBLANES, D), prev16(0)),
                  pl.BlockSpec((tm, D), rev(1)), tile, pl.BlockSpec((SUBLANES, D), prev8(0)), tile,
                  pl.BlockSpec((tm, N_GATES * D), rev(0)), cw, heads_in, heads_in, vec],
        out_specs=[pl.BlockSpec((tm, 2 * D), rev(2)), cw, vec, heads_out, heads_out, vec, vec, vec],
        out_shape=[S((T, IN_COLS), BF16), S((CONV_WIDTH, D), F32), S((1, D), F32),
                   S(head_pairs, F32), S(head_pairs, F32), S((1, D), F32), S((1, D), F32), S((1, D), F32)],
        scratch_shapes=[pltpu.VMEM((tm + SUBLANES, D), F32), pltpu.VMEM((tm + SUBLANES, D), F32),
                        pltpu.VMEM((tm + SUBLANES, D), F32), pltpu.VMEM((SUBLANES, D), F32),
                        pltpu.VMEM((SUBLANES, D), F32)] + [pltpu.VMEM((D // LANES, tm, LANES), F32)] * 3
        + [_RG_BLOCKS_BF16] * 2 + [blocks_f32] * 2,
        args=(dya, dproj, proj, proj, proj, hseq, hseq, xc, gates, conv_w, rg_wa, rg_wx, lam), sides=sides,
        aliases={1: 0})


def _inproj_dh(dproj, w_in, dh, first, count, name, sides=()):
    T = dproj.shape[0]
    tm = min(TM_MM, T)

    def body(*refs):
        dp_ref, w_ref, dh_ref = refs[-3:]
        dh = jnp.zeros((tm, D), F32)
        for p in range(N_PIECE):
            shard, part = divmod((p + DPROJ_ROTATION) % N_PIECE, IN_SHARD // PIECE)
            dh = dh + _dot_nt(dp_ref[:, PIECE * p:PIECE * (p + 1)], w_ref[shard, :, PIECE * part:PIECE * (part + 1)])
        dh_ref[...] = dh

    earlier = [] if dh is None else [dh]
    return _call(
        body, name=name, grid=(count,),
        in_specs=[_ANY] * len(earlier) + [pl.BlockSpec((tm, IN_COLS), lambda i: (first + i, 0)),
                                         _resident((N_SHARD, D, IN_SHARD))],
        out_specs=[pl.BlockSpec((tm, D), lambda i: (first + i, 0))],
        out_shape=[S((T, D), F32)],
        args=(*earlier, dproj, w_in), sides=sides, aliases={0: 0} if earlier else None)


def _inproj_norm_bwd(dh, x, g, dx1):
    T = x.shape[0]
    tm = min(TM_MM, T)

    def body(dh_ref, x_ref, g_ref, dx1_ref, dx_ref, dgm_ref):
        @pl.when(pl.program_id(0) == 0)
        def _():
            dgm_ref[...] = jnp.zeros_like(dgm_ref)

        dx, dg_rows = _rms_bwd(dh_ref[...], x_ref[...], g_ref[...])
        dgm_ref[...] += _colsum(dg_rows)
        dx_ref[...] = dx1_ref[...] + dx

    tile = pl.BlockSpec((tm, D), lambda i: (i, 0))
    vec = pl.BlockSpec((1, D), lambda i: (0, 0))
    return pl.pallas_call(
        body, name="inproj_norm_bwd", grid=(T // tm,),
        in_specs=[tile, tile, vec, tile],
        out_specs=[tile, vec],
        out_shape=[S((T, D), F32), S((1, D), F32)],
        compiler_params=_params(1),
    )(dh, x, g, dx1)


def _inproj_wgrad(h, dproj, sides=()):
    T = h.shape[0]
    tk = min(TK_WGRAD, T)
    per = IN_SHARD // PIECE

    def body(h_ref, *refs):
        pieces, o_ref = refs[:per], refs[per]

        @pl.when(pl.program_id(1) == 0)
        def _():
            o_ref[...] = jnp.zeros_like(o_ref)

        o_ref[0] += _dot_tn(h_ref[...], jnp.concatenate([p[...] for p in pieces], axis=1))

    def piece(i):
        return pl.BlockSpec((tk, PIECE), lambda j, k: (k, (per * j + i + N_PIECE - DPROJ_ROTATION) % N_PIECE))

    (out,), side_outs = _call(
        body, name="inproj_wgrad", grid=(N_SHARD, T // tk),
        in_specs=[pl.BlockSpec((tk, D), lambda j, k: (k, 0))] + [piece(i) for i in range(per)],
        out_specs=[pl.BlockSpec((1, D, IN_SHARD), lambda j, k: (j, 0, 0))],
        out_shape=[S((N_SHARD, D, IN_SHARD), F32)], args=(h,) + (dproj,) * per, sides=sides)
    return out, side_outs


def _row_tile(rows):
    for t in range(256, 0, -SUBLANES):
        if rows % t == 0:
            return t
    raise ValueError(rows)


def _add_halves(core, grads, theirs, name):
    n = len(grads)
    _, r, cols = grads[0].shape
    half = r // 2
    tr = _row_tile(half)
    nb = half // tr

    def body(core_ref, *refs):
        for g_ref, t_ref, o_ref in zip(refs[:n], refs[n:2 * n], refs[2 * n:]):
            o_ref[...] = (g_ref[...] + t_ref[...]).astype(BF16)

    blk = pl.BlockSpec((1, tr, cols), lambda s, i, core_ref: (s, i, 0))
    mine = pl.BlockSpec((1, tr, cols), lambda s, i, core_ref: (s, core_ref[0] * nb + i, 0))
    gs = pltpu.PrefetchScalarGridSpec(num_scalar_prefetch=1, grid=(N_SHARD, nb),
                                      in_specs=[mine] * n + [blk] * n, out_specs=[blk] * n)
    return pl.pallas_call(
        body, name=name, grid_spec=gs, out_shape=[S((N_SHARD, half, cols), BF16)] * n,
        compiler_params=_params(2, VMEM_LIMIT // 3),
    )(core, *grads, *theirs)


def _sum_shards(chip, owns, others, name):
    n = len(owns)
    _, half, cols = owns[0].shape
    tr = _row_tile(half)

    def body(chip_ref, *refs):
        for own_ref, oth_ref, o_ref in zip(refs[:n], refs[n:2 * n], refs[2 * n:]):
            acc = own_ref[0].astype(F32)
            for j in range(3):
                acc = acc + oth_ref[j].astype(F32)
            o_ref[...] = acc

    gs = pltpu.PrefetchScalarGridSpec(
        num_scalar_prefetch=1, grid=(half // tr,),
        in_specs=[pl.BlockSpec((1, tr, cols), lambda i, chip_ref: (chip_ref[0], i, 0))] * n
        + [pl.BlockSpec((3, tr, cols), lambda i, chip_ref: (0, i, 0))] * n,
        out_specs=[pl.BlockSpec((tr, cols), lambda i, chip_ref: (i, 0))] * n)
    return pl.pallas_call(
        body, name=name, grid_spec=gs, out_shape=[S((half, cols), F32)] * n,
        compiler_params=_params(1, VMEM_LIMIT // 3),
    )(chip, *owns, *others)


def _adamw(w, g, m, v):
    m = ADAM_B1 * m + (1.0 - ADAM_B1) * g
    v = ADAM_B2 * v + (1.0 - ADAM_B2) * (g * g)
    m_hat = m / (1.0 - ADAM_B1 ** ADAM_STEP)
    v_hat = v / (1.0 - ADAM_B2 ** ADAM_STEP)
    delta = -ADAM_LR * (m_hat / (jnp.sqrt(v_hat) + ADAM_EPS) + ADAM_WD * w)
    return delta, m, v


def _adamw_shard(core, mine, theirs, w, m, v, name):
    n = len(w)
    r, cols = w[0].shape
    half = r // 2
    tr = _row_tile(half)
    nb = half // tr

    def body(core_ref, *refs):
        groups = [refs[k * n:(k + 1) * n] for k in range(9)]
        for mine_ref, theirs_ref, w_ref, m_ref, v_ref, g_ref, d_ref, mo_ref, vo_ref in zip(*groups):
            g = jnp.where(pl.program_id(0) == core_ref[0], mine_ref[...], theirs_ref[...])
            g_ref[...] = g
            d_ref[...], mo_ref[...], vo_ref[...] = _adamw(w_ref[...], g, m_ref[...], v_ref[...])

    hblk = pl.BlockSpec((tr, cols), lambda h, i, core_ref: (i, 0))
    blk = pl.BlockSpec((tr, cols), lambda h, i, core_ref: (h * nb + i, 0))
    gs = pltpu.PrefetchScalarGridSpec(num_scalar_prefetch=1, grid=(2, nb),
                                      in_specs=[hblk] * (2 * n) + [blk] * (3 * n), out_specs=[blk] * (4 * n))
    res = pl.pallas_call(
        body, name=name, grid_spec=gs, out_shape=[S((r, cols), F32)] * (4 * n), compiler_params=_params(2),
    )(core, *mine, *theirs, *w, *m, *v)
    return [tuple(res[k * n + j] for k in range(4)) for j in range(n)]


def _adamw_whole(w, g, m, v, name):
    def body(w_ref, g_ref, m_ref, v_ref, d_ref, mo_ref, vo_ref):
        d_ref[...], mo_ref[...], vo_ref[...] = _adamw(w_ref[...], g_ref[...], m_ref[...], v_ref[...])

    return pl.pallas_call(body, name=name, out_shape=[S(w.shape, F32)] * 3)(w, g, m, v)


_VEC_ROWS = ("norm_mix_g", "conv_b", "rg_lambda", "sgu_ln_g", "sgu_ln_b", "norm_ffn_g", "norm_final_g", "rg_ba",
             "rg_bx")
_CONV_ROW = len(_VEC_ROWS)
_LOSS_ROW = _CONV_ROW + CONV_WIDTH
_VEC_PAD = -(_LOSS_ROW + 1) % SUBLANES
_HEAD_BIASES = ("rg_ba", "rg_bx")
_TENSORS = ("sgu_bs", "sgu_ws", "rg_wa", "rg_wx")
_HEAD_PAIRS = ("rg_wa", "rg_wx")


def _small_sum_adamw(parts, w, m, v):
    names = [n for n in _VEC_ROWS] + list(_TENSORS)
    n_parts = len(parts)

    def total(ref):
        acc = ref[0]
        for k in range(1, N_DEVICE):
            acc = acc + ref[k]
        return acc

    def body(*refs):
        part_refs, refs = refs[:n_parts], refs[n_parts:]
        w_refs, m_refs, v_refs = (dict(zip(names, refs[k * len(names):(k + 1) * len(names)])) for k in range(3))
        outs = refs[3 * len(names):]
        out_refs = {n: outs[4 * k:4 * k + 4] for k, n in enumerate(names)}
        conv_ref, loss_ref = outs[4 * len(names):]
        vec = total(part_refs[0])
        grads = {n: total(p) for n, p in zip(_TENSORS, part_refs[1:])}
        for n in _HEAD_PAIRS:
            pairs = grads[n]
            grads[n] = jnp.stack([pairs[k // 2, :, HEAD_DIM * (k % 2):HEAD_DIM * (k % 2 + 1)]
                                  for k in range(2 * pairs.shape[0])], axis=0)
        grads = {n: g[None] for n, g in grads.items()}
        for row, n in enumerate(_VEC_ROWS):
            g = vec[row:row + 1, :]
            if n in _HEAD_BIASES:
                g = jnp.concatenate([g[:, HEAD_DIM * h:HEAD_DIM * (h + 1)] for h in range(D // HEAD_DIM)], axis=0)[None]
            grads[n] = g
        for n in names:
            g_ref, d_ref, mo_ref, vo_ref = out_refs[n]
            g_ref[...] = grads[n]
            d_ref[...], mo_ref[...], vo_ref[...] = _adamw(w_refs[n][...], grads[n], m_refs[n][...], v_refs[n][...])
        conv_ref[...] = vec[_CONV_ROW:_CONV_ROW + CONV_WIDTH, :]
        loss_ref[...] = vec[_LOSS_ROW:_LOSS_ROW + 1, 0:1]

    res = pl.pallas_call(
        body, name="small_sum_adamw",
        out_shape=[S(w[n].shape, F32) for n in names for _ in range(4)] + [S((CONV_WIDTH, D), F32), S((1, 1), F32)],
        compiler_params=pltpu.CompilerParams(vmem_limit_bytes=VMEM_LIMIT),
    )(*parts, *[w[n] for n in names], *[m[n] for n in names], *[v[n] for n in names])
    return {n: tuple(res[4 * k:4 * k + 4]) for k, n in enumerate(names)}, res[-2], res[-1]


_BIG = ("w_in", "w_proj_a", "w_proj_b", "w_out", "w_gate_up", "w_down")
_WEIGHTS = ("norm_mix_g", "w_in", "conv_w", "conv_b", "rg_wa", "rg_ba", "rg_wx", "rg_bx", "rg_lambda", "sgu_ln_g",
            "sgu_ln_b", "sgu_ws", "sgu_bs", "w_proj_a", "w_proj_b", "w_out", "norm_ffn_g", "w_gate_up", "w_down",
            "norm_final_g")


def kernel(x, norm_mix_g, w_in, conv_w, conv_b, rg_wa, rg_ba, rg_wx, rg_bx, rg_lambda, sgu_ln_g, sgu_ln_b, sgu_ws, sgu_bs, w_proj_a, w_proj_b, w_out, norm_ffn_g, w_gate_up, w_down, norm_final_g, loss_target, m_norm_mix_g, m_w_in, m_conv_w, m_conv_b, m_rg_wa, m_rg_ba, m_rg_wx, m_rg_bx, m_rg_lambda, m_sgu_ln_g, m_sgu_ln_b, m_sgu_ws, m_sgu_bs, m_w_proj_a, m_w_proj_b, m_w_out, m_norm_ffn_g, m_w_gate_up, m_w_down, m_norm_final_g, v_norm_mix_g, v_w_in, v_conv_w, v_conv_b, v_rg_wa, v_rg_ba, v_rg_wx, v_rg_bx, v_rg_lambda, v_sgu_ln_g, v_sgu_ln_b, v_sgu_ws, v_sgu_bs, v_w_proj_a, v_w_proj_b, v_w_out, v_norm_ffn_g, v_w_gate_up, v_w_down, v_norm_final_g):
    args = dict(locals())
    w = {n: args[n] for n in _WEIGHTS}
    mom = {n: args["m_" + n] for n in _WEIGHTS}
    var = {n: args["v_" + n] for n in _WEIGHTS}
    xi, yi, ci = _position()
    core = ci.astype(jnp.int32).reshape(1)
    chip = (2 * xi + yi).astype(jnp.int32).reshape(1)

    bf = {n: w[n][0].astype(BF16) for n in _BIG}
    final_g = w["norm_final_g"].reshape(1, D)
    ba, bx = w["rg_ba"].reshape(1, D), w["rg_bx"].reshape(1, D)
    lam, ln_g, ln_b = w["rg_lambda"], w["sgu_ln_g"], w["sgu_ln_b"]
    x0, target = x[0], loss_target[0]

    def shard_major(g):
        return g.reshape(N_SHARD, g.shape[0] // N_SHARD, g.shape[1])

    def per_shape(names, fn, *lists):
        if len({a.shape for a in lists[0]}) == 1:
            return fn("_".join(names), *lists)
        return [r for k, n in enumerate(names) for r in fn(n, *[[a[k]] for a in lists])]

    def chip_sums(names, grads, theirs):
        return per_shape(names, lambda s, g, t: _add_halves(core, g, t, "add_halves_" + s), grads, theirs)

    def my_halves(names, sums, arrived):
        return per_shape(names, lambda s, p, a: _sum_shards(chip, p, a, "sum_shards_" + s), sums, arrived)

    (proj, h), ((w_in_a,), (conv_a,)) = _inproj_own(
        chip, x0, w["norm_mix_g"], bf["w_in"],
        sides=[_gather_half_side([bf["w_in"]], relations=(0, 1)), _gather_side([w["conv_w"][0]])])
    conv_cols = conv_a.shape[-1]
    conv_full = jnp.swapaxes(conv_a, 0, 1).reshape(CONV_WIDTH, D)
    (proj,), ((w_in_a,),) = _inproj_rest(
        chip, proj, h, w_in_a, 1, 2, "inproj_near",
        sides=[_gather_half_side([bf["w_in"]], relations=(2,), into=[w_in_a])])
    (proj,), _ = _inproj_rest(chip, proj, h, w_in_a, 3, 1, "inproj_far")
    (ya, xc, hseq, gates), ((w_pa_a, w_pb_a, w_out_a, w_gu_a, w_down_a),) = _rglru_fwd(
        proj, conv_full, w["conv_b"], w["rg_wa"], w["rg_wx"], ba, bx, lam,
        sides=[_gather_half_side([bf[n] for n in ("w_proj_a", "w_proj_b", "w_out", "w_gate_up", "w_down")])])
    wpa, wpb, wout, wdown = w_pa_a.reshape(D, D), w_pb_a.reshape(D, D), w_out_a.reshape(D, D), w_down_a.reshape(D_FF, D)
    yb, pa, pb, mb, x1 = _sgu_merge_fwd(x0, proj, ya, ln_g, ln_b, w["sgu_ws"], w["sgu_bs"], wpa, wpb, wout)
    h2, gu, act, dx2, dx2b, loss, d_final_g = _ffn_fwd_loss(x1, w["norm_ffn_g"], w_gu_a, wdown, final_g, target)

    dgu, dx1, dx1b, d_ffn_g = _ffn_bwd(dx2, dx2b, gu, wdown, w_gu_a, x1, w["norm_ffn_g"])
    ffn = ("w_gate_up", "w_down")
    g_ffn = [_matmul_tn(h2, dgu, FF_SHARD, True, "wgrad_gate_up")[0],
             shard_major(_matmul_tn(act, dx2b, D // 2, False, "wgrad_down")[0])]
    (dya, dproj, d_ws, d_bs, d_lng, d_lnb, g_pa, g_pb, g_out), (theirs_ffn,) = _merge_sgu_bwd(
        dx1b, proj, pa, pb, ya, yb, mb, ln_g, ln_b, w["sgu_ws"], w["sgu_bs"], wpa, wpb, wout,
        sides=[_halves_side(g_ffn)])
    sums_ffn = chip_sums(ffn, g_ffn, theirs_ffn)
    mix = ("w_proj_a", "w_proj_b", "w_out")
    g_mix = [shard_major(g) for g in (g_pa, g_pb, g_out)]
    (dproj, d_cw, d_cb, d_wa, d_wx, d_ba, d_bx, d_lam), (arrived_ffn, theirs_mix, sgu_parts) = _rglru_bwd(
        dya, dproj, proj, hseq, xc, gates, conv_full, w["rg_wa"], w["rg_wx"], lam,
        sides=[_scatter_side(sums_ffn), _halves_side(g_mix), _everyone_side([d_bs, d_ws])])
    mine_ffn = my_halves(ffn, sums_ffn, arrived_ffn)
    sums_mix = chip_sums(mix, g_mix, theirs_mix)
    g_in, ((wa_parts, wx_parts), other_ffn, arrived_mix) = _inproj_wgrad(
        h, dproj, sides=[_everyone_side([d_wa, d_wx]), _swap_side(mine_ffn),
                         _scatter_side(sums_mix)])
    mine_mix = my_halves(mix, sums_mix, arrived_mix)
    n_tiles = x0.shape[0] // min(TM_MM, x0.shape[0])
    n_first = max(1, n_tiles * 3 // 8)
    (dh,), (theirs_in, other_mix) = _inproj_dh(dproj, w_in_a, None, 0, n_first, "inproj_dh_a",
                                               sides=[_halves_side([g_in]), _swap_side(mine_mix)])
    sums_in = chip_sums(("w_in",), [g_in], theirs_in)
    (dh,), (arrived_in,) = _inproj_dh(dproj, w_in_a, dh, n_first, n_tiles - n_first, "inproj_dh_b",
                                      sides=[_scatter_side(sums_in)])
    mine_in = my_halves(("w_in",), sums_in, arrived_in)
    grad_x, d_mix_g = _inproj_norm_bwd(dh, x0, w["norm_mix_g"], dx1)
    rows = {"norm_mix_g": d_mix_g, "conv_b": d_cb, "rg_lambda": d_lam, "sgu_ln_g": d_lng, "sgu_ln_b": d_lnb,
            "norm_ffn_g": d_ffn_g, "norm_final_g": d_final_g, "rg_ba": d_ba, "rg_bx": d_bx}
    vec = jnp.concatenate([rows[n] for n in _VEC_ROWS]
                          + [d_cw, jnp.pad(loss, ((0, _VEC_PAD), (0, D - 1)))], axis=0)
    other_in, (vec_parts,) = _comm_only([_swap_side(mine_in), _everyone_side([vec])], "swap_w_in")
    small_parts = [vec_parts] + sgu_parts + [wa_parts, wx_parts]

    out = {}
    for names, gm, go in ((ffn, mine_ffn, other_ffn), (mix, mine_mix, other_mix), (("w_in",), mine_in, other_in)):
        results = per_shape(
            names, lambda s, *lists: _adamw_shard(core, *lists, "adamw_" + s),
            gm, go, [w[n][0] for n in names], [mom[n][0] for n in names], [var[n][0] for n in names])
        for n, res in zip(names, results):
            out[n] = tuple(a[None] for a in res)
    as_row = lambda t: {n: a.reshape(1, D) if n == "norm_final_g" else a for n, a in t.items()}
    small_out, conv_sum, loss_sum = _small_sum_adamw(small_parts, as_row(w), as_row(mom), as_row(var))
    out.update(small_out)
    out["norm_final_g"] = tuple(a.reshape(D) for a in small_out["norm_final_g"])
    conv_g = lax.dynamic_slice_in_dim(conv_sum, chip[0] * conv_cols, conv_cols, axis=1)
    d, mo, vo = _adamw_whole(w["conv_w"][0], conv_g, mom["conv_w"][0], var["conv_w"][0], "adamw_conv_w")
    out["conv_w"] = tuple(a[None] for a in (conv_g, d, mo, vo))

    return (loss_sum[0, 0], grad_x[None], *[out[n][0] for n in _WEIGHTS], *[out[n][1] for n in _WEIGHTS],
            *[out[n][2] for n in _WEIGHTS], *[out[n][3] for n in _WEIGHTS])
```
